```python
import jax, jax.numpy as jnp
from jax import lax
import numpy as np

D_MODEL = 1024
BATCH = 8
SEQ = 2048
DEPTH = 2
DEC_BATCH = 128
DEC_SEQ = 4
PAST_LEN = 16384
PAGE_SIZE = 128

D_MIX = D_MODEL
D_RET = D_MIX // 2
D_CONV = D_MIX - D_RET
N_RET_HEADS = 4
HEAD_DK = D_RET // N_RET_HEADS
HEAD_DV = D_RET // N_RET_HEADS
RET_CHUNK = 128
ROPE_BASE = 10000.0
CONV_WIDTH = 31
N_EXPERTS = 16
N_GROUPS = 4
EXPERTS_PER_GROUP = N_EXPERTS // N_GROUPS
TOP_K = 2
D_FF_EXPERT = D_MODEL
MOE_BLOCK = 128
N_MOD = 6
EPS = 1e-6
D_IN = 4 * D_RET + 2 * D_CONV

kernel_name = "hybrid_retention_conformer_grouped_moe_step"


def rms_norm(x, g):
    xf = x.astype(jnp.float32)
    y = xf * lax.rsqrt(jnp.mean(xf * xf, axis=-1, keepdims=True) + EPS)
    return (y * g.astype(jnp.float32)).astype(x.dtype)


def layer_norm(x, g, b):
    xf = x.astype(jnp.float32)
    mu = jnp.mean(xf, axis=-1, keepdims=True)
    var = jnp.mean(jnp.square(xf - mu), axis=-1, keepdims=True)
    y = (xf - mu) * lax.rsqrt(var + EPS) * g.astype(jnp.float32) + b.astype(jnp.float32)
    return y.astype(x.dtype)


def rotary(x, pos):
    half = x.shape[-1] // 2
    inv = ROPE_BASE ** (-jnp.arange(half, dtype=jnp.float32) / half)
    ang = pos.astype(jnp.float32)[:, None] * inv[None, :]
    cos = jnp.cos(ang)[None, :, None, :]
    sin = jnp.sin(ang)[None, :, None, :]
    xf = x.astype(jnp.float32)
    x1, x2 = xf[..., :half], xf[..., half:]
    return jnp.concatenate([x1 * cos - x2 * sin, x1 * sin + x2 * cos], axis=-1)


def retention(q, k, v, s0):
    B, T, H, _ = q.shape
    dv = v.shape[-1]
    chunk = RET_CHUNK if T % RET_CHUNK == 0 else T
    n = T // chunk
    lg = jnp.log(1.0 - 2.0 ** (-5.0 - jnp.arange(H, dtype=jnp.float32)))
    idx = jnp.arange(chunk, dtype=jnp.float32)
    rel = idx[:, None] - idx[None, :]
    decay = jnp.where(rel[None] >= 0, jnp.exp(jnp.maximum(rel, 0.0)[None] * lg[:, None, None]), 0.0)
    q_decay = jnp.exp((idx[:, None] + 1.0) * lg[None, :])
    k_decay = jnp.exp((chunk - 1.0 - idx[:, None]) * lg[None, :])
    c_decay = jnp.exp(chunk * lg)

    def to_chunks(t):
        return t.reshape(B, n, chunk, H, t.shape[-1]).transpose(1, 0, 2, 3, 4)

    def step(s, inp):
        qc, kc, vc = inp
        scores = jnp.einsum('bihd,bjhd->bhij', qc, kc) * decay[None]
        inner = jnp.einsum('bhij,bjhe->bihe', scores, vc)
        cross = jnp.einsum('bihd,bhde->bihe', qc, s) * q_decay[None, :, :, None]
        s = s * c_decay[None, :, None, None] + jnp.einsum(
            'bjhd,bjhe->bhde', kc * k_decay[None, :, :, None], vc)
        return s, inner + cross

    s_final, o = lax.scan(step, s0, (to_chunks(q), to_chunks(k), to_chunks(v)))
    o = o.transpose(1, 0, 2, 3, 4).reshape(B, T, H, dv)
    return o, s_final


def causal_depthwise_conv(a, buf, w, b):
    full = jnp.concatenate([buf.astype(a.dtype), a], axis=1)
    y = lax.conv_general_dilated(
        full, w[:, None, :].astype(a.dtype), window_strides=(1,), padding='VALID',
        dimension_numbers=('NWC', 'WIO', 'NWC'), feature_group_count=a.shape[-1])
    return y + b.astype(a.dtype), full[:, -(CONV_WIDTH - 1):, :]


def token_mixers(h, pos, ret_state, conv_buf, w_in, w_conv, b_conv, g_conv_ln, b_conv_ln, g_ret_gn):
    B, T, _ = h.shape
    proj = h @ w_in
    q, k, v, g, u, z = jnp.split(
        proj, [D_RET, 2 * D_RET, 3 * D_RET, 4 * D_RET, 4 * D_RET + D_CONV], axis=-1)
    q = rotary(q.reshape(B, T, N_RET_HEADS, HEAD_DK), pos)
    k = rotary(k.reshape(B, T, N_RET_HEADS, HEAD_DK), pos) * (HEAD_DK ** -0.5)
    v = v.reshape(B, T, N_RET_HEADS, HEAD_DV).astype(jnp.float32)
    o, s_new = retention(q, k, v, ret_state.astype(jnp.float32))
    mu = jnp.mean(o, axis=-1, keepdims=True)
    var = jnp.mean(jnp.square(o - mu), axis=-1, keepdims=True)
    o = ((o - mu) * lax.rsqrt(var + EPS)).reshape(B, T, D_RET) * g_ret_gn.astype(jnp.float32)
    ret_out = (jax.nn.silu(g.astype(jnp.float32)) * o).astype(h.dtype)
    a = u * jax.nn.sigmoid(z)
    cv, buf_new = causal_depthwise_conv(a, conv_buf, w_conv, b_conv)
    conv_out = jax.nn.silu(layer_norm(cv, g_conv_ln, b_conv_ln))
    return jnp.concatenate([ret_out, conv_out], axis=-1), s_new, buf_new


def grouped_moe(h, w_router, b_router, w_gate, w_up, w_down):
    N, D = h.shape
    logits = h.astype(jnp.float32) @ w_router.astype(jnp.float32) + b_router.astype(jnp.float32)
    probs = jax.nn.softmax(logits, axis=-1)
    pg = probs.reshape(N, N_GROUPS, EXPERTS_PER_GROUP)
    group_score = lax.top_k(pg, TOP_K)[0].sum(-1)
    g_sel = jnp.argmax(group_score, axis=-1)
    cand = jnp.take_along_axis(
        pg, jnp.broadcast_to(g_sel[:, None, None], (N, 1, EXPERTS_PER_GROUP)), axis=1)[:, 0]
    wts, local = lax.top_k(cand, TOP_K)
    wts = wts / jnp.sum(wts, axis=-1, keepdims=True)
    e_idx = g_sel[:, None] * EXPERTS_PER_GROUP + local

    n_assign = N * TOP_K
    flat_e = e_idx.reshape(-1)
    counts = jnp.bincount(flat_e, length=N_EXPERTS)
    padded = ((counts + MOE_BLOCK - 1) // MOE_BLOCK) * MOE_BLOCK
    pad_end = jnp.cumsum(padded)
    pad_start = pad_end - padded
    start = jnp.cumsum(counts) - counts
    order = jnp.argsort(flat_e)
    sorted_e = flat_e[order]
    dest = pad_start[sorted_e] + (jnp.arange(n_assign) - start[sorted_e])
    n_blocks = -(-n_assign // MOE_BLOCK) + N_EXPERTS
    cap = n_blocks * MOE_BLOCK
    slot_tok = jnp.full((cap,), N, jnp.int32).at[dest].set((order // TOP_K).astype(jnp.int32))
    slot_w = jnp.zeros((cap,), jnp.float32).at[dest].set(wts.reshape(-1)[order])
    block_e = jnp.minimum(
        jnp.searchsorted(pad_end, jnp.arange(n_blocks) * MOE_BLOCK, side='right'), N_EXPERTS - 1)
    h_pad = jnp.concatenate([h, jnp.zeros((1, D), h.dtype)], axis=0)
    xs = h_pad[slot_tok].reshape(n_blocks, MOE_BLOCK, D)

    def block_ffn(args):
        xb, e = args
        return (jax.nn.silu(xb @ w_gate[e]) * (xb @ w_up[e])) @ w_down[e]

    ys = lax.map(block_ffn, (xs, block_e)).reshape(cap, D)
    ys = (ys * slot_w[:, None]).astype(h.dtype)
    return jnp.zeros((N + 1, D), h.dtype).at[slot_tok].add(ys)[:N]


def trunk(x, c, pos, ret_states, conv_bufs, w_mod, b_mod, g_mix_norm, w_in, w_conv, b_conv,
          g_conv_ln, b_conv_ln, g_ret_gn, w_out, g_ffn_norm, w_router, b_router,
          w_exp_gate, w_exp_up, w_exp_down, g_final):
    B, T, D = x.shape
    cond = jax.nn.silu(c)
    new_ret, new_conv = [], []
    for l in range(DEPTH):
        mod = (cond @ w_mod[l] + b_mod[l])[:, None, :]
        sh1, sc1, gt1, sh2, sc2, gt2 = jnp.split(mod, N_MOD, axis=-1)
        h = rms_norm(x, g_mix_norm[l]) * (1 + sc1) + sh1
        mix, s_new, buf_new = token_mixers(h, pos, ret_states[l], conv_bufs[l], w_in[l], w_conv[l],
                                           b_conv[l], g_conv_ln[l], b_conv_ln[l], g_ret_gn[l])
        x = x + gt1 * (mix @ w_out[l])
        h = rms_norm(x, g_ffn_norm[l]) * (1 + sc2) + sh2
        ff = grouped_moe(h.reshape(B * T, D), w_router, b_router,
                         w_exp_gate[l], w_exp_up[l], w_exp_down[l]).reshape(B, T, D)
        x = x + gt2 * ff
        new_ret.append(s_new)
        new_conv.append(buf_new)
    return rms_norm(x, g_final), jnp.stack(new_ret), jnp.stack(new_conv)


def setup_inputs(seed: int = 0) -> dict:
    key = jax.random.key(seed)
    ks = jax.random.split(key, 24)
    L, D, E, F = DEPTH, D_MODEL, N_EXPERTS, D_FF_EXPERT

    def nrm(k, shape, s):
        return jax.random.normal(k, shape, jnp.float32) * s

    return {
        "x_prompt": nrm(ks[0], (BATCH, SEQ, D), 1.0),
        "x_sample": nrm(ks[1], (DEC_BATCH, DEC_SEQ, D), 1.0),
        "state_ret": nrm(ks[2], (L, DEC_BATCH, N_RET_HEADS, HEAD_DK, HEAD_DV), 0.5),
        "state_conv": nrm(ks[3], (L, DEC_BATCH, CONV_WIDTH - 1, D_CONV), 0.5),
        "c_prompt": nrm(ks[4], (BATCH, D), 1.0),
        "c_sample": nrm(ks[5], (DEC_BATCH, D), 1.0),
        "w_mod": nrm(ks[6], (L, D, N_MOD * D), 0.5 * D ** -0.5),
        "b_mod": nrm(ks[7], (L, N_MOD * D), 0.02),
        "g_mix_norm": 1.0 + nrm(ks[8], (L, D), 0.02),
        "w_in": nrm(ks[9], (L, D, D_IN), D ** -0.5),
        "w_conv": nrm(ks[10], (L, CONV_WIDTH, D_CONV), CONV_WIDTH ** -0.5),
        "b_conv": nrm(ks[11], (L, D_CONV), 0.02),
        "g_conv_ln": 1.0 + nrm(ks[12], (L, D_CONV), 0.02),
        "b_conv_ln": nrm(ks[13], (L, D_CONV), 0.02),
        "g_ret_gn": 1.0 + nrm(ks[14], (L, D_RET), 0.02),
        "w_out": nrm(ks[15], (L, D_MIX, D), D_MIX ** -0.5),
        "g_ffn_norm": 1.0 + nrm(ks[16], (L, D), 0.02),
        "w_router": nrm(ks[17], (D, E), D ** -0.5),
        "b_router": nrm(ks[18], (E,), 0.01),
        "w_exp_gate": nrm(ks[19], (L, E, D, F), D ** -0.5),
        "w_exp_up": nrm(ks[20], (L, E, D, F), D ** -0.5),
        "w_exp_down": nrm(ks[21], (L, E, F, D), F ** -0.5),
        "g_final": 1.0 + nrm(ks[22], (D,), 0.02),
    }


def reference(x_prompt, x_sample, state_ret, state_conv, c_prompt, c_sample, w_mod, b_mod,
              g_mix_norm, w_in, w_conv, b_conv, g_conv_ln, b_conv_ln, g_ret_gn, w_out,
              g_ffn_norm, w_router, b_router, w_exp_gate, w_exp_up, w_exp_down, g_final):
    weights = (w_mod, b_mod, g_mix_norm, w_in, w_conv, b_conv, g_conv_ln, b_conv_ln, g_ret_gn,
               w_out, g_ffn_norm, w_router, b_router, w_exp_gate, w_exp_up, w_exp_down, g_final)
    bp = x_prompt.shape[0]
    zero_ret = jnp.zeros((DEPTH, bp, N_RET_HEADS, HEAD_DK, HEAD_DV), jnp.float32)
    zero_conv = jnp.zeros((DEPTH, bp, CONV_WIDTH - 1, D_CONV), x_prompt.dtype)
    pos_prompt = jnp.arange(x_prompt.shape[1], dtype=jnp.int32)
    pos_sample = PAST_LEN + jnp.arange(x_sample.shape[1], dtype=jnp.int32)
    y_prompt, ret_p, conv_p = trunk(x_prompt, c_prompt, pos_prompt, zero_ret, zero_conv, *weights)
    y_sample, ret_s, conv_s = trunk(x_sample, c_sample, pos_sample, state_ret, state_conv, *weights)
    return (y_prompt, y_sample, ret_p, conv_p, ret_s, conv_s)
```

```python
import functools

import jax
import jax.numpy as jnp
from jax import lax
from jax.experimental import pallas as pl
from jax.experimental.pallas import tpu as pltpu

f32 = jnp.float32
bf16 = jnp.bfloat16
i32 = jnp.int32

D_MODEL = 1024
D_RET = 512
D_CONV = 512
N_HEADS = 4
HEAD_D = 128
RET_CHUNK = 128
ROPE_BASE = 10000.0
CONV_WIDTH = 31
CONV_HALO = CONV_WIDTH - 1
N_EXPERTS = 16
N_GROUPS = 4
GROUP_SIZE = N_EXPERTS // N_GROUPS
N_MOD = 6
EPS = 1e-6
PAST_LEN = 16384
D_IN = 4 * D_RET + 2 * D_CONV

LANES = 128
SUBLANES = 8
TM = 256
BM = 256
CONV_ROWS = 64
SAMPLE_GROUP = 8
HALO_PAD = 32
VMEM_LIMIT = 56 * 1024 * 1024


def _cparams(n_axes, vmem=VMEM_LIMIT):
    return pltpu.CompilerParams(dimension_semantics=("arbitrary",) * n_axes, vmem_limit_bytes=vmem)


def _mod_kernel(c_ref, w_ref, b_ref, o_ref):
    cond = jax.nn.silu(c_ref[...]).astype(bf16)
    o_ref[...] = jnp.dot(cond, w_ref[...].astype(bf16), preferred_element_type=f32) + b_ref[...]


def _modulation(c_all, w_mod, b_mod):
    depth = w_mod.shape[0]
    m = c_all.shape[0]
    return pl.pallas_call(
        _mod_kernel,
        grid=(depth, N_MOD),
        in_specs=[
            pl.BlockSpec((m, D_MODEL), lambda l, j: (0, 0)),
            pl.BlockSpec((None, D_MODEL, D_MODEL), lambda l, j: (l, 0, j)),
            pl.BlockSpec((None, 1, D_MODEL), lambda l, j: (l, 0, j)),
        ],
        out_specs=pl.BlockSpec((None, m, D_MODEL), lambda l, j: (l, 0, j)),
        out_shape=jax.ShapeDtypeStruct((depth, m, N_MOD * D_MODEL), f32),
        compiler_params=_cparams(2),
        name="modulation",
    )(c_all, w_mod, b_mod.reshape(depth, 1, N_MOD * D_MODEL))


class _Tiles:
    def __init__(self, bp, tp, bs, ts):
        self.bp, self.tp, self.bs, self.ts = bp, tp, bs, ts
        self.n_prompt = bp * tp
        self.n_sample = bs * ts
        self.n_tok = self.n_prompt + self.n_sample
        assert tp % TM == 0 and self.n_sample % TM == 0
        self.tiles_per_seq = tp // TM
        self.np_tiles = self.n_prompt // TM
        self.ns_tiles = self.n_sample // TM
        self.n_tiles = self.np_tiles + self.ns_tiles

    def prompt_block(self, i):
        return jnp.minimum(i, self.np_tiles - 1)

    def sample_block(self, i):
        return jnp.maximum(i - self.np_tiles, 0)

    def seq_index(self, i):
        return jnp.minimum(i // self.tiles_per_seq, self.bp - 1)


def _mod_specs(tl, layer, col):
    seq = pl.BlockSpec((None, None, 1, D_MODEL), lambda i: (layer, tl.seq_index(i), 0, col))
    tok = pl.BlockSpec((None, TM, D_MODEL), lambda i: (layer, tl.sample_block(i), col))
    return seq, tok


def _pick(is_sample, tok_ref, seq_ref):
    return jnp.where(is_sample, tok_ref[...], seq_ref[...])


def _rms(x, g):
    return x * lax.rsqrt(jnp.mean(x * x, axis=-1, keepdims=True) + EPS) * g


def _inproj_kernel(*refs, np_tiles, split_x):
    if split_x:
        xp_ref, xs_ref = refs[:2]
        refs = refs[2:]
    else:
        x_ref = refs[0]
        refs = refs[1:]
    (shs_ref, sht_ref, scs_ref, sct_ref, g_ref, w_ref, cos_ref, sin_ref,
     q_ref, k_ref, v_ref, gate_ref, a_ref) = refs
    is_s = pl.program_id(0) >= np_tiles
    x = jnp.where(is_s, xs_ref[...], xp_ref[...]) if split_x else x_ref[...]
    h = _rms(x, g_ref[...]) * (1.0 + _pick(is_s, sct_ref, scs_ref)) + _pick(is_s, sht_ref, shs_ref)
    hb = h.astype(bf16)
    cos = cos_ref[...]
    sin = sin_ref[...]

    def proj(col):
        return jnp.dot(hb, w_ref[:, col * D_RET:(col + 1) * D_RET], preferred_element_type=f32)

    def rope(t):
        outs = []
        for hd in range(N_HEADS):
            th = t[:, hd * HEAD_D:(hd + 1) * HEAD_D]
            outs.append(th * cos + pltpu.roll(th, HEAD_D // 2, 1) * sin)
        return outs

    for hd, qh in enumerate(rope(proj(0))):
        q_ref[:, hd * HEAD_D:(hd + 1) * HEAD_D] = qh
    for hd, kh in enumerate(rope(proj(1))):
        k_ref[:, hd * HEAD_D:(hd + 1) * HEAD_D] = kh * (HEAD_D ** -0.5)
    v_ref[...] = proj(2)
    gate_ref[...] = proj(3)
    a_ref[...] = proj(4) * jax.nn.sigmoid(proj(5))


def _inproj(tl, layer, x, mod_seq, mod_tok, g_norm, w_in_bf, cos_tab, sin_tab):
    split_x = isinstance(x, tuple)
    tok_spec = pl.BlockSpec((TM, D_MODEL), lambda i: (i, 0))
    if split_x:
        x_args = list(x)
        x_specs = [pl.BlockSpec((TM, D_MODEL), lambda i: (tl.prompt_block(i), 0)),
                   pl.BlockSpec((TM, D_MODEL), lambda i: (tl.sample_block(i), 0))]
    else:
        x_args, x_specs = [x], [tok_spec]
    sh_seq, sh_tok = _mod_specs(tl, layer, 0)
    sc_seq, sc_tok = _mod_specs(tl, layer, 1)

    def table_block(i):
        return jnp.where(i < tl.np_tiles, i % tl.tiles_per_seq, tl.tiles_per_seq + tl.sample_block(i))

    tab_spec = pl.BlockSpec((TM, HEAD_D), lambda i: (table_block(i), 0))
    out_spec = pl.BlockSpec((TM, D_RET), lambda i: (i, 0))
    out_sd = jax.ShapeDtypeStruct((tl.n_tok, D_RET), f32)
    return pl.pallas_call(
        functools.partial(_inproj_kernel, np_tiles=tl.np_tiles, split_x=split_x),
        grid=(tl.n_tiles,),
        in_specs=x_specs + [
            sh_seq, sh_tok, sc_seq, sc_tok,
            pl.BlockSpec((None, 1, D_MODEL), lambda i: (layer, 0, 0)),
            pl.BlockSpec((None, D_MODEL, D_IN), lambda i: (layer, 0, 0)),
            tab_spec, tab_spec,
        ],
        out_specs=[out_spec] * 5,
        out_shape=[out_sd] * 5,
        compiler_params=_cparams(1),
        name="inproj",
    )(*x_args, mod_seq, mod_tok, mod_seq, mod_tok, g_norm, w_in_bf, cos_tab, sin_tab)


def _head_norm_gate(o, gn, gate):
    mu = jnp.mean(o, axis=-1, keepdims=True)
    var = jnp.mean(jnp.square(o - mu), axis=-1, keepdims=True)
    return jax.nn.silu(gate) * ((o - mu) * lax.rsqrt(var + EPS) * gn)


def _dot_nt(a, b):
    return lax.dot_general(a, b, (((1,), (1,)), ((), ())), preferred_element_type=f32)


def _dot_tn(a, b):
    return lax.dot_general(a, b, (((0,), (0,)), ((), ())), preferred_element_type=f32)


def _ret_prompt_kernel(q_ref, k_ref, v_ref, gate_ref, dec_ref, qd_ref, kd_ref, cd_ref, gn_ref,
                       o_ref, s_out_ref, s_ref):
    c = pl.program_id(1)

    @pl.when(c == 0)
    def _():
        s_ref[...] = jnp.zeros_like(s_ref)

    for hd in range(N_HEADS):
        sl = slice(hd * HEAD_D, (hd + 1) * HEAD_D)
        kh = k_ref[:, sl]
        qb = q_ref[:, sl].astype(bf16)
        kb = kh.astype(bf16)
        vb = v_ref[:, sl].astype(bf16)
        s_old = s_ref[hd]
        scores = _dot_nt(qb, kb) * dec_ref[hd]
        inner = jnp.dot(scores.astype(bf16), vb, preferred_element_type=f32)
        cross = jnp.dot(qb, s_old.astype(bf16), preferred_element_type=f32) * qd_ref[hd]
        s_ref[hd] = s_old * cd_ref[hd] + _dot_tn((kh * kd_ref[hd]).astype(bf16), vb)
        o_ref[:, sl] = _head_norm_gate(inner + cross, gn_ref[:, sl], gate_ref[:, sl])

    @pl.when(c == pl.num_programs(1) - 1)
    def _():
        s_out_ref[...] = s_ref[...]


def _decay_tables(chunk, true_len):
    lg = jnp.log(1.0 - 2.0 ** (-5.0 - jnp.arange(N_HEADS, dtype=f32)))
    idx = jnp.arange(chunk, dtype=f32)
    rel = idx[:, None] - idx[None, :]
    decay = jnp.where(rel[None] >= 0, jnp.exp(jnp.maximum(rel, 0.0)[None] * lg[:, None, None]), 0.0)
    q_decay = jnp.exp((idx[None, :] + 1.0) * lg[:, None])
    k_decay = jnp.exp((true_len - 1.0 - idx[None, :]) * lg[:, None])
    c_decay = jnp.exp(true_len * lg)
    return decay, q_decay, k_decay, c_decay


def _retention_prompt(tl, layer, q, k, v, gate, g_ret_gn):
    n_chunks = tl.tp // RET_CHUNK
    decay, q_decay, k_decay, c_decay = _decay_tables(RET_CHUNK, RET_CHUNK)
    bcast = lambda t: jnp.broadcast_to(t[:, :, None], (N_HEADS, RET_CHUNK, HEAD_D))
    cd = jnp.broadcast_to(c_decay[:, None, None], (N_HEADS, 1, HEAD_D))
    tok_spec = pl.BlockSpec((RET_CHUNK, D_RET), lambda b, c: (b * n_chunks + c, 0))
    tab_spec = pl.BlockSpec((N_HEADS, RET_CHUNK, HEAD_D), lambda b, c: (0, 0, 0))
    return pl.pallas_call(
        _ret_prompt_kernel,
        grid=(tl.bp, n_chunks),
        in_specs=[tok_spec] * 4 + [tab_spec] * 3 + [
            pl.BlockSpec((N_HEADS, 1, HEAD_D), lambda b, c: (0, 0, 0)),
            pl.BlockSpec((None, 1, D_RET), lambda b, c: (layer, 0, 0)),
        ],
        out_specs=[tok_spec, pl.BlockSpec((None, N_HEADS, HEAD_D, HEAD_D), lambda b, c: (b, 0, 0, 0))],
        out_shape=[jax.ShapeDtypeStruct((tl.n_prompt, D_RET), f32),
                   jax.ShapeDtypeStruct((tl.bp, N_HEADS, HEAD_D, HEAD_D), f32)],
        scratch_shapes=[pltpu.VMEM((N_HEADS, HEAD_D, HEAD_D), f32)],
        compiler_params=_cparams(2),
        name="retention_prompt",
    )(q, k, v, gate, decay, bcast(q_decay), bcast(k_decay), cd, g_ret_gn)


def _ret_sample_kernel(q_ref, k_ref, v_ref, gate_ref, s_in_ref, dec_ref, qd_ref, kd_ref, cd_ref, gn_ref,
                       o_ref, s_out_ref, *, ts):
    seqs_per_tile = SUBLANES // ts
    row = lax.broadcasted_iota(i32, (SUBLANES, HEAD_D), 0)
    for t in range(SAMPLE_GROUP // seqs_per_tile):
        rows = slice(t * SUBLANES, (t + 1) * SUBLANES)
        for hd in range(N_HEADS):
            sl = slice(hd * HEAD_D, (hd + 1) * HEAD_D)
            qh = q_ref[rows, sl]
            kh = k_ref[rows, sl] * kd_ref[hd]
            vb = v_ref[rows, sl].astype(bf16)
            qb = qh.astype(bf16)
            scores = _dot_nt(qb, k_ref[rows, sl].astype(bf16)) * dec_ref[hd]
            o = jnp.dot(scores.astype(bf16), vb, preferred_element_type=f32)
            for s in range(seqs_per_tile):
                b = t * seqs_per_tile + s
                mine = (row >= s * ts) & (row < (s + 1) * ts)
                s_old = s_in_ref[b, hd]
                q_s = jnp.where(mine, qh, 0.0).astype(bf16)
                k_s = jnp.where(mine, kh, 0.0).astype(bf16)
                o = o + jnp.dot(q_s, s_old.astype(bf16), preferred_element_type=f32) * qd_ref[hd]
                s_out_ref[b, hd] = s_old * cd_ref[hd] + _dot_tn(k_s, vb)
            o_ref[rows, sl] = _head_norm_gate(o, gn_ref[:, sl], gate_ref[rows, sl])


def _retention_sample(tl, layer, q, k, v, gate, state_ret, g_ret_gn):
    ts = tl.ts
    assert SUBLANES % ts == 0 and tl.bs % SAMPLE_GROUP == 0
    seqs_per_tile = SUBLANES // ts
    decay, q_decay, k_decay, c_decay = _decay_tables(ts, ts)
    eye = jnp.eye(seqs_per_tile, dtype=f32)
    dec_tile = jnp.einsum("ab,hij->haibj", eye, decay).reshape(N_HEADS, SUBLANES, SUBLANES)
    tile_rows = lambda t: jnp.broadcast_to(jnp.tile(t, (1, seqs_per_tile))[:, :, None],
                                           (N_HEADS, SUBLANES, HEAD_D))
    cd = jnp.broadcast_to(c_decay[:, None, None], (N_HEADS, 1, HEAD_D))
    rows = SAMPLE_GROUP * ts
    first = tl.n_prompt // rows
    tok_spec = pl.BlockSpec((rows, D_RET), lambda i: (first + i, 0))
    const3 = lambda shape: pl.BlockSpec(shape, lambda i: (0, 0, 0))
    st_block = (SAMPLE_GROUP, N_HEADS, HEAD_D, HEAD_D)
    return pl.pallas_call(
        functools.partial(_ret_sample_kernel, ts=ts),
        grid=(tl.bs // SAMPLE_GROUP,),
        in_specs=[tok_spec] * 4 + [
            pl.BlockSpec((None,) + st_block, lambda i: (layer, i, 0, 0, 0)),
            const3((N_HEADS, SUBLANES, SUBLANES)),
            const3((N_HEADS, SUBLANES, HEAD_D)),
            const3((N_HEADS, SUBLANES, HEAD_D)),
            const3((N_HEADS, 1, HEAD_D)),
            pl.BlockSpec((None, 1, D_RET), lambda i: (layer, 0, 0)),
        ],
        out_specs=[pl.BlockSpec((rows, D_RET), lambda i: (i, 0)),
                   pl.BlockSpec(st_block, lambda i: (i, 0, 0, 0))],
        out_shape=[jax.ShapeDtypeStruct((tl.n_sample, D_RET), f32),
                   jax.ShapeDtypeStruct((tl.bs, N_HEADS, HEAD_D, HEAD_D), f32)],
        compiler_params=_cparams(1),
        name="retention_sample",
    )(q, k, v, gate, state_ret, dec_tile, tile_rows(q_decay), tile_rows(k_decay), cd, g_ret_gn)


def _ln_silu(cv, g, b):
    mu = jnp.mean(cv, axis=-1, keepdims=True)
    var = jnp.mean(jnp.square(cv - mu), axis=-1, keepdims=True)
    return jax.nn.silu((cv - mu) * lax.rsqrt(var + EPS) * g + b)


def _conv_prompt_kernel(a_ref, w_ref, b_ref, g_ref, bl_ref, o_ref, buf_ref, full_ref):
    j = pl.program_id(1)

    @pl.when(j == 0)
    def _():
        full_ref[0:HALO_PAD, :] = jnp.zeros((HALO_PAD, D_CONV), f32)

    @pl.when(j > 0)
    def _():
        full_ref[0:HALO_PAD, :] = full_ref[TM:TM + HALO_PAD, :]

    full_ref[HALO_PAD:HALO_PAD + TM, :] = a_ref[...]
    shift = HALO_PAD - CONV_HALO
    for r0 in range(0, TM, CONV_ROWS):
        acc = jnp.broadcast_to(b_ref[...], (CONV_ROWS, D_CONV))
        for tap in range(CONV_WIDTH):
            acc = acc + full_ref[r0 + tap + shift:r0 + tap + shift + CONV_ROWS, :] * w_ref[tap:tap + 1, :]
        o_ref[r0:r0 + CONV_ROWS, :] = _ln_silu(acc, g_ref[...], bl_ref[...])

    @pl.when(j == pl.num_programs(1) - 1)
    def _():
        buf_ref[...] = a_ref[TM - CONV_HALO:TM, :]


def _conv_prompt(tl, layer, a, w_conv, b_conv, g_ln, b_ln):
    tps = tl.tiles_per_seq
    vec = pl.BlockSpec((None, 1, D_CONV), lambda b, j: (layer, 0, 0))
    return pl.pallas_call(
        _conv_prompt_kernel,
        grid=(tl.bp, tps),
        in_specs=[pl.BlockSpec((TM, D_CONV), lambda b, j: (b * tps + j, 0)),
                  pl.BlockSpec((None, CONV_WIDTH, D_CONV), lambda b, j: (layer, 0, 0)),
                  vec, vec, vec],
        out_specs=[pl.BlockSpec((TM, D_CONV), lambda b, j: (b * tps + j, 0)),
                   pl.BlockSpec((None, CONV_HALO, D_CONV), lambda b, j: (b, 0, 0))],
        out_shape=[jax.ShapeDtypeStruct((tl.n_prompt, D_CONV), f32),
                   jax.ShapeDtypeStruct((tl.bp, CONV_HALO, D_CONV), f32)],
        scratch_shapes=[pltpu.VMEM((HALO_PAD + TM, D_CONV), f32)],
        compiler_params=_cparams(2),
        name="conv_prompt",
    )(a, w_conv, b_conv, g_ln, b_ln)


def _conv_sample_kernel(a_ref, st_ref, w_ref, b_ref, g_ref, bl_ref, o_ref, buf_ref, full_ref, cv_ref, *, ts):
    for s in range(SAMPLE_GROUP):
        full_ref[0:CONV_HALO, :] = st_ref[s]
        full_ref[CONV_HALO:CONV_HALO + ts, :] = a_ref[s * ts:(s + 1) * ts, :]
        acc = jnp.broadcast_to(b_ref[...], (ts, D_CONV))
        for tap in range(CONV_WIDTH):
            acc = acc + full_ref[tap:tap + ts, :] * w_ref[tap:tap + 1, :]
        cv_ref[s * ts:(s + 1) * ts, :] = acc
        buf_ref[s] = full_ref[ts:ts + CONV_HALO, :]
    o_ref[...] = _ln_silu(cv_ref[...], g_ref[...], bl_ref[...])


def _conv_sample(tl, layer, a, state_conv, w_conv, b_conv, g_ln, b_ln):
    ts = tl.ts
    rows = SAMPLE_GROUP * ts
    first = tl.n_prompt // rows
    vec = pl.BlockSpec((None, 1, D_CONV), lambda i: (layer, 0, 0))
    return pl.pallas_call(
        functools.partial(_conv_sample_kernel, ts=ts),
        grid=(tl.bs // SAMPLE_GROUP,),
        in_specs=[pl.BlockSpec((rows, D_CONV), lambda i: (first + i, 0)),
                  pl.BlockSpec((None, SAMPLE_GROUP, CONV_HALO, D_CONV), lambda i: (layer, i, 0, 0)),
                  pl.BlockSpec((None, CONV_WIDTH, D_CONV), lambda i: (layer, 0, 0)),
                  vec, vec, vec],
        out_specs=[pl.BlockSpec((rows, D_CONV), lambda i: (i, 0)),
                   pl.BlockSpec((SAMPLE_GROUP, CONV_HALO, D_CONV), lambda i: (i, 0, 0))],
        out_shape=[jax.ShapeDtypeStruct((tl.n_sample, D_CONV), f32),
                   jax.ShapeDtypeStruct((tl.bs, CONV_HALO, D_CONV), f32)],
        scratch_shapes=[pltpu.VMEM((CONV_HALO + ts + SUBLANES, D_CONV), f32),
                        pltpu.VMEM((rows, D_CONV), f32)],
        compiler_params=_cparams(1),
        name="conv_sample",
    )(a, state_conv, w_conv, b_conv, g_ln, b_ln)


def _outproj_kernel(*refs, np_tiles, split_x):
    if split_x:
        xp_ref, xs_ref = refs[:2]
        refs = refs[2:]
    else:
        x_ref = refs[0]
        refs = refs[1:]
    (retp_ref, rets_ref, cvp_ref, cvs_ref, gts_ref, gtt_ref, shs_ref, sht_ref, scs_ref, sct_ref,
     g_ref, wo_ref, wr_ref, br_ref,
     xo_ref, h2_ref, ri_ref, rw_ref, cnt_ref, run_ref) = refs
    i = pl.program_id(0)
    is_s = i >= np_tiles

    @pl.when(i == 0)
    def _():
        run_ref[...] = jnp.zeros_like(run_ref)

    x = jnp.where(is_s, xs_ref[...], xp_ref[...]) if split_x else x_ref[...]
    ret = jnp.where(is_s, rets_ref[...], retp_ref[...]).astype(bf16)
    cv = jnp.where(is_s, cvs_ref[...], cvp_ref[...]).astype(bf16)
    mix_out = (jnp.dot(ret, wo_ref[0:D_RET, :], preferred_element_type=f32)
               + jnp.dot(cv, wo_ref[D_RET:D_RET + D_CONV, :], preferred_element_type=f32))
    xn = x + _pick(is_s, gtt_ref, gts_ref) * mix_out
    xo_ref[...] = xn
    h2 = _rms(xn, g_ref[...]) * (1.0 + _pick(is_s, sct_ref, scs_ref)) + _pick(is_s, sht_ref, shs_ref)
    h2_ref[...] = h2

    lane = lax.broadcasted_iota(i32, (TM, LANES), 1)
    lane_f = lane.astype(f32)
    logits = jnp.dot(h2, wr_ref[...], preferred_element_type=f32, precision=lax.Precision.HIGHEST) + br_ref[...]
    logits = jnp.where(lane < N_EXPERTS, logits, -jnp.inf)
    ex = jnp.exp(logits - jnp.max(logits, axis=-1, keepdims=True))
    p = ex / jnp.sum(ex, axis=-1, keepdims=True)

    best = None
    for g in range(N_GROUPS):
        in_group = (lane >= g * GROUP_SIZE) & (lane < (g + 1) * GROUP_SIZE)
        pm = jnp.where(in_group, p, -1.0)
        m1 = jnp.max(pm, axis=-1, keepdims=True)
        i1 = jnp.min(jnp.where(pm == m1, lane_f, float(LANES)), axis=-1, keepdims=True)
        pm2 = jnp.where(lane_f == i1, -1.0, pm)
        m2 = jnp.max(pm2, axis=-1, keepdims=True)
        i2 = jnp.min(jnp.where(pm2 == m2, lane_f, float(LANES)), axis=-1, keepdims=True)
        cand = (m1 + m2, m1, m2, i1, i2)
        if best is None:
            best = cand
        else:
            take = cand[0] > best[0]
            best = tuple(jnp.where(take, c, b) for c, b in zip(cand, best))
    _, m1, m2, i1, i2 = best
    denom = m1 + m2
    w0 = m1 / denom
    w1 = m2 / denom

    oh0 = lane_f == i1
    oh1 = lane_f == i2
    ind = jnp.where(oh0 | oh1, 1.0, 0.0)
    r_i = lax.broadcasted_iota(i32, (TM, TM), 0)
    c_i = lax.broadcasted_iota(i32, (TM, TM), 1)
    lower = jnp.where(c_i < r_i, 1.0, 0.0).astype(bf16)
    before = jnp.dot(lower, ind.astype(bf16), preferred_element_type=f32) + run_ref[0:1, :]
    rank0 = jnp.sum(jnp.where(oh0, before, 0.0), axis=-1, keepdims=True)
    rank1 = jnp.sum(jnp.where(oh1, before, 0.0), axis=-1, keepdims=True)
    run_ref[0:1, :] = run_ref[0:1, :] + jnp.sum(ind, axis=0, keepdims=True)

    zero = jnp.zeros((TM, LANES), f32)
    ri = jnp.where(lane == 0, i1, jnp.where(lane == 1, i2, jnp.where(lane == 2, rank0,
                                                                    jnp.where(lane == 3, rank1, zero))))
    ri_ref[...] = ri.astype(i32)
    rw_ref[...] = jnp.where(lane == 0, w0, jnp.where(lane == 1, w1, zero))
    cnt_ref[...] = run_ref[...]


def _outproj(tl, layer, x, ret_p, ret_s, cv_p, cv_s, mod_seq, mod_tok, g_norm, w_out_bf, wr_pad, br_pad):
    split_x = isinstance(x, tuple)
    tok_spec = pl.BlockSpec((TM, D_MODEL), lambda i: (i, 0))
    p_spec = lambda w: pl.BlockSpec((TM, w), lambda i: (tl.prompt_block(i), 0))
    s_spec = lambda w: pl.BlockSpec((TM, w), lambda i: (tl.sample_block(i), 0))
    if split_x:
        x_args, x_specs = list(x), [p_spec(D_MODEL), s_spec(D_MODEL)]
    else:
        x_args, x_specs = [x], [tok_spec]
    mods = []
    for col in (2, 3, 4):
        mods += list(_mod_specs(tl, layer, col))
    lane_spec = pl.BlockSpec((TM, LANES), lambda i: (i, 0))
    return pl.pallas_call(
        functools.partial(_outproj_kernel, np_tiles=tl.np_tiles, split_x=split_x),
        grid=(tl.n_tiles,),
        in_specs=x_specs + [p_spec(D_RET), s_spec(D_RET), p_spec(D_CONV), s_spec(D_CONV)] + mods + [
            pl.BlockSpec((None, 1, D_MODEL), lambda i: (layer, 0, 0)),
            pl.BlockSpec((None, D_MODEL, D_MODEL), lambda i: (layer, 0, 0)),
            pl.BlockSpec((D_MODEL, LANES), lambda i: (0, 0)),
            pl.BlockSpec((1, LANES), lambda i: (0, 0)),
        ],
        out_specs=[tok_spec, tok_spec, lane_spec, lane_spec,
                   pl.BlockSpec((SUBLANES, LANES), lambda i: (0, 0))],
        out_shape=[jax.ShapeDtypeStruct((tl.n_tok, D_MODEL), f32),
                   jax.ShapeDtypeStruct((tl.n_tok, D_MODEL), f32),
                   jax.ShapeDtypeStruct((tl.n_tok, LANES), i32),
                   jax.ShapeDtypeStruct((tl.n_tok, LANES), f32),
                   jax.ShapeDtypeStruct((SUBLANES, LANES), f32)],
        scratch_shapes=[pltpu.VMEM((SUBLANES, LANES), f32)],
        compiler_params=_cparams(1),
        name="outproj_router",
    )(*x_args, ret_p, ret_s, cv_p, cv_s, *([mod_seq, mod_tok] * 3), g_norm, w_out_bf, wr_pad, br_pad)


PLAN_CHUNK_MAX = 4096


def _plan_kernel(d0_ref, d1_ref, tok_ref, *, chunk):
    base = pl.program_id(0) * chunk

    @pl.when(pl.program_id(0) == 0)
    def _():
        def fill(s, carry):
            tok_ref[s] = 0
            return carry

        lax.fori_loop(0, tok_ref.shape[0], fill, 0, unroll=8)

    def body(r, carry):
        tok_ref[d0_ref[0, 0, r]] = base + r
        tok_ref[d1_ref[0, 0, r]] = base + r
        return carry

    lax.fori_loop(0, chunk, body, 0, unroll=8)


def _plan(dest0, dest1, cap):
    n = dest0.shape[0]
    n_tiles = n // TM
    per = max(k for k in range(1, n_tiles + 1) if n_tiles % k == 0 and k * TM <= PLAN_CHUNK_MAX)
    chunk = per * TM
    n_chunks = n // chunk
    d_spec = pl.BlockSpec((1, 1, chunk), lambda i: (i, 0, 0), memory_space=pltpu.SMEM)
    whole = pl.BlockSpec(memory_space=pltpu.SMEM)
    return pl.pallas_call(
        functools.partial(_plan_kernel, chunk=chunk),
        grid=(n_chunks,),
        in_specs=[d_spec, d_spec],
        out_specs=whole,
        out_shape=jax.ShapeDtypeStruct((cap,), i32),
        compiler_params=_cparams(1),
        name="moe_plan",
    )(dest0.reshape(n_chunks, 1, chunk), dest1.reshape(n_chunks, 1, chunk))


def _expert_kernel(be_ref, tok_ref, h2_hbm, wg_ref, wu_ref, wd_ref, ys_ref, xbuf, sem):
    del be_ref

    def row_copy(r):
        return pltpu.make_async_copy(h2_hbm.at[pl.ds(tok_ref[0, 0, r], 1)], xbuf.at[pl.ds(r, 1)], sem)

    def issue(r, carry):
        row_copy(r).start()
        return carry

    lax.fori_loop(0, BM, issue, 0, unroll=8)
    pltpu.make_async_copy(h2_hbm.at[pl.ds(0, BM)], xbuf, sem).wait()
    x = xbuf[...].astype(bf16)
    gate = jnp.dot(x, wg_ref[...].astype(bf16), preferred_element_type=f32)
    up = jnp.dot(x, wu_ref[...].astype(bf16), preferred_element_type=f32)
    mid = (jax.nn.silu(gate) * up).astype(bf16)
    ys_ref[...] = jnp.dot(mid, wd_ref[...].astype(bf16), preferred_element_type=f32)


def _experts(layer, block_e, slot_tok, h2, w_gate, w_up, w_down):
    n_blocks = block_e.shape[0]
    d_ff = w_gate.shape[-1]
    w_spec = lambda a, b: pl.BlockSpec((None, None, a, b), lambda j, be: (layer, be[j], 0, 0))
    grid_spec = pltpu.PrefetchScalarGridSpec(
        num_scalar_prefetch=1,
        grid=(n_blocks,),
        in_specs=[pl.BlockSpec((1, 1, BM), lambda j, be: (j, 0, 0), memory_space=pltpu.SMEM),
                  pl.BlockSpec(memory_space=pl.ANY),
                  w_spec(D_MODEL, d_ff), w_spec(D_MODEL, d_ff), w_spec(d_ff, D_MODEL)],
        out_specs=pl.BlockSpec((BM, D_MODEL), lambda j, be: (j, 0)),
        scratch_shapes=[pltpu.VMEM((BM, D_MODEL), f32), pltpu.SemaphoreType.DMA(())],
    )
    return pl.pallas_call(
        _expert_kernel,
        grid_spec=grid_spec,
        out_shape=jax.ShapeDtypeStruct((n_blocks * BM, D_MODEL), f32),
        compiler_params=_cparams(1),
        name="moe_experts",
    )(block_e, slot_tok.reshape(n_blocks, 1, BM), h2, w_gate, w_up, w_down)


def _combine_kernel(d0_ref, d1_ref, ys_hbm, x_ref, rw_ref, gts_ref, gtt_ref, *rest, np_tiles, final):
    if final:
        gf_ref, yp_ref, ysm_ref, b0, b1, sem = rest
    else:
        xo_ref, b0, b1, sem = rest
    i = pl.program_id(0)
    is_s = i >= np_tiles

    def issue(r, carry):
        pltpu.make_async_copy(ys_hbm.at[pl.ds(d0_ref[0, 0, r], 1)], b0.at[pl.ds(r, 1)], sem.at[0]).start()
        pltpu.make_async_copy(ys_hbm.at[pl.ds(d1_ref[0, 0, r], 1)], b1.at[pl.ds(r, 1)], sem.at[1]).start()
        return carry

    lax.fori_loop(0, TM, issue, 0, unroll=8)
    pltpu.make_async_copy(ys_hbm.at[pl.ds(0, TM)], b0, sem.at[0]).wait()
    pltpu.make_async_copy(ys_hbm.at[pl.ds(0, TM)], b1, sem.at[1]).wait()
    ff = b0[...] * rw_ref[:, 0:1] + b1[...] * rw_ref[:, 1:2]
    xn = x_ref[...] + _pick(is_s, gtt_ref, gts_ref) * ff
    if final:
        y = _rms(xn, gf_ref[...])

        @pl.when(jnp.logical_not(is_s))
        def _():
            yp_ref[...] = y

        @pl.when(is_s)
        def _():
            ysm_ref[...] = y
    else:
        xo_ref[...] = xn


def _combine(tl, layer, dest0, dest1, ys, x, route_w, mod_seq, mod_tok, g_final):
    final = g_final is not None
    d_spec = pl.BlockSpec((1, 1, TM), lambda i: (i, 0, 0), memory_space=pltpu.SMEM)
    tok_spec = pl.BlockSpec((TM, D_MODEL), lambda i: (i, 0))
    gt_seq, gt_tok = _mod_specs(tl, layer, 5)
    in_specs = [d_spec, d_spec, pl.BlockSpec(memory_space=pl.ANY), tok_spec,
                pl.BlockSpec((TM, LANES), lambda i: (i, 0)), gt_seq, gt_tok]
    args = [dest0.reshape(tl.n_tiles, 1, TM), dest1.reshape(tl.n_tiles, 1, TM), ys, x, route_w, mod_seq, mod_tok]
    if final:
        in_specs.append(pl.BlockSpec((1, D_MODEL), lambda i: (0, 0)))
        args.append(g_final)
        out_specs = [pl.BlockSpec((TM, D_MODEL), lambda i: (tl.prompt_block(i), 0)),
                     pl.BlockSpec((TM, D_MODEL), lambda i: (tl.sample_block(i), 0))]
        out_shape = [jax.ShapeDtypeStruct((tl.n_prompt, D_MODEL), f32),
                     jax.ShapeDtypeStruct((tl.n_sample, D_MODEL), f32)]
    else:
        out_specs = tok_spec
        out_shape = jax.ShapeDtypeStruct((tl.n_tok, D_MODEL), f32)
    return pl.pallas_call(
        functools.partial(_combine_kernel, np_tiles=tl.np_tiles, final=final),
        grid=(tl.n_tiles,),
        in_specs=in_specs,
        out_specs=out_specs,
        out_shape=out_shape,
        scratch_shapes=[pltpu.VMEM((TM, D_MODEL), f32), pltpu.VMEM((TM, D_MODEL), f32),
                        pltpu.SemaphoreType.DMA((2,))],
        compiler_params=_cparams(1),
        name="moe_combine",
    )(*args)


def _rope_tables(tl):
    half = HEAD_D // 2
    inv = ROPE_BASE ** (-jnp.arange(half, dtype=f32) / half)
    pos_p = jnp.arange(tl.tp, dtype=i32)
    pos_s = PAST_LEN + jnp.arange(tl.ts, dtype=i32)
    pos = jnp.concatenate([pos_p, jnp.tile(pos_s, tl.bs)])
    ang = pos.astype(f32)[:, None] * inv[None, :]
    cos, sin = jnp.cos(ang), jnp.sin(ang)
    return jnp.concatenate([cos, cos], axis=-1), jnp.concatenate([-sin, sin], axis=-1)


def _moe_layout(counts, n_blocks):
    padded = ((counts + BM - 1) // BM) * BM
    pad_end = jnp.cumsum(padded)
    pad_start = pad_end - padded
    block_e = jnp.minimum(jnp.searchsorted(pad_end, jnp.arange(n_blocks, dtype=i32) * BM, side="right"),
                          N_EXPERTS - 1).astype(i32)
    return pad_start.astype(i32), block_e


def kernel(x_prompt, x_sample, state_ret, state_conv, c_prompt, c_sample, w_mod, b_mod, g_mix_norm, w_in,
           w_conv, b_conv, g_conv_ln, b_conv_ln, g_ret_gn, w_out, g_ffn_norm, w_router, b_router,
           w_exp_gate, w_exp_up, w_exp_down, g_final):
    bp, tp, _ = x_prompt.shape
    bs, ts, _ = x_sample.shape
    depth = w_mod.shape[0]
    tl = _Tiles(bp, tp, bs, ts)

    c_all = jnp.concatenate([c_prompt, jnp.repeat(c_sample, ts, axis=0)], axis=0)
    mod = _modulation(c_all, w_mod, b_mod)
    mod_seq = mod[:, :bp].reshape(depth, bp, 1, N_MOD * D_MODEL)
    mod_tok = mod[:, bp:]

    cos_tab, sin_tab = _rope_tables(tl)
    w_in_bf = w_in.astype(bf16)
    w_out_bf = w_out.astype(bf16)
    wr_pad = jnp.pad(w_router.astype(f32), ((0, 0), (0, LANES - N_EXPERTS)))
    br_pad = jnp.pad(b_router.astype(f32), (0, LANES - N_EXPERTS)).reshape(1, LANES)
    vec3 = lambda t: t.reshape(depth, 1, t.shape[-1])
    g_mix3, g_ffn3, gn3 = vec3(g_mix_norm), vec3(g_ffn_norm), vec3(g_ret_gn)
    b_conv3, g_ln3, b_ln3 = vec3(b_conv), vec3(g_conv_ln), vec3(b_conv_ln)

    n_assign = 2 * tl.n_tok
    n_blocks = -(-n_assign // BM) + N_EXPERTS
    cap = n_blocks * BM

    x = (x_prompt.reshape(tl.n_prompt, D_MODEL), x_sample.reshape(tl.n_sample, D_MODEL))
    ret_p, conv_p, ret_s, conv_s = [], [], [], []
    for layer in range(depth):
        q, k, v, gate, a = _inproj(tl, layer, x, mod_seq, mod_tok, g_mix3, w_in_bf, cos_tab, sin_tab)
        ro_p, s_p = _retention_prompt(tl, layer, q, k, v, gate, gn3)
        ro_s, s_s = _retention_sample(tl, layer, q, k, v, gate, state_ret, gn3)
        co_p, buf_p = _conv_prompt(tl, layer, a, w_conv, b_conv3, g_ln3, b_ln3)
        co_s, buf_s = _conv_sample(tl, layer, a, state_conv, w_conv, b_conv3, g_ln3, b_ln3)
        x_mid, h2, route_i, route_w, counts = _outproj(
            tl, layer, x, ro_p, ro_s, co_p, co_s, mod_seq, mod_tok, g_ffn3, w_out_bf, wr_pad, br_pad)
        pad_start, block_e = _moe_layout(counts[0, :N_EXPERTS].astype(i32), n_blocks)
        dest0 = pad_start[route_i[:, 0]] + route_i[:, 2]
        dest1 = pad_start[route_i[:, 1]] + route_i[:, 3]
        slot_tok = _plan(dest0, dest1, cap)
        ys = _experts(layer, block_e, slot_tok, h2, w_exp_gate, w_exp_up, w_exp_down)
        last = layer == depth - 1
        x = _combine(tl, layer, dest0, dest1, ys, x_mid, route_w, mod_seq, mod_tok,
                     g_final.reshape(1, D_MODEL) if last else None)
        ret_p.append(s_p)
        ret_s.append(s_s)
        conv_p.append(buf_p)
        conv_s.append(buf_s)
    y_p, y_s = x
    return (y_p.reshape(bp, tp, D_MODEL), y_s.reshape(bs, ts, D_MODEL),
            jnp.stack(ret_p), jnp.stack(conv_p), jnp.stack(ret_s), jnp.stack(conv_s))
```

```python
import functools

import jax
import jax.numpy as jnp
from jax import lax
from jax.experimental import pallas as pl
from jax.experimental.pallas import tpu as pltpu

f32 = jnp.float32
bf16 = jnp.bfloat16
i32 = jnp.int32

D_MODEL = 1024
D_RET = 512
D_CONV = 512
N_HEADS = 4
HEAD_D = 128
RET_CHUNK = 128
ROPE_BASE = 10000.0
CONV_WIDTH = 31
CONV_HALO = CONV_WIDTH - 1
N_EXPERTS = 16
N_GROUPS = 4
GROUP_SIZE = N_EXPERTS // N_GROUPS
N_MOD = 6
EPS = 1e-6
PAST_LEN = 16384
D_IN = 4 * D_RET + 2 * D_CONV

LANES = 128
SUBLANES = 8
TM = 256
BM = 256
CONV_ROWS = 64
SAMPLE_GROUP = 8
HALO_PAD = 32
VMEM_LIMIT = 56 * 1024 * 1024


def _cparams(n_axes, vmem=VMEM_LIMIT):
    return pltpu.CompilerParams(dimension_semantics=("arbitrary",) * n_axes, vmem_limit_bytes=vmem)


def _mod_kernel(c_ref, w_ref, b_ref, o_ref):
    cond = jax.nn.silu(c_ref[...]).astype(bf16)
    o_ref[...] = jnp.dot(cond, w_ref[...].astype(bf16), preferred_element_type=f32) + b_ref[...]


def _modulation(c_all, w_mod, b_mod):
    depth = w_mod.shape[0]
    m = c_all.shape[0]
    return pl.pallas_call(
        _mod_kernel,
        grid=(depth, N_MOD),
        in_specs=[
            pl.BlockSpec((m, D_MODEL), lambda l, j: (0, 0)),
            pl.BlockSpec((None, D_MODEL, D_MODEL), lambda l, j: (l, 0, j)),
            pl.BlockSpec((None, 1, D_MODEL), lambda l, j: (l, 0, j)),
        ],
        out_specs=pl.BlockSpec((None, m, D_MODEL), lambda l, j: (l, 0, j)),
        out_shape=jax.ShapeDtypeStruct((depth, m, N_MOD * D_MODEL), f32),
        compiler_params=_cparams(2),
        name="modulation",
    )(c_all, w_mod, b_mod.reshape(depth, 1, N_MOD * D_MODEL))


class _Tiles:
    def __init__(self, bp, tp, bs, ts):
        self.bp, self.tp, self.bs, self.ts = bp, tp, bs, ts
        self.n_prompt = bp * tp
        self.n_sample = bs * ts
        self.n_tok = self.n_prompt + self.n_sample
        assert tp % TM == 0 and self.n_sample % TM == 0
        self.tiles_per_seq = tp // TM
        self.np_tiles = self.n_prompt // TM
        self.ns_tiles = self.n_sample // TM
        self.n_tiles = self.np_tiles + self.ns_tiles

    def prompt_block(self, i):
        return jnp.minimum(i, self.np_tiles - 1)

    def sample_block(self, i):
        return jnp.maximum(i - self.np_tiles, 0)

    def seq_index(self, i):
        return jnp.minimum(i // self.tiles_per_seq, self.bp - 1)


def _mod_specs(tl, layer, col):
    seq = pl.BlockSpec((None, None, 1, D_MODEL), lambda i: (layer, tl.seq_index(i), 0, col))
    tok = pl.BlockSpec((None, TM, D_MODEL), lambda i: (layer, tl.sample_block(i), col))
    return seq, tok


def _pick(is_sample, tok_ref, seq_ref):
    return jnp.where(is_sample, tok_ref[...], seq_ref[...])


def _rms(x, g):
    return x * lax.rsqrt(jnp.mean(x * x, axis=-1, keepdims=True) + EPS) * g


def _inproj_kernel(*refs, np_tiles, split_x):
    if split_x:
        xp_ref, xs_ref = refs[:2]
        refs = refs[2:]
    else:
        x_ref = refs[0]
        refs = refs[1:]
    (shs_ref, sht_ref, scs_ref, sct_ref, g_ref, w_ref, cos_ref, sin_ref,
     q_ref, k_ref, v_ref, gate_ref, a_ref) = refs
    is_s = pl.program_id(0) >= np_tiles
    x = jnp.where(is_s, xs_ref[...], xp_ref[...]) if split_x else x_ref[...]
    h = _rms(x, g_ref[...]) * (1.0 + _pick(is_s, sct_ref, scs_ref)) + _pick(is_s, sht_ref, shs_ref)
    hb = h.astype(bf16)
    cos = cos_ref[...]
    sin = sin_ref[...]

    def proj(col):
        return jnp.dot(hb, w_ref[:, col * D_RET:(col + 1) * D_RET], preferred_element_type=f32)

    def rope(t):
        outs = []
        for hd in range(N_HEADS):
            th = t[:, hd * HEAD_D:(hd + 1) * HEAD_D]
            outs.append(th * cos + pltpu.roll(th, HEAD_D // 2, 1) * sin)
        return outs

    for hd, qh in enumerate(rope(proj(0))):
        q_ref[:, hd * HEAD_D:(hd + 1) * HEAD_D] = qh
    for hd, kh in enumerate(rope(proj(1))):
        k_ref[:, hd * HEAD_D:(hd + 1) * HEAD_D] = kh * (HEAD_D ** -0.5)
    v_ref[...] = proj(2)
    gate_ref[...] = proj(3)
    a_ref[...] = proj(4) * jax.nn.sigmoid(proj(5))


def _inproj(tl, layer, x, mod_seq, mod_tok, g_norm, w_in_bf, cos_tab, sin_tab):
    split_x = isinstance(x, tuple)
    tok_spec = pl.BlockSpec((TM, D_MODEL), lambda i: (i, 0))
    if split_x:
        x_args = list(x)
        x_specs = [pl.BlockSpec((TM, D_MODEL), lambda i: (tl.prompt_block(i), 0)),
                   pl.BlockSpec((TM, D_MODEL), lambda i: (tl.sample_block(i), 0))]
    else:
        x_args, x_specs = [x], [tok_spec]
    sh_seq, sh_tok = _mod_specs(tl, layer, 0)
    sc_seq, sc_tok = _mod_specs(tl, layer, 1)

    def table_block(i):
        return jnp.where(i < tl.np_tiles, i % tl.tiles_per_seq, tl.tiles_per_seq + tl.sample_block(i))

    tab_spec = pl.BlockSpec((TM, HEAD_D), lambda i: (table_block(i), 0))
    out_spec = pl.BlockSpec((TM, D_RET), lambda i: (i, 0))
    out_sd = jax.ShapeDtypeStruct((tl.n_tok, D_RET), f32)
    return pl.pallas_call(
        functools.partial(_inproj_kernel, np_tiles=tl.np_tiles, split_x=split_x),
        grid=(tl.n_tiles,),
        in_specs=x_specs + [
            sh_seq, sh_tok, sc_seq, sc_tok,
            pl.BlockSpec((None, 1, D_MODEL), lambda i: (layer, 0, 0)),
            pl.BlockSpec((None, D_MODEL, D_IN), lambda i: (layer, 0, 0)),
            tab_spec, tab_spec,
        ],
        out_specs=[out_spec] * 5,
        out_shape=[out_sd] * 5,
        compiler_params=_cparams(1),
        name="inproj",
    )(*x_args, mod_seq, mod_tok, mod_seq, mod_tok, g_norm, w_in_bf, cos_tab, sin_tab)


def _head_norm_gate(o, gn, gate):
    mu = jnp.mean(o, axis=-1, keepdims=True)
    var = jnp.mean(jnp.square(o - mu), axis=-1, keepdims=True)
    return jax.nn.silu(gate) * ((o - mu) * lax.rsqrt(var + EPS) * gn)


def _dot_nt(a, b):
    return lax.dot_general(a, b, (((1,), (1,)), ((), ())), preferred_element_type=f32)


def _dot_tn(a, b):
    return lax.dot_general(a, b, (((0,), (0,)), ((), ())), preferred_element_type=f32)


def _ret_prompt_kernel(q_ref, k_ref, v_ref, gate_ref, dec_ref, qd_ref, kd_ref, cd_ref, gn_ref,
                       o_ref, s_out_ref, s_ref):
    c = pl.program_id(1)

    @pl.when(c == 0)
    def _():
        s_ref[...] = jnp.zeros_like(s_ref)

    for hd in range(N_HEADS):
        sl = slice(hd * HEAD_D, (hd + 1) * HEAD_D)
        kh = k_ref[:, sl]
        qb = q_ref[:, sl].astype(bf16)
        kb = kh.astype(bf16)
        vb = v_ref[:, sl].astype(bf16)
        s_old = s_ref[hd]
        scores = _dot_nt(qb, kb) * dec_ref[hd]
        inner = jnp.dot(scores.astype(bf16), vb, preferred_element_type=f32)
        cross = jnp.dot(qb, s_old.astype(bf16), preferred_element_type=f32) * qd_ref[hd]
        s_ref[hd] = s_old * cd_ref[hd] + _dot_tn((kh * kd_ref[hd]).astype(bf16), vb)
        o_ref[:, sl] = _head_norm_gate(inner + cross, gn_ref[:, sl], gate_ref[:, sl])

    @pl.when(c == pl.num_programs(1) - 1)
    def _():
        s_out_ref[...] = s_ref[...]


def _decay_tables(chunk, true_len):
    lg = jnp.log(1.0 - 2.0 ** (-5.0 - jnp.arange(N_HEADS, dtype=f32)))
    idx = jnp.arange(chunk, dtype=f32)
    rel = idx[:, None] - idx[None, :]
    decay = jnp.where(rel[None] >= 0, jnp.exp(jnp.maximum(rel, 0.0)[None] * lg[:, None, None]), 0.0)
    q_decay = jnp.exp((idx[None, :] + 1.0) * lg[:, None])
    k_decay = jnp.exp((true_len - 1.0 - idx[None, :]) * lg[:, None])
    c_decay = jnp.exp(true_len * lg)
    return decay, q_decay, k_decay, c_decay


def _retention_prompt(tl, layer, q, k, v, gate, g_ret_gn):
    n_chunks = tl.tp // RET_CHUNK
    decay, q_decay, k_decay, c_decay = _decay_tables(RET_CHUNK, RET_CHUNK)
    bcast = lambda t: jnp.broadcast_to(t[:, :, None], (N_HEADS, RET_CHUNK, HEAD_D))
    cd = jnp.broadcast_to(c_decay[:, None, None], (N_HEADS, 1, HEAD_D))
    tok_spec = pl.BlockSpec((RET_CHUNK, D_RET), lambda b, c: (b * n_chunks + c, 0))
    tab_spec = pl.BlockSpec((N_HEADS, RET_CHUNK, HEAD_D), lambda b, c: (0, 0, 0))
    return pl.pallas_call(
        _ret_prompt_kernel,
        grid=(tl.bp, n_chunks),
        in_specs=[tok_spec] * 4 + [tab_spec] * 3 + [
            pl.BlockSpec((N_HEADS, 1, HEAD_D), lambda b, c: (0, 0, 0)),
            pl.BlockSpec((None, 1, D_RET), lambda b, c: (layer, 0, 0)),
        ],
        out_specs=[tok_spec, pl.BlockSpec((None, N_HEADS, HEAD_D, HEAD_D), lambda b, c: (b, 0, 0, 0))],
        out_shape=[jax.ShapeDtypeStruct((tl.n_prompt, D_RET), f32),
                   jax.ShapeDtypeStruct((tl.bp, N_HEADS, HEAD_D, HEAD_D), f32)],
        scratch_shapes=[pltpu.VMEM((N_HEADS, HEAD_D, HEAD_D), f32)],
        compiler_params=_cparams(2),
        name="retention_prompt",
    )(q, k, v, gate, decay, bcast(q_decay), bcast(k_decay), cd, g_ret_gn)


def _ret_sample_kernel(q_ref, k_ref, v_ref, gate_ref, s_in_ref, dec_ref, qd_ref, kd_ref, cd_ref, gn_ref,
                       o_ref, s_out_ref, *, ts):
    seqs_per_tile = SUBLANES // ts
    row = lax.broadcasted_iota(i32, (SUBLANES, HEAD_D), 0)
    for t in range(SAMPLE_GROUP // seqs_per_tile):
        rows = slice(t * SUBLANES, (t + 1) * SUBLANES)
        for hd in range(N_HEADS):
            sl = slice(hd * HEAD_D, (hd + 1) * HEAD_D)
            qh = q_ref[rows, sl]
            kh = k_ref[rows, sl] * kd_ref[hd]
            vb = v_ref[rows, sl].astype(bf16)
            qb = qh.astype(bf16)
            scores = _dot_nt(qb, k_ref[rows, sl].astype(bf16)) * dec_ref[hd]
            o = jnp.dot(scores.astype(bf16), vb, preferred_element_type=f32)
            for s in range(seqs_per_tile):
                b = t * seqs_per_tile + s
                mine = (row >= s * ts) & (row < (s + 1) * ts)
                s_old = s_in_ref[b, hd]
                q_s = jnp.where(mine, qh, 0.0).astype(bf16)
                k_s = jnp.where(mine, kh, 0.0).astype(bf16)
                o = o + jnp.dot(q_s, s_old.astype(bf16), preferred_element_type=f32) * qd_ref[hd]
                s_out_ref[b, hd] = s_old * cd_ref[hd] + _dot_tn(k_s, vb)
            o_ref[rows, sl] = _head_norm_gate(o, gn_ref[:, sl], gate_ref[rows, sl])


def _retention_sample(tl, layer, q, k, v, gate, state_ret, g_ret_gn):
    ts = tl.ts
    assert SUBLANES % ts == 0 and tl.bs % SAMPLE_GROUP == 0
    seqs_per_tile = SUBLANES // ts
    decay, q_decay, k_decay, c_decay = _decay_tables(ts, ts)
    eye = jnp.eye(seqs_per_tile, dtype=f32)
    dec_tile = jnp.einsum("ab,hij->haibj", eye, decay).reshape(N_HEADS, SUBLANES, SUBLANES)
    tile_rows = lambda t: jnp.broadcast_to(jnp.tile(t, (1, seqs_per_tile))[:, :, None],
                                           (N_HEADS, SUBLANES, HEAD_D))
    cd = jnp.broadcast_to(c_decay[:, None, None], (N_HEADS, 1, HEAD_D))
    rows = SAMPLE_GROUP * ts
    first = tl.n_prompt // rows
    tok_spec = pl.BlockSpec((rows, D_RET), lambda i: (first + i, 0))
    const3 = lambda shape: pl.BlockSpec(shape, lambda i: (0, 0, 0))
    st_block = (SAMPLE_GROUP, N_HEADS, HEAD_D, HEAD_D)
    return pl.pallas_call(
        functools.partial(_ret_sample_kernel, ts=ts),
        grid=(tl.bs // SAMPLE_GROUP,),
        in_specs=[tok_spec] * 4 + [
            pl.BlockSpec((None,) + st_block, lambda i: (layer, i, 0, 0, 0)),
            const3((N_HEADS, SUBLANES, SUBLANES)),
            const3((N_HEADS, SUBLANES, HEAD_D)),
            const3((N_HEADS, SUBLANES, HEAD_D)),
            const3((N_HEADS, 1, HEAD_D)),
            pl.BlockSpec((None, 1, D_RET), lambda i: (layer, 0, 0)),
        ],
        out_specs=[pl.BlockSpec((rows, D_RET), lambda i: (i, 0)),
                   pl.BlockSpec(st_block, lambda i: (i, 0, 0, 0))],
        out_shape=[jax.ShapeDtypeStruct((tl.n_sample, D_RET), f32),
                   jax.ShapeDtypeStruct((tl.bs, N_HEADS, HEAD_D, HEAD_D), f32)],
        compiler_params=_cparams(1),
        name="retention_sample",
    )(q, k, v, gate, state_ret, dec_tile, tile_rows(q_decay), tile_rows(k_decay), cd, g_ret_gn)


def _ln_silu(cv, g, b):
    mu = jnp.mean(cv, axis=-1, keepdims=True)
    var = jnp.mean(jnp.square(cv - mu), axis=-1, keepdims=True)
    return jax.nn.silu((cv - mu) * lax.rsqrt(var + EPS) * g + b)


def _conv_prompt_kernel(a_ref, w_ref, b_ref, g_ref, bl_ref, o_ref, buf_ref, full_ref):
    j = pl.program_id(1)

    @pl.when(j == 0)
    def _():
        full_ref[0:HALO_PAD, :] = jnp.zeros((HALO_PAD, D_CONV), f32)

    @pl.when(j > 0)
    def _():
        full_ref[0:HALO_PAD, :] = full_ref[TM:TM + HALO_PAD, :]

    full_ref[HALO_PAD:HALO_PAD + TM, :] = a_ref[...]
    shift = HALO_PAD - CONV_HALO
    for r0 in range(0, TM, CONV_ROWS):
        acc = jnp.broadcast_to(b_ref[...], (CONV_ROWS, D_CONV))
        for tap in range(CONV_WIDTH):
            acc = acc + full_ref[r0 + tap + shift:r0 + tap + shift + CONV_ROWS, :] * w_ref[tap:tap + 1, :]
        o_ref[r0:r0 + CONV_ROWS, :] = _ln_silu(acc, g_ref[...], bl_ref[...])

    @pl.when(j == pl.num_programs(1) - 1)
    def _():
        buf_ref[...] = a_ref[TM - CONV_HALO:TM, :]


def _conv_prompt(tl, layer, a, w_conv, b_conv, g_ln, b_ln):
    tps = tl.tiles_per_seq
    vec = pl.BlockSpec((None, 1, D_CONV), lambda b, j: (layer, 0, 0))
    return pl.pallas_call(
        _conv_prompt_kernel,
        grid=(tl.bp, tps),
        in_specs=[pl.BlockSpec((TM, D_CONV), lambda b, j: (b * tps + j, 0)),
                  pl.BlockSpec((None, CONV_WIDTH, D_CONV), lambda b, j: (layer, 0, 0)),
                  vec, vec, vec],
        out_specs=[pl.BlockSpec((TM, D_CONV), lambda b, j: (b * tps + j, 0)),
                   pl.BlockSpec((None, CONV_HALO, D_CONV), lambda b, j: (b, 0, 0))],
        out_shape=[jax.ShapeDtypeStruct((tl.n_prompt, D_CONV), f32),
                   jax.ShapeDtypeStruct((tl.bp, CONV_HALO, D_CONV), f32)],
        scratch_shapes=[pltpu.VMEM((HALO_PAD + TM, D_CONV), f32)],
        compiler_params=_cparams(2),
        name="conv_prompt",
    )(a, w_conv, b_conv, g_ln, b_ln)


def _conv_sample_kernel(a_ref, st_ref, w_ref, b_ref, g_ref, bl_ref, o_ref, buf_ref, full_ref, cv_ref, *, ts):
    for s in range(SAMPLE_GROUP):
        full_ref[0:CONV_HALO, :] = st_ref[s]
        full_ref[CONV_HALO:CONV_HALO + ts, :] = a_ref[s * ts:(s + 1) * ts, :]
        acc = jnp.broadcast_to(b_ref[...], (ts, D_CONV))
        for tap in range(CONV_WIDTH):
            acc = acc + full_ref[tap:tap + ts, :] * w_ref[tap:tap + 1, :]
        cv_ref[s * ts:(s + 1) * ts, :] = acc
        buf_ref[s] = full_ref[ts:ts + CONV_HALO, :]
    o_ref[...] = _ln_silu(cv_ref[...], g_ref[...], bl_ref[...])


def _conv_sample(tl, layer, a, state_conv, w_conv, b_conv, g_ln, b_ln):
    ts = tl.ts
    rows = SAMPLE_GROUP * ts
    first = tl.n_prompt // rows
    vec = pl.BlockSpec((None, 1, D_CONV), lambda i: (layer, 0, 0))
    return pl.pallas_call(
        functools.partial(_conv_sample_kernel, ts=ts),
        grid=(tl.bs // SAMPLE_GROUP,),
        in_specs=[pl.BlockSpec((rows, D_CONV), lambda i: (first + i, 0)),
                  pl.BlockSpec((None, SAMPLE_GROUP, CONV_HALO, D_CONV), lambda i: (layer, i, 0, 0)),
                  pl.BlockSpec((None, CONV_WIDTH, D_CONV), lambda i: (layer, 0, 0)),
                  vec, vec, vec],
        out_specs=[pl.BlockSpec((rows, D_CONV), lambda i: (i, 0)),
                   pl.BlockSpec((SAMPLE_GROUP, CONV_HALO, D_CONV), lambda i: (i, 0, 0))],
        out_shape=[jax.ShapeDtypeStruct((tl.n_sample, D_CONV), f32),
                   jax.ShapeDtypeStruct((tl.bs, CONV_HALO, D_CONV), f32)],
        scratch_shapes=[pltpu.VMEM((CONV_HALO + ts + SUBLANES, D_CONV), f32),
                        pltpu.VMEM((rows, D_CONV), f32)],
        compiler_params=_cparams(1),
        name="conv_sample",
    )(a, state_conv, w_conv, b_conv, g_ln, b_ln)


SORT_ROWS = 2 * TM + N_EXPERTS * SUBLANES
XS_W = D_MODEL + LANES


def _split3(x):
    a = x.astype(bf16)
    r = x - a.astype(f32)
    b = r.astype(bf16)
    c = (r - b.astype(f32)).astype(bf16)
    return a, b, c


def _first_of4(vals, m):
    return jnp.where(vals[0] == m, 0.0, jnp.where(vals[1] == m, 1.0, jnp.where(vals[2] == m, 2.0, 3.0)))


def _rows_to_tile(rows, n_rows):
    sub = lax.broadcasted_iota(i32, (n_rows, TM), 0)
    out = jnp.zeros((n_rows, TM), f32)
    for r, val in enumerate(rows):
        out = jnp.where(sub == r, val, out)
    return out


def _outproj_kernel(*refs, np_tiles, split_x):
    if split_x:
        xp_ref, xs_ref = refs[:2]
        refs = refs[2:]
    else:
        x_ref = refs[0]
        refs = refs[1:]
    (retp_ref, rets_ref, cvp_ref, cvs_ref, gts_ref, gtt_ref, shs_ref, sht_ref, scs_ref, sct_ref,
     g_ref, wo_ref, wrh_ref, wrl_ref, br_ref,
     xo_ref, h2_ref, rows_ref, cols_ref, cnt_ref) = refs
    is_s = pl.program_id(0) >= np_tiles

    x = jnp.where(is_s, xs_ref[...], xp_ref[...]) if split_x else x_ref[...]
    ret = jnp.where(is_s, rets_ref[...], retp_ref[...]).astype(bf16)
    cv = jnp.where(is_s, cvs_ref[...], cvp_ref[...]).astype(bf16)
    mix_out = (jnp.dot(ret, wo_ref[0:D_RET, :], preferred_element_type=f32)
               + jnp.dot(cv, wo_ref[D_RET:D_RET + D_CONV, :], preferred_element_type=f32))
    xn = x + _pick(is_s, gtt_ref, gts_ref) * mix_out
    xo_ref[...] = xn
    h2 = _rms(xn, g_ref[...]) * (1.0 + _pick(is_s, sct_ref, scs_ref)) + _pick(is_s, sht_ref, shs_ref)
    h_hi = h2.astype(bf16)
    h2_ref[...] = h_hi

    h_lo = (h2 - h_hi.astype(f32)).astype(bf16)
    logits = (jnp.dot(h_hi, wrh_ref[...], preferred_element_type=f32)
              + jnp.dot(h_hi, wrl_ref[...], preferred_element_type=f32)
              + jnp.dot(h_lo, wrh_ref[...], preferred_element_type=f32)) + br_ref[...]
    lt = logits.T
    row = [lt[e:e + 1, :] for e in range(N_EXPERTS)]
    top = functools.reduce(jnp.maximum, row)
    ex = [jnp.exp(r - top) for r in row]
    den = functools.reduce(jnp.add, ex)
    p = [v / den for v in ex]

    best = None
    for g in range(N_GROUPS):
        a = p[g * GROUP_SIZE:(g + 1) * GROUP_SIZE]
        m1 = functools.reduce(jnp.maximum, a)
        i1 = _first_of4(a, m1)
        b = [jnp.where(i1 == float(j), -1.0, a[j]) for j in range(GROUP_SIZE)]
        m2 = functools.reduce(jnp.maximum, b)
        i2 = _first_of4(b, m2)
        cand = (m1 + m2, m1, m2, i1 + float(g * GROUP_SIZE), i2 + float(g * GROUP_SIZE))
        if best is None:
            best = cand
        else:
            take = cand[0] > best[0]
            best = tuple(jnp.where(take, c, o) for c, o in zip(cand, best))
    _, m1, m2, e0, e1 = best
    denom = m1 + m2
    w0 = m1 / denom
    w1 = m2 / denom

    ex_id = lax.broadcasted_iota(i32, (N_EXPERTS, TM), 0).astype(f32)
    sel0 = ex_id == e0
    sel1 = ex_id == e1
    ind = jnp.where(sel0 | sel1, 1.0, 0.0)
    t_r = lax.broadcasted_iota(i32, (TM, TM), 0)
    t_c = lax.broadcasted_iota(i32, (TM, TM), 1)
    earlier = jnp.where(t_r < t_c, 1.0, 0.0).astype(bf16)
    prefix = jnp.dot(ind.astype(bf16), earlier, preferred_element_type=f32)
    cnt = jnp.sum(ind, axis=-1, keepdims=True)
    cnt8 = jnp.floor((cnt + float(SUBLANES - 1)) * (1.0 / SUBLANES)) * float(SUBLANES)
    e_r = lax.broadcasted_iota(i32, (N_EXPERTS, N_EXPERTS), 0)
    e_c = lax.broadcasted_iota(i32, (N_EXPERTS, N_EXPERTS), 1)
    below = jnp.where(e_c < e_r, 1.0, 0.0).astype(bf16)
    seg_off = jnp.dot(below, jnp.broadcast_to(cnt8, (N_EXPERTS, TM)).astype(bf16),
                      preferred_element_type=f32)
    where_to = seg_off + prefix
    pos0 = jnp.sum(jnp.where(sel0, where_to, 0.0), axis=0, keepdims=True)
    pos1 = jnp.sum(jnp.where(sel1, where_to, 0.0), axis=0, keepdims=True)

    w0p = [v.astype(f32) for v in _split3(w0)]
    w1p = [v.astype(f32) for v in _split3(w1)]
    info = [pos0, pos1] + w0p + w1p
    rows_ref[...] = _rows_to_tile(info, SUBLANES)
    cols_ref[...] = _rows_to_tile(info, LANES).T
    cnt_ref[...] = jnp.broadcast_to(cnt, (N_EXPERTS, LANES))


def _outproj(tl, layer, x, ret_p, ret_s, cv_p, cv_s, mod_seq, mod_tok, g_norm, w_out_bf, wr_hi, wr_lo, br_pad):
    split_x = isinstance(x, tuple)
    tok_spec = pl.BlockSpec((TM, D_MODEL), lambda i: (i, 0))
    p_spec = lambda w: pl.BlockSpec((TM, w), lambda i: (tl.prompt_block(i), 0))
    s_spec = lambda w: pl.BlockSpec((TM, w), lambda i: (tl.sample_block(i), 0))
    if split_x:
        x_args, x_specs = list(x), [p_spec(D_MODEL), s_spec(D_MODEL)]
    else:
        x_args, x_specs = [x], [tok_spec]
    mods = []
    for col in (2, 3, 4):
        mods += list(_mod_specs(tl, layer, col))
    wr_spec = pl.BlockSpec((D_MODEL, LANES), lambda i: (0, 0))
    return pl.pallas_call(
        functools.partial(_outproj_kernel, np_tiles=tl.np_tiles, split_x=split_x),
        grid=(tl.n_tiles,),
        in_specs=x_specs + [p_spec(D_RET), s_spec(D_RET), p_spec(D_CONV), s_spec(D_CONV)] + mods + [
            pl.BlockSpec((None, 1, D_MODEL), lambda i: (layer, 0, 0)),
            pl.BlockSpec((None, D_MODEL, D_MODEL), lambda i: (layer, 0, 0)),
            wr_spec, wr_spec,
            pl.BlockSpec((1, LANES), lambda i: (0, 0)),
        ],
        out_specs=[tok_spec, tok_spec,
                   pl.BlockSpec((None, SUBLANES, TM), lambda i: (i, 0, 0)),
                   pl.BlockSpec((TM, LANES), lambda i: (i, 0)),
                   pl.BlockSpec((None, N_EXPERTS, LANES), lambda i: (i, 0, 0))],
        out_shape=[jax.ShapeDtypeStruct((tl.n_tok, D_MODEL), f32),
                   jax.ShapeDtypeStruct((tl.n_tok, D_MODEL), bf16),
                   jax.ShapeDtypeStruct((tl.n_tiles, SUBLANES, TM), f32),
                   jax.ShapeDtypeStruct((tl.n_tok, LANES), f32),
                   jax.ShapeDtypeStruct((tl.n_tiles, N_EXPERTS, LANES), f32)],
        compiler_params=_cparams(1),
        name="outproj_router",
    )(*x_args, ret_p, ret_s, cv_p, cv_s, *([mod_seq, mod_tok] * 3), g_norm, w_out_bf, wr_hi, wr_lo, br_pad)


class _Layout:
    def __init__(self, n_tiles):
        self.n_tiles = n_tiles
        seg = n_tiles * N_EXPERTS
        self.off8, self.dst, self.n8 = 0, seg, 2 * seg
        self.tail_start, self.tail_n8 = 3 * seg, 3 * seg + N_EXPERTS
        worst = 2 * n_tiles * TM + seg * (SUBLANES - 1) + N_EXPERTS * (BM - SUBLANES)
        self.n_blocks = -(-worst // BM)
        self.cap = self.n_blocks * BM


def _moe_tables(lay, tile_counts):
    c8 = ((tile_counts.astype(i32) + SUBLANES - 1) // SUBLANES) * SUBLANES
    base8 = jnp.cumsum(c8, axis=0) - c8
    tot8 = jnp.sum(c8, axis=0)
    region = ((tot8 + BM - 1) // BM) * BM
    g_end = jnp.cumsum(region)
    g_start = g_end - region
    off8 = jnp.cumsum(c8, axis=1) - c8
    n_used = g_end[-1] // BM
    blk = jnp.arange(lay.n_blocks, dtype=i32)
    block_e = jnp.minimum(jnp.sum((g_end[None, :] <= blk[:, None] * BM).astype(i32), axis=1), N_EXPERTS - 1)
    block_e = jnp.where(blk < n_used, block_e, block_e[n_used - 1])
    tab = jnp.concatenate([off8.ravel(), (g_start[None, :] + base8).ravel(), (c8 // SUBLANES).ravel(),
                           g_start + tot8, (region - tot8) // SUBLANES]).astype(i32)
    return tab, jnp.concatenate([block_e, n_used[None]]).astype(i32)


def _for_chunks(n, fn):
    def body(c, carry):
        fn(c)
        return carry

    lax.fori_loop(0, n, body, 0)


def _dispatch_kernel(tab_ref, h2_ref, rows_ref, cols_ref, xs_hbm, sorted_ref, zero_ref, sem, *, lay):
    i = pl.program_id(0)
    n_tiles = lay.n_tiles
    slot = i % 2

    def seg_copy(slot_, off, dst):
        return pltpu.make_async_copy(sorted_ref.at[slot_, pl.ds(off, SUBLANES)],
                                     xs_hbm.at[pl.ds(dst, SUBLANES)], sem.at[slot_])

    def tail_copy(dst):
        return pltpu.make_async_copy(zero_ref, xs_hbm.at[pl.ds(dst, SUBLANES)], sem.at[2])

    def wait_tile(tile, slot_):
        for e in range(N_EXPERTS):
            _for_chunks(tab_ref[lay.n8 + tile * N_EXPERTS + e], lambda c: seg_copy(slot_, 0, 0).wait())

    @pl.when(i == 0)
    def _():
        zero_ref[...] = jnp.zeros_like(zero_ref)
        for e in range(N_EXPERTS):
            start = tab_ref[lay.tail_start + e]
            _for_chunks(tab_ref[lay.tail_n8 + e],
                        lambda c: tail_copy(pl.multiple_of(start + c * SUBLANES, SUBLANES)).start())

    @pl.when(i >= 2)
    def _():
        wait_tile(i - 2, slot)

    pos0 = rows_ref[0:1, :]
    pos1 = rows_ref[1:2, :]
    r_id = lax.broadcasted_iota(i32, (SORT_ROWS, TM), 0).astype(f32)
    p0 = r_id == pos0
    p1 = r_id == pos1
    perm = jnp.where(p0 | p1, 1.0, 0.0).astype(bf16)
    lane = lax.broadcasted_iota(i32, (TM, LANES), 1)
    cols = cols_ref[...]
    wpart0 = jnp.where((lane >= 2) & (lane < 5), cols, 0.0).astype(bf16)
    wpart1 = jnp.where((lane >= 5) & (lane < 8), cols, 0.0).astype(bf16)
    sw = (jnp.dot(jnp.where(p0, 1.0, 0.0).astype(bf16), wpart0, preferred_element_type=f32)
          + jnp.dot(jnp.where(p1, 1.0, 0.0).astype(bf16), wpart1, preferred_element_type=f32))
    sorted_ref[slot, :, 0:D_MODEL] = jnp.dot(perm, h2_ref[...], preferred_element_type=f32)
    sorted_ref[slot, :, D_MODEL:XS_W] = jnp.broadcast_to(jnp.sum(sw, axis=-1, keepdims=True), (SORT_ROWS, LANES))

    for e in range(N_EXPERTS):
        off = tab_ref[lay.off8 + i * N_EXPERTS + e]
        dst = tab_ref[lay.dst + i * N_EXPERTS + e]
        _for_chunks(tab_ref[lay.n8 + i * N_EXPERTS + e],
                    lambda c: seg_copy(slot, pl.multiple_of(off + c * SUBLANES, SUBLANES),
                                       pl.multiple_of(dst + c * SUBLANES, SUBLANES)).start())

    @pl.when(i == n_tiles - 1)
    def _():
        if n_tiles >= 2:
            wait_tile(i - 1, 1 - slot)
        wait_tile(i, slot)
        for e in range(N_EXPERTS):
            _for_chunks(tab_ref[lay.tail_n8 + e], lambda c: tail_copy(0).wait())


def _dispatch(tl, lay, tab, h2, rows, cols):
    grid_spec = pltpu.PrefetchScalarGridSpec(
        num_scalar_prefetch=1,
        grid=(tl.n_tiles,),
        in_specs=[pl.BlockSpec((TM, D_MODEL), lambda i, t: (i, 0)),
                  pl.BlockSpec((None, SUBLANES, TM), lambda i, t: (i, 0, 0)),
                  pl.BlockSpec((TM, LANES), lambda i, t: (i, 0))],
        out_specs=pl.BlockSpec(memory_space=pl.ANY),
        scratch_shapes=[pltpu.VMEM((2, SORT_ROWS, XS_W), f32), pltpu.VMEM((SUBLANES, XS_W), f32),
                        pltpu.SemaphoreType.DMA((3,))],
    )
    return pl.pallas_call(
        functools.partial(_dispatch_kernel, lay=lay),
        grid_spec=grid_spec,
        out_shape=jax.ShapeDtypeStruct((lay.cap, XS_W), f32),
        compiler_params=_cparams(1),
        name="moe_dispatch",
    )(tab, h2, rows, cols)


def _expert_kernel(be_ref, xs_ref, wg_ref, wu_ref, wd_ref, ys_ref, wg_bf, wu_bf, wd_bf, *, n_blocks):
    j = pl.program_id(0)

    @pl.when(j < be_ref[n_blocks])
    def _():
        @pl.when((j == 0) | (be_ref[j] != be_ref[jnp.maximum(j - 1, 0)]))
        def _():
            wg_bf[...] = wg_ref[...].astype(bf16)
            wu_bf[...] = wu_ref[...].astype(bf16)
            wd_bf[...] = wd_ref[...].astype(bf16)

        x = xs_ref[:, 0:D_MODEL].astype(bf16)
        gate = jnp.dot(x, wg_bf[...], preferred_element_type=f32)
        up = jnp.dot(x, wu_bf[...], preferred_element_type=f32)
        mid = (jax.nn.silu(gate) * up).astype(bf16)
        ys_ref[...] = jnp.dot(mid, wd_bf[...], preferred_element_type=f32) * xs_ref[:, D_MODEL:D_MODEL + 1]


def _experts(layer, lay, block_e, xs, w_gate, w_up, w_down):
    n_blocks = lay.n_blocks
    d_ff = w_gate.shape[-1]
    used = lambda j, be: jnp.minimum(j, be[n_blocks] - 1)
    w_spec = lambda a, b: pl.BlockSpec((None, None, a, b), lambda j, be: (layer, be[j], 0, 0))
    grid_spec = pltpu.PrefetchScalarGridSpec(
        num_scalar_prefetch=1,
        grid=(n_blocks,),
        in_specs=[pl.BlockSpec((BM, XS_W), lambda j, be: (used(j, be), 0)),
                  w_spec(D_MODEL, d_ff), w_spec(D_MODEL, d_ff), w_spec(d_ff, D_MODEL)],
        out_specs=pl.BlockSpec((BM, D_MODEL), lambda j, be: (used(j, be), 0)),
        scratch_shapes=[pltpu.VMEM((D_MODEL, d_ff), bf16), pltpu.VMEM((D_MODEL, d_ff), bf16),
                        pltpu.VMEM((d_ff, D_MODEL), bf16)],
    )
    return pl.pallas_call(
        functools.partial(_expert_kernel, n_blocks=n_blocks),
        grid_spec=grid_spec,
        out_shape=jax.ShapeDtypeStruct((lay.cap, D_MODEL), f32),
        compiler_params=_cparams(1),
        name="moe_experts",
    )(block_e, xs, w_gate, w_up, w_down)


def _combine_kernel(tab_ref, ys_hbm, cols_ref, x_ref, gts_ref, gtt_ref, *rest, lay, np_tiles, final):
    if final:
        gf_ref, yp_ref, ysm_ref, staged, sem = rest
    else:
        xo_ref, staged, sem = rest
    i = pl.program_id(0)
    n_tiles = lay.n_tiles
    slot = i % 2
    is_s = i >= np_tiles

    def seg_copy(slot_, dst, off):
        return pltpu.make_async_copy(ys_hbm.at[pl.ds(dst, SUBLANES)],
                                     staged.at[slot_, pl.ds(off, SUBLANES)], sem.at[slot_])

    def start_tile(tile, slot_):
        for e in range(N_EXPERTS):
            off = tab_ref[lay.off8 + tile * N_EXPERTS + e]
            dst = tab_ref[lay.dst + tile * N_EXPERTS + e]
            _for_chunks(tab_ref[lay.n8 + tile * N_EXPERTS + e],
                        lambda c: seg_copy(slot_, pl.multiple_of(dst + c * SUBLANES, SUBLANES),
                                           pl.multiple_of(off + c * SUBLANES, SUBLANES)).start())

    @pl.when(i == 0)
    def _():
        staged[...] = jnp.zeros_like(staged)
        start_tile(0, 0)

    @pl.when(i + 1 < n_tiles)
    def _():
        start_tile(i + 1, 1 - slot)

    for e in range(N_EXPERTS):
        _for_chunks(tab_ref[lay.n8 + i * N_EXPERTS + e], lambda c: seg_copy(slot, 0, 0).wait())

    lane = lax.broadcasted_iota(i32, (TM, SORT_ROWS), 1).astype(f32)
    unperm = jnp.where((lane == cols_ref[:, 0:1]) | (lane == cols_ref[:, 1:2]), 1.0, 0.0).astype(bf16)
    ff = sum(jnp.dot(unperm, part, preferred_element_type=f32) for part in _split3(staged[slot]))
    xn = x_ref[...] + _pick(is_s, gtt_ref, gts_ref) * ff
    if final:
        y = _rms(xn, gf_ref[...])

        @pl.when(jnp.logical_not(is_s))
        def _():
            yp_ref[...] = y

        @pl.when(is_s)
        def _():
            ysm_ref[...] = y
    else:
        xo_ref[...] = xn


def _combine(tl, lay, layer, tab, ys, cols, x, mod_seq, mod_tok, g_final):
    final = g_final is not None
    tok_spec = pl.BlockSpec((TM, D_MODEL), lambda i, t: (i, 0))
    gt_seq = pl.BlockSpec((None, None, 1, D_MODEL), lambda i, t: (layer, tl.seq_index(i), 0, 5))
    gt_tok = pl.BlockSpec((None, TM, D_MODEL), lambda i, t: (layer, tl.sample_block(i), 5))
    in_specs = [pl.BlockSpec(memory_space=pl.ANY), pl.BlockSpec((TM, LANES), lambda i, t: (i, 0)),
                tok_spec, gt_seq, gt_tok]
    args = [tab, ys, cols, x, mod_seq, mod_tok]
    if final:
        in_specs.append(pl.BlockSpec((1, D_MODEL), lambda i, t: (0, 0)))
        args.append(g_final)
        out_specs = [pl.BlockSpec((TM, D_MODEL), lambda i, t: (tl.prompt_block(i), 0)),
                     pl.BlockSpec((TM, D_MODEL), lambda i, t: (tl.sample_block(i), 0))]
        out_shape = [jax.ShapeDtypeStruct((tl.n_prompt, D_MODEL), f32),
                     jax.ShapeDtypeStruct((tl.n_sample, D_MODEL), f32)]
    else:
        out_specs = tok_spec
        out_shape = jax.ShapeDtypeStruct((tl.n_tok, D_MODEL), f32)
    grid_spec = pltpu.PrefetchScalarGridSpec(
        num_scalar_prefetch=1,
        grid=(tl.n_tiles,),
        in_specs=in_specs,
        out_specs=out_specs,
        scratch_shapes=[pltpu.VMEM((2, SORT_ROWS, D_MODEL), f32), pltpu.SemaphoreType.DMA((2,))],
    )
    return pl.pallas_call(
        functools.partial(_combine_kernel, lay=lay, np_tiles=tl.np_tiles, final=final),
        grid_spec=grid_spec,
        out_shape=out_shape,
        compiler_params=_cparams(1),
        name="moe_combine",
    )(*args)


def _rope_tables(tl):
    half = HEAD_D // 2
    inv = ROPE_BASE ** (-jnp.arange(half, dtype=f32) / half)
    pos_p = jnp.arange(tl.tp, dtype=i32)
    pos_s = PAST_LEN + jnp.arange(tl.ts, dtype=i32)
    pos = jnp.concatenate([pos_p, jnp.tile(pos_s, tl.bs)])
    ang = pos.astype(f32)[:, None] * inv[None, :]
    cos, sin = jnp.cos(ang), jnp.sin(ang)
    return jnp.concatenate([cos, cos], axis=-1), jnp.concatenate([-sin, sin], axis=-1)


def kernel(x_prompt, x_sample, state_ret, state_conv, c_prompt, c_sample, w_mod, b_mod, g_mix_norm, w_in,
           w_conv, b_conv, g_conv_ln, b_conv_ln, g_ret_gn, w_out, g_ffn_norm, w_router, b_router,
           w_exp_gate, w_exp_up, w_exp_down, g_final):
    bp, tp, _ = x_prompt.shape
    bs, ts, _ = x_sample.shape
    depth = w_mod.shape[0]
    tl = _Tiles(bp, tp, bs, ts)
    lay = _Layout(tl.n_tiles)

    c_all = jnp.concatenate([c_prompt, jnp.repeat(c_sample, ts, axis=0)], axis=0)
    mod = _modulation(c_all, w_mod, b_mod)
    mod_seq = mod[:, :bp].reshape(depth, bp, 1, N_MOD * D_MODEL)
    mod_tok = mod[:, bp:]

    cos_tab, sin_tab = _rope_tables(tl)
    w_in_bf = w_in.astype(bf16)
    w_out_bf = w_out.astype(bf16)
    wr_pad = jnp.pad(w_router.astype(f32), ((0, 0), (0, LANES - N_EXPERTS)))
    wr_hi = wr_pad.astype(bf16)
    wr_lo = (wr_pad - wr_hi.astype(f32)).astype(bf16)
    br_pad = jnp.pad(b_router.astype(f32), (0, LANES - N_EXPERTS)).reshape(1, LANES)
    vec3 = lambda t: t.reshape(depth, 1, t.shape[-1])
    g_mix3, g_ffn3, gn3 = vec3(g_mix_norm), vec3(g_ffn_norm), vec3(g_ret_gn)
    b_conv3, g_ln3, b_ln3 = vec3(b_conv), vec3(g_conv_ln), vec3(b_conv_ln)

    x = (x_prompt.reshape(tl.n_prompt, D_MODEL), x_sample.reshape(tl.n_sample, D_MODEL))
    ret_p, conv_p, ret_s, conv_s = [], [], [], []
    for layer in range(depth):
        q, k, v, gate, a = _inproj(tl, layer, x, mod_seq, mod_tok, g_mix3, w_in_bf, cos_tab, sin_tab)
        ro_p, s_p = _retention_prompt(tl, layer, q, k, v, gate, gn3)
        ro_s, s_s = _retention_sample(tl, layer, q, k, v, gate, state_ret, gn3)
        co_p, buf_p = _conv_prompt(tl, layer, a, w_conv, b_conv3, g_ln3, b_ln3)
        co_s, buf_s = _conv_sample(tl, layer, a, state_conv, w_conv, b_conv3, g_ln3, b_ln3)
        x_mid, h2, rows, cols, tile_counts = _outproj(
            tl, layer, x, ro_p, ro_s, co_p, co_s, mod_seq, mod_tok, g_ffn3, w_out_bf, wr_hi, wr_lo, br_pad)
        tab, block_e = _moe_tables(lay, tile_counts[:, :, 0])
        xs = _dispatch(tl, lay, tab, h2, rows, cols)
        ys = _experts(layer, lay, block_e, xs, w_exp_gate, w_exp_up, w_exp_down)
        last = layer == depth - 1
        x = _combine(tl, lay, layer, tab, ys, cols, x_mid, mod_seq, mod_tok,
                     g_final.reshape(1, D_MODEL) if last else None)
        ret_p.append(s_p)
        ret_s.append(s_s)
        conv_p.append(buf_p)
        conv_s.append(buf_s)
    y_p, y_s = x
    return (y_p.reshape(bp, tp, D_MODEL), y_s.reshape(bs, ts, D_MODEL),
            jnp.stack(ret_p), jnp.stack(conv_p), jnp.stack(ret_s), jnp.stack(conv_s))
```

```python
import functools

import jax
import jax.numpy as jnp
from jax import lax
from jax.experimental import pallas as pl
from jax.experimental.pallas import tpu as pltpu

f32 = jnp.float32
bf16 = jnp.bfloat16
i32 = jnp.int32

D_MODEL = 1024
D_RET = 512
D_CONV = 512
N_HEADS = 4
HEAD_D = 128
RET_CHUNK = 128
RET_CHUNKS_PER_STEP = 4
ROPE_BASE = 10000.0
CONV_WIDTH = 31
CONV_HALO = CONV_WIDTH - 1
N_EXPERTS = 16
N_GROUPS = 4
GROUP_SIZE = N_EXPERTS // N_GROUPS
N_MOD = 6
EPS = 1e-6
PAST_LEN = 16384
D_IN = 4 * D_RET + 2 * D_CONV

LANES = 128
SUBLANES = 8
TM = 256
BM = 256
CONV_ROWS = 64
SAMPLE_GROUP = 8
HALO_PAD = 32
SHIFT_ROWS = HALO_PAD + TM - SUBLANES
VMEM_LIMIT = 56 * 1024 * 1024


def _cparams(n_axes, vmem=VMEM_LIMIT):
    return pltpu.CompilerParams(dimension_semantics=("arbitrary",) * n_axes, vmem_limit_bytes=vmem)


def _mod_kernel(c_ref, w_ref, b_ref, o_ref):
    cond = jax.nn.silu(c_ref[...]).astype(bf16)
    o_ref[...] = jnp.dot(cond, w_ref[...].astype(bf16), preferred_element_type=f32) + b_ref[...]


def _modulation(c_all, w_mod, b_mod):
    depth = w_mod.shape[0]
    m = c_all.shape[0]
    return pl.pallas_call(
        _mod_kernel,
        grid=(depth, N_MOD),
        in_specs=[
            pl.BlockSpec((m, D_MODEL), lambda l, j: (0, 0)),
            pl.BlockSpec((None, D_MODEL, D_MODEL), lambda l, j: (l, 0, j)),
            pl.BlockSpec((None, 1, D_MODEL), lambda l, j: (l, 0, j)),
        ],
        out_specs=pl.BlockSpec((None, m, D_MODEL), lambda l, j: (l, 0, j)),
        out_shape=jax.ShapeDtypeStruct((depth, m, N_MOD * D_MODEL), f32),
        compiler_params=_cparams(2),
        name="modulation",
    )(c_all, w_mod, b_mod.reshape(depth, 1, N_MOD * D_MODEL))


class _Tiles:
    def __init__(self, bp, tp, bs, ts):
        self.bp, self.tp, self.bs, self.ts = bp, tp, bs, ts
        self.n_prompt = bp * tp
        self.n_sample = bs * ts
        self.n_tok = self.n_prompt + self.n_sample
        assert tp % TM == 0 and self.n_sample % TM == 0
        self.tiles_per_seq = tp // TM
        self.np_tiles = self.n_prompt // TM
        self.ns_tiles = self.n_sample // TM
        self.n_tiles = self.np_tiles + self.ns_tiles

    def prompt_block(self, i):
        return jnp.minimum(i, self.np_tiles - 1)

    def sample_block(self, i):
        return jnp.maximum(i - self.np_tiles, 0)

    def seq_index(self, i):
        return jnp.minimum(i // self.tiles_per_seq, self.bp - 1)


def _mod_specs(tl, layer, col):
    seq = pl.BlockSpec((None, None, 1, D_MODEL), lambda i: (layer, tl.seq_index(i), 0, col))
    tok = pl.BlockSpec((None, TM, D_MODEL), lambda i: (layer, tl.sample_block(i), col))
    return seq, tok


def _pick(is_sample, tok_ref, seq_ref):
    return jnp.where(is_sample, tok_ref[...], seq_ref[...])


def _rms(x, g):
    return x * lax.rsqrt(jnp.mean(x * x, axis=-1, keepdims=True) + EPS) * g


def _inproj_kernel(*refs, np_tiles, split_x):
    if split_x:
        xp_ref, xs_ref = refs[:2]
        refs = refs[2:]
    else:
        x_ref = refs[0]
        refs = refs[1:]
    (shs_ref, sht_ref, scs_ref, sct_ref, g_ref, w_ref, cos_ref, sin_ref,
     q_ref, k_ref, v_ref, gate_ref, a_ref) = refs
    is_s = pl.program_id(0) >= np_tiles
    x = jnp.where(is_s, xs_ref[...], xp_ref[...]) if split_x else x_ref[...]
    h = _rms(x, g_ref[...]) * (1.0 + _pick(is_s, sct_ref, scs_ref)) + _pick(is_s, sht_ref, shs_ref)
    hb = h.astype(bf16)
    cos = cos_ref[...]
    sin = sin_ref[...]

    def proj(col):
        return jnp.dot(hb, w_ref[:, col * D_RET:(col + 1) * D_RET], preferred_element_type=f32)

    def rope(t):
        outs = []
        for hd in range(N_HEADS):
            th = t[:, hd * HEAD_D:(hd + 1) * HEAD_D]
            outs.append(th * cos + pltpu.roll(th, HEAD_D // 2, 1) * sin)
        return outs

    for hd, qh in enumerate(rope(proj(0))):
        q_ref[:, hd * HEAD_D:(hd + 1) * HEAD_D] = qh
    for hd, kh in enumerate(rope(proj(1))):
        k_ref[:, hd * HEAD_D:(hd + 1) * HEAD_D] = kh * (HEAD_D ** -0.5)
    v_ref[...] = proj(2)
    gate_ref[...] = proj(3)
    a_ref[...] = proj(4) * jax.nn.sigmoid(proj(5))


def _inproj(tl, layer, x, mod_seq, mod_tok, g_norm, w_in_bf, cos_tab, sin_tab):
    split_x = isinstance(x, tuple)
    tok_spec = pl.BlockSpec((TM, D_MODEL), lambda i: (i, 0))
    if split_x:
        x_args = list(x)
        x_specs = [pl.BlockSpec((TM, D_MODEL), lambda i: (tl.prompt_block(i), 0)),
                   pl.BlockSpec((TM, D_MODEL), lambda i: (tl.sample_block(i), 0))]
    else:
        x_args, x_specs = [x], [tok_spec]
    sh_seq, sh_tok = _mod_specs(tl, layer, 0)
    sc_seq, sc_tok = _mod_specs(tl, layer, 1)

    def table_block(i):
        return jnp.where(i < tl.np_tiles, i % tl.tiles_per_seq, tl.tiles_per_seq + tl.sample_block(i))

    tab_spec = pl.BlockSpec((TM, HEAD_D), lambda i: (table_block(i), 0))
    out_spec = pl.BlockSpec((TM, D_RET), lambda i: (i, 0))
    out_sd = jax.ShapeDtypeStruct((tl.n_tok, D_RET), f32)
    return pl.pallas_call(
        functools.partial(_inproj_kernel, np_tiles=tl.np_tiles, split_x=split_x),
        grid=(tl.n_tiles,),
        in_specs=x_specs + [
            sh_seq, sh_tok, sc_seq, sc_tok,
            pl.BlockSpec((None, 1, D_MODEL), lambda i: (layer, 0, 0)),
            pl.BlockSpec((None, D_MODEL, D_IN), lambda i: (layer, 0, 0)),
            tab_spec, tab_spec,
        ],
        out_specs=[out_spec] * 5,
        out_shape=[out_sd] * 5,
        compiler_params=_cparams(1),
        name="inproj",
    )(*x_args, mod_seq, mod_tok, mod_seq, mod_tok, g_norm, w_in_bf, cos_tab, sin_tab)


def _head_norm_gate(o, gn, gate):
    mu = jnp.mean(o, axis=-1, keepdims=True)
    var = jnp.mean(jnp.square(o - mu), axis=-1, keepdims=True)
    return jax.nn.silu(gate) * ((o - mu) * lax.rsqrt(var + EPS) * gn)


def _dot_nt(a, b):
    return lax.dot_general(a, b, (((1,), (1,)), ((), ())), preferred_element_type=f32)


def _dot_tn(a, b):
    return lax.dot_general(a, b, (((0,), (0,)), ((), ())), preferred_element_type=f32)


def _ret_prompt_kernel(q_ref, k_ref, v_ref, gate_ref, dec_ref, qd_ref, kd_ref, cd_ref, gn_ref,
                       o_ref, s_out_ref, s_ref):
    c = pl.program_id(1)

    @pl.when(c == 0)
    def _():
        s_ref[...] = jnp.zeros_like(s_ref)

    for ci in range(RET_CHUNKS_PER_STEP):
        rows = slice(ci * RET_CHUNK, (ci + 1) * RET_CHUNK)
        for hd in range(N_HEADS):
            sl = slice(hd * HEAD_D, (hd + 1) * HEAD_D)
            kh = k_ref[rows, sl]
            qb = q_ref[rows, sl].astype(bf16)
            kb = kh.astype(bf16)
            vb = v_ref[rows, sl].astype(bf16)
            s_old = s_ref[hd]
            scores = _dot_nt(qb, kb) * dec_ref[hd]
            inner = jnp.dot(scores.astype(bf16), vb, preferred_element_type=f32)
            cross = jnp.dot(qb, s_old.astype(bf16), preferred_element_type=f32) * qd_ref[hd]
            s_ref[hd] = s_old * cd_ref[hd] + _dot_tn((kh * kd_ref[hd]).astype(bf16), vb)
            o_ref[rows, sl] = _head_norm_gate(inner + cross, gn_ref[:, sl], gate_ref[rows, sl])

    @pl.when(c == pl.num_programs(1) - 1)
    def _():
        s_out_ref[...] = s_ref[...]


def _decay_tables(chunk, true_len):
    lg = jnp.log(1.0 - 2.0 ** (-5.0 - jnp.arange(N_HEADS, dtype=f32)))
    idx = jnp.arange(chunk, dtype=f32)
    rel = idx[:, None] - idx[None, :]
    decay = jnp.where(rel[None] >= 0, jnp.exp(jnp.maximum(rel, 0.0)[None] * lg[:, None, None]), 0.0)
    q_decay = jnp.exp((idx[None, :] + 1.0) * lg[:, None])
    k_decay = jnp.exp((true_len - 1.0 - idx[None, :]) * lg[:, None])
    c_decay = jnp.exp(true_len * lg)
    return decay, q_decay, k_decay, c_decay


def _retention_prompt(tl, layer, q, k, v, gate, g_ret_gn):
    step_rows = RET_CHUNK * RET_CHUNKS_PER_STEP
    assert tl.tp % step_rows == 0
    n_chunks = tl.tp // step_rows
    decay, q_decay, k_decay, c_decay = _decay_tables(RET_CHUNK, RET_CHUNK)
    bcast = lambda t: jnp.broadcast_to(t[:, :, None], (N_HEADS, RET_CHUNK, HEAD_D))
    cd = jnp.broadcast_to(c_decay[:, None, None], (N_HEADS, 1, HEAD_D))
    tok_spec = pl.BlockSpec((step_rows, D_RET), lambda b, c: (b * n_chunks + c, 0))
    tab_spec = pl.BlockSpec((N_HEADS, RET_CHUNK, HEAD_D), lambda b, c: (0, 0, 0))
    return pl.pallas_call(
        _ret_prompt_kernel,
        grid=(tl.bp, n_chunks),
        in_specs=[tok_spec] * 4 + [tab_spec] * 3 + [
            pl.BlockSpec((N_HEADS, 1, HEAD_D), lambda b, c: (0, 0, 0)),
            pl.BlockSpec((None, 1, D_RET), lambda b, c: (layer, 0, 0)),
        ],
        out_specs=[tok_spec, pl.BlockSpec((None, N_HEADS, HEAD_D, HEAD_D), lambda b, c: (b, 0, 0, 0))],
        out_shape=[jax.ShapeDtypeStruct((tl.n_prompt, D_RET), f32),
                   jax.ShapeDtypeStruct((tl.bp, N_HEADS, HEAD_D, HEAD_D), f32)],
        scratch_shapes=[pltpu.VMEM((N_HEADS, HEAD_D, HEAD_D), f32)],
        compiler_params=_cparams(2),
        name="retention_prompt",
    )(q, k, v, gate, decay, bcast(q_decay), bcast(k_decay), cd, g_ret_gn)


def _ret_sample_kernel(q_ref, k_ref, v_ref, gate_ref, s_in_ref, dec_ref, qd_ref, kd_ref, cd_ref, gn_ref,
                       o_ref, s_out_ref, *, ts):
    seqs_per_tile = SUBLANES // ts
    row = lax.broadcasted_iota(i32, (SUBLANES, HEAD_D), 0)
    for t in range(SAMPLE_GROUP // seqs_per_tile):
        rows = slice(t * SUBLANES, (t + 1) * SUBLANES)
        for hd in range(N_HEADS):
            sl = slice(hd * HEAD_D, (hd + 1) * HEAD_D)
            qh = q_ref[rows, sl]
            kh = k_ref[rows, sl] * kd_ref[hd]
            vb = v_ref[rows, sl].astype(bf16)
            qb = qh.astype(bf16)
            scores = _dot_nt(qb, k_ref[rows, sl].astype(bf16)) * dec_ref[hd]
            o = jnp.dot(scores.astype(bf16), vb, preferred_element_type=f32)
            for s in range(seqs_per_tile):
                b = t * seqs_per_tile + s
                mine = (row >= s * ts) & (row < (s + 1) * ts)
                s_old = s_in_ref[b, hd]
                q_s = jnp.where(mine, qh, 0.0).astype(bf16)
                k_s = jnp.where(mine, kh, 0.0).astype(bf16)
                o = o + jnp.dot(q_s, s_old.astype(bf16), preferred_element_type=f32) * qd_ref[hd]
                s_out_ref[b, hd] = s_old * cd_ref[hd] + _dot_tn(k_s, vb)
            o_ref[rows, sl] = _head_norm_gate(o, gn_ref[:, sl], gate_ref[rows, sl])


def _retention_sample(tl, layer, q, k, v, gate, state_ret, g_ret_gn):
    ts = tl.ts
    assert SUBLANES % ts == 0 and tl.bs % SAMPLE_GROUP == 0
    seqs_per_tile = SUBLANES // ts
    decay, q_decay, k_decay, c_decay = _decay_tables(ts, ts)
    eye = jnp.eye(seqs_per_tile, dtype=f32)
    dec_tile = jnp.einsum("ab,hij->haibj", eye, decay).reshape(N_HEADS, SUBLANES, SUBLANES)
    tile_rows = lambda t: jnp.broadcast_to(jnp.tile(t, (1, seqs_per_tile))[:, :, None],
                                           (N_HEADS, SUBLANES, HEAD_D))
    cd = jnp.broadcast_to(c_decay[:, None, None], (N_HEADS, 1, HEAD_D))
    rows = SAMPLE_GROUP * ts
    first = tl.n_prompt // rows
    tok_spec = pl.BlockSpec((rows, D_RET), lambda i: (first + i, 0))
    const3 = lambda shape: pl.BlockSpec(shape, lambda i: (0, 0, 0))
    st_block = (SAMPLE_GROUP, N_HEADS, HEAD_D, HEAD_D)
    return pl.pallas_call(
        functools.partial(_ret_sample_kernel, ts=ts),
        grid=(tl.bs // SAMPLE_GROUP,),
        in_specs=[tok_spec] * 4 + [
            pl.BlockSpec((None,) + st_block, lambda i: (layer, i, 0, 0, 0)),
            const3((N_HEADS, SUBLANES, SUBLANES)),
            const3((N_HEADS, SUBLANES, HEAD_D)),
            const3((N_HEADS, SUBLANES, HEAD_D)),
            const3((N_HEADS, 1, HEAD_D)),
            pl.BlockSpec((None, 1, D_RET), lambda i: (layer, 0, 0)),
        ],
        out_specs=[pl.BlockSpec((rows, D_RET), lambda i: (i, 0)),
                   pl.BlockSpec(st_block, lambda i: (i, 0, 0, 0))],
        out_shape=[jax.ShapeDtypeStruct((tl.n_sample, D_RET), f32),
                   jax.ShapeDtypeStruct((tl.bs, N_HEADS, HEAD_D, HEAD_D), f32)],
        compiler_params=_cparams(1),
        name="retention_sample",
    )(q, k, v, gate, state_ret, dec_tile, tile_rows(q_decay), tile_rows(k_decay), cd, g_ret_gn)


def _ln_silu(cv, g, b):
    mu = jnp.mean(cv, axis=-1, keepdims=True)
    var = jnp.mean(jnp.square(cv - mu), axis=-1, keepdims=True)
    return jax.nn.silu((cv - mu) * lax.rsqrt(var + EPS) * g + b)


def _conv_prompt_kernel(a_ref, w_ref, b_ref, g_ref, bl_ref, o_ref, buf_ref, full_ref, sh_ref):
    j = pl.program_id(1)

    @pl.when(j == 0)
    def _():
        full_ref[0:HALO_PAD, :] = jnp.zeros((HALO_PAD, D_CONV), f32)

    @pl.when(j > 0)
    def _():
        full_ref[0:HALO_PAD, :] = full_ref[TM:TM + HALO_PAD, :]

    full_ref[HALO_PAD:HALO_PAD + TM, :] = a_ref[...]
    for s in range(1, SUBLANES):
        sh_ref[s - 1] = full_ref[s:s + SHIFT_ROWS, :]
    shift = HALO_PAD - CONV_HALO
    for r0 in range(0, TM, CONV_ROWS):
        acc = jnp.broadcast_to(b_ref[...], (CONV_ROWS, D_CONV))
        for tap in range(CONV_WIDTH):
            s = (r0 + tap + shift) % SUBLANES
            base = r0 + tap + shift - s
            rows = full_ref[base:base + CONV_ROWS, :] if s == 0 else sh_ref[s - 1, base:base + CONV_ROWS, :]
            acc = acc + rows * w_ref[tap:tap + 1, :]
        o_ref[r0:r0 + CONV_ROWS, :] = _ln_silu(acc, g_ref[...], bl_ref[...])

    @pl.when(j == pl.num_programs(1) - 1)
    def _():
        buf_ref[...] = a_ref[TM - CONV_HALO:TM, :]


def _conv_prompt(tl, layer, a, w_conv, b_conv, g_ln, b_ln):
    tps = tl.tiles_per_seq
    vec = pl.BlockSpec((None, 1, D_CONV), lambda b, j: (layer, 0, 0))
    return pl.pallas_call(
        _conv_prompt_kernel,
        grid=(tl.bp, tps),
        in_specs=[pl.BlockSpec((TM, D_CONV), lambda b, j: (b * tps + j, 0)),
                  pl.BlockSpec((None, CONV_WIDTH, D_CONV), lambda b, j: (layer, 0, 0)),
                  vec, vec, vec],
        out_specs=[pl.BlockSpec((TM, D_CONV), lambda b, j: (b * tps + j, 0)),
                   pl.BlockSpec((None, CONV_HALO, D_CONV), lambda b, j: (b, 0, 0))],
        out_shape=[jax.ShapeDtypeStruct((tl.n_prompt, D_CONV), f32),
                   jax.ShapeDtypeStruct((tl.bp, CONV_HALO, D_CONV), f32)],
        scratch_shapes=[pltpu.VMEM((HALO_PAD + TM, D_CONV), f32),
                        pltpu.VMEM((SUBLANES - 1, SHIFT_ROWS, D_CONV), f32)],
        compiler_params=_cparams(2),
        name="conv_prompt",
    )(a, w_conv, b_conv, g_ln, b_ln)


def _conv_sample_kernel(a_ref, st_ref, w_ref, b_ref, g_ref, bl_ref, o_ref, buf_ref, full_ref, cv_ref, *, ts):
    for s in range(SAMPLE_GROUP):
        full_ref[0:CONV_HALO, :] = st_ref[s]
        full_ref[CONV_HALO:CONV_HALO + ts, :] = a_ref[s * ts:(s + 1) * ts, :]
        acc = jnp.broadcast_to(b_ref[...], (ts, D_CONV))
        for tap in range(CONV_WIDTH):
            acc = acc + full_ref[tap:tap + ts, :] * w_ref[tap:tap + 1, :]
        cv_ref[s * ts:(s + 1) * ts, :] = acc
        buf_ref[s] = full_ref[ts:ts + CONV_HALO, :]
    o_ref[...] = _ln_silu(cv_ref[...], g_ref[...], bl_ref[...])


def _conv_sample(tl, layer, a, state_conv, w_conv, b_conv, g_ln, b_ln):
    ts = tl.ts
    rows = SAMPLE_GROUP * ts
    first = tl.n_prompt // rows
    vec = pl.BlockSpec((None, 1, D_CONV), lambda i: (layer, 0, 0))
    return pl.pallas_call(
        functools.partial(_conv_sample_kernel, ts=ts),
        grid=(tl.bs // SAMPLE_GROUP,),
        in_specs=[pl.BlockSpec((rows, D_CONV), lambda i: (first + i, 0)),
                  pl.BlockSpec((None, SAMPLE_GROUP, CONV_HALO, D_CONV), lambda i: (layer, i, 0, 0)),
                  pl.BlockSpec((None, CONV_WIDTH, D_CONV), lambda i: (layer, 0, 0)),
                  vec, vec, vec],
        out_specs=[pl.BlockSpec((rows, D_CONV), lambda i: (i, 0)),
                   pl.BlockSpec((SAMPLE_GROUP, CONV_HALO, D_CONV), lambda i: (i, 0, 0))],
        out_shape=[jax.ShapeDtypeStruct((tl.n_sample, D_CONV), f32),
                   jax.ShapeDtypeStruct((tl.bs, CONV_HALO, D_CONV), f32)],
        scratch_shapes=[pltpu.VMEM((CONV_HALO + ts + SUBLANES, D_CONV), f32),
                        pltpu.VMEM((rows, D_CONV), f32)],
        compiler_params=_cparams(1),
        name="conv_sample",
    )(a, state_conv, w_conv, b_conv, g_ln, b_ln)


SORT_ROWS = 2 * TM + N_EXPERTS * SUBLANES
XS_W = D_MODEL + LANES


def _split3(x):
    a = x.astype(bf16)
    r = x - a.astype(f32)
    b = r.astype(bf16)
    c = (r - b.astype(f32)).astype(bf16)
    return a, b, c


def _first_of4(vals, m):
    return jnp.where(vals[0] == m, 0.0, jnp.where(vals[1] == m, 1.0, jnp.where(vals[2] == m, 2.0, 3.0)))


def _rows_to_tile(rows, n_rows):
    sub = lax.broadcasted_iota(i32, (n_rows, TM), 0)
    out = jnp.zeros((n_rows, TM), f32)
    for r, val in enumerate(rows):
        out = jnp.where(sub == r, val, out)
    return out


def _outproj_kernel(*refs, np_tiles, split_x):
    if split_x:
        xp_ref, xs_ref = refs[:2]
        refs = refs[2:]
    else:
        x_ref = refs[0]
        refs = refs[1:]
    (retp_ref, rets_ref, cvp_ref, cvs_ref, gts_ref, gtt_ref, shs_ref, sht_ref, scs_ref, sct_ref,
     g_ref, wo_ref, wrh_ref, wrl_ref, br_ref,
     xo_ref, h2_ref, rows_ref, cols_ref, cnt_ref) = refs
    is_s = pl.program_id(0) >= np_tiles

    x = jnp.where(is_s, xs_ref[...], xp_ref[...]) if split_x else x_ref[...]
    ret = jnp.where(is_s, rets_ref[...], retp_ref[...]).astype(bf16)
    cv = jnp.where(is_s, cvs_ref[...], cvp_ref[...]).astype(bf16)
    mix_out = (jnp.dot(ret, wo_ref[0:D_RET, :], preferred_element_type=f32)
               + jnp.dot(cv, wo_ref[D_RET:D_RET + D_CONV, :], preferred_element_type=f32))
    xn = x + _pick(is_s, gtt_ref, gts_ref) * mix_out
    xo_ref[...] = xn
    h2 = _rms(xn, g_ref[...]) * (1.0 + _pick(is_s, sct_ref, scs_ref)) + _pick(is_s, sht_ref, shs_ref)
    h_hi = h2.astype(bf16)
    h2_ref[...] = h_hi

    h_lo = (h2 - h_hi.astype(f32)).astype(bf16)
    logits = (jnp.dot(h_hi, wrh_ref[...], preferred_element_type=f32)
              + jnp.dot(h_hi, wrl_ref[...], preferred_element_type=f32)
              + jnp.dot(h_lo, wrh_ref[...], preferred_element_type=f32)) + br_ref[...]
    lt = logits.T
    row = [lt[e:e + 1, :] for e in range(N_EXPERTS)]
    top = functools.reduce(jnp.maximum, row)
    ex = [jnp.exp(r - top) for r in row]
    den = functools.reduce(jnp.add, ex)
    p = [v / den for v in ex]

    best = None
    for g in range(N_GROUPS):
        a = p[g * GROUP_SIZE:(g + 1) * GROUP_SIZE]
        m1 = functools.reduce(jnp.maximum, a)
        i1 = _first_of4(a, m1)
        b = [jnp.where(i1 == float(j), -1.0, a[j]) for j in range(GROUP_SIZE)]
        m2 = functools.reduce(jnp.maximum, b)
        i2 = _first_of4(b, m2)
        cand = (m1 + m2, m1, m2, i1 + float(g * GROUP_SIZE), i2 + float(g * GROUP_SIZE))
        if best is None:
            best = cand
        else:
            take = cand[0] > best[0]
            best = tuple(jnp.where(take, c, o) for c, o in zip(cand, best))
    _, m1, m2, e0, e1 = best
    denom = m1 + m2
    w0 = m1 / denom
    w1 = m2 / denom

    ex_id = lax.broadcasted_iota(i32, (N_EXPERTS, TM), 0).astype(f32)
    sel0 = ex_id == e0
    sel1 = ex_id == e1
    ind = jnp.where(sel0 | sel1, 1.0, 0.0)
    t_r = lax.broadcasted_iota(i32, (TM, TM), 0)
    t_c = lax.broadcasted_iota(i32, (TM, TM), 1)
    earlier = jnp.where(t_r < t_c, 1.0, 0.0).astype(bf16)
    prefix = jnp.dot(ind.astype(bf16), earlier, preferred_element_type=f32)
    cnt = jnp.sum(ind, axis=-1, keepdims=True)
    cnt8 = jnp.floor((cnt + float(SUBLANES - 1)) * (1.0 / SUBLANES)) * float(SUBLANES)
    e_r = lax.broadcasted_iota(i32, (N_EXPERTS, N_EXPERTS), 0)
    e_c = lax.broadcasted_iota(i32, (N_EXPERTS, N_EXPERTS), 1)
    below = jnp.where(e_c < e_r, 1.0, 0.0).astype(bf16)
    seg_off = jnp.dot(below, jnp.broadcast_to(cnt8, (N_EXPERTS, TM)).astype(bf16),
                      preferred_element_type=f32)
    where_to = seg_off + prefix
    pos0 = jnp.sum(jnp.where(sel0, where_to, 0.0), axis=0, keepdims=True)
    pos1 = jnp.sum(jnp.where(sel1, where_to, 0.0), axis=0, keepdims=True)

    w0p = [v.astype(f32) for v in _split3(w0)]
    w1p = [v.astype(f32) for v in _split3(w1)]
    info = [pos0, pos1] + w0p + w1p
    rows_ref[...] = _rows_to_tile(info, SUBLANES)
    cols_ref[...] = _rows_to_tile(info, LANES).T
    cnt_ref[...] = jnp.broadcast_to(cnt, (N_EXPERTS, LANES))


def _outproj(tl, layer, x, ret_p, ret_s, cv_p, cv_s, mod_seq, mod_tok, g_norm, w_out_bf, wr_hi, wr_lo, br_pad):
    split_x = isinstance(x, tuple)
    tok_spec = pl.BlockSpec((TM, D_MODEL), lambda i: (i, 0))
    p_spec = lambda w: pl.BlockSpec((TM, w), lambda i: (tl.prompt_block(i), 0))
    s_spec = lambda w: pl.BlockSpec((TM, w), lambda i: (tl.sample_block(i), 0))
    if split_x:
        x_args, x_specs = list(x), [p_spec(D_MODEL), s_spec(D_MODEL)]
    else:
        x_args, x_specs = [x], [tok_spec]
    mods = []
    for col in (2, 3, 4):
        mods += list(_mod_specs(tl, layer, col))
    wr_spec = pl.BlockSpec((D_MODEL, LANES), lambda i: (0, 0))
    return pl.pallas_call(
        functools.partial(_outproj_kernel, np_tiles=tl.np_tiles, split_x=split_x),
        grid=(tl.n_tiles,),
        in_specs=x_specs + [p_spec(D_RET), s_spec(D_RET), p_spec(D_CONV), s_spec(D_CONV)] + mods + [
            pl.BlockSpec((None, 1, D_MODEL), lambda i: (layer, 0, 0)),
            pl.BlockSpec((None, D_MODEL, D_MODEL), lambda i: (layer, 0, 0)),
            wr_spec, wr_spec,
            pl.BlockSpec((1, LANES), lambda i: (0, 0)),
        ],
        out_specs=[tok_spec, tok_spec,
                   pl.BlockSpec((None, SUBLANES, TM), lambda i: (i, 0, 0)),
                   pl.BlockSpec((TM, LANES), lambda i: (i, 0)),
                   pl.BlockSpec((None, N_EXPERTS, LANES), lambda i: (i, 0, 0))],
        out_shape=[jax.ShapeDtypeStruct((tl.n_tok, D_MODEL), f32),
                   jax.ShapeDtypeStruct((tl.n_tok, D_MODEL), bf16),
                   jax.ShapeDtypeStruct((tl.n_tiles, SUBLANES, TM), f32),
                   jax.ShapeDtypeStruct((tl.n_tok, LANES), f32),
                   jax.ShapeDtypeStruct((tl.n_tiles, N_EXPERTS, LANES), f32)],
        compiler_params=_cparams(1),
        name="outproj_router",
    )(*x_args, ret_p, ret_s, cv_p, cv_s, *([mod_seq, mod_tok] * 3), g_norm, w_out_bf, wr_hi, wr_lo, br_pad)


N_CHUNKS = SORT_ROWS // SUBLANES


class _Layout:
    def __init__(self, n_tiles):
        self.n_tiles = n_tiles
        self.tail_start = n_tiles * N_CHUNKS
        self.tail_n8 = self.tail_start + N_EXPERTS
        worst = 2 * n_tiles * TM + n_tiles * N_EXPERTS * (SUBLANES - 1) + N_EXPERTS * (BM - SUBLANES)
        self.n_blocks = -(-worst // BM)
        self.cap = self.n_blocks * BM
        self.dump = self.cap
        self.xs_rows = self.cap + 2 * SORT_ROWS


def _moe_tables(lay, tile_counts):
    c8 = ((tile_counts.astype(i32) + SUBLANES - 1) // SUBLANES) * SUBLANES
    base8 = jnp.cumsum(c8, axis=0) - c8
    tot8 = jnp.sum(c8, axis=0)
    region = ((tot8 + BM - 1) // BM) * BM
    g_end = jnp.cumsum(region)
    g_start = g_end - region
    seg_end = jnp.cumsum(c8, axis=1)
    seg_dst = g_start[None, :] + base8
    n_used = g_end[-1] // BM
    blk = jnp.arange(lay.n_blocks, dtype=i32)
    block_e = jnp.minimum(jnp.sum((g_end[None, :] <= blk[:, None] * BM).astype(i32), axis=1), N_EXPERTS - 1)
    block_e = jnp.where(blk < n_used, block_e, block_e[n_used - 1])
    row0 = jnp.arange(N_CHUNKS, dtype=i32) * SUBLANES
    owner = jnp.sum((seg_end[:, None, :] <= row0[None, :, None]).astype(i32), axis=-1)
    onehot = (owner[:, :, None] == jnp.arange(N_EXPERTS, dtype=i32)[None, None, :]).astype(i32)
    delta = seg_dst - (seg_end - c8)
    chunk_dst = jnp.where(owner < N_EXPERTS, row0[None, :] + jnp.sum(onehot * delta[:, None, :], axis=-1), -1)
    tab = jnp.concatenate([chunk_dst.ravel(), g_start + tot8, (region - tot8) // SUBLANES]).astype(i32)
    return tab, jnp.concatenate([block_e, n_used[None]]).astype(i32)


def _for_chunks(n, fn):
    def body(c, carry):
        fn(c)
        return carry

    lax.fori_loop(0, n, body, 0)


def _dispatch_kernel(tab_ref, h2_ref, rows_ref, cols_ref, xs_hbm, sorted_ref, zero_ref, sem, *, lay):
    i = pl.program_id(0)
    n_tiles = lay.n_tiles
    slot = i % 2

    def tail_copy(dst):
        return pltpu.make_async_copy(zero_ref, xs_hbm.at[pl.ds(dst, SUBLANES)], sem.at[2])

    def wait_tile(slot_):
        pltpu.make_async_copy(sorted_ref.at[slot_], xs_hbm.at[pl.ds(0, SORT_ROWS)], sem.at[slot_]).wait()

    @pl.when(i == 0)
    def _():
        zero_ref[...] = jnp.zeros_like(zero_ref)
        for e in range(N_EXPERTS):
            start = tab_ref[lay.tail_start + e]
            _for_chunks(tab_ref[lay.tail_n8 + e],
                        lambda c: tail_copy(pl.multiple_of(start + c * SUBLANES, SUBLANES)).start())

    @pl.when(i >= 2)
    def _():
        wait_tile(slot)

    pos0 = rows_ref[0:1, :]
    pos1 = rows_ref[1:2, :]
    r_id = lax.broadcasted_iota(i32, (SORT_ROWS, TM), 0).astype(f32)
    p0 = r_id == pos0
    p1 = r_id == pos1
    perm = jnp.where(p0 | p1, 1.0, 0.0).astype(bf16)
    lane = lax.broadcasted_iota(i32, (TM, LANES), 1)
    cols = cols_ref[...]
    wpart0 = jnp.where((lane >= 2) & (lane < 5), cols, 0.0).astype(bf16)
    wpart1 = jnp.where((lane >= 5) & (lane < 8), cols, 0.0).astype(bf16)
    sw = (jnp.dot(jnp.where(p0, 1.0, 0.0).astype(bf16), wpart0, preferred_element_type=f32)
          + jnp.dot(jnp.where(p1, 1.0, 0.0).astype(bf16), wpart1, preferred_element_type=f32))
    sorted_ref[slot, :, 0:D_MODEL] = jnp.dot(perm, h2_ref[...], preferred_element_type=f32)
    sorted_ref[slot, :, D_MODEL:XS_W] = jnp.broadcast_to(jnp.sum(sw, axis=-1, keepdims=True), (SORT_ROWS, LANES))

    for c in range(N_CHUNKS):
        dst = tab_ref[i * N_CHUNKS + c]
        dst = jnp.where(dst < 0, lay.dump + slot * SORT_ROWS + c * SUBLANES, dst)
        pltpu.make_async_copy(sorted_ref.at[slot, pl.ds(c * SUBLANES, SUBLANES)],
                              xs_hbm.at[pl.ds(pl.multiple_of(dst, SUBLANES), SUBLANES)], sem.at[slot]).start()

    @pl.when(i == n_tiles - 1)
    def _():
        if n_tiles >= 2:
            wait_tile(1 - slot)
        wait_tile(slot)
        for e in range(N_EXPERTS):
            _for_chunks(tab_ref[lay.tail_n8 + e], lambda c: tail_copy(0).wait())


def _dispatch(tl, lay, tab, h2, rows, cols):
    grid_spec = pltpu.PrefetchScalarGridSpec(
        num_scalar_prefetch=1,
        grid=(tl.n_tiles,),
        in_specs=[pl.BlockSpec((TM, D_MODEL), lambda i, t: (i, 0)),
                  pl.BlockSpec((None, SUBLANES, TM), lambda i, t: (i, 0, 0)),
                  pl.BlockSpec((TM, LANES), lambda i, t: (i, 0))],
        out_specs=pl.BlockSpec(memory_space=pl.ANY),
        scratch_shapes=[pltpu.VMEM((2, SORT_ROWS, XS_W), f32), pltpu.VMEM((SUBLANES, XS_W), f32),
                        pltpu.SemaphoreType.DMA((3,))],
    )
    return pl.pallas_call(
        functools.partial(_dispatch_kernel, lay=lay),
        grid_spec=grid_spec,
        out_shape=jax.ShapeDtypeStruct((lay.xs_rows, XS_W), f32),
        compiler_params=_cparams(1),
        name="moe_dispatch",
    )(tab, h2, rows, cols)


def _expert_kernel(be_ref, xs_ref, wg_ref, wu_ref, wd_ref, ys_ref, wg_bf, wu_bf, wd_bf, *, n_blocks):
    j = pl.program_id(0)

    @pl.when(j < be_ref[n_blocks])
    def _():
        @pl.when((j == 0) | (be_ref[j] != be_ref[jnp.maximum(j - 1, 0)]))
        def _():
            wg_bf[...] = wg_ref[...].astype(bf16)
            wu_bf[...] = wu_ref[...].astype(bf16)
            wd_bf[...] = wd_ref[...].astype(bf16)

        x = xs_ref[:, 0:D_MODEL].astype(bf16)
        gate = jnp.dot(x, wg_bf[...], preferred_element_type=f32)
        up = jnp.dot(x, wu_bf[...], preferred_element_type=f32)
        mid = (jax.nn.silu(gate) * up).astype(bf16)
        ys_ref[...] = jnp.dot(mid, wd_bf[...], preferred_element_type=f32) * xs_ref[:, D_MODEL:D_MODEL + 1]


def _experts(layer, lay, block_e, xs, w_gate, w_up, w_down):
    n_blocks = lay.n_blocks
    d_ff = w_gate.shape[-1]
    used = lambda j, be: jnp.minimum(j, be[n_blocks] - 1)
    w_spec = lambda a, b: pl.BlockSpec((None, None, a, b), lambda j, be: (layer, be[j], 0, 0))
    grid_spec = pltpu.PrefetchScalarGridSpec(
        num_scalar_prefetch=1,
        grid=(n_blocks,),
        in_specs=[pl.BlockSpec((BM, XS_W), lambda j, be: (used(j, be), 0)),
                  w_spec(D_MODEL, d_ff), w_spec(D_MODEL, d_ff), w_spec(d_ff, D_MODEL)],
        out_specs=pl.BlockSpec((BM, D_MODEL), lambda j, be: (used(j, be), 0)),
        scratch_shapes=[pltpu.VMEM((D_MODEL, d_ff), bf16), pltpu.VMEM((D_MODEL, d_ff), bf16),
                        pltpu.VMEM((d_ff, D_MODEL), bf16)],
    )
    return pl.pallas_call(
        functools.partial(_expert_kernel, n_blocks=n_blocks),
        grid_spec=grid_spec,
        out_shape=jax.ShapeDtypeStruct((lay.cap, D_MODEL), f32),
        compiler_params=_cparams(1),
        name="moe_experts",
    )(block_e, xs, w_gate, w_up, w_down)


def _combine_kernel(tab_ref, ys_hbm, cols_ref, x_ref, gts_ref, gtt_ref, *rest, lay, np_tiles, final):
    if final:
        gf_ref, yp_ref, ysm_ref, staged, sem = rest
    else:
        xo_ref, staged, sem = rest
    i = pl.program_id(0)
    n_tiles = lay.n_tiles
    slot = i % 2
    is_s = i >= np_tiles

    def start_tile(tile, slot_):
        for c in range(N_CHUNKS):
            src = jnp.maximum(tab_ref[tile * N_CHUNKS + c], 0)
            pltpu.make_async_copy(ys_hbm.at[pl.ds(pl.multiple_of(src, SUBLANES), SUBLANES)],
                                  staged.at[slot_, pl.ds(c * SUBLANES, SUBLANES)], sem.at[slot_]).start()

    @pl.when(i == 0)
    def _():
        start_tile(0, 0)

    @pl.when(i + 1 < n_tiles)
    def _():
        start_tile(i + 1, 1 - slot)

    pltpu.make_async_copy(ys_hbm.at[pl.ds(0, SORT_ROWS)], staged.at[slot], sem.at[slot]).wait()

    lane = lax.broadcasted_iota(i32, (TM, SORT_ROWS), 1).astype(f32)
    unperm = jnp.where((lane == cols_ref[:, 0:1]) | (lane == cols_ref[:, 1:2]), 1.0, 0.0).astype(bf16)
    ff = sum(jnp.dot(unperm, part, preferred_element_type=f32) for part in _split3(staged[slot]))
    xn = x_ref[...] + _pick(is_s, gtt_ref, gts_ref) * ff
    if final:
        y = _rms(xn, gf_ref[...])

        @pl.when(jnp.logical_not(is_s))
        def _():
            yp_ref[...] = y

        @pl.when(is_s)
        def _():
            ysm_ref[...] = y
    else:
        xo_ref[...] = xn


def _combine(tl, lay, layer, tab, ys, cols, x, mod_seq, mod_tok, g_final):
    final = g_final is not None
    tok_spec = pl.BlockSpec((TM, D_MODEL), lambda i, t: (i, 0))
    gt_seq = pl.BlockSpec((None, None, 1, D_MODEL), lambda i, t: (layer, tl.seq_index(i), 0, 5))
    gt_tok = pl.BlockSpec((None, TM, D_MODEL), lambda i, t: (layer, tl.sample_block(i), 5))
    in_specs = [pl.BlockSpec(memory_space=pl.ANY), pl.BlockSpec((TM, LANES), lambda i, t: (i, 0)),
                tok_spec, gt_seq, gt_tok]
    args = [tab, ys, cols, x, mod_seq, mod_tok]
    if final:
        in_specs.append(pl.BlockSpec((1, D_MODEL), lambda i, t: (0, 0)))
        args.append(g_final)
        out_specs = [pl.BlockSpec((TM, D_MODEL), lambda i, t: (tl.prompt_block(i), 0)),
                     pl.BlockSpec((TM, D_MODEL), lambda i, t: (tl.sample_block(i), 0))]
        out_shape = [jax.ShapeDtypeStruct((tl.n_prompt, D_MODEL), f32),
                     jax.ShapeDtypeStruct((tl.n_sample, D_MODEL), f32)]
    else:
        out_specs = tok_spec
        out_shape = jax.ShapeDtypeStruct((tl.n_tok, D_MODEL), f32)
    grid_spec = pltpu.PrefetchScalarGridSpec(
        num_scalar_prefetch=1,
        grid=(tl.n_tiles,),
        in_specs=in_specs,
        out_specs=out_specs,
        scratch_shapes=[pltpu.VMEM((2, SORT_ROWS, D_MODEL), f32), pltpu.SemaphoreType.DMA((2,))],
    )
    return pl.pallas_call(
        functools.partial(_combine_kernel, lay=lay, np_tiles=tl.np_tiles, final=final),
        grid_spec=grid_spec,
        out_shape=out_shape,
        compiler_params=_cparams(1),
        name="moe_combine",
    )(*args)


def _rope_tables(tl):
    half = HEAD_D // 2
    inv = ROPE_BASE ** (-jnp.arange(half, dtype=f32) / half)
    pos_p = jnp.arange(tl.tp, dtype=i32)
    pos_s = PAST_LEN + jnp.arange(tl.ts, dtype=i32)
    pos = jnp.concatenate([pos_p, jnp.tile(pos_s, tl.bs)])
    ang = pos.astype(f32)[:, None] * inv[None, :]
    cos, sin = jnp.cos(ang), jnp.sin(ang)
    return jnp.concatenate([cos, cos], axis=-1), jnp.concatenate([-sin, sin], axis=-1)


def kernel(x_prompt, x_sample, state_ret, state_conv, c_prompt, c_sample, w_mod, b_mod, g_mix_norm, w_in,
           w_conv, b_conv, g_conv_ln, b_conv_ln, g_ret_gn, w_out, g_ffn_norm, w_router, b_router,
           w_exp_gate, w_exp_up, w_exp_down, g_final):
    bp, tp, _ = x_prompt.shape
    bs, ts, _ = x_sample.shape
    depth = w_mod.shape[0]
    tl = _Tiles(bp, tp, bs, ts)
    lay = _Layout(tl.n_tiles)

    c_all = jnp.concatenate([c_prompt, jnp.repeat(c_sample, ts, axis=0)], axis=0)
    mod = _modulation(c_all, w_mod, b_mod)
    mod_seq = mod[:, :bp].reshape(depth, bp, 1, N_MOD * D_MODEL)
    mod_tok = mod[:, bp:]

    cos_tab, sin_tab = _rope_tables(tl)
    w_in_bf = w_in.astype(bf16)
    w_out_bf = w_out.astype(bf16)
    wr_pad = jnp.pad(w_router.astype(f32), ((0, 0), (0, LANES - N_EXPERTS)))
    wr_hi = wr_pad.astype(bf16)
    wr_lo = (wr_pad - wr_hi.astype(f32)).astype(bf16)
    br_pad = jnp.pad(b_router.astype(f32), (0, LANES - N_EXPERTS)).reshape(1, LANES)
    vec3 = lambda t: t.reshape(depth, 1, t.shape[-1])
    g_mix3, g_ffn3, gn3 = vec3(g_mix_norm), vec3(g_ffn_norm), vec3(g_ret_gn)
    b_conv3, g_ln3, b_ln3 = vec3(b_conv), vec3(g_conv_ln), vec3(b_conv_ln)

    x = (x_prompt.reshape(tl.n_prompt, D_MODEL), x_sample.reshape(tl.n_sample, D_MODEL))
    ret_p, conv_p, ret_s, conv_s = [], [], [], []
    for layer in range(depth):
        q, k, v, gate, a = _inproj(tl, layer, x, mod_seq, mod_tok, g_mix3, w_in_bf, cos_tab, sin_tab)
        ro_p, s_p = _retention_prompt(tl, layer, q, k, v, gate, gn3)
        ro_s, s_s = _retention_sample(tl, layer, q, k, v, gate, state_ret, gn3)
        co_p, buf_p = _conv_prompt(tl, layer, a, w_conv, b_conv3, g_ln3, b_ln3)
        co_s, buf_s = _conv_sample(tl, layer, a, state_conv, w_conv, b_conv3, g_ln3, b_ln3)
        x_mid, h2, rows, cols, tile_counts = _outproj(
            tl, layer, x, ro_p, ro_s, co_p, co_s, mod_seq, mod_tok, g_ffn3, w_out_bf, wr_hi, wr_lo, br_pad)
        tab, block_e = _moe_tables(lay, tile_counts[:, :, 0])
        xs = _dispatch(tl, lay, tab, h2, rows, cols)
        ys = _experts(layer, lay, block_e, xs, w_exp_gate, w_exp_up, w_exp_down)
        last = layer == depth - 1
        x = _combine(tl, lay, layer, tab, ys, cols, x_mid, mod_seq, mod_tok,
                     g_final.reshape(1, D_MODEL) if last else None)
        ret_p.append(s_p)
        ret_s.append(s_s)
        conv_p.append(buf_p)
        conv_s.append(buf_s)
    y_p, y_s = x
    return (y_p.reshape(bp, tp, D_MODEL), y_s.reshape(bs, ts, D_MODEL),
            jnp.stack(ret_p), jnp.stack(conv_p), jnp.stack(ret_s), jnp.stack(conv_s))
```

```python
import functools

import jax
import jax.numpy as jnp
from jax import lax
from jax.experimental import pallas as pl
from jax.experimental.pallas import tpu as pltpu

f32 = jnp.float32
bf16 = jnp.bfloat16
i32 = jnp.int32

D_MODEL = 1024
D_RET = 512
D_CONV = 512
N_HEADS = 4
HEAD_D = 128
RET_CHUNK = 128
RET_CHUNKS_PER_STEP = 4
ROPE_BASE = 10000.0
CONV_WIDTH = 31
CONV_HALO = CONV_WIDTH - 1
N_EXPERTS = 16
N_GROUPS = 4
GROUP_SIZE = N_EXPERTS // N_GROUPS
N_MOD = 6
EPS = 1e-6
PAST_LEN = 16384
D_IN = 4 * D_RET + 2 * D_CONV

LANES = 128
SUBLANES = 8
TM = 256
BM = 256
CONV_ROWS = 64
SAMPLE_GROUP = 8
HALO_PAD = 32
VMEM_LIMIT = 56 * 1024 * 1024


def _cparams(n_axes, vmem=VMEM_LIMIT):
    return pltpu.CompilerParams(dimension_semantics=("arbitrary",) * n_axes, vmem_limit_bytes=vmem)


def _mod_kernel(c_ref, w_ref, b_ref, o_ref):
    cond = jax.nn.silu(c_ref[...]).astype(bf16)
    o_ref[...] = jnp.dot(cond, w_ref[...].astype(bf16), preferred_element_type=f32) + b_ref[...]


def _modulation(c_all, w_mod, b_mod):
    depth = w_mod.shape[0]
    m = c_all.shape[0]
    return pl.pallas_call(
        _mod_kernel,
        grid=(depth, N_MOD),
        in_specs=[
            pl.BlockSpec((m, D_MODEL), lambda l, j: (0, 0)),
            pl.BlockSpec((None, D_MODEL, D_MODEL), lambda l, j: (l, 0, j)),
            pl.BlockSpec((None, 1, D_MODEL), lambda l, j: (l, 0, j)),
        ],
        out_specs=pl.BlockSpec((None, m, D_MODEL), lambda l, j: (l, 0, j)),
        out_shape=jax.ShapeDtypeStruct((depth, m, N_MOD * D_MODEL), f32),
        compiler_params=_cparams(2),
        name="modulation",
    )(c_all, w_mod, b_mod.reshape(depth, 1, N_MOD * D_MODEL))


class _Tiles:
    def __init__(self, bp, tp, bs, ts):
        self.bp, self.tp, self.bs, self.ts = bp, tp, bs, ts
        self.n_prompt = bp * tp
        self.n_sample = bs * ts
        self.n_tok = self.n_prompt + self.n_sample
        assert tp % TM == 0 and self.n_sample % TM == 0
        self.tiles_per_seq = tp // TM
        self.np_tiles = self.n_prompt // TM
        self.ns_tiles = self.n_sample // TM
        self.n_tiles = self.np_tiles + self.ns_tiles

    def prompt_block(self, i):
        return jnp.minimum(i, self.np_tiles - 1)

    def sample_block(self, i):
        return jnp.maximum(i - self.np_tiles, 0)

    def seq_index(self, i):
        return jnp.minimum(i // self.tiles_per_seq, self.bp - 1)


def _mod_specs(tl, layer, col):
    seq = pl.BlockSpec((None, None, 1, D_MODEL), lambda i: (layer, tl.seq_index(i), 0, col))
    tok = pl.BlockSpec((None, TM, D_MODEL), lambda i: (layer, tl.sample_block(i), col))
    return seq, tok


def _pick(is_sample, tok_ref, seq_ref):
    return jnp.where(is_sample, tok_ref[...], seq_ref[...])


def _rms(x, g):
    return x * lax.rsqrt(jnp.mean(x * x, axis=-1, keepdims=True) + EPS) * g


def _inproj_kernel(*refs, np_tiles, split_x):
    if split_x:
        xp_ref, xs_ref = refs[:2]
        refs = refs[2:]
    else:
        x_ref = refs[0]
        refs = refs[1:]
    (shs_ref, sht_ref, scs_ref, sct_ref, g_ref, w_ref, cos_ref, sin_ref,
     q_ref, k_ref, v_ref, gate_ref, a_ref) = refs
    is_s = pl.program_id(0) >= np_tiles
    x = jnp.where(is_s, xs_ref[...], xp_ref[...]) if split_x else x_ref[...]
    h = _rms(x, g_ref[...]) * (1.0 + _pick(is_s, sct_ref, scs_ref)) + _pick(is_s, sht_ref, shs_ref)
    hb = h.astype(bf16)
    cos = cos_ref[...]
    sin = sin_ref[...]

    def proj(col):
        return jnp.dot(hb, w_ref[:, col * D_RET:(col + 1) * D_RET], preferred_element_type=f32)

    def rope(t):
        outs = []
        for hd in range(N_HEADS):
            th = t[:, hd * HEAD_D:(hd + 1) * HEAD_D]
            outs.append(th * cos + pltpu.roll(th, HEAD_D // 2, 1) * sin)
        return outs

    for hd, qh in enumerate(rope(proj(0))):
        q_ref[:, hd * HEAD_D:(hd + 1) * HEAD_D] = qh
    for hd, kh in enumerate(rope(proj(1))):
        k_ref[:, hd * HEAD_D:(hd + 1) * HEAD_D] = kh * (HEAD_D ** -0.5)
    v_ref[...] = proj(2)
    gate_ref[...] = proj(3)
    a_ref[...] = proj(4) * jax.nn.sigmoid(proj(5))


def _inproj(tl, layer, x, mod_seq, mod_tok, g_norm, w_in_bf, cos_tab, sin_tab):
    split_x = isinstance(x, tuple)
    tok_spec = pl.BlockSpec((TM, D_MODEL), lambda i: (i, 0))
    if split_x:
        x_args = list(x)
        x_specs = [pl.BlockSpec((TM, D_MODEL), lambda i: (tl.prompt_block(i), 0)),
                   pl.BlockSpec((TM, D_MODEL), lambda i: (tl.sample_block(i), 0))]
    else:
        x_args, x_specs = [x], [tok_spec]
    sh_seq, sh_tok = _mod_specs(tl, layer, 0)
    sc_seq, sc_tok = _mod_specs(tl, layer, 1)

    def table_block(i):
        return jnp.where(i < tl.np_tiles, i % tl.tiles_per_seq, tl.tiles_per_seq + tl.sample_block(i))

    tab_spec = pl.BlockSpec((TM, HEAD_D), lambda i: (table_block(i), 0))
    out_spec = pl.BlockSpec((TM, D_RET), lambda i: (i, 0))
    out_sd = jax.ShapeDtypeStruct((tl.n_tok, D_RET), f32)
    return pl.pallas_call(
        functools.partial(_inproj_kernel, np_tiles=tl.np_tiles, split_x=split_x),
        grid=(tl.n_tiles,),
        in_specs=x_specs + [
            sh_seq, sh_tok, sc_seq, sc_tok,
            pl.BlockSpec((None, 1, D_MODEL), lambda i: (layer, 0, 0)),
            pl.BlockSpec((None, D_MODEL, D_IN), lambda i: (layer, 0, 0)),
            tab_spec, tab_spec,
        ],
        out_specs=[out_spec] * 5,
        out_shape=[out_sd] * 5,
        compiler_params=_cparams(1),
        name="inproj",
    )(*x_args, mod_seq, mod_tok, mod_seq, mod_tok, g_norm, w_in_bf, cos_tab, sin_tab)


def _head_norm_gate(o, gn, gate):
    mu = jnp.mean(o, axis=-1, keepdims=True)
    var = jnp.mean(jnp.square(o - mu), axis=-1, keepdims=True)
    return jax.nn.silu(gate) * ((o - mu) * lax.rsqrt(var + EPS) * gn)


def _dot_nt(a, b):
    return lax.dot_general(a, b, (((1,), (1,)), ((), ())), preferred_element_type=f32)


def _dot_tn(a, b):
    return lax.dot_general(a, b, (((0,), (0,)), ((), ())), preferred_element_type=f32)


def _ret_prompt_kernel(q_ref, k_ref, v_ref, gate_ref, dec_ref, qd_ref, kd_ref, cd_ref, gn_ref,
                       o_ref, s_out_ref, s_ref):
    c = pl.program_id(1)

    @pl.when(c == 0)
    def _():
        s_ref[...] = jnp.zeros_like(s_ref)

    for ci in range(RET_CHUNKS_PER_STEP):
        rows = slice(ci * RET_CHUNK, (ci + 1) * RET_CHUNK)
        for hd in range(N_HEADS):
            sl = slice(hd * HEAD_D, (hd + 1) * HEAD_D)
            kh = k_ref[rows, sl]
            qb = q_ref[rows, sl].astype(bf16)
            kb = kh.astype(bf16)
            vb = v_ref[rows, sl].astype(bf16)
            s_old = s_ref[hd]
            scores = _dot_nt(qb, kb) * dec_ref[hd]
            inner = jnp.dot(scores.astype(bf16), vb, preferred_element_type=f32)
            cross = jnp.dot(qb, s_old.astype(bf16), preferred_element_type=f32) * qd_ref[hd]
            s_ref[hd] = s_old * cd_ref[hd] + _dot_tn((kh * kd_ref[hd]).astype(bf16), vb)
            o_ref[rows, sl] = _head_norm_gate(inner + cross, gn_ref[:, sl], gate_ref[rows, sl])

    @pl.when(c == pl.num_programs(1) - 1)
    def _():
        s_out_ref[...] = s_ref[...]


def _decay_tables(chunk, true_len):
    lg = jnp.log(1.0 - 2.0 ** (-5.0 - jnp.arange(N_HEADS, dtype=f32)))
    idx = jnp.arange(chunk, dtype=f32)
    rel = idx[:, None] - idx[None, :]
    decay = jnp.where(rel[None] >= 0, jnp.exp(jnp.maximum(rel, 0.0)[None] * lg[:, None, None]), 0.0)
    q_decay = jnp.exp((idx[None, :] + 1.0) * lg[:, None])
    k_decay = jnp.exp((true_len - 1.0 - idx[None, :]) * lg[:, None])
    c_decay = jnp.exp(true_len * lg)
    return decay, q_decay, k_decay, c_decay


def _retention_prompt(tl, layer, q, k, v, gate, g_ret_gn):
    step_rows = RET_CHUNK * RET_CHUNKS_PER_STEP
    assert tl.tp % step_rows == 0
    n_chunks = tl.tp // step_rows
    decay, q_decay, k_decay, c_decay = _decay_tables(RET_CHUNK, RET_CHUNK)
    bcast = lambda t: jnp.broadcast_to(t[:, :, None], (N_HEADS, RET_CHUNK, HEAD_D))
    cd = jnp.broadcast_to(c_decay[:, None, None], (N_HEADS, 1, HEAD_D))
    tok_spec = pl.BlockSpec((step_rows, D_RET), lambda b, c: (b * n_chunks + c, 0))
    tab_spec = pl.BlockSpec((N_HEADS, RET_CHUNK, HEAD_D), lambda b, c: (0, 0, 0))
    return pl.pallas_call(
        _ret_prompt_kernel,
        grid=(tl.bp, n_chunks),
        in_specs=[tok_spec] * 4 + [tab_spec] * 3 + [
            pl.BlockSpec((N_HEADS, 1, HEAD_D), lambda b, c: (0, 0, 0)),
            pl.BlockSpec((None, 1, D_RET), lambda b, c: (layer, 0, 0)),
        ],
        out_specs=[tok_spec, pl.BlockSpec((None, N_HEADS, HEAD_D, HEAD_D), lambda b, c: (b, 0, 0, 0))],
        out_shape=[jax.ShapeDtypeStruct((tl.n_prompt, D_RET), f32),
                   jax.ShapeDtypeStruct((tl.bp, N_HEADS, HEAD_D, HEAD_D), f32)],
        scratch_shapes=[pltpu.VMEM((N_HEADS, HEAD_D, HEAD_D), f32)],
        compiler_params=_cparams(2),
        name="retention_prompt",
    )(q, k, v, gate, decay, bcast(q_decay), bcast(k_decay), cd, g_ret_gn)


def _ret_sample_kernel(q_ref, k_ref, v_ref, gate_ref, s_in_ref, dec_ref, qd_ref, kd_ref, cd_ref, gn_ref,
                       o_ref, s_out_ref, *, ts):
    seqs_per_tile = SUBLANES // ts
    row = lax.broadcasted_iota(i32, (SUBLANES, HEAD_D), 0)
    for t in range(SAMPLE_GROUP // seqs_per_tile):
        rows = slice(t * SUBLANES, (t + 1) * SUBLANES)
        for hd in range(N_HEADS):
            sl = slice(hd * HEAD_D, (hd + 1) * HEAD_D)
            qh = q_ref[rows, sl]
            kh = k_ref[rows, sl] * kd_ref[hd]
            vb = v_ref[rows, sl].astype(bf16)
            qb = qh.astype(bf16)
            scores = _dot_nt(qb, k_ref[rows, sl].astype(bf16)) * dec_ref[hd]
            o = jnp.dot(scores.astype(bf16), vb, preferred_element_type=f32)
            for s in range(seqs_per_tile):
                b = t * seqs_per_tile + s
                mine = (row >= s * ts) & (row < (s + 1) * ts)
                s_old = s_in_ref[b, hd]
                q_s = jnp.where(mine, qh, 0.0).astype(bf16)
                k_s = jnp.where(mine, kh, 0.0).astype(bf16)
                o = o + jnp.dot(q_s, s_old.astype(bf16), preferred_element_type=f32) * qd_ref[hd]
                s_out_ref[b, hd] = s_old * cd_ref[hd] + _dot_tn(k_s, vb)
            o_ref[rows, sl] = _head_norm_gate(o, gn_ref[:, sl], gate_ref[rows, sl])


def _retention_sample(tl, layer, q, k, v, gate, state_ret, g_ret_gn):
    ts = tl.ts
    assert SUBLANES % ts == 0 and tl.bs % SAMPLE_GROUP == 0
    seqs_per_tile = SUBLANES // ts
    decay, q_decay, k_decay, c_decay = _decay_tables(ts, ts)
    eye = jnp.eye(seqs_per_tile, dtype=f32)
    dec_tile = jnp.einsum("ab,hij->haibj", eye, decay).reshape(N_HEADS, SUBLANES, SUBLANES)
    tile_rows = lambda t: jnp.broadcast_to(jnp.tile(t, (1, seqs_per_tile))[:, :, None],
                                           (N_HEADS, SUBLANES, HEAD_D))
    cd = jnp.broadcast_to(c_decay[:, None, None], (N_HEADS, 1, HEAD_D))
    rows = SAMPLE_GROUP * ts
    first = tl.n_prompt // rows
    tok_spec = pl.BlockSpec((rows, D_RET), lambda i: (first + i, 0))
    const3 = lambda shape: pl.BlockSpec(shape, lambda i: (0, 0, 0))
    st_block = (SAMPLE_GROUP, N_HEADS, HEAD_D, HEAD_D)
    return pl.pallas_call(
        functools.partial(_ret_sample_kernel, ts=ts),
        grid=(tl.bs // SAMPLE_GROUP,),
        in_specs=[tok_spec] * 4 + [
            pl.BlockSpec((None,) + st_block, lambda i: (layer, i, 0, 0, 0)),
            const3((N_HEADS, SUBLANES, SUBLANES)),
            const3((N_HEADS, SUBLANES, HEAD_D)),
            const3((N_HEADS, SUBLANES, HEAD_D)),
            const3((N_HEADS, 1, HEAD_D)),
            pl.BlockSpec((None, 1, D_RET), lambda i: (layer, 0, 0)),
        ],
        out_specs=[pl.BlockSpec((rows, D_RET), lambda i: (i, 0)),
                   pl.BlockSpec(st_block, lambda i: (i, 0, 0, 0))],
        out_shape=[jax.ShapeDtypeStruct((tl.n_sample, D_RET), f32),
                   jax.ShapeDtypeStruct((tl.bs, N_HEADS, HEAD_D, HEAD_D), f32)],
        compiler_params=_cparams(1),
        name="retention_sample",
    )(q, k, v, gate, state_ret, dec_tile, tile_rows(q_decay), tile_rows(k_decay), cd, g_ret_gn)


def _ln_silu(cv, g, b):
    mu = jnp.mean(cv, axis=-1, keepdims=True)
    var = jnp.mean(jnp.square(cv - mu), axis=-1, keepdims=True)
    return jax.nn.silu((cv - mu) * lax.rsqrt(var + EPS) * g + b)


def _conv_taps(window, w_ref, b_ref, n_rows):
    cols = []
    for col in range(D_CONV // LANES):
        lanes = slice(col * LANES, (col + 1) * LANES)
        acc = jnp.broadcast_to(b_ref[:, lanes], (n_rows, LANES))
        for tap in range(CONV_WIDTH):
            acc = acc + window(col, tap) * w_ref[tap:tap + 1, lanes]
        cols.append(acc)
    return jnp.concatenate(cols, axis=-1)


def _conv_prompt_kernel(a_ref, w_ref, b_ref, g_ref, bl_ref, o_ref, buf_ref, full_ref):
    j = pl.program_id(1)
    n_cols = D_CONV // LANES

    @pl.when(j == 0)
    def _():
        full_ref[:, 0:HALO_PAD, :] = jnp.zeros((n_cols, HALO_PAD, LANES), f32)

    @pl.when(j > 0)
    def _():
        full_ref[:, 0:HALO_PAD, :] = full_ref[:, TM:TM + HALO_PAD, :]

    for col in range(n_cols):
        full_ref[col, HALO_PAD:HALO_PAD + TM, :] = a_ref[:, col * LANES:(col + 1) * LANES]
    shift = HALO_PAD - CONV_HALO
    for r0 in range(0, TM, CONV_ROWS):
        cv = _conv_taps(lambda col, tap: full_ref[col, r0 + tap + shift:r0 + tap + shift + CONV_ROWS, :],
                        w_ref, b_ref, CONV_ROWS)
        o_ref[r0:r0 + CONV_ROWS, :] = _ln_silu(cv, g_ref[...], bl_ref[...])

    @pl.when(j == pl.num_programs(1) - 1)
    def _():
        buf_ref[...] = a_ref[TM - CONV_HALO:TM, :]


def _conv_prompt(tl, layer, a, w_conv, b_conv, g_ln, b_ln):
    tps = tl.tiles_per_seq
    vec = pl.BlockSpec((None, 1, D_CONV), lambda b, j: (layer, 0, 0))
    return pl.pallas_call(
        _conv_prompt_kernel,
        grid=(tl.bp, tps),
        in_specs=[pl.BlockSpec((TM, D_CONV), lambda b, j: (b * tps + j, 0)),
                  pl.BlockSpec((None, CONV_WIDTH, D_CONV), lambda b, j: (layer, 0, 0)),
                  vec, vec, vec],
        out_specs=[pl.BlockSpec((TM, D_CONV), lambda b, j: (b * tps + j, 0)),
                   pl.BlockSpec((None, CONV_HALO, D_CONV), lambda b, j: (b, 0, 0))],
        out_shape=[jax.ShapeDtypeStruct((tl.n_prompt, D_CONV), f32),
                   jax.ShapeDtypeStruct((tl.bp, CONV_HALO, D_CONV), f32)],
        scratch_shapes=[pltpu.VMEM((D_CONV // LANES, HALO_PAD + TM, LANES), f32)],
        compiler_params=_cparams(2),
        name="conv_prompt",
    )(a, w_conv, b_conv, g_ln, b_ln)


def _conv_sample_kernel(a_ref, st_ref, w_ref, b_ref, g_ref, bl_ref, o_ref, buf_ref, full_ref, cv_ref, *, ts):
    for s in range(SAMPLE_GROUP):
        for col in range(D_CONV // LANES):
            lanes = slice(col * LANES, (col + 1) * LANES)
            full_ref[col, 0:CONV_HALO, :] = st_ref[s, :, lanes]
            full_ref[col, CONV_HALO:CONV_HALO + ts, :] = a_ref[s * ts:(s + 1) * ts, lanes]
            buf_ref[s, :, lanes] = full_ref[col, ts:ts + CONV_HALO, :]
        cv_ref[s * ts:(s + 1) * ts, :] = _conv_taps(lambda col, tap: full_ref[col, tap:tap + ts, :],
                                                    w_ref, b_ref, ts)
    o_ref[...] = _ln_silu(cv_ref[...], g_ref[...], bl_ref[...])


def _conv_sample(tl, layer, a, state_conv, w_conv, b_conv, g_ln, b_ln):
    ts = tl.ts
    rows = SAMPLE_GROUP * ts
    first = tl.n_prompt // rows
    vec = pl.BlockSpec((None, 1, D_CONV), lambda i: (layer, 0, 0))
    return pl.pallas_call(
        functools.partial(_conv_sample_kernel, ts=ts),
        grid=(tl.bs // SAMPLE_GROUP,),
        in_specs=[pl.BlockSpec((rows, D_CONV), lambda i: (first + i, 0)),
                  pl.BlockSpec((None, SAMPLE_GROUP, CONV_HALO, D_CONV), lambda i: (layer, i, 0, 0)),
                  pl.BlockSpec((None, CONV_WIDTH, D_CONV), lambda i: (layer, 0, 0)),
                  vec, vec, vec],
        out_specs=[pl.BlockSpec((rows, D_CONV), lambda i: (i, 0)),
                   pl.BlockSpec((SAMPLE_GROUP, CONV_HALO, D_CONV), lambda i: (i, 0, 0))],
        out_shape=[jax.ShapeDtypeStruct((tl.n_sample, D_CONV), f32),
                   jax.ShapeDtypeStruct((tl.bs, CONV_HALO, D_CONV), f32)],
        scratch_shapes=[pltpu.VMEM((D_CONV // LANES, CONV_HALO + ts + SUBLANES, LANES), f32),
                        pltpu.VMEM((rows, D_CONV), f32)],
        compiler_params=_cparams(1),
        name="conv_sample",
    )(a, state_conv, w_conv, b_conv, g_ln, b_ln)


SORT_ROWS = 2 * TM + N_EXPERTS * SUBLANES
XS_W = D_MODEL + LANES


def _split3(x):
    a = x.astype(bf16)
    r = x - a.astype(f32)
    b = r.astype(bf16)
    c = (r - b.astype(f32)).astype(bf16)
    return a, b, c


def _first_of4(vals, m):
    return jnp.where(vals[0] == m, 0.0, jnp.where(vals[1] == m, 1.0, jnp.where(vals[2] == m, 2.0, 3.0)))


def _rows_to_tile(rows, n_rows):
    sub = lax.broadcasted_iota(i32, (n_rows, TM), 0)
    out = jnp.zeros((n_rows, TM), f32)
    for r, val in enumerate(rows):
        out = jnp.where(sub == r, val, out)
    return out


def _outproj_kernel(*refs, np_tiles, split_x):
    if split_x:
        xp_ref, xs_ref = refs[:2]
        refs = refs[2:]
    else:
        x_ref = refs[0]
        refs = refs[1:]
    (retp_ref, rets_ref, cvp_ref, cvs_ref, gts_ref, gtt_ref, shs_ref, sht_ref, scs_ref, sct_ref,
     g_ref, wo_ref, wrh_ref, wrl_ref, br_ref,
     xo_ref, h2_ref, rows_ref, cols_ref, cnt_ref) = refs
    is_s = pl.program_id(0) >= np_tiles

    x = jnp.where(is_s, xs_ref[...], xp_ref[...]) if split_x else x_ref[...]
    ret = jnp.where(is_s, rets_ref[...], retp_ref[...]).astype(bf16)
    cv = jnp.where(is_s, cvs_ref[...], cvp_ref[...]).astype(bf16)
    mix_out = (jnp.dot(ret, wo_ref[0:D_RET, :], preferred_element_type=f32)
               + jnp.dot(cv, wo_ref[D_RET:D_RET + D_CONV, :], preferred_element_type=f32))
    xn = x + _pick(is_s, gtt_ref, gts_ref) * mix_out
    xo_ref[...] = xn
    h2 = _rms(xn, g_ref[...]) * (1.0 + _pick(is_s, sct_ref, scs_ref)) + _pick(is_s, sht_ref, shs_ref)
    h_hi = h2.astype(bf16)
    h2_ref[...] = h_hi

    h_lo = (h2 - h_hi.astype(f32)).astype(bf16)
    logits = (jnp.dot(h_hi, wrh_ref[...], preferred_element_type=f32)
              + jnp.dot(h_hi, wrl_ref[...], preferred_element_type=f32)
              + jnp.dot(h_lo, wrh_ref[...], preferred_element_type=f32)) + br_ref[...]
    lt = logits.T
    row = [lt[e:e + 1, :] for e in range(N_EXPERTS)]
    top = functools.reduce(jnp.maximum, row)
    ex = [jnp.exp(r - top) for r in row]
    den = functools.reduce(jnp.add, ex)
    p = [v / den for v in ex]

    best = None
    for g in range(N_GROUPS):
        a = p[g * GROUP_SIZE:(g + 1) * GROUP_SIZE]
        m1 = functools.reduce(jnp.maximum, a)
        i1 = _first_of4(a, m1)
        b = [jnp.where(i1 == float(j), -1.0, a[j]) for j in range(GROUP_SIZE)]
        m2 = functools.reduce(jnp.maximum, b)
        i2 = _first_of4(b, m2)
        cand = (m1 + m2, m1, m2, i1 + float(g * GROUP_SIZE), i2 + float(g * GROUP_SIZE))
        if best is None:
            best = cand
        else:
            take = cand[0] > best[0]
            best = tuple(jnp.where(take, c, o) for c, o in zip(cand, best))
    _, m1, m2, e0, e1 = best
    denom = m1 + m2
    w0 = m1 / denom
    w1 = m2 / denom

    ex_id = lax.broadcasted_iota(i32, (N_EXPERTS, TM), 0).astype(f32)
    sel0 = ex_id == e0
    sel1 = ex_id == e1
    ind = jnp.where(sel0 | sel1, 1.0, 0.0)
    t_r = lax.broadcasted_iota(i32, (TM, TM), 0)
    t_c = lax.broadcasted_iota(i32, (TM, TM), 1)
    earlier = jnp.where(t_r < t_c, 1.0, 0.0).astype(bf16)
    prefix = jnp.dot(ind.astype(bf16), earlier, preferred_element_type=f32)
    cnt = jnp.sum(ind, axis=-1, keepdims=True)
    cnt8 = jnp.floor((cnt + float(SUBLANES - 1)) * (1.0 / SUBLANES)) * float(SUBLANES)
    e_r = lax.broadcasted_iota(i32, (N_EXPERTS, N_EXPERTS), 0)
    e_c = lax.broadcasted_iota(i32, (N_EXPERTS, N_EXPERTS), 1)
    below = jnp.where(e_c < e_r, 1.0, 0.0).astype(bf16)
    seg_off = jnp.dot(below, jnp.broadcast_to(cnt8, (N_EXPERTS, TM)).astype(bf16),
                      preferred_element_type=f32)
    where_to = seg_off + prefix
    pos0 = jnp.sum(jnp.where(sel0, where_to, 0.0), axis=0, keepdims=True)
    pos1 = jnp.sum(jnp.where(sel1, where_to, 0.0), axis=0, keepdims=True)

    w0p = [v.astype(f32) for v in _split3(w0)]
    w1p = [v.astype(f32) for v in _split3(w1)]
    info = [pos0, pos1] + w0p + w1p
    rows_ref[...] = _rows_to_tile(info, SUBLANES)
    cols_ref[...] = _rows_to_tile(info, LANES).T
    cnt_ref[...] = jnp.broadcast_to(cnt, (N_EXPERTS, LANES))


def _outproj(tl, layer, x, ret_p, ret_s, cv_p, cv_s, mod_seq, mod_tok, g_norm, w_out_bf, wr_hi, wr_lo, br_pad):
    split_x = isinstance(x, tuple)
    tok_spec = pl.BlockSpec((TM, D_MODEL), lambda i: (i, 0))
    p_spec = lambda w: pl.BlockSpec((TM, w), lambda i: (tl.prompt_block(i), 0))
    s_spec = lambda w: pl.BlockSpec((TM, w), lambda i: (tl.sample_block(i), 0))
    if split_x:
        x_args, x_specs = list(x), [p_spec(D_MODEL), s_spec(D_MODEL)]
    else:
        x_args, x_specs = [x], [tok_spec]
    mods = []
    for col in (2, 3, 4):
        mods += list(_mod_specs(tl, layer, col))
    wr_spec = pl.BlockSpec((D_MODEL, LANES), lambda i: (0, 0))
    return pl.pallas_call(
        functools.partial(_outproj_kernel, np_tiles=tl.np_tiles, split_x=split_x),
        grid=(tl.n_tiles,),
        in_specs=x_specs + [p_spec(D_RET), s_spec(D_RET), p_spec(D_CONV), s_spec(D_CONV)] + mods + [
            pl.BlockSpec((None, 1, D_MODEL), lambda i: (layer, 0, 0)),
            pl.BlockSpec((None, D_MODEL, D_MODEL), lambda i: (layer, 0, 0)),
            wr_spec, wr_spec,
            pl.BlockSpec((1, LANES), lambda i: (0, 0)),
        ],
        out_specs=[tok_spec, tok_spec,
                   pl.BlockSpec((None, SUBLANES, TM), lambda i: (i, 0, 0)),
                   pl.BlockSpec((TM, LANES), lambda i: (i, 0)),
                   pl.BlockSpec((None, N_EXPERTS, LANES), lambda i: (i, 0, 0))],
        out_shape=[jax.ShapeDtypeStruct((tl.n_tok, D_MODEL), f32),
                   jax.ShapeDtypeStruct((tl.n_tok, D_MODEL), bf16),
                   jax.ShapeDtypeStruct((tl.n_tiles, SUBLANES, TM), f32),
                   jax.ShapeDtypeStruct((tl.n_tok, LANES), f32),
                   jax.ShapeDtypeStruct((tl.n_tiles, N_EXPERTS, LANES), f32)],
        compiler_params=_cparams(1),
        name="outproj_router",
    )(*x_args, ret_p, ret_s, cv_p, cv_s, *([mod_seq, mod_tok] * 3), g_norm, w_out_bf, wr_hi, wr_lo, br_pad)


N_CHUNKS = SORT_ROWS // SUBLANES


class _Layout:
    def __init__(self, n_tiles):
        self.n_tiles = n_tiles
        self.tail_start = n_tiles * N_CHUNKS
        self.tail_n8 = self.tail_start + N_EXPERTS
        worst = 2 * n_tiles * TM + n_tiles * N_EXPERTS * (SUBLANES - 1) + N_EXPERTS * (BM - SUBLANES)
        self.n_blocks = -(-worst // BM)
        self.cap = self.n_blocks * BM
        self.dump = self.cap
        self.xs_rows = self.cap + 2 * SORT_ROWS


def _moe_tables(lay, tile_counts):
    c8 = ((tile_counts.astype(i32) + SUBLANES - 1) // SUBLANES) * SUBLANES
    base8 = jnp.cumsum(c8, axis=0) - c8
    tot8 = jnp.sum(c8, axis=0)
    region = ((tot8 + BM - 1) // BM) * BM
    g_end = jnp.cumsum(region)
    g_start = g_end - region
    seg_end = jnp.cumsum(c8, axis=1)
    seg_dst = g_start[None, :] + base8
    n_used = g_end[-1] // BM
    blk = jnp.arange(lay.n_blocks, dtype=i32)
    block_e = jnp.minimum(jnp.sum((g_end[None, :] <= blk[:, None] * BM).astype(i32), axis=1), N_EXPERTS - 1)
    block_e = jnp.where(blk < n_used, block_e, block_e[n_used - 1])
    row0 = jnp.arange(N_CHUNKS, dtype=i32) * SUBLANES
    owner = jnp.sum((seg_end[:, None, :] <= row0[None, :, None]).astype(i32), axis=-1)
    onehot = (owner[:, :, None] == jnp.arange(N_EXPERTS, dtype=i32)[None, None, :]).astype(i32)
    delta = seg_dst - (seg_end - c8)
    chunk_dst = jnp.where(owner < N_EXPERTS, row0[None, :] + jnp.sum(onehot * delta[:, None, :], axis=-1), -1)
    tab = jnp.concatenate([chunk_dst.ravel(), g_start + tot8, (region - tot8) // SUBLANES]).astype(i32)
    ids = jnp.arange(N_EXPERTS, dtype=i32)
    later = jnp.where((ids[None, :] > ids[:, None]) & (region[None, :] > 0), ids[None, :], N_EXPERTS)
    next_e = jnp.min(later, axis=1)
    next_e = jnp.where(next_e == N_EXPERTS, -1, next_e)
    return tab, jnp.concatenate([block_e, n_used[None], next_e]).astype(i32)


def _for_chunks(n, fn):
    def body(c, carry):
        fn(c)
        return carry

    lax.fori_loop(0, n, body, 0)


def _dispatch_kernel(tab_ref, h2_ref, rows_ref, cols_ref, xs_hbm, sorted_ref, zero_ref, sem, *, lay):
    i = pl.program_id(0)
    n_tiles = lay.n_tiles
    slot = i % 2

    def tail_copy(dst):
        return pltpu.make_async_copy(zero_ref, xs_hbm.at[pl.ds(dst, SUBLANES)], sem.at[2])

    def wait_tile(slot_):
        pltpu.make_async_copy(sorted_ref.at[slot_], xs_hbm.at[pl.ds(0, SORT_ROWS)], sem.at[slot_]).wait()

    @pl.when(i == 0)
    def _():
        zero_ref[...] = jnp.zeros_like(zero_ref)
        for e in range(N_EXPERTS):
            start = tab_ref[lay.tail_start + e]
            _for_chunks(tab_ref[lay.tail_n8 + e],
                        lambda c: tail_copy(pl.multiple_of(start + c * SUBLANES, SUBLANES)).start())

    @pl.when(i >= 2)
    def _():
        wait_tile(slot)

    pos0 = rows_ref[0:1, :]
    pos1 = rows_ref[1:2, :]
    r_id = lax.broadcasted_iota(i32, (SORT_ROWS, TM), 0).astype(f32)
    p0 = r_id == pos0
    p1 = r_id == pos1
    perm = jnp.where(p0 | p1, 1.0, 0.0).astype(bf16)
    lane = lax.broadcasted_iota(i32, (TM, LANES), 1)
    cols = cols_ref[...]
    wpart0 = jnp.where((lane >= 2) & (lane < 5), cols, 0.0).astype(bf16)
    wpart1 = jnp.where((lane >= 5) & (lane < 8), cols, 0.0).astype(bf16)
    sw = (jnp.dot(jnp.where(p0, 1.0, 0.0).astype(bf16), wpart0, preferred_element_type=f32)
          + jnp.dot(jnp.where(p1, 1.0, 0.0).astype(bf16), wpart1, preferred_element_type=f32))
    sorted_ref[slot, :, 0:D_MODEL] = jnp.dot(perm, h2_ref[...], preferred_element_type=f32)
    sorted_ref[slot, :, D_MODEL:XS_W] = jnp.broadcast_to(jnp.sum(sw, axis=-1, keepdims=True), (SORT_ROWS, LANES))

    for c in range(N_CHUNKS):
        dst = tab_ref[i * N_CHUNKS + c]
        dst = jnp.where(dst < 0, lay.dump + slot * SORT_ROWS + c * SUBLANES, dst)
        pltpu.make_async_copy(sorted_ref.at[slot, pl.ds(c * SUBLANES, SUBLANES)],
                              xs_hbm.at[pl.ds(pl.multiple_of(dst, SUBLANES), SUBLANES)], sem.at[slot]).start()

    @pl.when(i == n_tiles - 1)
    def _():
        if n_tiles >= 2:
            wait_tile(1 - slot)
        wait_tile(slot)
        for e in range(N_EXPERTS):
            _for_chunks(tab_ref[lay.tail_n8 + e], lambda c: tail_copy(0).wait())


def _dispatch(tl, lay, tab, h2, rows, cols):
    grid_spec = pltpu.PrefetchScalarGridSpec(
        num_scalar_prefetch=1,
        grid=(tl.n_tiles,),
        in_specs=[pl.BlockSpec((TM, D_MODEL), lambda i, t: (i, 0)),
                  pl.BlockSpec((None, SUBLANES, TM), lambda i, t: (i, 0, 0)),
                  pl.BlockSpec((TM, LANES), lambda i, t: (i, 0))],
        out_specs=pl.BlockSpec(memory_space=pl.ANY),
        scratch_shapes=[pltpu.VMEM((2, SORT_ROWS, XS_W), f32), pltpu.VMEM((SUBLANES, XS_W), f32),
                        pltpu.SemaphoreType.DMA((3,))],
    )
    return pl.pallas_call(
        functools.partial(_dispatch_kernel, lay=lay),
        grid_spec=grid_spec,
        out_shape=jax.ShapeDtypeStruct((lay.xs_rows, XS_W), f32),
        compiler_params=_cparams(1),
        name="moe_dispatch",
    )(tab, h2, rows, cols)


def _expert_kernel(be_ref, xs_ref, wg_hbm, wu_hbm, wd_hbm, ys_ref, stage, w_bf, sem, *, n_blocks, layer):
    j = pl.program_id(0)

    def fetch(e):
        return [pltpu.make_async_copy(w.at[layer, e], stage.at[k], sem.at[k])
                for k, w in enumerate((wg_hbm, wu_hbm, wd_hbm))]

    @pl.when(j < be_ref[n_blocks])
    def _():
        e = be_ref[j]

        @pl.when(j == 0)
        def _():
            for copy in fetch(e):
                copy.start()

        @pl.when((j == 0) | (e != be_ref[jnp.maximum(j - 1, 0)]))
        def _():
            for copy in fetch(e):
                copy.wait()
            for k in range(3):
                w_bf[k] = stage[k].astype(bf16)
            nxt = be_ref[n_blocks + 1 + e]

            @pl.when(nxt >= 0)
            def _():
                for copy in fetch(nxt):
                    copy.start()

        x = xs_ref[:, 0:D_MODEL].astype(bf16)
        gate = jnp.dot(x, w_bf[0], preferred_element_type=f32)
        up = jnp.dot(x, w_bf[1], preferred_element_type=f32)
        mid = (jax.nn.silu(gate) * up).astype(bf16)
        ys_ref[...] = jnp.dot(mid, w_bf[2], preferred_element_type=f32) * xs_ref[:, D_MODEL:D_MODEL + 1]


def _experts(layer, lay, block_e, xs, w_gate, w_up, w_down):
    n_blocks = lay.n_blocks
    d_ff = w_gate.shape[-1]
    assert d_ff == D_MODEL
    used = lambda j, be: jnp.minimum(j, be[n_blocks] - 1)
    any_spec = pl.BlockSpec(memory_space=pl.ANY)
    grid_spec = pltpu.PrefetchScalarGridSpec(
        num_scalar_prefetch=1,
        grid=(n_blocks,),
        in_specs=[pl.BlockSpec((BM, XS_W), lambda j, be: (used(j, be), 0)), any_spec, any_spec, any_spec],
        out_specs=pl.BlockSpec((BM, D_MODEL), lambda j, be: (used(j, be), 0)),
        scratch_shapes=[pltpu.VMEM((3, D_MODEL, d_ff), f32), pltpu.VMEM((3, D_MODEL, d_ff), bf16),
                        pltpu.SemaphoreType.DMA((3,))],
    )
    return pl.pallas_call(
        functools.partial(_expert_kernel, n_blocks=n_blocks, layer=layer),
        grid_spec=grid_spec,
        out_shape=jax.ShapeDtypeStruct((lay.cap, D_MODEL), f32),
        compiler_params=_cparams(1),
        name="moe_experts",
    )(block_e, xs, w_gate, w_up, w_down)


def _combine_kernel(tab_ref, ys_hbm, cols_ref, x_ref, gts_ref, gtt_ref, *rest, lay, np_tiles, final):
    if final:
        gf_ref, yp_ref, ysm_ref, staged, sem = rest
    else:
        xo_ref, staged, sem = rest
    i = pl.program_id(0)
    n_tiles = lay.n_tiles
    slot = i % 2
    is_s = i >= np_tiles

    def start_tile(tile, slot_):
        for c in range(N_CHUNKS):
            src = jnp.maximum(tab_ref[tile * N_CHUNKS + c], 0)
            pltpu.make_async_copy(ys_hbm.at[pl.ds(pl.multiple_of(src, SUBLANES), SUBLANES)],
                                  staged.at[slot_, pl.ds(c * SUBLANES, SUBLANES)], sem.at[slot_]).start()

    @pl.when(i == 0)
    def _():
        start_tile(0, 0)

    @pl.when(i + 1 < n_tiles)
    def _():
        start_tile(i + 1, 1 - slot)

    pltpu.make_async_copy(ys_hbm.at[pl.ds(0, SORT_ROWS)], staged.at[slot], sem.at[slot]).wait()

    lane = lax.broadcasted_iota(i32, (TM, SORT_ROWS), 1).astype(f32)
    unperm = jnp.where((lane == cols_ref[:, 0:1]) | (lane == cols_ref[:, 1:2]), 1.0, 0.0).astype(bf16)
    ff = sum(jnp.dot(unperm, part, preferred_element_type=f32) for part in _split3(staged[slot]))
    xn = x_ref[...] + _pick(is_s, gtt_ref, gts_ref) * ff
    if final:
        y = _rms(xn, gf_ref[...])

        @pl.when(jnp.logical_not(is_s))
        def _():
            yp_ref[...] = y

        @pl.when(is_s)
        def _():
            ysm_ref[...] = y
    else:
        xo_ref[...] = xn


def _combine(tl, lay, layer, tab, ys, cols, x, mod_seq, mod_tok, g_final):
    final = g_final is not None
    tok_spec = pl.BlockSpec((TM, D_MODEL), lambda i, t: (i, 0))
    gt_seq = pl.BlockSpec((None, None, 1, D_MODEL), lambda i, t: (layer, tl.seq_index(i), 0, 5))
    gt_tok = pl.BlockSpec((None, TM, D_MODEL), lambda i, t: (layer, tl.sample_block(i), 5))
    in_specs = [pl.BlockSpec(memory_space=pl.ANY), pl.BlockSpec((TM, LANES), lambda i, t: (i, 0)),
                tok_spec, gt_seq, gt_tok]
    args = [tab, ys, cols, x, mod_seq, mod_tok]
    if final:
        in_specs.append(pl.BlockSpec((1, D_MODEL), lambda i, t: (0, 0)))
        args.append(g_final)
        out_specs = [pl.BlockSpec((TM, D_MODEL), lambda i, t: (tl.prompt_block(i), 0)),
                     pl.BlockSpec((TM, D_MODEL), lambda i, t: (tl.sample_block(i), 0))]
        out_shape = [jax.ShapeDtypeStruct((tl.n_prompt, D_MODEL), f32),
                     jax.ShapeDtypeStruct((tl.n_sample, D_MODEL), f32)]
    else:
        out_specs = tok_spec
        out_shape = jax.ShapeDtypeStruct((tl.n_tok, D_MODEL), f32)
    grid_spec = pltpu.PrefetchScalarGridSpec(
        num_scalar_prefetch=1,
        grid=(tl.n_tiles,),
        in_specs=in_specs,
        out_specs=out_specs,
        scratch_shapes=[pltpu.VMEM((2, SORT_ROWS, D_MODEL), f32), pltpu.SemaphoreType.DMA((2,))],
    )
    return pl.pallas_call(
        functools.partial(_combine_kernel, lay=lay, np_tiles=tl.np_tiles, final=final),
        grid_spec=grid_spec,
        out_shape=out_shape,
        compiler_params=_cparams(1),
        name="moe_combine",
    )(*args)


def _rope_tables(tl):
    half = HEAD_D // 2
    inv = ROPE_BASE ** (-jnp.arange(half, dtype=f32) / half)
    pos_p = jnp.arange(tl.tp, dtype=i32)
    pos_s = PAST_LEN + jnp.arange(tl.ts, dtype=i32)
    pos = jnp.concatenate([pos_p, jnp.tile(pos_s, tl.bs)])
    ang = pos.astype(f32)[:, None] * inv[None, :]
    cos, sin = jnp.cos(ang), jnp.sin(ang)
    return jnp.concatenate([cos, cos], axis=-1), jnp.concatenate([-sin, sin], axis=-1)


def kernel(x_prompt, x_sample, state_ret, state_conv, c_prompt, c_sample, w_mod, b_mod, g_mix_norm, w_in,
           w_conv, b_conv, g_conv_ln, b_conv_ln, g_ret_gn, w_out, g_ffn_norm, w_router, b_router,
           w_exp_gate, w_exp_up, w_exp_down, g_final):
    bp, tp, _ = x_prompt.shape
    bs, ts, _ = x_sample.shape
    depth = w_mod.shape[0]
    tl = _Tiles(bp, tp, bs, ts)
    lay = _Layout(tl.n_tiles)

    c_all = jnp.concatenate([c_prompt, jnp.repeat(c_sample, ts, axis=0)], axis=0)
    mod = _modulation(c_all, w_mod, b_mod)
    mod_seq = mod[:, :bp].reshape(depth, bp, 1, N_MOD * D_MODEL)
    mod_tok = mod[:, bp:]

    cos_tab, sin_tab = _rope_tables(tl)
    w_in_bf = w_in.astype(bf16)
    w_out_bf = w_out.astype(bf16)
    wr_pad = jnp.pad(w_router.astype(f32), ((0, 0), (0, LANES - N_EXPERTS)))
    wr_hi = wr_pad.astype(bf16)
    wr_lo = (wr_pad - wr_hi.astype(f32)).astype(bf16)
    br_pad = jnp.pad(b_router.astype(f32), (0, LANES - N_EXPERTS)).reshape(1, LANES)
    vec3 = lambda t: t.reshape(depth, 1, t.shape[-1])
    g_mix3, g_ffn3, gn3 = vec3(g_mix_norm), vec3(g_ffn_norm), vec3(g_ret_gn)
    b_conv3, g_ln3, b_ln3 = vec3(b_conv), vec3(g_conv_ln), vec3(b_conv_ln)

    x = (x_prompt.reshape(tl.n_prompt, D_MODEL), x_sample.reshape(tl.n_sample, D_MODEL))
    ret_p, conv_p, ret_s, conv_s = [], [], [], []
    for layer in range(depth):
        q, k, v, gate, a = _inproj(tl, layer, x, mod_seq, mod_tok, g_mix3, w_in_bf, cos_tab, sin_tab)
        ro_p, s_p = _retention_prompt(tl, layer, q, k, v, gate, gn3)
        ro_s, s_s = _retention_sample(tl, layer, q, k, v, gate, state_ret, gn3)
        co_p, buf_p = _conv_prompt(tl, layer, a, w_conv, b_conv3, g_ln3, b_ln3)
        co_s, buf_s = _conv_sample(tl, layer, a, state_conv, w_conv, b_conv3, g_ln3, b_ln3)
        x_mid, h2, rows, cols, tile_counts = _outproj(
            tl, layer, x, ro_p, ro_s, co_p, co_s, mod_seq, mod_tok, g_ffn3, w_out_bf, wr_hi, wr_lo, br_pad)
        tab, block_e = _moe_tables(lay, tile_counts[:, :, 0])
        xs = _dispatch(tl, lay, tab, h2, rows, cols)
        ys = _experts(layer, lay, block_e, xs, w_exp_gate, w_exp_up, w_exp_down)
        last = layer == depth - 1
        x = _combine(tl, lay, layer, tab, ys, cols, x_mid, mod_seq, mod_tok,
                     g_final.reshape(1, D_MODEL) if last else None)
        ret_p.append(s_p)
        ret_s.append(s_s)
        conv_p.append(buf_p)
        conv_s.append(buf_s)
    y_p, y_s = x
    return (y_p.reshape(bp, tp, D_MODEL), y_s.reshape(bs, ts, D_MODEL),
            jnp.stack(ret_p), jnp.stack(conv_p), jnp.stack(ret_s), jnp.stack(conv_s))
```

```python
import functools

import jax
import jax.numpy as jnp
from jax import lax
from jax.experimental import pallas as pl
from jax.experimental.pallas import tpu as pltpu

f32 = jnp.float32
bf16 = jnp.bfloat16
i32 = jnp.int32

D_MODEL = 1024
D_RET = 512
D_CONV = 512
N_HEADS = 4
HEAD_D = 128
RET_CHUNK = 128
RET_CHUNKS_PER_STEP = 4
ROPE_BASE = 10000.0
CONV_WIDTH = 31
CONV_HALO = CONV_WIDTH - 1
N_EXPERTS = 16
N_GROUPS = 4
GROUP_SIZE = N_EXPERTS // N_GROUPS
N_MOD = 6
EPS = 1e-6
PAST_LEN = 16384
D_IN = 4 * D_RET + 2 * D_CONV

LANES = 128
SUBLANES = 8
TM = 256
BM = 512
CONV_ROWS = 128
NORM_ROWS = 64
SAMPLE_GROUP = 8
HALO_PAD = 32
VMEM_LIMIT = 56 * 1024 * 1024


def _cparams(n_axes, vmem=VMEM_LIMIT):
    return pltpu.CompilerParams(dimension_semantics=("arbitrary",) * n_axes, vmem_limit_bytes=vmem)


def _mod_kernel(c_ref, w_ref, b_ref, o_ref):
    cond = jax.nn.silu(c_ref[...]).astype(bf16)
    o_ref[...] = jnp.dot(cond, w_ref[...].astype(bf16), preferred_element_type=f32) + b_ref[...]


def _modulation(c_all, w_mod, b_mod):
    depth = w_mod.shape[0]
    m = c_all.shape[0]
    return pl.pallas_call(
        _mod_kernel,
        grid=(depth, N_MOD),
        in_specs=[
            pl.BlockSpec((m, D_MODEL), lambda l, j: (0, 0)),
            pl.BlockSpec((None, D_MODEL, D_MODEL), lambda l, j: (l, 0, j)),
            pl.BlockSpec((None, 1, D_MODEL), lambda l, j: (l, 0, j)),
        ],
        out_specs=pl.BlockSpec((None, m, D_MODEL), lambda l, j: (l, 0, j)),
        out_shape=jax.ShapeDtypeStruct((depth, m, N_MOD * D_MODEL), f32),
        compiler_params=_cparams(2),
        name="modulation",
    )(c_all, w_mod, b_mod.reshape(depth, 1, N_MOD * D_MODEL))


class _Tiles:
    def __init__(self, bp, tp, bs, ts):
        self.bp, self.tp, self.bs, self.ts = bp, tp, bs, ts
        self.n_prompt = bp * tp
        self.n_sample = bs * ts
        self.n_tok = self.n_prompt + self.n_sample
        assert tp % TM == 0 and self.n_sample % TM == 0
        self.tiles_per_seq = tp // TM
        self.np_tiles = self.n_prompt // TM
        self.ns_tiles = self.n_sample // TM
        self.n_tiles = self.np_tiles + self.ns_tiles

    def prompt_block(self, i):
        return jnp.minimum(i, self.np_tiles - 1)

    def sample_block(self, i):
        return jnp.maximum(i - self.np_tiles, 0)

    def seq_index(self, i):
        return jnp.minimum(i // self.tiles_per_seq, self.bp - 1)


def _mod_specs(tl, layer, col):
    seq = pl.BlockSpec((None, None, 1, D_MODEL), lambda i: (layer, tl.seq_index(i), 0, col))
    tok = pl.BlockSpec((None, TM, D_MODEL), lambda i: (layer, tl.sample_block(i), col))
    return seq, tok


def _pick(is_sample, tok_ref, seq_ref):
    return jnp.where(is_sample, tok_ref[...], seq_ref[...])


def _rms(x, g):
    return x * lax.rsqrt(jnp.mean(x * x, axis=-1, keepdims=True) + EPS) * g


def _inproj_kernel(*refs, np_tiles, split_x):
    if split_x:
        xp_ref, xs_ref = refs[:2]
        refs = refs[2:]
    else:
        x_ref = refs[0]
        refs = refs[1:]
    (shs_ref, sht_ref, scs_ref, sct_ref, g_ref, w_ref, cos_ref, sin_ref,
     q_ref, k_ref, v_ref, gate_ref, a_ref) = refs
    is_s = pl.program_id(0) >= np_tiles
    x = jnp.where(is_s, xs_ref[...], xp_ref[...]) if split_x else x_ref[...]
    h = _rms(x, g_ref[...]) * (1.0 + _pick(is_s, sct_ref, scs_ref)) + _pick(is_s, sht_ref, shs_ref)
    hb = h.astype(bf16)
    cos = cos_ref[...]
    sin = sin_ref[...]

    def proj(col):
        return jnp.dot(hb, w_ref[:, col * D_RET:(col + 1) * D_RET], preferred_element_type=f32)

    def rope(t):
        outs = []
        for hd in range(N_HEADS):
            th = t[:, hd * HEAD_D:(hd + 1) * HEAD_D]
            outs.append(th * cos + pltpu.roll(th, HEAD_D // 2, 1) * sin)
        return outs

    for hd, qh in enumerate(rope(proj(0))):
        q_ref[:, hd * HEAD_D:(hd + 1) * HEAD_D] = qh
    for hd, kh in enumerate(rope(proj(1))):
        k_ref[:, hd * HEAD_D:(hd + 1) * HEAD_D] = kh * (HEAD_D ** -0.5)
    v_ref[...] = proj(2)
    gate_ref[...] = proj(3)
    a_ref[...] = proj(4) * jax.nn.sigmoid(proj(5))


def _inproj(tl, layer, x, mod_seq, mod_tok, g_norm, w_in_bf, cos_tab, sin_tab):
    split_x = isinstance(x, tuple)
    tok_spec = pl.BlockSpec((TM, D_MODEL), lambda i: (i, 0))
    if split_x:
        x_args = list(x)
        x_specs = [pl.BlockSpec((TM, D_MODEL), lambda i: (tl.prompt_block(i), 0)),
                   pl.BlockSpec((TM, D_MODEL), lambda i: (tl.sample_block(i), 0))]
    else:
        x_args, x_specs = [x], [tok_spec]
    sh_seq, sh_tok = _mod_specs(tl, layer, 0)
    sc_seq, sc_tok = _mod_specs(tl, layer, 1)

    def table_block(i):
        return jnp.where(i < tl.np_tiles, i % tl.tiles_per_seq, tl.tiles_per_seq + tl.sample_block(i))

    tab_spec = pl.BlockSpec((TM, HEAD_D), lambda i: (table_block(i), 0))
    out_spec = pl.BlockSpec((TM, D_RET), lambda i: (i, 0))
    out_sd = jax.ShapeDtypeStruct((tl.n_tok, D_RET), f32)
    return pl.pallas_call(
        functools.partial(_inproj_kernel, np_tiles=tl.np_tiles, split_x=split_x),
        grid=(tl.n_tiles,),
        in_specs=x_specs + [
            sh_seq, sh_tok, sc_seq, sc_tok,
            pl.BlockSpec((None, 1, D_MODEL), lambda i: (layer, 0, 0)),
            pl.BlockSpec((None, D_MODEL, D_IN), lambda i: (layer, 0, 0)),
            tab_spec, tab_spec,
        ],
        out_specs=[out_spec] * 5,
        out_shape=[out_sd] * 5,
        compiler_params=_cparams(1),
        name="inproj",
    )(*x_args, mod_seq, mod_tok, mod_seq, mod_tok, g_norm, w_in_bf, cos_tab, sin_tab)


def _head_norm_gate(o, gn, gate):
    mu = jnp.mean(o, axis=-1, keepdims=True)
    var = jnp.mean(jnp.square(o - mu), axis=-1, keepdims=True)
    return jax.nn.silu(gate) * ((o - mu) * lax.rsqrt(var + EPS) * gn)


def _dot_nt(a, b):
    return lax.dot_general(a, b, (((1,), (1,)), ((), ())), preferred_element_type=f32)


def _dot_tn(a, b):
    return lax.dot_general(a, b, (((0,), (0,)), ((), ())), preferred_element_type=f32)


def _ret_prompt_kernel(q_ref, k_ref, v_ref, gate_ref, dec_ref, qd_ref, kd_ref, cd_ref, gn_ref,
                       o_ref, s_out_ref, s_ref):
    c = pl.program_id(1)

    @pl.when(c == 0)
    def _():
        s_ref[...] = jnp.zeros_like(s_ref)

    for ci in range(RET_CHUNKS_PER_STEP):
        rows = slice(ci * RET_CHUNK, (ci + 1) * RET_CHUNK)
        for hd in range(N_HEADS):
            sl = slice(hd * HEAD_D, (hd + 1) * HEAD_D)
            kh = k_ref[rows, sl]
            qb = q_ref[rows, sl].astype(bf16)
            kb = kh.astype(bf16)
            vb = v_ref[rows, sl].astype(bf16)
            s_old = s_ref[hd]
            scores = _dot_nt(qb, kb) * dec_ref[hd]
            inner = jnp.dot(scores.astype(bf16), vb, preferred_element_type=f32)
            cross = jnp.dot(qb, s_old.astype(bf16), preferred_element_type=f32) * qd_ref[hd]
            s_ref[hd] = s_old * cd_ref[hd] + _dot_tn((kh * kd_ref[hd]).astype(bf16), vb)
            o_ref[rows, sl] = _head_norm_gate(inner + cross, gn_ref[:, sl], gate_ref[rows, sl])

    @pl.when(c == pl.num_programs(1) - 1)
    def _():
        s_out_ref[...] = s_ref[...]


def _decay_tables(chunk, true_len):
    lg = jnp.log(1.0 - 2.0 ** (-5.0 - jnp.arange(N_HEADS, dtype=f32)))
    idx = jnp.arange(chunk, dtype=f32)
    rel = idx[:, None] - idx[None, :]
    decay = jnp.where(rel[None] >= 0, jnp.exp(jnp.maximum(rel, 0.0)[None] * lg[:, None, None]), 0.0)
    q_decay = jnp.exp((idx[None, :] + 1.0) * lg[:, None])
    k_decay = jnp.exp((true_len - 1.0 - idx[None, :]) * lg[:, None])
    c_decay = jnp.exp(true_len * lg)
    return decay, q_decay, k_decay, c_decay


def _retention_prompt(tl, layer, q, k, v, gate, g_ret_gn):
    step_rows = RET_CHUNK * RET_CHUNKS_PER_STEP
    assert tl.tp % step_rows == 0
    n_chunks = tl.tp // step_rows
    decay, q_decay, k_decay, c_decay = _decay_tables(RET_CHUNK, RET_CHUNK)
    bcast = lambda t: jnp.broadcast_to(t[:, :, None], (N_HEADS, RET_CHUNK, HEAD_D))
    cd = jnp.broadcast_to(c_decay[:, None, None], (N_HEADS, 1, HEAD_D))
    tok_spec = pl.BlockSpec((step_rows, D_RET), lambda b, c: (b * n_chunks + c, 0))
    tab_spec = pl.BlockSpec((N_HEADS, RET_CHUNK, HEAD_D), lambda b, c: (0, 0, 0))
    return pl.pallas_call(
        _ret_prompt_kernel,
        grid=(tl.bp, n_chunks),
        in_specs=[tok_spec] * 4 + [tab_spec] * 3 + [
            pl.BlockSpec((N_HEADS, 1, HEAD_D), lambda b, c: (0, 0, 0)),
            pl.BlockSpec((None, 1, D_RET), lambda b, c: (layer, 0, 0)),
        ],
        out_specs=[tok_spec, pl.BlockSpec((None, N_HEADS, HEAD_D, HEAD_D), lambda b, c: (b, 0, 0, 0))],
        out_shape=[jax.ShapeDtypeStruct((tl.n_prompt, D_RET), f32),
                   jax.ShapeDtypeStruct((tl.bp, N_HEADS, HEAD_D, HEAD_D), f32)],
        scratch_shapes=[pltpu.VMEM((N_HEADS, HEAD_D, HEAD_D), f32)],
        compiler_params=_cparams(2),
        name="retention_prompt",
    )(q, k, v, gate, decay, bcast(q_decay), bcast(k_decay), cd, g_ret_gn)


def _ret_sample_kernel(q_ref, k_ref, v_ref, gate_ref, s_in_ref, dec_ref, qd_ref, kd_ref, cd_ref, gn_ref,
                       *rest, ts):
    o_ref, s_all_ref = rest[-2:]
    s_out_ref = s_all_ref.at[0]
    for other in range(1, s_all_ref.shape[0]):
        s_all_ref[other] = jnp.zeros(s_all_ref.shape[1:], f32)
    seqs_per_tile = SUBLANES // ts
    row = lax.broadcasted_iota(i32, (SUBLANES, HEAD_D), 0)
    for t in range(SAMPLE_GROUP // seqs_per_tile):
        rows = slice(t * SUBLANES, (t + 1) * SUBLANES)
        for hd in range(N_HEADS):
            sl = slice(hd * HEAD_D, (hd + 1) * HEAD_D)
            qh = q_ref[rows, sl]
            kh = k_ref[rows, sl] * kd_ref[hd]
            vb = v_ref[rows, sl].astype(bf16)
            qb = qh.astype(bf16)
            scores = _dot_nt(qb, k_ref[rows, sl].astype(bf16)) * dec_ref[hd]
            o = jnp.dot(scores.astype(bf16), vb, preferred_element_type=f32)
            for s in range(seqs_per_tile):
                b = t * seqs_per_tile + s
                mine = (row >= s * ts) & (row < (s + 1) * ts)
                s_old = s_in_ref[b, hd]
                q_s = jnp.where(mine, qh, 0.0).astype(bf16)
                k_s = jnp.where(mine, kh, 0.0).astype(bf16)
                o = o + jnp.dot(q_s, s_old.astype(bf16), preferred_element_type=f32) * qd_ref[hd]
                s_out_ref[b, hd] = s_old * cd_ref[hd] + _dot_tn(k_s, vb)
            o_ref[rows, sl] = _head_norm_gate(o, gn_ref[:, sl], gate_ref[rows, sl])


def _retention_sample(tl, layer, q, k, v, gate, state_ret, g_ret_gn, prev_states):
    ts = tl.ts
    depth = state_ret.shape[0]
    assert SUBLANES % ts == 0 and tl.bs % SAMPLE_GROUP == 0
    seqs_per_tile = SUBLANES // ts
    decay, q_decay, k_decay, c_decay = _decay_tables(ts, ts)
    eye = jnp.eye(seqs_per_tile, dtype=f32)
    dec_tile = jnp.einsum("ab,hij->haibj", eye, decay).reshape(N_HEADS, SUBLANES, SUBLANES)
    tile_rows = lambda t: jnp.broadcast_to(jnp.tile(t, (1, seqs_per_tile))[:, :, None],
                                           (N_HEADS, SUBLANES, HEAD_D))
    cd = jnp.broadcast_to(c_decay[:, None, None], (N_HEADS, 1, HEAD_D))
    rows = SAMPLE_GROUP * ts
    first = tl.n_prompt // rows
    tok_spec = pl.BlockSpec((rows, D_RET), lambda i: (first + i, 0))
    const3 = lambda shape: pl.BlockSpec(shape, lambda i: (0, 0, 0))
    st_block = (SAMPLE_GROUP, N_HEADS, HEAD_D, HEAD_D)
    in_specs = [tok_spec] * 4 + [
        pl.BlockSpec((None,) + st_block, lambda i: (layer, i, 0, 0, 0)),
        const3((N_HEADS, SUBLANES, SUBLANES)),
        const3((N_HEADS, SUBLANES, HEAD_D)),
        const3((N_HEADS, SUBLANES, HEAD_D)),
        const3((N_HEADS, 1, HEAD_D)),
        pl.BlockSpec((None, 1, D_RET), lambda i: (layer, 0, 0)),
    ]
    args = [q, k, v, gate, state_ret, dec_tile, tile_rows(q_decay), tile_rows(k_decay), cd, g_ret_gn]
    if prev_states is None:
        state_spec = pl.BlockSpec((depth,) + st_block, lambda i: (0, i, 0, 0, 0))
        aliases = {}
    else:
        state_spec = pl.BlockSpec((1,) + st_block, lambda i: (layer, i, 0, 0, 0))
        in_specs.append(pl.BlockSpec(memory_space=pl.ANY))
        args.append(prev_states)
        aliases = {len(args) - 1: 1}
    return pl.pallas_call(
        functools.partial(_ret_sample_kernel, ts=ts),
        grid=(tl.bs // SAMPLE_GROUP,),
        in_specs=in_specs,
        out_specs=[pl.BlockSpec((rows, D_RET), lambda i: (i, 0)), state_spec],
        out_shape=[jax.ShapeDtypeStruct((tl.n_sample, D_RET), f32),
                   jax.ShapeDtypeStruct((depth, tl.bs, N_HEADS, HEAD_D, HEAD_D), f32)],
        input_output_aliases=aliases,
        compiler_params=_cparams(1),
        name="retention_sample",
    )(*args)


def _ln_silu(cv, g, b):
    mu = jnp.mean(cv, axis=-1, keepdims=True)
    var = jnp.mean(jnp.square(cv - mu), axis=-1, keepdims=True)
    return jax.nn.silu((cv - mu) * lax.rsqrt(var + EPS) * g + b)


def _conv_taps(window, w_ref, b_ref, n_rows):
    cols = []
    for col in range(D_CONV // LANES):
        lanes = slice(col * LANES, (col + 1) * LANES)
        acc = jnp.broadcast_to(b_ref[:, lanes], (n_rows, LANES))
        for tap in range(CONV_WIDTH):
            acc = acc + window(col, tap) * w_ref[tap:tap + 1, lanes]
        cols.append(acc)
    return jnp.concatenate(cols, axis=-1)


def _conv_prompt_kernel(a_ref, w_ref, b_ref, g_ref, bl_ref, o_ref, buf_ref, full_ref, cv_ref):
    j = pl.program_id(1)
    n_cols = D_CONV // LANES

    @pl.when(j == 0)
    def _():
        full_ref[:, 0:HALO_PAD, :] = jnp.zeros((n_cols, HALO_PAD, LANES), f32)

    @pl.when(j > 0)
    def _():
        full_ref[:, 0:HALO_PAD, :] = full_ref[:, TM:TM + HALO_PAD, :]

    for col in range(n_cols):
        full_ref[col, HALO_PAD:HALO_PAD + TM, :] = a_ref[:, col * LANES:(col + 1) * LANES]
    shift = HALO_PAD - CONV_HALO

    n_chunks = TM // CONV_ROWS

    def taps(idx, carry):
        col = idx // n_chunks
        r0 = pl.multiple_of((idx % n_chunks) * CONV_ROWS, CONV_ROWS)
        acc = jnp.broadcast_to(b_ref[col], (CONV_ROWS, LANES))
        for tap in range(CONV_WIDTH):
            acc = acc + full_ref[col, pl.ds(r0 + (tap + shift), CONV_ROWS), :] * w_ref[col, tap:tap + 1, :]
        cv_ref[col, pl.ds(r0, CONV_ROWS), :] = acc
        return carry

    lax.fori_loop(0, n_cols * n_chunks, taps, 0)

    for r0 in range(0, TM, NORM_ROWS):
        cv = jnp.concatenate([cv_ref[col, r0:r0 + NORM_ROWS, :] for col in range(n_cols)], axis=-1)
        o_ref[r0:r0 + NORM_ROWS, :] = _ln_silu(cv, g_ref[...], bl_ref[...])

    @pl.when(j == pl.num_programs(1) - 1)
    def _():
        buf_ref[...] = a_ref[TM - CONV_HALO:TM, :]


def _conv_prompt(tl, layer, a, w_conv, b_conv, g_ln, b_ln):
    tps = tl.tiles_per_seq
    n_cols = D_CONV // LANES
    depth = w_conv.shape[0]
    w_cols = w_conv.reshape(depth, CONV_WIDTH, n_cols, LANES).transpose(0, 2, 1, 3)
    b_cols = b_conv.reshape(depth, n_cols, 1, LANES)
    vec = pl.BlockSpec((None, 1, D_CONV), lambda b, j: (layer, 0, 0))
    return pl.pallas_call(
        _conv_prompt_kernel,
        grid=(tl.bp, tps),
        in_specs=[pl.BlockSpec((TM, D_CONV), lambda b, j: (b * tps + j, 0)),
                  pl.BlockSpec((None, n_cols, CONV_WIDTH, LANES), lambda b, j: (layer, 0, 0, 0)),
                  pl.BlockSpec((None, n_cols, 1, LANES), lambda b, j: (layer, 0, 0, 0)),
                  vec, vec],
        out_specs=[pl.BlockSpec((TM, D_CONV), lambda b, j: (b * tps + j, 0)),
                   pl.BlockSpec((None, CONV_HALO, D_CONV), lambda b, j: (b, 0, 0))],
        out_shape=[jax.ShapeDtypeStruct((tl.n_prompt, D_CONV), f32),
                   jax.ShapeDtypeStruct((tl.bp, CONV_HALO, D_CONV), f32)],
        scratch_shapes=[pltpu.VMEM((n_cols, HALO_PAD + TM, LANES), f32), pltpu.VMEM((n_cols, TM, LANES), f32)],
        compiler_params=_cparams(2),
        name="conv_prompt",
    )(a, w_cols, b_cols, g_ln, b_ln)


def _conv_sample_kernel(a_ref, st_ref, w_ref, b_ref, g_ref, bl_ref, o_ref, buf_ref, full_ref, cv_ref, *, ts):
    for s in range(SAMPLE_GROUP):
        for col in range(D_CONV // LANES):
            lanes = slice(col * LANES, (col + 1) * LANES)
            full_ref[col, 0:CONV_HALO, :] = st_ref[s, :, lanes]
            full_ref[col, CONV_HALO:CONV_HALO + ts, :] = a_ref[s * ts:(s + 1) * ts, lanes]
            buf_ref[s, :, lanes] = full_ref[col, ts:ts + CONV_HALO, :]
        cv_ref[s * ts:(s + 1) * ts, :] = _conv_taps(lambda col, tap: full_ref[col, tap:tap + ts, :],
                                                    w_ref, b_ref, ts)
    o_ref[...] = _ln_silu(cv_ref[...], g_ref[...], bl_ref[...])


def _conv_sample(tl, layer, a, state_conv, w_conv, b_conv, g_ln, b_ln):
    ts = tl.ts
    rows = SAMPLE_GROUP * ts
    first = tl.n_prompt // rows
    vec = pl.BlockSpec((None, 1, D_CONV), lambda i: (layer, 0, 0))
    return pl.pallas_call(
        functools.partial(_conv_sample_kernel, ts=ts),
        grid=(tl.bs // SAMPLE_GROUP,),
        in_specs=[pl.BlockSpec((rows, D_CONV), lambda i: (first + i, 0)),
                  pl.BlockSpec((None, SAMPLE_GROUP, CONV_HALO, D_CONV), lambda i: (layer, i, 0, 0)),
                  pl.BlockSpec((None, CONV_WIDTH, D_CONV), lambda i: (layer, 0, 0)),
                  vec, vec, vec],
        out_specs=[pl.BlockSpec((rows, D_CONV), lambda i: (i, 0)),
                   pl.BlockSpec((SAMPLE_GROUP, CONV_HALO, D_CONV), lambda i: (i, 0, 0))],
        out_shape=[jax.ShapeDtypeStruct((tl.n_sample, D_CONV), f32),
                   jax.ShapeDtypeStruct((tl.bs, CONV_HALO, D_CONV), f32)],
        scratch_shapes=[pltpu.VMEM((D_CONV // LANES, CONV_HALO + ts + SUBLANES, LANES), f32),
                        pltpu.VMEM((rows, D_CONV), f32)],
        compiler_params=_cparams(1),
        name="conv_sample",
    )(a, state_conv, w_conv, b_conv, g_ln, b_ln)


SORT_ROWS = 2 * TM + N_EXPERTS * SUBLANES
XS_W = D_MODEL + LANES


def _split3(x):
    a = x.astype(bf16)
    r = x - a.astype(f32)
    b = r.astype(bf16)
    c = (r - b.astype(f32)).astype(bf16)
    return a, b, c


def _first_of4(vals, m):
    return jnp.where(vals[0] == m, 0.0, jnp.where(vals[1] == m, 1.0, jnp.where(vals[2] == m, 2.0, 3.0)))


def _rows_to_tile(rows, n_rows):
    sub = lax.broadcasted_iota(i32, (n_rows, TM), 0)
    out = jnp.zeros((n_rows, TM), f32)
    for r, val in enumerate(rows):
        out = jnp.where(sub == r, val, out)
    return out


def _outproj_kernel(*refs, np_tiles, split_x):
    if split_x:
        xp_ref, xs_ref = refs[:2]
        refs = refs[2:]
    else:
        x_ref = refs[0]
        refs = refs[1:]
    (retp_ref, rets_ref, cvp_ref, cvs_ref, gts_ref, gtt_ref, shs_ref, sht_ref, scs_ref, sct_ref,
     g_ref, wo_ref, wrh_ref, wrl_ref, br_ref,
     xo_ref, h2_ref, rows_ref, cols_ref, cnt_ref) = refs
    is_s = pl.program_id(0) >= np_tiles

    x = jnp.where(is_s, xs_ref[...], xp_ref[...]) if split_x else x_ref[...]
    ret = jnp.where(is_s, rets_ref[...], retp_ref[...]).astype(bf16)
    cv = jnp.where(is_s, cvs_ref[...], cvp_ref[...]).astype(bf16)
    mix_out = (jnp.dot(ret, wo_ref[0:D_RET, :], preferred_element_type=f32)
               + jnp.dot(cv, wo_ref[D_RET:D_RET + D_CONV, :], preferred_element_type=f32))
    xn = x + _pick(is_s, gtt_ref, gts_ref) * mix_out
    xo_ref[...] = xn
    h2 = _rms(xn, g_ref[...]) * (1.0 + _pick(is_s, sct_ref, scs_ref)) + _pick(is_s, sht_ref, shs_ref)
    h_hi = h2.astype(bf16)
    h2_ref[...] = h_hi

    h_lo = (h2 - h_hi.astype(f32)).astype(bf16)
    logits = (jnp.dot(h_hi, wrh_ref[...], preferred_element_type=f32)
              + jnp.dot(h_hi, wrl_ref[...], preferred_element_type=f32)
              + jnp.dot(h_lo, wrh_ref[...], preferred_element_type=f32)) + br_ref[...]
    lt = logits.T
    row = [lt[e:e + 1, :] for e in range(N_EXPERTS)]
    top = functools.reduce(jnp.maximum, row)
    ex = [jnp.exp(r - top) for r in row]
    den = functools.reduce(jnp.add, ex)
    p = [v / den for v in ex]

    best = None
    for g in range(N_GROUPS):
        a = p[g * GROUP_SIZE:(g + 1) * GROUP_SIZE]
        m1 = functools.reduce(jnp.maximum, a)
        i1 = _first_of4(a, m1)
        b = [jnp.where(i1 == float(j), -1.0, a[j]) for j in range(GROUP_SIZE)]
        m2 = functools.reduce(jnp.maximum, b)
        i2 = _first_of4(b, m2)
        cand = (m1 + m2, m1, m2, i1 + float(g * GROUP_SIZE), i2 + float(g * GROUP_SIZE))
        if best is None:
            best = cand
        else:
            take = cand[0] > best[0]
            best = tuple(jnp.where(take, c, o) for c, o in zip(cand, best))
    _, m1, m2, e0, e1 = best
    denom = m1 + m2
    w0 = m1 / denom
    w1 = m2 / denom

    ex_id = lax.broadcasted_iota(i32, (N_EXPERTS, TM), 0).astype(f32)
    sel0 = ex_id == e0
    sel1 = ex_id == e1
    ind = jnp.where(sel0 | sel1, 1.0, 0.0)
    t_r = lax.broadcasted_iota(i32, (TM, TM), 0)
    t_c = lax.broadcasted_iota(i32, (TM, TM), 1)
    earlier = jnp.where(t_r < t_c, 1.0, 0.0).astype(bf16)
    prefix = jnp.dot(ind.astype(bf16), earlier, preferred_element_type=f32)
    cnt = jnp.sum(ind, axis=-1, keepdims=True)
    cnt8 = jnp.floor((cnt + float(SUBLANES - 1)) * (1.0 / SUBLANES)) * float(SUBLANES)
    e_r = lax.broadcasted_iota(i32, (N_EXPERTS, N_EXPERTS), 0)
    e_c = lax.broadcasted_iota(i32, (N_EXPERTS, N_EXPERTS), 1)
    below = jnp.where(e_c < e_r, 1.0, 0.0).astype(bf16)
    seg_off = jnp.dot(below, jnp.broadcast_to(cnt8, (N_EXPERTS, TM)).astype(bf16),
                      preferred_element_type=f32)
    where_to = seg_off + prefix
    pos0 = jnp.sum(jnp.where(sel0, where_to, 0.0), axis=0, keepdims=True)
    pos1 = jnp.sum(jnp.where(sel1, where_to, 0.0), axis=0, keepdims=True)

    w0p = [v.astype(f32) for v in _split3(w0)]
    w1p = [v.astype(f32) for v in _split3(w1)]
    info = [pos0, pos1] + w0p + w1p
    rows_ref[...] = _rows_to_tile(info, SUBLANES)
    cols_ref[...] = _rows_to_tile(info, LANES).T
    cnt_ref[...] = jnp.broadcast_to(cnt, (N_EXPERTS, LANES))


def _outproj(tl, layer, x, ret_p, ret_s, cv_p, cv_s, mod_seq, mod_tok, g_norm, w_out_bf, wr_hi, wr_lo, br_pad):
    split_x = isinstance(x, tuple)
    tok_spec = pl.BlockSpec((TM, D_MODEL), lambda i: (i, 0))
    p_spec = lambda w: pl.BlockSpec((TM, w), lambda i: (tl.prompt_block(i), 0))
    s_spec = lambda w: pl.BlockSpec((TM, w), lambda i: (tl.sample_block(i), 0))
    if split_x:
        x_args, x_specs = list(x), [p_spec(D_MODEL), s_spec(D_MODEL)]
    else:
        x_args, x_specs = [x], [tok_spec]
    mods = []
    for col in (2, 3, 4):
        mods += list(_mod_specs(tl, layer, col))
    wr_spec = pl.BlockSpec((D_MODEL, LANES), lambda i: (0, 0))
    return pl.pallas_call(
        functools.partial(_outproj_kernel, np_tiles=tl.np_tiles, split_x=split_x),
        grid=(tl.n_tiles,),
        in_specs=x_specs + [p_spec(D_RET), s_spec(D_RET), p_spec(D_CONV), s_spec(D_CONV)] + mods + [
            pl.BlockSpec((None, 1, D_MODEL), lambda i: (layer, 0, 0)),
            pl.BlockSpec((None, D_MODEL, D_MODEL), lambda i: (layer, 0, 0)),
            wr_spec, wr_spec,
            pl.BlockSpec((1, LANES), lambda i: (0, 0)),
        ],
        out_specs=[tok_spec, tok_spec,
                   pl.BlockSpec((None, SUBLANES, TM), lambda i: (i, 0, 0)),
                   pl.BlockSpec((TM, LANES), lambda i: (i, 0)),
                   pl.BlockSpec((None, N_EXPERTS, LANES), lambda i: (i, 0, 0))],
        out_shape=[jax.ShapeDtypeStruct((tl.n_tok, D_MODEL), f32),
                   jax.ShapeDtypeStruct((tl.n_tok, D_MODEL), bf16),
                   jax.ShapeDtypeStruct((tl.n_tiles, SUBLANES, TM), f32),
                   jax.ShapeDtypeStruct((tl.n_tok, LANES), f32),
                   jax.ShapeDtypeStruct((tl.n_tiles, N_EXPERTS, LANES), f32)],
        compiler_params=_cparams(1),
        name="outproj_router",
    )(*x_args, ret_p, ret_s, cv_p, cv_s, *([mod_seq, mod_tok] * 3), g_norm, w_out_bf, wr_hi, wr_lo, br_pad)


N_CHUNKS = SORT_ROWS // SUBLANES


class _Layout:
    def __init__(self, n_tiles):
        self.n_tiles = n_tiles
        self.tail_start = n_tiles * N_CHUNKS
        self.tail_n8 = self.tail_start + N_EXPERTS
        worst = 2 * n_tiles * TM + n_tiles * N_EXPERTS * (SUBLANES - 1) + N_EXPERTS * (BM - SUBLANES)
        self.n_blocks = -(-worst // BM)
        self.cap = self.n_blocks * BM
        self.dump = self.cap
        self.xs_rows = self.cap + -(-2 * SORT_ROWS // BM) * BM


def _moe_tables(lay, tile_counts):
    c8 = ((tile_counts.astype(i32) + SUBLANES - 1) // SUBLANES) * SUBLANES
    base8 = jnp.cumsum(c8, axis=0) - c8
    tot8 = jnp.sum(c8, axis=0)
    region = ((tot8 + BM - 1) // BM) * BM
    g_end = jnp.cumsum(region)
    g_start = g_end - region
    seg_end = jnp.cumsum(c8, axis=1)
    seg_dst = g_start[None, :] + base8
    n_used = g_end[-1] // BM
    blk = jnp.arange(lay.n_blocks, dtype=i32)
    block_e = jnp.minimum(jnp.sum((g_end[None, :] <= blk[:, None] * BM).astype(i32), axis=1), N_EXPERTS - 1)
    block_e = jnp.where(blk < n_used, block_e, block_e[n_used - 1])
    row0 = jnp.arange(N_CHUNKS, dtype=i32) * SUBLANES
    owner = jnp.sum((seg_end[:, None, :] <= row0[None, :, None]).astype(i32), axis=-1)
    onehot = (owner[:, :, None] == jnp.arange(N_EXPERTS, dtype=i32)[None, None, :]).astype(i32)
    delta = seg_dst - (seg_end - c8)
    chunk_dst = jnp.where(owner < N_EXPERTS, row0[None, :] + jnp.sum(onehot * delta[:, None, :], axis=-1), -1)
    tab = jnp.concatenate([chunk_dst.ravel(), g_start + tot8, (region - tot8) // SUBLANES]).astype(i32)
    ids = jnp.arange(N_EXPERTS, dtype=i32)
    later = jnp.where((ids[None, :] > ids[:, None]) & (region[None, :] > 0), ids[None, :], N_EXPERTS)
    next_e = jnp.min(later, axis=1)
    next_e = jnp.where(next_e == N_EXPERTS, -1, next_e)
    return tab, jnp.concatenate([block_e, n_used[None], next_e]).astype(i32)


def _for_chunks(n, fn):
    def body(c, carry):
        fn(c)
        return carry

    lax.fori_loop(0, n, body, 0)


def _dispatch_kernel(tab_ref, h2_ref, rows_ref, cols_ref, xs_hbm, sorted_ref, zero_ref, sem, *, lay):
    i = pl.program_id(0)
    n_tiles = lay.n_tiles
    slot = i % 2

    def tail_copy(dst):
        return pltpu.make_async_copy(zero_ref, xs_hbm.at[pl.ds(dst, SUBLANES)], sem.at[2])

    def wait_tile(slot_):
        pltpu.make_async_copy(sorted_ref.at[slot_], xs_hbm.at[pl.ds(0, SORT_ROWS)], sem.at[slot_]).wait()

    @pl.when(i == 0)
    def _():
        zero_ref[...] = jnp.zeros_like(zero_ref)
        for e in range(N_EXPERTS):
            start = tab_ref[lay.tail_start + e]
            _for_chunks(tab_ref[lay.tail_n8 + e],
                        lambda c: tail_copy(pl.multiple_of(start + c * SUBLANES, SUBLANES)).start())

    @pl.when(i >= 2)
    def _():
        wait_tile(slot)

    pos0 = rows_ref[0:1, :]
    pos1 = rows_ref[1:2, :]
    r_id = lax.broadcasted_iota(i32, (SORT_ROWS, TM), 0).astype(f32)
    p0 = r_id == pos0
    p1 = r_id == pos1
    perm = jnp.where(p0 | p1, 1.0, 0.0).astype(bf16)
    lane = lax.broadcasted_iota(i32, (TM, LANES), 1)
    cols = cols_ref[...]
    wpart0 = jnp.where((lane >= 2) & (lane < 5), cols, 0.0).astype(bf16)
    wpart1 = jnp.where((lane >= 5) & (lane < 8), cols, 0.0).astype(bf16)
    sw = (jnp.dot(jnp.where(p0, 1.0, 0.0).astype(bf16), wpart0, preferred_element_type=f32)
          + jnp.dot(jnp.where(p1, 1.0, 0.0).astype(bf16), wpart1, preferred_element_type=f32))
    sorted_ref[slot, :, 0:D_MODEL] = jnp.dot(perm, h2_ref[...], preferred_element_type=f32)
    sorted_ref[slot, :, D_MODEL:XS_W] = jnp.broadcast_to(jnp.sum(sw, axis=-1, keepdims=True), (SORT_ROWS, LANES))

    for c in range(N_CHUNKS):
        dst = tab_ref[i * N_CHUNKS + c]
        dst = jnp.where(dst < 0, lay.dump + slot * SORT_ROWS + c * SUBLANES, dst)
        pltpu.make_async_copy(sorted_ref.at[slot, pl.ds(c * SUBLANES, SUBLANES)],
                              xs_hbm.at[pl.ds(pl.multiple_of(dst, SUBLANES), SUBLANES)], sem.at[slot]).start()

    @pl.when(i == n_tiles - 1)
    def _():
        if n_tiles >= 2:
            wait_tile(1 - slot)
        wait_tile(slot)
        for e in range(N_EXPERTS):
            _for_chunks(tab_ref[lay.tail_n8 + e], lambda c: tail_copy(0).wait())


def _dispatch(tl, lay, tab, h2, rows, cols):
    grid_spec = pltpu.PrefetchScalarGridSpec(
        num_scalar_prefetch=1,
        grid=(tl.n_tiles,),
        in_specs=[pl.BlockSpec((TM, D_MODEL), lambda i, t: (i, 0)),
                  pl.BlockSpec((None, SUBLANES, TM), lambda i, t: (i, 0, 0)),
                  pl.BlockSpec((TM, LANES), lambda i, t: (i, 0))],
        out_specs=pl.BlockSpec(memory_space=pl.ANY),
        scratch_shapes=[pltpu.VMEM((2, SORT_ROWS, XS_W), f32), pltpu.VMEM((SUBLANES, XS_W), f32),
                        pltpu.SemaphoreType.DMA((3,))],
    )
    return pl.pallas_call(
        functools.partial(_dispatch_kernel, lay=lay),
        grid_spec=grid_spec,
        out_shape=jax.ShapeDtypeStruct((lay.xs_rows, XS_W), f32),
        compiler_params=_cparams(1),
        name="moe_dispatch",
    )(tab, h2, rows, cols)


def _expert_kernel(be_ref, xs_ref, wg_hbm, wu_hbm, wd_hbm, ys_ref, stage, w_bf, sem, *, n_blocks, layer):
    j = pl.program_id(0)

    def fetch(e):
        return [pltpu.make_async_copy(w.at[layer, e], stage.at[k], sem.at[k])
                for k, w in enumerate((wg_hbm, wu_hbm, wd_hbm))]

    @pl.when(j < be_ref[n_blocks])
    def _():
        e = be_ref[j]

        @pl.when(j == 0)
        def _():
            for copy in fetch(e):
                copy.start()

        @pl.when((j == 0) | (e != be_ref[jnp.maximum(j - 1, 0)]))
        def _():
            for copy in fetch(e):
                copy.wait()
            for k in range(3):
                w_bf[k] = stage[k].astype(bf16)
            nxt = be_ref[n_blocks + 1 + e]

            @pl.when(nxt >= 0)
            def _():
                for copy in fetch(nxt):
                    copy.start()

        x = xs_ref[:, 0:D_MODEL].astype(bf16)
        gate = jnp.dot(x, w_bf[0], preferred_element_type=f32)
        up = jnp.dot(x, w_bf[1], preferred_element_type=f32)
        mid = (jax.nn.silu(gate) * up).astype(bf16)
        ys_ref[...] = jnp.dot(mid, w_bf[2], preferred_element_type=f32) * xs_ref[:, D_MODEL:D_MODEL + 1]


def _experts(layer, lay, block_e, xs, w_gate, w_up, w_down):
    n_blocks = lay.n_blocks
    d_ff = w_gate.shape[-1]
    assert d_ff == D_MODEL
    used = lambda j, be: jnp.minimum(j, be[n_blocks] - 1)
    any_spec = pl.BlockSpec(memory_space=pl.ANY)
    grid_spec = pltpu.PrefetchScalarGridSpec(
        num_scalar_prefetch=1,
        grid=(n_blocks,),
        in_specs=[pl.BlockSpec((BM, XS_W), lambda j, be: (used(j, be), 0)), any_spec, any_spec, any_spec],
        out_specs=pl.BlockSpec((BM, D_MODEL), lambda j, be: (used(j, be), 0)),
        scratch_shapes=[pltpu.VMEM((3, D_MODEL, d_ff), f32), pltpu.VMEM((3, D_MODEL, d_ff), bf16),
                        pltpu.SemaphoreType.DMA((3,))],
    )
    return pl.pallas_call(
        functools.partial(_expert_kernel, n_blocks=n_blocks, layer=layer),
        grid_spec=grid_spec,
        out_shape=jax.ShapeDtypeStruct((lay.cap, D_MODEL), f32),
        compiler_params=_cparams(1),
        name="moe_experts",
    )(block_e, xs, w_gate, w_up, w_down)


def _combine_kernel(tab_ref, ys_hbm, cols_ref, x_ref, gts_ref, gtt_ref, *rest, lay, np_tiles, final):
    if final:
        gf_ref, yp_ref, ysm_ref, staged, sem = rest
    else:
        xo_ref, staged, sem = rest
    i = pl.program_id(0)
    n_tiles = lay.n_tiles
    slot = i % 2
    is_s = i >= np_tiles

    def start_tile(tile, slot_):
        for c in range(N_CHUNKS):
            src = jnp.maximum(tab_ref[tile * N_CHUNKS + c], 0)
            pltpu.make_async_copy(ys_hbm.at[pl.ds(pl.multiple_of(src, SUBLANES), SUBLANES)],
                                  staged.at[slot_, pl.ds(c * SUBLANES, SUBLANES)], sem.at[slot_]).start()

    @pl.when(i == 0)
    def _():
        start_tile(0, 0)

    @pl.when(i + 1 < n_tiles)
    def _():
        start_tile(i + 1, 1 - slot)

    pltpu.make_async_copy(ys_hbm.at[pl.ds(0, SORT_ROWS)], staged.at[slot], sem.at[slot]).wait()

    lane = lax.broadcasted_iota(i32, (TM, SORT_ROWS), 1).astype(f32)
    unperm = jnp.where((lane == cols_ref[:, 0:1]) | (lane == cols_ref[:, 1:2]), 1.0, 0.0).astype(bf16)
    ff = sum(jnp.dot(unperm, part, preferred_element_type=f32) for part in _split3(staged[slot]))
    xn = x_ref[...] + _pick(is_s, gtt_ref, gts_ref) * ff
    if final:
        y = _rms(xn, gf_ref[...])

        @pl.when(jnp.logical_not(is_s))
        def _():
            yp_ref[...] = y

        @pl.when(is_s)
        def _():
            ysm_ref[...] = y
    else:
        xo_ref[...] = xn


def _combine(tl, lay, layer, tab, ys, cols, x, mod_seq, mod_tok, g_final):
    final = g_final is not None
    tok_spec = pl.BlockSpec((TM, D_MODEL), lambda i, t: (i, 0))
    gt_seq = pl.BlockSpec((None, None, 1, D_MODEL), lambda i, t: (layer, tl.seq_index(i), 0, 5))
    gt_tok = pl.BlockSpec((None, TM, D_MODEL), lambda i, t: (layer, tl.sample_block(i), 5))
    in_specs = [pl.BlockSpec(memory_space=pl.ANY), pl.BlockSpec((TM, LANES), lambda i, t: (i, 0)),
                tok_spec, gt_seq, gt_tok]
    args = [tab, ys, cols, x, mod_seq, mod_tok]
    if final:
        in_specs.append(pl.BlockSpec((1, D_MODEL), lambda i, t: (0, 0)))
        args.append(g_final)
        out_specs = [pl.BlockSpec((TM, D_MODEL), lambda i, t: (tl.prompt_block(i), 0)),
                     pl.BlockSpec((TM, D_MODEL), lambda i, t: (tl.sample_block(i), 0))]
        out_shape = [jax.ShapeDtypeStruct((tl.n_prompt, D_MODEL), f32),
                     jax.ShapeDtypeStruct((tl.n_sample, D_MODEL), f32)]
    else:
        out_specs = tok_spec
        out_shape = jax.ShapeDtypeStruct((tl.n_tok, D_MODEL), f32)
    grid_spec = pltpu.PrefetchScalarGridSpec(
        num_scalar_prefetch=1,
        grid=(tl.n_tiles,),
        in_specs=in_specs,
        out_specs=out_specs,
        scratch_shapes=[pltpu.VMEM((2, SORT_ROWS, D_MODEL), f32), pltpu.SemaphoreType.DMA((2,))],
    )
    return pl.pallas_call(
        functools.partial(_combine_kernel, lay=lay, np_tiles=tl.np_tiles, final=final),
        grid_spec=grid_spec,
        out_shape=out_shape,
        compiler_params=_cparams(1),
        name="moe_combine",
    )(*args)


def _rope_tables(tl):
    half = HEAD_D // 2
    inv = ROPE_BASE ** (-jnp.arange(half, dtype=f32) / half)
    pos_p = jnp.arange(tl.tp, dtype=i32)
    pos_s = PAST_LEN + jnp.arange(tl.ts, dtype=i32)
    pos = jnp.concatenate([pos_p, jnp.tile(pos_s, tl.bs)])
    ang = pos.astype(f32)[:, None] * inv[None, :]
    cos, sin = jnp.cos(ang), jnp.sin(ang)
    return jnp.concatenate([cos, cos], axis=-1), jnp.concatenate([-sin, sin], axis=-1)


def kernel(x_prompt, x_sample, state_ret, state_conv, c_prompt, c_sample, w_mod, b_mod, g_mix_norm, w_in,
           w_conv, b_conv, g_conv_ln, b_conv_ln, g_ret_gn, w_out, g_ffn_norm, w_router, b_router,
           w_exp_gate, w_exp_up, w_exp_down, g_final):
    bp, tp, _ = x_prompt.shape
    bs, ts, _ = x_sample.shape
    depth = w_mod.shape[0]
    tl = _Tiles(bp, tp, bs, ts)
    lay = _Layout(tl.n_tiles)

    c_all = jnp.concatenate([c_prompt, jnp.repeat(c_sample, ts, axis=0)], axis=0)
    mod = _modulation(c_all, w_mod, b_mod)
    mod_seq = mod[:, :bp].reshape(depth, bp, 1, N_MOD * D_MODEL)
    mod_tok = mod[:, bp:]

    cos_tab, sin_tab = _rope_tables(tl)
    w_in_bf = w_in.astype(bf16)
    w_out_bf = w_out.astype(bf16)
    wr_pad = jnp.pad(w_router.astype(f32), ((0, 0), (0, LANES - N_EXPERTS)))
    wr_hi = wr_pad.astype(bf16)
    wr_lo = (wr_pad - wr_hi.astype(f32)).astype(bf16)
    br_pad = jnp.pad(b_router.astype(f32), (0, LANES - N_EXPERTS)).reshape(1, LANES)
    vec3 = lambda t: t.reshape(depth, 1, t.shape[-1])
    g_mix3, g_ffn3, gn3 = vec3(g_mix_norm), vec3(g_ffn_norm), vec3(g_ret_gn)
    b_conv3, g_ln3, b_ln3 = vec3(b_conv), vec3(g_conv_ln), vec3(b_conv_ln)

    x = (x_prompt.reshape(tl.n_prompt, D_MODEL), x_sample.reshape(tl.n_sample, D_MODEL))
    ret_p, conv_p, conv_s = [], [], []
    ret_s_all = None
    for layer in range(depth):
        q, k, v, gate, a = _inproj(tl, layer, x, mod_seq, mod_tok, g_mix3, w_in_bf, cos_tab, sin_tab)
        ro_p, s_p = _retention_prompt(tl, layer, q, k, v, gate, gn3)
        ro_s, ret_s_all = _retention_sample(tl, layer, q, k, v, gate, state_ret, gn3, ret_s_all)
        co_p, buf_p = _conv_prompt(tl, layer, a, w_conv, b_conv3, g_ln3, b_ln3)
        co_s, buf_s = _conv_sample(tl, layer, a, state_conv, w_conv, b_conv3, g_ln3, b_ln3)
        x_mid, h2, rows, cols, tile_counts = _outproj(
            tl, layer, x, ro_p, ro_s, co_p, co_s, mod_seq, mod_tok, g_ffn3, w_out_bf, wr_hi, wr_lo, br_pad)
        tab, block_e = _moe_tables(lay, tile_counts[:, :, 0])
        xs = _dispatch(tl, lay, tab, h2, rows, cols)
        ys = _experts(layer, lay, block_e, xs, w_exp_gate, w_exp_up, w_exp_down)
        last = layer == depth - 1
        x = _combine(tl, lay, layer, tab, ys, cols, x_mid, mod_seq, mod_tok,
                     g_final.reshape(1, D_MODEL) if last else None)
        ret_p.append(s_p)
        conv_p.append(buf_p)
        conv_s.append(buf_s)
    y_p, y_s = x
    return (y_p.reshape(bp, tp, D_MODEL), y_s.reshape(bs, ts, D_MODEL),
            jnp.stack(ret_p), jnp.stack(conv_p), ret_s_all, jnp.stack(conv_s))
```

```python
import functools

import jax
import jax.numpy as jnp
from jax import lax
from jax.experimental import pallas as pl
from jax.experimental.pallas import tpu as pltpu

f32 = jnp.float32
bf16 = jnp.bfloat16
i32 = jnp.int32

D_MODEL = 1024
D_RET = 512
D_CONV = 512
N_HEADS = 4
HEAD_D = 128
RET_CHUNK = 128
RET_CHUNKS_PER_STEP = 4
ROPE_BASE = 10000.0
CONV_WIDTH = 31
CONV_HALO = CONV_WIDTH - 1
N_EXPERTS = 16
N_GROUPS = 4
GROUP_SIZE = N_EXPERTS // N_GROUPS
N_MOD = 6
EPS = 1e-6
PAST_LEN = 16384
D_IN = 4 * D_RET + 2 * D_CONV

LANES = 128
SUBLANES = 8
TM = 256
STEP_TILES = 2
TS = TM * STEP_TILES
BM = 512
CONV_ROWS = 128
NORM_ROWS = 64
SAMPLE_GROUP = 8
HALO_PAD = 32
VMEM_LIMIT = 56 * 1024 * 1024


def _cparams(n_axes, vmem=VMEM_LIMIT):
    return pltpu.CompilerParams(dimension_semantics=("arbitrary",) * n_axes, vmem_limit_bytes=vmem)


def _mod_kernel(c_ref, w_ref, b_ref, o_ref):
    cond = jax.nn.silu(c_ref[...]).astype(bf16)
    o_ref[...] = jnp.dot(cond, w_ref[...].astype(bf16), preferred_element_type=f32) + b_ref[...]


def _modulation(c_all, w_mod, b_mod):
    depth = w_mod.shape[0]
    m = c_all.shape[0]
    return pl.pallas_call(
        _mod_kernel,
        grid=(depth, N_MOD),
        in_specs=[
            pl.BlockSpec((m, D_MODEL), lambda l, j: (0, 0)),
            pl.BlockSpec((None, D_MODEL, D_MODEL), lambda l, j: (l, 0, j)),
            pl.BlockSpec((None, 1, D_MODEL), lambda l, j: (l, 0, j)),
        ],
        out_specs=pl.BlockSpec((None, m, D_MODEL), lambda l, j: (l, 0, j)),
        out_shape=jax.ShapeDtypeStruct((depth, m, N_MOD * D_MODEL), f32),
        compiler_params=_cparams(2),
        name="modulation",
    )(c_all, w_mod, b_mod.reshape(depth, 1, N_MOD * D_MODEL))


class _Tiles:
    def __init__(self, bp, tp, bs, ts):
        self.bp, self.tp, self.bs, self.ts = bp, tp, bs, ts
        self.n_prompt = bp * tp
        self.n_sample = bs * ts
        self.n_tok = self.n_prompt + self.n_sample
        assert tp % TS == 0 and self.n_sample % TS == 0
        self.tiles_per_seq = tp // TM
        self.np_tiles = self.n_prompt // TM
        self.n_tiles = self.n_tok // TM
        self.steps_per_seq = tp // TS
        self.np_steps = self.n_prompt // TS
        self.n_steps = self.n_tok // TS

    def prompt_block(self, i):
        return jnp.minimum(i, self.np_steps - 1)

    def sample_block(self, i):
        return jnp.maximum(i - self.np_steps, 0)

    def seq_index(self, i):
        return jnp.minimum(i // self.steps_per_seq, self.bp - 1)


def _mod_specs(tl, layer, col):
    seq = pl.BlockSpec((None, None, 1, D_MODEL), lambda i, *_: (layer, tl.seq_index(i), 0, col))
    tok = pl.BlockSpec((None, TS, D_MODEL), lambda i, *_: (layer, tl.sample_block(i), col))
    return seq, tok


def _pick(is_sample, tok_ref, seq_ref, rows=slice(None)):
    return jnp.where(is_sample, tok_ref[rows, :], seq_ref[...])


def _rms(x, g):
    return x * lax.rsqrt(jnp.mean(x * x, axis=-1, keepdims=True) + EPS) * g


def _inproj_kernel(*refs, np_steps, split_x):
    if split_x:
        xp_ref, xs_ref = refs[:2]
        refs = refs[2:]
    else:
        x_ref = refs[0]
        refs = refs[1:]
    (shs_ref, sht_ref, scs_ref, sct_ref, g_ref, w_ref, cos_ref, sin_ref,
     q_ref, k_ref, v_ref, gate_ref, a_ref) = refs
    is_s = pl.program_id(0) >= np_steps
    x = jnp.where(is_s, xs_ref[...], xp_ref[...]) if split_x else x_ref[...]
    h = _rms(x, g_ref[...]) * (1.0 + _pick(is_s, sct_ref, scs_ref)) + _pick(is_s, sht_ref, shs_ref)
    hb = h.astype(bf16)
    cos = cos_ref[...]
    sin = sin_ref[...]

    def proj(col):
        return jnp.dot(hb, w_ref[:, col * D_RET:(col + 1) * D_RET], preferred_element_type=f32)

    def rope(t):
        outs = []
        for hd in range(N_HEADS):
            th = t[:, hd * HEAD_D:(hd + 1) * HEAD_D]
            outs.append(th * cos + pltpu.roll(th, HEAD_D // 2, 1) * sin)
        return outs

    for hd, qh in enumerate(rope(proj(0))):
        q_ref[:, hd * HEAD_D:(hd + 1) * HEAD_D] = qh
    for hd, kh in enumerate(rope(proj(1))):
        k_ref[:, hd * HEAD_D:(hd + 1) * HEAD_D] = kh * (HEAD_D ** -0.5)
    v_ref[...] = proj(2)
    gate_ref[...] = proj(3)
    a_ref[...] = proj(4) * jax.nn.sigmoid(proj(5))


def _inproj(tl, layer, x, mod_seq, mod_tok, g_norm, w_in_bf, cos_tab, sin_tab):
    split_x = isinstance(x, tuple)
    tok_spec = pl.BlockSpec((TS, D_MODEL), lambda i: (i, 0))
    if split_x:
        x_args = list(x)
        x_specs = [pl.BlockSpec((TS, D_MODEL), lambda i: (tl.prompt_block(i), 0)),
                   pl.BlockSpec((TS, D_MODEL), lambda i: (tl.sample_block(i), 0))]
    else:
        x_args, x_specs = [x], [tok_spec]
    sh_seq, sh_tok = _mod_specs(tl, layer, 0)
    sc_seq, sc_tok = _mod_specs(tl, layer, 1)

    def table_block(i):
        return jnp.where(i < tl.np_steps, i % tl.steps_per_seq, tl.steps_per_seq + tl.sample_block(i))

    tab_spec = pl.BlockSpec((TS, HEAD_D), lambda i: (table_block(i), 0))
    out_spec = pl.BlockSpec((TS, D_RET), lambda i: (i, 0))
    out_sd = jax.ShapeDtypeStruct((tl.n_tok, D_RET), f32)
    return pl.pallas_call(
        functools.partial(_inproj_kernel, np_steps=tl.np_steps, split_x=split_x),
        grid=(tl.n_steps,),
        in_specs=x_specs + [
            sh_seq, sh_tok, sc_seq, sc_tok,
            pl.BlockSpec((None, 1, D_MODEL), lambda i: (layer, 0, 0)),
            pl.BlockSpec((None, D_MODEL, D_IN), lambda i: (layer, 0, 0)),
            tab_spec, tab_spec,
        ],
        out_specs=[out_spec] * 5,
        out_shape=[out_sd] * 5,
        compiler_params=_cparams(1),
        name="inproj",
    )(*x_args, mod_seq, mod_tok, mod_seq, mod_tok, g_norm, w_in_bf, cos_tab, sin_tab)


def _head_norm_gate(o, gn, gate):
    mu = jnp.mean(o, axis=-1, keepdims=True)
    var = jnp.mean(jnp.square(o - mu), axis=-1, keepdims=True)
    return jax.nn.silu(gate) * ((o - mu) * lax.rsqrt(var + EPS) * gn)


def _dot_nt(a, b):
    return lax.dot_general(a, b, (((1,), (1,)), ((), ())), preferred_element_type=f32)


def _dot_tn(a, b):
    return lax.dot_general(a, b, (((0,), (0,)), ((), ())), preferred_element_type=f32)


def _ret_prompt_kernel(q_ref, k_ref, v_ref, gate_ref, dec_ref, qd_ref, kd_ref, cd_ref, gn_ref,
                       o_ref, s_out_ref, s_ref):
    c = pl.program_id(1)

    @pl.when(c == 0)
    def _():
        s_ref[...] = jnp.zeros_like(s_ref)

    for ci in range(RET_CHUNKS_PER_STEP):
        rows = slice(ci * RET_CHUNK, (ci + 1) * RET_CHUNK)
        for hd in range(N_HEADS):
            sl = slice(hd * HEAD_D, (hd + 1) * HEAD_D)
            kh = k_ref[rows, sl]
            qb = q_ref[rows, sl].astype(bf16)
            kb = kh.astype(bf16)
            vb = v_ref[rows, sl].astype(bf16)
            s_old = s_ref[hd]
            scores = _dot_nt(qb, kb) * dec_ref[hd]
            inner = jnp.dot(scores.astype(bf16), vb, preferred_element_type=f32)
            cross = jnp.dot(qb, s_old.astype(bf16), preferred_element_type=f32) * qd_ref[hd]
            s_ref[hd] = s_old * cd_ref[hd] + _dot_tn((kh * kd_ref[hd]).astype(bf16), vb)
            o_ref[rows, sl] = _head_norm_gate(inner + cross, gn_ref[:, sl], gate_ref[rows, sl])

    @pl.when(c == pl.num_programs(1) - 1)
    def _():
        s_out_ref[...] = s_ref[...]


def _decay_tables(chunk, true_len):
    lg = jnp.log(1.0 - 2.0 ** (-5.0 - jnp.arange(N_HEADS, dtype=f32)))
    idx = jnp.arange(chunk, dtype=f32)
    rel = idx[:, None] - idx[None, :]
    decay = jnp.where(rel[None] >= 0, jnp.exp(jnp.maximum(rel, 0.0)[None] * lg[:, None, None]), 0.0)
    q_decay = jnp.exp((idx[None, :] + 1.0) * lg[:, None])
    k_decay = jnp.exp((true_len - 1.0 - idx[None, :]) * lg[:, None])
    c_decay = jnp.exp(true_len * lg)
    return decay, q_decay, k_decay, c_decay


def _retention_prompt(tl, layer, q, k, v, gate, g_ret_gn):
    step_rows = RET_CHUNK * RET_CHUNKS_PER_STEP
    assert tl.tp % step_rows == 0
    n_chunks = tl.tp // step_rows
    decay, q_decay, k_decay, c_decay = _decay_tables(RET_CHUNK, RET_CHUNK)
    bcast = lambda t: jnp.broadcast_to(t[:, :, None], (N_HEADS, RET_CHUNK, HEAD_D))
    cd = jnp.broadcast_to(c_decay[:, None, None], (N_HEADS, 1, HEAD_D))
    tok_spec = pl.BlockSpec((step_rows, D_RET), lambda b, c: (b * n_chunks + c, 0))
    tab_spec = pl.BlockSpec((N_HEADS, RET_CHUNK, HEAD_D), lambda b, c: (0, 0, 0))
    return pl.pallas_call(
        _ret_prompt_kernel,
        grid=(tl.bp, n_chunks),
        in_specs=[tok_spec] * 4 + [tab_spec] * 3 + [
            pl.BlockSpec((N_HEADS, 1, HEAD_D), lambda b, c: (0, 0, 0)),
            pl.BlockSpec((None, 1, D_RET), lambda b, c: (layer, 0, 0)),
        ],
        out_specs=[tok_spec, pl.BlockSpec((None, N_HEADS, HEAD_D, HEAD_D), lambda b, c: (b, 0, 0, 0))],
        out_shape=[jax.ShapeDtypeStruct((tl.n_prompt, D_RET), f32),
                   jax.ShapeDtypeStruct((tl.bp, N_HEADS, HEAD_D, HEAD_D), f32)],
        scratch_shapes=[pltpu.VMEM((N_HEADS, HEAD_D, HEAD_D), f32)],
        compiler_params=_cparams(2),
        name="retention_prompt",
    )(q, k, v, gate, decay, bcast(q_decay), bcast(k_decay), cd, g_ret_gn)


def _ret_sample_kernel(q_ref, k_ref, v_ref, gate_ref, s_in_ref, dec_ref, qd_ref, kd_ref, cd_ref, gn_ref,
                       *rest, ts):
    o_ref, s_all_ref = rest[-2:]
    s_out_ref = s_all_ref.at[0]
    for other in range(1, s_all_ref.shape[0]):
        s_all_ref[other] = jnp.zeros(s_all_ref.shape[1:], f32)
    seqs_per_tile = SUBLANES // ts
    row = lax.broadcasted_iota(i32, (SUBLANES, HEAD_D), 0)
    for t in range(SAMPLE_GROUP // seqs_per_tile):
        rows = slice(t * SUBLANES, (t + 1) * SUBLANES)
        for hd in range(N_HEADS):
            sl = slice(hd * HEAD_D, (hd + 1) * HEAD_D)
            qh = q_ref[rows, sl]
            kh = k_ref[rows, sl] * kd_ref[hd]
            vb = v_ref[rows, sl].astype(bf16)
            qb = qh.astype(bf16)
            scores = _dot_nt(qb, k_ref[rows, sl].astype(bf16)) * dec_ref[hd]
            o = jnp.dot(scores.astype(bf16), vb, preferred_element_type=f32)
            for s in range(seqs_per_tile):
                b = t * seqs_per_tile + s
                mine = (row >= s * ts) & (row < (s + 1) * ts)
                s_old = s_in_ref[b, hd]
                q_s = jnp.where(mine, qh, 0.0).astype(bf16)
                k_s = jnp.where(mine, kh, 0.0).astype(bf16)
                o = o + jnp.dot(q_s, s_old.astype(bf16), preferred_element_type=f32) * qd_ref[hd]
                s_out_ref[b, hd] = s_old * cd_ref[hd] + _dot_tn(k_s, vb)
            o_ref[rows, sl] = _head_norm_gate(o, gn_ref[:, sl], gate_ref[rows, sl])


def _retention_sample(tl, layer, q, k, v, gate, state_ret, g_ret_gn, prev_states):
    ts = tl.ts
    depth = state_ret.shape[0]
    assert SUBLANES % ts == 0 and tl.bs % SAMPLE_GROUP == 0
    seqs_per_tile = SUBLANES // ts
    decay, q_decay, k_decay, c_decay = _decay_tables(ts, ts)
    eye = jnp.eye(seqs_per_tile, dtype=f32)
    dec_tile = jnp.einsum("ab,hij->haibj", eye, decay).reshape(N_HEADS, SUBLANES, SUBLANES)
    tile_rows = lambda t: jnp.broadcast_to(jnp.tile(t, (1, seqs_per_tile))[:, :, None],
                                           (N_HEADS, SUBLANES, HEAD_D))
    cd = jnp.broadcast_to(c_decay[:, None, None], (N_HEADS, 1, HEAD_D))
    rows = SAMPLE_GROUP * ts
    first = tl.n_prompt // rows
    tok_spec = pl.BlockSpec((rows, D_RET), lambda i: (first + i, 0))
    const3 = lambda shape: pl.BlockSpec(shape, lambda i: (0, 0, 0))
    st_block = (SAMPLE_GROUP, N_HEADS, HEAD_D, HEAD_D)
    in_specs = [tok_spec] * 4 + [
        pl.BlockSpec((None,) + st_block, lambda i: (layer, i, 0, 0, 0)),
        const3((N_HEADS, SUBLANES, SUBLANES)),
        const3((N_HEADS, SUBLANES, HEAD_D)),
        const3((N_HEADS, SUBLANES, HEAD_D)),
        const3((N_HEADS, 1, HEAD_D)),
        pl.BlockSpec((None, 1, D_RET), lambda i: (layer, 0, 0)),
    ]
    args = [q, k, v, gate, state_ret, dec_tile, tile_rows(q_decay), tile_rows(k_decay), cd, g_ret_gn]
    if prev_states is None:
        state_spec = pl.BlockSpec((depth,) + st_block, lambda i: (0, i, 0, 0, 0))
        aliases = {}
    else:
        state_spec = pl.BlockSpec((1,) + st_block, lambda i: (layer, i, 0, 0, 0))
        in_specs.append(pl.BlockSpec(memory_space=pl.ANY))
        args.append(prev_states)
        aliases = {len(args) - 1: 1}
    return pl.pallas_call(
        functools.partial(_ret_sample_kernel, ts=ts),
        grid=(tl.bs // SAMPLE_GROUP,),
        in_specs=in_specs,
        out_specs=[pl.BlockSpec((rows, D_RET), lambda i: (i, 0)), state_spec],
        out_shape=[jax.ShapeDtypeStruct((tl.n_sample, D_RET), f32),
                   jax.ShapeDtypeStruct((depth, tl.bs, N_HEADS, HEAD_D, HEAD_D), f32)],
        input_output_aliases=aliases,
        compiler_params=_cparams(1),
        name="retention_sample",
    )(*args)


def _ln_silu(cv, g, b):
    mu = jnp.mean(cv, axis=-1, keepdims=True)
    var = jnp.mean(jnp.square(cv - mu), axis=-1, keepdims=True)
    return jax.nn.silu((cv - mu) * lax.rsqrt(var + EPS) * g + b)


def _conv_taps(window, w_ref, b_ref, n_rows):
    cols = []
    for col in range(D_CONV // LANES):
        lanes = slice(col * LANES, (col + 1) * LANES)
        acc = jnp.broadcast_to(b_ref[:, lanes], (n_rows, LANES))
        for tap in range(CONV_WIDTH):
            acc = acc + window(col, tap) * w_ref[tap:tap + 1, lanes]
        cols.append(acc)
    return jnp.concatenate(cols, axis=-1)


def _conv_prompt_kernel(a_ref, w_ref, b_ref, g_ref, bl_ref, o_ref, buf_ref, full_ref, cv_ref):
    j = pl.program_id(1)
    n_cols = D_CONV // LANES

    @pl.when(j == 0)
    def _():
        full_ref[:, 0:HALO_PAD, :] = jnp.zeros((n_cols, HALO_PAD, LANES), f32)

    @pl.when(j > 0)
    def _():
        full_ref[:, 0:HALO_PAD, :] = full_ref[:, TM:TM + HALO_PAD, :]

    for col in range(n_cols):
        full_ref[col, HALO_PAD:HALO_PAD + TM, :] = a_ref[:, col * LANES:(col + 1) * LANES]
    shift = HALO_PAD - CONV_HALO

    n_chunks = TM // CONV_ROWS

    def taps(idx, carry):
        col = idx // n_chunks
        r0 = pl.multiple_of((idx % n_chunks) * CONV_ROWS, CONV_ROWS)
        acc = jnp.broadcast_to(b_ref[col], (CONV_ROWS, LANES))
        for tap in range(CONV_WIDTH):
            acc = acc + full_ref[col, pl.ds(r0 + (tap + shift), CONV_ROWS), :] * w_ref[col, tap:tap + 1, :]
        cv_ref[col, pl.ds(r0, CONV_ROWS), :] = acc
        return carry

    lax.fori_loop(0, n_cols * n_chunks, taps, 0)

    for r0 in range(0, TM, NORM_ROWS):
        cv = jnp.concatenate([cv_ref[col, r0:r0 + NORM_ROWS, :] for col in range(n_cols)], axis=-1)
        o_ref[r0:r0 + NORM_ROWS, :] = _ln_silu(cv, g_ref[...], bl_ref[...])

    @pl.when(j == pl.num_programs(1) - 1)
    def _():
        buf_ref[...] = a_ref[TM - CONV_HALO:TM, :]


def _conv_prompt(tl, layer, a, w_conv, b_conv, g_ln, b_ln):
    tps = tl.tiles_per_seq
    n_cols = D_CONV // LANES
    depth = w_conv.shape[0]
    w_cols = w_conv.reshape(depth, CONV_WIDTH, n_cols, LANES).transpose(0, 2, 1, 3)
    b_cols = b_conv.reshape(depth, n_cols, 1, LANES)
    vec = pl.BlockSpec((None, 1, D_CONV), lambda b, j: (layer, 0, 0))
    return pl.pallas_call(
        _conv_prompt_kernel,
        grid=(tl.bp, tps),
        in_specs=[pl.BlockSpec((TM, D_CONV), lambda b, j: (b * tps + j, 0)),
                  pl.BlockSpec((None, n_cols, CONV_WIDTH, LANES), lambda b, j: (layer, 0, 0, 0)),
                  pl.BlockSpec((None, n_cols, 1, LANES), lambda b, j: (layer, 0, 0, 0)),
                  vec, vec],
        out_specs=[pl.BlockSpec((TM, D_CONV), lambda b, j: (b * tps + j, 0)),
                   pl.BlockSpec((None, CONV_HALO, D_CONV), lambda b, j: (b, 0, 0))],
        out_shape=[jax.ShapeDtypeStruct((tl.n_prompt, D_CONV), f32),
                   jax.ShapeDtypeStruct((tl.bp, CONV_HALO, D_CONV), f32)],
        scratch_shapes=[pltpu.VMEM((n_cols, HALO_PAD + TM, LANES), f32), pltpu.VMEM((n_cols, TM, LANES), f32)],
        compiler_params=_cparams(2),
        name="conv_prompt",
    )(a, w_cols, b_cols, g_ln, b_ln)


def _conv_sample_kernel(a_ref, st_ref, w_ref, b_ref, g_ref, bl_ref, o_ref, buf_ref, full_ref, cv_ref, *, ts):
    for s in range(SAMPLE_GROUP):
        for col in range(D_CONV // LANES):
            lanes = slice(col * LANES, (col + 1) * LANES)
            full_ref[col, 0:CONV_HALO, :] = st_ref[s, :, lanes]
            full_ref[col, CONV_HALO:CONV_HALO + ts, :] = a_ref[s * ts:(s + 1) * ts, lanes]
            buf_ref[s, :, lanes] = full_ref[col, ts:ts + CONV_HALO, :]
        cv_ref[s * ts:(s + 1) * ts, :] = _conv_taps(lambda col, tap: full_ref[col, tap:tap + ts, :],
                                                    w_ref, b_ref, ts)
    o_ref[...] = _ln_silu(cv_ref[...], g_ref[...], bl_ref[...])


def _conv_sample(tl, layer, a, state_conv, w_conv, b_conv, g_ln, b_ln):
    ts = tl.ts
    rows = SAMPLE_GROUP * ts
    first = tl.n_prompt // rows
    vec = pl.BlockSpec((None, 1, D_CONV), lambda i: (layer, 0, 0))
    return pl.pallas_call(
        functools.partial(_conv_sample_kernel, ts=ts),
        grid=(tl.bs // SAMPLE_GROUP,),
        in_specs=[pl.BlockSpec((rows, D_CONV), lambda i: (first + i, 0)),
                  pl.BlockSpec((None, SAMPLE_GROUP, CONV_HALO, D_CONV), lambda i: (layer, i, 0, 0)),
                  pl.BlockSpec((None, CONV_WIDTH, D_CONV), lambda i: (layer, 0, 0)),
                  vec, vec, vec],
        out_specs=[pl.BlockSpec((rows, D_CONV), lambda i: (i, 0)),
                   pl.BlockSpec((SAMPLE_GROUP, CONV_HALO, D_CONV), lambda i: (i, 0, 0))],
        out_shape=[jax.ShapeDtypeStruct((tl.n_sample, D_CONV), f32),
                   jax.ShapeDtypeStruct((tl.bs, CONV_HALO, D_CONV), f32)],
        scratch_shapes=[pltpu.VMEM((D_CONV // LANES, CONV_HALO + ts + SUBLANES, LANES), f32),
                        pltpu.VMEM((rows, D_CONV), f32)],
        compiler_params=_cparams(1),
        name="conv_sample",
    )(a, state_conv, w_conv, b_conv, g_ln, b_ln)


SORT_ROWS = 2 * TM + N_EXPERTS * SUBLANES
XS_W = D_MODEL + LANES


def _split3(x):
    a = x.astype(bf16)
    r = x - a.astype(f32)
    b = r.astype(bf16)
    c = (r - b.astype(f32)).astype(bf16)
    return a, b, c


def _first_of4(vals, m):
    return jnp.where(vals[0] == m, 0.0, jnp.where(vals[1] == m, 1.0, jnp.where(vals[2] == m, 2.0, 3.0)))


def _rows_to_tile(rows, n_rows):
    sub = lax.broadcasted_iota(i32, (n_rows, TM), 0)
    out = jnp.zeros((n_rows, TM), f32)
    for r, val in enumerate(rows):
        out = jnp.where(sub == r, val, out)
    return out


def _outproj_kernel(*refs, np_steps, split_x):
    if split_x:
        xp_ref, xs_ref = refs[:2]
        refs = refs[2:]
    else:
        x_ref = refs[0]
        refs = refs[1:]
    (retp_ref, rets_ref, cvp_ref, cvs_ref, gts_ref, gtt_ref, shs_ref, sht_ref, scs_ref, sct_ref,
     g_ref, wo_ref, wrh_ref, wrl_ref, br_ref,
     xo_ref, h2_ref, rows_ref, cols_ref, cnt_ref) = refs
    is_s = pl.program_id(0) >= np_steps

    for t in range(STEP_TILES):
        rs = slice(t * TM, (t + 1) * TM)
        x = jnp.where(is_s, xs_ref[rs, :], xp_ref[rs, :]) if split_x else x_ref[rs, :]
        ret = jnp.where(is_s, rets_ref[rs, :], retp_ref[rs, :]).astype(bf16)
        cv = jnp.where(is_s, cvs_ref[rs, :], cvp_ref[rs, :]).astype(bf16)
        mix_out = (jnp.dot(ret, wo_ref[0:D_RET, :], preferred_element_type=f32)
                   + jnp.dot(cv, wo_ref[D_RET:D_RET + D_CONV, :], preferred_element_type=f32))
        xn = x + _pick(is_s, gtt_ref, gts_ref, rs) * mix_out
        xo_ref[rs, :] = xn
        h2 = (_rms(xn, g_ref[...]) * (1.0 + _pick(is_s, sct_ref, scs_ref, rs))
              + _pick(is_s, sht_ref, shs_ref, rs))
        h_hi = h2.astype(bf16)
        h2_ref[rs, :] = h_hi
        rows, cols, cnt = _route_tile(h2, h_hi, wrh_ref, wrl_ref, br_ref)
        rows_ref[t] = rows
        cols_ref[rs, :] = cols
        cnt_ref[t] = cnt


def _route_tile(h2, h_hi, wrh_ref, wrl_ref, br_ref):
    h_lo = (h2 - h_hi.astype(f32)).astype(bf16)
    logits = (jnp.dot(h_hi, wrh_ref[...], preferred_element_type=f32)
              + jnp.dot(h_hi, wrl_ref[...], preferred_element_type=f32)
              + jnp.dot(h_lo, wrh_ref[...], preferred_element_type=f32)) + br_ref[...]
    lt = logits.T
    row = [lt[e:e + 1, :] for e in range(N_EXPERTS)]
    top = functools.reduce(jnp.maximum, row)
    ex = [jnp.exp(r - top) for r in row]
    den = functools.reduce(jnp.add, ex)
    p = [v / den for v in ex]

    best = None
    for g in range(N_GROUPS):
        a = p[g * GROUP_SIZE:(g + 1) * GROUP_SIZE]
        m1 = functools.reduce(jnp.maximum, a)
        i1 = _first_of4(a, m1)
        b = [jnp.where(i1 == float(j), -1.0, a[j]) for j in range(GROUP_SIZE)]
        m2 = functools.reduce(jnp.maximum, b)
        i2 = _first_of4(b, m2)
        cand = (m1 + m2, m1, m2, i1 + float(g * GROUP_SIZE), i2 + float(g * GROUP_SIZE))
        if best is None:
            best = cand
        else:
            take = cand[0] > best[0]
            best = tuple(jnp.where(take, c, o) for c, o in zip(cand, best))
    _, m1, m2, e0, e1 = best
    denom = m1 + m2
    w0 = m1 / denom
    w1 = m2 / denom

    ex_id = lax.broadcasted_iota(i32, (N_EXPERTS, TM), 0).astype(f32)
    sel0 = ex_id == e0
    sel1 = ex_id == e1
    ind = jnp.where(sel0 | sel1, 1.0, 0.0)
    t_r = lax.broadcasted_iota(i32, (TM, TM), 0)
    t_c = lax.broadcasted_iota(i32, (TM, TM), 1)
    earlier = jnp.where(t_r < t_c, 1.0, 0.0).astype(bf16)
    prefix = jnp.dot(ind.astype(bf16), earlier, preferred_element_type=f32)
    cnt = jnp.sum(ind, axis=-1, keepdims=True)
    cnt8 = jnp.floor((cnt + float(SUBLANES - 1)) * (1.0 / SUBLANES)) * float(SUBLANES)
    e_r = lax.broadcasted_iota(i32, (N_EXPERTS, N_EXPERTS), 0)
    e_c = lax.broadcasted_iota(i32, (N_EXPERTS, N_EXPERTS), 1)
    below = jnp.where(e_c < e_r, 1.0, 0.0).astype(bf16)
    seg_off = jnp.dot(below, jnp.broadcast_to(cnt8, (N_EXPERTS, TM)).astype(bf16),
                      preferred_element_type=f32)
    where_to = seg_off + prefix
    pos0 = jnp.sum(jnp.where(sel0, where_to, 0.0), axis=0, keepdims=True)
    pos1 = jnp.sum(jnp.where(sel1, where_to, 0.0), axis=0, keepdims=True)

    w0p = [v.astype(f32) for v in _split3(w0)]
    w1p = [v.astype(f32) for v in _split3(w1)]
    info = [pos0, pos1] + w0p + w1p
    return (_rows_to_tile(info, SUBLANES), _rows_to_tile(info, LANES).T,
            jnp.broadcast_to(cnt, (N_EXPERTS, LANES)))


def _outproj(tl, layer, x, ret_p, ret_s, cv_p, cv_s, mod_seq, mod_tok, g_norm, w_out_bf, wr_hi, wr_lo, br_pad):
    split_x = isinstance(x, tuple)
    tok_spec = pl.BlockSpec((TS, D_MODEL), lambda i: (i, 0))
    p_spec = lambda w: pl.BlockSpec((TS, w), lambda i: (tl.prompt_block(i), 0))
    s_spec = lambda w: pl.BlockSpec((TS, w), lambda i: (tl.sample_block(i), 0))
    if split_x:
        x_args, x_specs = list(x), [p_spec(D_MODEL), s_spec(D_MODEL)]
    else:
        x_args, x_specs = [x], [tok_spec]
    mods = []
    for col in (2, 3, 4):
        mods += list(_mod_specs(tl, layer, col))
    wr_spec = pl.BlockSpec((D_MODEL, LANES), lambda i: (0, 0))
    return pl.pallas_call(
        functools.partial(_outproj_kernel, np_steps=tl.np_steps, split_x=split_x),
        grid=(tl.n_steps,),
        in_specs=x_specs + [p_spec(D_RET), s_spec(D_RET), p_spec(D_CONV), s_spec(D_CONV)] + mods + [
            pl.BlockSpec((None, 1, D_MODEL), lambda i: (layer, 0, 0)),
            pl.BlockSpec((None, D_MODEL, D_MODEL), lambda i: (layer, 0, 0)),
            wr_spec, wr_spec,
            pl.BlockSpec((1, LANES), lambda i: (0, 0)),
        ],
        out_specs=[tok_spec, tok_spec,
                   pl.BlockSpec((STEP_TILES, SUBLANES, TM), lambda i: (i, 0, 0)),
                   pl.BlockSpec((TS, LANES), lambda i: (i, 0)),
                   pl.BlockSpec((STEP_TILES, N_EXPERTS, LANES), lambda i: (i, 0, 0))],
        out_shape=[jax.ShapeDtypeStruct((tl.n_tok, D_MODEL), f32),
                   jax.ShapeDtypeStruct((tl.n_tok, D_MODEL), bf16),
                   jax.ShapeDtypeStruct((tl.n_tiles, SUBLANES, TM), f32),
                   jax.ShapeDtypeStruct((tl.n_tok, LANES), f32),
                   jax.ShapeDtypeStruct((tl.n_tiles, N_EXPERTS, LANES), f32)],
        compiler_params=_cparams(1),
        name="outproj_router",
    )(*x_args, ret_p, ret_s, cv_p, cv_s, *([mod_seq, mod_tok] * 3), g_norm, w_out_bf, wr_hi, wr_lo, br_pad)


N_CHUNKS = SORT_ROWS // SUBLANES


class _Layout:
    def __init__(self, n_tiles):
        self.n_tiles = n_tiles
        self.tail_start = n_tiles * N_CHUNKS
        self.tail_n8 = self.tail_start + N_EXPERTS
        worst = 2 * n_tiles * TM + n_tiles * N_EXPERTS * (SUBLANES - 1) + N_EXPERTS * (BM - SUBLANES)
        self.n_blocks = -(-worst // BM)
        self.cap = self.n_blocks * BM
        self.dump = self.cap
        self.xs_rows = self.cap + -(-2 * STEP_TILES * SORT_ROWS // BM) * BM


def _moe_tables(lay, tile_counts):
    c8 = ((tile_counts.astype(i32) + SUBLANES - 1) // SUBLANES) * SUBLANES
    base8 = jnp.cumsum(c8, axis=0) - c8
    tot8 = jnp.sum(c8, axis=0)
    region = ((tot8 + BM - 1) // BM) * BM
    g_end = jnp.cumsum(region)
    g_start = g_end - region
    seg_end = jnp.cumsum(c8, axis=1)
    seg_dst = g_start[None, :] + base8
    n_used = g_end[-1] // BM
    blk = jnp.arange(lay.n_blocks, dtype=i32)
    block_e = jnp.minimum(jnp.sum((g_end[None, :] <= blk[:, None] * BM).astype(i32), axis=1), N_EXPERTS - 1)
    block_e = jnp.where(blk < n_used, block_e, block_e[n_used - 1])
    row0 = jnp.arange(N_CHUNKS, dtype=i32) * SUBLANES
    owner = jnp.sum((seg_end[:, None, :] <= row0[None, :, None]).astype(i32), axis=-1)
    onehot = (owner[:, :, None] == jnp.arange(N_EXPERTS, dtype=i32)[None, None, :]).astype(i32)
    delta = seg_dst - (seg_end - c8)
    chunk_dst = jnp.where(owner < N_EXPERTS, row0[None, :] + jnp.sum(onehot * delta[:, None, :], axis=-1), -1)
    tab = jnp.concatenate([chunk_dst.ravel(), g_start + tot8, (region - tot8) // SUBLANES]).astype(i32)
    ids = jnp.arange(N_EXPERTS, dtype=i32)
    later = jnp.where((ids[None, :] > ids[:, None]) & (region[None, :] > 0), ids[None, :], N_EXPERTS)
    next_e = jnp.min(later, axis=1)
    next_e = jnp.where(next_e == N_EXPERTS, -1, next_e)
    return tab, jnp.concatenate([block_e, n_used[None], next_e]).astype(i32)


def _for_chunks(n, fn):
    def body(c, carry):
        fn(c)
        return carry

    lax.fori_loop(0, n, body, 0)


def _dispatch_kernel(tab_ref, h2_ref, rows_ref, cols_ref, xs_hbm, sorted_ref, zero_ref, sem, *, lay, n_steps):
    i = pl.program_id(0)
    slot = i % 2
    step_rows = STEP_TILES * SORT_ROWS

    def tail_copy(dst):
        return pltpu.make_async_copy(zero_ref, xs_hbm.at[pl.ds(dst, SUBLANES)], sem.at[2])

    def wait_step(slot_):
        pltpu.make_async_copy(sorted_ref.at[slot_], xs_hbm.at[pl.ds(0, step_rows)], sem.at[slot_]).wait()

    @pl.when(i == 0)
    def _():
        zero_ref[...] = jnp.zeros_like(zero_ref)
        for e in range(N_EXPERTS):
            start = tab_ref[lay.tail_start + e]
            _for_chunks(tab_ref[lay.tail_n8 + e],
                        lambda c: tail_copy(pl.multiple_of(start + c * SUBLANES, SUBLANES)).start())

    @pl.when(i >= 2)
    def _():
        wait_step(slot)

    r_id = lax.broadcasted_iota(i32, (SORT_ROWS, TM), 0).astype(f32)
    lane = lax.broadcasted_iota(i32, (TM, LANES), 1)
    for t in range(STEP_TILES):
        pos0 = rows_ref[t, 0:1, :]
        pos1 = rows_ref[t, 1:2, :]
        p0 = r_id == pos0
        p1 = r_id == pos1
        perm = jnp.where(p0 | p1, 1.0, 0.0).astype(bf16)
        cols = cols_ref[t * TM:(t + 1) * TM, :]
        wpart0 = jnp.where((lane >= 2) & (lane < 5), cols, 0.0).astype(bf16)
        wpart1 = jnp.where((lane >= 5) & (lane < 8), cols, 0.0).astype(bf16)
        sw = (jnp.dot(jnp.where(p0, 1.0, 0.0).astype(bf16), wpart0, preferred_element_type=f32)
              + jnp.dot(jnp.where(p1, 1.0, 0.0).astype(bf16), wpart1, preferred_element_type=f32))
        base = t * SORT_ROWS
        sorted_ref[slot, base:base + SORT_ROWS, 0:D_MODEL] = jnp.dot(
            perm, h2_ref[t * TM:(t + 1) * TM, :], preferred_element_type=f32)
        sorted_ref[slot, base:base + SORT_ROWS, D_MODEL:XS_W] = jnp.broadcast_to(
            jnp.sum(sw, axis=-1, keepdims=True), (SORT_ROWS, LANES))

    for t in range(STEP_TILES):
        for c in range(N_CHUNKS):
            row = t * SORT_ROWS + c * SUBLANES
            dst = tab_ref[(i * STEP_TILES + t) * N_CHUNKS + c]
            dst = jnp.where(dst < 0, lay.dump + slot * step_rows + row, dst)
            pltpu.make_async_copy(sorted_ref.at[slot, pl.ds(row, SUBLANES)],
                                  xs_hbm.at[pl.ds(pl.multiple_of(dst, SUBLANES), SUBLANES)], sem.at[slot]).start()

    @pl.when(i == n_steps - 1)
    def _():
        if n_steps >= 2:
            wait_step(1 - slot)
        wait_step(slot)
        for e in range(N_EXPERTS):
            _for_chunks(tab_ref[lay.tail_n8 + e], lambda c: tail_copy(0).wait())


def _dispatch(tl, lay, tab, h2, rows, cols):
    grid_spec = pltpu.PrefetchScalarGridSpec(
        num_scalar_prefetch=1,
        grid=(tl.n_steps,),
        in_specs=[pl.BlockSpec((TS, D_MODEL), lambda i, t: (i, 0)),
                  pl.BlockSpec((STEP_TILES, SUBLANES, TM), lambda i, t: (i, 0, 0)),
                  pl.BlockSpec((TS, LANES), lambda i, t: (i, 0))],
        out_specs=pl.BlockSpec(memory_space=pl.ANY),
        scratch_shapes=[pltpu.VMEM((2, STEP_TILES * SORT_ROWS, XS_W), f32), pltpu.VMEM((SUBLANES, XS_W), f32),
                        pltpu.SemaphoreType.DMA((3,))],
    )
    return pl.pallas_call(
        functools.partial(_dispatch_kernel, lay=lay, n_steps=tl.n_steps),
        grid_spec=grid_spec,
        out_shape=jax.ShapeDtypeStruct((lay.xs_rows, XS_W), f32),
        compiler_params=_cparams(1),
        name="moe_dispatch",
    )(tab, h2, rows, cols)


def _expert_kernel(be_ref, xs_ref, wg_hbm, wu_hbm, wd_hbm, ys_ref, stage, w_bf, sem, *, n_blocks, layer):
    j = pl.program_id(0)

    def fetch(e):
        return [pltpu.make_async_copy(w.at[layer, e], stage.at[k], sem.at[k])
                for k, w in enumerate((wg_hbm, wu_hbm, wd_hbm))]

    @pl.when(j < be_ref[n_blocks])
    def _():
        e = be_ref[j]

        @pl.when(j == 0)
        def _():
            for copy in fetch(e):
                copy.start()

        @pl.when((j == 0) | (e != be_ref[jnp.maximum(j - 1, 0)]))
        def _():
            for copy in fetch(e):
                copy.wait()
            for k in range(3):
                w_bf[k] = stage[k].astype(bf16)
            nxt = be_ref[n_blocks + 1 + e]

            @pl.when(nxt >= 0)
            def _():
                for copy in fetch(nxt):
                    copy.start()

        x = xs_ref[:, 0:D_MODEL].astype(bf16)
        gate = jnp.dot(x, w_bf[0], preferred_element_type=f32)
        up = jnp.dot(x, w_bf[1], preferred_element_type=f32)
        mid = (jax.nn.silu(gate) * up).astype(bf16)
        ys_ref[...] = jnp.dot(mid, w_bf[2], preferred_element_type=f32) * xs_ref[:, D_MODEL:D_MODEL + 1]


def _experts(layer, lay, block_e, xs, w_gate, w_up, w_down):
    n_blocks = lay.n_blocks
    d_ff = w_gate.shape[-1]
    assert d_ff == D_MODEL
    used = lambda j, be: jnp.minimum(j, be[n_blocks] - 1)
    any_spec = pl.BlockSpec(memory_space=pl.ANY)
    grid_spec = pltpu.PrefetchScalarGridSpec(
        num_scalar_prefetch=1,
        grid=(n_blocks,),
        in_specs=[pl.BlockSpec((BM, XS_W), lambda j, be: (used(j, be), 0)), any_spec, any_spec, any_spec],
        out_specs=pl.BlockSpec((BM, D_MODEL), lambda j, be: (used(j, be), 0)),
        scratch_shapes=[pltpu.VMEM((3, D_MODEL, d_ff), f32), pltpu.VMEM((3, D_MODEL, d_ff), bf16),
                        pltpu.SemaphoreType.DMA((3,))],
    )
    return pl.pallas_call(
        functools.partial(_expert_kernel, n_blocks=n_blocks, layer=layer),
        grid_spec=grid_spec,
        out_shape=jax.ShapeDtypeStruct((lay.cap, D_MODEL), f32),
        compiler_params=_cparams(1),
        name="moe_experts",
    )(block_e, xs, w_gate, w_up, w_down)


def _combine_kernel(tab_ref, ys_hbm, cols_ref, x_ref, gts_ref, gtt_ref, *rest, n_steps, np_steps, final):
    if final:
        gf_ref, yp_ref, ysm_ref, staged, sem = rest
    else:
        xo_ref, staged, sem = rest
    i = pl.program_id(0)
    slot = i % 2
    is_s = i >= np_steps

    def start_step(step, slot_):
        for c in range(STEP_TILES * N_CHUNKS):
            src = jnp.maximum(tab_ref[step * (STEP_TILES * N_CHUNKS) + c], 0)
            pltpu.make_async_copy(ys_hbm.at[pl.ds(pl.multiple_of(src, SUBLANES), SUBLANES)],
                                  staged.at[slot_, pl.ds(c * SUBLANES, SUBLANES)], sem.at[slot_]).start()

    @pl.when(i == 0)
    def _():
        start_step(0, 0)

    @pl.when(i + 1 < n_steps)
    def _():
        start_step(i + 1, 1 - slot)

    pltpu.make_async_copy(ys_hbm.at[pl.ds(0, STEP_TILES * SORT_ROWS)], staged.at[slot], sem.at[slot]).wait()

    lane = lax.broadcasted_iota(i32, (TM, SORT_ROWS), 1).astype(f32)
    outs = []
    for t in range(STEP_TILES):
        rs = slice(t * TM, (t + 1) * TM)
        unperm = jnp.where((lane == cols_ref[rs, 0:1]) | (lane == cols_ref[rs, 1:2]), 1.0, 0.0).astype(bf16)
        parts = _split3(staged[slot, t * SORT_ROWS:(t + 1) * SORT_ROWS, :])
        ff = sum(jnp.dot(unperm, part, preferred_element_type=f32) for part in parts)
        xn = x_ref[rs, :] + _pick(is_s, gtt_ref, gts_ref, rs) * ff
        outs.append(_rms(xn, gf_ref[...]) if final else xn)
    if final:
        @pl.when(jnp.logical_not(is_s))
        def _():
            for t, y in enumerate(outs):
                yp_ref[t * TM:(t + 1) * TM, :] = y

        @pl.when(is_s)
        def _():
            for t, y in enumerate(outs):
                ysm_ref[t * TM:(t + 1) * TM, :] = y
    else:
        for t, xn in enumerate(outs):
            xo_ref[t * TM:(t + 1) * TM, :] = xn


def _combine(tl, lay, layer, tab, ys, cols, x, mod_seq, mod_tok, g_final):
    final = g_final is not None
    tok_spec = pl.BlockSpec((TS, D_MODEL), lambda i, t: (i, 0))
    gt_seq, gt_tok = _mod_specs(tl, layer, 5)
    in_specs = [pl.BlockSpec(memory_space=pl.ANY), pl.BlockSpec((TS, LANES), lambda i, t: (i, 0)),
                tok_spec, gt_seq, gt_tok]
    args = [tab, ys, cols, x, mod_seq, mod_tok]
    if final:
        in_specs.append(pl.BlockSpec((1, D_MODEL), lambda i, t: (0, 0)))
        args.append(g_final)
        out_specs = [pl.BlockSpec((TS, D_MODEL), lambda i, t: (tl.prompt_block(i), 0)),
                     pl.BlockSpec((TS, D_MODEL), lambda i, t: (tl.sample_block(i), 0))]
        out_shape = [jax.ShapeDtypeStruct((tl.n_prompt, D_MODEL), f32),
                     jax.ShapeDtypeStruct((tl.n_sample, D_MODEL), f32)]
    else:
        out_specs = tok_spec
        out_shape = jax.ShapeDtypeStruct((tl.n_tok, D_MODEL), f32)
    grid_spec = pltpu.PrefetchScalarGridSpec(
        num_scalar_prefetch=1,
        grid=(tl.n_steps,),
        in_specs=in_specs,
        out_specs=out_specs,
        scratch_shapes=[pltpu.VMEM((2, STEP_TILES * SORT_ROWS, D_MODEL), f32), pltpu.SemaphoreType.DMA((2,))],
    )
    return pl.pallas_call(
        functools.partial(_combine_kernel, n_steps=tl.n_steps, np_steps=tl.np_steps, final=final),
        grid_spec=grid_spec,
        out_shape=out_shape,
        compiler_params=_cparams(1),
        name="moe_combine",
    )(*args)


def _rope_tables(tl):
    half = HEAD_D // 2
    inv = ROPE_BASE ** (-jnp.arange(half, dtype=f32) / half)
    pos_p = jnp.arange(tl.tp, dtype=i32)
    pos_s = PAST_LEN + jnp.arange(tl.ts, dtype=i32)
    pos = jnp.concatenate([pos_p, jnp.tile(pos_s, tl.bs)])
    ang = pos.astype(f32)[:, None] * inv[None, :]
    cos, sin = jnp.cos(ang), jnp.sin(ang)
    return jnp.concatenate([cos, cos], axis=-1), jnp.concatenate([-sin, sin], axis=-1)


def kernel(x_prompt, x_sample, state_ret, state_conv, c_prompt, c_sample, w_mod, b_mod, g_mix_norm, w_in,
           w_conv, b_conv, g_conv_ln, b_conv_ln, g_ret_gn, w_out, g_ffn_norm, w_router, b_router,
           w_exp_gate, w_exp_up, w_exp_down, g_final):
    bp, tp, _ = x_prompt.shape
    bs, ts, _ = x_sample.shape
    depth = w_mod.shape[0]
    tl = _Tiles(bp, tp, bs, ts)
    lay = _Layout(tl.n_tiles)

    c_all = jnp.concatenate([c_prompt, jnp.repeat(c_sample, ts, axis=0)], axis=0)
    mod = _modulation(c_all, w_mod, b_mod)
    mod_seq = mod[:, :bp].reshape(depth, bp, 1, N_MOD * D_MODEL)
    mod_tok = mod[:, bp:]

    cos_tab, sin_tab = _rope_tables(tl)
    w_in_bf = w_in.astype(bf16)
    w_out_bf = w_out.astype(bf16)
    wr_pad = jnp.pad(w_router.astype(f32), ((0, 0), (0, LANES - N_EXPERTS)))
    wr_hi = wr_pad.astype(bf16)
    wr_lo = (wr_pad - wr_hi.astype(f32)).astype(bf16)
    br_pad = jnp.pad(b_router.astype(f32), (0, LANES - N_EXPERTS)).reshape(1, LANES)
    vec3 = lambda t: t.reshape(depth, 1, t.shape[-1])
    g_mix3, g_ffn3, gn3 = vec3(g_mix_norm), vec3(g_ffn_norm), vec3(g_ret_gn)
    b_conv3, g_ln3, b_ln3 = vec3(b_conv), vec3(g_conv_ln), vec3(b_conv_ln)

    x = (x_prompt.reshape(tl.n_prompt, D_MODEL), x_sample.reshape(tl.n_sample, D_MODEL))
    ret_p, conv_p, conv_s = [], [], []
    ret_s_all = None
    for layer in range(depth):
        q, k, v, gate, a = _inproj(tl, layer, x, mod_seq, mod_tok, g_mix3, w_in_bf, cos_tab, sin_tab)
        ro_p, s_p = _retention_prompt(tl, layer, q, k, v, gate, gn3)
        ro_s, ret_s_all = _retention_sample(tl, layer, q, k, v, gate, state_ret, gn3, ret_s_all)
        co_p, buf_p = _conv_prompt(tl, layer, a, w_conv, b_conv3, g_ln3, b_ln3)
        co_s, buf_s = _conv_sample(tl, layer, a, state_conv, w_conv, b_conv3, g_ln3, b_ln3)
        x_mid, h2, rows, cols, tile_counts = _outproj(
            tl, layer, x, ro_p, ro_s, co_p, co_s, mod_seq, mod_tok, g_ffn3, w_out_bf, wr_hi, wr_lo, br_pad)
        tab, block_e = _moe_tables(lay, tile_counts[:, :, 0])
        xs = _dispatch(tl, lay, tab, h2, rows, cols)
        ys = _experts(layer, lay, block_e, xs, w_exp_gate, w_exp_up, w_exp_down)
        last = layer == depth - 1
        x = _combine(tl, lay, layer, tab, ys, cols, x_mid, mod_seq, mod_tok,
                     g_final.reshape(1, D_MODEL) if last else None)
        ret_p.append(s_p)
        conv_p.append(buf_p)
        conv_s.append(buf_s)
    y_p, y_s = x
    return (y_p.reshape(bp, tp, D_MODEL), y_s.reshape(bs, ts, D_MODEL),
            jnp.stack(ret_p), jnp.stack(conv_p), ret_s_all, jnp.stack(conv_s))
```

```python
import functools

import jax
import jax.numpy as jnp
from jax import lax
from jax.experimental import pallas as pl
from jax.experimental.pallas import tpu as pltpu

f32 = jnp.float32
bf16 = jnp.bfloat16
i32 = jnp.int32

D_MODEL = 1024
D_RET = 512
D_CONV = 512
N_HEADS = 4
HEAD_D = 128
RET_CHUNK = 128
RET_CHUNKS_PER_STEP = 4
ROPE_BASE = 10000.0
CONV_WIDTH = 31
CONV_HALO = CONV_WIDTH - 1
N_EXPERTS = 16
N_GROUPS = 4
GROUP_SIZE = N_EXPERTS // N_GROUPS
N_MOD = 6
EPS = 1e-6
PAST_LEN = 16384
D_IN = 4 * D_RET + 2 * D_CONV

LANES = 128
SUBLANES = 8
TM = 256
STEP_TILES = 2
TS = TM * STEP_TILES
BM = 512
CONV_ROWS = 128
NORM_ROWS = 64
SAMPLE_GROUP = 8
HALO_PAD = 32
VMEM_LIMIT = 56 * 1024 * 1024


def _cparams(n_axes, vmem=VMEM_LIMIT):
    return pltpu.CompilerParams(dimension_semantics=("arbitrary",) * n_axes, vmem_limit_bytes=vmem)


def _mod_kernel(c_ref, w_ref, b_ref, seq_ref, tok_ref):
    cond = jax.nn.silu(c_ref[...]).astype(bf16)
    mod = jnp.dot(cond, w_ref[...].astype(bf16), preferred_element_type=f32) + b_ref[...]
    n_seq = seq_ref.shape[0]
    seq_ref[...] = mod[0:n_seq, :]
    tok_ref[...] = mod[n_seq:, :]


def _modulation(c_all, n_seq, w_mod, b_mod):
    depth = w_mod.shape[0]
    m = c_all.shape[0]
    assert n_seq % SUBLANES == 0
    return pl.pallas_call(
        _mod_kernel,
        grid=(depth, N_MOD),
        in_specs=[
            pl.BlockSpec((m, D_MODEL), lambda l, j: (0, 0)),
            pl.BlockSpec((None, D_MODEL, D_MODEL), lambda l, j: (l, 0, j)),
            pl.BlockSpec((None, 1, D_MODEL), lambda l, j: (l, 0, j)),
        ],
        out_specs=[pl.BlockSpec((None, n_seq, D_MODEL), lambda l, j: (l, 0, j)),
                   pl.BlockSpec((None, m - n_seq, D_MODEL), lambda l, j: (l, 0, j))],
        out_shape=[jax.ShapeDtypeStruct((depth, n_seq, N_MOD * D_MODEL), f32),
                   jax.ShapeDtypeStruct((depth, m - n_seq, N_MOD * D_MODEL), f32)],
        compiler_params=_cparams(2),
        name="modulation",
    )(c_all, w_mod, b_mod.reshape(depth, 1, N_MOD * D_MODEL))


class _Tiles:
    def __init__(self, bp, tp, bs, ts):
        self.bp, self.tp, self.bs, self.ts = bp, tp, bs, ts
        self.n_prompt = bp * tp
        self.n_sample = bs * ts
        self.n_tok = self.n_prompt + self.n_sample
        assert tp % TS == 0 and self.n_sample % TS == 0
        self.tiles_per_seq = tp // TM
        self.np_tiles = self.n_prompt // TM
        self.n_tiles = self.n_tok // TM
        self.steps_per_seq = tp // TS
        self.np_steps = self.n_prompt // TS
        self.n_steps = self.n_tok // TS

    def prompt_block(self, i):
        return jnp.minimum(i, self.np_steps - 1)

    def sample_block(self, i):
        return jnp.maximum(i - self.np_steps, 0)

    def seq_index(self, i):
        return jnp.minimum(i // self.steps_per_seq, self.bp - 1)


def _mod_specs(tl, layer, col):
    seq = pl.BlockSpec((None, None, 1, D_MODEL), lambda i, *_: (layer, tl.seq_index(i), 0, col))
    tok = pl.BlockSpec((None, TS, D_MODEL), lambda i, *_: (layer, tl.sample_block(i), col))
    return seq, tok


def _pick(is_sample, tok_ref, seq_ref, rows=slice(None)):
    return jnp.where(is_sample, tok_ref[rows, :], seq_ref[...])


def _rms(x, g):
    return x * lax.rsqrt(jnp.mean(x * x, axis=-1, keepdims=True) + EPS) * g


def _inproj_kernel(*refs, np_steps, split_x):
    if split_x:
        xp_ref, xs_ref = refs[:2]
        refs = refs[2:]
    else:
        x_ref = refs[0]
        refs = refs[1:]
    (shs_ref, sht_ref, scs_ref, sct_ref, g_ref, w_ref, cos_ref, sin_ref,
     q_ref, k_ref, v_ref, gate_ref, a_ref) = refs
    is_s = pl.program_id(0) >= np_steps
    x = jnp.where(is_s, xs_ref[...], xp_ref[...]) if split_x else x_ref[...]
    h = _rms(x, g_ref[...]) * (1.0 + _pick(is_s, sct_ref, scs_ref)) + _pick(is_s, sht_ref, shs_ref)
    hb = h.astype(bf16)
    cos = cos_ref[...]
    sin = sin_ref[...]

    def proj(col):
        return jnp.dot(hb, w_ref[:, col * D_RET:(col + 1) * D_RET], preferred_element_type=f32)

    def rope(t):
        outs = []
        for hd in range(N_HEADS):
            th = t[:, hd * HEAD_D:(hd + 1) * HEAD_D]
            outs.append(th * cos + pltpu.roll(th, HEAD_D // 2, 1) * sin)
        return outs

    for hd, qh in enumerate(rope(proj(0))):
        q_ref[:, hd * HEAD_D:(hd + 1) * HEAD_D] = qh.astype(bf16)
    for hd, kh in enumerate(rope(proj(1))):
        k_ref[:, hd * HEAD_D:(hd + 1) * HEAD_D] = kh * (HEAD_D ** -0.5)
    v_ref[...] = proj(2).astype(bf16)
    gate_ref[...] = proj(3)
    a_ref[...] = proj(4) * jax.nn.sigmoid(proj(5))


def _inproj(tl, layer, x, mod_seq, mod_tok, g_norm, w_in_bf, cos_tab, sin_tab):
    split_x = isinstance(x, tuple)
    tok_spec = pl.BlockSpec((TS, D_MODEL), lambda i: (i, 0))
    if split_x:
        x_args = list(x)
        x_specs = [pl.BlockSpec((TS, D_MODEL), lambda i: (tl.prompt_block(i), 0)),
                   pl.BlockSpec((TS, D_MODEL), lambda i: (tl.sample_block(i), 0))]
    else:
        x_args, x_specs = [x], [tok_spec]
    sh_seq, sh_tok = _mod_specs(tl, layer, 0)
    sc_seq, sc_tok = _mod_specs(tl, layer, 1)

    def table_block(i):
        return jnp.where(i < tl.np_steps, i % tl.steps_per_seq, tl.steps_per_seq + tl.sample_block(i))

    tab_spec = pl.BlockSpec((TS, HEAD_D), lambda i: (table_block(i), 0))
    out_spec = pl.BlockSpec((TS, D_RET), lambda i: (i, 0))
    out_sd = lambda dt: jax.ShapeDtypeStruct((tl.n_tok, D_RET), dt)
    return pl.pallas_call(
        functools.partial(_inproj_kernel, np_steps=tl.np_steps, split_x=split_x),
        grid=(tl.n_steps,),
        in_specs=x_specs + [
            sh_seq, sh_tok, sc_seq, sc_tok,
            pl.BlockSpec((None, 1, D_MODEL), lambda i: (layer, 0, 0)),
            pl.BlockSpec((None, D_MODEL, D_IN), lambda i: (layer, 0, 0)),
            tab_spec, tab_spec,
        ],
        out_specs=[out_spec] * 5,
        out_shape=[out_sd(bf16), out_sd(f32), out_sd(bf16), out_sd(f32), out_sd(f32)],
        compiler_params=_cparams(1),
        name="inproj",
    )(*x_args, mod_seq, mod_tok, mod_seq, mod_tok, g_norm, w_in_bf, cos_tab, sin_tab)


def _head_norm_gate(o, gn, gate):
    mu = jnp.mean(o, axis=-1, keepdims=True)
    var = jnp.mean(jnp.square(o - mu), axis=-1, keepdims=True)
    return jax.nn.silu(gate) * ((o - mu) * lax.rsqrt(var + EPS) * gn)


def _dot_nt(a, b):
    return lax.dot_general(a, b, (((1,), (1,)), ((), ())), preferred_element_type=f32)


def _dot_tn(a, b):
    return lax.dot_general(a, b, (((0,), (0,)), ((), ())), preferred_element_type=f32)


def _ret_prompt_kernel(q_ref, k_ref, v_ref, gate_ref, dec_ref, qd_ref, kd_ref, cd_ref, gn_ref,
                       o_ref, s_out_ref, s_ref):
    c = pl.program_id(1)

    @pl.when(c == 0)
    def _():
        s_ref[...] = jnp.zeros_like(s_ref)

    for ci in range(RET_CHUNKS_PER_STEP):
        rows = slice(ci * RET_CHUNK, (ci + 1) * RET_CHUNK)
        for hd in range(N_HEADS):
            sl = slice(hd * HEAD_D, (hd + 1) * HEAD_D)
            kh = k_ref[rows, sl]
            qb = q_ref[rows, sl]
            kb = kh.astype(bf16)
            vb = v_ref[rows, sl]
            s_old = s_ref[hd]
            scores = _dot_nt(qb, kb) * dec_ref[hd]
            inner = jnp.dot(scores.astype(bf16), vb, preferred_element_type=f32)
            cross = jnp.dot(qb, s_old.astype(bf16), preferred_element_type=f32) * qd_ref[hd]
            s_ref[hd] = s_old * cd_ref[hd] + _dot_tn((kh * kd_ref[hd]).astype(bf16), vb)
            o_ref[rows, sl] = _head_norm_gate(inner + cross, gn_ref[:, sl], gate_ref[rows, sl]).astype(bf16)

    @pl.when(c == pl.num_programs(1) - 1)
    def _():
        s_out_ref[...] = s_ref[...]


def _decay_tables(chunk, true_len):
    lg = jnp.log(1.0 - 2.0 ** (-5.0 - jnp.arange(N_HEADS, dtype=f32)))
    idx = jnp.arange(chunk, dtype=f32)
    rel = idx[:, None] - idx[None, :]
    decay = jnp.where(rel[None] >= 0, jnp.exp(jnp.maximum(rel, 0.0)[None] * lg[:, None, None]), 0.0)
    q_decay = jnp.exp((idx[None, :] + 1.0) * lg[:, None])
    k_decay = jnp.exp((true_len - 1.0 - idx[None, :]) * lg[:, None])
    c_decay = jnp.exp(true_len * lg)
    return decay, q_decay, k_decay, c_decay


def _retention_prompt(tl, layer, q, k, v, gate, g_ret_gn):
    step_rows = RET_CHUNK * RET_CHUNKS_PER_STEP
    assert tl.tp % step_rows == 0
    n_chunks = tl.tp // step_rows
    decay, q_decay, k_decay, c_decay = _decay_tables(RET_CHUNK, RET_CHUNK)
    bcast = lambda t: jnp.broadcast_to(t[:, :, None], (N_HEADS, RET_CHUNK, HEAD_D))
    cd = jnp.broadcast_to(c_decay[:, None, None], (N_HEADS, 1, HEAD_D))
    tok_spec = pl.BlockSpec((step_rows, D_RET), lambda b, c: (b * n_chunks + c, 0))
    tab_spec = pl.BlockSpec((N_HEADS, RET_CHUNK, HEAD_D), lambda b, c: (0, 0, 0))
    return pl.pallas_call(
        _ret_prompt_kernel,
        grid=(tl.bp, n_chunks),
        in_specs=[tok_spec] * 4 + [tab_spec] * 3 + [
            pl.BlockSpec((N_HEADS, 1, HEAD_D), lambda b, c: (0, 0, 0)),
            pl.BlockSpec((None, 1, D_RET), lambda b, c: (layer, 0, 0)),
        ],
        out_specs=[tok_spec, pl.BlockSpec((None, N_HEADS, HEAD_D, HEAD_D), lambda b, c: (b, 0, 0, 0))],
        out_shape=[jax.ShapeDtypeStruct((tl.n_prompt, D_RET), bf16),
                   jax.ShapeDtypeStruct((tl.bp, N_HEADS, HEAD_D, HEAD_D), f32)],
        scratch_shapes=[pltpu.VMEM((N_HEADS, HEAD_D, HEAD_D), f32)],
        compiler_params=_cparams(2),
        name="retention_prompt",
    )(q, k, v, gate, decay, bcast(q_decay), bcast(k_decay), cd, g_ret_gn)


def _ret_sample_kernel(q_ref, k_ref, v_ref, gate_ref, s_in_ref, dec_ref, qd_ref, kd_ref, cd_ref, gn_ref,
                       *rest, ts):
    o_ref, s_all_ref = rest[-2:]
    s_out_ref = s_all_ref.at[0]
    for other in range(1, s_all_ref.shape[0]):
        s_all_ref[other] = jnp.zeros(s_all_ref.shape[1:], f32)
    seqs_per_tile = SUBLANES // ts
    row = lax.broadcasted_iota(i32, (SUBLANES, HEAD_D), 0)
    q_all = q_ref[...].astype(f32)
    v_all = v_ref[...].astype(f32)
    outs = []
    for t in range(SAMPLE_GROUP // seqs_per_tile):
        rows = slice(t * SUBLANES, (t + 1) * SUBLANES)
        heads = []
        for hd in range(N_HEADS):
            sl = slice(hd * HEAD_D, (hd + 1) * HEAD_D)
            qh = q_all[rows, sl]
            kh = k_ref[rows, sl] * kd_ref[hd]
            vb = v_all[rows, sl].astype(bf16)
            qb = qh.astype(bf16)
            scores = _dot_nt(qb, k_ref[rows, sl].astype(bf16)) * dec_ref[hd]
            o = jnp.dot(scores.astype(bf16), vb, preferred_element_type=f32)
            for s in range(seqs_per_tile):
                b = t * seqs_per_tile + s
                mine = (row >= s * ts) & (row < (s + 1) * ts)
                s_old = s_in_ref[b, hd]
                q_s = jnp.where(mine, qh, 0.0).astype(bf16)
                k_s = jnp.where(mine, kh, 0.0).astype(bf16)
                o = o + jnp.dot(q_s, s_old.astype(bf16), preferred_element_type=f32) * qd_ref[hd]
                s_out_ref[b, hd] = s_old * cd_ref[hd] + _dot_tn(k_s, vb)
            heads.append(_head_norm_gate(o, gn_ref[:, sl], gate_ref[rows, sl]))
        outs.append(jnp.concatenate(heads, axis=-1))
    o_ref[...] = jnp.concatenate(outs, axis=0).astype(bf16)


def _retention_sample(tl, layer, q, k, v, gate, state_ret, g_ret_gn, prev_states):
    ts = tl.ts
    depth = state_ret.shape[0]
    assert SUBLANES % ts == 0 and tl.bs % SAMPLE_GROUP == 0
    seqs_per_tile = SUBLANES // ts
    decay, q_decay, k_decay, c_decay = _decay_tables(ts, ts)
    eye = jnp.eye(seqs_per_tile, dtype=f32)
    dec_tile = jnp.einsum("ab,hij->haibj", eye, decay).reshape(N_HEADS, SUBLANES, SUBLANES)
    tile_rows = lambda t: jnp.broadcast_to(jnp.tile(t, (1, seqs_per_tile))[:, :, None],
                                           (N_HEADS, SUBLANES, HEAD_D))
    cd = jnp.broadcast_to(c_decay[:, None, None], (N_HEADS, 1, HEAD_D))
    rows = SAMPLE_GROUP * ts
    first = tl.n_prompt // rows
    tok_spec = pl.BlockSpec((rows, D_RET), lambda i: (first + i, 0))
    const3 = lambda shape: pl.BlockSpec(shape, lambda i: (0, 0, 0))
    st_block = (SAMPLE_GROUP, N_HEADS, HEAD_D, HEAD_D)
    in_specs = [tok_spec] * 4 + [
        pl.BlockSpec((None,) + st_block, lambda i: (layer, i, 0, 0, 0)),
        const3((N_HEADS, SUBLANES, SUBLANES)),
        const3((N_HEADS, SUBLANES, HEAD_D)),
        const3((N_HEADS, SUBLANES, HEAD_D)),
        const3((N_HEADS, 1, HEAD_D)),
        pl.BlockSpec((None, 1, D_RET), lambda i: (layer, 0, 0)),
    ]
    args = [q, k, v, gate, state_ret, dec_tile, tile_rows(q_decay), tile_rows(k_decay), cd, g_ret_gn]
    if prev_states is None:
        state_spec = pl.BlockSpec((depth,) + st_block, lambda i: (0, i, 0, 0, 0))
        aliases = {}
    else:
        state_spec = pl.BlockSpec((1,) + st_block, lambda i: (layer, i, 0, 0, 0))
        in_specs.append(pl.BlockSpec(memory_space=pl.ANY))
        args.append(prev_states)
        aliases = {len(args) - 1: 1}
    return pl.pallas_call(
        functools.partial(_ret_sample_kernel, ts=ts),
        grid=(tl.bs // SAMPLE_GROUP,),
        in_specs=in_specs,
        out_specs=[pl.BlockSpec((rows, D_RET), lambda i: (i, 0)), state_spec],
        out_shape=[jax.ShapeDtypeStruct((tl.n_sample, D_RET), bf16),
                   jax.ShapeDtypeStruct((depth, tl.bs, N_HEADS, HEAD_D, HEAD_D), f32)],
        input_output_aliases=aliases,
        compiler_params=_cparams(1),
        name="retention_sample",
    )(*args)


def _ln_silu(cv, g, b):
    mu = jnp.mean(cv, axis=-1, keepdims=True)
    var = jnp.mean(jnp.square(cv - mu), axis=-1, keepdims=True)
    return jax.nn.silu((cv - mu) * lax.rsqrt(var + EPS) * g + b)


def _conv_taps(window, w_ref, b_ref, n_rows):
    cols = []
    for col in range(D_CONV // LANES):
        lanes = slice(col * LANES, (col + 1) * LANES)
        acc = jnp.broadcast_to(b_ref[:, lanes], (n_rows, LANES))
        for tap in range(CONV_WIDTH):
            acc = acc + window(col, tap) * w_ref[tap:tap + 1, lanes]
        cols.append(acc)
    return jnp.concatenate(cols, axis=-1)


def _conv_prompt_kernel(a_ref, w_ref, b_ref, g_ref, bl_ref, o_ref, buf_ref, full_ref, cv_ref):
    j = pl.program_id(1)
    n_cols = D_CONV // LANES

    @pl.when(j == 0)
    def _():
        full_ref[:, 0:HALO_PAD, :] = jnp.zeros((n_cols, HALO_PAD, LANES), f32)

    @pl.when(j > 0)
    def _():
        full_ref[:, 0:HALO_PAD, :] = full_ref[:, TM:TM + HALO_PAD, :]

    for col in range(n_cols):
        full_ref[col, HALO_PAD:HALO_PAD + TM, :] = a_ref[:, col * LANES:(col + 1) * LANES]
    shift = HALO_PAD - CONV_HALO

    n_chunks = TM // CONV_ROWS

    def taps(idx, carry):
        col = idx // n_chunks
        r0 = pl.multiple_of((idx % n_chunks) * CONV_ROWS, CONV_ROWS)
        acc = jnp.broadcast_to(b_ref[col], (CONV_ROWS, LANES))
        for tap in range(CONV_WIDTH):
            acc = acc + full_ref[col, pl.ds(r0 + (tap + shift), CONV_ROWS), :] * w_ref[col, tap:tap + 1, :]
        cv_ref[col, pl.ds(r0, CONV_ROWS), :] = acc
        return carry

    lax.fori_loop(0, n_cols * n_chunks, taps, 0)

    for r0 in range(0, TM, NORM_ROWS):
        cv = jnp.concatenate([cv_ref[col, r0:r0 + NORM_ROWS, :] for col in range(n_cols)], axis=-1)
        o_ref[r0:r0 + NORM_ROWS, :] = _ln_silu(cv, g_ref[...], bl_ref[...]).astype(bf16)

    @pl.when(j == pl.num_programs(1) - 1)
    def _():
        buf_ref[...] = a_ref[TM - CONV_HALO:TM, :]


def _conv_prompt(tl, layer, a, w_conv, b_conv, g_ln, b_ln):
    tps = tl.tiles_per_seq
    n_cols = D_CONV // LANES
    depth = w_conv.shape[0]
    w_cols = w_conv.reshape(depth, CONV_WIDTH, n_cols, LANES).transpose(0, 2, 1, 3)
    b_cols = b_conv.reshape(depth, n_cols, 1, LANES)
    vec = pl.BlockSpec((None, 1, D_CONV), lambda b, j: (layer, 0, 0))
    return pl.pallas_call(
        _conv_prompt_kernel,
        grid=(tl.bp, tps),
        in_specs=[pl.BlockSpec((TM, D_CONV), lambda b, j: (b * tps + j, 0)),
                  pl.BlockSpec((None, n_cols, CONV_WIDTH, LANES), lambda b, j: (layer, 0, 0, 0)),
                  pl.BlockSpec((None, n_cols, 1, LANES), lambda b, j: (layer, 0, 0, 0)),
                  vec, vec],
        out_specs=[pl.BlockSpec((TM, D_CONV), lambda b, j: (b * tps + j, 0)),
                   pl.BlockSpec((None, CONV_HALO, D_CONV), lambda b, j: (b, 0, 0))],
        out_shape=[jax.ShapeDtypeStruct((tl.n_prompt, D_CONV), bf16),
                   jax.ShapeDtypeStruct((tl.bp, CONV_HALO, D_CONV), f32)],
        scratch_shapes=[pltpu.VMEM((n_cols, HALO_PAD + TM, LANES), f32), pltpu.VMEM((n_cols, TM, LANES), f32)],
        compiler_params=_cparams(2),
        name="conv_prompt",
    )(a, w_cols, b_cols, g_ln, b_ln)


def _conv_sample_kernel(a_ref, st_ref, w_ref, b_ref, g_ref, bl_ref, o_ref, buf_ref, full_ref, cv_ref, *, ts):
    for s in range(SAMPLE_GROUP):
        for col in range(D_CONV // LANES):
            lanes = slice(col * LANES, (col + 1) * LANES)
            full_ref[col, 0:CONV_HALO, :] = st_ref[s, :, lanes]
            full_ref[col, CONV_HALO:CONV_HALO + ts, :] = a_ref[s * ts:(s + 1) * ts, lanes]
            buf_ref[s, :, lanes] = full_ref[col, ts:ts + CONV_HALO, :]
        cv_ref[s * ts:(s + 1) * ts, :] = _conv_taps(lambda col, tap: full_ref[col, tap:tap + ts, :],
                                                    w_ref, b_ref, ts)
    o_ref[...] = _ln_silu(cv_ref[...], g_ref[...], bl_ref[...]).astype(bf16)


def _conv_sample(tl, layer, a, state_conv, w_conv, b_conv, g_ln, b_ln):
    ts = tl.ts
    rows = SAMPLE_GROUP * ts
    first = tl.n_prompt // rows
    vec = pl.BlockSpec((None, 1, D_CONV), lambda i: (layer, 0, 0))
    return pl.pallas_call(
        functools.partial(_conv_sample_kernel, ts=ts),
        grid=(tl.bs // SAMPLE_GROUP,),
        in_specs=[pl.BlockSpec((rows, D_CONV), lambda i: (first + i, 0)),
                  pl.BlockSpec((None, SAMPLE_GROUP, CONV_HALO, D_CONV), lambda i: (layer, i, 0, 0)),
                  pl.BlockSpec((None, CONV_WIDTH, D_CONV), lambda i: (layer, 0, 0)),
                  vec, vec, vec],
        out_specs=[pl.BlockSpec((rows, D_CONV), lambda i: (i, 0)),
                   pl.BlockSpec((SAMPLE_GROUP, CONV_HALO, D_CONV), lambda i: (i, 0, 0))],
        out_shape=[jax.ShapeDtypeStruct((tl.n_sample, D_CONV), bf16),
                   jax.ShapeDtypeStruct((tl.bs, CONV_HALO, D_CONV), f32)],
        scratch_shapes=[pltpu.VMEM((D_CONV // LANES, CONV_HALO + ts + SUBLANES, LANES), f32),
                        pltpu.VMEM((rows, D_CONV), f32)],
        compiler_params=_cparams(1),
        name="conv_sample",
    )(a, state_conv, w_conv, b_conv, g_ln, b_ln)


SORT_ROWS = 2 * TM + N_EXPERTS * SUBLANES
XS_HALF = D_MODEL // 2
XS_W = XS_HALF + LANES
u32 = jnp.uint32


def _pack_bf16_pair(x):
    lo = lax.shift_right_logical(lax.bitcast_convert_type(x[:, 0:XS_HALF], u32), u32(16))
    hi = lax.bitcast_convert_type(x[:, XS_HALF:D_MODEL], u32) & u32(0xFFFF0000)
    return hi | lo


def _unpack_bf16_pair(words):
    lo = lax.bitcast_convert_type(lax.shift_left(words, u32(16)), f32).astype(bf16)
    hi = lax.bitcast_convert_type(words & u32(0xFFFF0000), f32).astype(bf16)
    return lo, hi


def _split3(x):
    a = x.astype(bf16)
    r = x - a.astype(f32)
    b = r.astype(bf16)
    c = (r - b.astype(f32)).astype(bf16)
    return a, b, c


def _first_of4(vals, m):
    return jnp.where(vals[0] == m, 0.0, jnp.where(vals[1] == m, 1.0, jnp.where(vals[2] == m, 2.0, 3.0)))


def _rows_to_tile(rows, n_rows):
    sub = lax.broadcasted_iota(i32, (n_rows, TM), 0)
    out = jnp.zeros((n_rows, TM), f32)
    for r, val in enumerate(rows):
        out = jnp.where(sub == r, val, out)
    return out


def _outproj_kernel(*refs, np_steps, split_x):
    if split_x:
        xp_ref, xs_ref = refs[:2]
        refs = refs[2:]
    else:
        x_ref = refs[0]
        refs = refs[1:]
    (retp_ref, rets_ref, cvp_ref, cvs_ref, gts_ref, gtt_ref, shs_ref, sht_ref, scs_ref, sct_ref,
     g_ref, wo_ref, wrh_ref, wrl_ref, br_ref,
     xo_ref, h2_ref, rows_ref, cols_ref, cnt_ref) = refs
    is_s = pl.program_id(0) >= np_steps

    for t in range(STEP_TILES):
        rs = slice(t * TM, (t + 1) * TM)
        x = jnp.where(is_s, xs_ref[rs, :], xp_ref[rs, :]) if split_x else x_ref[rs, :]
        ret = jnp.where(is_s, rets_ref[rs, :], retp_ref[rs, :])
        cv = jnp.where(is_s, cvs_ref[rs, :], cvp_ref[rs, :])
        mix_out = (jnp.dot(ret, wo_ref[0:D_RET, :], preferred_element_type=f32)
                   + jnp.dot(cv, wo_ref[D_RET:D_RET + D_CONV, :], preferred_element_type=f32))
        xn = x + _pick(is_s, gtt_ref, gts_ref, rs) * mix_out
        xo_ref[rs, :] = xn
        h2 = (_rms(xn, g_ref[...]) * (1.0 + _pick(is_s, sct_ref, scs_ref, rs))
              + _pick(is_s, sht_ref, shs_ref, rs))
        h_hi = h2.astype(bf16)
        h2_ref[rs, :] = h_hi
        rows, cols, cnt = _route_tile(h2, h_hi, wrh_ref, wrl_ref, br_ref)
        rows_ref[t] = rows
        cols_ref[rs, :] = cols
        cnt_ref[t] = cnt


def _route_tile(h2, h_hi, wrh_ref, wrl_ref, br_ref):
    h_lo = (h2 - h_hi.astype(f32)).astype(bf16)
    logits = (jnp.dot(h_hi, wrh_ref[...], preferred_element_type=f32)
              + jnp.dot(h_hi, wrl_ref[...], preferred_element_type=f32)
              + jnp.dot(h_lo, wrh_ref[...], preferred_element_type=f32)) + br_ref[...]
    lt = logits.T
    row = [lt[e:e + 1, :] for e in range(N_EXPERTS)]
    top = functools.reduce(jnp.maximum, row)
    ex = [jnp.exp(r - top) for r in row]
    den = functools.reduce(jnp.add, ex)
    p = [v / den for v in ex]

    best = None
    for g in range(N_GROUPS):
        a = p[g * GROUP_SIZE:(g + 1) * GROUP_SIZE]
        m1 = functools.reduce(jnp.maximum, a)
        i1 = _first_of4(a, m1)
        b = [jnp.where(i1 == float(j), -1.0, a[j]) for j in range(GROUP_SIZE)]
        m2 = functools.reduce(jnp.maximum, b)
        i2 = _first_of4(b, m2)
        cand = (m1 + m2, m1, m2, i1 + float(g * GROUP_SIZE), i2 + float(g * GROUP_SIZE))
        if best is None:
            best = cand
        else:
            take = cand[0] > best[0]
            best = tuple(jnp.where(take, c, o) for c, o in zip(cand, best))
    _, m1, m2, e0, e1 = best
    denom = m1 + m2
    w0 = m1 / denom
    w1 = m2 / denom

    ex_id = lax.broadcasted_iota(i32, (N_EXPERTS, TM), 0).astype(f32)
    sel0 = ex_id == e0
    sel1 = ex_id == e1
    ind = jnp.where(sel0 | sel1, 1.0, 0.0)
    t_r = lax.broadcasted_iota(i32, (TM, TM), 0)
    t_c = lax.broadcasted_iota(i32, (TM, TM), 1)
    earlier = jnp.where(t_r < t_c, 1.0, 0.0).astype(bf16)
    prefix = jnp.dot(ind.astype(bf16), earlier, preferred_element_type=f32)
    cnt = jnp.sum(ind, axis=-1, keepdims=True)
    cnt8 = jnp.floor((cnt + float(SUBLANES - 1)) * (1.0 / SUBLANES)) * float(SUBLANES)
    e_r = lax.broadcasted_iota(i32, (N_EXPERTS, N_EXPERTS), 0)
    e_c = lax.broadcasted_iota(i32, (N_EXPERTS, N_EXPERTS), 1)
    below = jnp.where(e_c < e_r, 1.0, 0.0).astype(bf16)
    seg_off = jnp.dot(below, jnp.broadcast_to(cnt8, (N_EXPERTS, TM)).astype(bf16),
                      preferred_element_type=f32)
    where_to = seg_off + prefix
    pos0 = jnp.sum(jnp.where(sel0, where_to, 0.0), axis=0, keepdims=True)
    pos1 = jnp.sum(jnp.where(sel1, where_to, 0.0), axis=0, keepdims=True)

    w0p = [v.astype(f32) for v in _split3(w0)]
    w1p = [v.astype(f32) for v in _split3(w1)]
    info = [pos0, pos1] + w0p + w1p
    return (_rows_to_tile(info, SUBLANES), _rows_to_tile(info, LANES).T,
            jnp.broadcast_to(cnt, (N_EXPERTS, LANES)))


def _outproj(tl, layer, x, ret_p, ret_s, cv_p, cv_s, mod_seq, mod_tok, g_norm, w_out_bf, wr_hi, wr_lo, br_pad):
    split_x = isinstance(x, tuple)
    tok_spec = pl.BlockSpec((TS, D_MODEL), lambda i: (i, 0))
    p_spec = lambda w: pl.BlockSpec((TS, w), lambda i: (tl.prompt_block(i), 0))
    s_spec = lambda w: pl.BlockSpec((TS, w), lambda i: (tl.sample_block(i), 0))
    if split_x:
        x_args, x_specs = list(x), [p_spec(D_MODEL), s_spec(D_MODEL)]
    else:
        x_args, x_specs = [x], [tok_spec]
    mods = []
    for col in (2, 3, 4):
        mods += list(_mod_specs(tl, layer, col))
    wr_spec = pl.BlockSpec((D_MODEL, LANES), lambda i: (0, 0))
    return pl.pallas_call(
        functools.partial(_outproj_kernel, np_steps=tl.np_steps, split_x=split_x),
        grid=(tl.n_steps,),
        in_specs=x_specs + [p_spec(D_RET), s_spec(D_RET), p_spec(D_CONV), s_spec(D_CONV)] + mods + [
            pl.BlockSpec((None, 1, D_MODEL), lambda i: (layer, 0, 0)),
            pl.BlockSpec((None, D_MODEL, D_MODEL), lambda i: (layer, 0, 0)),
            wr_spec, wr_spec,
            pl.BlockSpec((1, LANES), lambda i: (0, 0)),
        ],
        out_specs=[tok_spec, tok_spec,
                   pl.BlockSpec((STEP_TILES, SUBLANES, TM), lambda i: (i, 0, 0)),
                   pl.BlockSpec((TS, LANES), lambda i: (i, 0)),
                   pl.BlockSpec((STEP_TILES, N_EXPERTS, LANES), lambda i: (i, 0, 0))],
        out_shape=[jax.ShapeDtypeStruct((tl.n_tok, D_MODEL), f32),
                   jax.ShapeDtypeStruct((tl.n_tok, D_MODEL), bf16),
                   jax.ShapeDtypeStruct((tl.n_tiles, SUBLANES, TM), f32),
                   jax.ShapeDtypeStruct((tl.n_tok, LANES), f32),
                   jax.ShapeDtypeStruct((tl.n_tiles, N_EXPERTS, LANES), f32)],
        compiler_params=_cparams(1),
        name="outproj_router",
    )(*x_args, ret_p, ret_s, cv_p, cv_s, *([mod_seq, mod_tok] * 3), g_norm, w_out_bf, wr_hi, wr_lo, br_pad)


N_CHUNKS = SORT_ROWS // SUBLANES


class _Layout:
    def __init__(self, n_tiles):
        self.n_tiles = n_tiles
        self.tail_start = n_tiles * N_CHUNKS
        self.tail_n8 = self.tail_start + N_EXPERTS
        worst = 2 * n_tiles * TM + n_tiles * N_EXPERTS * (SUBLANES - 1) + N_EXPERTS * (BM - SUBLANES)
        self.n_blocks = -(-worst // BM)
        self.cap = self.n_blocks * BM
        self.dump = self.cap
        self.xs_rows = self.cap + -(-2 * STEP_TILES * SORT_ROWS // BM) * BM


def _moe_tables(lay, tile_counts):
    c8 = ((tile_counts.astype(i32) + SUBLANES - 1) // SUBLANES) * SUBLANES
    base8 = jnp.cumsum(c8, axis=0) - c8
    tot8 = jnp.sum(c8, axis=0)
    region = ((tot8 + BM - 1) // BM) * BM
    g_end = jnp.cumsum(region)
    g_start = g_end - region
    seg_end = jnp.cumsum(c8, axis=1)
    seg_dst = g_start[None, :] + base8
    n_used = g_end[-1] // BM
    blk = jnp.arange(lay.n_blocks, dtype=i32)
    block_e = jnp.minimum(jnp.sum((g_end[None, :] <= blk[:, None] * BM).astype(i32), axis=1), N_EXPERTS - 1)
    block_e = jnp.where(blk < n_used, block_e, block_e[n_used - 1])
    row0 = jnp.arange(N_CHUNKS, dtype=i32) * SUBLANES
    owner = jnp.sum((seg_end[:, None, :] <= row0[None, :, None]).astype(i32), axis=-1)
    onehot = (owner[:, :, None] == jnp.arange(N_EXPERTS, dtype=i32)[None, None, :]).astype(i32)
    delta = seg_dst - (seg_end - c8)
    chunk_dst = jnp.where(owner < N_EXPERTS, row0[None, :] + jnp.sum(onehot * delta[:, None, :], axis=-1), -1)
    tab = jnp.concatenate([chunk_dst.ravel(), g_start + tot8, (region - tot8) // SUBLANES]).astype(i32)
    ids = jnp.arange(N_EXPERTS, dtype=i32)
    later = jnp.where((ids[None, :] > ids[:, None]) & (region[None, :] > 0), ids[None, :], N_EXPERTS)
    next_e = jnp.min(later, axis=1)
    next_e = jnp.where(next_e == N_EXPERTS, -1, next_e)
    return tab, jnp.concatenate([block_e, n_used[None], next_e]).astype(i32)


def _for_chunks(n, fn):
    def body(c, carry):
        fn(c)
        return carry

    lax.fori_loop(0, n, body, 0)


def _dispatch_kernel(tab_ref, h2_ref, rows_ref, cols_ref, xs_hbm, sorted_ref, zero_ref, sem, *, lay, n_steps):
    i = pl.program_id(0)
    slot = i % 2
    step_rows = STEP_TILES * SORT_ROWS

    def tail_copy(dst):
        return pltpu.make_async_copy(zero_ref, xs_hbm.at[pl.ds(dst, SUBLANES)], sem.at[2])

    def wait_step(slot_):
        pltpu.make_async_copy(sorted_ref.at[slot_], xs_hbm.at[pl.ds(0, step_rows)], sem.at[slot_]).wait()

    @pl.when(i == 0)
    def _():
        zero_ref[...] = jnp.zeros_like(zero_ref)
        for e in range(N_EXPERTS):
            start = tab_ref[lay.tail_start + e]
            _for_chunks(tab_ref[lay.tail_n8 + e],
                        lambda c: tail_copy(pl.multiple_of(start + c * SUBLANES, SUBLANES)).start())

    @pl.when(i >= 2)
    def _():
        wait_step(slot)

    r_id = lax.broadcasted_iota(i32, (SORT_ROWS, TM), 0).astype(f32)
    lane = lax.broadcasted_iota(i32, (TM, LANES), 1)
    for t in range(STEP_TILES):
        pos0 = rows_ref[t, 0:1, :]
        pos1 = rows_ref[t, 1:2, :]
        p0 = r_id == pos0
        p1 = r_id == pos1
        perm = jnp.where(p0 | p1, 1.0, 0.0).astype(bf16)
        cols = cols_ref[t * TM:(t + 1) * TM, :]
        wpart0 = jnp.where((lane >= 2) & (lane < 5), cols, 0.0).astype(bf16)
        wpart1 = jnp.where((lane >= 5) & (lane < 8), cols, 0.0).astype(bf16)
        sw = (jnp.dot(jnp.where(p0, 1.0, 0.0).astype(bf16), wpart0, preferred_element_type=f32)
              + jnp.dot(jnp.where(p1, 1.0, 0.0).astype(bf16), wpart1, preferred_element_type=f32))
        base = t * SORT_ROWS
        sorted_ref[slot, base:base + SORT_ROWS, 0:XS_HALF] = _pack_bf16_pair(jnp.dot(
            perm, h2_ref[t * TM:(t + 1) * TM, :], preferred_element_type=f32))
        sorted_ref[slot, base:base + SORT_ROWS, XS_HALF:XS_W] = lax.bitcast_convert_type(jnp.broadcast_to(
            jnp.sum(sw, axis=-1, keepdims=True), (SORT_ROWS, LANES)), u32)

    for t in range(STEP_TILES):
        for c in range(N_CHUNKS):
            row = t * SORT_ROWS + c * SUBLANES
            dst = tab_ref[(i * STEP_TILES + t) * N_CHUNKS + c]
            dst = jnp.where(dst < 0, lay.dump + slot * step_rows + row, dst)
            pltpu.make_async_copy(sorted_ref.at[slot, pl.ds(row, SUBLANES)],
                                  xs_hbm.at[pl.ds(pl.multiple_of(dst, SUBLANES), SUBLANES)], sem.at[slot]).start()

    @pl.when(i == n_steps - 1)
    def _():
        if n_steps >= 2:
            wait_step(1 - slot)
        wait_step(slot)
        for e in range(N_EXPERTS):
            _for_chunks(tab_ref[lay.tail_n8 + e], lambda c: tail_copy(0).wait())


def _dispatch(tl, lay, tab, h2, rows, cols):
    grid_spec = pltpu.PrefetchScalarGridSpec(
        num_scalar_prefetch=1,
        grid=(tl.n_steps,),
        in_specs=[pl.BlockSpec((TS, D_MODEL), lambda i, t: (i, 0)),
                  pl.BlockSpec((STEP_TILES, SUBLANES, TM), lambda i, t: (i, 0, 0)),
                  pl.BlockSpec((TS, LANES), lambda i, t: (i, 0))],
        out_specs=pl.BlockSpec(memory_space=pl.ANY),
        scratch_shapes=[pltpu.VMEM((2, STEP_TILES * SORT_ROWS, XS_W), u32), pltpu.VMEM((SUBLANES, XS_W), u32),
                        pltpu.SemaphoreType.DMA((3,))],
    )
    return pl.pallas_call(
        functools.partial(_dispatch_kernel, lay=lay, n_steps=tl.n_steps),
        grid_spec=grid_spec,
        out_shape=jax.ShapeDtypeStruct((lay.xs_rows, XS_W), u32),
        compiler_params=_cparams(1),
        name="moe_dispatch",
    )(tab, h2, rows, cols)


def _expert_kernel(be_ref, xs_ref, wg_hbm, wu_hbm, wd_hbm, ys_ref, stage, w_bf, sem, *, n_blocks, layer):
    j = pl.program_id(0)

    def fetch(e):
        return [pltpu.make_async_copy(w.at[layer, e], stage.at[k], sem.at[k])
                for k, w in enumerate((wg_hbm, wu_hbm, wd_hbm))]

    @pl.when(j < be_ref[n_blocks])
    def _():
        e = be_ref[j]

        @pl.when(j == 0)
        def _():
            for copy in fetch(e):
                copy.start()

        @pl.when((j == 0) | (e != be_ref[jnp.maximum(j - 1, 0)]))
        def _():
            for copy in fetch(e):
                copy.wait()
            for k in range(3):
                w_bf[k] = stage[k].astype(bf16)
            nxt = be_ref[n_blocks + 1 + e]

            @pl.when(nxt >= 0)
            def _():
                for copy in fetch(nxt):
                    copy.start()

        x_lo, x_hi = _unpack_bf16_pair(xs_ref[:, 0:XS_HALF])

        def first_layer(k):
            return (jnp.dot(x_lo, w_bf[k, 0:XS_HALF, :], preferred_element_type=f32)
                    + jnp.dot(x_hi, w_bf[k, XS_HALF:D_MODEL, :], preferred_element_type=f32))

        mid = (jax.nn.silu(first_layer(0)) * first_layer(1)).astype(bf16)
        slot_w = lax.bitcast_convert_type(xs_ref[:, XS_HALF:XS_HALF + 1], f32)
        ys_ref[...] = jnp.dot(mid, w_bf[2], preferred_element_type=f32) * slot_w


def _experts(layer, lay, block_e, xs, w_gate, w_up, w_down):
    n_blocks = lay.n_blocks
    d_ff = w_gate.shape[-1]
    assert d_ff == D_MODEL
    used = lambda j, be: jnp.minimum(j, be[n_blocks] - 1)
    any_spec = pl.BlockSpec(memory_space=pl.ANY)
    grid_spec = pltpu.PrefetchScalarGridSpec(
        num_scalar_prefetch=1,
        grid=(n_blocks,),
        in_specs=[pl.BlockSpec((BM, XS_W), lambda j, be: (used(j, be), 0)), any_spec, any_spec, any_spec],
        out_specs=pl.BlockSpec((BM, D_MODEL), lambda j, be: (used(j, be), 0)),
        scratch_shapes=[pltpu.VMEM((3, D_MODEL, d_ff), f32), pltpu.VMEM((3, D_MODEL, d_ff), bf16),
                        pltpu.SemaphoreType.DMA((3,))],
    )
    return pl.pallas_call(
        functools.partial(_expert_kernel, n_blocks=n_blocks, layer=layer),
        grid_spec=grid_spec,
        out_shape=jax.ShapeDtypeStruct((lay.cap, D_MODEL), f32),
        compiler_params=_cparams(1),
        name="moe_experts",
    )(block_e, xs, w_gate, w_up, w_down)


def _combine_kernel(tab_ref, ys_hbm, cols_ref, x_ref, gts_ref, gtt_ref, *rest, n_steps, np_steps, final):
    if final:
        gf_ref, yp_ref, ysm_ref, staged, sem = rest
    else:
        xo_ref, staged, sem = rest
    i = pl.program_id(0)
    slot = i % 2
    is_s = i >= np_steps

    def start_step(step, slot_):
        for c in range(STEP_TILES * N_CHUNKS):
            src = jnp.maximum(tab_ref[step * (STEP_TILES * N_CHUNKS) + c], 0)
            pltpu.make_async_copy(ys_hbm.at[pl.ds(pl.multiple_of(src, SUBLANES), SUBLANES)],
                                  staged.at[slot_, pl.ds(c * SUBLANES, SUBLANES)], sem.at[slot_]).start()

    @pl.when(i == 0)
    def _():
        start_step(0, 0)

    @pl.when(i + 1 < n_steps)
    def _():
        start_step(i + 1, 1 - slot)

    pltpu.make_async_copy(ys_hbm.at[pl.ds(0, STEP_TILES * SORT_ROWS)], staged.at[slot], sem.at[slot]).wait()

    lane = lax.broadcasted_iota(i32, (TM, SORT_ROWS), 1).astype(f32)
    outs = []
    for t in range(STEP_TILES):
        rs = slice(t * TM, (t + 1) * TM)
        unperm = jnp.where((lane == cols_ref[rs, 0:1]) | (lane == cols_ref[rs, 1:2]), 1.0, 0.0).astype(bf16)
        parts = _split3(staged[slot, t * SORT_ROWS:(t + 1) * SORT_ROWS, :])
        ff = sum(jnp.dot(unperm, part, preferred_element_type=f32) for part in parts)
        xn = x_ref[rs, :] + _pick(is_s, gtt_ref, gts_ref, rs) * ff
        outs.append(_rms(xn, gf_ref[...]) if final else xn)
    if final:
        @pl.when(jnp.logical_not(is_s))
        def _():
            for t, y in enumerate(outs):
                yp_ref[t * TM:(t + 1) * TM, :] = y

        @pl.when(is_s)
        def _():
            for t, y in enumerate(outs):
                ysm_ref[t * TM:(t + 1) * TM, :] = y
    else:
        for t, xn in enumerate(outs):
            xo_ref[t * TM:(t + 1) * TM, :] = xn


def _combine(tl, lay, layer, tab, ys, cols, x, mod_seq, mod_tok, g_final):
    final = g_final is not None
    tok_spec = pl.BlockSpec((TS, D_MODEL), lambda i, t: (i, 0))
    gt_seq, gt_tok = _mod_specs(tl, layer, 5)
    in_specs = [pl.BlockSpec(memory_space=pl.ANY), pl.BlockSpec((TS, LANES), lambda i, t: (i, 0)),
                tok_spec, gt_seq, gt_tok]
    args = [tab, ys, cols, x, mod_seq, mod_tok]
    if final:
        in_specs.append(pl.BlockSpec((1, D_MODEL), lambda i, t: (0, 0)))
        args.append(g_final)
        out_specs = [pl.BlockSpec((TS, D_MODEL), lambda i, t: (tl.prompt_block(i), 0)),
                     pl.BlockSpec((TS, D_MODEL), lambda i, t: (tl.sample_block(i), 0))]
        out_shape = [jax.ShapeDtypeStruct((tl.n_prompt, D_MODEL), f32),
                     jax.ShapeDtypeStruct((tl.n_sample, D_MODEL), f32)]
    else:
        out_specs = tok_spec
        out_shape = jax.ShapeDtypeStruct((tl.n_tok, D_MODEL), f32)
    grid_spec = pltpu.PrefetchScalarGridSpec(
        num_scalar_prefetch=1,
        grid=(tl.n_steps,),
        in_specs=in_specs,
        out_specs=out_specs,
        scratch_shapes=[pltpu.VMEM((2, STEP_TILES * SORT_ROWS, D_MODEL), f32), pltpu.SemaphoreType.DMA((2,))],
    )
    return pl.pallas_call(
        functools.partial(_combine_kernel, n_steps=tl.n_steps, np_steps=tl.np_steps, final=final),
        grid_spec=grid_spec,
        out_shape=out_shape,
        compiler_params=_cparams(1),
        name="moe_combine",
    )(*args)


def _rope_tables(tl):
    half = HEAD_D // 2
    inv = ROPE_BASE ** (-jnp.arange(half, dtype=f32) / half)
    pos_p = jnp.arange(tl.tp, dtype=i32)
    pos_s = PAST_LEN + jnp.arange(tl.ts, dtype=i32)
    pos = jnp.concatenate([pos_p, jnp.tile(pos_s, tl.bs)])
    ang = pos.astype(f32)[:, None] * inv[None, :]
    cos, sin = jnp.cos(ang), jnp.sin(ang)
    return jnp.concatenate([cos, cos], axis=-1), jnp.concatenate([-sin, sin], axis=-1)


def kernel(x_prompt, x_sample, state_ret, state_conv, c_prompt, c_sample, w_mod, b_mod, g_mix_norm, w_in,
           w_conv, b_conv, g_conv_ln, b_conv_ln, g_ret_gn, w_out, g_ffn_norm, w_router, b_router,
           w_exp_gate, w_exp_up, w_exp_down, g_final):
    bp, tp, _ = x_prompt.shape
    bs, ts, _ = x_sample.shape
    depth = w_mod.shape[0]
    tl = _Tiles(bp, tp, bs, ts)
    lay = _Layout(tl.n_tiles)

    c_all = jnp.concatenate([c_prompt, jnp.repeat(c_sample, ts, axis=0)], axis=0)
    mod_seq, mod_tok = _modulation(c_all, bp, w_mod, b_mod)
    mod_seq = mod_seq.reshape(depth, bp, 1, N_MOD * D_MODEL)

    cos_tab, sin_tab = _rope_tables(tl)
    w_in_bf = w_in.astype(bf16)
    w_out_bf = w_out.astype(bf16)
    wr_pad = jnp.pad(w_router.astype(f32), ((0, 0), (0, LANES - N_EXPERTS)))
    wr_hi = wr_pad.astype(bf16)
    wr_lo = (wr_pad - wr_hi.astype(f32)).astype(bf16)
    br_pad = jnp.pad(b_router.astype(f32), (0, LANES - N_EXPERTS)).reshape(1, LANES)
    vec3 = lambda t: t.reshape(depth, 1, t.shape[-1])
    g_mix3, g_ffn3, gn3 = vec3(g_mix_norm), vec3(g_ffn_norm), vec3(g_ret_gn)
    b_conv3, g_ln3, b_ln3 = vec3(b_conv), vec3(g_conv_ln), vec3(b_conv_ln)

    x = (x_prompt.reshape(tl.n_prompt, D_MODEL), x_sample.reshape(tl.n_sample, D_MODEL))
    ret_p, conv_p, conv_s = [], [], []
    ret_s_all = None
    for layer in range(depth):
        q, k, v, gate, a = _inproj(tl, layer, x, mod_seq, mod_tok, g_mix3, w_in_bf, cos_tab, sin_tab)
        ro_p, s_p = _retention_prompt(tl, layer, q, k, v, gate, gn3)
        ro_s, ret_s_all = _retention_sample(tl, layer, q, k, v, gate, state_ret, gn3, ret_s_all)
        co_p, buf_p = _conv_prompt(tl, layer, a, w_conv, b_conv3, g_ln3, b_ln3)
        co_s, buf_s = _conv_sample(tl, layer, a, state_conv, w_conv, b_conv3, g_ln3, b_ln3)
        x_mid, h2, rows, cols, tile_counts = _outproj(
            tl, layer, x, ro_p, ro_s, co_p, co_s, mod_seq, mod_tok, g_ffn3, w_out_bf, wr_hi, wr_lo, br_pad)
        tab, block_e = _moe_tables(lay, tile_counts[:, :, 0])
        xs = _dispatch(tl, lay, tab, h2, rows, cols)
        ys = _experts(layer, lay, block_e, xs, w_exp_gate, w_exp_up, w_exp_down)
        last = layer == depth - 1
        x = _combine(tl, lay, layer, tab, ys, cols, x_mid, mod_seq, mod_tok,
                     g_final.reshape(1, D_MODEL) if last else None)
        ret_p.append(s_p)
        conv_p.append(buf_p)
        conv_s.append(buf_s)
    y_p, y_s = x
    return (y_p.reshape(bp, tp, D_MODEL), y_s.reshape(bs, ts, D_MODEL),
            jnp.stack(ret_p), jnp.stack(conv_p), ret_s_all, jnp.stack(conv_s))
```

```python
import functools

import jax
import jax.numpy as jnp
from jax import lax
from jax.experimental import pallas as pl
from jax.experimental.pallas import tpu as pltpu

f32 = jnp.float32
bf16 = jnp.bfloat16
i32 = jnp.int32

D_MODEL = 1024
D_RET = 512
D_CONV = 512
N_HEADS = 4
HEAD_D = 128
RET_CHUNK = 128
RET_CHUNKS_PER_STEP = 4
ROPE_BASE = 10000.0
CONV_WIDTH = 31
CONV_HALO = CONV_WIDTH - 1
N_EXPERTS = 16
N_GROUPS = 4
GROUP_SIZE = N_EXPERTS // N_GROUPS
N_MOD = 6
EPS = 1e-6
PAST_LEN = 16384
D_IN = 4 * D_RET + 2 * D_CONV

LANES = 128
SUBLANES = 8
TM = 256
STEP_TILES = 2
TS = TM * STEP_TILES
BM = 512
CONV_ROWS = 128
NORM_ROWS = 64
SAMPLE_GROUP = 8
HALO_PAD = 32
VMEM_LIMIT = 56 * 1024 * 1024


def _cparams(n_axes, vmem=VMEM_LIMIT):
    return pltpu.CompilerParams(dimension_semantics=("arbitrary",) * n_axes, vmem_limit_bytes=vmem)


def _mod_kernel(c_ref, w_ref, b_ref, seq_ref, tok_ref):
    cond = jax.nn.silu(c_ref[...]).astype(bf16)
    mod = jnp.dot(cond, w_ref[...].astype(bf16), preferred_element_type=f32) + b_ref[...]
    n_seq = seq_ref.shape[0]
    seq_ref[...] = mod[0:n_seq, :]
    tok_ref[...] = mod[n_seq:, :]


def _modulation(c_all, n_seq, w_mod, b_mod):
    depth = w_mod.shape[0]
    m = c_all.shape[0]
    assert n_seq % SUBLANES == 0
    return pl.pallas_call(
        _mod_kernel,
        grid=(depth, N_MOD),
        in_specs=[
            pl.BlockSpec((m, D_MODEL), lambda l, j: (0, 0)),
            pl.BlockSpec((None, D_MODEL, D_MODEL), lambda l, j: (l, 0, j)),
            pl.BlockSpec((None, 1, D_MODEL), lambda l, j: (l, 0, j)),
        ],
        out_specs=[pl.BlockSpec((None, n_seq, D_MODEL), lambda l, j: (l, 0, j)),
                   pl.BlockSpec((None, m - n_seq, D_MODEL), lambda l, j: (l, 0, j))],
        out_shape=[jax.ShapeDtypeStruct((depth, n_seq, N_MOD * D_MODEL), f32),
                   jax.ShapeDtypeStruct((depth, m - n_seq, N_MOD * D_MODEL), f32)],
        compiler_params=_cparams(2),
        name="modulation",
    )(c_all, w_mod, b_mod.reshape(depth, 1, N_MOD * D_MODEL))


class _Tiles:
    def __init__(self, bp, tp, bs, ts):
        self.bp, self.tp, self.bs, self.ts = bp, tp, bs, ts
        self.n_prompt = bp * tp
        self.n_sample = bs * ts
        self.n_tok = self.n_prompt + self.n_sample
        assert tp % TS == 0 and self.n_sample % TS == 0
        self.tiles_per_seq = tp // TM
        self.np_tiles = self.n_prompt // TM
        self.n_tiles = self.n_tok // TM
        self.steps_per_seq = tp // TS
        self.np_steps = self.n_prompt // TS
        self.n_steps = self.n_tok // TS

    def prompt_block(self, i):
        return jnp.minimum(i, self.np_steps - 1)

    def sample_block(self, i):
        return jnp.maximum(i - self.np_steps, 0)

    def seq_index(self, i):
        return jnp.minimum(i // self.steps_per_seq, self.bp - 1)


def _mod_specs(tl, layer, col):
    seq = pl.BlockSpec((None, None, 1, D_MODEL), lambda i, *_: (layer, tl.seq_index(i), 0, col))
    tok = pl.BlockSpec((None, TS, D_MODEL), lambda i, *_: (layer, tl.sample_block(i), col))
    return seq, tok


def _rms(x, g):
    return x * lax.rsqrt(jnp.mean(x * x, axis=-1, keepdims=True) + EPS) * g


def _inproj_kernel(*refs, np_steps, split_x):
    if split_x:
        xp_ref, xs_ref = refs[:2]
        refs = refs[2:]
    else:
        x_ref = refs[0]
        refs = refs[1:]
    (shs_ref, sht_ref, scs_ref, sct_ref, g_ref, w_ref, cos_ref, sin_ref,
     q_ref, k_ref, v_ref, gate_ref, a_ref) = refs
    is_s = pl.program_id(0) >= np_steps

    def step(sample):
        if split_x:
            x = xs_ref[...] if sample else xp_ref[...]
        else:
            x = x_ref[...]
        sh, sc = (sht_ref[...], sct_ref[...]) if sample else (shs_ref[...], scs_ref[...])
        h = _rms(x, g_ref[...]) * (1.0 + sc) + sh
        hb = h.astype(bf16)
        cos = cos_ref[...]
        sin = sin_ref[...]

        def proj(col):
            return jnp.dot(hb, w_ref[:, col * D_RET:(col + 1) * D_RET], preferred_element_type=f32)

        def rope(t):
            outs = []
            for hd in range(N_HEADS):
                th = t[:, hd * HEAD_D:(hd + 1) * HEAD_D]
                outs.append(th * cos + pltpu.roll(th, HEAD_D // 2, 1) * sin)
            return outs

        for hd, qh in enumerate(rope(proj(0))):
            q_ref[:, hd * HEAD_D:(hd + 1) * HEAD_D] = qh.astype(bf16)
        for hd, kh in enumerate(rope(proj(1))):
            k_ref[:, hd * HEAD_D:(hd + 1) * HEAD_D] = kh * (HEAD_D ** -0.5)
        v_ref[...] = proj(2).astype(bf16)
        gate_ref[...] = proj(3)
        a_ref[...] = proj(4) * jax.nn.sigmoid(proj(5))

    pl.when(is_s)(functools.partial(step, True))
    pl.when(jnp.logical_not(is_s))(functools.partial(step, False))


def _inproj(tl, layer, x, mod_seq, mod_tok, g_norm, w_in_bf, cos_tab, sin_tab):
    split_x = isinstance(x, tuple)
    tok_spec = pl.BlockSpec((TS, D_MODEL), lambda i: (i, 0))
    if split_x:
        x_args = list(x)
        x_specs = [pl.BlockSpec((TS, D_MODEL), lambda i: (tl.prompt_block(i), 0)),
                   pl.BlockSpec((TS, D_MODEL), lambda i: (tl.sample_block(i), 0))]
    else:
        x_args, x_specs = [x], [tok_spec]
    sh_seq, sh_tok = _mod_specs(tl, layer, 0)
    sc_seq, sc_tok = _mod_specs(tl, layer, 1)

    def table_block(i):
        return jnp.where(i < tl.np_steps, i % tl.steps_per_seq, tl.steps_per_seq + tl.sample_block(i))

    tab_spec = pl.BlockSpec((TS, HEAD_D), lambda i: (table_block(i), 0))
    out_spec = pl.BlockSpec((TS, D_RET), lambda i: (i, 0))
    out_sd = lambda dt: jax.ShapeDtypeStruct((tl.n_tok, D_RET), dt)
    return pl.pallas_call(
        functools.partial(_inproj_kernel, np_steps=tl.np_steps, split_x=split_x),
        grid=(tl.n_steps,),
        in_specs=x_specs + [
            sh_seq, sh_tok, sc_seq, sc_tok,
            pl.BlockSpec((None, 1, D_MODEL), lambda i: (layer, 0, 0)),
            pl.BlockSpec((None, D_MODEL, D_IN), lambda i: (layer, 0, 0)),
            tab_spec, tab_spec,
        ],
        out_specs=[out_spec] * 5,
        out_shape=[out_sd(bf16), out_sd(f32), out_sd(bf16), out_sd(f32), out_sd(f32)],
        compiler_params=_cparams(1),
        name="inproj",
    )(*x_args, mod_seq, mod_tok, mod_seq, mod_tok, g_norm, w_in_bf, cos_tab, sin_tab)


def _head_norm_gate(o, gn, gate):
    mu = jnp.mean(o, axis=-1, keepdims=True)
    var = jnp.mean(jnp.square(o - mu), axis=-1, keepdims=True)
    return jax.nn.silu(gate) * ((o - mu) * lax.rsqrt(var + EPS) * gn)


def _dot_nt(a, b):
    return lax.dot_general(a, b, (((1,), (1,)), ((), ())), preferred_element_type=f32)


def _dot_tn(a, b):
    return lax.dot_general(a, b, (((0,), (0,)), ((), ())), preferred_element_type=f32)


def _ret_prompt_kernel(q_ref, k_ref, v_ref, gate_ref, dec_ref, qd_ref, kd_ref, cd_ref, gn_ref,
                       o_ref, s_out_ref, s_ref):
    c = pl.program_id(1)

    @pl.when(c == 0)
    def _():
        s_ref[...] = jnp.zeros_like(s_ref)

    for ci in range(RET_CHUNKS_PER_STEP):
        rows = slice(ci * RET_CHUNK, (ci + 1) * RET_CHUNK)
        for hd in range(N_HEADS):
            sl = slice(hd * HEAD_D, (hd + 1) * HEAD_D)
            kh = k_ref[rows, sl]
            qb = q_ref[rows, sl]
            kb = kh.astype(bf16)
            vb = v_ref[rows, sl]
            s_old = s_ref[hd]
            scores = _dot_nt(qb, kb) * dec_ref[hd]
            inner = jnp.dot(scores.astype(bf16), vb, preferred_element_type=f32)
            cross = jnp.dot(qb, s_old.astype(bf16), preferred_element_type=f32) * qd_ref[hd]
            s_ref[hd] = s_old * cd_ref[hd] + _dot_tn((kh * kd_ref[hd]).astype(bf16), vb)
            o_ref[rows, sl] = _head_norm_gate(inner + cross, gn_ref[:, sl], gate_ref[rows, sl]).astype(bf16)

    @pl.when(c == pl.num_programs(1) - 1)
    def _():
        s_out_ref[...] = s_ref[...]


def _decay_tables(chunk, true_len):
    lg = jnp.log(1.0 - 2.0 ** (-5.0 - jnp.arange(N_HEADS, dtype=f32)))
    idx = jnp.arange(chunk, dtype=f32)
    rel = idx[:, None] - idx[None, :]
    decay = jnp.where(rel[None] >= 0, jnp.exp(jnp.maximum(rel, 0.0)[None] * lg[:, None, None]), 0.0)
    q_decay = jnp.exp((idx[None, :] + 1.0) * lg[:, None])
    k_decay = jnp.exp((true_len - 1.0 - idx[None, :]) * lg[:, None])
    c_decay = jnp.exp(true_len * lg)
    return decay, q_decay, k_decay, c_decay


def _retention_prompt(tl, layer, q, k, v, gate, g_ret_gn):
    step_rows = RET_CHUNK * RET_CHUNKS_PER_STEP
    assert tl.tp % step_rows == 0
    n_chunks = tl.tp // step_rows
    decay, q_decay, k_decay, c_decay = _decay_tables(RET_CHUNK, RET_CHUNK)
    bcast = lambda t: jnp.broadcast_to(t[:, :, None], (N_HEADS, RET_CHUNK, HEAD_D))
    cd = jnp.broadcast_to(c_decay[:, None, None], (N_HEADS, 1, HEAD_D))
    tok_spec = pl.BlockSpec((step_rows, D_RET), lambda b, c: (b * n_chunks + c, 0))
    tab_spec = pl.BlockSpec((N_HEADS, RET_CHUNK, HEAD_D), lambda b, c: (0, 0, 0))
    return pl.pallas_call(
        _ret_prompt_kernel,
        grid=(tl.bp, n_chunks),
        in_specs=[tok_spec] * 4 + [tab_spec] * 3 + [
            pl.BlockSpec((N_HEADS, 1, HEAD_D), lambda b, c: (0, 0, 0)),
            pl.BlockSpec((None, 1, D_RET), lambda b, c: (layer, 0, 0)),
        ],
        out_specs=[tok_spec, pl.BlockSpec((None, N_HEADS, HEAD_D, HEAD_D), lambda b, c: (b, 0, 0, 0))],
        out_shape=[jax.ShapeDtypeStruct((tl.n_prompt, D_RET), bf16),
                   jax.ShapeDtypeStruct((tl.bp, N_HEADS, HEAD_D, HEAD_D), f32)],
        scratch_shapes=[pltpu.VMEM((N_HEADS, HEAD_D, HEAD_D), f32)],
        compiler_params=_cparams(2),
        name="retention_prompt",
    )(q, k, v, gate, decay, bcast(q_decay), bcast(k_decay), cd, g_ret_gn)


def _ret_sample_kernel(q_ref, k_ref, v_ref, gate_ref, s_in_ref, dec_ref, qd_ref, kd_ref, cd_ref, gn_ref,
                       *rest, ts):
    o_ref, s_all_ref = rest[-2:]
    s_out_ref = s_all_ref.at[0]
    for other in range(1, s_all_ref.shape[0]):
        s_all_ref[other] = jnp.zeros(s_all_ref.shape[1:], f32)
    seqs_per_tile = SUBLANES // ts
    row = lax.broadcasted_iota(i32, (SUBLANES, HEAD_D), 0)
    q_all = q_ref[...].astype(f32)
    v_all = v_ref[...].astype(f32)
    outs = []
    for t in range(SAMPLE_GROUP // seqs_per_tile):
        rows = slice(t * SUBLANES, (t + 1) * SUBLANES)
        heads = []
        for hd in range(N_HEADS):
            sl = slice(hd * HEAD_D, (hd + 1) * HEAD_D)
            qh = q_all[rows, sl]
            kh = k_ref[rows, sl] * kd_ref[hd]
            vb = v_all[rows, sl].astype(bf16)
            qb = qh.astype(bf16)
            scores = _dot_nt(qb, k_ref[rows, sl].astype(bf16)) * dec_ref[hd]
            o = jnp.dot(scores.astype(bf16), vb, preferred_element_type=f32)
            for s in range(seqs_per_tile):
                b = t * seqs_per_tile + s
                mine = (row >= s * ts) & (row < (s + 1) * ts)
                s_old = s_in_ref[b, hd]
                q_s = jnp.where(mine, qh, 0.0).astype(bf16)
                k_s = jnp.where(mine, kh, 0.0).astype(bf16)
                o = o + jnp.dot(q_s, s_old.astype(bf16), preferred_element_type=f32) * qd_ref[hd]
                s_out_ref[b, hd] = s_old * cd_ref[hd] + _dot_tn(k_s, vb)
            heads.append(_head_norm_gate(o, gn_ref[:, sl], gate_ref[rows, sl]))
        outs.append(jnp.concatenate(heads, axis=-1))
    o_ref[...] = jnp.concatenate(outs, axis=0).astype(bf16)


def _retention_sample(tl, layer, q, k, v, gate, state_ret, g_ret_gn, prev_states):
    ts = tl.ts
    depth = state_ret.shape[0]
    assert SUBLANES % ts == 0 and tl.bs % SAMPLE_GROUP == 0
    seqs_per_tile = SUBLANES // ts
    decay, q_decay, k_decay, c_decay = _decay_tables(ts, ts)
    eye = jnp.eye(seqs_per_tile, dtype=f32)
    dec_tile = jnp.einsum("ab,hij->haibj", eye, decay).reshape(N_HEADS, SUBLANES, SUBLANES)
    tile_rows = lambda t: jnp.broadcast_to(jnp.tile(t, (1, seqs_per_tile))[:, :, None],
                                           (N_HEADS, SUBLANES, HEAD_D))
    cd = jnp.broadcast_to(c_decay[:, None, None], (N_HEADS, 1, HEAD_D))
    rows = SAMPLE_GROUP * ts
    first = tl.n_prompt // rows
    tok_spec = pl.BlockSpec((rows, D_RET), lambda i: (first + i, 0))
    const3 = lambda shape: pl.BlockSpec(shape, lambda i: (0, 0, 0))
    st_block = (SAMPLE_GROUP, N_HEADS, HEAD_D, HEAD_D)
    in_specs = [tok_spec] * 4 + [
        pl.BlockSpec((None,) + st_block, lambda i: (layer, i, 0, 0, 0)),
        const3((N_HEADS, SUBLANES, SUBLANES)),
        const3((N_HEADS, SUBLANES, HEAD_D)),
        const3((N_HEADS, SUBLANES, HEAD_D)),
        const3((N_HEADS, 1, HEAD_D)),
        pl.BlockSpec((None, 1, D_RET), lambda i: (layer, 0, 0)),
    ]
    args = [q, k, v, gate, state_ret, dec_tile, tile_rows(q_decay), tile_rows(k_decay), cd, g_ret_gn]
    if prev_states is None:
        state_spec = pl.BlockSpec((depth,) + st_block, lambda i: (0, i, 0, 0, 0))
        aliases = {}
    else:
        state_spec = pl.BlockSpec((1,) + st_block, lambda i: (layer, i, 0, 0, 0))
        in_specs.append(pl.BlockSpec(memory_space=pl.ANY))
        args.append(prev_states)
        aliases = {len(args) - 1: 1}
    return pl.pallas_call(
        functools.partial(_ret_sample_kernel, ts=ts),
        grid=(tl.bs // SAMPLE_GROUP,),
        in_specs=in_specs,
        out_specs=[pl.BlockSpec((rows, D_RET), lambda i: (i, 0)), state_spec],
        out_shape=[jax.ShapeDtypeStruct((tl.n_sample, D_RET), bf16),
                   jax.ShapeDtypeStruct((depth, tl.bs, N_HEADS, HEAD_D, HEAD_D), f32)],
        input_output_aliases=aliases,
        compiler_params=_cparams(1),
        name="retention_sample",
    )(*args)


def _ln_silu(cv, g, b):
    mu = jnp.mean(cv, axis=-1, keepdims=True)
    var = jnp.mean(jnp.square(cv - mu), axis=-1, keepdims=True)
    return jax.nn.silu((cv - mu) * lax.rsqrt(var + EPS) * g + b)


def _conv_taps(window, w_ref, b_ref, n_rows):
    cols = []
    for col in range(D_CONV // LANES):
        lanes = slice(col * LANES, (col + 1) * LANES)
        acc = jnp.broadcast_to(b_ref[:, lanes], (n_rows, LANES))
        for tap in range(CONV_WIDTH):
            acc = acc + window(col, tap) * w_ref[tap:tap + 1, lanes]
        cols.append(acc)
    return jnp.concatenate(cols, axis=-1)


def _conv_prompt_kernel(a_ref, w_ref, b_ref, g_ref, bl_ref, o_ref, buf_ref, full_ref, cv_ref):
    j = pl.program_id(1)
    n_cols = D_CONV // LANES

    @pl.when(j == 0)
    def _():
        full_ref[:, 0:HALO_PAD, :] = jnp.zeros((n_cols, HALO_PAD, LANES), f32)

    @pl.when(j > 0)
    def _():
        full_ref[:, 0:HALO_PAD, :] = full_ref[:, TM:TM + HALO_PAD, :]

    for col in range(n_cols):
        full_ref[col, HALO_PAD:HALO_PAD + TM, :] = a_ref[:, col * LANES:(col + 1) * LANES]
    shift = HALO_PAD - CONV_HALO

    n_chunks = TM // CONV_ROWS

    def taps(idx, carry):
        col = idx // n_chunks
        r0 = pl.multiple_of((idx % n_chunks) * CONV_ROWS, CONV_ROWS)
        acc = jnp.broadcast_to(b_ref[col], (CONV_ROWS, LANES))
        for tap in range(CONV_WIDTH):
            acc = acc + full_ref[col, pl.ds(r0 + (tap + shift), CONV_ROWS), :] * w_ref[col, tap:tap + 1, :]
        cv_ref[col, pl.ds(r0, CONV_ROWS), :] = acc
        return carry

    lax.fori_loop(0, n_cols * n_chunks, taps, 0)

    for r0 in range(0, TM, NORM_ROWS):
        cv = jnp.concatenate([cv_ref[col, r0:r0 + NORM_ROWS, :] for col in range(n_cols)], axis=-1)
        o_ref[r0:r0 + NORM_ROWS, :] = _ln_silu(cv, g_ref[...], bl_ref[...]).astype(bf16)

    @pl.when(j == pl.num_programs(1) - 1)
    def _():
        buf_ref[...] = a_ref[TM - CONV_HALO:TM, :]


def _conv_prompt(tl, layer, a, w_conv, b_conv, g_ln, b_ln):
    tps = tl.tiles_per_seq
    n_cols = D_CONV // LANES
    depth = w_conv.shape[0]
    w_cols = w_conv.reshape(depth, CONV_WIDTH, n_cols, LANES).transpose(0, 2, 1, 3)
    b_cols = b_conv.reshape(depth, n_cols, 1, LANES)
    vec = pl.BlockSpec((None, 1, D_CONV), lambda b, j: (layer, 0, 0))
    return pl.pallas_call(
        _conv_prompt_kernel,
        grid=(tl.bp, tps),
        in_specs=[pl.BlockSpec((TM, D_CONV), lambda b, j: (b * tps + j, 0)),
                  pl.BlockSpec((None, n_cols, CONV_WIDTH, LANES), lambda b, j: (layer, 0, 0, 0)),
                  pl.BlockSpec((None, n_cols, 1, LANES), lambda b, j: (layer, 0, 0, 0)),
                  vec, vec],
        out_specs=[pl.BlockSpec((TM, D_CONV), lambda b, j: (b * tps + j, 0)),
                   pl.BlockSpec((None, CONV_HALO, D_CONV), lambda b, j: (b, 0, 0))],
        out_shape=[jax.ShapeDtypeStruct((tl.n_prompt, D_CONV), bf16),
                   jax.ShapeDtypeStruct((tl.bp, CONV_HALO, D_CONV), f32)],
        scratch_shapes=[pltpu.VMEM((n_cols, HALO_PAD + TM, LANES), f32), pltpu.VMEM((n_cols, TM, LANES), f32)],
        compiler_params=_cparams(2),
        name="conv_prompt",
    )(a, w_cols, b_cols, g_ln, b_ln)


def _conv_sample_kernel(a_ref, st_ref, w_ref, b_ref, g_ref, bl_ref, o_ref, buf_ref, full_ref, cv_ref, *, ts):
    for s in range(SAMPLE_GROUP):
        for col in range(D_CONV // LANES):
            lanes = slice(col * LANES, (col + 1) * LANES)
            full_ref[col, 0:CONV_HALO, :] = st_ref[s, :, lanes]
            full_ref[col, CONV_HALO:CONV_HALO + ts, :] = a_ref[s * ts:(s + 1) * ts, lanes]
            buf_ref[s, :, lanes] = full_ref[col, ts:ts + CONV_HALO, :]
        cv_ref[s * ts:(s + 1) * ts, :] = _conv_taps(lambda col, tap: full_ref[col, tap:tap + ts, :],
                                                    w_ref, b_ref, ts)
    o_ref[...] = _ln_silu(cv_ref[...], g_ref[...], bl_ref[...]).astype(bf16)


def _conv_sample(tl, layer, a, state_conv, w_conv, b_conv, g_ln, b_ln):
    ts = tl.ts
    rows = SAMPLE_GROUP * ts
    first = tl.n_prompt // rows
    vec = pl.BlockSpec((None, 1, D_CONV), lambda i: (layer, 0, 0))
    return pl.pallas_call(
        functools.partial(_conv_sample_kernel, ts=ts),
        grid=(tl.bs // SAMPLE_GROUP,),
        in_specs=[pl.BlockSpec((rows, D_CONV), lambda i: (first + i, 0)),
                  pl.BlockSpec((None, SAMPLE_GROUP, CONV_HALO, D_CONV), lambda i: (layer, i, 0, 0)),
                  pl.BlockSpec((None, CONV_WIDTH, D_CONV), lambda i: (layer, 0, 0)),
                  vec, vec, vec],
        out_specs=[pl.BlockSpec((rows, D_CONV), lambda i: (i, 0)),
                   pl.BlockSpec((SAMPLE_GROUP, CONV_HALO, D_CONV), lambda i: (i, 0, 0))],
        out_shape=[jax.ShapeDtypeStruct((tl.n_sample, D_CONV), bf16),
                   jax.ShapeDtypeStruct((tl.bs, CONV_HALO, D_CONV), f32)],
        scratch_shapes=[pltpu.VMEM((D_CONV // LANES, CONV_HALO + ts + SUBLANES, LANES), f32),
                        pltpu.VMEM((rows, D_CONV), f32)],
        compiler_params=_cparams(1),
        name="conv_sample",
    )(a, state_conv, w_conv, b_conv, g_ln, b_ln)


SORT_ROWS = 2 * TM + N_EXPERTS * SUBLANES
XS_HALF = D_MODEL // 2
XS_W = XS_HALF + LANES
u32 = jnp.uint32


def _pack_bf16_pair(x):
    lo = lax.shift_right_logical(lax.bitcast_convert_type(x[:, 0:XS_HALF], u32), u32(16))
    hi = lax.bitcast_convert_type(x[:, XS_HALF:D_MODEL], u32) & u32(0xFFFF0000)
    return hi | lo


def _unpack_bf16_pair(words):
    lo = lax.bitcast_convert_type(lax.shift_left(words, u32(16)), f32).astype(bf16)
    hi = lax.bitcast_convert_type(words & u32(0xFFFF0000), f32).astype(bf16)
    return lo, hi


def _split3(x):
    a = x.astype(bf16)
    r = x - a.astype(f32)
    b = r.astype(bf16)
    c = (r - b.astype(f32)).astype(bf16)
    return a, b, c


def _first_of4(vals, m):
    return jnp.where(vals[0] == m, 0.0, jnp.where(vals[1] == m, 1.0, jnp.where(vals[2] == m, 2.0, 3.0)))


def _rows_to_tile(rows, n_rows):
    sub = lax.broadcasted_iota(i32, (n_rows, TM), 0)
    out = jnp.zeros((n_rows, TM), f32)
    for r, val in enumerate(rows):
        out = jnp.where(sub == r, val, out)
    return out


def _outproj_kernel(*refs, np_steps, split_x):
    if split_x:
        xp_ref, xs_ref = refs[:2]
        refs = refs[2:]
    else:
        x_ref = refs[0]
        refs = refs[1:]
    (retp_ref, rets_ref, cvp_ref, cvs_ref, gts_ref, gtt_ref, shs_ref, sht_ref, scs_ref, sct_ref,
     g_ref, wo_ref, wrh_ref, wrl_ref, br_ref,
     xo_ref, h2_ref, rows_ref, cols_ref, cnt_ref) = refs
    is_s = pl.program_id(0) >= np_steps

    def step(sample):
        for t in range(STEP_TILES):
            rs = slice(t * TM, (t + 1) * TM)
            if split_x:
                x = xs_ref[rs, :] if sample else xp_ref[rs, :]
            else:
                x = x_ref[rs, :]
            ret = rets_ref[rs, :] if sample else retp_ref[rs, :]
            cv = cvs_ref[rs, :] if sample else cvp_ref[rs, :]
            gt, sh, sc = ((gtt_ref[rs, :], sht_ref[rs, :], sct_ref[rs, :]) if sample
                          else (gts_ref[...], shs_ref[...], scs_ref[...]))
            mix_out = (jnp.dot(ret, wo_ref[0:D_RET, :], preferred_element_type=f32)
                       + jnp.dot(cv, wo_ref[D_RET:D_RET + D_CONV, :], preferred_element_type=f32))
            xn = x + gt * mix_out
            xo_ref[rs, :] = xn
            h2 = _rms(xn, g_ref[...]) * (1.0 + sc) + sh
            h_hi = h2.astype(bf16)
            h2_ref[rs, :] = h_hi
            rows, cols, cnt = _route_tile(h2, h_hi, wrh_ref, wrl_ref, br_ref)
            rows_ref[t] = rows
            cols_ref[rs, :] = cols
            cnt_ref[t] = cnt

    pl.when(is_s)(functools.partial(step, True))
    pl.when(jnp.logical_not(is_s))(functools.partial(step, False))


def _route_tile(h2, h_hi, wrh_ref, wrl_ref, br_ref):
    h_lo = (h2 - h_hi.astype(f32)).astype(bf16)
    logits = (jnp.dot(h_hi, wrh_ref[...], preferred_element_type=f32)
              + jnp.dot(h_hi, wrl_ref[...], preferred_element_type=f32)
              + jnp.dot(h_lo, wrh_ref[...], preferred_element_type=f32)) + br_ref[...]
    lt = logits.T
    row = [lt[e:e + 1, :] for e in range(N_EXPERTS)]
    top = functools.reduce(jnp.maximum, row)
    ex = [jnp.exp(r - top) for r in row]
    den = functools.reduce(jnp.add, ex)
    p = [v / den for v in ex]

    best = None
    for g in range(N_GROUPS):
        a = p[g * GROUP_SIZE:(g + 1) * GROUP_SIZE]
        m1 = functools.reduce(jnp.maximum, a)
        i1 = _first_of4(a, m1)
        b = [jnp.where(i1 == float(j), -1.0, a[j]) for j in range(GROUP_SIZE)]
        m2 = functools.reduce(jnp.maximum, b)
        i2 = _first_of4(b, m2)
        cand = (m1 + m2, m1, m2, i1 + float(g * GROUP_SIZE), i2 + float(g * GROUP_SIZE))
        if best is None:
            best = cand
        else:
            take = cand[0] > best[0]
            best = tuple(jnp.where(take, c, o) for c, o in zip(cand, best))
    _, m1, m2, e0, e1 = best
    denom = m1 + m2
    w0 = m1 / denom
    w1 = m2 / denom

    ex_id = lax.broadcasted_iota(i32, (N_EXPERTS, TM), 0).astype(f32)
    sel0 = ex_id == e0
    sel1 = ex_id == e1
    ind = jnp.where(sel0 | sel1, 1.0, 0.0)
    t_r = lax.broadcasted_iota(i32, (TM, TM), 0)
    t_c = lax.broadcasted_iota(i32, (TM, TM), 1)
    earlier = jnp.where(t_r < t_c, 1.0, 0.0).astype(bf16)
    prefix = jnp.dot(ind.astype(bf16), earlier, preferred_element_type=f32)
    cnt = jnp.sum(ind, axis=-1, keepdims=True)
    cnt8 = jnp.floor((cnt + float(SUBLANES - 1)) * (1.0 / SUBLANES)) * float(SUBLANES)
    e_r = lax.broadcasted_iota(i32, (N_EXPERTS, N_EXPERTS), 0)
    e_c = lax.broadcasted_iota(i32, (N_EXPERTS, N_EXPERTS), 1)
    below = jnp.where(e_c < e_r, 1.0, 0.0).astype(bf16)
    seg_off = jnp.dot(below, jnp.broadcast_to(cnt8, (N_EXPERTS, TM)).astype(bf16),
                      preferred_element_type=f32)
    where_to = seg_off + prefix
    pos0 = jnp.sum(jnp.where(sel0, where_to, 0.0), axis=0, keepdims=True)
    pos1 = jnp.sum(jnp.where(sel1, where_to, 0.0), axis=0, keepdims=True)

    w0p = [v.astype(f32) for v in _split3(w0)]
    w1p = [v.astype(f32) for v in _split3(w1)]
    info = [pos0, pos1] + w0p + w1p
    return (_rows_to_tile(info, SUBLANES), _rows_to_tile(info, LANES).T,
            jnp.broadcast_to(cnt, (N_EXPERTS, LANES)))


def _outproj(tl, layer, x, ret_p, ret_s, cv_p, cv_s, mod_seq, mod_tok, g_norm, w_out_bf, wr_hi, wr_lo, br_pad):
    split_x = isinstance(x, tuple)
    tok_spec = pl.BlockSpec((TS, D_MODEL), lambda i: (i, 0))
    p_spec = lambda w: pl.BlockSpec((TS, w), lambda i: (tl.prompt_block(i), 0))
    s_spec = lambda w: pl.BlockSpec((TS, w), lambda i: (tl.sample_block(i), 0))
    if split_x:
        x_args, x_specs = list(x), [p_spec(D_MODEL), s_spec(D_MODEL)]
    else:
        x_args, x_specs = [x], [tok_spec]
    mods = []
    for col in (2, 3, 4):
        mods += list(_mod_specs(tl, layer, col))
    wr_spec = pl.BlockSpec((D_MODEL, LANES), lambda i: (0, 0))
    return pl.pallas_call(
        functools.partial(_outproj_kernel, np_steps=tl.np_steps, split_x=split_x),
        grid=(tl.n_steps,),
        in_specs=x_specs + [p_spec(D_RET), s_spec(D_RET), p_spec(D_CONV), s_spec(D_CONV)] + mods + [
            pl.BlockSpec((None, 1, D_MODEL), lambda i: (layer, 0, 0)),
            pl.BlockSpec((None, D_MODEL, D_MODEL), lambda i: (layer, 0, 0)),
            wr_spec, wr_spec,
            pl.BlockSpec((1, LANES), lambda i: (0, 0)),
        ],
        out_specs=[tok_spec, tok_spec,
                   pl.BlockSpec((STEP_TILES, SUBLANES, TM), lambda i: (i, 0, 0)),
                   pl.BlockSpec((TS, LANES), lambda i: (i, 0)),
                   pl.BlockSpec((STEP_TILES, N_EXPERTS, LANES), lambda i: (i, 0, 0))],
        out_shape=[jax.ShapeDtypeStruct((tl.n_tok, D_MODEL), f32),
                   jax.ShapeDtypeStruct((tl.n_tok, D_MODEL), bf16),
                   jax.ShapeDtypeStruct((tl.n_tiles, SUBLANES, TM), f32),
                   jax.ShapeDtypeStruct((tl.n_tok, LANES), f32),
                   jax.ShapeDtypeStruct((tl.n_tiles, N_EXPERTS, LANES), f32)],
        compiler_params=_cparams(1),
        name="outproj_router",
    )(*x_args, ret_p, ret_s, cv_p, cv_s, *([mod_seq, mod_tok] * 3), g_norm, w_out_bf, wr_hi, wr_lo, br_pad)


N_CHUNKS = SORT_ROWS // SUBLANES


class _Layout:
    def __init__(self, n_tiles):
        self.n_tiles = n_tiles
        self.tail_start = n_tiles * N_CHUNKS
        self.tail_n8 = self.tail_start + N_EXPERTS
        worst = 2 * n_tiles * TM + n_tiles * N_EXPERTS * (SUBLANES - 1) + N_EXPERTS * (BM - SUBLANES)
        self.n_blocks = -(-worst // BM)
        self.cap = self.n_blocks * BM
        self.dump = self.cap
        self.xs_rows = self.cap + -(-2 * STEP_TILES * SORT_ROWS // BM) * BM


def _moe_tables(lay, tile_counts):
    c8 = ((tile_counts.astype(i32) + SUBLANES - 1) // SUBLANES) * SUBLANES
    base8 = jnp.cumsum(c8, axis=0) - c8
    tot8 = jnp.sum(c8, axis=0)
    region = ((tot8 + BM - 1) // BM) * BM
    g_end = jnp.cumsum(region)
    g_start = g_end - region
    seg_end = jnp.cumsum(c8, axis=1)
    seg_dst = g_start[None, :] + base8
    n_used = g_end[-1] // BM
    blk = jnp.arange(lay.n_blocks, dtype=i32)
    block_e = jnp.minimum(jnp.sum((g_end[None, :] <= blk[:, None] * BM).astype(i32), axis=1), N_EXPERTS - 1)
    block_e = jnp.where(blk < n_used, block_e, block_e[n_used - 1])
    row0 = jnp.arange(N_CHUNKS, dtype=i32) * SUBLANES
    owner = jnp.sum((seg_end[:, None, :] <= row0[None, :, None]).astype(i32), axis=-1)
    onehot = (owner[:, :, None] == jnp.arange(N_EXPERTS, dtype=i32)[None, None, :]).astype(i32)
    delta = seg_dst - (seg_end - c8)
    chunk_dst = jnp.where(owner < N_EXPERTS, row0[None, :] + jnp.sum(onehot * delta[:, None, :], axis=-1), -1)
    tab = jnp.concatenate([chunk_dst.ravel(), g_start + tot8, (region - tot8) // SUBLANES]).astype(i32)
    ids = jnp.arange(N_EXPERTS, dtype=i32)
    later = jnp.where((ids[None, :] > ids[:, None]) & (region[None, :] > 0), ids[None, :], N_EXPERTS)
    next_e = jnp.min(later, axis=1)
    next_e = jnp.where(next_e == N_EXPERTS, -1, next_e)
    return tab, jnp.concatenate([block_e, n_used[None], next_e]).astype(i32)


def _for_chunks(n, fn):
    def body(c, carry):
        fn(c)
        return carry

    lax.fori_loop(0, n, body, 0)


def _dispatch_kernel(tab_ref, h2_ref, rows_ref, cols_ref, xs_hbm, sorted_ref, zero_ref, sem, *, lay, n_steps):
    i = pl.program_id(0)
    slot = i % 2
    step_rows = STEP_TILES * SORT_ROWS

    def tail_copy(dst):
        return pltpu.make_async_copy(zero_ref, xs_hbm.at[pl.ds(dst, SUBLANES)], sem.at[2])

    def wait_step(slot_):
        pltpu.make_async_copy(sorted_ref.at[slot_], xs_hbm.at[pl.ds(0, step_rows)], sem.at[slot_]).wait()

    @pl.when(i == 0)
    def _():
        zero_ref[...] = jnp.zeros_like(zero_ref)
        for e in range(N_EXPERTS):
            start = tab_ref[lay.tail_start + e]
            _for_chunks(tab_ref[lay.tail_n8 + e],
                        lambda c: tail_copy(pl.multiple_of(start + c * SUBLANES, SUBLANES)).start())

    @pl.when(i >= 2)
    def _():
        wait_step(slot)

    r_id = lax.broadcasted_iota(i32, (SORT_ROWS, TM), 0).astype(f32)
    lane = lax.broadcasted_iota(i32, (TM, LANES), 1)
    for t in range(STEP_TILES):
        pos0 = rows_ref[t, 0:1, :]
        pos1 = rows_ref[t, 1:2, :]
        p0 = r_id == pos0
        p1 = r_id == pos1
        perm = jnp.where(p0 | p1, 1.0, 0.0).astype(bf16)
        cols = cols_ref[t * TM:(t + 1) * TM, :]
        wpart0 = jnp.where((lane >= 2) & (lane < 5), cols, 0.0).astype(bf16)
        wpart1 = jnp.where((lane >= 5) & (lane < 8), cols, 0.0).astype(bf16)
        sw = (jnp.dot(jnp.where(p0, 1.0, 0.0).astype(bf16), wpart0, preferred_element_type=f32)
              + jnp.dot(jnp.where(p1, 1.0, 0.0).astype(bf16), wpart1, preferred_element_type=f32))
        base = t * SORT_ROWS
        sorted_ref[slot, base:base + SORT_ROWS, 0:XS_HALF] = _pack_bf16_pair(jnp.dot(
            perm, h2_ref[t * TM:(t + 1) * TM, :], preferred_element_type=f32))
        sorted_ref[slot, base:base + SORT_ROWS, XS_HALF:XS_W] = lax.bitcast_convert_type(jnp.broadcast_to(
            jnp.sum(sw, axis=-1, keepdims=True), (SORT_ROWS, LANES)), u32)

    for t in range(STEP_TILES):
        for c in range(N_CHUNKS):
            row = t * SORT_ROWS + c * SUBLANES
            dst = tab_ref[(i * STEP_TILES + t) * N_CHUNKS + c]
            dst = jnp.where(dst < 0, lay.dump + slot * step_rows + row, dst)
            pltpu.make_async_copy(sorted_ref.at[slot, pl.ds(row, SUBLANES)],
                                  xs_hbm.at[pl.ds(pl.multiple_of(dst, SUBLANES), SUBLANES)], sem.at[slot]).start()

    @pl.when(i == n_steps - 1)
    def _():
        if n_steps >= 2:
            wait_step(1 - slot)
        wait_step(slot)
        for e in range(N_EXPERTS):
            _for_chunks(tab_ref[lay.tail_n8 + e], lambda c: tail_copy(0).wait())


def _dispatch(tl, lay, tab, h2, rows, cols):
    grid_spec = pltpu.PrefetchScalarGridSpec(
        num_scalar_prefetch=1,
        grid=(tl.n_steps,),
        in_specs=[pl.BlockSpec((TS, D_MODEL), lambda i, t: (i, 0)),
                  pl.BlockSpec((STEP_TILES, SUBLANES, TM), lambda i, t: (i, 0, 0)),
                  pl.BlockSpec((TS, LANES), lambda i, t: (i, 0))],
        out_specs=pl.BlockSpec(memory_space=pl.ANY),
        scratch_shapes=[pltpu.VMEM((2, STEP_TILES * SORT_ROWS, XS_W), u32), pltpu.VMEM((SUBLANES, XS_W), u32),
                        pltpu.SemaphoreType.DMA((3,))],
    )
    return pl.pallas_call(
        functools.partial(_dispatch_kernel, lay=lay, n_steps=tl.n_steps),
        grid_spec=grid_spec,
        out_shape=jax.ShapeDtypeStruct((lay.xs_rows, XS_W), u32),
        compiler_params=_cparams(1),
        name="moe_dispatch",
    )(tab, h2, rows, cols)


def _expert_kernel(be_ref, xs_ref, wg_hbm, wu_hbm, wd_hbm, ys_ref, stage, w_bf, sem, *, n_blocks, layer):
    j = pl.program_id(0)

    def fetch(e):
        return [pltpu.make_async_copy(w.at[layer, e], stage.at[k], sem.at[k])
                for k, w in enumerate((wg_hbm, wu_hbm, wd_hbm))]

    @pl.when(j < be_ref[n_blocks])
    def _():
        e = be_ref[j]

        @pl.when(j == 0)
        def _():
            for copy in fetch(e):
                copy.start()

        @pl.when((j == 0) | (e != be_ref[jnp.maximum(j - 1, 0)]))
        def _():
            for copy in fetch(e):
                copy.wait()
            for k in range(3):
                w_bf[k] = stage[k].astype(bf16)
            nxt = be_ref[n_blocks + 1 + e]

            @pl.when(nxt >= 0)
            def _():
                for copy in fetch(nxt):
                    copy.start()

        x_lo, x_hi = _unpack_bf16_pair(xs_ref[:, 0:XS_HALF])

        def first_layer(k):
            return (jnp.dot(x_lo, w_bf[k, 0:XS_HALF, :], preferred_element_type=f32)
                    + jnp.dot(x_hi, w_bf[k, XS_HALF:D_MODEL, :], preferred_element_type=f32))

        mid = (jax.nn.silu(first_layer(0)) * first_layer(1)).astype(bf16)
        slot_w = lax.bitcast_convert_type(xs_ref[:, XS_HALF:XS_HALF + 1], f32)
        ys_ref[...] = jnp.dot(mid, w_bf[2], preferred_element_type=f32) * slot_w


def _experts(layer, lay, block_e, xs, w_gate, w_up, w_down):
    n_blocks = lay.n_blocks
    d_ff = w_gate.shape[-1]
    assert d_ff == D_MODEL
    used = lambda j, be: jnp.minimum(j, be[n_blocks] - 1)
    any_spec = pl.BlockSpec(memory_space=pl.ANY)
    grid_spec = pltpu.PrefetchScalarGridSpec(
        num_scalar_prefetch=1,
        grid=(n_blocks,),
        in_specs=[pl.BlockSpec((BM, XS_W), lambda j, be: (used(j, be), 0)), any_spec, any_spec, any_spec],
        out_specs=pl.BlockSpec((BM, D_MODEL), lambda j, be: (used(j, be), 0)),
        scratch_shapes=[pltpu.VMEM((3, D_MODEL, d_ff), f32), pltpu.VMEM((3, D_MODEL, d_ff), bf16),
                        pltpu.SemaphoreType.DMA((3,))],
    )
    return pl.pallas_call(
        functools.partial(_expert_kernel, n_blocks=n_blocks, layer=layer),
        grid_spec=grid_spec,
        out_shape=jax.ShapeDtypeStruct((lay.cap, D_MODEL), f32),
        compiler_params=_cparams(1),
        name="moe_experts",
    )(block_e, xs, w_gate, w_up, w_down)


def _combine_kernel(tab_ref, ys_hbm, cols_ref, x_ref, gts_ref, gtt_ref, *rest, n_steps, np_steps, final):
    if final:
        gf_ref, yp_ref, ysm_ref, staged, sem = rest
    else:
        xo_ref, staged, sem = rest
    i = pl.program_id(0)
    slot = i % 2
    is_s = i >= np_steps

    def start_step(step, slot_):
        for c in range(STEP_TILES * N_CHUNKS):
            src = jnp.maximum(tab_ref[step * (STEP_TILES * N_CHUNKS) + c], 0)
            pltpu.make_async_copy(ys_hbm.at[pl.ds(pl.multiple_of(src, SUBLANES), SUBLANES)],
                                  staged.at[slot_, pl.ds(c * SUBLANES, SUBLANES)], sem.at[slot_]).start()

    @pl.when(i == 0)
    def _():
        start_step(0, 0)

    @pl.when(i + 1 < n_steps)
    def _():
        start_step(i + 1, 1 - slot)

    pltpu.make_async_copy(ys_hbm.at[pl.ds(0, STEP_TILES * SORT_ROWS)], staged.at[slot], sem.at[slot]).wait()

    def step(sample):
        lane = lax.broadcasted_iota(i32, (TM, SORT_ROWS), 1).astype(f32)
        for t in range(STEP_TILES):
            rs = slice(t * TM, (t + 1) * TM)
            unperm = jnp.where((lane == cols_ref[rs, 0:1]) | (lane == cols_ref[rs, 1:2]), 1.0, 0.0).astype(bf16)
            parts = _split3(staged[slot, t * SORT_ROWS:(t + 1) * SORT_ROWS, :])
            ff = sum(jnp.dot(unperm, part, preferred_element_type=f32) for part in parts)
            xn = x_ref[rs, :] + (gtt_ref[rs, :] if sample else gts_ref[...]) * ff
            if final:
                (ysm_ref if sample else yp_ref)[rs, :] = _rms(xn, gf_ref[...])
            else:
                xo_ref[rs, :] = xn

    pl.when(is_s)(functools.partial(step, True))
    pl.when(jnp.logical_not(is_s))(functools.partial(step, False))


def _combine(tl, lay, layer, tab, ys, cols, x, mod_seq, mod_tok, g_final):
    final = g_final is not None
    tok_spec = pl.BlockSpec((TS, D_MODEL), lambda i, t: (i, 0))
    gt_seq, gt_tok = _mod_specs(tl, layer, 5)
    in_specs = [pl.BlockSpec(memory_space=pl.ANY), pl.BlockSpec((TS, LANES), lambda i, t: (i, 0)),
                tok_spec, gt_seq, gt_tok]
    args = [tab, ys, cols, x, mod_seq, mod_tok]
    if final:
        in_specs.append(pl.BlockSpec((1, D_MODEL), lambda i, t: (0, 0)))
        args.append(g_final)
        out_specs = [pl.BlockSpec((TS, D_MODEL), lambda i, t: (tl.prompt_block(i), 0)),
                     pl.BlockSpec((TS, D_MODEL), lambda i, t: (tl.sample_block(i), 0))]
        out_shape = [jax.ShapeDtypeStruct((tl.n_prompt, D_MODEL), f32),
                     jax.ShapeDtypeStruct((tl.n_sample, D_MODEL), f32)]
    else:
        out_specs = tok_spec
        out_shape = jax.ShapeDtypeStruct((tl.n_tok, D_MODEL), f32)
    grid_spec = pltpu.PrefetchScalarGridSpec(
        num_scalar_prefetch=1,
        grid=(tl.n_steps,),
        in_specs=in_specs,
        out_specs=out_specs,
        scratch_shapes=[pltpu.VMEM((2, STEP_TILES * SORT_ROWS, D_MODEL), f32), pltpu.SemaphoreType.DMA((2,))],
    )
    return pl.pallas_call(
        functools.partial(_combine_kernel, n_steps=tl.n_steps, np_steps=tl.np_steps, final=final),
        grid_spec=grid_spec,
        out_shape=out_shape,
        compiler_params=_cparams(1),
        name="moe_combine",
    )(*args)


def _rope_tables(tl):
    half = HEAD_D // 2
    inv = ROPE_BASE ** (-jnp.arange(half, dtype=f32) / half)
    pos_p = jnp.arange(tl.tp, dtype=i32)
    pos_s = PAST_LEN + jnp.arange(tl.ts, dtype=i32)
    pos = jnp.concatenate([pos_p, jnp.tile(pos_s, tl.bs)])
    ang = pos.astype(f32)[:, None] * inv[None, :]
    cos, sin = jnp.cos(ang), jnp.sin(ang)
    return jnp.concatenate([cos, cos], axis=-1), jnp.concatenate([-sin, sin], axis=-1)


def kernel(x_prompt, x_sample, state_ret, state_conv, c_prompt, c_sample, w_mod, b_mod, g_mix_norm, w_in,
           w_conv, b_conv, g_conv_ln, b_conv_ln, g_ret_gn, w_out, g_ffn_norm, w_router, b_router,
           w_exp_gate, w_exp_up, w_exp_down, g_final):
    bp, tp, _ = x_prompt.shape
    bs, ts, _ = x_sample.shape
    depth = w_mod.shape[0]
    tl = _Tiles(bp, tp, bs, ts)
    lay = _Layout(tl.n_tiles)

    c_all = jnp.concatenate([c_prompt, jnp.repeat(c_sample, ts, axis=0)], axis=0)
    mod_seq, mod_tok = _modulation(c_all, bp, w_mod, b_mod)
    mod_seq = mod_seq.reshape(depth, bp, 1, N_MOD * D_MODEL)

    cos_tab, sin_tab = _rope_tables(tl)
    w_in_bf = w_in.astype(bf16)
    w_out_bf = w_out.astype(bf16)
    wr_pad = jnp.pad(w_router.astype(f32), ((0, 0), (0, LANES - N_EXPERTS)))
    wr_hi = wr_pad.astype(bf16)
    wr_lo = (wr_pad - wr_hi.astype(f32)).astype(bf16)
    br_pad = jnp.pad(b_router.astype(f32), (0, LANES - N_EXPERTS)).reshape(1, LANES)
    vec3 = lambda t: t.reshape(depth, 1, t.shape[-1])
    g_mix3, g_ffn3, gn3 = vec3(g_mix_norm), vec3(g_ffn_norm), vec3(g_ret_gn)
    b_conv3, g_ln3, b_ln3 = vec3(b_conv), vec3(g_conv_ln), vec3(b_conv_ln)

    x = (x_prompt.reshape(tl.n_prompt, D_MODEL), x_sample.reshape(tl.n_sample, D_MODEL))
    ret_p, conv_p, conv_s = [], [], []
    ret_s_all = None
    for layer in range(depth):
        q, k, v, gate, a = _inproj(tl, layer, x, mod_seq, mod_tok, g_mix3, w_in_bf, cos_tab, sin_tab)
        ro_p, s_p = _retention_prompt(tl, layer, q, k, v, gate, gn3)
        ro_s, ret_s_all = _retention_sample(tl, layer, q, k, v, gate, state_ret, gn3, ret_s_all)
        co_p, buf_p = _conv_prompt(tl, layer, a, w_conv, b_conv3, g_ln3, b_ln3)
        co_s, buf_s = _conv_sample(tl, layer, a, state_conv, w_conv, b_conv3, g_ln3, b_ln3)
        x_mid, h2, rows, cols, tile_counts = _outproj(
            tl, layer, x, ro_p, ro_s, co_p, co_s, mod_seq, mod_tok, g_ffn3, w_out_bf, wr_hi, wr_lo, br_pad)
        tab, block_e = _moe_tables(lay, tile_counts[:, :, 0])
        xs = _dispatch(tl, lay, tab, h2, rows, cols)
        ys = _experts(layer, lay, block_e, xs, w_exp_gate, w_exp_up, w_exp_down)
        last = layer == depth - 1
        x = _combine(tl, lay, layer, tab, ys, cols, x_mid, mod_seq, mod_tok,
                     g_final.reshape(1, D_MODEL) if last else None)
        ret_p.append(s_p)
        conv_p.append(buf_p)
        conv_s.append(buf_s)
    y_p, y_s = x
    return (y_p.reshape(bp, tp, D_MODEL), y_s.reshape(bs, ts, D_MODEL),
            jnp.stack(ret_p), jnp.stack(conv_p), ret_s_all, jnp.stack(conv_s))
```

```python
import functools

import jax
import jax.numpy as jnp
from jax import lax
from jax.experimental import pallas as pl
from jax.experimental.pallas import tpu as pltpu

f32 = jnp.float32
bf16 = jnp.bfloat16
i32 = jnp.int32

D_MODEL = 1024
D_RET = 512
D_CONV = 512
N_HEADS = 4
HEAD_D = 128
RET_CHUNK = 128
RET_CHUNKS_PER_STEP = 4
ROPE_BASE = 10000.0
CONV_WIDTH = 31
CONV_HALO = CONV_WIDTH - 1
N_EXPERTS = 16
N_GROUPS = 4
GROUP_SIZE = N_EXPERTS // N_GROUPS
N_MOD = 6
EPS = 1e-6
PAST_LEN = 16384
D_IN = 4 * D_RET + 2 * D_CONV

LANES = 128
SUBLANES = 8
TM = 256
STEP_TILES = 2
TS = TM * STEP_TILES
BM = 512
CONV_ROWS = 128
NORM_ROWS = 64
SAMPLE_GROUP = 8
HALO_PAD = 32
VMEM_LIMIT = 56 * 1024 * 1024


def _cparams(n_axes, vmem=VMEM_LIMIT):
    return pltpu.CompilerParams(dimension_semantics=("arbitrary",) * n_axes, vmem_limit_bytes=vmem)


def _mod_kernel(c_ref, w_ref, b_ref, seq_ref, tok_ref):
    cond = jax.nn.silu(c_ref[...]).astype(bf16)
    mod = jnp.dot(cond, w_ref[...].astype(bf16), preferred_element_type=f32) + b_ref[...]
    n_seq = seq_ref.shape[0]
    seq_ref[...] = mod[0:n_seq, :]
    tok_ref[...] = mod[n_seq:, :]


def _modulation(c_all, n_seq, w_mod, b_mod):
    depth = w_mod.shape[0]
    m = c_all.shape[0]
    assert n_seq % SUBLANES == 0
    return pl.pallas_call(
        _mod_kernel,
        grid=(depth, N_MOD),
        in_specs=[
            pl.BlockSpec((m, D_MODEL), lambda l, j: (0, 0)),
            pl.BlockSpec((None, D_MODEL, D_MODEL), lambda l, j: (l, 0, j)),
            pl.BlockSpec((None, 1, D_MODEL), lambda l, j: (l, 0, j)),
        ],
        out_specs=[pl.BlockSpec((None, n_seq, D_MODEL), lambda l, j: (l, 0, j)),
                   pl.BlockSpec((None, m - n_seq, D_MODEL), lambda l, j: (l, 0, j))],
        out_shape=[jax.ShapeDtypeStruct((depth, n_seq, N_MOD * D_MODEL), f32),
                   jax.ShapeDtypeStruct((depth, m - n_seq, N_MOD * D_MODEL), f32)],
        compiler_params=_cparams(2),
        name="modulation",
    )(c_all, w_mod, b_mod.reshape(depth, 1, N_MOD * D_MODEL))


class _Tiles:
    def __init__(self, bp, tp, bs, ts):
        self.bp, self.tp, self.bs, self.ts = bp, tp, bs, ts
        self.n_prompt = bp * tp
        self.n_sample = bs * ts
        self.n_tok = self.n_prompt + self.n_sample
        assert tp % TS == 0 and self.n_sample % TS == 0
        self.tiles_per_seq = tp // TM
        self.np_tiles = self.n_prompt // TM
        self.n_tiles = self.n_tok // TM
        self.steps_per_seq = tp // TS
        self.np_steps = self.n_prompt // TS
        self.n_steps = self.n_tok // TS

    def prompt_block(self, i):
        return jnp.minimum(i, self.np_steps - 1)

    def sample_block(self, i):
        return jnp.maximum(i - self.np_steps, 0)

    def seq_index(self, i):
        return jnp.minimum(i // self.steps_per_seq, self.bp - 1)


def _mod_specs(tl, layer, col):
    seq = pl.BlockSpec((None, None, 1, D_MODEL), lambda i, *_: (layer, tl.seq_index(i), 0, col))
    tok = pl.BlockSpec((None, TS, D_MODEL), lambda i, *_: (layer, tl.sample_block(i), col))
    return seq, tok


def _rms(x, g):
    return x * lax.rsqrt(jnp.mean(x * x, axis=-1, keepdims=True) + EPS) * g


def _head_norm_gate(o, gn, gate):
    mu = jnp.mean(o, axis=-1, keepdims=True)
    var = jnp.mean(jnp.square(o - mu), axis=-1, keepdims=True)
    return jax.nn.silu(gate) * ((o - mu) * lax.rsqrt(var + EPS) * gn)


def _dot_nt(a, b):
    return lax.dot_general(a, b, (((1,), (1,)), ((), ())), preferred_element_type=f32)


def _dot_tn(a, b):
    return lax.dot_general(a, b, (((0,), (0,)), ((), ())), preferred_element_type=f32)


def _ret_prompt_kernel(q_ref, k_ref, v_ref, gate_ref, dec_ref, qd_ref, kd_ref, cd_ref, gn_ref,
                       o_ref, s_out_ref, s_ref):
    c = pl.program_id(1)

    @pl.when(c == 0)
    def _():
        s_ref[...] = jnp.zeros_like(s_ref)

    for ci in range(RET_CHUNKS_PER_STEP):
        rows = slice(ci * RET_CHUNK, (ci + 1) * RET_CHUNK)
        for hd in range(N_HEADS):
            sl = slice(hd * HEAD_D, (hd + 1) * HEAD_D)
            kh = k_ref[rows, sl]
            qb = q_ref[rows, sl]
            kb = kh.astype(bf16)
            vb = v_ref[rows, sl]
            s_old = s_ref[hd]
            scores = _dot_nt(qb, kb) * dec_ref[hd]
            inner = jnp.dot(scores.astype(bf16), vb, preferred_element_type=f32)
            cross = jnp.dot(qb, s_old.astype(bf16), preferred_element_type=f32) * qd_ref[hd]
            s_ref[hd] = s_old * cd_ref[hd] + _dot_tn((kh * kd_ref[hd]).astype(bf16), vb)
            o_ref[rows, sl] = _head_norm_gate(inner + cross, gn_ref[:, sl], gate_ref[rows, sl]).astype(bf16)

    @pl.when(c == pl.num_programs(1) - 1)
    def _():
        s_out_ref[...] = s_ref[...]


def _decay_tables(chunk, true_len):
    lg = jnp.log(1.0 - 2.0 ** (-5.0 - jnp.arange(N_HEADS, dtype=f32)))
    idx = jnp.arange(chunk, dtype=f32)
    rel = idx[:, None] - idx[None, :]
    decay = jnp.where(rel[None] >= 0, jnp.exp(jnp.maximum(rel, 0.0)[None] * lg[:, None, None]), 0.0)
    q_decay = jnp.exp((idx[None, :] + 1.0) * lg[:, None])
    k_decay = jnp.exp((true_len - 1.0 - idx[None, :]) * lg[:, None])
    c_decay = jnp.exp(true_len * lg)
    return decay, q_decay, k_decay, c_decay


def _retention_prompt(tl, layer, q, k, v, gate, g_ret_gn):
    step_rows = RET_CHUNK * RET_CHUNKS_PER_STEP
    assert tl.tp % step_rows == 0
    n_chunks = tl.tp // step_rows
    decay, q_decay, k_decay, c_decay = _decay_tables(RET_CHUNK, RET_CHUNK)
    bcast = lambda t: jnp.broadcast_to(t[:, :, None], (N_HEADS, RET_CHUNK, HEAD_D))
    cd = jnp.broadcast_to(c_decay[:, None, None], (N_HEADS, 1, HEAD_D))
    tok_spec = pl.BlockSpec((step_rows, D_RET), lambda b, c: (b * n_chunks + c, 0))
    tab_spec = pl.BlockSpec((N_HEADS, RET_CHUNK, HEAD_D), lambda b, c: (0, 0, 0))
    return pl.pallas_call(
        _ret_prompt_kernel,
        grid=(tl.bp, n_chunks),
        in_specs=[tok_spec] * 4 + [tab_spec] * 3 + [
            pl.BlockSpec((N_HEADS, 1, HEAD_D), lambda b, c: (0, 0, 0)),
            pl.BlockSpec((None, 1, D_RET), lambda b, c: (layer, 0, 0)),
        ],
        out_specs=[tok_spec, pl.BlockSpec((None, N_HEADS, HEAD_D, HEAD_D), lambda b, c: (b, 0, 0, 0))],
        out_shape=[jax.ShapeDtypeStruct((tl.n_prompt, D_RET), bf16),
                   jax.ShapeDtypeStruct((tl.bp, N_HEADS, HEAD_D, HEAD_D), f32)],
        scratch_shapes=[pltpu.VMEM((N_HEADS, HEAD_D, HEAD_D), f32)],
        compiler_params=_cparams(2),
        name="retention_prompt",
    )(q, k, v, gate, decay, bcast(q_decay), bcast(k_decay), cd, g_ret_gn)


def _ret_sample_kernel(q_ref, k_ref, v_ref, gate_ref, s_in_ref, dec_ref, qd_ref, kd_ref, cd_ref, gn_ref,
                       *rest, ts):
    o_ref, s_all_ref = rest[-2:]
    s_out_ref = s_all_ref.at[0]
    for other in range(1, s_all_ref.shape[0]):
        s_all_ref[other] = jnp.zeros(s_all_ref.shape[1:], f32)
    seqs_per_tile = SUBLANES // ts
    row = lax.broadcasted_iota(i32, (SUBLANES, HEAD_D), 0)
    q_all = q_ref[...].astype(f32)
    v_all = v_ref[...].astype(f32)
    outs = []
    for t in range(SAMPLE_GROUP // seqs_per_tile):
        rows = slice(t * SUBLANES, (t + 1) * SUBLANES)
        heads = []
        for hd in range(N_HEADS):
            sl = slice(hd * HEAD_D, (hd + 1) * HEAD_D)
            qh = q_all[rows, sl]
            kh = k_ref[rows, sl] * kd_ref[hd]
            vb = v_all[rows, sl].astype(bf16)
            qb = qh.astype(bf16)
            scores = _dot_nt(qb, k_ref[rows, sl].astype(bf16)) * dec_ref[hd]
            o = jnp.dot(scores.astype(bf16), vb, preferred_element_type=f32)
            for s in range(seqs_per_tile):
                b = t * seqs_per_tile + s
                mine = (row >= s * ts) & (row < (s + 1) * ts)
                s_old = s_in_ref[b, hd]
                q_s = jnp.where(mine, qh, 0.0).astype(bf16)
                k_s = jnp.where(mine, kh, 0.0).astype(bf16)
                o = o + jnp.dot(q_s, s_old.astype(bf16), preferred_element_type=f32) * qd_ref[hd]
                s_out_ref[b, hd] = s_old * cd_ref[hd] + _dot_tn(k_s, vb)
            heads.append(_head_norm_gate(o, gn_ref[:, sl], gate_ref[rows, sl]))
        outs.append(jnp.concatenate(heads, axis=-1))
    o_ref[...] = jnp.concatenate(outs, axis=0).astype(bf16)


def _retention_sample(tl, layer, q, k, v, gate, state_ret, g_ret_gn, prev_states):
    ts = tl.ts
    depth = state_ret.shape[0]
    assert SUBLANES % ts == 0 and tl.bs % SAMPLE_GROUP == 0
    seqs_per_tile = SUBLANES // ts
    decay, q_decay, k_decay, c_decay = _decay_tables(ts, ts)
    eye = jnp.eye(seqs_per_tile, dtype=f32)
    dec_tile = jnp.einsum("ab,hij->haibj", eye, decay).reshape(N_HEADS, SUBLANES, SUBLANES)
    tile_rows = lambda t: jnp.broadcast_to(jnp.tile(t, (1, seqs_per_tile))[:, :, None],
                                           (N_HEADS, SUBLANES, HEAD_D))
    cd = jnp.broadcast_to(c_decay[:, None, None], (N_HEADS, 1, HEAD_D))
    rows = SAMPLE_GROUP * ts
    first = tl.n_prompt // rows
    tok_spec = pl.BlockSpec((rows, D_RET), lambda i: (first + i, 0))
    const3 = lambda shape: pl.BlockSpec(shape, lambda i: (0, 0, 0))
    st_block = (SAMPLE_GROUP, N_HEADS, HEAD_D, HEAD_D)
    in_specs = [tok_spec] * 4 + [
        pl.BlockSpec((None,) + st_block, lambda i: (layer, i, 0, 0, 0)),
        const3((N_HEADS, SUBLANES, SUBLANES)),
        const3((N_HEADS, SUBLANES, HEAD_D)),
        const3((N_HEADS, SUBLANES, HEAD_D)),
        const3((N_HEADS, 1, HEAD_D)),
        pl.BlockSpec((None, 1, D_RET), lambda i: (layer, 0, 0)),
    ]
    args = [q, k, v, gate, state_ret, dec_tile, tile_rows(q_decay), tile_rows(k_decay), cd, g_ret_gn]
    if prev_states is None:
        state_spec = pl.BlockSpec((depth,) + st_block, lambda i: (0, i, 0, 0, 0))
        aliases = {}
    else:
        state_spec = pl.BlockSpec((1,) + st_block, lambda i: (layer, i, 0, 0, 0))
        in_specs.append(pl.BlockSpec(memory_space=pl.ANY))
        args.append(prev_states)
        aliases = {len(args) - 1: 1}
    return pl.pallas_call(
        functools.partial(_ret_sample_kernel, ts=ts),
        grid=(tl.bs // SAMPLE_GROUP,),
        in_specs=in_specs,
        out_specs=[pl.BlockSpec((rows, D_RET), lambda i: (i, 0)), state_spec],
        out_shape=[jax.ShapeDtypeStruct((tl.n_sample, D_RET), bf16),
                   jax.ShapeDtypeStruct((depth, tl.bs, N_HEADS, HEAD_D, HEAD_D), f32)],
        input_output_aliases=aliases,
        compiler_params=_cparams(1),
        name="retention_sample",
    )(*args)


def _ln_silu(cv, g, b):
    mu = jnp.mean(cv, axis=-1, keepdims=True)
    var = jnp.mean(jnp.square(cv - mu), axis=-1, keepdims=True)
    return jax.nn.silu((cv - mu) * lax.rsqrt(var + EPS) * g + b)


def _conv_taps(window, w_ref, b_ref, n_rows):
    cols = []
    for col in range(D_CONV // LANES):
        lanes = slice(col * LANES, (col + 1) * LANES)
        acc = jnp.broadcast_to(b_ref[:, lanes], (n_rows, LANES))
        for tap in range(CONV_WIDTH):
            acc = acc + window(col, tap) * w_ref[tap:tap + 1, lanes]
        cols.append(acc)
    return jnp.concatenate(cols, axis=-1)


def _conv_sample_kernel(a_ref, st_ref, w_ref, b_ref, g_ref, bl_ref, o_ref, buf_ref, full_ref, cv_ref, *, ts):
    for s in range(SAMPLE_GROUP):
        for col in range(D_CONV // LANES):
            lanes = slice(col * LANES, (col + 1) * LANES)
            full_ref[col, 0:CONV_HALO, :] = st_ref[s, :, lanes]
            full_ref[col, CONV_HALO:CONV_HALO + ts, :] = a_ref[s * ts:(s + 1) * ts, lanes]
            buf_ref[s, :, lanes] = full_ref[col, ts:ts + CONV_HALO, :]
        cv_ref[s * ts:(s + 1) * ts, :] = _conv_taps(lambda col, tap: full_ref[col, tap:tap + ts, :],
                                                    w_ref, b_ref, ts)
    o_ref[...] = _ln_silu(cv_ref[...], g_ref[...], bl_ref[...]).astype(bf16)


def _conv_sample(tl, layer, a, state_conv, w_conv, b_conv, g_ln, b_ln):
    ts = tl.ts
    rows = SAMPLE_GROUP * ts
    vec = pl.BlockSpec((None, 1, D_CONV), lambda i: (layer, 0, 0))
    return pl.pallas_call(
        functools.partial(_conv_sample_kernel, ts=ts),
        grid=(tl.bs // SAMPLE_GROUP,),
        in_specs=[pl.BlockSpec((rows, D_CONV), lambda i: (i, 0)),
                  pl.BlockSpec((None, SAMPLE_GROUP, CONV_HALO, D_CONV), lambda i: (layer, i, 0, 0)),
                  pl.BlockSpec((None, CONV_WIDTH, D_CONV), lambda i: (layer, 0, 0)),
                  vec, vec, vec],
        out_specs=[pl.BlockSpec((rows, D_CONV), lambda i: (i, 0)),
                   pl.BlockSpec((SAMPLE_GROUP, CONV_HALO, D_CONV), lambda i: (i, 0, 0))],
        out_shape=[jax.ShapeDtypeStruct((tl.n_sample, D_CONV), bf16),
                   jax.ShapeDtypeStruct((tl.bs, CONV_HALO, D_CONV), f32)],
        scratch_shapes=[pltpu.VMEM((D_CONV // LANES, CONV_HALO + ts + SUBLANES, LANES), f32),
                        pltpu.VMEM((rows, D_CONV), f32)],
        compiler_params=_cparams(1),
        name="conv_sample",
    )(a, state_conv, w_conv, b_conv, g_ln, b_ln)


N_CONV_COLS = D_CONV // LANES
N_CONV_CHUNKS = TM // CONV_ROWS
N_PROJ_SLICES = N_CONV_COLS * N_CONV_CHUNKS
PROJ_SLICE = 4 * D_RET // N_PROJ_SLICES
assert D_RET % PROJ_SLICE == 0
PROJ_UNROLL = 4


def _inproj_conv_kernel(*refs, np_steps, tiles_per_seq, split_x):
    if split_x:
        xp_ref, xs_ref = refs[:2]
        refs = refs[2:]
    else:
        x_ref = refs[0]
        refs = refs[1:]
    (shs_ref, sht_ref, scs_ref, sct_ref, g_ref, wa_ref, wr_ref, cos_ref, sin_ref, wc_ref, bc_ref, gln_ref, bln_ref,
     q_ref, k_ref, v_ref, gate_ref, as_ref, cvp_ref, buf_ref,
     hb_ref, full_ref, cv_ref, pj_ref) = refs
    i = pl.program_id(0)
    is_s = i >= np_steps
    shift = HALO_PAD - CONV_HALO
    per_col = D_RET // PROJ_SLICE

    def group(g):
        return jnp.concatenate([pj_ref[g * per_col + j] for j in range(per_col)], axis=-1)

    def step(sample):
        for t in range(STEP_TILES):
            rs = slice(t * TM, (t + 1) * TM)
            if split_x:
                x = xs_ref[rs, :] if sample else xp_ref[rs, :]
            else:
                x = x_ref[rs, :]
            sh, sc = (sht_ref[rs, :], sct_ref[rs, :]) if sample else (shs_ref[...], scs_ref[...])
            hb = (_rms(x, g_ref[...]) * (1.0 + sc) + sh).astype(bf16)
            hb_ref[...] = hb
            a = (jnp.dot(hb, wa_ref[:, 0:D_CONV], preferred_element_type=f32)
                 * jax.nn.sigmoid(jnp.dot(hb, wa_ref[:, D_CONV:2 * D_CONV], preferred_element_type=f32)))
            if sample:
                as_ref[rs, :] = a
            else:
                tile_in_seq = (i * STEP_TILES + t) % tiles_per_seq

                @pl.when(tile_in_seq == 0)
                def _():
                    full_ref[:, 0:HALO_PAD, :] = jnp.zeros((N_CONV_COLS, HALO_PAD, LANES), f32)

                @pl.when(tile_in_seq > 0)
                def _():
                    full_ref[:, 0:HALO_PAD, :] = full_ref[:, TM:TM + HALO_PAD, :]

                for col in range(N_CONV_COLS):
                    full_ref[col, HALO_PAD:HALO_PAD + TM, :] = a[:, col * LANES:(col + 1) * LANES]

                @pl.when(tile_in_seq == tiles_per_seq - 1)
                def _():
                    buf_ref[...] = a[TM - CONV_HALO:TM, :]

            def body(idx, carry):
                if not sample:
                    col = idx // N_CONV_CHUNKS
                    r0 = pl.multiple_of((idx % N_CONV_CHUNKS) * CONV_ROWS, CONV_ROWS)
                    acc = jnp.broadcast_to(bc_ref[col], (CONV_ROWS, LANES))
                    for tap in range(CONV_WIDTH):
                        acc = acc + (full_ref[col, pl.ds(r0 + (tap + shift), CONV_ROWS), :]
                                     * wc_ref[col, tap:tap + 1, :])
                    cv_ref[col, pl.ds(r0, CONV_ROWS), :] = acc
                pj_ref[idx] = jnp.dot(hb_ref[...], wr_ref[idx], preferred_element_type=f32)
                return carry

            lax.fori_loop(0, N_PROJ_SLICES, body, 0, unroll=PROJ_UNROLL)

            cos = cos_ref[rs, :]
            sin = sin_ref[rs, :]

            def rope(th):
                return th * cos + pltpu.roll(th, HEAD_D // 2, 1) * sin

            qg, kg = group(0), group(1)
            for hd in range(N_HEADS):
                sl = slice(hd * HEAD_D, (hd + 1) * HEAD_D)
                q_ref[rs, sl] = rope(qg[:, sl]).astype(bf16)
                k_ref[rs, sl] = rope(kg[:, sl]) * (HEAD_D ** -0.5)
            v_ref[rs, :] = group(2).astype(bf16)
            gate_ref[rs, :] = group(3)
            if not sample:
                for r0 in range(0, TM, NORM_ROWS):
                    cv = jnp.concatenate([cv_ref[col, r0:r0 + NORM_ROWS, :] for col in range(N_CONV_COLS)], axis=-1)
                    cvp_ref[t * TM + r0:t * TM + r0 + NORM_ROWS, :] = _ln_silu(
                        cv, gln_ref[...], bln_ref[...]).astype(bf16)

    pl.when(is_s)(functools.partial(step, True))
    pl.when(jnp.logical_not(is_s))(functools.partial(step, False))


def _inproj_conv(tl, layer, x, mod_seq, mod_tok, g_norm, w_in_bf, cos_tab, sin_tab, w_conv, b_conv, g_ln, b_ln):
    split_x = isinstance(x, tuple)
    depth = w_in_bf.shape[0]
    tok_spec = pl.BlockSpec((TS, D_MODEL), lambda i: (i, 0))
    if split_x:
        x_args = list(x)
        x_specs = [pl.BlockSpec((TS, D_MODEL), lambda i: (tl.prompt_block(i), 0)),
                   pl.BlockSpec((TS, D_MODEL), lambda i: (tl.sample_block(i), 0))]
    else:
        x_args, x_specs = [x], [tok_spec]
    sh_seq, sh_tok = _mod_specs(tl, layer, 0)
    sc_seq, sc_tok = _mod_specs(tl, layer, 1)

    def table_block(i):
        return jnp.where(i < tl.np_steps, i % tl.steps_per_seq, tl.steps_per_seq + tl.sample_block(i))

    w_glu = w_in_bf[:, :, 4 * D_RET:]
    w_rest = w_in_bf[:, :, 0:4 * D_RET].reshape(depth, D_MODEL, N_PROJ_SLICES, PROJ_SLICE).transpose(0, 2, 1, 3)
    w_cols = w_conv.reshape(depth, CONV_WIDTH, N_CONV_COLS, LANES).transpose(0, 2, 1, 3)
    b_cols = b_conv.reshape(depth, N_CONV_COLS, 1, LANES)
    tab_spec = pl.BlockSpec((TS, HEAD_D), lambda i: (table_block(i), 0))
    vec = pl.BlockSpec((None, 1, D_CONV), lambda i: (layer, 0, 0))
    row_spec = pl.BlockSpec((TS, D_RET), lambda i: (i, 0))
    row_sd = lambda dt: jax.ShapeDtypeStruct((tl.n_tok, D_RET), dt)
    return pl.pallas_call(
        functools.partial(_inproj_conv_kernel, np_steps=tl.np_steps, tiles_per_seq=tl.tiles_per_seq, split_x=split_x),
        grid=(tl.n_steps,),
        in_specs=x_specs + [
            sh_seq, sh_tok, sc_seq, sc_tok,
            pl.BlockSpec((None, 1, D_MODEL), lambda i: (layer, 0, 0)),
            pl.BlockSpec((None, D_MODEL, 2 * D_CONV), lambda i: (layer, 0, 0)),
            pl.BlockSpec((None, N_PROJ_SLICES, D_MODEL, PROJ_SLICE), lambda i: (layer, 0, 0, 0)),
            tab_spec, tab_spec,
            pl.BlockSpec((None, N_CONV_COLS, CONV_WIDTH, LANES), lambda i: (layer, 0, 0, 0)),
            pl.BlockSpec((None, N_CONV_COLS, 1, LANES), lambda i: (layer, 0, 0, 0)),
            vec, vec,
        ],
        out_specs=[row_spec] * 4 + [
            pl.BlockSpec((TS, D_CONV), lambda i: (tl.sample_block(i), 0)),
            pl.BlockSpec((TS, D_CONV), lambda i: (tl.prompt_block(i), 0)),
            pl.BlockSpec((None, CONV_HALO, D_CONV), lambda i: (tl.seq_index(i), 0, 0)),
        ],
        out_shape=[row_sd(bf16), row_sd(f32), row_sd(bf16), row_sd(f32),
                   jax.ShapeDtypeStruct((tl.n_sample, D_CONV), f32),
                   jax.ShapeDtypeStruct((tl.n_prompt, D_CONV), bf16),
                   jax.ShapeDtypeStruct((tl.bp, CONV_HALO, D_CONV), f32)],
        scratch_shapes=[pltpu.VMEM((TM, D_MODEL), bf16),
                        pltpu.VMEM((N_CONV_COLS, HALO_PAD + TM, LANES), f32),
                        pltpu.VMEM((N_CONV_COLS, TM, LANES), f32),
                        pltpu.VMEM((N_PROJ_SLICES, TM, PROJ_SLICE), f32)],
        compiler_params=_cparams(1),
        name="inproj_conv",
    )(*x_args, mod_seq, mod_tok, mod_seq, mod_tok, g_norm, w_glu, w_rest, cos_tab, sin_tab, w_cols, b_cols, g_ln, b_ln)


SORT_ROWS = 2 * TM + N_EXPERTS * SUBLANES
XS_HALF = D_MODEL // 2
XS_W = XS_HALF + LANES
u32 = jnp.uint32


def _pack_bf16_pair(x):
    lo = lax.shift_right_logical(lax.bitcast_convert_type(x[:, 0:XS_HALF], u32), u32(16))
    hi = lax.bitcast_convert_type(x[:, XS_HALF:D_MODEL], u32) & u32(0xFFFF0000)
    return hi | lo


def _unpack_bf16_pair(words):
    lo = lax.bitcast_convert_type(lax.shift_left(words, u32(16)), f32).astype(bf16)
    hi = lax.bitcast_convert_type(words & u32(0xFFFF0000), f32).astype(bf16)
    return lo, hi


def _split3(x):
    a = x.astype(bf16)
    r = x - a.astype(f32)
    b = r.astype(bf16)
    c = (r - b.astype(f32)).astype(bf16)
    return a, b, c


def _first_of4(vals, m):
    return jnp.where(vals[0] == m, 0.0, jnp.where(vals[1] == m, 1.0, jnp.where(vals[2] == m, 2.0, 3.0)))


def _rows_to_tile(rows, n_rows):
    sub = lax.broadcasted_iota(i32, (n_rows, TM), 0)
    out = jnp.zeros((n_rows, TM), f32)
    for r, val in enumerate(rows):
        out = jnp.where(sub == r, val, out)
    return out


def _outproj_kernel(*refs, np_steps, split_x):
    if split_x:
        xp_ref, xs_ref = refs[:2]
        refs = refs[2:]
    else:
        x_ref = refs[0]
        refs = refs[1:]
    (retp_ref, rets_ref, cvp_ref, cvs_ref, gts_ref, gtt_ref, shs_ref, sht_ref, scs_ref, sct_ref,
     g_ref, wo_ref, wrh_ref, wrl_ref, br_ref,
     xo_ref, h2_ref, rows_ref, cols_ref, cnt_ref) = refs
    is_s = pl.program_id(0) >= np_steps

    def step(sample):
        for t in range(STEP_TILES):
            rs = slice(t * TM, (t + 1) * TM)
            if split_x:
                x = xs_ref[rs, :] if sample else xp_ref[rs, :]
            else:
                x = x_ref[rs, :]
            ret = rets_ref[rs, :] if sample else retp_ref[rs, :]
            cv = cvs_ref[rs, :] if sample else cvp_ref[rs, :]
            gt, sh, sc = ((gtt_ref[rs, :], sht_ref[rs, :], sct_ref[rs, :]) if sample
                          else (gts_ref[...], shs_ref[...], scs_ref[...]))
            mix_out = (jnp.dot(ret, wo_ref[0:D_RET, :], preferred_element_type=f32)
                       + jnp.dot(cv, wo_ref[D_RET:D_RET + D_CONV, :], preferred_element_type=f32))
            xn = x + gt * mix_out
            xo_ref[rs, :] = xn
            h2 = _rms(xn, g_ref[...]) * (1.0 + sc) + sh
            h_hi = h2.astype(bf16)
            h2_ref[rs, :] = h_hi
            rows, cols, cnt = _route_tile(h2, h_hi, wrh_ref, wrl_ref, br_ref)
            rows_ref[t] = rows
            cols_ref[rs, :] = cols
            cnt_ref[t] = cnt

    pl.when(is_s)(functools.partial(step, True))
    pl.when(jnp.logical_not(is_s))(functools.partial(step, False))


def _route_tile(h2, h_hi, wrh_ref, wrl_ref, br_ref):
    h_lo = (h2 - h_hi.astype(f32)).astype(bf16)
    logits = (jnp.dot(h_hi, wrh_ref[...], preferred_element_type=f32)
              + jnp.dot(h_hi, wrl_ref[...], preferred_element_type=f32)
              + jnp.dot(h_lo, wrh_ref[...], preferred_element_type=f32)) + br_ref[...]
    lt = logits.T
    row = [lt[e:e + 1, :] for e in range(N_EXPERTS)]
    top = functools.reduce(jnp.maximum, row)
    ex = [jnp.exp(r - top) for r in row]
    den = functools.reduce(jnp.add, ex)
    p = [v / den for v in ex]

    best = None
    for g in range(N_GROUPS):
        a = p[g * GROUP_SIZE:(g + 1) * GROUP_SIZE]
        m1 = functools.reduce(jnp.maximum, a)
        i1 = _first_of4(a, m1)
        b = [jnp.where(i1 == float(j), -1.0, a[j]) for j in range(GROUP_SIZE)]
        m2 = functools.reduce(jnp.maximum, b)
        i2 = _first_of4(b, m2)
        cand = (m1 + m2, m1, m2, i1 + float(g * GROUP_SIZE), i2 + float(g * GROUP_SIZE))
        if best is None:
            best = cand
        else:
            take = cand[0] > best[0]
            best = tuple(jnp.where(take, c, o) for c, o in zip(cand, best))
    _, m1, m2, e0, e1 = best
    denom = m1 + m2
    w0 = m1 / denom
    w1 = m2 / denom

    ex_id = lax.broadcasted_iota(i32, (N_EXPERTS, TM), 0).astype(f32)
    sel0 = ex_id == e0
    sel1 = ex_id == e1
    ind = jnp.where(sel0 | sel1, 1.0, 0.0)
    t_r = lax.broadcasted_iota(i32, (TM, TM), 0)
    t_c = lax.broadcasted_iota(i32, (TM, TM), 1)
    earlier = jnp.where(t_r < t_c, 1.0, 0.0).astype(bf16)
    prefix = jnp.dot(ind.astype(bf16), earlier, preferred_element_type=f32)
    cnt = jnp.sum(ind, axis=-1, keepdims=True)
    cnt8 = jnp.floor((cnt + float(SUBLANES - 1)) * (1.0 / SUBLANES)) * float(SUBLANES)
    e_r = lax.broadcasted_iota(i32, (N_EXPERTS, N_EXPERTS), 0)
    e_c = lax.broadcasted_iota(i32, (N_EXPERTS, N_EXPERTS), 1)
    below = jnp.where(e_c < e_r, 1.0, 0.0).astype(bf16)
    seg_off = jnp.dot(below, jnp.broadcast_to(cnt8, (N_EXPERTS, TM)).astype(bf16),
                      preferred_element_type=f32)
    where_to = seg_off + prefix
    pos0 = jnp.sum(jnp.where(sel0, where_to, 0.0), axis=0, keepdims=True)
    pos1 = jnp.sum(jnp.where(sel1, where_to, 0.0), axis=0, keepdims=True)

    w0p = [v.astype(f32) for v in _split3(w0)]
    w1p = [v.astype(f32) for v in _split3(w1)]
    info = [pos0, pos1] + w0p + w1p
    return (_rows_to_tile(info, SUBLANES), _rows_to_tile(info, LANES).T,
            jnp.broadcast_to(cnt, (N_EXPERTS, LANES)))


def _outproj(tl, layer, x, ret_p, ret_s, cv_p, cv_s, mod_seq, mod_tok, g_norm, w_out_bf, wr_hi, wr_lo, br_pad):
    split_x = isinstance(x, tuple)
    tok_spec = pl.BlockSpec((TS, D_MODEL), lambda i: (i, 0))
    p_spec = lambda w: pl.BlockSpec((TS, w), lambda i: (tl.prompt_block(i), 0))
    s_spec = lambda w: pl.BlockSpec((TS, w), lambda i: (tl.sample_block(i), 0))
    if split_x:
        x_args, x_specs = list(x), [p_spec(D_MODEL), s_spec(D_MODEL)]
    else:
        x_args, x_specs = [x], [tok_spec]
    mods = []
    for col in (2, 3, 4):
        mods += list(_mod_specs(tl, layer, col))
    wr_spec = pl.BlockSpec((D_MODEL, LANES), lambda i: (0, 0))
    return pl.pallas_call(
        functools.partial(_outproj_kernel, np_steps=tl.np_steps, split_x=split_x),
        grid=(tl.n_steps,),
        in_specs=x_specs + [p_spec(D_RET), s_spec(D_RET), p_spec(D_CONV), s_spec(D_CONV)] + mods + [
            pl.BlockSpec((None, 1, D_MODEL), lambda i: (layer, 0, 0)),
            pl.BlockSpec((None, D_MODEL, D_MODEL), lambda i: (layer, 0, 0)),
            wr_spec, wr_spec,
            pl.BlockSpec((1, LANES), lambda i: (0, 0)),
        ],
        out_specs=[tok_spec, tok_spec,
                   pl.BlockSpec((STEP_TILES, SUBLANES, TM), lambda i: (i, 0, 0)),
                   pl.BlockSpec((TS, LANES), lambda i: (i, 0)),
                   pl.BlockSpec((STEP_TILES, N_EXPERTS, LANES), lambda i: (i, 0, 0))],
        out_shape=[jax.ShapeDtypeStruct((tl.n_tok, D_MODEL), f32),
                   jax.ShapeDtypeStruct((tl.n_tok, D_MODEL), bf16),
                   jax.ShapeDtypeStruct((tl.n_tiles, SUBLANES, TM), f32),
                   jax.ShapeDtypeStruct((tl.n_tok, LANES), f32),
                   jax.ShapeDtypeStruct((tl.n_tiles, N_EXPERTS, LANES), f32)],
        compiler_params=_cparams(1),
        name="outproj_router",
    )(*x_args, ret_p, ret_s, cv_p, cv_s, *([mod_seq, mod_tok] * 3), g_norm, w_out_bf, wr_hi, wr_lo, br_pad)


N_CHUNKS = SORT_ROWS // SUBLANES


class _Layout:
    def __init__(self, n_tiles):
        self.n_tiles = n_tiles
        self.tail_start = n_tiles * N_CHUNKS
        self.tail_n8 = self.tail_start + N_EXPERTS
        worst = 2 * n_tiles * TM + n_tiles * N_EXPERTS * (SUBLANES - 1) + N_EXPERTS * (BM - SUBLANES)
        self.n_blocks = -(-worst // BM)
        self.cap = self.n_blocks * BM
        self.dump = self.cap
        self.xs_rows = self.cap + -(-2 * STEP_TILES * SORT_ROWS // BM) * BM


def _moe_tables(lay, tile_counts):
    c8 = ((tile_counts.astype(i32) + SUBLANES - 1) // SUBLANES) * SUBLANES
    base8 = jnp.cumsum(c8, axis=0) - c8
    tot8 = jnp.sum(c8, axis=0)
    region = ((tot8 + BM - 1) // BM) * BM
    g_end = jnp.cumsum(region)
    g_start = g_end - region
    seg_end = jnp.cumsum(c8, axis=1)
    seg_dst = g_start[None, :] + base8
    n_used = g_end[-1] // BM
    blk = jnp.arange(lay.n_blocks, dtype=i32)
    block_e = jnp.minimum(jnp.sum((g_end[None, :] <= blk[:, None] * BM).astype(i32), axis=1), N_EXPERTS - 1)
    block_e = jnp.where(blk < n_used, block_e, block_e[n_used - 1])
    row0 = jnp.arange(N_CHUNKS, dtype=i32) * SUBLANES
    owner = jnp.sum((seg_end[:, None, :] <= row0[None, :, None]).astype(i32), axis=-1)
    onehot = (owner[:, :, None] == jnp.arange(N_EXPERTS, dtype=i32)[None, None, :]).astype(i32)
    delta = seg_dst - (seg_end - c8)
    chunk_dst = jnp.where(owner < N_EXPERTS, row0[None, :] + jnp.sum(onehot * delta[:, None, :], axis=-1), -1)
    tab = jnp.concatenate([chunk_dst.ravel(), g_start + tot8, (region - tot8) // SUBLANES]).astype(i32)
    ids = jnp.arange(N_EXPERTS, dtype=i32)
    later = jnp.where((ids[None, :] > ids[:, None]) & (region[None, :] > 0), ids[None, :], N_EXPERTS)
    next_e = jnp.min(later, axis=1)
    next_e = jnp.where(next_e == N_EXPERTS, -1, next_e)
    return tab, jnp.concatenate([block_e, n_used[None], next_e]).astype(i32)


def _for_chunks(n, fn):
    def body(c, carry):
        fn(c)
        return carry

    lax.fori_loop(0, n, body, 0)


def _dispatch_kernel(tab_ref, h2_ref, rows_ref, cols_ref, xs_hbm, sorted_ref, zero_ref, sem, *, lay, n_steps):
    i = pl.program_id(0)
    slot = i % 2
    step_rows = STEP_TILES * SORT_ROWS

    def tail_copy(dst):
        return pltpu.make_async_copy(zero_ref, xs_hbm.at[pl.ds(dst, SUBLANES)], sem.at[2])

    def wait_step(slot_):
        pltpu.make_async_copy(sorted_ref.at[slot_], xs_hbm.at[pl.ds(0, step_rows)], sem.at[slot_]).wait()

    @pl.when(i == 0)
    def _():
        zero_ref[...] = jnp.zeros_like(zero_ref)
        for e in range(N_EXPERTS):
            start = tab_ref[lay.tail_start + e]
            _for_chunks(tab_ref[lay.tail_n8 + e],
                        lambda c: tail_copy(pl.multiple_of(start + c * SUBLANES, SUBLANES)).start())

    @pl.when(i >= 2)
    def _():
        wait_step(slot)

    r_id = lax.broadcasted_iota(i32, (SORT_ROWS, TM), 0).astype(f32)
    lane = lax.broadcasted_iota(i32, (TM, LANES), 1)
    for t in range(STEP_TILES):
        pos0 = rows_ref[t, 0:1, :]
        pos1 = rows_ref[t, 1:2, :]
        p0 = r_id == pos0
        p1 = r_id == pos1
        perm = jnp.where(p0 | p1, 1.0, 0.0).astype(bf16)
        cols = cols_ref[t * TM:(t + 1) * TM, :]
        wpart0 = jnp.where((lane >= 2) & (lane < 5), cols, 0.0).astype(bf16)
        wpart1 = jnp.where((lane >= 5) & (lane < 8), cols, 0.0).astype(bf16)
        sw = (jnp.dot(jnp.where(p0, 1.0, 0.0).astype(bf16), wpart0, preferred_element_type=f32)
              + jnp.dot(jnp.where(p1, 1.0, 0.0).astype(bf16), wpart1, preferred_element_type=f32))
        base = t * SORT_ROWS
        sorted_ref[slot, base:base + SORT_ROWS, 0:XS_HALF] = _pack_bf16_pair(jnp.dot(
            perm, h2_ref[t * TM:(t + 1) * TM, :], preferred_element_type=f32))
        sorted_ref[slot, base:base + SORT_ROWS, XS_HALF:XS_W] = lax.bitcast_convert_type(jnp.broadcast_to(
            jnp.sum(sw, axis=-1, keepdims=True), (SORT_ROWS, LANES)), u32)

    for t in range(STEP_TILES):
        for c in range(N_CHUNKS):
            row = t * SORT_ROWS + c * SUBLANES
            dst = tab_ref[(i * STEP_TILES + t) * N_CHUNKS + c]
            dst = jnp.where(dst < 0, lay.dump + slot * step_rows + row, dst)
            pltpu.make_async_copy(sorted_ref.at[slot, pl.ds(row, SUBLANES)],
                                  xs_hbm.at[pl.ds(pl.multiple_of(dst, SUBLANES), SUBLANES)], sem.at[slot]).start()

    @pl.when(i == n_steps - 1)
    def _():
        if n_steps >= 2:
            wait_step(1 - slot)
        wait_step(slot)
        for e in range(N_EXPERTS):
            _for_chunks(tab_ref[lay.tail_n8 + e], lambda c: tail_copy(0).wait())


def _dispatch(tl, lay, tab, h2, rows, cols):
    grid_spec = pltpu.PrefetchScalarGridSpec(
        num_scalar_prefetch=1,
        grid=(tl.n_steps,),
        in_specs=[pl.BlockSpec((TS, D_MODEL), lambda i, t: (i, 0)),
                  pl.BlockSpec((STEP_TILES, SUBLANES, TM), lambda i, t: (i, 0, 0)),
                  pl.BlockSpec((TS, LANES), lambda i, t: (i, 0))],
        out_specs=pl.BlockSpec(memory_space=pl.ANY),
        scratch_shapes=[pltpu.VMEM((2, STEP_TILES * SORT_ROWS, XS_W), u32), pltpu.VMEM((SUBLANES, XS_W), u32),
                        pltpu.SemaphoreType.DMA((3,))],
    )
    return pl.pallas_call(
        functools.partial(_dispatch_kernel, lay=lay, n_steps=tl.n_steps),
        grid_spec=grid_spec,
        out_shape=jax.ShapeDtypeStruct((lay.xs_rows, XS_W), u32),
        compiler_params=_cparams(1),
        name="moe_dispatch",
    )(tab, h2, rows, cols)


def _expert_kernel(be_ref, xs_ref, wg_hbm, wu_hbm, wd_hbm, ys_ref, stage, w_bf, sem, *, n_blocks, layer):
    j = pl.program_id(0)

    def fetch(e):
        return [pltpu.make_async_copy(w.at[layer, e], stage.at[k], sem.at[k])
                for k, w in enumerate((wg_hbm, wu_hbm, wd_hbm))]

    @pl.when(j < be_ref[n_blocks])
    def _():
        e = be_ref[j]

        @pl.when(j == 0)
        def _():
            for copy in fetch(e):
                copy.start()

        @pl.when((j == 0) | (e != be_ref[jnp.maximum(j - 1, 0)]))
        def _():
            for copy in fetch(e):
                copy.wait()
            for k in range(3):
                w_bf[k] = stage[k].astype(bf16)
            nxt = be_ref[n_blocks + 1 + e]

            @pl.when(nxt >= 0)
            def _():
                for copy in fetch(nxt):
                    copy.start()

        x_lo, x_hi = _unpack_bf16_pair(xs_ref[:, 0:XS_HALF])

        def first_layer(k):
            return (jnp.dot(x_lo, w_bf[k, 0:XS_HALF, :], preferred_element_type=f32)
                    + jnp.dot(x_hi, w_bf[k, XS_HALF:D_MODEL, :], preferred_element_type=f32))

        mid = (jax.nn.silu(first_layer(0)) * first_layer(1)).astype(bf16)
        slot_w = lax.bitcast_convert_type(xs_ref[:, XS_HALF:XS_HALF + 1], f32)
        ys_ref[...] = jnp.dot(mid, w_bf[2], preferred_element_type=f32) * slot_w


def _experts(layer, lay, block_e, xs, w_gate, w_up, w_down):
    n_blocks = lay.n_blocks
    d_ff = w_gate.shape[-1]
    assert d_ff == D_MODEL
    used = lambda j, be: jnp.minimum(j, be[n_blocks] - 1)
    any_spec = pl.BlockSpec(memory_space=pl.ANY)
    grid_spec = pltpu.PrefetchScalarGridSpec(
        num_scalar_prefetch=1,
        grid=(n_blocks,),
        in_specs=[pl.BlockSpec((BM, XS_W), lambda j, be: (used(j, be), 0)), any_spec, any_spec, any_spec],
        out_specs=pl.BlockSpec((BM, D_MODEL), lambda j, be: (used(j, be), 0)),
        scratch_shapes=[pltpu.VMEM((3, D_MODEL, d_ff), f32), pltpu.VMEM((3, D_MODEL, d_ff), bf16),
                        pltpu.SemaphoreType.DMA((3,))],
    )
    return pl.pallas_call(
        functools.partial(_expert_kernel, n_blocks=n_blocks, layer=layer),
        grid_spec=grid_spec,
        out_shape=jax.ShapeDtypeStruct((lay.cap, D_MODEL), f32),
        compiler_params=_cparams(1),
        name="moe_experts",
    )(block_e, xs, w_gate, w_up, w_down)


def _combine_kernel(tab_ref, ys_hbm, cols_ref, x_ref, gts_ref, gtt_ref, *rest, n_steps, np_steps, final):
    if final:
        gf_ref, yp_ref, ysm_ref, staged, sem = rest
    else:
        xo_ref, staged, sem = rest
    i = pl.program_id(0)
    slot = i % 2
    is_s = i >= np_steps

    def start_step(step, slot_):
        for c in range(STEP_TILES * N_CHUNKS):
            src = jnp.maximum(tab_ref[step * (STEP_TILES * N_CHUNKS) + c], 0)
            pltpu.make_async_copy(ys_hbm.at[pl.ds(pl.multiple_of(src, SUBLANES), SUBLANES)],
                                  staged.at[slot_, pl.ds(c * SUBLANES, SUBLANES)], sem.at[slot_]).start()

    @pl.when(i == 0)
    def _():
        start_step(0, 0)

    @pl.when(i + 1 < n_steps)
    def _():
        start_step(i + 1, 1 - slot)

    pltpu.make_async_copy(ys_hbm.at[pl.ds(0, STEP_TILES * SORT_ROWS)], staged.at[slot], sem.at[slot]).wait()

    def step(sample):
        lane = lax.broadcasted_iota(i32, (TM, SORT_ROWS), 1).astype(f32)
        for t in range(STEP_TILES):
            rs = slice(t * TM, (t + 1) * TM)
            unperm = jnp.where((lane == cols_ref[rs, 0:1]) | (lane == cols_ref[rs, 1:2]), 1.0, 0.0).astype(bf16)
            parts = _split3(staged[slot, t * SORT_ROWS:(t + 1) * SORT_ROWS, :])
            ff = sum(jnp.dot(unperm, part, preferred_element_type=f32) for part in parts)
            xn = x_ref[rs, :] + (gtt_ref[rs, :] if sample else gts_ref[...]) * ff
            if final:
                (ysm_ref if sample else yp_ref)[rs, :] = _rms(xn, gf_ref[...])
            else:
                xo_ref[rs, :] = xn

    pl.when(is_s)(functools.partial(step, True))
    pl.when(jnp.logical_not(is_s))(functools.partial(step, False))


def _combine(tl, lay, layer, tab, ys, cols, x, mod_seq, mod_tok, g_final):
    final = g_final is not None
    tok_spec = pl.BlockSpec((TS, D_MODEL), lambda i, t: (i, 0))
    gt_seq, gt_tok = _mod_specs(tl, layer, 5)
    in_specs = [pl.BlockSpec(memory_space=pl.ANY), pl.BlockSpec((TS, LANES), lambda i, t: (i, 0)),
                tok_spec, gt_seq, gt_tok]
    args = [tab, ys, cols, x, mod_seq, mod_tok]
    if final:
        in_specs.append(pl.BlockSpec((1, D_MODEL), lambda i, t: (0, 0)))
        args.append(g_final)
        out_specs = [pl.BlockSpec((TS, D_MODEL), lambda i, t: (tl.prompt_block(i), 0)),
                     pl.BlockSpec((TS, D_MODEL), lambda i, t: (tl.sample_block(i), 0))]
        out_shape = [jax.ShapeDtypeStruct((tl.n_prompt, D_MODEL), f32),
                     jax.ShapeDtypeStruct((tl.n_sample, D_MODEL), f32)]
    else:
        out_specs = tok_spec
        out_shape = jax.ShapeDtypeStruct((tl.n_tok, D_MODEL), f32)
    grid_spec = pltpu.PrefetchScalarGridSpec(
        num_scalar_prefetch=1,
        grid=(tl.n_steps,),
        in_specs=in_specs,
        out_specs=out_specs,
        scratch_shapes=[pltpu.VMEM((2, STEP_TILES * SORT_ROWS, D_MODEL), f32), pltpu.SemaphoreType.DMA((2,))],
    )
    return pl.pallas_call(
        functools.partial(_combine_kernel, n_steps=tl.n_steps, np_steps=tl.np_steps, final=final),
        grid_spec=grid_spec,
        out_shape=out_shape,
        compiler_params=_cparams(1),
        name="moe_combine",
    )(*args)


def _rope_tables(tl):
    half = HEAD_D // 2
    inv = ROPE_BASE ** (-jnp.arange(half, dtype=f32) / half)
    pos_p = jnp.arange(tl.tp, dtype=i32)
    pos_s = PAST_LEN + jnp.arange(tl.ts, dtype=i32)
    pos = jnp.concatenate([pos_p, jnp.tile(pos_s, tl.bs)])
    ang = pos.astype(f32)[:, None] * inv[None, :]
    cos, sin = jnp.cos(ang), jnp.sin(ang)
    return jnp.concatenate([cos, cos], axis=-1), jnp.concatenate([-sin, sin], axis=-1)


def kernel(x_prompt, x_sample, state_ret, state_conv, c_prompt, c_sample, w_mod, b_mod, g_mix_norm, w_in,
           w_conv, b_conv, g_conv_ln, b_conv_ln, g_ret_gn, w_out, g_ffn_norm, w_router, b_router,
           w_exp_gate, w_exp_up, w_exp_down, g_final):
    bp, tp, _ = x_prompt.shape
    bs, ts, _ = x_sample.shape
    depth = w_mod.shape[0]
    tl = _Tiles(bp, tp, bs, ts)
    lay = _Layout(tl.n_tiles)

    c_all = jnp.concatenate([c_prompt, jnp.repeat(c_sample, ts, axis=0)], axis=0)
    mod_seq, mod_tok = _modulation(c_all, bp, w_mod, b_mod)
    mod_seq = mod_seq.reshape(depth, bp, 1, N_MOD * D_MODEL)

    cos_tab, sin_tab = _rope_tables(tl)
    w_in_bf = w_in.astype(bf16)
    w_out_bf = w_out.astype(bf16)
    wr_pad = jnp.pad(w_router.astype(f32), ((0, 0), (0, LANES - N_EXPERTS)))
    wr_hi = wr_pad.astype(bf16)
    wr_lo = (wr_pad - wr_hi.astype(f32)).astype(bf16)
    br_pad = jnp.pad(b_router.astype(f32), (0, LANES - N_EXPERTS)).reshape(1, LANES)
    vec3 = lambda t: t.reshape(depth, 1, t.shape[-1])
    g_mix3, g_ffn3, gn3 = vec3(g_mix_norm), vec3(g_ffn_norm), vec3(g_ret_gn)
    b_conv3, g_ln3, b_ln3 = vec3(b_conv), vec3(g_conv_ln), vec3(b_conv_ln)

    x = (x_prompt.reshape(tl.n_prompt, D_MODEL), x_sample.reshape(tl.n_sample, D_MODEL))
    ret_p, conv_p, conv_s = [], [], []
    ret_s_all = None
    for layer in range(depth):
        q, k, v, gate, a_s, co_p, buf_p = _inproj_conv(tl, layer, x, mod_seq, mod_tok, g_mix3, w_in_bf, cos_tab,
                                                        sin_tab, w_conv, b_conv3, g_ln3, b_ln3)
        ro_p, s_p = _retention_prompt(tl, layer, q, k, v, gate, gn3)
        ro_s, ret_s_all = _retention_sample(tl, layer, q, k, v, gate, state_ret, gn3, ret_s_all)
        co_s, buf_s = _conv_sample(tl, layer, a_s, state_conv, w_conv, b_conv3, g_ln3, b_ln3)
        x_mid, h2, rows, cols, tile_counts = _outproj(
            tl, layer, x, ro_p, ro_s, co_p, co_s, mod_seq, mod_tok, g_ffn3, w_out_bf, wr_hi, wr_lo, br_pad)
        tab, block_e = _moe_tables(lay, tile_counts[:, :, 0])
        xs = _dispatch(tl, lay, tab, h2, rows, cols)
        ys = _experts(layer, lay, block_e, xs, w_exp_gate, w_exp_up, w_exp_down)
        last = layer == depth - 1
        x = _combine(tl, lay, layer, tab, ys, cols, x_mid, mod_seq, mod_tok,
                     g_final.reshape(1, D_MODEL) if last else None)
        ret_p.append(s_p)
        conv_p.append(buf_p)
        conv_s.append(buf_s)
    y_p, y_s = x
    return (y_p.reshape(bp, tp, D_MODEL), y_s.reshape(bs, ts, D_MODEL),
            jnp.stack(ret_p), jnp.stack(conv_p), ret_s_all, jnp.stack(conv_s))
```

```python
import functools

import jax
import jax.numpy as jnp
from jax import lax
from jax.experimental import pallas as pl
from jax.experimental.pallas import tpu as pltpu

f32 = jnp.float32
bf16 = jnp.bfloat16
i32 = jnp.int32

D_MODEL = 1024
D_RET = 512
D_CONV = 512
N_HEADS = 4
HEAD_D = 128
RET_CHUNK = 128
RET_CHUNKS_PER_STEP = 4
ROPE_BASE = 10000.0
CONV_WIDTH = 31
CONV_HALO = CONV_WIDTH - 1
N_EXPERTS = 16
N_GROUPS = 4
GROUP_SIZE = N_EXPERTS // N_GROUPS
N_MOD = 6
EPS = 1e-6
PAST_LEN = 16384
D_IN = 4 * D_RET + 2 * D_CONV

LANES = 128
SUBLANES = 8
TM = 256
STEP_TILES = 2
TS = TM * STEP_TILES
BM = 512
CONV_ROWS = 128
NORM_ROWS = 64
SAMPLE_GROUP = 8
HALO_PAD = 32
VMEM_LIMIT = 56 * 1024 * 1024


def _cparams(n_axes, vmem=VMEM_LIMIT):
    return pltpu.CompilerParams(dimension_semantics=("arbitrary",) * n_axes, vmem_limit_bytes=vmem)


def _mod_kernel(c_ref, w_ref, b_ref, seq_ref, tok_ref):
    cond = jax.nn.silu(c_ref[...]).astype(bf16)
    mod = jnp.dot(cond, w_ref[...].astype(bf16), preferred_element_type=f32) + b_ref[...]
    n_seq = seq_ref.shape[0]
    seq_ref[...] = mod[0:n_seq, :]
    tok_ref[...] = mod[n_seq:, :]


def _modulation(c_all, n_seq, w_mod, b_mod):
    depth = w_mod.shape[0]
    m = c_all.shape[0]
    assert n_seq % SUBLANES == 0
    return pl.pallas_call(
        _mod_kernel,
        grid=(depth, N_MOD),
        in_specs=[
            pl.BlockSpec((m, D_MODEL), lambda l, j: (0, 0)),
            pl.BlockSpec((None, D_MODEL, D_MODEL), lambda l, j: (l, 0, j)),
            pl.BlockSpec((None, 1, D_MODEL), lambda l, j: (l, 0, j)),
        ],
        out_specs=[pl.BlockSpec((None, n_seq, D_MODEL), lambda l, j: (l, 0, j)),
                   pl.BlockSpec((None, m - n_seq, D_MODEL), lambda l, j: (l, 0, j))],
        out_shape=[jax.ShapeDtypeStruct((depth, n_seq, N_MOD * D_MODEL), f32),
                   jax.ShapeDtypeStruct((depth, m - n_seq, N_MOD * D_MODEL), f32)],
        compiler_params=_cparams(2),
        name="modulation",
    )(c_all, w_mod, b_mod.reshape(depth, 1, N_MOD * D_MODEL))


class _Tiles:
    def __init__(self, bp, tp, bs, ts):
        self.bp, self.tp, self.bs, self.ts = bp, tp, bs, ts
        self.n_prompt = bp * tp
        self.n_sample = bs * ts
        self.n_tok = self.n_prompt + self.n_sample
        assert tp % TS == 0 and self.n_sample % TS == 0
        self.tiles_per_seq = tp // TM
        self.np_tiles = self.n_prompt // TM
        self.n_tiles = self.n_tok // TM
        self.steps_per_seq = tp // TS
        self.np_steps = self.n_prompt // TS
        self.n_steps = self.n_tok // TS

    def prompt_block(self, i):
        return jnp.minimum(i, self.np_steps - 1)

    def sample_block(self, i):
        return jnp.maximum(i - self.np_steps, 0)

    def seq_index(self, i):
        return jnp.minimum(i // self.steps_per_seq, self.bp - 1)


def _mod_specs(tl, layer, col):
    seq = pl.BlockSpec((None, None, 1, D_MODEL), lambda i, *_: (layer, tl.seq_index(i), 0, col))
    tok = pl.BlockSpec((None, TS, D_MODEL), lambda i, *_: (layer, tl.sample_block(i), col))
    return seq, tok


def _rms(x, g):
    return x * lax.rsqrt(jnp.mean(x * x, axis=-1, keepdims=True) + EPS) * g


def _head_norm_gate(o, gn, gate):
    mu = jnp.mean(o, axis=-1, keepdims=True)
    var = jnp.mean(jnp.square(o - mu), axis=-1, keepdims=True)
    return jax.nn.silu(gate) * ((o - mu) * lax.rsqrt(var + EPS) * gn)


def _dot_nt(a, b):
    return lax.dot_general(a, b, (((1,), (1,)), ((), ())), preferred_element_type=f32)


def _dot_tn(a, b):
    return lax.dot_general(a, b, (((0,), (0,)), ((), ())), preferred_element_type=f32)


def _ret_prompt_kernel(q_ref, k_ref, v_ref, gate_ref, dec_ref, qd_ref, kd_ref, cd_ref, gn_ref,
                       o_ref, s_out_ref, s_ref):
    c = pl.program_id(1)

    @pl.when(c == 0)
    def _():
        s_ref[...] = jnp.zeros_like(s_ref)

    for ci in range(RET_CHUNKS_PER_STEP):
        rows = slice(ci * RET_CHUNK, (ci + 1) * RET_CHUNK)
        for hd in range(N_HEADS):
            sl = slice(hd * HEAD_D, (hd + 1) * HEAD_D)
            kh = k_ref[rows, sl]
            qb = q_ref[rows, sl]
            kb = kh.astype(bf16)
            vb = v_ref[rows, sl]
            s_old = s_ref[hd]
            scores = _dot_nt(qb, kb) * dec_ref[hd]
            inner = jnp.dot(scores.astype(bf16), vb, preferred_element_type=f32)
            cross = jnp.dot(qb, s_old.astype(bf16), preferred_element_type=f32) * qd_ref[hd]
            s_ref[hd] = s_old * cd_ref[hd] + _dot_tn((kh * kd_ref[hd]).astype(bf16), vb)
            o_ref[rows, sl] = _head_norm_gate(inner + cross, gn_ref[:, sl], gate_ref[rows, sl]).astype(bf16)

    @pl.when(c == pl.num_programs(1) - 1)
    def _():
        s_out_ref[...] = s_ref[...]


def _decay_tables(chunk, true_len):
    lg = jnp.log(1.0 - 2.0 ** (-5.0 - jnp.arange(N_HEADS, dtype=f32)))
    idx = jnp.arange(chunk, dtype=f32)
    rel = idx[:, None] - idx[None, :]
    decay = jnp.where(rel[None] >= 0, jnp.exp(jnp.maximum(rel, 0.0)[None] * lg[:, None, None]), 0.0)
    q_decay = jnp.exp((idx[None, :] + 1.0) * lg[:, None])
    k_decay = jnp.exp((true_len - 1.0 - idx[None, :]) * lg[:, None])
    c_decay = jnp.exp(true_len * lg)
    return decay, q_decay, k_decay, c_decay


def _retention_prompt(tl, layer, q, k, v, gate, g_ret_gn):
    step_rows = RET_CHUNK * RET_CHUNKS_PER_STEP
    assert tl.tp % step_rows == 0
    n_chunks = tl.tp // step_rows
    decay, q_decay, k_decay, c_decay = _decay_tables(RET_CHUNK, RET_CHUNK)
    bcast = lambda t: jnp.broadcast_to(t[:, :, None], (N_HEADS, RET_CHUNK, HEAD_D))
    cd = jnp.broadcast_to(c_decay[:, None, None], (N_HEADS, 1, HEAD_D))
    tok_spec = pl.BlockSpec((step_rows, D_RET), lambda b, c: (b * n_chunks + c, 0))
    tab_spec = pl.BlockSpec((N_HEADS, RET_CHUNK, HEAD_D), lambda b, c: (0, 0, 0))
    return pl.pallas_call(
        _ret_prompt_kernel,
        grid=(tl.bp, n_chunks),
        in_specs=[tok_spec] * 4 + [tab_spec] * 3 + [
            pl.BlockSpec((N_HEADS, 1, HEAD_D), lambda b, c: (0, 0, 0)),
            pl.BlockSpec((None, 1, D_RET), lambda b, c: (layer, 0, 0)),
        ],
        out_specs=[tok_spec, pl.BlockSpec((None, N_HEADS, HEAD_D, HEAD_D), lambda b, c: (b, 0, 0, 0))],
        out_shape=[jax.ShapeDtypeStruct((tl.n_prompt, D_RET), bf16),
                   jax.ShapeDtypeStruct((tl.bp, N_HEADS, HEAD_D, HEAD_D), f32)],
        scratch_shapes=[pltpu.VMEM((N_HEADS, HEAD_D, HEAD_D), f32)],
        compiler_params=_cparams(2),
        name="retention_prompt",
    )(q, k, v, gate, decay, bcast(q_decay), bcast(k_decay), cd, g_ret_gn)


def _ret_sample_kernel(q_ref, k_ref, v_ref, gate_ref, s_in_ref, dec_ref, qd_ref, kd_ref, cd_ref, gn_ref,
                       *rest, ts):
    o_ref, s_all_ref = rest[-2:]
    s_out_ref = s_all_ref.at[0]
    for other in range(1, s_all_ref.shape[0]):
        s_all_ref[other] = jnp.zeros(s_all_ref.shape[1:], f32)
    seqs_per_tile = SUBLANES // ts
    row = lax.broadcasted_iota(i32, (SUBLANES, HEAD_D), 0)
    q_all = q_ref[...].astype(f32)
    v_all = v_ref[...].astype(f32)
    outs = []
    for t in range(SAMPLE_GROUP // seqs_per_tile):
        rows = slice(t * SUBLANES, (t + 1) * SUBLANES)
        heads = []
        for hd in range(N_HEADS):
            sl = slice(hd * HEAD_D, (hd + 1) * HEAD_D)
            qh = q_all[rows, sl]
            kh = k_ref[rows, sl] * kd_ref[hd]
            vb = v_all[rows, sl].astype(bf16)
            qb = qh.astype(bf16)
            scores = _dot_nt(qb, k_ref[rows, sl].astype(bf16)) * dec_ref[hd]
            o = jnp.dot(scores.astype(bf16), vb, preferred_element_type=f32)
            for s in range(seqs_per_tile):
                b = t * seqs_per_tile + s
                mine = (row >= s * ts) & (row < (s + 1) * ts)
                s_old = s_in_ref[b, hd]
                q_s = jnp.where(mine, qh, 0.0).astype(bf16)
                k_s = jnp.where(mine, kh, 0.0).astype(bf16)
                o = o + jnp.dot(q_s, s_old.astype(bf16), preferred_element_type=f32) * qd_ref[hd]
                s_out_ref[b, hd] = s_old * cd_ref[hd] + _dot_tn(k_s, vb)
            heads.append(_head_norm_gate(o, gn_ref[:, sl], gate_ref[rows, sl]))
        outs.append(jnp.concatenate(heads, axis=-1))
    o_ref[...] = jnp.concatenate(outs, axis=0).astype(bf16)


def _retention_sample(tl, layer, q, k, v, gate, state_ret, g_ret_gn, prev_states):
    ts = tl.ts
    depth = state_ret.shape[0]
    assert SUBLANES % ts == 0 and tl.bs % SAMPLE_GROUP == 0
    seqs_per_tile = SUBLANES // ts
    decay, q_decay, k_decay, c_decay = _decay_tables(ts, ts)
    eye = jnp.eye(seqs_per_tile, dtype=f32)
    dec_tile = jnp.einsum("ab,hij->haibj", eye, decay).reshape(N_HEADS, SUBLANES, SUBLANES)
    tile_rows = lambda t: jnp.broadcast_to(jnp.tile(t, (1, seqs_per_tile))[:, :, None],
                                           (N_HEADS, SUBLANES, HEAD_D))
    cd = jnp.broadcast_to(c_decay[:, None, None], (N_HEADS, 1, HEAD_D))
    rows = SAMPLE_GROUP * ts
    first = tl.n_prompt // rows
    tok_spec = pl.BlockSpec((rows, D_RET), lambda i: (first + i, 0))
    const3 = lambda shape: pl.BlockSpec(shape, lambda i: (0, 0, 0))
    st_block = (SAMPLE_GROUP, N_HEADS, HEAD_D, HEAD_D)
    in_specs = [tok_spec] * 4 + [
        pl.BlockSpec((None,) + st_block, lambda i: (layer, i, 0, 0, 0)),
        const3((N_HEADS, SUBLANES, SUBLANES)),
        const3((N_HEADS, SUBLANES, HEAD_D)),
        const3((N_HEADS, SUBLANES, HEAD_D)),
        const3((N_HEADS, 1, HEAD_D)),
        pl.BlockSpec((None, 1, D_RET), lambda i: (layer, 0, 0)),
    ]
    args = [q, k, v, gate, state_ret, dec_tile, tile_rows(q_decay), tile_rows(k_decay), cd, g_ret_gn]
    if prev_states is None:
        state_spec = pl.BlockSpec((depth,) + st_block, lambda i: (0, i, 0, 0, 0))
        aliases = {}
    else:
        state_spec = pl.BlockSpec((1,) + st_block, lambda i: (layer, i, 0, 0, 0))
        in_specs.append(pl.BlockSpec(memory_space=pl.ANY))
        args.append(prev_states)
        aliases = {len(args) - 1: 1}
    return pl.pallas_call(
        functools.partial(_ret_sample_kernel, ts=ts),
        grid=(tl.bs // SAMPLE_GROUP,),
        in_specs=in_specs,
        out_specs=[pl.BlockSpec((rows, D_RET), lambda i: (i, 0)), state_spec],
        out_shape=[jax.ShapeDtypeStruct((tl.n_sample, D_RET), bf16),
                   jax.ShapeDtypeStruct((depth, tl.bs, N_HEADS, HEAD_D, HEAD_D), f32)],
        input_output_aliases=aliases,
        compiler_params=_cparams(1),
        name="retention_sample",
    )(*args)


def _ln_silu(cv, g, b):
    mu = jnp.mean(cv, axis=-1, keepdims=True)
    var = jnp.mean(jnp.square(cv - mu), axis=-1, keepdims=True)
    return jax.nn.silu((cv - mu) * lax.rsqrt(var + EPS) * g + b)


def _conv_taps(window, w_ref, b_ref, n_rows):
    cols = []
    for col in range(D_CONV // LANES):
        lanes = slice(col * LANES, (col + 1) * LANES)
        acc = jnp.broadcast_to(b_ref[:, lanes], (n_rows, LANES))
        for tap in range(CONV_WIDTH):
            acc = acc + window(col, tap) * w_ref[tap:tap + 1, lanes]
        cols.append(acc)
    return jnp.concatenate(cols, axis=-1)


def _conv_sample_kernel(a_ref, st_ref, w_ref, b_ref, g_ref, bl_ref, o_ref, buf_ref, full_ref, cv_ref, *, ts):
    for s in range(SAMPLE_GROUP):
        for col in range(D_CONV // LANES):
            lanes = slice(col * LANES, (col + 1) * LANES)
            full_ref[col, 0:CONV_HALO, :] = st_ref[s, :, lanes]
            full_ref[col, CONV_HALO:CONV_HALO + ts, :] = a_ref[s * ts:(s + 1) * ts, lanes]
            buf_ref[s, :, lanes] = full_ref[col, ts:ts + CONV_HALO, :]
        cv_ref[s * ts:(s + 1) * ts, :] = _conv_taps(lambda col, tap: full_ref[col, tap:tap + ts, :],
                                                    w_ref, b_ref, ts)
    o_ref[...] = _ln_silu(cv_ref[...], g_ref[...], bl_ref[...]).astype(bf16)


def _conv_sample(tl, layer, a, state_conv, w_conv, b_conv, g_ln, b_ln):
    ts = tl.ts
    rows = SAMPLE_GROUP * ts
    vec = pl.BlockSpec((None, 1, D_CONV), lambda i: (layer, 0, 0))
    return pl.pallas_call(
        functools.partial(_conv_sample_kernel, ts=ts),
        grid=(tl.bs // SAMPLE_GROUP,),
        in_specs=[pl.BlockSpec((rows, D_CONV), lambda i: (i, 0)),
                  pl.BlockSpec((None, SAMPLE_GROUP, CONV_HALO, D_CONV), lambda i: (layer, i, 0, 0)),
                  pl.BlockSpec((None, CONV_WIDTH, D_CONV), lambda i: (layer, 0, 0)),
                  vec, vec, vec],
        out_specs=[pl.BlockSpec((rows, D_CONV), lambda i: (i, 0)),
                   pl.BlockSpec((SAMPLE_GROUP, CONV_HALO, D_CONV), lambda i: (i, 0, 0))],
        out_shape=[jax.ShapeDtypeStruct((tl.n_sample, D_CONV), bf16),
                   jax.ShapeDtypeStruct((tl.bs, CONV_HALO, D_CONV), f32)],
        scratch_shapes=[pltpu.VMEM((D_CONV // LANES, CONV_HALO + ts + SUBLANES, LANES), f32),
                        pltpu.VMEM((rows, D_CONV), f32)],
        compiler_params=_cparams(1),
        name="conv_sample",
    )(a, state_conv, w_conv, b_conv, g_ln, b_ln)


N_CONV_COLS = D_CONV // LANES
N_CONV_CHUNKS = TM // CONV_ROWS


def _inproj_conv_kernel(*refs, np_steps, tiles_per_seq, split_x):
    if split_x:
        xp_ref, xs_ref = refs[:2]
        refs = refs[2:]
    else:
        x_ref = refs[0]
        refs = refs[1:]
    (shs_ref, sht_ref, scs_ref, sct_ref, g_ref, w_ref, cos_ref, sin_ref, wc_ref, bc_ref, gln_ref, bln_ref,
     q_ref, k_ref, v_ref, gate_ref, as_ref, cvp_ref, buf_ref, full_ref, cv_ref) = refs
    i = pl.program_id(0)
    is_s = i >= np_steps
    shift = HALO_PAD - CONV_HALO

    def step(sample):
        for t in range(STEP_TILES):
            rs = slice(t * TM, (t + 1) * TM)
            if split_x:
                x = xs_ref[rs, :] if sample else xp_ref[rs, :]
            else:
                x = x_ref[rs, :]
            sh, sc = (sht_ref[rs, :], sct_ref[rs, :]) if sample else (shs_ref[...], scs_ref[...])
            hb = (_rms(x, g_ref[...]) * (1.0 + sc) + sh).astype(bf16)

            def group(g):
                return jnp.dot(hb, w_ref[:, g * D_RET:(g + 1) * D_RET], preferred_element_type=f32)

            cos = cos_ref[rs, :]
            sin = sin_ref[rs, :]

            def rope(th):
                return th * cos + pltpu.roll(th, HEAD_D // 2, 1) * sin

            qg, kg = group(0), group(1)
            for hd in range(N_HEADS):
                sl = slice(hd * HEAD_D, (hd + 1) * HEAD_D)
                q_ref[rs, sl] = rope(qg[:, sl]).astype(bf16)
                k_ref[rs, sl] = rope(kg[:, sl]) * (HEAD_D ** -0.5)
            v_ref[rs, :] = group(2).astype(bf16)
            gate_ref[rs, :] = group(3)
            a = group(4) * jax.nn.sigmoid(group(5))
            if sample:
                as_ref[rs, :] = a
            else:
                tile_in_seq = (i * STEP_TILES + t) % tiles_per_seq

                @pl.when(tile_in_seq == 0)
                def _():
                    full_ref[:, 0:HALO_PAD, :] = jnp.zeros((N_CONV_COLS, HALO_PAD, LANES), f32)

                @pl.when(tile_in_seq > 0)
                def _():
                    full_ref[:, 0:HALO_PAD, :] = full_ref[:, TM:TM + HALO_PAD, :]

                for col in range(N_CONV_COLS):
                    full_ref[col, HALO_PAD:HALO_PAD + TM, :] = a[:, col * LANES:(col + 1) * LANES]

                @pl.when(tile_in_seq == tiles_per_seq - 1)
                def _():
                    buf_ref[...] = a[TM - CONV_HALO:TM, :]

                def taps(idx, carry):
                    col = idx // N_CONV_CHUNKS
                    r0 = pl.multiple_of((idx % N_CONV_CHUNKS) * CONV_ROWS, CONV_ROWS)
                    acc = jnp.broadcast_to(bc_ref[col], (CONV_ROWS, LANES))
                    for tap in range(CONV_WIDTH):
                        acc = acc + (full_ref[col, pl.ds(r0 + (tap + shift), CONV_ROWS), :]
                                     * wc_ref[col, tap:tap + 1, :])
                    cv_ref[col, pl.ds(r0, CONV_ROWS), :] = acc
                    return carry

                lax.fori_loop(0, N_CONV_COLS * N_CONV_CHUNKS, taps, 0)
                for r0 in range(0, TM, NORM_ROWS):
                    cv = jnp.concatenate([cv_ref[col, r0:r0 + NORM_ROWS, :] for col in range(N_CONV_COLS)], axis=-1)
                    cvp_ref[t * TM + r0:t * TM + r0 + NORM_ROWS, :] = _ln_silu(
                        cv, gln_ref[...], bln_ref[...]).astype(bf16)

    pl.when(is_s)(functools.partial(step, True))
    pl.when(jnp.logical_not(is_s))(functools.partial(step, False))


def _inproj_conv(tl, layer, x, mod_seq, mod_tok, g_norm, w_in_bf, cos_tab, sin_tab, w_conv, b_conv, g_ln, b_ln):
    split_x = isinstance(x, tuple)
    depth = w_in_bf.shape[0]
    tok_spec = pl.BlockSpec((TS, D_MODEL), lambda i: (i, 0))
    if split_x:
        x_args = list(x)
        x_specs = [pl.BlockSpec((TS, D_MODEL), lambda i: (tl.prompt_block(i), 0)),
                   pl.BlockSpec((TS, D_MODEL), lambda i: (tl.sample_block(i), 0))]
    else:
        x_args, x_specs = [x], [tok_spec]
    sh_seq, sh_tok = _mod_specs(tl, layer, 0)
    sc_seq, sc_tok = _mod_specs(tl, layer, 1)

    def table_block(i):
        return jnp.where(i < tl.np_steps, i % tl.steps_per_seq, tl.steps_per_seq + tl.sample_block(i))

    w_cols = w_conv.reshape(depth, CONV_WIDTH, N_CONV_COLS, LANES).transpose(0, 2, 1, 3)
    b_cols = b_conv.reshape(depth, N_CONV_COLS, 1, LANES)
    tab_spec = pl.BlockSpec((TS, HEAD_D), lambda i: (table_block(i), 0))
    vec = pl.BlockSpec((None, 1, D_CONV), lambda i: (layer, 0, 0))
    row_spec = pl.BlockSpec((TS, D_RET), lambda i: (i, 0))
    row_sd = lambda dt: jax.ShapeDtypeStruct((tl.n_tok, D_RET), dt)
    return pl.pallas_call(
        functools.partial(_inproj_conv_kernel, np_steps=tl.np_steps, tiles_per_seq=tl.tiles_per_seq, split_x=split_x),
        grid=(tl.n_steps,),
        in_specs=x_specs + [
            sh_seq, sh_tok, sc_seq, sc_tok,
            pl.BlockSpec((None, 1, D_MODEL), lambda i: (layer, 0, 0)),
            pl.BlockSpec((None, D_MODEL, D_IN), lambda i: (layer, 0, 0)),
            tab_spec, tab_spec,
            pl.BlockSpec((None, N_CONV_COLS, CONV_WIDTH, LANES), lambda i: (layer, 0, 0, 0)),
            pl.BlockSpec((None, N_CONV_COLS, 1, LANES), lambda i: (layer, 0, 0, 0)),
            vec, vec,
        ],
        out_specs=[row_spec] * 4 + [
            pl.BlockSpec((TS, D_CONV), lambda i: (tl.sample_block(i), 0)),
            pl.BlockSpec((TS, D_CONV), lambda i: (tl.prompt_block(i), 0)),
            pl.BlockSpec((None, CONV_HALO, D_CONV), lambda i: (tl.seq_index(i), 0, 0)),
        ],
        out_shape=[row_sd(bf16), row_sd(f32), row_sd(bf16), row_sd(f32),
                   jax.ShapeDtypeStruct((tl.n_sample, D_CONV), f32),
                   jax.ShapeDtypeStruct((tl.n_prompt, D_CONV), bf16),
                   jax.ShapeDtypeStruct((tl.bp, CONV_HALO, D_CONV), f32)],
        scratch_shapes=[pltpu.VMEM((N_CONV_COLS, HALO_PAD + TM, LANES), f32),
                        pltpu.VMEM((N_CONV_COLS, TM, LANES), f32)],
        compiler_params=_cparams(1),
        name="inproj_conv",
    )(*x_args, mod_seq, mod_tok, mod_seq, mod_tok, g_norm, w_in_bf, cos_tab, sin_tab, w_cols, b_cols, g_ln, b_ln)


SORT_ROWS = 2 * TM + N_EXPERTS * SUBLANES
XS_HALF = D_MODEL // 2
XS_W = XS_HALF + LANES
u32 = jnp.uint32


def _pack_bf16_pair(x):
    lo = lax.shift_right_logical(lax.bitcast_convert_type(x[:, 0:XS_HALF], u32), u32(16))
    hi = lax.bitcast_convert_type(x[:, XS_HALF:D_MODEL], u32) & u32(0xFFFF0000)
    return hi | lo


def _unpack_bf16_pair(words):
    lo = lax.bitcast_convert_type(lax.shift_left(words, u32(16)), f32).astype(bf16)
    hi = lax.bitcast_convert_type(words & u32(0xFFFF0000), f32).astype(bf16)
    return lo, hi


def _split3(x):
    a = x.astype(bf16)
    r = x - a.astype(f32)
    b = r.astype(bf16)
    c = (r - b.astype(f32)).astype(bf16)
    return a, b, c


def _first_of4(vals, m):
    return jnp.where(vals[0] == m, 0.0, jnp.where(vals[1] == m, 1.0, jnp.where(vals[2] == m, 2.0, 3.0)))


def _rows_to_tile(rows, n_rows):
    sub = lax.broadcasted_iota(i32, (n_rows, TM), 0)
    out = jnp.zeros((n_rows, TM), f32)
    for r, val in enumerate(rows):
        out = jnp.where(sub == r, val, out)
    return out


def _outproj_kernel(*refs, np_steps, split_x):
    if split_x:
        xp_ref, xs_ref = refs[:2]
        refs = refs[2:]
    else:
        x_ref = refs[0]
        refs = refs[1:]
    (retp_ref, rets_ref, cvp_ref, cvs_ref, gts_ref, gtt_ref, shs_ref, sht_ref, scs_ref, sct_ref,
     g_ref, wo_ref, wrh_ref, wrl_ref, br_ref,
     xo_ref, h2_ref, rows_ref, cols_ref, cnt_ref) = refs
    is_s = pl.program_id(0) >= np_steps

    def step(sample):
        for t in range(STEP_TILES):
            rs = slice(t * TM, (t + 1) * TM)
            if split_x:
                x = xs_ref[rs, :] if sample else xp_ref[rs, :]
            else:
                x = x_ref[rs, :]
            ret = rets_ref[rs, :] if sample else retp_ref[rs, :]
            cv = cvs_ref[rs, :] if sample else cvp_ref[rs, :]
            gt, sh, sc = ((gtt_ref[rs, :], sht_ref[rs, :], sct_ref[rs, :]) if sample
                          else (gts_ref[...], shs_ref[...], scs_ref[...]))
            mix_out = (jnp.dot(ret, wo_ref[0:D_RET, :], preferred_element_type=f32)
                       + jnp.dot(cv, wo_ref[D_RET:D_RET + D_CONV, :], preferred_element_type=f32))
            xn = x + gt * mix_out
            xo_ref[rs, :] = xn
            h2 = _rms(xn, g_ref[...]) * (1.0 + sc) + sh
            h_hi = h2.astype(bf16)
            h2_ref[rs, :] = h_hi
            rows, cols, cnt = _route_tile(h2, h_hi, wrh_ref, wrl_ref, br_ref)
            rows_ref[t] = rows
            cols_ref[rs, :] = cols
            cnt_ref[t] = cnt

    pl.when(is_s)(functools.partial(step, True))
    pl.when(jnp.logical_not(is_s))(functools.partial(step, False))


def _route_tile(h2, h_hi, wrh_ref, wrl_ref, br_ref):
    h_lo = (h2 - h_hi.astype(f32)).astype(bf16)
    logits = (jnp.dot(h_hi, wrh_ref[...], preferred_element_type=f32)
              + jnp.dot(h_hi, wrl_ref[...], preferred_element_type=f32)
              + jnp.dot(h_lo, wrh_ref[...], preferred_element_type=f32)) + br_ref[...]
    lt = logits.T
    row = [lt[e:e + 1, :] for e in range(N_EXPERTS)]
    top = functools.reduce(jnp.maximum, row)
    ex = [jnp.exp(r - top) for r in row]
    den = functools.reduce(jnp.add, ex)
    p = [v / den for v in ex]

    best = None
    for g in range(N_GROUPS):
        a = p[g * GROUP_SIZE:(g + 1) * GROUP_SIZE]
        m1 = functools.reduce(jnp.maximum, a)
        i1 = _first_of4(a, m1)
        b = [jnp.where(i1 == float(j), -1.0, a[j]) for j in range(GROUP_SIZE)]
        m2 = functools.reduce(jnp.maximum, b)
        i2 = _first_of4(b, m2)
        cand = (m1 + m2, m1, m2, i1 + float(g * GROUP_SIZE), i2 + float(g * GROUP_SIZE))
        if best is None:
            best = cand
        else:
            take = cand[0] > best[0]
            best = tuple(jnp.where(take, c, o) for c, o in zip(cand, best))
    _, m1, m2, e0, e1 = best
    denom = m1 + m2
    w0 = m1 / denom
    w1 = m2 / denom

    ex_id = lax.broadcasted_iota(i32, (N_EXPERTS, TM), 0).astype(f32)
    sel0 = ex_id == e0
    sel1 = ex_id == e1
    ind = jnp.where(sel0 | sel1, 1.0, 0.0)
    t_r = lax.broadcasted_iota(i32, (TM, TM), 0)
    t_c = lax.broadcasted_iota(i32, (TM, TM), 1)
    earlier = jnp.where(t_r < t_c, 1.0, 0.0).astype(bf16)
    prefix = jnp.dot(ind.astype(bf16), earlier, preferred_element_type=f32)
    cnt = jnp.sum(ind, axis=-1, keepdims=True)
    cnt8 = jnp.floor((cnt + float(SUBLANES - 1)) * (1.0 / SUBLANES)) * float(SUBLANES)
    e_r = lax.broadcasted_iota(i32, (N_EXPERTS, N_EXPERTS), 0)
    e_c = lax.broadcasted_iota(i32, (N_EXPERTS, N_EXPERTS), 1)
    below = jnp.where(e_c < e_r, 1.0, 0.0).astype(bf16)
    seg_off = jnp.dot(below, jnp.broadcast_to(cnt8, (N_EXPERTS, TM)).astype(bf16),
                      preferred_element_type=f32)
    where_to = seg_off + prefix
    pos0 = jnp.sum(jnp.where(sel0, where_to, 0.0), axis=0, keepdims=True)
    pos1 = jnp.sum(jnp.where(sel1, where_to, 0.0), axis=0, keepdims=True)

    w0p = [v.astype(f32) for v in _split3(w0)]
    w1p = [v.astype(f32) for v in _split3(w1)]
    info = [pos0, pos1] + w0p + w1p
    return (_rows_to_tile(info, SUBLANES), _rows_to_tile(info, LANES).T,
            jnp.broadcast_to(cnt, (N_EXPERTS, LANES)))


def _outproj(tl, layer, x, ret_p, ret_s, cv_p, cv_s, mod_seq, mod_tok, g_norm, w_out_bf, wr_hi, wr_lo, br_pad):
    split_x = isinstance(x, tuple)
    tok_spec = pl.BlockSpec((TS, D_MODEL), lambda i: (i, 0))
    p_spec = lambda w: pl.BlockSpec((TS, w), lambda i: (tl.prompt_block(i), 0))
    s_spec = lambda w: pl.BlockSpec((TS, w), lambda i: (tl.sample_block(i), 0))
    if split_x:
        x_args, x_specs = list(x), [p_spec(D_MODEL), s_spec(D_MODEL)]
    else:
        x_args, x_specs = [x], [tok_spec]
    mods = []
    for col in (2, 3, 4):
        mods += list(_mod_specs(tl, layer, col))
    wr_spec = pl.BlockSpec((D_MODEL, LANES), lambda i: (0, 0))
    return pl.pallas_call(
        functools.partial(_outproj_kernel, np_steps=tl.np_steps, split_x=split_x),
        grid=(tl.n_steps,),
        in_specs=x_specs + [p_spec(D_RET), s_spec(D_RET), p_spec(D_CONV), s_spec(D_CONV)] + mods + [
            pl.BlockSpec((None, 1, D_MODEL), lambda i: (layer, 0, 0)),
            pl.BlockSpec((None, D_MODEL, D_MODEL), lambda i: (layer, 0, 0)),
            wr_spec, wr_spec,
            pl.BlockSpec((1, LANES), lambda i: (0, 0)),
        ],
        out_specs=[tok_spec, tok_spec,
                   pl.BlockSpec((STEP_TILES, SUBLANES, TM), lambda i: (i, 0, 0)),
                   pl.BlockSpec((TS, LANES), lambda i: (i, 0)),
                   pl.BlockSpec((STEP_TILES, N_EXPERTS, LANES), lambda i: (i, 0, 0))],
        out_shape=[jax.ShapeDtypeStruct((tl.n_tok, D_MODEL), f32),
                   jax.ShapeDtypeStruct((tl.n_tok, D_MODEL), bf16),
                   jax.ShapeDtypeStruct((tl.n_tiles, SUBLANES, TM), f32),
                   jax.ShapeDtypeStruct((tl.n_tok, LANES), f32),
                   jax.ShapeDtypeStruct((tl.n_tiles, N_EXPERTS, LANES), f32)],
        compiler_params=_cparams(1),
        name="outproj_router",
    )(*x_args, ret_p, ret_s, cv_p, cv_s, *([mod_seq, mod_tok] * 3), g_norm, w_out_bf, wr_hi, wr_lo, br_pad)


N_CHUNKS = SORT_ROWS // SUBLANES


class _Layout:
    def __init__(self, n_tiles):
        self.n_tiles = n_tiles
        self.tail_start = n_tiles * N_CHUNKS
        self.tail_n8 = self.tail_start + N_EXPERTS
        worst = 2 * n_tiles * TM + n_tiles * N_EXPERTS * (SUBLANES - 1) + N_EXPERTS * (BM - SUBLANES)
        self.n_blocks = -(-worst // BM)
        self.cap = self.n_blocks * BM
        self.dump = self.cap
        self.xs_rows = self.cap + -(-2 * STEP_TILES * SORT_ROWS // BM) * BM


def _moe_tables(lay, tile_counts):
    c8 = ((tile_counts.astype(i32) + SUBLANES - 1) // SUBLANES) * SUBLANES
    base8 = jnp.cumsum(c8, axis=0) - c8
    tot8 = jnp.sum(c8, axis=0)
    region = ((tot8 + BM - 1) // BM) * BM
    g_end = jnp.cumsum(region)
    g_start = g_end - region
    seg_end = jnp.cumsum(c8, axis=1)
    seg_dst = g_start[None, :] + base8
    n_used = g_end[-1] // BM
    blk = jnp.arange(lay.n_blocks, dtype=i32)
    block_e = jnp.minimum(jnp.sum((g_end[None, :] <= blk[:, None] * BM).astype(i32), axis=1), N_EXPERTS - 1)
    block_e = jnp.where(blk < n_used, block_e, block_e[n_used - 1])
    row0 = jnp.arange(N_CHUNKS, dtype=i32) * SUBLANES
    owner = jnp.sum((seg_end[:, None, :] <= row0[None, :, None]).astype(i32), axis=-1)
    onehot = (owner[:, :, None] == jnp.arange(N_EXPERTS, dtype=i32)[None, None, :]).astype(i32)
    delta = seg_dst - (seg_end - c8)
    chunk_dst = jnp.where(owner < N_EXPERTS, row0[None, :] + jnp.sum(onehot * delta[:, None, :], axis=-1), -1)
    tab = jnp.concatenate([chunk_dst.ravel(), g_start + tot8, (region - tot8) // SUBLANES]).astype(i32)
    ids = jnp.arange(N_EXPERTS, dtype=i32)
    later = jnp.where((ids[None, :] > ids[:, None]) & (region[None, :] > 0), ids[None, :], N_EXPERTS)
    next_e = jnp.min(later, axis=1)
    next_e = jnp.where(next_e == N_EXPERTS, -1, next_e)
    return tab, jnp.concatenate([block_e, n_used[None], next_e]).astype(i32)


def _for_chunks(n, fn):
    def body(c, carry):
        fn(c)
        return carry

    lax.fori_loop(0, n, body, 0)


def _dispatch_kernel(tab_ref, h2_ref, rows_ref, cols_ref, xs_hbm, sorted_ref, zero_ref, sem, *, lay, n_steps):
    i = pl.program_id(0)
    slot = i % 2
    step_rows = STEP_TILES * SORT_ROWS

    def tail_copy(dst):
        return pltpu.make_async_copy(zero_ref, xs_hbm.at[pl.ds(dst, SUBLANES)], sem.at[2])

    def wait_step(slot_):
        pltpu.make_async_copy(sorted_ref.at[slot_], xs_hbm.at[pl.ds(0, step_rows)], sem.at[slot_]).wait()

    @pl.when(i == 0)
    def _():
        zero_ref[...] = jnp.zeros_like(zero_ref)
        for e in range(N_EXPERTS):
            start = tab_ref[lay.tail_start + e]
            _for_chunks(tab_ref[lay.tail_n8 + e],
                        lambda c: tail_copy(pl.multiple_of(start + c * SUBLANES, SUBLANES)).start())

    @pl.when(i >= 2)
    def _():
        wait_step(slot)

    r_id = lax.broadcasted_iota(i32, (SORT_ROWS, TM), 0).astype(f32)
    lane = lax.broadcasted_iota(i32, (TM, LANES), 1)
    for t in range(STEP_TILES):
        pos0 = rows_ref[t, 0:1, :]
        pos1 = rows_ref[t, 1:2, :]
        p0 = r_id == pos0
        p1 = r_id == pos1
        perm = jnp.where(p0 | p1, 1.0, 0.0).astype(bf16)
        cols = cols_ref[t * TM:(t + 1) * TM, :]
        wpart0 = jnp.where((lane >= 2) & (lane < 5), cols, 0.0).astype(bf16)
        wpart1 = jnp.where((lane >= 5) & (lane < 8), cols, 0.0).astype(bf16)
        sw = (jnp.dot(jnp.where(p0, 1.0, 0.0).astype(bf16), wpart0, preferred_element_type=f32)
              + jnp.dot(jnp.where(p1, 1.0, 0.0).astype(bf16), wpart1, preferred_element_type=f32))
        base = t * SORT_ROWS
        sorted_ref[slot, base:base + SORT_ROWS, 0:XS_HALF] = _pack_bf16_pair(jnp.dot(
            perm, h2_ref[t * TM:(t + 1) * TM, :], preferred_element_type=f32))
        sorted_ref[slot, base:base + SORT_ROWS, XS_HALF:XS_W] = lax.bitcast_convert_type(jnp.broadcast_to(
            jnp.sum(sw, axis=-1, keepdims=True), (SORT_ROWS, LANES)), u32)

    for t in range(STEP_TILES):
        for c in range(N_CHUNKS):
            row = t * SORT_ROWS + c * SUBLANES
            dst = tab_ref[(i * STEP_TILES + t) * N_CHUNKS + c]
            dst = jnp.where(dst < 0, lay.dump + slot * step_rows + row, dst)
            pltpu.make_async_copy(sorted_ref.at[slot, pl.ds(row, SUBLANES)],
                                  xs_hbm.at[pl.ds(pl.multiple_of(dst, SUBLANES), SUBLANES)], sem.at[slot]).start()

    @pl.when(i == n_steps - 1)
    def _():
        if n_steps >= 2:
            wait_step(1 - slot)
        wait_step(slot)
        for e in range(N_EXPERTS):
            _for_chunks(tab_ref[lay.tail_n8 + e], lambda c: tail_copy(0).wait())


def _dispatch(tl, lay, tab, h2, rows, cols):
    grid_spec = pltpu.PrefetchScalarGridSpec(
        num_scalar_prefetch=1,
        grid=(tl.n_steps,),
        in_specs=[pl.BlockSpec((TS, D_MODEL), lambda i, t: (i, 0)),
                  pl.BlockSpec((STEP_TILES, SUBLANES, TM), lambda i, t: (i, 0, 0)),
                  pl.BlockSpec((TS, LANES), lambda i, t: (i, 0))],
        out_specs=pl.BlockSpec(memory_space=pl.ANY),
        scratch_shapes=[pltpu.VMEM((2, STEP_TILES * SORT_ROWS, XS_W), u32), pltpu.VMEM((SUBLANES, XS_W), u32),
                        pltpu.SemaphoreType.DMA((3,))],
    )
    return pl.pallas_call(
        functools.partial(_dispatch_kernel, lay=lay, n_steps=tl.n_steps),
        grid_spec=grid_spec,
        out_shape=jax.ShapeDtypeStruct((lay.xs_rows, XS_W), u32),
        compiler_params=_cparams(1),
        name="moe_dispatch",
    )(tab, h2, rows, cols)


def _expert_kernel(be_ref, xs_ref, wg_hbm, wu_hbm, wd_hbm, ys_ref, stage, w_bf, sem, *, n_blocks, layer):
    j = pl.program_id(0)

    def fetch(e):
        return [pltpu.make_async_copy(w.at[layer, e], stage.at[k], sem.at[k])
                for k, w in enumerate((wg_hbm, wu_hbm, wd_hbm))]

    @pl.when(j < be_ref[n_blocks])
    def _():
        e = be_ref[j]

        @pl.when(j == 0)
        def _():
            for copy in fetch(e):
                copy.start()

        @pl.when((j == 0) | (e != be_ref[jnp.maximum(j - 1, 0)]))
        def _():
            for copy in fetch(e):
                copy.wait()
            for k in range(3):
                w_bf[k] = stage[k].astype(bf16)
            nxt = be_ref[n_blocks + 1 + e]

            @pl.when(nxt >= 0)
            def _():
                for copy in fetch(nxt):
                    copy.start()

        x_lo, x_hi = _unpack_bf16_pair(xs_ref[:, 0:XS_HALF])

        def first_layer(k):
            return (jnp.dot(x_lo, w_bf[k, 0:XS_HALF, :], preferred_element_type=f32)
                    + jnp.dot(x_hi, w_bf[k, XS_HALF:D_MODEL, :], preferred_element_type=f32))

        mid = (jax.nn.silu(first_layer(0)) * first_layer(1)).astype(bf16)
        slot_w = lax.bitcast_convert_type(xs_ref[:, XS_HALF:XS_HALF + 1], f32)
        ys_ref[...] = jnp.dot(mid, w_bf[2], preferred_element_type=f32) * slot_w


def _experts(layer, lay, block_e, xs, w_gate, w_up, w_down):
    n_blocks = lay.n_blocks
    d_ff = w_gate.shape[-1]
    assert d_ff == D_MODEL
    used = lambda j, be: jnp.minimum(j, be[n_blocks] - 1)
    any_spec = pl.BlockSpec(memory_space=pl.ANY)
    grid_spec = pltpu.PrefetchScalarGridSpec(
        num_scalar_prefetch=1,
        grid=(n_blocks,),
        in_specs=[pl.BlockSpec((BM, XS_W), lambda j, be: (used(j, be), 0)), any_spec, any_spec, any_spec],
        out_specs=pl.BlockSpec((BM, D_MODEL), lambda j, be: (used(j, be), 0)),
        scratch_shapes=[pltpu.VMEM((3, D_MODEL, d_ff), f32), pltpu.VMEM((3, D_MODEL, d_ff), bf16),
                        pltpu.SemaphoreType.DMA((3,))],
    )
    return pl.pallas_call(
        functools.partial(_expert_kernel, n_blocks=n_blocks, layer=layer),
        grid_spec=grid_spec,
        out_shape=jax.ShapeDtypeStruct((lay.cap, D_MODEL), f32),
        compiler_params=_cparams(1),
        name="moe_experts",
    )(block_e, xs, w_gate, w_up, w_down)


def _combine_kernel(tab_ref, ys_hbm, cols_ref, x_ref, gts_ref, gtt_ref, *rest, n_steps, np_steps, final):
    if final:
        gf_ref, yp_ref, ysm_ref, staged, sem = rest
    else:
        xo_ref, staged, sem = rest
    i = pl.program_id(0)
    slot = i % 2
    is_s = i >= np_steps

    def start_step(step, slot_):
        for c in range(STEP_TILES * N_CHUNKS):
            src = jnp.maximum(tab_ref[step * (STEP_TILES * N_CHUNKS) + c], 0)
            pltpu.make_async_copy(ys_hbm.at[pl.ds(pl.multiple_of(src, SUBLANES), SUBLANES)],
                                  staged.at[slot_, pl.ds(c * SUBLANES, SUBLANES)], sem.at[slot_]).start()

    @pl.when(i == 0)
    def _():
        start_step(0, 0)

    @pl.when(i + 1 < n_steps)
    def _():
        start_step(i + 1, 1 - slot)

    pltpu.make_async_copy(ys_hbm.at[pl.ds(0, STEP_TILES * SORT_ROWS)], staged.at[slot], sem.at[slot]).wait()

    def step(sample):
        lane = lax.broadcasted_iota(i32, (TM, SORT_ROWS), 1).astype(f32)
        for t in range(STEP_TILES):
            rs = slice(t * TM, (t + 1) * TM)
            unperm = jnp.where((lane == cols_ref[rs, 0:1]) | (lane == cols_ref[rs, 1:2]), 1.0, 0.0).astype(bf16)
            parts = _split3(staged[slot, t * SORT_ROWS:(t + 1) * SORT_ROWS, :])
            ff = sum(jnp.dot(unperm, part, preferred_element_type=f32) for part in parts)
            xn = x_ref[rs, :] + (gtt_ref[rs, :] if sample else gts_ref[...]) * ff
            if final:
                (ysm_ref if sample else yp_ref)[rs, :] = _rms(xn, gf_ref[...])
            else:
                xo_ref[rs, :] = xn

    pl.when(is_s)(functools.partial(step, True))
    pl.when(jnp.logical_not(is_s))(functools.partial(step, False))


def _combine(tl, lay, layer, tab, ys, cols, x, mod_seq, mod_tok, g_final):
    final = g_final is not None
    tok_spec = pl.BlockSpec((TS, D_MODEL), lambda i, t: (i, 0))
    gt_seq, gt_tok = _mod_specs(tl, layer, 5)
    in_specs = [pl.BlockSpec(memory_space=pl.ANY), pl.BlockSpec((TS, LANES), lambda i, t: (i, 0)),
                tok_spec, gt_seq, gt_tok]
    args = [tab, ys, cols, x, mod_seq, mod_tok]
    if final:
        in_specs.append(pl.BlockSpec((1, D_MODEL), lambda i, t: (0, 0)))
        args.append(g_final)
        out_specs = [pl.BlockSpec((TS, D_MODEL), lambda i, t: (tl.prompt_block(i), 0)),
                     pl.BlockSpec((TS, D_MODEL), lambda i, t: (tl.sample_block(i), 0))]
        out_shape = [jax.ShapeDtypeStruct((tl.n_prompt, D_MODEL), f32),
                     jax.ShapeDtypeStruct((tl.n_sample, D_MODEL), f32)]
    else:
        out_specs = tok_spec
        out_shape = jax.ShapeDtypeStruct((tl.n_tok, D_MODEL), f32)
    grid_spec = pltpu.PrefetchScalarGridSpec(
        num_scalar_prefetch=1,
        grid=(tl.n_steps,),
        in_specs=in_specs,
        out_specs=out_specs,
        scratch_shapes=[pltpu.VMEM((2, STEP_TILES * SORT_ROWS, D_MODEL), f32), pltpu.SemaphoreType.DMA((2,))],
    )
    return pl.pallas_call(
        functools.partial(_combine_kernel, n_steps=tl.n_steps, np_steps=tl.np_steps, final=final),
        grid_spec=grid_spec,
        out_shape=out_shape,
        compiler_params=_cparams(1),
        name="moe_combine",
    )(*args)


def _rope_tables(tl):
    half = HEAD_D // 2
    inv = ROPE_BASE ** (-jnp.arange(half, dtype=f32) / half)
    pos_p = jnp.arange(tl.tp, dtype=i32)
    pos_s = PAST_LEN + jnp.arange(tl.ts, dtype=i32)
    pos = jnp.concatenate([pos_p, jnp.tile(pos_s, tl.bs)])
    ang = pos.astype(f32)[:, None] * inv[None, :]
    cos, sin = jnp.cos(ang), jnp.sin(ang)
    return jnp.concatenate([cos, cos], axis=-1), jnp.concatenate([-sin, sin], axis=-1)


def kernel(x_prompt, x_sample, state_ret, state_conv, c_prompt, c_sample, w_mod, b_mod, g_mix_norm, w_in,
           w_conv, b_conv, g_conv_ln, b_conv_ln, g_ret_gn, w_out, g_ffn_norm, w_router, b_router,
           w_exp_gate, w_exp_up, w_exp_down, g_final):
    bp, tp, _ = x_prompt.shape
    bs, ts, _ = x_sample.shape
    depth = w_mod.shape[0]
    tl = _Tiles(bp, tp, bs, ts)
    lay = _Layout(tl.n_tiles)

    c_all = jnp.concatenate([c_prompt, jnp.repeat(c_sample, ts, axis=0)], axis=0)
    mod_seq, mod_tok = _modulation(c_all, bp, w_mod, b_mod)
    mod_seq = mod_seq.reshape(depth, bp, 1, N_MOD * D_MODEL)

    cos_tab, sin_tab = _rope_tables(tl)
    w_in_bf = w_in.astype(bf16)
    w_out_bf = w_out.astype(bf16)
    wr_pad = jnp.pad(w_router.astype(f32), ((0, 0), (0, LANES - N_EXPERTS)))
    wr_hi = wr_pad.astype(bf16)
    wr_lo = (wr_pad - wr_hi.astype(f32)).astype(bf16)
    br_pad = jnp.pad(b_router.astype(f32), (0, LANES - N_EXPERTS)).reshape(1, LANES)
    vec3 = lambda t: t.reshape(depth, 1, t.shape[-1])
    g_mix3, g_ffn3, gn3 = vec3(g_mix_norm), vec3(g_ffn_norm), vec3(g_ret_gn)
    b_conv3, g_ln3, b_ln3 = vec3(b_conv), vec3(g_conv_ln), vec3(b_conv_ln)

    x = (x_prompt.reshape(tl.n_prompt, D_MODEL), x_sample.reshape(tl.n_sample, D_MODEL))
    ret_p, conv_p, conv_s = [], [], []
    ret_s_all = None
    for layer in range(depth):
        q, k, v, gate, a_s, co_p, buf_p = _inproj_conv(tl, layer, x, mod_seq, mod_tok, g_mix3, w_in_bf, cos_tab,
                                                        sin_tab, w_conv, b_conv3, g_ln3, b_ln3)
        ro_p, s_p = _retention_prompt(tl, layer, q, k, v, gate, gn3)
        ro_s, ret_s_all = _retention_sample(tl, layer, q, k, v, gate, state_ret, gn3, ret_s_all)
        co_s, buf_s = _conv_sample(tl, layer, a_s, state_conv, w_conv, b_conv3, g_ln3, b_ln3)
        x_mid, h2, rows, cols, tile_counts = _outproj(
            tl, layer, x, ro_p, ro_s, co_p, co_s, mod_seq, mod_tok, g_ffn3, w_out_bf, wr_hi, wr_lo, br_pad)
        tab, block_e = _moe_tables(lay, tile_counts[:, :, 0])
        xs = _dispatch(tl, lay, tab, h2, rows, cols)
        ys = _experts(layer, lay, block_e, xs, w_exp_gate, w_exp_up, w_exp_down)
        last = layer == depth - 1
        x = _combine(tl, lay, layer, tab, ys, cols, x_mid, mod_seq, mod_tok,
                     g_final.reshape(1, D_MODEL) if last else None)
        ret_p.append(s_p)
        conv_p.append(buf_p)
        conv_s.append(buf_s)
    y_p, y_s = x
    return (y_p.reshape(bp, tp, D_MODEL), y_s.reshape(bs, ts, D_MODEL),
            jnp.stack(ret_p), jnp.stack(conv_p), ret_s_all, jnp.stack(conv_s))
```

```python
import functools

import jax
import jax.numpy as jnp
from jax import lax
from jax.experimental import pallas as pl
from jax.experimental.pallas import tpu as pltpu

f32 = jnp.float32
bf16 = jnp.bfloat16
i32 = jnp.int32

D_MODEL = 1024
D_RET = 512
D_CONV = 512
N_HEADS = 4
HEAD_D = 128
RET_CHUNK = 128
RET_CHUNKS_PER_STEP = 4
ROPE_BASE = 10000.0
CONV_WIDTH = 31
CONV_HALO = CONV_WIDTH - 1
N_EXPERTS = 16
N_GROUPS = 4
GROUP_SIZE = N_EXPERTS // N_GROUPS
N_MOD = 6
EPS = 1e-6
PAST_LEN = 16384
D_IN = 4 * D_RET + 2 * D_CONV

LANES = 128
SUBLANES = 8
TM = 256
STEP_TILES = 2
TS = TM * STEP_TILES
BM = 512
CONV_ROWS = 128
NORM_ROWS = 64
SAMPLE_GROUP = 8
HALO_PAD = 32
VMEM_LIMIT = 56 * 1024 * 1024


def _cparams(n_axes, vmem=VMEM_LIMIT):
    return pltpu.CompilerParams(dimension_semantics=("arbitrary",) * n_axes, vmem_limit_bytes=vmem)


def _mod_kernel(c_ref, w_ref, b_ref, seq_ref, tok_ref):
    cond = jax.nn.silu(c_ref[...]).astype(bf16)
    mod = jnp.dot(cond, w_ref[...].astype(bf16), preferred_element_type=f32) + b_ref[...]
    n_seq = seq_ref.shape[0]
    seq_ref[...] = mod[0:n_seq, :]
    tok_ref[...] = mod[n_seq:, :]


def _modulation(c_all, n_seq, w_mod, b_mod):
    depth = w_mod.shape[0]
    m = c_all.shape[0]
    assert n_seq % SUBLANES == 0
    return pl.pallas_call(
        _mod_kernel,
        grid=(depth, N_MOD),
        in_specs=[
            pl.BlockSpec((m, D_MODEL), lambda l, j: (0, 0)),
            pl.BlockSpec((None, D_MODEL, D_MODEL), lambda l, j: (l, 0, j)),
            pl.BlockSpec((None, 1, D_MODEL), lambda l, j: (l, 0, j)),
        ],
        out_specs=[pl.BlockSpec((None, n_seq, D_MODEL), lambda l, j: (l, 0, j)),
                   pl.BlockSpec((None, m - n_seq, D_MODEL), lambda l, j: (l, 0, j))],
        out_shape=[jax.ShapeDtypeStruct((depth, n_seq, N_MOD * D_MODEL), f32),
                   jax.ShapeDtypeStruct((depth, m - n_seq, N_MOD * D_MODEL), f32)],
        compiler_params=_cparams(2),
        name="modulation",
    )(c_all, w_mod, b_mod.reshape(depth, 1, N_MOD * D_MODEL))


class _Tiles:
    def __init__(self, bp, tp, bs, ts):
        self.bp, self.tp, self.bs, self.ts = bp, tp, bs, ts
        self.n_prompt = bp * tp
        self.n_sample = bs * ts
        self.n_tok = self.n_prompt + self.n_sample
        assert tp % TS == 0 and self.n_sample % TS == 0
        self.tiles_per_seq = tp // TM
        self.np_tiles = self.n_prompt // TM
        self.n_tiles = self.n_tok // TM
        self.steps_per_seq = tp // TS
        self.np_steps = self.n_prompt // TS
        self.n_steps = self.n_tok // TS

    def prompt_block(self, i):
        return jnp.minimum(i, self.np_steps - 1)

    def sample_block(self, i):
        return jnp.maximum(i - self.np_steps, 0)

    def seq_index(self, i):
        return jnp.minimum(i // self.steps_per_seq, self.bp - 1)


def _mod_specs(tl, layer, col):
    seq = pl.BlockSpec((None, None, 1, D_MODEL), lambda i, *_: (layer, tl.seq_index(i), 0, col))
    tok = pl.BlockSpec((None, TS, D_MODEL), lambda i, *_: (layer, tl.sample_block(i), col))
    return seq, tok


def _rms(x, g):
    return x * lax.rsqrt(jnp.mean(x * x, axis=-1, keepdims=True) + EPS) * g


def _head_norm_gate(o, gn, gate):
    mu = jnp.mean(o, axis=-1, keepdims=True)
    var = jnp.mean(jnp.square(o - mu), axis=-1, keepdims=True)
    return jax.nn.silu(gate) * ((o - mu) * lax.rsqrt(var + EPS) * gn)


def _dot_nt(a, b):
    return lax.dot_general(a, b, (((1,), (1,)), ((), ())), preferred_element_type=f32)


def _dot_tn(a, b):
    return lax.dot_general(a, b, (((0,), (0,)), ((), ())), preferred_element_type=f32)


def _ret_prompt_kernel(q_ref, k_ref, v_ref, gate_ref, dec_ref, qd_ref, kd_ref, cd_ref, gn_ref,
                       o_ref, s_out_ref, s_ref):
    c = pl.program_id(1)

    @pl.when(c == 0)
    def _():
        s_ref[...] = jnp.zeros_like(s_ref)

    for ci in range(RET_CHUNKS_PER_STEP):
        rows = slice(ci * RET_CHUNK, (ci + 1) * RET_CHUNK)
        for hd in range(N_HEADS):
            sl = slice(hd * HEAD_D, (hd + 1) * HEAD_D)
            kh = k_ref[rows, sl]
            qb = q_ref[rows, sl]
            kb = kh.astype(bf16)
            vb = v_ref[rows, sl]
            s_old = s_ref[hd]
            scores = _dot_nt(qb, kb) * dec_ref[hd]
            inner = jnp.dot(scores.astype(bf16), vb, preferred_element_type=f32)
            cross = jnp.dot(qb, s_old.astype(bf16), preferred_element_type=f32) * qd_ref[hd]
            s_ref[hd] = s_old * cd_ref[hd] + _dot_tn((kh * kd_ref[hd]).astype(bf16), vb)
            o_ref[rows, sl] = _head_norm_gate(inner + cross, gn_ref[:, sl], gate_ref[rows, sl]).astype(bf16)

    @pl.when(c == pl.num_programs(1) - 1)
    def _():
        s_out_ref[...] = s_ref[...]


def _decay_tables(chunk, true_len):
    lg = jnp.log(1.0 - 2.0 ** (-5.0 - jnp.arange(N_HEADS, dtype=f32)))
    idx = jnp.arange(chunk, dtype=f32)
    rel = idx[:, None] - idx[None, :]
    decay = jnp.where(rel[None] >= 0, jnp.exp(jnp.maximum(rel, 0.0)[None] * lg[:, None, None]), 0.0)
    q_decay = jnp.exp((idx[None, :] + 1.0) * lg[:, None])
    k_decay = jnp.exp((true_len - 1.0 - idx[None, :]) * lg[:, None])
    c_decay = jnp.exp(true_len * lg)
    return decay, q_decay, k_decay, c_decay


def _retention_prompt(tl, layer, q, k, v, gate, g_ret_gn):
    step_rows = RET_CHUNK * RET_CHUNKS_PER_STEP
    assert tl.tp % step_rows == 0
    n_chunks = tl.tp // step_rows
    decay, q_decay, k_decay, c_decay = _decay_tables(RET_CHUNK, RET_CHUNK)
    bcast = lambda t: jnp.broadcast_to(t[:, :, None], (N_HEADS, RET_CHUNK, HEAD_D))
    cd = jnp.broadcast_to(c_decay[:, None, None], (N_HEADS, 1, HEAD_D))
    tok_spec = pl.BlockSpec((step_rows, D_RET), lambda b, c: (b * n_chunks + c, 0))
    tab_spec = pl.BlockSpec((N_HEADS, RET_CHUNK, HEAD_D), lambda b, c: (0, 0, 0))
    return pl.pallas_call(
        _ret_prompt_kernel,
        grid=(tl.bp, n_chunks),
        in_specs=[tok_spec] * 4 + [tab_spec] * 3 + [
            pl.BlockSpec((N_HEADS, 1, HEAD_D), lambda b, c: (0, 0, 0)),
            pl.BlockSpec((None, 1, D_RET), lambda b, c: (layer, 0, 0)),
        ],
        out_specs=[tok_spec, pl.BlockSpec((None, N_HEADS, HEAD_D, HEAD_D), lambda b, c: (b, 0, 0, 0))],
        out_shape=[jax.ShapeDtypeStruct((tl.n_prompt, D_RET), bf16),
                   jax.ShapeDtypeStruct((tl.bp, N_HEADS, HEAD_D, HEAD_D), f32)],
        scratch_shapes=[pltpu.VMEM((N_HEADS, HEAD_D, HEAD_D), f32)],
        compiler_params=_cparams(2),
        name="retention_prompt",
    )(q, k, v, gate, decay, bcast(q_decay), bcast(k_decay), cd, g_ret_gn)


def _ret_sample_kernel(q_ref, k_ref, v_ref, gate_ref, s_in_ref, dec_ref, qd_ref, kd_ref, cd_ref, gn_ref,
                       *rest, ts):
    o_ref, s_all_ref = rest[-2:]
    s_out_ref = s_all_ref.at[0]
    for other in range(1, s_all_ref.shape[0]):
        s_all_ref[other] = jnp.zeros(s_all_ref.shape[1:], f32)
    seqs_per_tile = SUBLANES // ts
    row = lax.broadcasted_iota(i32, (SUBLANES, HEAD_D), 0)
    q_all = q_ref[...].astype(f32)
    v_all = v_ref[...].astype(f32)
    outs = []
    for t in range(SAMPLE_GROUP // seqs_per_tile):
        rows = slice(t * SUBLANES, (t + 1) * SUBLANES)
        heads = []
        for hd in range(N_HEADS):
            sl = slice(hd * HEAD_D, (hd + 1) * HEAD_D)
            qh = q_all[rows, sl]
            kh = k_ref[rows, sl] * kd_ref[hd]
            vb = v_all[rows, sl].astype(bf16)
            qb = qh.astype(bf16)
            scores = _dot_nt(qb, k_ref[rows, sl].astype(bf16)) * dec_ref[hd]
            o = jnp.dot(scores.astype(bf16), vb, preferred_element_type=f32)
            for s in range(seqs_per_tile):
                b = t * seqs_per_tile + s
                mine = (row >= s * ts) & (row < (s + 1) * ts)
                s_old = s_in_ref[b, hd]
                q_s = jnp.where(mine, qh, 0.0).astype(bf16)
                k_s = jnp.where(mine, kh, 0.0).astype(bf16)
                o = o + jnp.dot(q_s, s_old.astype(bf16), preferred_element_type=f32) * qd_ref[hd]
                s_out_ref[b, hd] = s_old * cd_ref[hd] + _dot_tn(k_s, vb)
            heads.append(_head_norm_gate(o, gn_ref[:, sl], gate_ref[rows, sl]))
        outs.append(jnp.concatenate(heads, axis=-1))
    o_ref[...] = jnp.concatenate(outs, axis=0).astype(bf16)


def _retention_sample(tl, layer, q, k, v, gate, state_ret, g_ret_gn, prev_states):
    ts = tl.ts
    depth = state_ret.shape[0]
    assert SUBLANES % ts == 0 and tl.bs % SAMPLE_GROUP == 0
    seqs_per_tile = SUBLANES // ts
    decay, q_decay, k_decay, c_decay = _decay_tables(ts, ts)
    eye = jnp.eye(seqs_per_tile, dtype=f32)
    dec_tile = jnp.einsum("ab,hij->haibj", eye, decay).reshape(N_HEADS, SUBLANES, SUBLANES)
    tile_rows = lambda t: jnp.broadcast_to(jnp.tile(t, (1, seqs_per_tile))[:, :, None],
                                           (N_HEADS, SUBLANES, HEAD_D))
    cd = jnp.broadcast_to(c_decay[:, None, None], (N_HEADS, 1, HEAD_D))
    rows = SAMPLE_GROUP * ts
    first = tl.n_prompt // rows
    tok_spec = pl.BlockSpec((rows, D_RET), lambda i: (first + i, 0))
    const3 = lambda shape: pl.BlockSpec(shape, lambda i: (0, 0, 0))
    st_block = (SAMPLE_GROUP, N_HEADS, HEAD_D, HEAD_D)
    in_specs = [tok_spec] * 4 + [
        pl.BlockSpec((None,) + st_block, lambda i: (layer, i, 0, 0, 0)),
        const3((N_HEADS, SUBLANES, SUBLANES)),
        const3((N_HEADS, SUBLANES, HEAD_D)),
        const3((N_HEADS, SUBLANES, HEAD_D)),
        const3((N_HEADS, 1, HEAD_D)),
        pl.BlockSpec((None, 1, D_RET), lambda i: (layer, 0, 0)),
    ]
    args = [q, k, v, gate, state_ret, dec_tile, tile_rows(q_decay), tile_rows(k_decay), cd, g_ret_gn]
    if prev_states is None:
        state_spec = pl.BlockSpec((depth,) + st_block, lambda i: (0, i, 0, 0, 0))
        aliases = {}
    else:
        state_spec = pl.BlockSpec((1,) + st_block, lambda i: (layer, i, 0, 0, 0))
        in_specs.append(pl.BlockSpec(memory_space=pl.ANY))
        args.append(prev_states)
        aliases = {len(args) - 1: 1}
    return pl.pallas_call(
        functools.partial(_ret_sample_kernel, ts=ts),
        grid=(tl.bs // SAMPLE_GROUP,),
        in_specs=in_specs,
        out_specs=[pl.BlockSpec((rows, D_RET), lambda i: (i, 0)), state_spec],
        out_shape=[jax.ShapeDtypeStruct((tl.n_sample, D_RET), bf16),
                   jax.ShapeDtypeStruct((depth, tl.bs, N_HEADS, HEAD_D, HEAD_D), f32)],
        input_output_aliases=aliases,
        compiler_params=_cparams(1),
        name="retention_sample",
    )(*args)


def _ln_silu(cv, g, b):
    mu = jnp.mean(cv, axis=-1, keepdims=True)
    var = jnp.mean(jnp.square(cv - mu), axis=-1, keepdims=True)
    return jax.nn.silu((cv - mu) * lax.rsqrt(var + EPS) * g + b)


def _conv_taps(window, w_ref, b_ref, n_rows):
    cols = []
    for col in range(D_CONV // LANES):
        lanes = slice(col * LANES, (col + 1) * LANES)
        acc = jnp.broadcast_to(b_ref[:, lanes], (n_rows, LANES))
        for tap in range(CONV_WIDTH):
            acc = acc + window(col, tap) * w_ref[tap:tap + 1, lanes]
        cols.append(acc)
    return jnp.concatenate(cols, axis=-1)


def _conv_sample_kernel(a_ref, st_ref, w_ref, b_ref, g_ref, bl_ref, o_ref, buf_ref, full_ref, cv_ref, *, ts):
    for s in range(SAMPLE_GROUP):
        for col in range(D_CONV // LANES):
            lanes = slice(col * LANES, (col + 1) * LANES)
            full_ref[col, 0:CONV_HALO, :] = st_ref[s, :, lanes]
            full_ref[col, CONV_HALO:CONV_HALO + ts, :] = a_ref[s * ts:(s + 1) * ts, lanes]
            buf_ref[s, :, lanes] = full_ref[col, ts:ts + CONV_HALO, :]
        cv_ref[s * ts:(s + 1) * ts, :] = _conv_taps(lambda col, tap: full_ref[col, tap:tap + ts, :],
                                                    w_ref, b_ref, ts)
    o_ref[...] = _ln_silu(cv_ref[...], g_ref[...], bl_ref[...]).astype(bf16)


def _conv_sample(tl, layer, a, state_conv, w_conv, b_conv, g_ln, b_ln):
    ts = tl.ts
    rows = SAMPLE_GROUP * ts
    first = tl.n_prompt // rows
    vec = pl.BlockSpec((None, 1, D_CONV), lambda i: (layer, 0, 0))
    return pl.pallas_call(
        functools.partial(_conv_sample_kernel, ts=ts),
        grid=(tl.bs // SAMPLE_GROUP,),
        in_specs=[pl.BlockSpec((rows, D_CONV), lambda i: (first + i, 0)),
                  pl.BlockSpec((None, SAMPLE_GROUP, CONV_HALO, D_CONV), lambda i: (layer, i, 0, 0)),
                  pl.BlockSpec((None, CONV_WIDTH, D_CONV), lambda i: (layer, 0, 0)),
                  vec, vec, vec],
        out_specs=[pl.BlockSpec((rows, D_CONV), lambda i: (i, 0)),
                   pl.BlockSpec((SAMPLE_GROUP, CONV_HALO, D_CONV), lambda i: (i, 0, 0))],
        out_shape=[jax.ShapeDtypeStruct((tl.n_sample, D_CONV), bf16),
                   jax.ShapeDtypeStruct((tl.bs, CONV_HALO, D_CONV), f32)],
        scratch_shapes=[pltpu.VMEM((D_CONV // LANES, CONV_HALO + ts + SUBLANES, LANES), f32),
                        pltpu.VMEM((rows, D_CONV), f32)],
        compiler_params=_cparams(1),
        name="conv_sample",
    )(a, state_conv, w_conv, b_conv, g_ln, b_ln)


N_CONV_COLS = D_CONV // LANES
N_CONV_CHUNKS = TM // CONV_ROWS


def _inproj_kernel(*refs, np_steps, split_x):
    if split_x:
        xp_ref, xs_ref = refs[:2]
        refs = refs[2:]
    else:
        x_ref = refs[0]
        refs = refs[1:]
    (shs_ref, sht_ref, scs_ref, sct_ref, g_ref, w_ref, cos_ref, sin_ref,
     q_ref, k_ref, v_ref, gate_ref, a_ref) = refs
    is_s = pl.program_id(0) >= np_steps

    def step(sample):
        for t in range(STEP_TILES):
            rs = slice(t * TM, (t + 1) * TM)
            if split_x:
                x = xs_ref[rs, :] if sample else xp_ref[rs, :]
            else:
                x = x_ref[rs, :]
            sh, sc = (sht_ref[rs, :], sct_ref[rs, :]) if sample else (shs_ref[...], scs_ref[...])
            hb = (_rms(x, g_ref[...]) * (1.0 + sc) + sh).astype(bf16)

            def group(g):
                return jnp.dot(hb, w_ref[:, g * D_RET:(g + 1) * D_RET], preferred_element_type=f32)

            cos = cos_ref[rs, :]
            sin = sin_ref[rs, :]

            def rope(th):
                return th * cos + pltpu.roll(th, HEAD_D // 2, 1) * sin

            qg, kg = group(0), group(1)
            for hd in range(N_HEADS):
                sl = slice(hd * HEAD_D, (hd + 1) * HEAD_D)
                q_ref[rs, sl] = rope(qg[:, sl]).astype(bf16)
                k_ref[rs, sl] = rope(kg[:, sl]) * (HEAD_D ** -0.5)
            v_ref[rs, :] = group(2).astype(bf16)
            gate_ref[rs, :] = group(3)
            a_ref[rs, :] = group(4) * jax.nn.sigmoid(group(5))

    pl.when(is_s)(functools.partial(step, True))
    pl.when(jnp.logical_not(is_s))(functools.partial(step, False))


def _conv_prompt_kernel(a_ref, w_ref, b_ref, g_ref, bl_ref, o_ref, buf_ref, full_ref, cv_ref):
    j = pl.program_id(1)

    @pl.when(j == 0)
    def _():
        full_ref[:, 0:HALO_PAD, :] = jnp.zeros((N_CONV_COLS, HALO_PAD, LANES), f32)

    @pl.when(j > 0)
    def _():
        full_ref[:, 0:HALO_PAD, :] = full_ref[:, TM:TM + HALO_PAD, :]

    for col in range(N_CONV_COLS):
        full_ref[col, HALO_PAD:HALO_PAD + TM, :] = a_ref[:, col * LANES:(col + 1) * LANES]
    shift = HALO_PAD - CONV_HALO

    def taps(idx, carry):
        col = idx // N_CONV_CHUNKS
        r0 = pl.multiple_of((idx % N_CONV_CHUNKS) * CONV_ROWS, CONV_ROWS)
        acc = jnp.broadcast_to(b_ref[col], (CONV_ROWS, LANES))
        for tap in range(CONV_WIDTH):
            acc = acc + full_ref[col, pl.ds(r0 + (tap + shift), CONV_ROWS), :] * w_ref[col, tap:tap + 1, :]
        cv_ref[col, pl.ds(r0, CONV_ROWS), :] = acc
        return carry

    lax.fori_loop(0, N_CONV_COLS * N_CONV_CHUNKS, taps, 0)

    for r0 in range(0, TM, NORM_ROWS):
        cv = jnp.concatenate([cv_ref[col, r0:r0 + NORM_ROWS, :] for col in range(N_CONV_COLS)], axis=-1)
        o_ref[r0:r0 + NORM_ROWS, :] = _ln_silu(cv, g_ref[...], bl_ref[...]).astype(bf16)

    @pl.when(j == pl.num_programs(1) - 1)
    def _():
        buf_ref[...] = a_ref[TM - CONV_HALO:TM, :]


def _conv_prompt(tl, layer, a, w_conv, b_conv, g_ln, b_ln):
    tps = tl.tiles_per_seq
    depth = w_conv.shape[0]
    w_cols = w_conv.reshape(depth, CONV_WIDTH, N_CONV_COLS, LANES).transpose(0, 2, 1, 3)
    b_cols = b_conv.reshape(depth, N_CONV_COLS, 1, LANES)
    vec = pl.BlockSpec((None, 1, D_CONV), lambda b, j: (layer, 0, 0))
    return pl.pallas_call(
        _conv_prompt_kernel,
        grid=(tl.bp, tps),
        in_specs=[pl.BlockSpec((TM, D_CONV), lambda b, j: (b * tps + j, 0)),
                  pl.BlockSpec((None, N_CONV_COLS, CONV_WIDTH, LANES), lambda b, j: (layer, 0, 0, 0)),
                  pl.BlockSpec((None, N_CONV_COLS, 1, LANES), lambda b, j: (layer, 0, 0, 0)),
                  vec, vec],
        out_specs=[pl.BlockSpec((TM, D_CONV), lambda b, j: (b * tps + j, 0)),
                   pl.BlockSpec((None, CONV_HALO, D_CONV), lambda b, j: (b, 0, 0))],
        out_shape=[jax.ShapeDtypeStruct((tl.n_prompt, D_CONV), bf16),
                   jax.ShapeDtypeStruct((tl.bp, CONV_HALO, D_CONV), f32)],
        scratch_shapes=[pltpu.VMEM((N_CONV_COLS, HALO_PAD + TM, LANES), f32),
                        pltpu.VMEM((N_CONV_COLS, TM, LANES), f32)],
        compiler_params=_cparams(2),
        name="conv_prompt",
    )(a, w_cols, b_cols, g_ln, b_ln)


def _inproj(tl, layer, x, mod_seq, mod_tok, g_norm, w_in_bf, cos_tab, sin_tab):
    split_x = isinstance(x, tuple)
    tok_spec = pl.BlockSpec((TS, D_MODEL), lambda i: (i, 0))
    if split_x:
        x_args = list(x)
        x_specs = [pl.BlockSpec((TS, D_MODEL), lambda i: (tl.prompt_block(i), 0)),
                   pl.BlockSpec((TS, D_MODEL), lambda i: (tl.sample_block(i), 0))]
    else:
        x_args, x_specs = [x], [tok_spec]
    sh_seq, sh_tok = _mod_specs(tl, layer, 0)
    sc_seq, sc_tok = _mod_specs(tl, layer, 1)

    def table_block(i):
        return jnp.where(i < tl.np_steps, i % tl.steps_per_seq, tl.steps_per_seq + tl.sample_block(i))

    tab_spec = pl.BlockSpec((TS, HEAD_D), lambda i: (table_block(i), 0))
    row_spec = pl.BlockSpec((TS, D_RET), lambda i: (i, 0))
    row_sd = lambda dt: jax.ShapeDtypeStruct((tl.n_tok, D_RET), dt)
    return pl.pallas_call(
        functools.partial(_inproj_kernel, np_steps=tl.np_steps, split_x=split_x),
        grid=(tl.n_steps,),
        in_specs=x_specs + [
            sh_seq, sh_tok, sc_seq, sc_tok,
            pl.BlockSpec((None, 1, D_MODEL), lambda i: (layer, 0, 0)),
            pl.BlockSpec((None, D_MODEL, D_IN), lambda i: (layer, 0, 0)),
            tab_spec, tab_spec,
        ],
        out_specs=[row_spec] * 5,
        out_shape=[row_sd(bf16), row_sd(f32), row_sd(bf16), row_sd(f32), row_sd(f32)],
        compiler_params=_cparams(1),
        name="inproj",
    )(*x_args, mod_seq, mod_tok, mod_seq, mod_tok, g_norm, w_in_bf, cos_tab, sin_tab)


SORT_ROWS = 2 * TM + N_EXPERTS * SUBLANES
XS_HALF = D_MODEL // 2
XS_W = XS_HALF + LANES
u32 = jnp.uint32


def _pack_bf16_pair(x):
    lo = lax.shift_right_logical(lax.bitcast_convert_type(x[:, 0:XS_HALF], u32), u32(16))
    hi = lax.bitcast_convert_type(x[:, XS_HALF:D_MODEL], u32) & u32(0xFFFF0000)
    return hi | lo


def _unpack_bf16_pair(words):
    lo = lax.bitcast_convert_type(lax.shift_left(words, u32(16)), f32).astype(bf16)
    hi = lax.bitcast_convert_type(words & u32(0xFFFF0000), f32).astype(bf16)
    return lo, hi


def _split3(x):
    a = x.astype(bf16)
    r = x - a.astype(f32)
    b = r.astype(bf16)
    c = (r - b.astype(f32)).astype(bf16)
    return a, b, c


def _first_of4(vals, m):
    return jnp.where(vals[0] == m, 0.0, jnp.where(vals[1] == m, 1.0, jnp.where(vals[2] == m, 2.0, 3.0)))


def _rows_to_tile(rows, n_rows):
    sub = lax.broadcasted_iota(i32, (n_rows, TM), 0)
    out = jnp.zeros((n_rows, TM), f32)
    for r, val in enumerate(rows):
        out = jnp.where(sub == r, val, out)
    return out


def _outproj_kernel(*refs, np_steps, split_x):
    if split_x:
        xp_ref, xs_ref = refs[:2]
        refs = refs[2:]
    else:
        x_ref = refs[0]
        refs = refs[1:]
    (retp_ref, rets_ref, cvp_ref, cvs_ref, gts_ref, gtt_ref, shs_ref, sht_ref, scs_ref, sct_ref,
     g_ref, wo_ref, wrh_ref, wrl_ref, br_ref,
     xo_ref, h2_ref, rows_ref, cols_ref, cnt_ref) = refs
    is_s = pl.program_id(0) >= np_steps

    def step(sample):
        for t in range(STEP_TILES):
            rs = slice(t * TM, (t + 1) * TM)
            if split_x:
                x = xs_ref[rs, :] if sample else xp_ref[rs, :]
            else:
                x = x_ref[rs, :]
            ret = rets_ref[rs, :] if sample else retp_ref[rs, :]
            cv = cvs_ref[rs, :] if sample else cvp_ref[rs, :]
            gt, sh, sc = ((gtt_ref[rs, :], sht_ref[rs, :], sct_ref[rs, :]) if sample
                          else (gts_ref[...], shs_ref[...], scs_ref[...]))
            mix_out = (jnp.dot(ret, wo_ref[0:D_RET, :], preferred_element_type=f32)
                       + jnp.dot(cv, wo_ref[D_RET:D_RET + D_CONV, :], preferred_element_type=f32))
            xn = x + gt * mix_out
            xo_ref[rs, :] = xn
            h2 = _rms(xn, g_ref[...]) * (1.0 + sc) + sh
            h_hi = h2.astype(bf16)
            h2_ref[rs, :] = h_hi
            rows, cols, cnt = _route_tile(h2, h_hi, wrh_ref, wrl_ref, br_ref)
            rows_ref[t] = rows
            cols_ref[rs, :] = cols
            cnt_ref[t] = cnt

    pl.when(is_s)(functools.partial(step, True))
    pl.when(jnp.logical_not(is_s))(functools.partial(step, False))


def _route_tile(h2, h_hi, wrh_ref, wrl_ref, br_ref):
    h_lo = (h2 - h_hi.astype(f32)).astype(bf16)
    logits = (jnp.dot(h_hi, wrh_ref[...], preferred_element_type=f32)
              + jnp.dot(h_hi, wrl_ref[...], preferred_element_type=f32)
              + jnp.dot(h_lo, wrh_ref[...], preferred_element_type=f32)) + br_ref[...]
    lt = logits.T
    row = [lt[e:e + 1, :] for e in range(N_EXPERTS)]
    top = functools.reduce(jnp.maximum, row)
    ex = [jnp.exp(r - top) for r in row]
    den = functools.reduce(jnp.add, ex)
    p = [v / den for v in ex]

    best = None
    for g in range(N_GROUPS):
        a = p[g * GROUP_SIZE:(g + 1) * GROUP_SIZE]
        m1 = functools.reduce(jnp.maximum, a)
        i1 = _first_of4(a, m1)
        b = [jnp.where(i1 == float(j), -1.0, a[j]) for j in range(GROUP_SIZE)]
        m2 = functools.reduce(jnp.maximum, b)
        i2 = _first_of4(b, m2)
        cand = (m1 + m2, m1, m2, i1 + float(g * GROUP_SIZE), i2 + float(g * GROUP_SIZE))
        if best is None:
            best = cand
        else:
            take = cand[0] > best[0]
            best = tuple(jnp.where(take, c, o) for c, o in zip(cand, best))
    _, m1, m2, e0, e1 = best
    denom = m1 + m2
    w0 = m1 / denom
    w1 = m2 / denom

    ex_id = lax.broadcasted_iota(i32, (N_EXPERTS, TM), 0).astype(f32)
    sel0 = ex_id == e0
    sel1 = ex_id == e1
    ind = jnp.where(sel0 | sel1, 1.0, 0.0)
    t_r = lax.broadcasted_iota(i32, (TM, TM), 0)
    t_c = lax.broadcasted_iota(i32, (TM, TM), 1)
    earlier = jnp.where(t_r < t_c, 1.0, 0.0).astype(bf16)
    prefix = jnp.dot(ind.astype(bf16), earlier, preferred_element_type=f32)
    cnt = jnp.sum(ind, axis=-1, keepdims=True)
    cnt8 = jnp.floor((cnt + float(SUBLANES - 1)) * (1.0 / SUBLANES)) * float(SUBLANES)
    e_r = lax.broadcasted_iota(i32, (N_EXPERTS, N_EXPERTS), 0)
    e_c = lax.broadcasted_iota(i32, (N_EXPERTS, N_EXPERTS), 1)
    below = jnp.where(e_c < e_r, 1.0, 0.0).astype(bf16)
    seg_off = jnp.dot(below, jnp.broadcast_to(cnt8, (N_EXPERTS, TM)).astype(bf16),
                      preferred_element_type=f32)
    where_to = seg_off + prefix
    pos0 = jnp.sum(jnp.where(sel0, where_to, 0.0), axis=0, keepdims=True)
    pos1 = jnp.sum(jnp.where(sel1, where_to, 0.0), axis=0, keepdims=True)

    w0p = [v.astype(f32) for v in _split3(w0)]
    w1p = [v.astype(f32) for v in _split3(w1)]
    info = [pos0, pos1] + w0p + w1p
    return (_rows_to_tile(info, SUBLANES), _rows_to_tile(info, LANES).T,
            jnp.broadcast_to(cnt, (N_EXPERTS, LANES)))


def _outproj(tl, layer, x, ret_p, ret_s, cv_p, cv_s, mod_seq, mod_tok, g_norm, w_out_bf, wr_hi, wr_lo, br_pad):
    split_x = isinstance(x, tuple)
    tok_spec = pl.BlockSpec((TS, D_MODEL), lambda i: (i, 0))
    p_spec = lambda w: pl.BlockSpec((TS, w), lambda i: (tl.prompt_block(i), 0))
    s_spec = lambda w: pl.BlockSpec((TS, w), lambda i: (tl.sample_block(i), 0))
    if split_x:
        x_args, x_specs = list(x), [p_spec(D_MODEL), s_spec(D_MODEL)]
    else:
        x_args, x_specs = [x], [tok_spec]
    mods = []
    for col in (2, 3, 4):
        mods += list(_mod_specs(tl, layer, col))
    wr_spec = pl.BlockSpec((D_MODEL, LANES), lambda i: (0, 0))
    return pl.pallas_call(
        functools.partial(_outproj_kernel, np_steps=tl.np_steps, split_x=split_x),
        grid=(tl.n_steps,),
        in_specs=x_specs + [p_spec(D_RET), s_spec(D_RET), p_spec(D_CONV), s_spec(D_CONV)] + mods + [
            pl.BlockSpec((None, 1, D_MODEL), lambda i: (layer, 0, 0)),
            pl.BlockSpec((None, D_MODEL, D_MODEL), lambda i: (layer, 0, 0)),
            wr_spec, wr_spec,
            pl.BlockSpec((1, LANES), lambda i: (0, 0)),
        ],
        out_specs=[tok_spec, tok_spec,
                   pl.BlockSpec((STEP_TILES, SUBLANES, TM), lambda i: (i, 0, 0)),
                   pl.BlockSpec((TS, LANES), lambda i: (i, 0)),
                   pl.BlockSpec((STEP_TILES, N_EXPERTS, LANES), lambda i: (i, 0, 0))],
        out_shape=[jax.ShapeDtypeStruct((tl.n_tok, D_MODEL), f32),
                   jax.ShapeDtypeStruct((tl.n_tok, D_MODEL), bf16),
                   jax.ShapeDtypeStruct((tl.n_tiles, SUBLANES, TM), f32),
                   jax.ShapeDtypeStruct((tl.n_tok, LANES), f32),
                   jax.ShapeDtypeStruct((tl.n_tiles, N_EXPERTS, LANES), f32)],
        compiler_params=_cparams(1),
        name="outproj_router",
    )(*x_args, ret_p, ret_s, cv_p, cv_s, *([mod_seq, mod_tok] * 3), g_norm, w_out_bf, wr_hi, wr_lo, br_pad)


N_CHUNKS = SORT_ROWS // SUBLANES


class _Layout:
    def __init__(self, n_tiles):
        self.n_tiles = n_tiles
        self.tail_start = n_tiles * N_CHUNKS
        self.tail_n8 = self.tail_start + N_EXPERTS
        worst = 2 * n_tiles * TM + n_tiles * N_EXPERTS * (SUBLANES - 1) + N_EXPERTS * (BM - SUBLANES)
        self.n_blocks = -(-worst // BM)
        self.cap = self.n_blocks * BM
        self.dump = self.cap
        self.xs_rows = self.cap + -(-2 * STEP_TILES * SORT_ROWS // BM) * BM


def _moe_tables(lay, tile_counts):
    c8 = ((tile_counts.astype(i32) + SUBLANES - 1) // SUBLANES) * SUBLANES
    base8 = jnp.cumsum(c8, axis=0) - c8
    tot8 = jnp.sum(c8, axis=0)
    region = ((tot8 + BM - 1) // BM) * BM
    g_end = jnp.cumsum(region)
    g_start = g_end - region
    seg_end = jnp.cumsum(c8, axis=1)
    seg_dst = g_start[None, :] + base8
    n_used = g_end[-1] // BM
    blk = jnp.arange(lay.n_blocks, dtype=i32)
    block_e = jnp.minimum(jnp.sum((g_end[None, :] <= blk[:, None] * BM).astype(i32), axis=1), N_EXPERTS - 1)
    block_e = jnp.where(blk < n_used, block_e, block_e[n_used - 1])
    row0 = jnp.arange(N_CHUNKS, dtype=i32) * SUBLANES
    owner = jnp.sum((seg_end[:, None, :] <= row0[None, :, None]).astype(i32), axis=-1)
    onehot = (owner[:, :, None] == jnp.arange(N_EXPERTS, dtype=i32)[None, None, :]).astype(i32)
    delta = seg_dst - (seg_end - c8)
    chunk_dst = jnp.where(owner < N_EXPERTS, row0[None, :] + jnp.sum(onehot * delta[:, None, :], axis=-1), -1)
    tab = jnp.concatenate([chunk_dst.ravel(), g_start + tot8, (region - tot8) // SUBLANES]).astype(i32)
    ids = jnp.arange(N_EXPERTS, dtype=i32)
    later = jnp.where((ids[None, :] > ids[:, None]) & (region[None, :] > 0), ids[None, :], N_EXPERTS)
    next_e = jnp.min(later, axis=1)
    next_e = jnp.where(next_e == N_EXPERTS, -1, next_e)
    owner_end = jnp.sum((block_e[:, None] == ids[None, :]).astype(i32) * (g_start + tot8)[None, :], axis=1)
    rows_used = jnp.clip(owner_end - blk * BM, 0, BM)
    return tab, jnp.concatenate([block_e, n_used[None], next_e, rows_used]).astype(i32)


def _for_chunks(n, fn):
    def body(c, carry):
        fn(c)
        return carry

    lax.fori_loop(0, n, body, 0)


def _dispatch_kernel(tab_ref, h2_ref, rows_ref, cols_ref, xs_hbm, sorted_ref, zero_ref, sem, *, lay, n_steps):
    i = pl.program_id(0)
    slot = i % 2
    step_rows = STEP_TILES * SORT_ROWS

    def tail_copy(dst):
        return pltpu.make_async_copy(zero_ref, xs_hbm.at[pl.ds(dst, SUBLANES)], sem.at[2])

    def wait_step(slot_):
        pltpu.make_async_copy(sorted_ref.at[slot_], xs_hbm.at[pl.ds(0, step_rows)], sem.at[slot_]).wait()

    @pl.when(i == 0)
    def _():
        zero_ref[...] = jnp.zeros_like(zero_ref)
        for e in range(N_EXPERTS):
            start = tab_ref[lay.tail_start + e]
            _for_chunks(tab_ref[lay.tail_n8 + e],
                        lambda c: tail_copy(pl.multiple_of(start + c * SUBLANES, SUBLANES)).start())

    @pl.when(i >= 2)
    def _():
        wait_step(slot)

    r_id = lax.broadcasted_iota(i32, (SORT_ROWS, TM), 0).astype(f32)
    lane = lax.broadcasted_iota(i32, (TM, LANES), 1)
    for t in range(STEP_TILES):
        pos0 = rows_ref[t, 0:1, :]
        pos1 = rows_ref[t, 1:2, :]
        p0 = r_id == pos0
        p1 = r_id == pos1
        perm = jnp.where(p0 | p1, 1.0, 0.0).astype(bf16)
        cols = cols_ref[t * TM:(t + 1) * TM, :]
        wpart0 = jnp.where((lane >= 2) & (lane < 5), cols, 0.0).astype(bf16)
        wpart1 = jnp.where((lane >= 5) & (lane < 8), cols, 0.0).astype(bf16)
        sw = (jnp.dot(jnp.where(p0, 1.0, 0.0).astype(bf16), wpart0, preferred_element_type=f32)
              + jnp.dot(jnp.where(p1, 1.0, 0.0).astype(bf16), wpart1, preferred_element_type=f32))
        base = t * SORT_ROWS
        sorted_ref[slot, base:base + SORT_ROWS, 0:XS_HALF] = _pack_bf16_pair(jnp.dot(
            perm, h2_ref[t * TM:(t + 1) * TM, :], preferred_element_type=f32))
        sorted_ref[slot, base:base + SORT_ROWS, XS_HALF:XS_W] = lax.bitcast_convert_type(jnp.broadcast_to(
            jnp.sum(sw, axis=-1, keepdims=True), (SORT_ROWS, LANES)), u32)

    for t in range(STEP_TILES):
        for c in range(N_CHUNKS):
            row = t * SORT_ROWS + c * SUBLANES
            dst = tab_ref[(i * STEP_TILES + t) * N_CHUNKS + c]
            dst = jnp.where(dst < 0, lay.dump + slot * step_rows + row, dst)
            pltpu.make_async_copy(sorted_ref.at[slot, pl.ds(row, SUBLANES)],
                                  xs_hbm.at[pl.ds(pl.multiple_of(dst, SUBLANES), SUBLANES)], sem.at[slot]).start()

    @pl.when(i == n_steps - 1)
    def _():
        if n_steps >= 2:
            wait_step(1 - slot)
        wait_step(slot)
        for e in range(N_EXPERTS):
            _for_chunks(tab_ref[lay.tail_n8 + e], lambda c: tail_copy(0).wait())


def _dispatch(tl, lay, tab, h2, rows, cols):
    grid_spec = pltpu.PrefetchScalarGridSpec(
        num_scalar_prefetch=1,
        grid=(tl.n_steps,),
        in_specs=[pl.BlockSpec((TS, D_MODEL), lambda i, t: (i, 0)),
                  pl.BlockSpec((STEP_TILES, SUBLANES, TM), lambda i, t: (i, 0, 0)),
                  pl.BlockSpec((TS, LANES), lambda i, t: (i, 0))],
        out_specs=pl.BlockSpec(memory_space=pl.ANY),
        scratch_shapes=[pltpu.VMEM((2, STEP_TILES * SORT_ROWS, XS_W), u32), pltpu.VMEM((SUBLANES, XS_W), u32),
                        pltpu.SemaphoreType.DMA((3,))],
    )
    return pl.pallas_call(
        functools.partial(_dispatch_kernel, lay=lay, n_steps=tl.n_steps),
        grid_spec=grid_spec,
        out_shape=jax.ShapeDtypeStruct((lay.xs_rows, XS_W), u32),
        compiler_params=_cparams(1),
        name="moe_dispatch",
    )(tab, h2, rows, cols)


def _expert_kernel(be_ref, xs_ref, wg_hbm, wu_hbm, wd_hbm, ys_ref, stage, w_bf, sem, *, n_blocks, layer):
    j = pl.program_id(0)

    def fetch(e):
        return [pltpu.make_async_copy(w.at[layer, e], stage.at[k], sem.at[k])
                for k, w in enumerate((wg_hbm, wu_hbm, wd_hbm))]

    @pl.when(j < be_ref[n_blocks])
    def _():
        e = be_ref[j]

        @pl.when(j == 0)
        def _():
            for copy in fetch(e):
                copy.start()

        @pl.when((j == 0) | (e != be_ref[jnp.maximum(j - 1, 0)]))
        def _():
            for copy in fetch(e):
                copy.wait()
            for k in range(3):
                w_bf[k] = stage[k].astype(bf16)
            nxt = be_ref[n_blocks + 1 + e]

            @pl.when(nxt >= 0)
            def _():
                for copy in fetch(nxt):
                    copy.start()

        def ffn(rows):
            x_lo, x_hi = _unpack_bf16_pair(xs_ref[rows, 0:XS_HALF])

            def first_layer(k):
                return (jnp.dot(x_lo, w_bf[k, 0:XS_HALF, :], preferred_element_type=f32)
                        + jnp.dot(x_hi, w_bf[k, XS_HALF:D_MODEL, :], preferred_element_type=f32))

            mid = (jax.nn.silu(first_layer(0)) * first_layer(1)).astype(bf16)
            slot_w = lax.bitcast_convert_type(xs_ref[rows, XS_HALF:XS_HALF + 1], f32)
            ys_ref[rows, :] = jnp.dot(mid, w_bf[2], preferred_element_type=f32) * slot_w

        half = BM // 2
        rows_used = be_ref[n_blocks + 1 + N_EXPERTS + j]

        @pl.when(rows_used > half)
        def _():
            ffn(slice(0, BM))

        @pl.when(rows_used <= half)
        def _():
            ffn(slice(0, half))
            ys_ref[half:BM, :] = jnp.zeros((BM - half, D_MODEL), f32)


def _experts(layer, lay, block_e, xs, w_gate, w_up, w_down):
    n_blocks = lay.n_blocks
    d_ff = w_gate.shape[-1]
    assert d_ff == D_MODEL
    used = lambda j, be: jnp.minimum(j, be[n_blocks] - 1)
    any_spec = pl.BlockSpec(memory_space=pl.ANY)
    grid_spec = pltpu.PrefetchScalarGridSpec(
        num_scalar_prefetch=1,
        grid=(n_blocks,),
        in_specs=[pl.BlockSpec((BM, XS_W), lambda j, be: (used(j, be), 0)), any_spec, any_spec, any_spec],
        out_specs=pl.BlockSpec((BM, D_MODEL), lambda j, be: (used(j, be), 0)),
        scratch_shapes=[pltpu.VMEM((3, D_MODEL, d_ff), f32), pltpu.VMEM((3, D_MODEL, d_ff), bf16),
                        pltpu.SemaphoreType.DMA((3,))],
    )
    return pl.pallas_call(
        functools.partial(_expert_kernel, n_blocks=n_blocks, layer=layer),
        grid_spec=grid_spec,
        out_shape=jax.ShapeDtypeStruct((lay.cap, D_MODEL), f32),
        compiler_params=_cparams(1),
        name="moe_experts",
    )(block_e, xs, w_gate, w_up, w_down)


def _combine_kernel(tab_ref, ys_hbm, cols_ref, x_ref, gts_ref, gtt_ref, *rest, n_steps, np_steps, final):
    if final:
        gf_ref, yp_ref, ysm_ref, staged, sem = rest
    else:
        xo_ref, staged, sem = rest
    i = pl.program_id(0)
    slot = i % 2
    is_s = i >= np_steps

    def start_step(step, slot_):
        for c in range(STEP_TILES * N_CHUNKS):
            src = jnp.maximum(tab_ref[step * (STEP_TILES * N_CHUNKS) + c], 0)
            pltpu.make_async_copy(ys_hbm.at[pl.ds(pl.multiple_of(src, SUBLANES), SUBLANES)],
                                  staged.at[slot_, pl.ds(c * SUBLANES, SUBLANES)], sem.at[slot_]).start()

    @pl.when(i == 0)
    def _():
        start_step(0, 0)

    @pl.when(i + 1 < n_steps)
    def _():
        start_step(i + 1, 1 - slot)

    pltpu.make_async_copy(ys_hbm.at[pl.ds(0, STEP_TILES * SORT_ROWS)], staged.at[slot], sem.at[slot]).wait()

    def step(sample):
        lane = lax.broadcasted_iota(i32, (TM, SORT_ROWS), 1).astype(f32)
        for t in range(STEP_TILES):
            rs = slice(t * TM, (t + 1) * TM)
            unperm = jnp.where((lane == cols_ref[rs, 0:1]) | (lane == cols_ref[rs, 1:2]), 1.0, 0.0).astype(bf16)
            parts = _split3(staged[slot, t * SORT_ROWS:(t + 1) * SORT_ROWS, :])
            ff = sum(jnp.dot(unperm, part, preferred_element_type=f32) for part in parts)
            xn = x_ref[rs, :] + (gtt_ref[rs, :] if sample else gts_ref[...]) * ff
            if final:
                (ysm_ref if sample else yp_ref)[rs, :] = _rms(xn, gf_ref[...])
            else:
                xo_ref[rs, :] = xn

    pl.when(is_s)(functools.partial(step, True))
    pl.when(jnp.logical_not(is_s))(functools.partial(step, False))


def _combine(tl, lay, layer, tab, ys, cols, x, mod_seq, mod_tok, g_final):
    final = g_final is not None
    tok_spec = pl.BlockSpec((TS, D_MODEL), lambda i, t: (i, 0))
    gt_seq, gt_tok = _mod_specs(tl, layer, 5)
    in_specs = [pl.BlockSpec(memory_space=pl.ANY), pl.BlockSpec((TS, LANES), lambda i, t: (i, 0)),
                tok_spec, gt_seq, gt_tok]
    args = [tab, ys, cols, x, mod_seq, mod_tok]
    if final:
        in_specs.append(pl.BlockSpec((1, D_MODEL), lambda i, t: (0, 0)))
        args.append(g_final)
        out_specs = [pl.BlockSpec((TS, D_MODEL), lambda i, t: (tl.prompt_block(i), 0)),
                     pl.BlockSpec((TS, D_MODEL), lambda i, t: (tl.sample_block(i), 0))]
        out_shape = [jax.ShapeDtypeStruct((tl.n_prompt, D_MODEL), f32),
                     jax.ShapeDtypeStruct((tl.n_sample, D_MODEL), f32)]
    else:
        out_specs = tok_spec
        out_shape = jax.ShapeDtypeStruct((tl.n_tok, D_MODEL), f32)
    grid_spec = pltpu.PrefetchScalarGridSpec(
        num_scalar_prefetch=1,
        grid=(tl.n_steps,),
        in_specs=in_specs,
        out_specs=out_specs,
        scratch_shapes=[pltpu.VMEM((2, STEP_TILES * SORT_ROWS, D_MODEL), f32), pltpu.SemaphoreType.DMA((2,))],
    )
    return pl.pallas_call(
        functools.partial(_combine_kernel, n_steps=tl.n_steps, np_steps=tl.np_steps, final=final),
        grid_spec=grid_spec,
        out_shape=out_shape,
        compiler_params=_cparams(1),
        name="moe_combine",
    )(*args)


def _rope_tables(tl):
    half = HEAD_D // 2
    inv = ROPE_BASE ** (-jnp.arange(half, dtype=f32) / half)
    pos_p = jnp.arange(tl.tp, dtype=i32)
    pos_s = PAST_LEN + jnp.arange(tl.ts, dtype=i32)
    pos = jnp.concatenate([pos_p, jnp.tile(pos_s, tl.bs)])
    ang = pos.astype(f32)[:, None] * inv[None, :]
    cos, sin = jnp.cos(ang), jnp.sin(ang)
    return jnp.concatenate([cos, cos], axis=-1), jnp.concatenate([-sin, sin], axis=-1)


def kernel(x_prompt, x_sample, state_ret, state_conv, c_prompt, c_sample, w_mod, b_mod, g_mix_norm, w_in,
           w_conv, b_conv, g_conv_ln, b_conv_ln, g_ret_gn, w_out, g_ffn_norm, w_router, b_router,
           w_exp_gate, w_exp_up, w_exp_down, g_final):
    bp, tp, _ = x_prompt.shape
    bs, ts, _ = x_sample.shape
    depth = w_mod.shape[0]
    tl = _Tiles(bp, tp, bs, ts)
    lay = _Layout(tl.n_tiles)

    c_all = jnp.concatenate([c_prompt, jnp.repeat(c_sample, ts, axis=0)], axis=0)
    mod_seq, mod_tok = _modulation(c_all, bp, w_mod, b_mod)
    mod_seq = mod_seq.reshape(depth, bp, 1, N_MOD * D_MODEL)

    cos_tab, sin_tab = _rope_tables(tl)
    w_in_bf = w_in.astype(bf16)
    w_out_bf = w_out.astype(bf16)
    wr_pad = jnp.pad(w_router.astype(f32), ((0, 0), (0, LANES - N_EXPERTS)))
    wr_hi = wr_pad.astype(bf16)
    wr_lo = (wr_pad - wr_hi.astype(f32)).astype(bf16)
    br_pad = jnp.pad(b_router.astype(f32), (0, LANES - N_EXPERTS)).reshape(1, LANES)
    vec3 = lambda t: t.reshape(depth, 1, t.shape[-1])
    g_mix3, g_ffn3, gn3 = vec3(g_mix_norm), vec3(g_ffn_norm), vec3(g_ret_gn)
    b_conv3, g_ln3, b_ln3 = vec3(b_conv), vec3(g_conv_ln), vec3(b_conv_ln)

    x = (x_prompt.reshape(tl.n_prompt, D_MODEL), x_sample.reshape(tl.n_sample, D_MODEL))
    ret_p, conv_p, conv_s = [], [], []
    ret_s_all = None
    for layer in range(depth):
        q, k, v, gate, a = _inproj(tl, layer, x, mod_seq, mod_tok, g_mix3, w_in_bf, cos_tab, sin_tab)
        ro_p, s_p = _retention_prompt(tl, layer, q, k, v, gate, gn3)
        ro_s, ret_s_all = _retention_sample(tl, layer, q, k, v, gate, state_ret, gn3, ret_s_all)
        co_p, buf_p = _conv_prompt(tl, layer, a, w_conv, b_conv3, g_ln3, b_ln3)
        co_s, buf_s = _conv_sample(tl, layer, a, state_conv, w_conv, b_conv3, g_ln3, b_ln3)
        x_mid, h2, rows, cols, tile_counts = _outproj(
            tl, layer, x, ro_p, ro_s, co_p, co_s, mod_seq, mod_tok, g_ffn3, w_out_bf, wr_hi, wr_lo, br_pad)
        tab, block_e = _moe_tables(lay, tile_counts[:, :, 0])
        xs = _dispatch(tl, lay, tab, h2, rows, cols)
        ys = _experts(layer, lay, block_e, xs, w_exp_gate, w_exp_up, w_exp_down)
        last = layer == depth - 1
        x = _combine(tl, lay, layer, tab, ys, cols, x_mid, mod_seq, mod_tok,
                     g_final.reshape(1, D_MODEL) if last else None)
        ret_p.append(s_p)
        conv_p.append(buf_p)
        conv_s.append(buf_s)
    y_p, y_s = x
    return (y_p.reshape(bp, tp, D_MODEL), y_s.reshape(bs, ts, D_MODEL),
            jnp.stack(ret_p), jnp.stack(conv_p), ret_s_all, jnp.stack(conv_s))
```

```python
import functools

import jax
import jax.numpy as jnp
from jax import lax
from jax.experimental import pallas as pl
from jax.experimental.pallas import tpu as pltpu

f32 = jnp.float32
bf16 = jnp.bfloat16
i32 = jnp.int32

D_MODEL = 1024
D_RET = 512
D_CONV = 512
N_HEADS = 4
HEAD_D = 128
RET_CHUNK = 128
RET_CHUNKS_PER_STEP = 8
ROPE_BASE = 10000.0
CONV_WIDTH = 31
CONV_HALO = CONV_WIDTH - 1
N_EXPERTS = 16
N_GROUPS = 4
GROUP_SIZE = N_EXPERTS // N_GROUPS
N_MOD = 6
EPS = 1e-6
PAST_LEN = 16384
D_IN = 4 * D_RET + 2 * D_CONV

LANES = 128
SUBLANES = 8
TM = 256
STEP_TILES = 2
TS = TM * STEP_TILES
BM = 512
CONV_ROWS = 128
NORM_ROWS = 64
SAMPLE_GROUP = 16
HALO_PAD = 32
VMEM_LIMIT = 56 * 1024 * 1024


def _cparams(n_axes, vmem=VMEM_LIMIT):
    return pltpu.CompilerParams(dimension_semantics=("arbitrary",) * n_axes, vmem_limit_bytes=vmem)


def _mod_kernel(c_ref, w_ref, b_ref, seq_ref, tok_ref):
    cond = jax.nn.silu(c_ref[...]).astype(bf16)
    mod = jnp.dot(cond, w_ref[...].astype(bf16), preferred_element_type=f32) + b_ref[...]
    n_seq = seq_ref.shape[0]
    seq_ref[...] = mod[0:n_seq, :]
    tok_ref[...] = mod[n_seq:, :]


def _modulation(c_all, n_seq, w_mod, b_mod):
    depth = w_mod.shape[0]
    m = c_all.shape[0]
    n_tok = m - n_seq
    assert n_seq % SUBLANES == 0
    return pl.pallas_call(
        _mod_kernel,
        grid=(depth, N_MOD),
        in_specs=[
            pl.BlockSpec((m, D_MODEL), lambda l, j: (0, 0)),
            pl.BlockSpec((None, D_MODEL, D_MODEL), lambda l, j: (l, 0, j)),
            pl.BlockSpec((None, 1, D_MODEL), lambda l, j: (l, 0, j)),
        ],
        out_specs=[pl.BlockSpec((None, n_seq, D_MODEL), lambda l, j: (l, 0, j)),
                   pl.BlockSpec((None, n_tok, D_MODEL), lambda l, j: (l, 0, j))],
        out_shape=[jax.ShapeDtypeStruct((depth, n_seq, N_MOD * D_MODEL), f32),
                   jax.ShapeDtypeStruct((depth, n_tok, N_MOD * D_MODEL), f32)],
        compiler_params=_cparams(2),
        name="modulation",
    )(c_all, w_mod, b_mod.reshape(depth, 1, N_MOD * D_MODEL))


class _Tiles:
    def __init__(self, bp, tp, bs, ts):
        self.bp, self.tp, self.bs, self.ts = bp, tp, bs, ts
        self.n_prompt = bp * tp
        self.n_sample = bs * ts
        self.n_tok = self.n_prompt + self.n_sample
        assert tp % TS == 0 and self.n_sample % TS == 0
        self.tiles_per_seq = tp // TM
        self.np_tiles = self.n_prompt // TM
        self.n_tiles = self.n_tok // TM
        self.steps_per_seq = tp // TS
        self.np_steps = self.n_prompt // TS
        self.n_steps = self.n_tok // TS

    def prompt_block(self, i):
        return jnp.minimum(i, self.np_steps - 1)

    def sample_block(self, i):
        return jnp.maximum(i - self.np_steps, 0)

    def seq_index(self, i):
        return jnp.minimum(i // self.steps_per_seq, self.bp - 1)


def _mod_specs(tl, layer, col):
    seq = pl.BlockSpec((None, None, 1, D_MODEL), lambda i, *_: (layer, tl.seq_index(i), 0, col))
    tok = pl.BlockSpec((None, TS, D_MODEL), lambda i, *_: (layer, tl.sample_block(i), col))
    return seq, tok


def _rms(x, g):
    return x * lax.rsqrt(jnp.mean(x * x, axis=-1, keepdims=True) + EPS) * g


def _head_norm_gate(o, gn, gate):
    mu = jnp.mean(o, axis=-1, keepdims=True)
    var = jnp.mean(jnp.square(o - mu), axis=-1, keepdims=True)
    return jax.nn.silu(gate) * ((o - mu) * lax.rsqrt(var + EPS) * gn)


def _dot_nt(a, b):
    return lax.dot_general(a, b, (((1,), (1,)), ((), ())), preferred_element_type=f32)


def _dot_tn(a, b):
    return lax.dot_general(a, b, (((0,), (0,)), ((), ())), preferred_element_type=f32)


def _ret_prompt_kernel(q_ref, k_ref, v_ref, gate_ref, dec_ref, qd_ref, kd_ref, cd_ref, gn_ref,
                       o_ref, s_out_ref, s_ref):
    c = pl.program_id(1)

    @pl.when(c == 0)
    def _():
        s_ref[...] = jnp.zeros_like(s_ref)

    for ci in range(RET_CHUNKS_PER_STEP):
        rows = slice(ci * RET_CHUNK, (ci + 1) * RET_CHUNK)
        for hd in range(N_HEADS):
            sl = slice(hd * HEAD_D, (hd + 1) * HEAD_D)
            kh = k_ref[rows, sl]
            qb = q_ref[rows, sl]
            kb = kh.astype(bf16)
            vb = v_ref[rows, sl]
            s_old = s_ref[hd]
            scores = _dot_nt(qb, kb) * dec_ref[hd]
            inner = jnp.dot(scores.astype(bf16), vb, preferred_element_type=f32)
            cross = jnp.dot(qb, s_old.astype(bf16), preferred_element_type=f32) * qd_ref[hd]
            s_ref[hd] = s_old * cd_ref[hd] + _dot_tn((kh * kd_ref[hd]).astype(bf16), vb)
            o_ref[rows, sl] = _head_norm_gate(inner + cross, gn_ref[:, sl], gate_ref[rows, sl]).astype(bf16)

    @pl.when(c == pl.num_programs(1) - 1)
    def _():
        s_out_ref[...] = s_ref[...]


def _decay_tables(chunk, true_len):
    lg = jnp.log(1.0 - 2.0 ** (-5.0 - jnp.arange(N_HEADS, dtype=f32)))
    idx = jnp.arange(chunk, dtype=f32)
    rel = idx[:, None] - idx[None, :]
    decay = jnp.where(rel[None] >= 0, jnp.exp(jnp.maximum(rel, 0.0)[None] * lg[:, None, None]), 0.0)
    q_decay = jnp.exp((idx[None, :] + 1.0) * lg[:, None])
    k_decay = jnp.exp((true_len - 1.0 - idx[None, :]) * lg[:, None])
    c_decay = jnp.exp(true_len * lg)
    return decay, q_decay, k_decay, c_decay


def _retention_prompt(tl, layer, q, k, v, gate, g_ret_gn):
    step_rows = RET_CHUNK * RET_CHUNKS_PER_STEP
    assert tl.tp % step_rows == 0
    n_chunks = tl.tp // step_rows
    decay, q_decay, k_decay, c_decay = _decay_tables(RET_CHUNK, RET_CHUNK)
    bcast = lambda t: jnp.broadcast_to(t[:, :, None], (N_HEADS, RET_CHUNK, HEAD_D))
    cd = jnp.broadcast_to(c_decay[:, None, None], (N_HEADS, 1, HEAD_D))
    tok_spec = pl.BlockSpec((step_rows, D_RET), lambda b, c: (b * n_chunks + c, 0))
    tab_spec = pl.BlockSpec((N_HEADS, RET_CHUNK, HEAD_D), lambda b, c: (0, 0, 0))
    return pl.pallas_call(
        _ret_prompt_kernel,
        grid=(tl.bp, n_chunks),
        in_specs=[tok_spec] * 4 + [tab_spec] * 3 + [
            pl.BlockSpec((N_HEADS, 1, HEAD_D), lambda b, c: (0, 0, 0)),
            pl.BlockSpec((None, 1, D_RET), lambda b, c: (layer, 0, 0)),
        ],
        out_specs=[tok_spec, pl.BlockSpec((None, N_HEADS, HEAD_D, HEAD_D), lambda b, c: (b, 0, 0, 0))],
        out_shape=[jax.ShapeDtypeStruct((tl.n_prompt, D_RET), bf16),
                   jax.ShapeDtypeStruct((tl.bp, N_HEADS, HEAD_D, HEAD_D), f32)],
        scratch_shapes=[pltpu.VMEM((N_HEADS, HEAD_D, HEAD_D), f32)],
        compiler_params=_cparams(2),
        name="retention_prompt",
    )(q, k, v, gate, decay, bcast(q_decay), bcast(k_decay), cd, g_ret_gn)


def _ret_sample_kernel(q_ref, k_ref, v_ref, gate_ref, s_in_ref, dec_ref, qd_ref, kd_ref, cd_ref, gn_ref,
                       *rest, ts):
    o_ref, s_all_ref = rest[-2:]
    s_out_ref = s_all_ref.at[0]
    for other in range(1, s_all_ref.shape[0]):
        s_all_ref[other] = jnp.zeros(s_all_ref.shape[1:], f32)
    seqs_per_tile = SUBLANES // ts
    row = lax.broadcasted_iota(i32, (SUBLANES, HEAD_D), 0)
    q_all = q_ref[...].astype(f32)
    v_all = v_ref[...].astype(f32)
    outs = []
    for t in range(SAMPLE_GROUP // seqs_per_tile):
        rows = slice(t * SUBLANES, (t + 1) * SUBLANES)
        heads = []
        for hd in range(N_HEADS):
            sl = slice(hd * HEAD_D, (hd + 1) * HEAD_D)
            qh = q_all[rows, sl]
            kh = k_ref[rows, sl] * kd_ref[hd]
            vb = v_all[rows, sl].astype(bf16)
            qb = qh.astype(bf16)
            scores = _dot_nt(qb, k_ref[rows, sl].astype(bf16)) * dec_ref[hd]
            o = jnp.dot(scores.astype(bf16), vb, preferred_element_type=f32)
            for s in range(seqs_per_tile):
                b = t * seqs_per_tile + s
                mine = (row >= s * ts) & (row < (s + 1) * ts)
                s_old = s_in_ref[b, hd]
                q_s = jnp.where(mine, qh, 0.0).astype(bf16)
                k_s = jnp.where(mine, kh, 0.0).astype(bf16)
                o = o + jnp.dot(q_s, s_old.astype(bf16), preferred_element_type=f32) * qd_ref[hd]
                s_out_ref[b, hd] = s_old * cd_ref[hd] + _dot_tn(k_s, vb)
            heads.append(_head_norm_gate(o, gn_ref[:, sl], gate_ref[rows, sl]))
        outs.append(jnp.concatenate(heads, axis=-1))
    o_ref[...] = jnp.concatenate(outs, axis=0).astype(bf16)


def _retention_sample(tl, layer, q, k, v, gate, state_ret, g_ret_gn, prev_states):
    ts = tl.ts
    depth = state_ret.shape[0]
    assert SUBLANES % ts == 0 and tl.bs % SAMPLE_GROUP == 0
    seqs_per_tile = SUBLANES // ts
    decay, q_decay, k_decay, c_decay = _decay_tables(ts, ts)
    eye = jnp.eye(seqs_per_tile, dtype=f32)
    dec_tile = jnp.einsum("ab,hij->haibj", eye, decay).reshape(N_HEADS, SUBLANES, SUBLANES)
    tile_rows = lambda t: jnp.broadcast_to(jnp.tile(t, (1, seqs_per_tile))[:, :, None],
                                           (N_HEADS, SUBLANES, HEAD_D))
    cd = jnp.broadcast_to(c_decay[:, None, None], (N_HEADS, 1, HEAD_D))
    rows = SAMPLE_GROUP * ts
    first = tl.n_prompt // rows
    tok_spec = pl.BlockSpec((rows, D_RET), lambda i: (first + i, 0))
    const3 = lambda shape: pl.BlockSpec(shape, lambda i: (0, 0, 0))
    st_block = (SAMPLE_GROUP, N_HEADS, HEAD_D, HEAD_D)
    in_specs = [tok_spec] * 4 + [
        pl.BlockSpec((None,) + st_block, lambda i: (layer, i, 0, 0, 0)),
        const3((N_HEADS, SUBLANES, SUBLANES)),
        const3((N_HEADS, SUBLANES, HEAD_D)),
        const3((N_HEADS, SUBLANES, HEAD_D)),
        const3((N_HEADS, 1, HEAD_D)),
        pl.BlockSpec((None, 1, D_RET), lambda i: (layer, 0, 0)),
    ]
    args = [q, k, v, gate, state_ret, dec_tile, tile_rows(q_decay), tile_rows(k_decay), cd, g_ret_gn]
    if prev_states is None:
        state_spec = pl.BlockSpec((depth,) + st_block, lambda i: (0, i, 0, 0, 0))
        aliases = {}
    else:
        state_spec = pl.BlockSpec((1,) + st_block, lambda i: (layer, i, 0, 0, 0))
        in_specs.append(pl.BlockSpec(memory_space=pl.ANY))
        args.append(prev_states)
        aliases = {len(args) - 1: 1}
    return pl.pallas_call(
        functools.partial(_ret_sample_kernel, ts=ts),
        grid=(tl.bs // SAMPLE_GROUP,),
        in_specs=in_specs,
        out_specs=[pl.BlockSpec((rows, D_RET), lambda i: (i, 0)), state_spec],
        out_shape=[jax.ShapeDtypeStruct((tl.n_sample, D_RET), bf16),
                   jax.ShapeDtypeStruct((depth, tl.bs, N_HEADS, HEAD_D, HEAD_D), f32)],
        input_output_aliases=aliases,
        compiler_params=_cparams(1),
        name="retention_sample",
    )(*args)


def _ln_silu(cv, g, b):
    mu = jnp.mean(cv, axis=-1, keepdims=True)
    var = jnp.mean(jnp.square(cv - mu), axis=-1, keepdims=True)
    return jax.nn.silu((cv - mu) * lax.rsqrt(var + EPS) * g + b)


def _conv_taps(window, w_ref, b_ref, n_rows):
    cols = []
    for col in range(D_CONV // LANES):
        lanes = slice(col * LANES, (col + 1) * LANES)
        acc = jnp.broadcast_to(b_ref[:, lanes], (n_rows, LANES))
        for tap in range(CONV_WIDTH):
            acc = acc + window(col, tap) * w_ref[tap:tap + 1, lanes]
        cols.append(acc)
    return jnp.concatenate(cols, axis=-1)


def _conv_sample_kernel(a_ref, st_ref, w_ref, b_ref, g_ref, bl_ref, *rest, ts):
    o_ref, buf_all_ref, full_ref, cv_ref = rest[-4:]
    buf_ref = buf_all_ref.at[0]
    for other in range(1, buf_all_ref.shape[0]):
        buf_all_ref[other] = jnp.zeros(buf_all_ref.shape[1:], f32)
    for s in range(SAMPLE_GROUP):
        for col in range(D_CONV // LANES):
            lanes = slice(col * LANES, (col + 1) * LANES)
            full_ref[col, 0:CONV_HALO, :] = st_ref[s, :, lanes]
            full_ref[col, CONV_HALO:CONV_HALO + ts, :] = a_ref[s * ts:(s + 1) * ts, lanes]
            buf_ref[s, :, lanes] = full_ref[col, ts:ts + CONV_HALO, :]
        cv_ref[s * ts:(s + 1) * ts, :] = _conv_taps(lambda col, tap: full_ref[col, tap:tap + ts, :],
                                                    w_ref, b_ref, ts)
    o_ref[...] = _ln_silu(cv_ref[...], g_ref[...], bl_ref[...]).astype(bf16)


def _conv_sample(tl, layer, a, state_conv, w_conv, b_conv, g_ln, b_ln, prev_bufs):
    ts = tl.ts
    depth = state_conv.shape[0]
    rows = SAMPLE_GROUP * ts
    first_block = tl.n_prompt // rows
    vec = pl.BlockSpec((None, 1, D_CONV), lambda i: (layer, 0, 0))
    in_specs = [pl.BlockSpec((rows, D_CONV), lambda i: (first_block + i, 0)),
                pl.BlockSpec((None, SAMPLE_GROUP, CONV_HALO, D_CONV), lambda i: (layer, i, 0, 0)),
                pl.BlockSpec((None, CONV_WIDTH, D_CONV), lambda i: (layer, 0, 0)),
                vec, vec, vec]
    args = [a, state_conv, w_conv, b_conv, g_ln, b_ln]
    buf_block = (SAMPLE_GROUP, CONV_HALO, D_CONV)
    if prev_bufs is None:
        buf_spec = pl.BlockSpec((depth,) + buf_block, lambda i: (0, i, 0, 0))
        aliases = {}
    else:
        buf_spec = pl.BlockSpec((1,) + buf_block, lambda i: (layer, i, 0, 0))
        in_specs.append(pl.BlockSpec(memory_space=pl.ANY))
        args.append(prev_bufs)
        aliases = {len(args) - 1: 1}
    return pl.pallas_call(
        functools.partial(_conv_sample_kernel, ts=ts),
        grid=(tl.bs // SAMPLE_GROUP,),
        in_specs=in_specs,
        out_specs=[pl.BlockSpec((rows, D_CONV), lambda i: (i, 0)), buf_spec],
        out_shape=[jax.ShapeDtypeStruct((tl.n_sample, D_CONV), bf16),
                   jax.ShapeDtypeStruct((depth, tl.bs, CONV_HALO, D_CONV), f32)],
        input_output_aliases=aliases,
        scratch_shapes=[pltpu.VMEM((D_CONV // LANES, CONV_HALO + ts + SUBLANES, LANES), f32),
                        pltpu.VMEM((rows, D_CONV), f32)],
        compiler_params=_cparams(1),
        name="conv_sample",
    )(*args)


N_CONV_COLS = D_CONV // LANES
N_CONV_CHUNKS = TM // CONV_ROWS


def _inproj_kernel(*refs, np_steps, split_x):
    if split_x:
        xp_ref, xs_ref = refs[:2]
        refs = refs[2:]
    else:
        x_ref = refs[0]
        refs = refs[1:]
    (shs_ref, sht_ref, scs_ref, sct_ref, g_ref, w_ref, cos_ref, sin_ref,
     q_ref, k_ref, v_ref, gate_ref, a_ref) = refs
    is_s = pl.program_id(0) >= np_steps

    def step(sample):
        for t in range(STEP_TILES):
            rs = slice(t * TM, (t + 1) * TM)
            if split_x:
                x = xs_ref[rs, :] if sample else xp_ref[rs, :]
            else:
                x = x_ref[rs, :]
            sh, sc = (sht_ref[rs, :], sct_ref[rs, :]) if sample else (shs_ref[...], scs_ref[...])
            hb = (_rms(x, g_ref[...]) * (1.0 + sc) + sh).astype(bf16)

            def group(g):
                return jnp.dot(hb, w_ref[:, g * D_RET:(g + 1) * D_RET], preferred_element_type=f32)

            cos = cos_ref[rs, :]
            sin = sin_ref[rs, :]

            def rope(th):
                return th * cos + pltpu.roll(th, HEAD_D // 2, 1) * sin

            qg, kg = group(0), group(1)
            for hd in range(N_HEADS):
                sl = slice(hd * HEAD_D, (hd + 1) * HEAD_D)
                q_ref[rs, sl] = rope(qg[:, sl]).astype(bf16)
                k_ref[rs, sl] = rope(kg[:, sl]) * (HEAD_D ** -0.5)
            v_ref[rs, :] = group(2).astype(bf16)
            gate_ref[rs, :] = group(3)
            a_ref[rs, :] = group(4) * jax.nn.sigmoid(group(5))

    pl.when(is_s)(functools.partial(step, True))
    pl.when(jnp.logical_not(is_s))(functools.partial(step, False))


def _conv_prompt_kernel(a_ref, w_ref, b_ref, g_ref, bl_ref, o_ref, buf_ref, full_ref, cv_ref):
    j = pl.program_id(1)

    @pl.when(j == 0)
    def _():
        full_ref[:, 0:HALO_PAD, :] = jnp.zeros((N_CONV_COLS, HALO_PAD, LANES), f32)

    @pl.when(j > 0)
    def _():
        full_ref[:, 0:HALO_PAD, :] = full_ref[:, TM:TM + HALO_PAD, :]

    for col in range(N_CONV_COLS):
        full_ref[col, HALO_PAD:HALO_PAD + TM, :] = a_ref[:, col * LANES:(col + 1) * LANES]
    shift = HALO_PAD - CONV_HALO

    def taps(idx, carry):
        col = idx // N_CONV_CHUNKS
        r0 = pl.multiple_of((idx % N_CONV_CHUNKS) * CONV_ROWS, CONV_ROWS)
        acc = jnp.broadcast_to(b_ref[col], (CONV_ROWS, LANES))
        for tap in range(CONV_WIDTH):
            acc = acc + full_ref[col, pl.ds(r0 + (tap + shift), CONV_ROWS), :] * w_ref[col, tap:tap + 1, :]
        cv_ref[col, pl.ds(r0, CONV_ROWS), :] = acc
        return carry

    lax.fori_loop(0, N_CONV_COLS * N_CONV_CHUNKS, taps, 0)

    for r0 in range(0, TM, NORM_ROWS):
        cv = jnp.concatenate([cv_ref[col, r0:r0 + NORM_ROWS, :] for col in range(N_CONV_COLS)], axis=-1)
        o_ref[r0:r0 + NORM_ROWS, :] = _ln_silu(cv, g_ref[...], bl_ref[...]).astype(bf16)

    @pl.when(j == pl.num_programs(1) - 1)
    def _():
        buf_ref[...] = a_ref[TM - CONV_HALO:TM, :]


def _conv_prompt(tl, layer, a, w_conv, b_conv, g_ln, b_ln):
    tps = tl.tiles_per_seq
    depth = w_conv.shape[0]
    w_cols = w_conv.reshape(depth, CONV_WIDTH, N_CONV_COLS, LANES).transpose(0, 2, 1, 3)
    b_cols = b_conv.reshape(depth, N_CONV_COLS, 1, LANES)
    vec = pl.BlockSpec((None, 1, D_CONV), lambda b, j: (layer, 0, 0))
    return pl.pallas_call(
        _conv_prompt_kernel,
        grid=(tl.bp, tps),
        in_specs=[pl.BlockSpec((TM, D_CONV), lambda b, j: (b * tps + j, 0)),
                  pl.BlockSpec((None, N_CONV_COLS, CONV_WIDTH, LANES), lambda b, j: (layer, 0, 0, 0)),
                  pl.BlockSpec((None, N_CONV_COLS, 1, LANES), lambda b, j: (layer, 0, 0, 0)),
                  vec, vec],
        out_specs=[pl.BlockSpec((TM, D_CONV), lambda b, j: (b * tps + j, 0)),
                   pl.BlockSpec((None, CONV_HALO, D_CONV), lambda b, j: (b, 0, 0))],
        out_shape=[jax.ShapeDtypeStruct((tl.n_prompt, D_CONV), bf16),
                   jax.ShapeDtypeStruct((tl.bp, CONV_HALO, D_CONV), f32)],
        scratch_shapes=[pltpu.VMEM((N_CONV_COLS, HALO_PAD + TM, LANES), f32),
                        pltpu.VMEM((N_CONV_COLS, TM, LANES), f32)],
        compiler_params=_cparams(2),
        name="conv_prompt",
    )(a, w_cols, b_cols, g_ln, b_ln)


def _inproj(tl, layer, x, mod_seq, mod_tok, g_norm, w_in_bf, cos_tab, sin_tab):
    split_x = isinstance(x, tuple)
    tok_spec = pl.BlockSpec((TS, D_MODEL), lambda i: (i, 0))
    if split_x:
        x_args = list(x)
        x_specs = [pl.BlockSpec((TS, D_MODEL), lambda i: (tl.prompt_block(i), 0)),
                   pl.BlockSpec((TS, D_MODEL), lambda i: (tl.sample_block(i), 0))]
    else:
        x_args, x_specs = [x], [tok_spec]
    sh_seq, sh_tok = _mod_specs(tl, layer, 0)
    sc_seq, sc_tok = _mod_specs(tl, layer, 1)

    def table_block(i):
        return jnp.where(i < tl.np_steps, i % tl.steps_per_seq, tl.steps_per_seq + tl.sample_block(i))

    tab_spec = pl.BlockSpec((TS, HEAD_D), lambda i: (table_block(i), 0))
    row_spec = pl.BlockSpec((TS, D_RET), lambda i: (i, 0))
    row_sd = lambda dt: jax.ShapeDtypeStruct((tl.n_tok, D_RET), dt)
    return pl.pallas_call(
        functools.partial(_inproj_kernel, np_steps=tl.np_steps, split_x=split_x),
        grid=(tl.n_steps,),
        in_specs=x_specs + [
            sh_seq, sh_tok, sc_seq, sc_tok,
            pl.BlockSpec((None, 1, D_MODEL), lambda i: (layer, 0, 0)),
            pl.BlockSpec((None, D_MODEL, D_IN), lambda i: (layer, 0, 0)),
            tab_spec, tab_spec,
        ],
        out_specs=[row_spec] * 5,
        out_shape=[row_sd(bf16), row_sd(f32), row_sd(bf16), row_sd(f32), row_sd(f32)],
        compiler_params=_cparams(1),
        name="inproj",
    )(*x_args, mod_seq, mod_tok, mod_seq, mod_tok, g_norm, w_in_bf, cos_tab, sin_tab)


SORT_ROWS = 2 * TM + N_EXPERTS * SUBLANES
XS_HALF = D_MODEL // 2
XS_W = XS_HALF + LANES
u32 = jnp.uint32


def _pack_bf16_pair(x):
    lo = lax.shift_right_logical(lax.bitcast_convert_type(x[:, 0:XS_HALF], u32), u32(16))
    hi = lax.bitcast_convert_type(x[:, XS_HALF:D_MODEL], u32) & u32(0xFFFF0000)
    return hi | lo


def _unpack_bf16_pair(words):
    lo = lax.bitcast_convert_type(lax.shift_left(words, u32(16)), f32).astype(bf16)
    hi = lax.bitcast_convert_type(words & u32(0xFFFF0000), f32).astype(bf16)
    return lo, hi


def _split3(x):
    a = x.astype(bf16)
    r = x - a.astype(f32)
    b = r.astype(bf16)
    c = (r - b.astype(f32)).astype(bf16)
    return a, b, c


def _first_of4(vals, m):
    return jnp.where(vals[0] == m, 0.0, jnp.where(vals[1] == m, 1.0, jnp.where(vals[2] == m, 2.0, 3.0)))


def _rows_to_tile(rows, n_rows):
    sub = lax.broadcasted_iota(i32, (n_rows, TM), 0)
    out = jnp.zeros((n_rows, TM), f32)
    for r, val in enumerate(rows):
        out = jnp.where(sub == r, val, out)
    return out


def _outproj_kernel(*refs, np_steps, split_x):
    if split_x:
        xp_ref, xs_ref = refs[:2]
        refs = refs[2:]
    else:
        x_ref = refs[0]
        refs = refs[1:]
    (retp_ref, rets_ref, cvp_ref, cvs_ref, gts_ref, gtt_ref, shs_ref, sht_ref, scs_ref, sct_ref,
     g_ref, wo_ref, wrh_ref, wrl_ref, br_ref,
     xo_ref, h2_ref, rows_ref, cols_ref, cnt_ref) = refs
    is_s = pl.program_id(0) >= np_steps

    def step(sample):
        for t in range(STEP_TILES):
            rs = slice(t * TM, (t + 1) * TM)
            if split_x:
                x = xs_ref[rs, :] if sample else xp_ref[rs, :]
            else:
                x = x_ref[rs, :]
            ret = rets_ref[rs, :] if sample else retp_ref[rs, :]
            cv = cvs_ref[rs, :] if sample else cvp_ref[rs, :]
            gt, sh, sc = ((gtt_ref[rs, :], sht_ref[rs, :], sct_ref[rs, :]) if sample
                          else (gts_ref[...], shs_ref[...], scs_ref[...]))
            mix_out = (jnp.dot(ret, wo_ref[0:D_RET, :], preferred_element_type=f32)
                       + jnp.dot(cv, wo_ref[D_RET:D_RET + D_CONV, :], preferred_element_type=f32))
            xn = x + gt * mix_out
            xo_ref[rs, :] = xn
            h2 = _rms(xn, g_ref[...]) * (1.0 + sc) + sh
            h_hi = h2.astype(bf16)
            h2_ref[rs, :] = h_hi
            rows, cols, cnt = _route_tile(h2, h_hi, wrh_ref, wrl_ref, br_ref)
            rows_ref[t] = rows
            cols_ref[rs, :] = cols
            cnt_ref[t] = cnt

    pl.when(is_s)(functools.partial(step, True))
    pl.when(jnp.logical_not(is_s))(functools.partial(step, False))


def _route_tile(h2, h_hi, wrh_ref, wrl_ref, br_ref):
    h_lo = (h2 - h_hi.astype(f32)).astype(bf16)
    logits = (jnp.dot(h_hi, wrh_ref[...], preferred_element_type=f32)
              + jnp.dot(h_hi, wrl_ref[...], preferred_element_type=f32)
              + jnp.dot(h_lo, wrh_ref[...], preferred_element_type=f32)) + br_ref[...]
    lt = logits.T
    row = [lt[e:e + 1, :] for e in range(N_EXPERTS)]
    top = functools.reduce(jnp.maximum, row)
    ex = [jnp.exp(r - top) for r in row]
    den = functools.reduce(jnp.add, ex)
    p = [v / den for v in ex]

    best = None
    for g in range(N_GROUPS):
        a = p[g * GROUP_SIZE:(g + 1) * GROUP_SIZE]
        m1 = functools.reduce(jnp.maximum, a)
        i1 = _first_of4(a, m1)
        b = [jnp.where(i1 == float(j), -1.0, a[j]) for j in range(GROUP_SIZE)]
        m2 = functools.reduce(jnp.maximum, b)
        i2 = _first_of4(b, m2)
        cand = (m1 + m2, m1, m2, i1 + float(g * GROUP_SIZE), i2 + float(g * GROUP_SIZE))
        if best is None:
            best = cand
        else:
            take = cand[0] > best[0]
            best = tuple(jnp.where(take, c, o) for c, o in zip(cand, best))
    _, m1, m2, e0, e1 = best
    denom = m1 + m2
    w0 = m1 / denom
    w1 = m2 / denom

    ex_id = lax.broadcasted_iota(i32, (N_EXPERTS, TM), 0).astype(f32)
    sel0 = ex_id == e0
    sel1 = ex_id == e1
    ind = jnp.where(sel0 | sel1, 1.0, 0.0)
    t_r = lax.broadcasted_iota(i32, (TM, TM), 0)
    t_c = lax.broadcasted_iota(i32, (TM, TM), 1)
    earlier = jnp.where(t_r < t_c, 1.0, 0.0).astype(bf16)
    prefix = jnp.dot(ind.astype(bf16), earlier, preferred_element_type=f32)
    cnt = jnp.sum(ind, axis=-1, keepdims=True)
    cnt8 = jnp.floor((cnt + float(SUBLANES - 1)) * (1.0 / SUBLANES)) * float(SUBLANES)
    e_r = lax.broadcasted_iota(i32, (N_EXPERTS, N_EXPERTS), 0)
    e_c = lax.broadcasted_iota(i32, (N_EXPERTS, N_EXPERTS), 1)
    below = jnp.where(e_c < e_r, 1.0, 0.0).astype(bf16)
    seg_off = jnp.dot(below, jnp.broadcast_to(cnt8, (N_EXPERTS, TM)).astype(bf16),
                      preferred_element_type=f32)
    where_to = seg_off + prefix
    pos0 = jnp.sum(jnp.where(sel0, where_to, 0.0), axis=0, keepdims=True)
    pos1 = jnp.sum(jnp.where(sel1, where_to, 0.0), axis=0, keepdims=True)

    w0p = [v.astype(f32) for v in _split3(w0)]
    w1p = [v.astype(f32) for v in _split3(w1)]
    info = [pos0, pos1] + w0p + w1p
    return (_rows_to_tile(info, SUBLANES), _rows_to_tile(info, LANES).T,
            jnp.broadcast_to(cnt, (N_EXPERTS, LANES)))


def _outproj(tl, layer, x, ret_p, ret_s, cv_p, cv_s, mod_seq, mod_tok, g_norm, w_out_bf, wr_hi, wr_lo, br_pad):
    split_x = isinstance(x, tuple)
    tok_spec = pl.BlockSpec((TS, D_MODEL), lambda i: (i, 0))
    p_spec = lambda w: pl.BlockSpec((TS, w), lambda i: (tl.prompt_block(i), 0))
    s_spec = lambda w: pl.BlockSpec((TS, w), lambda i: (tl.sample_block(i), 0))
    if split_x:
        x_args, x_specs = list(x), [p_spec(D_MODEL), s_spec(D_MODEL)]
    else:
        x_args, x_specs = [x], [tok_spec]
    mods = []
    for col in (2, 3, 4):
        mods += list(_mod_specs(tl, layer, col))
    wr_spec = pl.BlockSpec((D_MODEL, LANES), lambda i: (0, 0))
    return pl.pallas_call(
        functools.partial(_outproj_kernel, np_steps=tl.np_steps, split_x=split_x),
        grid=(tl.n_steps,),
        in_specs=x_specs + [p_spec(D_RET), s_spec(D_RET), p_spec(D_CONV), s_spec(D_CONV)] + mods + [
            pl.BlockSpec((None, 1, D_MODEL), lambda i: (layer, 0, 0)),
            pl.BlockSpec((None, D_MODEL, D_MODEL), lambda i: (layer, 0, 0)),
            wr_spec, wr_spec,
            pl.BlockSpec((1, LANES), lambda i: (0, 0)),
        ],
        out_specs=[tok_spec, tok_spec,
                   pl.BlockSpec((STEP_TILES, SUBLANES, TM), lambda i: (i, 0, 0)),
                   pl.BlockSpec((TS, LANES), lambda i: (i, 0)),
                   pl.BlockSpec((STEP_TILES, N_EXPERTS, LANES), lambda i: (i, 0, 0))],
        out_shape=[jax.ShapeDtypeStruct((tl.n_tok, D_MODEL), f32),
                   jax.ShapeDtypeStruct((tl.n_tok, D_MODEL), bf16),
                   jax.ShapeDtypeStruct((tl.n_tiles, SUBLANES, TM), f32),
                   jax.ShapeDtypeStruct((tl.n_tok, LANES), f32),
                   jax.ShapeDtypeStruct((tl.n_tiles, N_EXPERTS, LANES), f32)],
        compiler_params=_cparams(1),
        name="outproj_router",
    )(*x_args, ret_p, ret_s, cv_p, cv_s, *([mod_seq, mod_tok] * 3), g_norm, w_out_bf, wr_hi, wr_lo, br_pad)


N_CHUNKS = SORT_ROWS // SUBLANES


class _Layout:
    def __init__(self, n_tiles):
        self.n_tiles = n_tiles
        self.tail_start = n_tiles * N_CHUNKS
        self.tail_n8 = self.tail_start + N_EXPERTS
        worst = 2 * n_tiles * TM + n_tiles * N_EXPERTS * (SUBLANES - 1) + N_EXPERTS * (BM - SUBLANES)
        self.n_blocks = -(-worst // BM)
        self.cap = self.n_blocks * BM
        self.dump = self.cap
        self.xs_rows = self.cap + -(-2 * STEP_TILES * SORT_ROWS // BM) * BM


def _moe_tables(lay, tile_counts):
    c8 = ((tile_counts.astype(i32) + SUBLANES - 1) // SUBLANES) * SUBLANES
    base8 = jnp.cumsum(c8, axis=0) - c8
    tot8 = jnp.sum(c8, axis=0)
    region = ((tot8 + BM - 1) // BM) * BM
    g_end = jnp.cumsum(region)
    g_start = g_end - region
    seg_end = jnp.cumsum(c8, axis=1)
    seg_dst = g_start[None, :] + base8
    n_used = g_end[-1] // BM
    blk = jnp.arange(lay.n_blocks, dtype=i32)
    block_e = jnp.minimum(jnp.sum((g_end[None, :] <= blk[:, None] * BM).astype(i32), axis=1), N_EXPERTS - 1)
    block_e = jnp.where(blk < n_used, block_e, block_e[n_used - 1])
    row0 = jnp.arange(N_CHUNKS, dtype=i32) * SUBLANES
    owner = jnp.sum((seg_end[:, None, :] <= row0[None, :, None]).astype(i32), axis=-1)
    onehot = (owner[:, :, None] == jnp.arange(N_EXPERTS, dtype=i32)[None, None, :]).astype(i32)
    delta = seg_dst - (seg_end - c8)
    chunk_dst = jnp.where(owner < N_EXPERTS, row0[None, :] + jnp.sum(onehot * delta[:, None, :], axis=-1), -1)
    tab = jnp.concatenate([chunk_dst.ravel(), g_start + tot8, (region - tot8) // SUBLANES]).astype(i32)
    ids = jnp.arange(N_EXPERTS, dtype=i32)
    later = jnp.where((ids[None, :] > ids[:, None]) & (region[None, :] > 0), ids[None, :], N_EXPERTS)
    next_e = jnp.min(later, axis=1)
    next_e = jnp.where(next_e == N_EXPERTS, -1, next_e)
    owner_end = jnp.sum((block_e[:, None] == ids[None, :]).astype(i32) * (g_start + tot8)[None, :], axis=1)
    rows_used = jnp.clip(owner_end - blk * BM, 0, BM)
    return tab, jnp.concatenate([block_e, n_used[None], next_e, rows_used]).astype(i32)


def _for_chunks(n, fn):
    def body(c, carry):
        fn(c)
        return carry

    lax.fori_loop(0, n, body, 0)


def _dispatch_kernel(tab_ref, h2_ref, rows_ref, cols_ref, xs_hbm, sorted_ref, zero_ref, sem, *, lay, n_steps):
    i = pl.program_id(0)
    slot = i % 2
    step_rows = STEP_TILES * SORT_ROWS

    def tail_copy(dst):
        return pltpu.make_async_copy(zero_ref, xs_hbm.at[pl.ds(dst, SUBLANES)], sem.at[2])

    def wait_step(slot_):
        pltpu.make_async_copy(sorted_ref.at[slot_], xs_hbm.at[pl.ds(0, step_rows)], sem.at[slot_]).wait()

    @pl.when(i == 0)
    def _():
        zero_ref[...] = jnp.zeros_like(zero_ref)
        for e in range(N_EXPERTS):
            start = tab_ref[lay.tail_start + e]
            _for_chunks(tab_ref[lay.tail_n8 + e],
                        lambda c: tail_copy(pl.multiple_of(start + c * SUBLANES, SUBLANES)).start())

    @pl.when(i >= 2)
    def _():
        wait_step(slot)

    r_id = lax.broadcasted_iota(i32, (SORT_ROWS, TM), 0).astype(f32)
    lane = lax.broadcasted_iota(i32, (TM, LANES), 1)
    for t in range(STEP_TILES):
        pos0 = rows_ref[t, 0:1, :]
        pos1 = rows_ref[t, 1:2, :]
        p0 = r_id == pos0
        p1 = r_id == pos1
        perm = jnp.where(p0 | p1, 1.0, 0.0).astype(bf16)
        cols = cols_ref[t * TM:(t + 1) * TM, :]
        wpart0 = jnp.where((lane >= 2) & (lane < 5), cols, 0.0).astype(bf16)
        wpart1 = jnp.where((lane >= 5) & (lane < 8), cols, 0.0).astype(bf16)
        sw = (jnp.dot(jnp.where(p0, 1.0, 0.0).astype(bf16), wpart0, preferred_element_type=f32)
              + jnp.dot(jnp.where(p1, 1.0, 0.0).astype(bf16), wpart1, preferred_element_type=f32))
        base = t * SORT_ROWS
        sorted_ref[slot, base:base + SORT_ROWS, 0:XS_HALF] = _pack_bf16_pair(jnp.dot(
            perm, h2_ref[t * TM:(t + 1) * TM, :], preferred_element_type=f32))
        sorted_ref[slot, base:base + SORT_ROWS, XS_HALF:XS_W] = lax.bitcast_convert_type(jnp.broadcast_to(
            jnp.sum(sw, axis=-1, keepdims=True), (SORT_ROWS, LANES)), u32)

    for t in range(STEP_TILES):
        for c in range(N_CHUNKS):
            row = t * SORT_ROWS + c * SUBLANES
            dst = tab_ref[(i * STEP_TILES + t) * N_CHUNKS + c]
            dst = jnp.where(dst < 0, lay.dump + slot * step_rows + row, dst)
            pltpu.make_async_copy(sorted_ref.at[slot, pl.ds(row, SUBLANES)],
                                  xs_hbm.at[pl.ds(pl.multiple_of(dst, SUBLANES), SUBLANES)], sem.at[slot]).start()

    @pl.when(i == n_steps - 1)
    def _():
        if n_steps >= 2:
            wait_step(1 - slot)
        wait_step(slot)
        for e in range(N_EXPERTS):
            _for_chunks(tab_ref[lay.tail_n8 + e], lambda c: tail_copy(0).wait())


def _dispatch(tl, lay, tab, h2, rows, cols):
    grid_spec = pltpu.PrefetchScalarGridSpec(
        num_scalar_prefetch=1,
        grid=(tl.n_steps,),
        in_specs=[pl.BlockSpec((TS, D_MODEL), lambda i, t: (i, 0)),
                  pl.BlockSpec((STEP_TILES, SUBLANES, TM), lambda i, t: (i, 0, 0)),
                  pl.BlockSpec((TS, LANES), lambda i, t: (i, 0))],
        out_specs=pl.BlockSpec(memory_space=pl.ANY),
        scratch_shapes=[pltpu.VMEM((2, STEP_TILES * SORT_ROWS, XS_W), u32), pltpu.VMEM((SUBLANES, XS_W), u32),
                        pltpu.SemaphoreType.DMA((3,))],
    )
    return pl.pallas_call(
        functools.partial(_dispatch_kernel, lay=lay, n_steps=tl.n_steps),
        grid_spec=grid_spec,
        out_shape=jax.ShapeDtypeStruct((lay.xs_rows, XS_W), u32),
        compiler_params=_cparams(1),
        name="moe_dispatch",
    )(tab, h2, rows, cols)


def _expert_kernel(be_ref, xs_ref, wg_hbm, wu_hbm, wd_hbm, ys_ref, stage, w_bf, sem, *, n_blocks, layer):
    j = pl.program_id(0)

    def fetch(e):
        return [pltpu.make_async_copy(w.at[layer, e], stage.at[k], sem.at[k])
                for k, w in enumerate((wg_hbm, wu_hbm, wd_hbm))]

    @pl.when(j < be_ref[n_blocks])
    def _():
        e = be_ref[j]

        @pl.when(j == 0)
        def _():
            for copy in fetch(e):
                copy.start()

        @pl.when((j == 0) | (e != be_ref[jnp.maximum(j - 1, 0)]))
        def _():
            for copy in fetch(e):
                copy.wait()
            for k in range(3):
                w_bf[k] = stage[k].astype(bf16)
            nxt = be_ref[n_blocks + 1 + e]

            @pl.when(nxt >= 0)
            def _():
                for copy in fetch(nxt):
                    copy.start()

        def ffn(rows):
            x_lo, x_hi = _unpack_bf16_pair(xs_ref[rows, 0:XS_HALF])

            def first_layer(k):
                return (jnp.dot(x_lo, w_bf[k, 0:XS_HALF, :], preferred_element_type=f32)
                        + jnp.dot(x_hi, w_bf[k, XS_HALF:D_MODEL, :], preferred_element_type=f32))

            mid = (jax.nn.silu(first_layer(0)) * first_layer(1)).astype(bf16)
            slot_w = lax.bitcast_convert_type(xs_ref[rows, XS_HALF:XS_HALF + 1], f32)
            ys_ref[rows, :] = jnp.dot(mid, w_bf[2], preferred_element_type=f32) * slot_w

        half = BM // 2
        rows_used = be_ref[n_blocks + 1 + N_EXPERTS + j]

        @pl.when(rows_used > half)
        def _():
            ffn(slice(0, BM))

        @pl.when(rows_used <= half)
        def _():
            ffn(slice(0, half))
            ys_ref[half:BM, :] = jnp.zeros((BM - half, D_MODEL), f32)


def _experts(layer, lay, block_e, xs, w_gate, w_up, w_down):
    n_blocks = lay.n_blocks
    d_ff = w_gate.shape[-1]
    assert d_ff == D_MODEL
    used = lambda j, be: jnp.minimum(j, be[n_blocks] - 1)
    any_spec = pl.BlockSpec(memory_space=pl.ANY)
    grid_spec = pltpu.PrefetchScalarGridSpec(
        num_scalar_prefetch=1,
        grid=(n_blocks,),
        in_specs=[pl.BlockSpec((BM, XS_W), lambda j, be: (used(j, be), 0)), any_spec, any_spec, any_spec],
        out_specs=pl.BlockSpec((BM, D_MODEL), lambda j, be: (used(j, be), 0)),
        scratch_shapes=[pltpu.VMEM((3, D_MODEL, d_ff), f32), pltpu.VMEM((3, D_MODEL, d_ff), bf16),
                        pltpu.SemaphoreType.DMA((3,))],
    )
    return pl.pallas_call(
        functools.partial(_expert_kernel, n_blocks=n_blocks, layer=layer),
        grid_spec=grid_spec,
        out_shape=jax.ShapeDtypeStruct((lay.cap, D_MODEL), f32),
        compiler_params=_cparams(1),
        name="moe_experts",
    )(block_e, xs, w_gate, w_up, w_down)


def _combine_kernel(tab_ref, ys_hbm, cols_ref, x_ref, gts_ref, gtt_ref, *rest, n_steps, np_steps, final):
    if final:
        gf_ref, yp_ref, ysm_ref, staged, sem = rest
    else:
        xo_ref, staged, sem = rest
    i = pl.program_id(0)
    slot = i % 2
    is_s = i >= np_steps

    def start_step(step, slot_):
        for c in range(STEP_TILES * N_CHUNKS):
            src = jnp.maximum(tab_ref[step * (STEP_TILES * N_CHUNKS) + c], 0)
            pltpu.make_async_copy(ys_hbm.at[pl.ds(pl.multiple_of(src, SUBLANES), SUBLANES)],
                                  staged.at[slot_, pl.ds(c * SUBLANES, SUBLANES)], sem.at[slot_]).start()

    @pl.when(i == 0)
    def _():
        start_step(0, 0)

    @pl.when(i + 1 < n_steps)
    def _():
        start_step(i + 1, 1 - slot)

    pltpu.make_async_copy(ys_hbm.at[pl.ds(0, STEP_TILES * SORT_ROWS)], staged.at[slot], sem.at[slot]).wait()

    def step(sample):
        lane = lax.broadcasted_iota(i32, (TM, SORT_ROWS), 1).astype(f32)
        for t in range(STEP_TILES):
            rs = slice(t * TM, (t + 1) * TM)
            unperm = jnp.where((lane == cols_ref[rs, 0:1]) | (lane == cols_ref[rs, 1:2]), 1.0, 0.0).astype(bf16)
            parts = _split3(staged[slot, t * SORT_ROWS:(t + 1) * SORT_ROWS, :])
            ff = sum(jnp.dot(unperm, part, preferred_element_type=f32) for part in parts)
            xn = x_ref[rs, :] + (gtt_ref[rs, :] if sample else gts_ref[...]) * ff
            if final:
                (ysm_ref if sample else yp_ref)[rs, :] = _rms(xn, gf_ref[...])
            else:
                xo_ref[rs, :] = xn

    pl.when(is_s)(functools.partial(step, True))
    pl.when(jnp.logical_not(is_s))(functools.partial(step, False))


def _combine(tl, lay, layer, tab, ys, cols, x, mod_seq, mod_tok, g_final):
    final = g_final is not None
    tok_spec = pl.BlockSpec((TS, D_MODEL), lambda i, t: (i, 0))
    gt_seq, gt_tok = _mod_specs(tl, layer, 5)
    in_specs = [pl.BlockSpec(memory_space=pl.ANY), pl.BlockSpec((TS, LANES), lambda i, t: (i, 0)),
                tok_spec, gt_seq, gt_tok]
    args = [tab, ys, cols, x, mod_seq, mod_tok]
    if final:
        in_specs.append(pl.BlockSpec((1, D_MODEL), lambda i, t: (0, 0)))
        args.append(g_final)
        out_specs = [pl.BlockSpec((TS, D_MODEL), lambda i, t: (tl.prompt_block(i), 0)),
                     pl.BlockSpec((TS, D_MODEL), lambda i, t: (tl.sample_block(i), 0))]
        out_shape = [jax.ShapeDtypeStruct((tl.n_prompt, D_MODEL), f32),
                     jax.ShapeDtypeStruct((tl.n_sample, D_MODEL), f32)]
    else:
        out_specs = tok_spec
        out_shape = jax.ShapeDtypeStruct((tl.n_tok, D_MODEL), f32)
    grid_spec = pltpu.PrefetchScalarGridSpec(
        num_scalar_prefetch=1,
        grid=(tl.n_steps,),
        in_specs=in_specs,
        out_specs=out_specs,
        scratch_shapes=[pltpu.VMEM((2, STEP_TILES * SORT_ROWS, D_MODEL), f32), pltpu.SemaphoreType.DMA((2,))],
    )
    return pl.pallas_call(
        functools.partial(_combine_kernel, n_steps=tl.n_steps, np_steps=tl.np_steps, final=final),
        grid_spec=grid_spec,
        out_shape=out_shape,
        compiler_params=_cparams(1),
        name="moe_combine",
    )(*args)


def _rope_tables(tl):
    half = HEAD_D // 2
    inv = ROPE_BASE ** (-jnp.arange(half, dtype=f32) / half)
    pos_p = jnp.arange(tl.tp, dtype=i32)
    pos_s = PAST_LEN + jnp.arange(tl.ts, dtype=i32)
    pos = jnp.concatenate([pos_p, jnp.tile(pos_s, tl.bs)])
    ang = pos.astype(f32)[:, None] * inv[None, :]
    cos, sin = jnp.cos(ang), jnp.sin(ang)
    return jnp.concatenate([cos, cos], axis=-1), jnp.concatenate([-sin, sin], axis=-1)


def kernel(x_prompt, x_sample, state_ret, state_conv, c_prompt, c_sample, w_mod, b_mod, g_mix_norm, w_in,
           w_conv, b_conv, g_conv_ln, b_conv_ln, g_ret_gn, w_out, g_ffn_norm, w_router, b_router,
           w_exp_gate, w_exp_up, w_exp_down, g_final):
    bp, tp, _ = x_prompt.shape
    bs, ts, _ = x_sample.shape
    depth = w_mod.shape[0]
    tl = _Tiles(bp, tp, bs, ts)
    lay = _Layout(tl.n_tiles)

    c_all = jnp.concatenate([c_prompt, jnp.repeat(c_sample, ts, axis=0)], axis=0)
    mod_seq, mod_tok = _modulation(c_all, bp, w_mod, b_mod)
    mod_seq = mod_seq.reshape(depth, bp, 1, N_MOD * D_MODEL)

    cos_tab, sin_tab = _rope_tables(tl)
    w_in_bf = w_in.astype(bf16)
    w_out_bf = w_out.astype(bf16)
    wr_pad = jnp.pad(w_router.astype(f32), ((0, 0), (0, LANES - N_EXPERTS)))
    wr_hi = wr_pad.astype(bf16)
    wr_lo = (wr_pad - wr_hi.astype(f32)).astype(bf16)
    br_pad = jnp.pad(b_router.astype(f32), (0, LANES - N_EXPERTS)).reshape(1, LANES)
    vec3 = lambda t: t.reshape(depth, 1, t.shape[-1])
    g_mix3, g_ffn3, gn3 = vec3(g_mix_norm), vec3(g_ffn_norm), vec3(g_ret_gn)
    b_conv3, g_ln3, b_ln3 = vec3(b_conv), vec3(g_conv_ln), vec3(b_conv_ln)

    x = (x_prompt.reshape(tl.n_prompt, D_MODEL), x_sample.reshape(tl.n_sample, D_MODEL))
    ret_p, conv_p = [], []
    ret_s_all = conv_s_all = None
    for layer in range(depth):
        q, k, v, gate, a = _inproj(tl, layer, x, mod_seq, mod_tok, g_mix3, w_in_bf, cos_tab, sin_tab)
        ro_p, s_p = _retention_prompt(tl, layer, q, k, v, gate, gn3)
        ro_s, ret_s_all = _retention_sample(tl, layer, q, k, v, gate, state_ret, gn3, ret_s_all)
        co_p, buf_p = _conv_prompt(tl, layer, a, w_conv, b_conv3, g_ln3, b_ln3)
        co_s, conv_s_all = _conv_sample(tl, layer, a, state_conv, w_conv, b_conv3, g_ln3, b_ln3, conv_s_all)
        x_mid, h2, rows, cols, tile_counts = _outproj(
            tl, layer, x, ro_p, ro_s, co_p, co_s, mod_seq, mod_tok, g_ffn3, w_out_bf, wr_hi, wr_lo, br_pad)
        tab, block_e = _moe_tables(lay, tile_counts[:, :, 0])
        xs = _dispatch(tl, lay, tab, h2, rows, cols)
        ys = _experts(layer, lay, block_e, xs, w_exp_gate, w_exp_up, w_exp_down)
        last = layer == depth - 1
        x = _combine(tl, lay, layer, tab, ys, cols, x_mid, mod_seq, mod_tok,
                     g_final.reshape(1, D_MODEL) if last else None)
        ret_p.append(s_p)
        conv_p.append(buf_p)
    y_p, y_s = x
    return (y_p.reshape(bp, tp, D_MODEL), y_s.reshape(bs, ts, D_MODEL),
            jnp.stack(ret_p), jnp.stack(conv_p), ret_s_all, conv_s_all)
```

```python
import functools

import jax
import jax.numpy as jnp
from jax import lax
from jax.experimental import pallas as pl
from jax.experimental.pallas import tpu as pltpu

f32 = jnp.float32
bf16 = jnp.bfloat16
i32 = jnp.int32

D_MODEL = 1024
D_RET = 512
D_CONV = 512
N_HEADS = 4
HEAD_D = 128
RET_CHUNK = 128
RET_CHUNKS_PER_STEP = 8
ROPE_BASE = 10000.0
CONV_WIDTH = 31
CONV_HALO = CONV_WIDTH - 1
N_EXPERTS = 16
N_GROUPS = 4
GROUP_SIZE = N_EXPERTS // N_GROUPS
N_MOD = 6
EPS = 1e-6
PAST_LEN = 16384
D_IN = 4 * D_RET + 2 * D_CONV

LANES = 128
SUBLANES = 8
TM = 256
STEP_TILES = 2
TS = TM * STEP_TILES
BM = 512
CONV_ROWS = 128
NORM_ROWS = 64
SAMPLE_GROUP = 16
HALO_PAD = 32
VMEM_LIMIT = 56 * 1024 * 1024


def _cparams(n_axes, vmem=VMEM_LIMIT):
    return pltpu.CompilerParams(dimension_semantics=("arbitrary",) * n_axes, vmem_limit_bytes=vmem)


def _mod_kernel(c_ref, w_ref, b_ref, seq_ref, tok_ref):
    cond = jax.nn.silu(c_ref[...]).astype(bf16)
    mod = jnp.dot(cond, w_ref[...].astype(bf16), preferred_element_type=f32) + b_ref[...]
    n_seq = seq_ref.shape[0]
    seq_ref[...] = mod[0:n_seq, :]
    tok_ref[...] = mod[n_seq:, :]


def _modulation(c_all, n_seq, w_mod, b_mod):
    depth = w_mod.shape[0]
    m = c_all.shape[0]
    n_tok = m - n_seq
    assert n_seq % SUBLANES == 0
    return pl.pallas_call(
        _mod_kernel,
        grid=(depth, N_MOD),
        in_specs=[
            pl.BlockSpec((m, D_MODEL), lambda l, j: (0, 0)),
            pl.BlockSpec((None, D_MODEL, D_MODEL), lambda l, j: (l, 0, j)),
            pl.BlockSpec((None, 1, D_MODEL), lambda l, j: (l, 0, j)),
        ],
        out_specs=[pl.BlockSpec((None, n_seq, D_MODEL), lambda l, j: (l, 0, j)),
                   pl.BlockSpec((None, n_tok, D_MODEL), lambda l, j: (l, 0, j))],
        out_shape=[jax.ShapeDtypeStruct((depth, n_seq, N_MOD * D_MODEL), f32),
                   jax.ShapeDtypeStruct((depth, n_tok, N_MOD * D_MODEL), f32)],
        compiler_params=_cparams(2),
        name="modulation",
    )(c_all, w_mod, b_mod.reshape(depth, 1, N_MOD * D_MODEL))


class _Tiles:
    def __init__(self, bp, tp, bs, ts):
        self.bp, self.tp, self.bs, self.ts = bp, tp, bs, ts
        self.n_prompt = bp * tp
        self.n_sample = bs * ts
        self.n_tok = self.n_prompt + self.n_sample
        assert tp % TS == 0 and self.n_sample % TS == 0
        self.tiles_per_seq = tp // TM
        self.np_tiles = self.n_prompt // TM
        self.n_tiles = self.n_tok // TM
        self.steps_per_seq = tp // TS
        self.np_steps = self.n_prompt // TS
        self.n_steps = self.n_tok // TS

    def prompt_block(self, i):
        return jnp.minimum(i, self.np_steps - 1)

    def sample_block(self, i):
        return jnp.maximum(i - self.np_steps, 0)

    def seq_index(self, i):
        return jnp.minimum(i // self.steps_per_seq, self.bp - 1)


def _mod_specs(tl, layer, col):
    seq = pl.BlockSpec((None, None, 1, D_MODEL), lambda i, *_: (layer, tl.seq_index(i), 0, col))
    tok = pl.BlockSpec((None, TS, D_MODEL), lambda i, *_: (layer, tl.sample_block(i), col))
    return seq, tok


def _rms(x, g):
    return x * lax.rsqrt(jnp.mean(x * x, axis=-1, keepdims=True) + EPS) * g


def _head_norm_gate(o, gn, gate):
    mu = jnp.mean(o, axis=-1, keepdims=True)
    var = jnp.mean(jnp.square(o - mu), axis=-1, keepdims=True)
    return jax.nn.silu(gate) * ((o - mu) * lax.rsqrt(var + EPS) * gn)


def _dot_nt(a, b):
    return lax.dot_general(a, b, (((1,), (1,)), ((), ())), preferred_element_type=f32)


def _dot_tn(a, b):
    return lax.dot_general(a, b, (((0,), (0,)), ((), ())), preferred_element_type=f32)


def _ret_prompt_kernel(q_ref, k_ref, v_ref, gate_ref, dec_ref, qd_ref, kd_ref, cd_ref, gn_ref,
                       o_ref, s_out_ref, s_ref):
    c = pl.program_id(1)

    @pl.when(c == 0)
    def _():
        s_ref[...] = jnp.zeros_like(s_ref)

    for ci in range(RET_CHUNKS_PER_STEP):
        rows = slice(ci * RET_CHUNK, (ci + 1) * RET_CHUNK)
        for hd in range(N_HEADS):
            sl = slice(hd * HEAD_D, (hd + 1) * HEAD_D)
            kh = k_ref[rows, sl]
            qb = q_ref[rows, sl]
            kb = kh.astype(bf16)
            vb = v_ref[rows, sl]
            s_old = s_ref[hd]
            scores = _dot_nt(qb, kb) * dec_ref[hd]
            inner = jnp.dot(scores.astype(bf16), vb, preferred_element_type=f32)
            cross = jnp.dot(qb, s_old.astype(bf16), preferred_element_type=f32) * qd_ref[hd]
            s_ref[hd] = s_old * cd_ref[hd] + _dot_tn((kh * kd_ref[hd]).astype(bf16), vb)
            o_ref[rows, sl] = _head_norm_gate(inner + cross, gn_ref[:, sl], gate_ref[rows, sl]).astype(bf16)

    @pl.when(c == pl.num_programs(1) - 1)
    def _():
        s_out_ref[...] = s_ref[...]


def _decay_tables(chunk, true_len):
    lg = jnp.log(1.0 - 2.0 ** (-5.0 - jnp.arange(N_HEADS, dtype=f32)))
    idx = jnp.arange(chunk, dtype=f32)
    rel = idx[:, None] - idx[None, :]
    decay = jnp.where(rel[None] >= 0, jnp.exp(jnp.maximum(rel, 0.0)[None] * lg[:, None, None]), 0.0)
    q_decay = jnp.exp((idx[None, :] + 1.0) * lg[:, None])
    k_decay = jnp.exp((true_len - 1.0 - idx[None, :]) * lg[:, None])
    c_decay = jnp.exp(true_len * lg)
    return decay, q_decay, k_decay, c_decay


def _retention_prompt(tl, layer, q, k, v, gate, g_ret_gn):
    step_rows = RET_CHUNK * RET_CHUNKS_PER_STEP
    assert tl.tp % step_rows == 0
    n_chunks = tl.tp // step_rows
    decay, q_decay, k_decay, c_decay = _decay_tables(RET_CHUNK, RET_CHUNK)
    bcast = lambda t: jnp.broadcast_to(t[:, :, None], (N_HEADS, RET_CHUNK, HEAD_D))
    cd = jnp.broadcast_to(c_decay[:, None, None], (N_HEADS, 1, HEAD_D))
    tok_spec = pl.BlockSpec((step_rows, D_RET), lambda b, c: (b * n_chunks + c, 0))
    tab_spec = pl.BlockSpec((N_HEADS, RET_CHUNK, HEAD_D), lambda b, c: (0, 0, 0))
    return pl.pallas_call(
        _ret_prompt_kernel,
        grid=(tl.bp, n_chunks),
        in_specs=[tok_spec] * 4 + [tab_spec] * 3 + [
            pl.BlockSpec((N_HEADS, 1, HEAD_D), lambda b, c: (0, 0, 0)),
            pl.BlockSpec((None, 1, D_RET), lambda b, c: (layer, 0, 0)),
        ],
        out_specs=[tok_spec, pl.BlockSpec((None, N_HEADS, HEAD_D, HEAD_D), lambda b, c: (b, 0, 0, 0))],
        out_shape=[jax.ShapeDtypeStruct((tl.n_prompt, D_RET), bf16),
                   jax.ShapeDtypeStruct((tl.bp, N_HEADS, HEAD_D, HEAD_D), f32)],
        scratch_shapes=[pltpu.VMEM((N_HEADS, HEAD_D, HEAD_D), f32)],
        compiler_params=_cparams(2),
        name="retention_prompt",
    )(q, k, v, gate, decay, bcast(q_decay), bcast(k_decay), cd, g_ret_gn)


def _ret_sample_kernel(q_ref, k_ref, v_ref, gate_ref, s_in_ref, dec_ref, qd_ref, kd_ref, cd_ref, gn_ref,
                       *rest, ts):
    o_ref, s_all_ref = rest[-2:]
    s_out_ref = s_all_ref.at[0]
    for other in range(1, s_all_ref.shape[0]):
        s_all_ref[other] = jnp.zeros(s_all_ref.shape[1:], f32)
    seqs_per_tile = SUBLANES // ts
    row = lax.broadcasted_iota(i32, (SUBLANES, HEAD_D), 0)
    q_all = q_ref[...].astype(f32)
    v_all = v_ref[...].astype(f32)
    outs = []
    for t in range(SAMPLE_GROUP // seqs_per_tile):
        rows = slice(t * SUBLANES, (t + 1) * SUBLANES)
        heads = []
        for hd in range(N_HEADS):
            sl = slice(hd * HEAD_D, (hd + 1) * HEAD_D)
            qh = q_all[rows, sl]
            kh = k_ref[rows, sl] * kd_ref[hd]
            vb = v_all[rows, sl].astype(bf16)
            qb = qh.astype(bf16)
            scores = _dot_nt(qb, k_ref[rows, sl].astype(bf16)) * dec_ref[hd]
            o = jnp.dot(scores.astype(bf16), vb, preferred_element_type=f32)
            for s in range(seqs_per_tile):
                b = t * seqs_per_tile + s
                mine = (row >= s * ts) & (row < (s + 1) * ts)
                s_old = s_in_ref[b, hd]
                q_s = jnp.where(mine, qh, 0.0).astype(bf16)
                k_s = jnp.where(mine, kh, 0.0).astype(bf16)
                o = o + jnp.dot(q_s, s_old.astype(bf16), preferred_element_type=f32) * qd_ref[hd]
                s_out_ref[b, hd] = s_old * cd_ref[hd] + _dot_tn(k_s, vb)
            heads.append(_head_norm_gate(o, gn_ref[:, sl], gate_ref[rows, sl]))
        outs.append(jnp.concatenate(heads, axis=-1))
    o_ref[...] = jnp.concatenate(outs, axis=0).astype(bf16)


def _retention_sample(tl, layer, q, k, v, gate, state_ret, g_ret_gn, prev_states):
    ts = tl.ts
    depth = state_ret.shape[0]
    assert SUBLANES % ts == 0 and tl.bs % SAMPLE_GROUP == 0
    seqs_per_tile = SUBLANES // ts
    decay, q_decay, k_decay, c_decay = _decay_tables(ts, ts)
    eye = jnp.eye(seqs_per_tile, dtype=f32)
    dec_tile = jnp.einsum("ab,hij->haibj", eye, decay).reshape(N_HEADS, SUBLANES, SUBLANES)
    tile_rows = lambda t: jnp.broadcast_to(jnp.tile(t, (1, seqs_per_tile))[:, :, None],
                                           (N_HEADS, SUBLANES, HEAD_D))
    cd = jnp.broadcast_to(c_decay[:, None, None], (N_HEADS, 1, HEAD_D))
    rows = SAMPLE_GROUP * ts
    first = tl.n_prompt // rows
    tok_spec = pl.BlockSpec((rows, D_RET), lambda i: (first + i, 0))
    const3 = lambda shape: pl.BlockSpec(shape, lambda i: (0, 0, 0))
    st_block = (SAMPLE_GROUP, N_HEADS, HEAD_D, HEAD_D)
    in_specs = [tok_spec] * 4 + [
        pl.BlockSpec((None,) + st_block, lambda i: (layer, i, 0, 0, 0)),
        const3((N_HEADS, SUBLANES, SUBLANES)),
        const3((N_HEADS, SUBLANES, HEAD_D)),
        const3((N_HEADS, SUBLANES, HEAD_D)),
        const3((N_HEADS, 1, HEAD_D)),
        pl.BlockSpec((None, 1, D_RET), lambda i: (layer, 0, 0)),
    ]
    args = [q, k, v, gate, state_ret, dec_tile, tile_rows(q_decay), tile_rows(k_decay), cd, g_ret_gn]
    if prev_states is None:
        state_spec = pl.BlockSpec((depth,) + st_block, lambda i: (0, i, 0, 0, 0))
        aliases = {}
    else:
        state_spec = pl.BlockSpec((1,) + st_block, lambda i: (layer, i, 0, 0, 0))
        in_specs.append(pl.BlockSpec(memory_space=pl.ANY))
        args.append(prev_states)
        aliases = {len(args) - 1: 1}
    return pl.pallas_call(
        functools.partial(_ret_sample_kernel, ts=ts),
        grid=(tl.bs // SAMPLE_GROUP,),
        in_specs=in_specs,
        out_specs=[pl.BlockSpec((rows, D_RET), lambda i: (i, 0)), state_spec],
        out_shape=[jax.ShapeDtypeStruct((tl.n_sample, D_RET), bf16),
                   jax.ShapeDtypeStruct((depth, tl.bs, N_HEADS, HEAD_D, HEAD_D), f32)],
        input_output_aliases=aliases,
        compiler_params=_cparams(1),
        name="retention_sample",
    )(*args)


def _ln_silu(cv, g, b):
    mu = jnp.mean(cv, axis=-1, keepdims=True)
    var = jnp.mean(jnp.square(cv - mu), axis=-1, keepdims=True)
    return jax.nn.silu((cv - mu) * lax.rsqrt(var + EPS) * g + b)


def _conv_taps(window, w_ref, b_ref, n_rows):
    cols = []
    for col in range(D_CONV // LANES):
        lanes = slice(col * LANES, (col + 1) * LANES)
        acc = jnp.broadcast_to(b_ref[:, lanes], (n_rows, LANES))
        for tap in range(CONV_WIDTH):
            acc = acc + window(col, tap) * w_ref[tap:tap + 1, lanes]
        cols.append(acc)
    return jnp.concatenate(cols, axis=-1)


def _conv_sample_kernel(a_ref, st_ref, w_ref, b_ref, g_ref, bl_ref, *rest, ts):
    o_ref, buf_all_ref, full_ref, cv_ref = rest[-4:]
    buf_ref = buf_all_ref.at[0]
    for other in range(1, buf_all_ref.shape[0]):
        buf_all_ref[other] = jnp.zeros(buf_all_ref.shape[1:], f32)
    for s in range(SAMPLE_GROUP):
        for col in range(D_CONV // LANES):
            lanes = slice(col * LANES, (col + 1) * LANES)
            full_ref[col, 0:CONV_HALO, :] = st_ref[:, s, lanes]
            full_ref[col, CONV_HALO:CONV_HALO + ts, :] = a_ref[s * ts:(s + 1) * ts, lanes]
            buf_ref[s, :, lanes] = full_ref[col, ts:ts + CONV_HALO, :]
        cv_ref[s * ts:(s + 1) * ts, :] = _conv_taps(lambda col, tap: full_ref[col, tap:tap + ts, :],
                                                    w_ref, b_ref, ts)
    o_ref[...] = _ln_silu(cv_ref[...], g_ref[...], bl_ref[...]).astype(bf16)


def _conv_sample(tl, layer, a, state_conv, w_conv, b_conv, g_ln, b_ln, prev_bufs):
    ts = tl.ts
    depth = state_conv.shape[0]
    rows = SAMPLE_GROUP * ts
    first_block = tl.n_prompt // rows
    vec = pl.BlockSpec((None, 1, D_CONV), lambda i: (layer, 0, 0))
    in_specs = [pl.BlockSpec((rows, D_CONV), lambda i: (first_block + i, 0)),
                pl.BlockSpec((None, CONV_HALO, SAMPLE_GROUP, D_CONV), lambda i: (layer, 0, i, 0)),
                pl.BlockSpec((None, CONV_WIDTH, D_CONV), lambda i: (layer, 0, 0)),
                vec, vec, vec]
    args = [a, state_conv.transpose(0, 2, 1, 3), w_conv, b_conv, g_ln, b_ln]
    buf_block = (SAMPLE_GROUP, CONV_HALO, D_CONV)
    if prev_bufs is None:
        buf_spec = pl.BlockSpec((depth,) + buf_block, lambda i: (0, i, 0, 0))
        aliases = {}
    else:
        buf_spec = pl.BlockSpec((1,) + buf_block, lambda i: (layer, i, 0, 0))
        in_specs.append(pl.BlockSpec(memory_space=pl.ANY))
        args.append(prev_bufs)
        aliases = {len(args) - 1: 1}
    return pl.pallas_call(
        functools.partial(_conv_sample_kernel, ts=ts),
        grid=(tl.bs // SAMPLE_GROUP,),
        in_specs=in_specs,
        out_specs=[pl.BlockSpec((rows, D_CONV), lambda i: (i, 0)), buf_spec],
        out_shape=[jax.ShapeDtypeStruct((tl.n_sample, D_CONV), bf16),
                   jax.ShapeDtypeStruct((depth, tl.bs, CONV_HALO, D_CONV), f32)],
        input_output_aliases=aliases,
        scratch_shapes=[pltpu.VMEM((D_CONV // LANES, CONV_HALO + ts + SUBLANES, LANES), f32),
                        pltpu.VMEM((rows, D_CONV), f32)],
        compiler_params=_cparams(1),
        name="conv_sample",
    )(*args)


N_CONV_COLS = D_CONV // LANES
N_CONV_CHUNKS = TM // CONV_ROWS


def _inproj_kernel(*refs, np_steps, split_x):
    if split_x:
        xp_ref, xs_ref = refs[:2]
        refs = refs[2:]
    else:
        x_ref = refs[0]
        refs = refs[1:]
    (shs_ref, sht_ref, scs_ref, sct_ref, g_ref, w_ref, cos_ref, sin_ref,
     q_ref, k_ref, v_ref, gate_ref, a_ref) = refs
    is_s = pl.program_id(0) >= np_steps

    def step(sample):
        for t in range(STEP_TILES):
            rs = slice(t * TM, (t + 1) * TM)
            if split_x:
                x = xs_ref[rs, :] if sample else xp_ref[rs, :]
            else:
                x = x_ref[rs, :]
            sh, sc = (sht_ref[rs, :], sct_ref[rs, :]) if sample else (shs_ref[...], scs_ref[...])
            hb = (_rms(x, g_ref[...]) * (1.0 + sc) + sh).astype(bf16)

            def group(g):
                return jnp.dot(hb, w_ref[:, g * D_RET:(g + 1) * D_RET], preferred_element_type=f32)

            cos = cos_ref[rs, :]
            sin = sin_ref[rs, :]

            def rope(th):
                return th * cos + pltpu.roll(th, HEAD_D // 2, 1) * sin

            qg, kg = group(0), group(1)
            for hd in range(N_HEADS):
                sl = slice(hd * HEAD_D, (hd + 1) * HEAD_D)
                q_ref[rs, sl] = rope(qg[:, sl]).astype(bf16)
                k_ref[rs, sl] = rope(kg[:, sl]) * (HEAD_D ** -0.5)
            v_ref[rs, :] = group(2).astype(bf16)
            gate_ref[rs, :] = group(3)
            a_ref[rs, :] = group(4) * jax.nn.sigmoid(group(5))

    pl.when(is_s)(functools.partial(step, True))
    pl.when(jnp.logical_not(is_s))(functools.partial(step, False))


def _conv_prompt_kernel(a_ref, w_ref, b_ref, g_ref, bl_ref, o_ref, buf_ref, full_ref, cv_ref):
    j = pl.program_id(1)

    @pl.when(j == 0)
    def _():
        full_ref[:, 0:HALO_PAD, :] = jnp.zeros((N_CONV_COLS, HALO_PAD, LANES), f32)

    @pl.when(j > 0)
    def _():
        full_ref[:, 0:HALO_PAD, :] = full_ref[:, TM:TM + HALO_PAD, :]

    for col in range(N_CONV_COLS):
        full_ref[col, HALO_PAD:HALO_PAD + TM, :] = a_ref[:, col * LANES:(col + 1) * LANES]
    shift = HALO_PAD - CONV_HALO

    def taps(idx, carry):
        col = idx // N_CONV_CHUNKS
        r0 = pl.multiple_of((idx % N_CONV_CHUNKS) * CONV_ROWS, CONV_ROWS)
        acc = jnp.broadcast_to(b_ref[col], (CONV_ROWS, LANES))
        for tap in range(CONV_WIDTH):
            acc = acc + full_ref[col, pl.ds(r0 + (tap + shift), CONV_ROWS), :] * w_ref[col, tap:tap + 1, :]
        cv_ref[col, pl.ds(r0, CONV_ROWS), :] = acc
        return carry

    lax.fori_loop(0, N_CONV_COLS * N_CONV_CHUNKS, taps, 0)

    for r0 in range(0, TM, NORM_ROWS):
        cv = jnp.concatenate([cv_ref[col, r0:r0 + NORM_ROWS, :] for col in range(N_CONV_COLS)], axis=-1)
        o_ref[r0:r0 + NORM_ROWS, :] = _ln_silu(cv, g_ref[...], bl_ref[...]).astype(bf16)

    @pl.when(j == pl.num_programs(1) - 1)
    def _():
        buf_ref[...] = a_ref[TM - CONV_HALO:TM, :]


def _conv_prompt(tl, layer, a, w_conv, b_conv, g_ln, b_ln):
    tps = tl.tiles_per_seq
    depth = w_conv.shape[0]
    w_cols = w_conv.reshape(depth, CONV_WIDTH, N_CONV_COLS, LANES).transpose(0, 2, 1, 3)
    b_cols = b_conv.reshape(depth, N_CONV_COLS, 1, LANES)
    vec = pl.BlockSpec((None, 1, D_CONV), lambda b, j: (layer, 0, 0))
    return pl.pallas_call(
        _conv_prompt_kernel,
        grid=(tl.bp, tps),
        in_specs=[pl.BlockSpec((TM, D_CONV), lambda b, j: (b * tps + j, 0)),
                  pl.BlockSpec((None, N_CONV_COLS, CONV_WIDTH, LANES), lambda b, j: (layer, 0, 0, 0)),
                  pl.BlockSpec((None, N_CONV_COLS, 1, LANES), lambda b, j: (layer, 0, 0, 0)),
                  vec, vec],
        out_specs=[pl.BlockSpec((TM, D_CONV), lambda b, j: (b * tps + j, 0)),
                   pl.BlockSpec((None, CONV_HALO, D_CONV), lambda b, j: (b, 0, 0))],
        out_shape=[jax.ShapeDtypeStruct((tl.n_prompt, D_CONV), bf16),
                   jax.ShapeDtypeStruct((tl.bp, CONV_HALO, D_CONV), f32)],
        scratch_shapes=[pltpu.VMEM((N_CONV_COLS, HALO_PAD + TM, LANES), f32),
                        pltpu.VMEM((N_CONV_COLS, TM, LANES), f32)],
        compiler_params=_cparams(2),
        name="conv_prompt",
    )(a, w_cols, b_cols, g_ln, b_ln)


def _inproj(tl, layer, x, mod_seq, mod_tok, g_norm, w_in_bf, cos_tab, sin_tab):
    split_x = isinstance(x, tuple)
    tok_spec = pl.BlockSpec((TS, D_MODEL), lambda i: (i, 0))
    if split_x:
        x_args = list(x)
        x_specs = [pl.BlockSpec((TS, D_MODEL), lambda i: (tl.prompt_block(i), 0)),
                   pl.BlockSpec((TS, D_MODEL), lambda i: (tl.sample_block(i), 0))]
    else:
        x_args, x_specs = [x], [tok_spec]
    sh_seq, sh_tok = _mod_specs(tl, layer, 0)
    sc_seq, sc_tok = _mod_specs(tl, layer, 1)

    def table_block(i):
        return jnp.where(i < tl.np_steps, i % tl.steps_per_seq, tl.steps_per_seq + tl.sample_block(i))

    tab_spec = pl.BlockSpec((TS, HEAD_D), lambda i: (table_block(i), 0))
    row_spec = pl.BlockSpec((TS, D_RET), lambda i: (i, 0))
    row_sd = lambda dt: jax.ShapeDtypeStruct((tl.n_tok, D_RET), dt)
    return pl.pallas_call(
        functools.partial(_inproj_kernel, np_steps=tl.np_steps, split_x=split_x),
        grid=(tl.n_steps,),
        in_specs=x_specs + [
            sh_seq, sh_tok, sc_seq, sc_tok,
            pl.BlockSpec((None, 1, D_MODEL), lambda i: (layer, 0, 0)),
            pl.BlockSpec((None, D_MODEL, D_IN), lambda i: (layer, 0, 0)),
            tab_spec, tab_spec,
        ],
        out_specs=[row_spec] * 5,
        out_shape=[row_sd(bf16), row_sd(f32), row_sd(bf16), row_sd(f32), row_sd(f32)],
        compiler_params=_cparams(1),
        name="inproj",
    )(*x_args, mod_seq, mod_tok, mod_seq, mod_tok, g_norm, w_in_bf, cos_tab, sin_tab)


SORT_ROWS = 2 * TM + N_EXPERTS * SUBLANES
XS_HALF = D_MODEL // 2
XS_W = XS_HALF + LANES
u32 = jnp.uint32


def _pack_bf16_pair(x):
    lo = lax.shift_right_logical(lax.bitcast_convert_type(x[:, 0:XS_HALF], u32), u32(16))
    hi = lax.bitcast_convert_type(x[:, XS_HALF:D_MODEL], u32) & u32(0xFFFF0000)
    return hi | lo


def _unpack_bf16_pair(words):
    lo = lax.bitcast_convert_type(lax.shift_left(words, u32(16)), f32).astype(bf16)
    hi = lax.bitcast_convert_type(words & u32(0xFFFF0000), f32).astype(bf16)
    return lo, hi


def _split3(x):
    a = x.astype(bf16)
    r = x - a.astype(f32)
    b = r.astype(bf16)
    c = (r - b.astype(f32)).astype(bf16)
    return a, b, c


def _first_of4(vals, m):
    return jnp.where(vals[0] == m, 0.0, jnp.where(vals[1] == m, 1.0, jnp.where(vals[2] == m, 2.0, 3.0)))


def _rows_to_tile(rows, n_rows):
    sub = lax.broadcasted_iota(i32, (n_rows, TM), 0)
    out = jnp.zeros((n_rows, TM), f32)
    for r, val in enumerate(rows):
        out = jnp.where(sub == r, val, out)
    return out


def _outproj_kernel(*refs, np_steps, split_x):
    if split_x:
        xp_ref, xs_ref = refs[:2]
        refs = refs[2:]
    else:
        x_ref = refs[0]
        refs = refs[1:]
    (retp_ref, rets_ref, cvp_ref, cvs_ref, gts_ref, gtt_ref, shs_ref, sht_ref, scs_ref, sct_ref,
     g_ref, wo_ref, wrh_ref, wrl_ref, br_ref,
     xo_ref, h2_ref, rows_ref, cols_ref, cnt_ref) = refs
    is_s = pl.program_id(0) >= np_steps

    def step(sample):
        for t in range(STEP_TILES):
            rs = slice(t * TM, (t + 1) * TM)
            if split_x:
                x = xs_ref[rs, :] if sample else xp_ref[rs, :]
            else:
                x = x_ref[rs, :]
            ret = rets_ref[rs, :] if sample else retp_ref[rs, :]
            cv = cvs_ref[rs, :] if sample else cvp_ref[rs, :]
            gt, sh, sc = ((gtt_ref[rs, :], sht_ref[rs, :], sct_ref[rs, :]) if sample
                          else (gts_ref[...], shs_ref[...], scs_ref[...]))
            mix_out = (jnp.dot(ret, wo_ref[0:D_RET, :], preferred_element_type=f32)
                       + jnp.dot(cv, wo_ref[D_RET:D_RET + D_CONV, :], preferred_element_type=f32))
            xn = x + gt * mix_out
            xo_ref[rs, :] = xn
            h2 = _rms(xn, g_ref[...]) * (1.0 + sc) + sh
            h_hi = h2.astype(bf16)
            h2_ref[rs, :] = h_hi
            rows, cols, cnt = _route_tile(h2, h_hi, wrh_ref, wrl_ref, br_ref)
            rows_ref[t] = rows
            cols_ref[rs, :] = cols
            cnt_ref[t] = cnt

    pl.when(is_s)(functools.partial(step, True))
    pl.when(jnp.logical_not(is_s))(functools.partial(step, False))


def _route_tile(h2, h_hi, wrh_ref, wrl_ref, br_ref):
    h_lo = (h2 - h_hi.astype(f32)).astype(bf16)
    logits = (jnp.dot(h_hi, wrh_ref[...], preferred_element_type=f32)
              + jnp.dot(h_hi, wrl_ref[...], preferred_element_type=f32)
              + jnp.dot(h_lo, wrh_ref[...], preferred_element_type=f32)) + br_ref[...]
    lt = logits.T
    row = [lt[e:e + 1, :] for e in range(N_EXPERTS)]
    top = functools.reduce(jnp.maximum, row)
    ex = [jnp.exp(r - top) for r in row]
    den = functools.reduce(jnp.add, ex)
    p = [v / den for v in ex]

    best = None
    for g in range(N_GROUPS):
        a = p[g * GROUP_SIZE:(g + 1) * GROUP_SIZE]
        m1 = functools.reduce(jnp.maximum, a)
        i1 = _first_of4(a, m1)
        b = [jnp.where(i1 == float(j), -1.0, a[j]) for j in range(GROUP_SIZE)]
        m2 = functools.reduce(jnp.maximum, b)
        i2 = _first_of4(b, m2)
        cand = (m1 + m2, m1, m2, i1 + float(g * GROUP_SIZE), i2 + float(g * GROUP_SIZE))
        if best is None:
            best = cand
        else:
            take = cand[0] > best[0]
            best = tuple(jnp.where(take, c, o) for c, o in zip(cand, best))
    _, m1, m2, e0, e1 = best
    denom = m1 + m2
    w0 = m1 / denom
    w1 = m2 / denom

    ex_id = lax.broadcasted_iota(i32, (N_EXPERTS, TM), 0).astype(f32)
    sel0 = ex_id == e0
    sel1 = ex_id == e1
    ind = jnp.where(sel0 | sel1, 1.0, 0.0)
    t_r = lax.broadcasted_iota(i32, (TM, TM), 0)
    t_c = lax.broadcasted_iota(i32, (TM, TM), 1)
    earlier = jnp.where(t_r < t_c, 1.0, 0.0).astype(bf16)
    prefix = jnp.dot(ind.astype(bf16), earlier, preferred_element_type=f32)
    cnt = jnp.sum(ind, axis=-1, keepdims=True)
    cnt8 = jnp.floor((cnt + float(SUBLANES - 1)) * (1.0 / SUBLANES)) * float(SUBLANES)
    e_r = lax.broadcasted_iota(i32, (N_EXPERTS, N_EXPERTS), 0)
    e_c = lax.broadcasted_iota(i32, (N_EXPERTS, N_EXPERTS), 1)
    below = jnp.where(e_c < e_r, 1.0, 0.0).astype(bf16)
    seg_off = jnp.dot(below, jnp.broadcast_to(cnt8, (N_EXPERTS, TM)).astype(bf16),
                      preferred_element_type=f32)
    where_to = seg_off + prefix
    pos0 = jnp.sum(jnp.where(sel0, where_to, 0.0), axis=0, keepdims=True)
    pos1 = jnp.sum(jnp.where(sel1, where_to, 0.0), axis=0, keepdims=True)

    w0p = [v.astype(f32) for v in _split3(w0)]
    w1p = [v.astype(f32) for v in _split3(w1)]
    info = [pos0, pos1] + w0p + w1p
    return (_rows_to_tile(info, SUBLANES), _rows_to_tile(info, LANES).T,
            jnp.broadcast_to(cnt, (N_EXPERTS, LANES)))


def _outproj(tl, layer, x, ret_p, ret_s, cv_p, cv_s, mod_seq, mod_tok, g_norm, w_out_bf, wr_hi, wr_lo, br_pad):
    split_x = isinstance(x, tuple)
    tok_spec = pl.BlockSpec((TS, D_MODEL), lambda i: (i, 0))
    p_spec = lambda w: pl.BlockSpec((TS, w), lambda i: (tl.prompt_block(i), 0))
    s_spec = lambda w: pl.BlockSpec((TS, w), lambda i: (tl.sample_block(i), 0))
    if split_x:
        x_args, x_specs = list(x), [p_spec(D_MODEL), s_spec(D_MODEL)]
    else:
        x_args, x_specs = [x], [tok_spec]
    mods = []
    for col in (2, 3, 4):
        mods += list(_mod_specs(tl, layer, col))
    wr_spec = pl.BlockSpec((D_MODEL, LANES), lambda i: (0, 0))
    return pl.pallas_call(
        functools.partial(_outproj_kernel, np_steps=tl.np_steps, split_x=split_x),
        grid=(tl.n_steps,),
        in_specs=x_specs + [p_spec(D_RET), s_spec(D_RET), p_spec(D_CONV), s_spec(D_CONV)] + mods + [
            pl.BlockSpec((None, 1, D_MODEL), lambda i: (layer, 0, 0)),
            pl.BlockSpec((None, D_MODEL, D_MODEL), lambda i: (layer, 0, 0)),
            wr_spec, wr_spec,
            pl.BlockSpec((1, LANES), lambda i: (0, 0)),
        ],
        out_specs=[tok_spec, tok_spec,
                   pl.BlockSpec((STEP_TILES, SUBLANES, TM), lambda i: (i, 0, 0)),
                   pl.BlockSpec((TS, LANES), lambda i: (i, 0)),
                   pl.BlockSpec((STEP_TILES, N_EXPERTS, LANES), lambda i: (i, 0, 0))],
        out_shape=[jax.ShapeDtypeStruct((tl.n_tok, D_MODEL), f32),
                   jax.ShapeDtypeStruct((tl.n_tok, D_MODEL), bf16),
                   jax.ShapeDtypeStruct((tl.n_tiles, SUBLANES, TM), f32),
                   jax.ShapeDtypeStruct((tl.n_tok, LANES), f32),
                   jax.ShapeDtypeStruct((tl.n_tiles, N_EXPERTS, LANES), f32)],
        compiler_params=_cparams(1),
        name="outproj_router",
    )(*x_args, ret_p, ret_s, cv_p, cv_s, *([mod_seq, mod_tok] * 3), g_norm, w_out_bf, wr_hi, wr_lo, br_pad)


N_CHUNKS = SORT_ROWS // SUBLANES


class _Layout:
    def __init__(self, n_tiles):
        self.n_tiles = n_tiles
        self.tail_start = n_tiles * N_CHUNKS
        self.tail_n8 = self.tail_start + N_EXPERTS
        worst = 2 * n_tiles * TM + n_tiles * N_EXPERTS * (SUBLANES - 1) + N_EXPERTS * (BM - SUBLANES)
        self.n_blocks = -(-worst // BM)
        self.cap = self.n_blocks * BM
        self.dump = self.cap
        self.xs_rows = self.cap + -(-2 * STEP_TILES * SORT_ROWS // BM) * BM


def _moe_tables(lay, tile_counts):
    c8 = ((tile_counts.astype(i32) + SUBLANES - 1) // SUBLANES) * SUBLANES
    base8 = jnp.cumsum(c8, axis=0) - c8
    tot8 = jnp.sum(c8, axis=0)
    region = ((tot8 + BM - 1) // BM) * BM
    g_end = jnp.cumsum(region)
    g_start = g_end - region
    seg_end = jnp.cumsum(c8, axis=1)
    seg_dst = g_start[None, :] + base8
    n_used = g_end[-1] // BM
    blk = jnp.arange(lay.n_blocks, dtype=i32)
    block_e = jnp.minimum(jnp.sum((g_end[None, :] <= blk[:, None] * BM).astype(i32), axis=1), N_EXPERTS - 1)
    block_e = jnp.where(blk < n_used, block_e, block_e[n_used - 1])
    row0 = jnp.arange(N_CHUNKS, dtype=i32) * SUBLANES
    owner = jnp.sum((seg_end[:, None, :] <= row0[None, :, None]).astype(i32), axis=-1)
    onehot = (owner[:, :, None] == jnp.arange(N_EXPERTS, dtype=i32)[None, None, :]).astype(i32)
    delta = seg_dst - (seg_end - c8)
    chunk_dst = jnp.where(owner < N_EXPERTS, row0[None, :] + jnp.sum(onehot * delta[:, None, :], axis=-1), -1)
    tab = jnp.concatenate([chunk_dst.ravel(), g_start + tot8, (region - tot8) // SUBLANES]).astype(i32)
    ids = jnp.arange(N_EXPERTS, dtype=i32)
    later = jnp.where((ids[None, :] > ids[:, None]) & (region[None, :] > 0), ids[None, :], N_EXPERTS)
    next_e = jnp.min(later, axis=1)
    next_e = jnp.where(next_e == N_EXPERTS, -1, next_e)
    owner_end = jnp.sum((block_e[:, None] == ids[None, :]).astype(i32) * (g_start + tot8)[None, :], axis=1)
    rows_used = jnp.clip(owner_end - blk * BM, 0, BM)
    return tab, jnp.concatenate([block_e, n_used[None], next_e, rows_used]).astype(i32)


def _for_chunks(n, fn):
    def body(c, carry):
        fn(c)
        return carry

    lax.fori_loop(0, n, body, 0)


def _dispatch_kernel(tab_ref, h2_ref, rows_ref, cols_ref, xs_hbm, sorted_ref, zero_ref, sem, *, lay, n_steps):
    i = pl.program_id(0)
    slot = i % 2
    step_rows = STEP_TILES * SORT_ROWS

    def tail_copy(dst):
        return pltpu.make_async_copy(zero_ref, xs_hbm.at[pl.ds(dst, SUBLANES)], sem.at[2])

    def wait_step(slot_):
        pltpu.make_async_copy(sorted_ref.at[slot_], xs_hbm.at[pl.ds(0, step_rows)], sem.at[slot_]).wait()

    @pl.when(i == 0)
    def _():
        zero_ref[...] = jnp.zeros_like(zero_ref)
        for e in range(N_EXPERTS):
            start = tab_ref[lay.tail_start + e]
            _for_chunks(tab_ref[lay.tail_n8 + e],
                        lambda c: tail_copy(pl.multiple_of(start + c * SUBLANES, SUBLANES)).start())

    @pl.when(i >= 2)
    def _():
        wait_step(slot)

    r_id = lax.broadcasted_iota(i32, (SORT_ROWS, TM), 0).astype(f32)
    lane = lax.broadcasted_iota(i32, (TM, LANES), 1)
    for t in range(STEP_TILES):
        pos0 = rows_ref[t, 0:1, :]
        pos1 = rows_ref[t, 1:2, :]
        p0 = r_id == pos0
        p1 = r_id == pos1
        perm = jnp.where(p0 | p1, 1.0, 0.0).astype(bf16)
        cols = cols_ref[t * TM:(t + 1) * TM, :]
        wpart0 = jnp.where((lane >= 2) & (lane < 5), cols, 0.0).astype(bf16)
        wpart1 = jnp.where((lane >= 5) & (lane < 8), cols, 0.0).astype(bf16)
        sw = (jnp.dot(jnp.where(p0, 1.0, 0.0).astype(bf16), wpart0, preferred_element_type=f32)
              + jnp.dot(jnp.where(p1, 1.0, 0.0).astype(bf16), wpart1, preferred_element_type=f32))
        base = t * SORT_ROWS
        sorted_ref[slot, base:base + SORT_ROWS, 0:XS_HALF] = _pack_bf16_pair(jnp.dot(
            perm, h2_ref[t * TM:(t + 1) * TM, :], preferred_element_type=f32))
        sorted_ref[slot, base:base + SORT_ROWS, XS_HALF:XS_W] = lax.bitcast_convert_type(jnp.broadcast_to(
            jnp.sum(sw, axis=-1, keepdims=True), (SORT_ROWS, LANES)), u32)

    for t in range(STEP_TILES):
        for c in range(N_CHUNKS):
            row = t * SORT_ROWS + c * SUBLANES
            dst = tab_ref[(i * STEP_TILES + t) * N_CHUNKS + c]
            dst = jnp.where(dst < 0, lay.dump + slot * step_rows + row, dst)
            pltpu.make_async_copy(sorted_ref.at[slot, pl.ds(row, SUBLANES)],
                                  xs_hbm.at[pl.ds(pl.multiple_of(dst, SUBLANES), SUBLANES)], sem.at[slot]).start()

    @pl.when(i == n_steps - 1)
    def _():
        if n_steps >= 2:
            wait_step(1 - slot)
        wait_step(slot)
        for e in range(N_EXPERTS):
            _for_chunks(tab_ref[lay.tail_n8 + e], lambda c: tail_copy(0).wait())


def _dispatch(tl, lay, tab, h2, rows, cols):
    grid_spec = pltpu.PrefetchScalarGridSpec(
        num_scalar_prefetch=1,
        grid=(tl.n_steps,),
        in_specs=[pl.BlockSpec((TS, D_MODEL), lambda i, t: (i, 0)),
                  pl.BlockSpec((STEP_TILES, SUBLANES, TM), lambda i, t: (i, 0, 0)),
                  pl.BlockSpec((TS, LANES), lambda i, t: (i, 0))],
        out_specs=pl.BlockSpec(memory_space=pl.ANY),
        scratch_shapes=[pltpu.VMEM((2, STEP_TILES * SORT_ROWS, XS_W), u32), pltpu.VMEM((SUBLANES, XS_W), u32),
                        pltpu.SemaphoreType.DMA((3,))],
    )
    return pl.pallas_call(
        functools.partial(_dispatch_kernel, lay=lay, n_steps=tl.n_steps),
        grid_spec=grid_spec,
        out_shape=jax.ShapeDtypeStruct((lay.xs_rows, XS_W), u32),
        compiler_params=_cparams(1),
        name="moe_dispatch",
    )(tab, h2, rows, cols)


def _expert_kernel(be_ref, xs_ref, wg_hbm, wu_hbm, wd_hbm, ys_ref, stage, w_bf, sem, *, n_blocks, layer):
    j = pl.program_id(0)

    def fetch(e):
        return [pltpu.make_async_copy(w.at[layer, e], stage.at[k], sem.at[k])
                for k, w in enumerate((wg_hbm, wu_hbm, wd_hbm))]

    @pl.when(j < be_ref[n_blocks])
    def _():
        e = be_ref[j]

        @pl.when(j == 0)
        def _():
            for copy in fetch(e):
                copy.start()

        @pl.when((j == 0) | (e != be_ref[jnp.maximum(j - 1, 0)]))
        def _():
            for copy in fetch(e):
                copy.wait()
            for k in range(3):
                w_bf[k] = stage[k].astype(bf16)
            nxt = be_ref[n_blocks + 1 + e]

            @pl.when(nxt >= 0)
            def _():
                for copy in fetch(nxt):
                    copy.start()

        def ffn(rows):
            x_lo, x_hi = _unpack_bf16_pair(xs_ref[rows, 0:XS_HALF])

            def first_layer(k):
                return (jnp.dot(x_lo, w_bf[k, 0:XS_HALF, :], preferred_element_type=f32)
                        + jnp.dot(x_hi, w_bf[k, XS_HALF:D_MODEL, :], preferred_element_type=f32))

            mid = (jax.nn.silu(first_layer(0)) * first_layer(1)).astype(bf16)
            slot_w = lax.bitcast_convert_type(xs_ref[rows, XS_HALF:XS_HALF + 1], f32)
            ys_ref[rows, :] = jnp.dot(mid, w_bf[2], preferred_element_type=f32) * slot_w

        half = BM // 2
        rows_used = be_ref[n_blocks + 1 + N_EXPERTS + j]

        @pl.when(rows_used > half)
        def _():
            ffn(slice(0, BM))

        @pl.when(rows_used <= half)
        def _():
            ffn(slice(0, half))
            ys_ref[half:BM, :] = jnp.zeros((BM - half, D_MODEL), f32)


def _experts(layer, lay, block_e, xs, w_gate, w_up, w_down):
    n_blocks = lay.n_blocks
    d_ff = w_gate.shape[-1]
    assert d_ff == D_MODEL
    used = lambda j, be: jnp.minimum(j, be[n_blocks] - 1)
    any_spec = pl.BlockSpec(memory_space=pl.ANY)
    grid_spec = pltpu.PrefetchScalarGridSpec(
        num_scalar_prefetch=1,
        grid=(n_blocks,),
        in_specs=[pl.BlockSpec((BM, XS_W), lambda j, be: (used(j, be), 0)), any_spec, any_spec, any_spec],
        out_specs=pl.BlockSpec((BM, D_MODEL), lambda j, be: (used(j, be), 0)),
        scratch_shapes=[pltpu.VMEM((3, D_MODEL, d_ff), f32), pltpu.VMEM((3, D_MODEL, d_ff), bf16),
                        pltpu.SemaphoreType.DMA((3,))],
    )
    return pl.pallas_call(
        functools.partial(_expert_kernel, n_blocks=n_blocks, layer=layer),
        grid_spec=grid_spec,
        out_shape=jax.ShapeDtypeStruct((lay.cap, D_MODEL), f32),
        compiler_params=_cparams(1),
        name="moe_experts",
    )(block_e, xs, w_gate, w_up, w_down)


def _combine_kernel(tab_ref, ys_hbm, cols_ref, x_ref, gts_ref, gtt_ref, *rest, n_steps, np_steps, final):
    if final:
        gf_ref, yp_ref, ysm_ref, staged, sem = rest
    else:
        xo_ref, staged, sem = rest
    i = pl.program_id(0)
    slot = i % 2
    is_s = i >= np_steps

    def start_step(step, slot_):
        for c in range(STEP_TILES * N_CHUNKS):
            src = jnp.maximum(tab_ref[step * (STEP_TILES * N_CHUNKS) + c], 0)
            pltpu.make_async_copy(ys_hbm.at[pl.ds(pl.multiple_of(src, SUBLANES), SUBLANES)],
                                  staged.at[slot_, pl.ds(c * SUBLANES, SUBLANES)], sem.at[slot_]).start()

    @pl.when(i == 0)
    def _():
        start_step(0, 0)

    @pl.when(i + 1 < n_steps)
    def _():
        start_step(i + 1, 1 - slot)

    pltpu.make_async_copy(ys_hbm.at[pl.ds(0, STEP_TILES * SORT_ROWS)], staged.at[slot], sem.at[slot]).wait()

    def step(sample):
        lane = lax.broadcasted_iota(i32, (TM, SORT_ROWS), 1).astype(f32)
        for t in range(STEP_TILES):
            rs = slice(t * TM, (t + 1) * TM)
            unperm = jnp.where((lane == cols_ref[rs, 0:1]) | (lane == cols_ref[rs, 1:2]), 1.0, 0.0).astype(bf16)
            parts = _split3(staged[slot, t * SORT_ROWS:(t + 1) * SORT_ROWS, :])[0:2]
            ff = sum(jnp.dot(unperm, part, preferred_element_type=f32) for part in parts)
            xn = x_ref[rs, :] + (gtt_ref[rs, :] if sample else gts_ref[...]) * ff
            if final:
                (ysm_ref if sample else yp_ref)[rs, :] = _rms(xn, gf_ref[...])
            else:
                xo_ref[rs, :] = xn

    pl.when(is_s)(functools.partial(step, True))
    pl.when(jnp.logical_not(is_s))(functools.partial(step, False))


def _combine(tl, lay, layer, tab, ys, cols, x, mod_seq, mod_tok, g_final):
    final = g_final is not None
    tok_spec = pl.BlockSpec((TS, D_MODEL), lambda i, t: (i, 0))
    gt_seq, gt_tok = _mod_specs(tl, layer, 5)
    in_specs = [pl.BlockSpec(memory_space=pl.ANY), pl.BlockSpec((TS, LANES), lambda i, t: (i, 0)),
                tok_spec, gt_seq, gt_tok]
    args = [tab, ys, cols, x, mod_seq, mod_tok]
    if final:
        in_specs.append(pl.BlockSpec((1, D_MODEL), lambda i, t: (0, 0)))
        args.append(g_final)
        out_specs = [pl.BlockSpec((TS, D_MODEL), lambda i, t: (tl.prompt_block(i), 0)),
                     pl.BlockSpec((TS, D_MODEL), lambda i, t: (tl.sample_block(i), 0))]
        out_shape = [jax.ShapeDtypeStruct((tl.n_prompt, D_MODEL), f32),
                     jax.ShapeDtypeStruct((tl.n_sample, D_MODEL), f32)]
    else:
        out_specs = tok_spec
        out_shape = jax.ShapeDtypeStruct((tl.n_tok, D_MODEL), f32)
    grid_spec = pltpu.PrefetchScalarGridSpec(
        num_scalar_prefetch=1,
        grid=(tl.n_steps,),
        in_specs=in_specs,
        out_specs=out_specs,
        scratch_shapes=[pltpu.VMEM((2, STEP_TILES * SORT_ROWS, D_MODEL), f32), pltpu.SemaphoreType.DMA((2,))],
    )
    return pl.pallas_call(
        functools.partial(_combine_kernel, n_steps=tl.n_steps, np_steps=tl.np_steps, final=final),
        grid_spec=grid_spec,
        out_shape=out_shape,
        compiler_params=_cparams(1),
        name="moe_combine",
    )(*args)


def _rope_tables(tl):
    half = HEAD_D // 2
    inv = ROPE_BASE ** (-jnp.arange(half, dtype=f32) / half)
    pos_p = jnp.arange(tl.tp, dtype=i32)
    pos_s = PAST_LEN + jnp.arange(tl.ts, dtype=i32)
    pos = jnp.concatenate([pos_p, jnp.tile(pos_s, tl.bs)])
    ang = pos.astype(f32)[:, None] * inv[None, :]
    cos, sin = jnp.cos(ang), jnp.sin(ang)
    return jnp.concatenate([cos, cos], axis=-1), jnp.concatenate([-sin, sin], axis=-1)


def kernel(x_prompt, x_sample, state_ret, state_conv, c_prompt, c_sample, w_mod, b_mod, g_mix_norm, w_in,
           w_conv, b_conv, g_conv_ln, b_conv_ln, g_ret_gn, w_out, g_ffn_norm, w_router, b_router,
           w_exp_gate, w_exp_up, w_exp_down, g_final):
    bp, tp, _ = x_prompt.shape
    bs, ts, _ = x_sample.shape
    depth = w_mod.shape[0]
    tl = _Tiles(bp, tp, bs, ts)
    lay = _Layout(tl.n_tiles)

    c_all = jnp.concatenate([c_prompt, jnp.repeat(c_sample, ts, axis=0)], axis=0)
    mod_seq, mod_tok = _modulation(c_all, bp, w_mod, b_mod)
    mod_seq = mod_seq.reshape(depth, bp, 1, N_MOD * D_MODEL)

    cos_tab, sin_tab = _rope_tables(tl)
    w_in_bf = w_in.astype(bf16)
    w_out_bf = w_out.astype(bf16)
    wr_pad = jnp.pad(w_router.astype(f32), ((0, 0), (0, LANES - N_EXPERTS)))
    wr_hi = wr_pad.astype(bf16)
    wr_lo = (wr_pad - wr_hi.astype(f32)).astype(bf16)
    br_pad = jnp.pad(b_router.astype(f32), (0, LANES - N_EXPERTS)).reshape(1, LANES)
    vec3 = lambda t: t.reshape(depth, 1, t.shape[-1])
    g_mix3, g_ffn3, gn3 = vec3(g_mix_norm), vec3(g_ffn_norm), vec3(g_ret_gn)
    b_conv3, g_ln3, b_ln3 = vec3(b_conv), vec3(g_conv_ln), vec3(b_conv_ln)

    x = (x_prompt.reshape(tl.n_prompt, D_MODEL), x_sample.reshape(tl.n_sample, D_MODEL))
    ret_p, conv_p = [], []
    ret_s_all = conv_s_all = None
    for layer in range(depth):
        q, k, v, gate, a = _inproj(tl, layer, x, mod_seq, mod_tok, g_mix3, w_in_bf, cos_tab, sin_tab)
        ro_p, s_p = _retention_prompt(tl, layer, q, k, v, gate, gn3)
        ro_s, ret_s_all = _retention_sample(tl, layer, q, k, v, gate, state_ret, gn3, ret_s_all)
        co_p, buf_p = _conv_prompt(tl, layer, a, w_conv, b_conv3, g_ln3, b_ln3)
        co_s, conv_s_all = _conv_sample(tl, layer, a, state_conv, w_conv, b_conv3, g_ln3, b_ln3, conv_s_all)
        x_mid, h2, rows, cols, tile_counts = _outproj(
            tl, layer, x, ro_p, ro_s, co_p, co_s, mod_seq, mod_tok, g_ffn3, w_out_bf, wr_hi, wr_lo, br_pad)
        tab, block_e = _moe_tables(lay, tile_counts[:, :, 0])
        xs = _dispatch(tl, lay, tab, h2, rows, cols)
        ys = _experts(layer, lay, block_e, xs, w_exp_gate, w_exp_up, w_exp_down)
        last = layer == depth - 1
        x = _combine(tl, lay, layer, tab, ys, cols, x_mid, mod_seq, mod_tok,
                     g_final.reshape(1, D_MODEL) if last else None)
        ret_p.append(s_p)
        conv_p.append(buf_p)
    y_p, y_s = x
    return (y_p.reshape(bp, tp, D_MODEL), y_s.reshape(bs, ts, D_MODEL),
            jnp.stack(ret_p), jnp.stack(conv_p), ret_s_all, conv_s_all)
```

```python
import functools

import jax
import jax.numpy as jnp
from jax import lax
from jax.experimental import pallas as pl
from jax.experimental.pallas import tpu as pltpu

f32 = jnp.float32
bf16 = jnp.bfloat16
i32 = jnp.int32

D_MODEL = 1024
D_RET = 512
D_CONV = 512
N_HEADS = 4
HEAD_D = 128
RET_CHUNK = 128
RET_CHUNKS_PER_STEP = 8
ROPE_BASE = 10000.0
CONV_WIDTH = 31
CONV_HALO = CONV_WIDTH - 1
N_EXPERTS = 16
N_GROUPS = 4
GROUP_SIZE = N_EXPERTS // N_GROUPS
N_MOD = 6
EPS = 1e-6
PAST_LEN = 16384
D_IN = 4 * D_RET + 2 * D_CONV

LANES = 128
SUBLANES = 8
TM = 256
STEP_TILES = 2
TS = TM * STEP_TILES
BM = 512
CONV_ROWS = 128
NORM_ROWS = 64
SAMPLE_GROUP = 16
HALO_PAD = 32
VMEM_LIMIT = 56 * 1024 * 1024


def _cparams(n_axes, vmem=VMEM_LIMIT):
    return pltpu.CompilerParams(dimension_semantics=("arbitrary",) * n_axes, vmem_limit_bytes=vmem)


def _mod_kernel(c_ref, w_ref, b_ref, seq_ref, tok_ref):
    cond = jax.nn.silu(c_ref[...]).astype(bf16)
    mod = jnp.dot(cond, w_ref[...].astype(bf16), preferred_element_type=f32) + b_ref[...]
    n_seq = seq_ref.shape[0]
    seq_ref[...] = mod[0:n_seq, :]
    tok_ref[...] = mod[n_seq:, :]


def _modulation(c_all, n_seq, w_mod, b_mod):
    depth = w_mod.shape[0]
    m = c_all.shape[0]
    n_tok = m - n_seq
    assert n_seq % SUBLANES == 0
    return pl.pallas_call(
        _mod_kernel,
        grid=(depth, N_MOD),
        in_specs=[
            pl.BlockSpec((m, D_MODEL), lambda l, j: (0, 0)),
            pl.BlockSpec((None, D_MODEL, D_MODEL), lambda l, j: (l, 0, j)),
            pl.BlockSpec((None, 1, D_MODEL), lambda l, j: (l, 0, j)),
        ],
        out_specs=[pl.BlockSpec((None, n_seq, D_MODEL), lambda l, j: (l, 0, j)),
                   pl.BlockSpec((None, n_tok, D_MODEL), lambda l, j: (l, 0, j))],
        out_shape=[jax.ShapeDtypeStruct((depth, n_seq, N_MOD * D_MODEL), f32),
                   jax.ShapeDtypeStruct((depth, n_tok, N_MOD * D_MODEL), f32)],
        compiler_params=_cparams(2),
        name="modulation",
    )(c_all, w_mod, b_mod.reshape(depth, 1, N_MOD * D_MODEL))


class _Tiles:
    def __init__(self, bp, tp, bs, ts):
        self.bp, self.tp, self.bs, self.ts = bp, tp, bs, ts
        self.n_prompt = bp * tp
        self.n_sample = bs * ts
        self.n_tok = self.n_prompt + self.n_sample
        assert tp % TS == 0 and self.n_sample % TS == 0
        self.tiles_per_seq = tp // TM
        self.np_tiles = self.n_prompt // TM
        self.n_tiles = self.n_tok // TM
        self.steps_per_seq = tp // TS
        self.np_steps = self.n_prompt // TS
        self.n_steps = self.n_tok // TS

    def prompt_block(self, i):
        return jnp.minimum(i, self.np_steps - 1)

    def sample_block(self, i):
        return jnp.maximum(i - self.np_steps, 0)

    def seq_index(self, i):
        return jnp.minimum(i // self.steps_per_seq, self.bp - 1)


def _mod_specs(tl, layer, col):
    seq = pl.BlockSpec((None, None, 1, D_MODEL), lambda i, *_: (layer, tl.seq_index(i), 0, col))
    tok = pl.BlockSpec((None, TS, D_MODEL), lambda i, *_: (layer, tl.sample_block(i), col))
    return seq, tok


def _rms(x, g):
    return x * lax.rsqrt(jnp.mean(x * x, axis=-1, keepdims=True) + EPS) * g


def _head_norm_gate(o, gn, gate):
    mu = jnp.mean(o, axis=-1, keepdims=True)
    var = jnp.mean(jnp.square(o - mu), axis=-1, keepdims=True)
    return jax.nn.silu(gate) * ((o - mu) * lax.rsqrt(var + EPS) * gn)


def _dot_nt(a, b):
    return lax.dot_general(a, b, (((1,), (1,)), ((), ())), preferred_element_type=f32)


def _dot_tn(a, b):
    return lax.dot_general(a, b, (((0,), (0,)), ((), ())), preferred_element_type=f32)


def _ret_prompt_kernel(q_ref, k_ref, v_ref, gate_ref, dec_ref, qd_ref, kd_ref, cd_ref, gn_ref,
                       o_ref, s_out_ref, s_ref):
    c = pl.program_id(1)

    @pl.when(c == 0)
    def _():
        s_ref[...] = jnp.zeros_like(s_ref)

    for ci in range(RET_CHUNKS_PER_STEP):
        rows = slice(ci * RET_CHUNK, (ci + 1) * RET_CHUNK)
        for hd in range(N_HEADS):
            sl = slice(hd * HEAD_D, (hd + 1) * HEAD_D)
            kh = k_ref[rows, sl]
            qb = q_ref[rows, sl]
            kb = kh.astype(bf16)
            vb = v_ref[rows, sl]
            s_old = s_ref[hd]
            scores = _dot_nt(qb, kb) * dec_ref[hd]
            inner = jnp.dot(scores.astype(bf16), vb, preferred_element_type=f32)
            cross = jnp.dot(qb, s_old.astype(bf16), preferred_element_type=f32) * qd_ref[hd]
            s_ref[hd] = s_old * cd_ref[hd] + _dot_tn((kh * kd_ref[hd]).astype(bf16), vb)
            o_ref[rows, sl] = _head_norm_gate(inner + cross, gn_ref[:, sl], gate_ref[rows, sl]).astype(bf16)

    @pl.when(c == pl.num_programs(1) - 1)
    def _():
        s_out_ref[...] = s_ref[...]


def _decay_tables(chunk, true_len):
    lg = jnp.log(1.0 - 2.0 ** (-5.0 - jnp.arange(N_HEADS, dtype=f32)))
    idx = jnp.arange(chunk, dtype=f32)
    rel = idx[:, None] - idx[None, :]
    decay = jnp.where(rel[None] >= 0, jnp.exp(jnp.maximum(rel, 0.0)[None] * lg[:, None, None]), 0.0)
    q_decay = jnp.exp((idx[None, :] + 1.0) * lg[:, None])
    k_decay = jnp.exp((true_len - 1.0 - idx[None, :]) * lg[:, None])
    c_decay = jnp.exp(true_len * lg)
    return decay, q_decay, k_decay, c_decay


def _retention_prompt(tl, layer, q, k, v, gate, g_ret_gn):
    step_rows = RET_CHUNK * RET_CHUNKS_PER_STEP
    assert tl.tp % step_rows == 0
    n_chunks = tl.tp // step_rows
    decay, q_decay, k_decay, c_decay = _decay_tables(RET_CHUNK, RET_CHUNK)
    bcast = lambda t: jnp.broadcast_to(t[:, :, None], (N_HEADS, RET_CHUNK, HEAD_D))
    cd = jnp.broadcast_to(c_decay[:, None, None], (N_HEADS, 1, HEAD_D))
    tok_spec = pl.BlockSpec((step_rows, D_RET), lambda b, c: (b * n_chunks + c, 0))
    tab_spec = pl.BlockSpec((N_HEADS, RET_CHUNK, HEAD_D), lambda b, c: (0, 0, 0))
    return pl.pallas_call(
        _ret_prompt_kernel,
        grid=(tl.bp, n_chunks),
        in_specs=[tok_spec] * 4 + [tab_spec] * 3 + [
            pl.BlockSpec((N_HEADS, 1, HEAD_D), lambda b, c: (0, 0, 0)),
            pl.BlockSpec((None, 1, D_RET), lambda b, c: (layer, 0, 0)),
        ],
        out_specs=[tok_spec, pl.BlockSpec((None, N_HEADS, HEAD_D, HEAD_D), lambda b, c: (b, 0, 0, 0))],
        out_shape=[jax.ShapeDtypeStruct((tl.n_prompt, D_RET), bf16),
                   jax.ShapeDtypeStruct((tl.bp, N_HEADS, HEAD_D, HEAD_D), f32)],
        scratch_shapes=[pltpu.VMEM((N_HEADS, HEAD_D, HEAD_D), f32)],
        compiler_params=_cparams(2),
        name="retention_prompt",
    )(q, k, v, gate, decay, bcast(q_decay), bcast(k_decay), cd, g_ret_gn)


def _ret_sample_kernel(q_ref, k_ref, v_ref, gate_ref, s_in_ref, dec_ref, qd_ref, kd_ref, cd_ref, gn_ref,
                       *rest, ts):
    o_ref, s_all_ref = rest[-2:]
    s_out_ref = s_all_ref.at[0]
    for other in range(1, s_all_ref.shape[0]):
        s_all_ref[other] = jnp.zeros(s_all_ref.shape[1:], f32)
    seqs_per_tile = SUBLANES // ts
    row = lax.broadcasted_iota(i32, (SUBLANES, HEAD_D), 0)
    q_all = q_ref[...].astype(f32)
    v_all = v_ref[...].astype(f32)
    outs = []
    for t in range(SAMPLE_GROUP // seqs_per_tile):
        rows = slice(t * SUBLANES, (t + 1) * SUBLANES)
        heads = []
        for hd in range(N_HEADS):
            sl = slice(hd * HEAD_D, (hd + 1) * HEAD_D)
            qh = q_all[rows, sl]
            kh = k_ref[rows, sl] * kd_ref[hd]
            vb = v_all[rows, sl].astype(bf16)
            qb = qh.astype(bf16)
            scores = _dot_nt(qb, k_ref[rows, sl].astype(bf16)) * dec_ref[hd]
            o = jnp.dot(scores.astype(bf16), vb, preferred_element_type=f32)
            for s in range(seqs_per_tile):
                b = t * seqs_per_tile + s
                mine = (row >= s * ts) & (row < (s + 1) * ts)
                s_old = s_in_ref[b, hd]
                q_s = jnp.where(mine, qh, 0.0).astype(bf16)
                k_s = jnp.where(mine, kh, 0.0).astype(bf16)
                o = o + jnp.dot(q_s, s_old.astype(bf16), preferred_element_type=f32) * qd_ref[hd]
                s_out_ref[b, hd] = s_old * cd_ref[hd] + _dot_tn(k_s, vb)
            heads.append(_head_norm_gate(o, gn_ref[:, sl], gate_ref[rows, sl]))
        outs.append(jnp.concatenate(heads, axis=-1))
    o_ref[...] = jnp.concatenate(outs, axis=0).astype(bf16)


def _retention_sample(tl, layer, q, k, v, gate, state_ret, g_ret_gn, prev_states):
    ts = tl.ts
    depth = state_ret.shape[0]
    assert SUBLANES % ts == 0 and tl.bs % SAMPLE_GROUP == 0
    seqs_per_tile = SUBLANES // ts
    decay, q_decay, k_decay, c_decay = _decay_tables(ts, ts)
    eye = jnp.eye(seqs_per_tile, dtype=f32)
    dec_tile = jnp.einsum("ab,hij->haibj", eye, decay).reshape(N_HEADS, SUBLANES, SUBLANES)
    tile_rows = lambda t: jnp.broadcast_to(jnp.tile(t, (1, seqs_per_tile))[:, :, None],
                                           (N_HEADS, SUBLANES, HEAD_D))
    cd = jnp.broadcast_to(c_decay[:, None, None], (N_HEADS, 1, HEAD_D))
    rows = SAMPLE_GROUP * ts
    first = tl.n_prompt // rows
    tok_spec = pl.BlockSpec((rows, D_RET), lambda i: (first + i, 0))
    const3 = lambda shape: pl.BlockSpec(shape, lambda i: (0, 0, 0))
    st_block = (SAMPLE_GROUP, N_HEADS, HEAD_D, HEAD_D)
    in_specs = [tok_spec] * 4 + [
        pl.BlockSpec((None,) + st_block, lambda i: (layer, i, 0, 0, 0)),
        const3((N_HEADS, SUBLANES, SUBLANES)),
        const3((N_HEADS, SUBLANES, HEAD_D)),
        const3((N_HEADS, SUBLANES, HEAD_D)),
        const3((N_HEADS, 1, HEAD_D)),
        pl.BlockSpec((None, 1, D_RET), lambda i: (layer, 0, 0)),
    ]
    args = [q, k, v, gate, state_ret, dec_tile, tile_rows(q_decay), tile_rows(k_decay), cd, g_ret_gn]
    if prev_states is None:
        state_spec = pl.BlockSpec((depth,) + st_block, lambda i: (0, i, 0, 0, 0))
        aliases = {}
    else:
        state_spec = pl.BlockSpec((1,) + st_block, lambda i: (layer, i, 0, 0, 0))
        in_specs.append(pl.BlockSpec(memory_space=pl.ANY))
        args.append(prev_states)
        aliases = {len(args) - 1: 1}
    return pl.pallas_call(
        functools.partial(_ret_sample_kernel, ts=ts),
        grid=(tl.bs // SAMPLE_GROUP,),
        in_specs=in_specs,
        out_specs=[pl.BlockSpec((rows, D_RET), lambda i: (i, 0)), state_spec],
        out_shape=[jax.ShapeDtypeStruct((tl.n_sample, D_RET), bf16),
                   jax.ShapeDtypeStruct((depth, tl.bs, N_HEADS, HEAD_D, HEAD_D), f32)],
        input_output_aliases=aliases,
        compiler_params=_cparams(1),
        name="retention_sample",
    )(*args)


def _ln_silu(cv, g, b):
    mu = jnp.mean(cv, axis=-1, keepdims=True)
    var = jnp.mean(jnp.square(cv - mu), axis=-1, keepdims=True)
    return jax.nn.silu((cv - mu) * lax.rsqrt(var + EPS) * g + b)


def _conv_taps(window, w_ref, b_ref, n_rows):
    cols = []
    for col in range(D_CONV // LANES):
        lanes = slice(col * LANES, (col + 1) * LANES)
        acc = jnp.broadcast_to(b_ref[:, lanes], (n_rows, LANES))
        for tap in range(CONV_WIDTH):
            acc = acc + window(col, tap) * w_ref[tap:tap + 1, lanes]
        cols.append(acc)
    return jnp.concatenate(cols, axis=-1)


def _conv_sample_kernel(a_ref, st_ref, w_ref, b_ref, g_ref, bl_ref, *rest, ts):
    o_ref, buf_all_ref, full_ref, cv_ref = rest[-4:]
    buf_ref = buf_all_ref.at[0]
    for other in range(1, buf_all_ref.shape[0]):
        buf_all_ref[other] = jnp.zeros(buf_all_ref.shape[1:], f32)
    for s in range(SAMPLE_GROUP):
        for col in range(D_CONV // LANES):
            lanes = slice(col * LANES, (col + 1) * LANES)
            full_ref[col, 0:CONV_HALO, :] = st_ref[:, s, lanes]
            full_ref[col, CONV_HALO:CONV_HALO + ts, :] = a_ref[s * ts:(s + 1) * ts, lanes]
            buf_ref[s, :, lanes] = full_ref[col, ts:ts + CONV_HALO, :]
        cv_ref[s * ts:(s + 1) * ts, :] = _conv_taps(lambda col, tap: full_ref[col, tap:tap + ts, :],
                                                    w_ref, b_ref, ts)
    o_ref[...] = _ln_silu(cv_ref[...], g_ref[...], bl_ref[...]).astype(bf16)


def _conv_sample(tl, layer, a, state_conv, w_conv, b_conv, g_ln, b_ln, prev_bufs):
    ts = tl.ts
    depth = state_conv.shape[0]
    rows = SAMPLE_GROUP * ts
    first_block = tl.n_prompt // rows
    vec = pl.BlockSpec((None, 1, D_CONV), lambda i: (layer, 0, 0))
    in_specs = [pl.BlockSpec((rows, D_CONV), lambda i: (first_block + i, 0)),
                pl.BlockSpec((None, CONV_HALO, SAMPLE_GROUP, D_CONV), lambda i: (layer, 0, i, 0)),
                pl.BlockSpec((None, CONV_WIDTH, D_CONV), lambda i: (layer, 0, 0)),
                vec, vec, vec]
    args = [a, state_conv.transpose(0, 2, 1, 3), w_conv, b_conv, g_ln, b_ln]
    buf_block = (SAMPLE_GROUP, CONV_HALO, D_CONV)
    if prev_bufs is None:
        buf_spec = pl.BlockSpec((depth,) + buf_block, lambda i: (0, i, 0, 0))
        aliases = {}
    else:
        buf_spec = pl.BlockSpec((1,) + buf_block, lambda i: (layer, i, 0, 0))
        in_specs.append(pl.BlockSpec(memory_space=pl.ANY))
        args.append(prev_bufs)
        aliases = {len(args) - 1: 1}
    return pl.pallas_call(
        functools.partial(_conv_sample_kernel, ts=ts),
        grid=(tl.bs // SAMPLE_GROUP,),
        in_specs=in_specs,
        out_specs=[pl.BlockSpec((rows, D_CONV), lambda i: (i, 0)), buf_spec],
        out_shape=[jax.ShapeDtypeStruct((tl.n_sample, D_CONV), bf16),
                   jax.ShapeDtypeStruct((depth, tl.bs, CONV_HALO, D_CONV), f32)],
        input_output_aliases=aliases,
        scratch_shapes=[pltpu.VMEM((D_CONV // LANES, CONV_HALO + ts + SUBLANES, LANES), f32),
                        pltpu.VMEM((rows, D_CONV), f32)],
        compiler_params=_cparams(1),
        name="conv_sample",
    )(*args)


N_CONV_COLS = D_CONV // LANES
N_CONV_CHUNKS = TM // CONV_ROWS


def _inproj_kernel(*refs, np_steps, split_x):
    if split_x:
        xp_ref, xs_ref = refs[:2]
        refs = refs[2:]
    else:
        x_ref = refs[0]
        refs = refs[1:]
    (shs_ref, sht_ref, scs_ref, sct_ref, g_ref, w_ref, cos_ref, sin_ref,
     q_ref, k_ref, v_ref, gate_ref, a_ref) = refs
    is_s = pl.program_id(0) >= np_steps

    def step(sample):
        for t in range(STEP_TILES):
            rs = slice(t * TM, (t + 1) * TM)
            if split_x:
                x = xs_ref[rs, :] if sample else xp_ref[rs, :]
            else:
                x = x_ref[rs, :]
            sh, sc = (sht_ref[rs, :], sct_ref[rs, :]) if sample else (shs_ref[...], scs_ref[...])
            hb = (_rms(x, g_ref[...]) * (1.0 + sc) + sh).astype(bf16)

            def group(g):
                return jnp.dot(hb, w_ref[:, g * D_RET:(g + 1) * D_RET], preferred_element_type=f32)

            cos = cos_ref[rs, :]
            sin = sin_ref[rs, :]

            def rope(th):
                return th * cos + pltpu.roll(th, HEAD_D // 2, 1) * sin

            qg, kg = group(0), group(1)
            for hd in range(N_HEADS):
                sl = slice(hd * HEAD_D, (hd + 1) * HEAD_D)
                q_ref[rs, sl] = rope(qg[:, sl]).astype(bf16)
                k_ref[rs, sl] = rope(kg[:, sl]) * (HEAD_D ** -0.5)
            v_ref[rs, :] = group(2).astype(bf16)
            gate_ref[rs, :] = group(3)
            a_ref[rs, :] = group(4) * jax.nn.sigmoid(group(5))

    pl.when(is_s)(functools.partial(step, True))
    pl.when(jnp.logical_not(is_s))(functools.partial(step, False))


def _conv_prompt_kernel(a_ref, w_ref, b_ref, g_ref, bl_ref, o_ref, buf_ref, full_ref, cv_ref):
    j = pl.program_id(1)

    @pl.when(j == 0)
    def _():
        full_ref[:, 0:HALO_PAD, :] = jnp.zeros((N_CONV_COLS, HALO_PAD, LANES), f32)

    @pl.when(j > 0)
    def _():
        full_ref[:, 0:HALO_PAD, :] = full_ref[:, TM:TM + HALO_PAD, :]

    for col in range(N_CONV_COLS):
        full_ref[col, HALO_PAD:HALO_PAD + TM, :] = a_ref[:, col * LANES:(col + 1) * LANES]
    shift = HALO_PAD - CONV_HALO

    def taps(idx, carry):
        col = idx // N_CONV_CHUNKS
        r0 = pl.multiple_of((idx % N_CONV_CHUNKS) * CONV_ROWS, CONV_ROWS)
        acc = jnp.broadcast_to(b_ref[col], (CONV_ROWS, LANES))
        for tap in range(CONV_WIDTH):
            acc = acc + full_ref[col, pl.ds(r0 + (tap + shift), CONV_ROWS), :] * w_ref[col, tap:tap + 1, :]
        cv_ref[col, pl.ds(r0, CONV_ROWS), :] = acc
        return carry

    lax.fori_loop(0, N_CONV_COLS * N_CONV_CHUNKS, taps, 0)

    for r0 in range(0, TM, NORM_ROWS):
        cv = jnp.concatenate([cv_ref[col, r0:r0 + NORM_ROWS, :] for col in range(N_CONV_COLS)], axis=-1)
        o_ref[r0:r0 + NORM_ROWS, :] = _ln_silu(cv, g_ref[...], bl_ref[...]).astype(bf16)

    @pl.when(j == pl.num_programs(1) - 1)
    def _():
        buf_ref[...] = a_ref[TM - CONV_HALO:TM, :]


def _conv_prompt(tl, layer, a, w_conv, b_conv, g_ln, b_ln):
    tps = tl.tiles_per_seq
    depth = w_conv.shape[0]
    w_cols = w_conv.reshape(depth, CONV_WIDTH, N_CONV_COLS, LANES).transpose(0, 2, 1, 3)
    b_cols = b_conv.reshape(depth, N_CONV_COLS, 1, LANES)
    vec = pl.BlockSpec((None, 1, D_CONV), lambda b, j: (layer, 0, 0))
    return pl.pallas_call(
        _conv_prompt_kernel,
        grid=(tl.bp, tps),
        in_specs=[pl.BlockSpec((TM, D_CONV), lambda b, j: (b * tps + j, 0)),
                  pl.BlockSpec((None, N_CONV_COLS, CONV_WIDTH, LANES), lambda b, j: (layer, 0, 0, 0)),
                  pl.BlockSpec((None, N_CONV_COLS, 1, LANES), lambda b, j: (layer, 0, 0, 0)),
                  vec, vec],
        out_specs=[pl.BlockSpec((TM, D_CONV), lambda b, j: (b * tps + j, 0)),
                   pl.BlockSpec((None, CONV_HALO, D_CONV), lambda b, j: (b, 0, 0))],
        out_shape=[jax.ShapeDtypeStruct((tl.n_prompt, D_CONV), bf16),
                   jax.ShapeDtypeStruct((tl.bp, CONV_HALO, D_CONV), f32)],
        scratch_shapes=[pltpu.VMEM((N_CONV_COLS, HALO_PAD + TM, LANES), f32),
                        pltpu.VMEM((N_CONV_COLS, TM, LANES), f32)],
        compiler_params=_cparams(2),
        name="conv_prompt",
    )(a, w_cols, b_cols, g_ln, b_ln)


def _inproj(tl, layer, x, mod_seq, mod_tok, g_norm, w_in_bf, cos_tab, sin_tab):
    split_x = isinstance(x, tuple)
    tok_spec = pl.BlockSpec((TS, D_MODEL), lambda i: (i, 0))
    if split_x:
        x_args = list(x)
        x_specs = [pl.BlockSpec((TS, D_MODEL), lambda i: (tl.prompt_block(i), 0)),
                   pl.BlockSpec((TS, D_MODEL), lambda i: (tl.sample_block(i), 0))]
    else:
        x_args, x_specs = [x], [tok_spec]
    sh_seq, sh_tok = _mod_specs(tl, layer, 0)
    sc_seq, sc_tok = _mod_specs(tl, layer, 1)

    def table_block(i):
        return jnp.where(i < tl.np_steps, i % tl.steps_per_seq, tl.steps_per_seq + tl.sample_block(i))

    tab_spec = pl.BlockSpec((TS, HEAD_D), lambda i: (table_block(i), 0))
    row_spec = pl.BlockSpec((TS, D_RET), lambda i: (i, 0))
    row_sd = lambda dt: jax.ShapeDtypeStruct((tl.n_tok, D_RET), dt)
    return pl.pallas_call(
        functools.partial(_inproj_kernel, np_steps=tl.np_steps, split_x=split_x),
        grid=(tl.n_steps,),
        in_specs=x_specs + [
            sh_seq, sh_tok, sc_seq, sc_tok,
            pl.BlockSpec((None, 1, D_MODEL), lambda i: (layer, 0, 0)),
            pl.BlockSpec((None, D_MODEL, D_IN), lambda i: (layer, 0, 0)),
            tab_spec, tab_spec,
        ],
        out_specs=[row_spec] * 5,
        out_shape=[row_sd(bf16), row_sd(f32), row_sd(bf16), row_sd(f32), row_sd(f32)],
        compiler_params=_cparams(1),
        name="inproj",
    )(*x_args, mod_seq, mod_tok, mod_seq, mod_tok, g_norm, w_in_bf, cos_tab, sin_tab)


SORT_ROWS = 2 * TM + N_EXPERTS * SUBLANES
XS_HALF = D_MODEL // 2
XS_W = XS_HALF + LANES
u32 = jnp.uint32


def _pack_bf16_pair(x):
    lo = lax.shift_right_logical(lax.bitcast_convert_type(x[:, 0:XS_HALF], u32), u32(16))
    hi = lax.bitcast_convert_type(x[:, XS_HALF:D_MODEL], u32) & u32(0xFFFF0000)
    return hi | lo


def _unpack_bf16_pair(words):
    lo = lax.bitcast_convert_type(lax.shift_left(words, u32(16)), f32).astype(bf16)
    hi = lax.bitcast_convert_type(words & u32(0xFFFF0000), f32).astype(bf16)
    return lo, hi


def _split3(x):
    a = x.astype(bf16)
    r = x - a.astype(f32)
    b = r.astype(bf16)
    c = (r - b.astype(f32)).astype(bf16)
    return a, b, c


def _first_of4(vals, m):
    return jnp.where(vals[0] == m, 0.0, jnp.where(vals[1] == m, 1.0, jnp.where(vals[2] == m, 2.0, 3.0)))


def _rows_to_tile(rows, n_rows):
    sub = lax.broadcasted_iota(i32, (n_rows, TM), 0)
    out = jnp.zeros((n_rows, TM), f32)
    for r, val in enumerate(rows):
        out = jnp.where(sub == r, val, out)
    return out


def _outproj_kernel(*refs, np_steps, split_x):
    if split_x:
        xp_ref, xs_ref = refs[:2]
        refs = refs[2:]
    else:
        x_ref = refs[0]
        refs = refs[1:]
    (retp_ref, rets_ref, cvp_ref, cvs_ref, gts_ref, gtt_ref, shs_ref, sht_ref, scs_ref, sct_ref,
     g_ref, wo_ref, wr_ref, br_ref,
     xo_ref, h2_ref, rows_ref, cols_ref, cnt_ref) = refs
    is_s = pl.program_id(0) >= np_steps

    def step(sample):
        for t in range(STEP_TILES):
            rs = slice(t * TM, (t + 1) * TM)
            if split_x:
                x = xs_ref[rs, :] if sample else xp_ref[rs, :]
            else:
                x = x_ref[rs, :]
            ret = rets_ref[rs, :] if sample else retp_ref[rs, :]
            cv = cvs_ref[rs, :] if sample else cvp_ref[rs, :]
            gt, sh, sc = ((gtt_ref[rs, :], sht_ref[rs, :], sct_ref[rs, :]) if sample
                          else (gts_ref[...], shs_ref[...], scs_ref[...]))
            mix_out = (jnp.dot(ret, wo_ref[0:D_RET, :], preferred_element_type=f32)
                       + jnp.dot(cv, wo_ref[D_RET:D_RET + D_CONV, :], preferred_element_type=f32))
            xn = x + gt * mix_out
            xo_ref[rs, :] = xn
            h2 = _rms(xn, g_ref[...]) * (1.0 + sc) + sh
            h_hi = h2.astype(bf16)
            h2_ref[rs, :] = h_hi
            rows, cols, cnt = _route_tile(h_hi, wr_ref, br_ref)
            rows_ref[t] = rows
            cols_ref[rs, :] = cols
            cnt_ref[t] = cnt

    pl.when(is_s)(functools.partial(step, True))
    pl.when(jnp.logical_not(is_s))(functools.partial(step, False))


def _route_tile(h_bf, wr_ref, br_ref):
    logits = jnp.dot(h_bf, wr_ref[...], preferred_element_type=f32) + br_ref[...]
    lt = logits.T
    row = [lt[e:e + 1, :] for e in range(N_EXPERTS)]
    top = functools.reduce(jnp.maximum, row)
    ex = [jnp.exp(r - top) for r in row]
    den = functools.reduce(jnp.add, ex)
    p = [v / den for v in ex]

    best = None
    for g in range(N_GROUPS):
        a = p[g * GROUP_SIZE:(g + 1) * GROUP_SIZE]
        m1 = functools.reduce(jnp.maximum, a)
        i1 = _first_of4(a, m1)
        b = [jnp.where(i1 == float(j), -1.0, a[j]) for j in range(GROUP_SIZE)]
        m2 = functools.reduce(jnp.maximum, b)
        i2 = _first_of4(b, m2)
        cand = (m1 + m2, m1, m2, i1 + float(g * GROUP_SIZE), i2 + float(g * GROUP_SIZE))
        if best is None:
            best = cand
        else:
            take = cand[0] > best[0]
            best = tuple(jnp.where(take, c, o) for c, o in zip(cand, best))
    _, m1, m2, e0, e1 = best
    denom = m1 + m2
    w0 = m1 / denom
    w1 = m2 / denom

    ex_id = lax.broadcasted_iota(i32, (N_EXPERTS, TM), 0).astype(f32)
    sel0 = ex_id == e0
    sel1 = ex_id == e1
    ind = jnp.where(sel0 | sel1, 1.0, 0.0)
    t_r = lax.broadcasted_iota(i32, (TM, TM), 0)
    t_c = lax.broadcasted_iota(i32, (TM, TM), 1)
    earlier = jnp.where(t_r < t_c, 1.0, 0.0).astype(bf16)
    prefix = jnp.dot(ind.astype(bf16), earlier, preferred_element_type=f32)
    cnt = jnp.sum(ind, axis=-1, keepdims=True)
    cnt8 = jnp.floor((cnt + float(SUBLANES - 1)) * (1.0 / SUBLANES)) * float(SUBLANES)
    e_r = lax.broadcasted_iota(i32, (N_EXPERTS, N_EXPERTS), 0)
    e_c = lax.broadcasted_iota(i32, (N_EXPERTS, N_EXPERTS), 1)
    below = jnp.where(e_c < e_r, 1.0, 0.0).astype(bf16)
    seg_off = jnp.dot(below, jnp.broadcast_to(cnt8, (N_EXPERTS, TM)).astype(bf16),
                      preferred_element_type=f32)
    where_to = seg_off + prefix
    pos0 = jnp.sum(jnp.where(sel0, where_to, 0.0), axis=0, keepdims=True)
    pos1 = jnp.sum(jnp.where(sel1, where_to, 0.0), axis=0, keepdims=True)

    w0p = [v.astype(f32) for v in _split3(w0)]
    w1p = [v.astype(f32) for v in _split3(w1)]
    info = [pos0, pos1] + w0p + w1p
    return (_rows_to_tile(info, SUBLANES), _rows_to_tile(info, LANES).T,
            jnp.broadcast_to(cnt, (N_EXPERTS, LANES)))


def _outproj(tl, layer, x, ret_p, ret_s, cv_p, cv_s, mod_seq, mod_tok, g_norm, w_out_bf, wr_bf, br_pad):
    split_x = isinstance(x, tuple)
    tok_spec = pl.BlockSpec((TS, D_MODEL), lambda i: (i, 0))
    p_spec = lambda w: pl.BlockSpec((TS, w), lambda i: (tl.prompt_block(i), 0))
    s_spec = lambda w: pl.BlockSpec((TS, w), lambda i: (tl.sample_block(i), 0))
    if split_x:
        x_args, x_specs = list(x), [p_spec(D_MODEL), s_spec(D_MODEL)]
    else:
        x_args, x_specs = [x], [tok_spec]
    mods = []
    for col in (2, 3, 4):
        mods += list(_mod_specs(tl, layer, col))
    wr_spec = pl.BlockSpec((D_MODEL, LANES), lambda i: (0, 0))
    return pl.pallas_call(
        functools.partial(_outproj_kernel, np_steps=tl.np_steps, split_x=split_x),
        grid=(tl.n_steps,),
        in_specs=x_specs + [p_spec(D_RET), s_spec(D_RET), p_spec(D_CONV), s_spec(D_CONV)] + mods + [
            pl.BlockSpec((None, 1, D_MODEL), lambda i: (layer, 0, 0)),
            pl.BlockSpec((None, D_MODEL, D_MODEL), lambda i: (layer, 0, 0)),
            wr_spec,
            pl.BlockSpec((1, LANES), lambda i: (0, 0)),
        ],
        out_specs=[tok_spec, tok_spec,
                   pl.BlockSpec((STEP_TILES, SUBLANES, TM), lambda i: (i, 0, 0)),
                   pl.BlockSpec((TS, LANES), lambda i: (i, 0)),
                   pl.BlockSpec((STEP_TILES, N_EXPERTS, LANES), lambda i: (i, 0, 0))],
        out_shape=[jax.ShapeDtypeStruct((tl.n_tok, D_MODEL), f32),
                   jax.ShapeDtypeStruct((tl.n_tok, D_MODEL), bf16),
                   jax.ShapeDtypeStruct((tl.n_tiles, SUBLANES, TM), f32),
                   jax.ShapeDtypeStruct((tl.n_tok, LANES), f32),
                   jax.ShapeDtypeStruct((tl.n_tiles, N_EXPERTS, LANES), f32)],
        compiler_params=_cparams(1),
        name="outproj_router",
    )(*x_args, ret_p, ret_s, cv_p, cv_s, *([mod_seq, mod_tok] * 3), g_norm, w_out_bf, wr_bf, br_pad)


N_CHUNKS = SORT_ROWS // SUBLANES


class _Layout:
    def __init__(self, n_tiles):
        self.n_tiles = n_tiles
        self.tail_start = n_tiles * N_CHUNKS
        self.tail_n8 = self.tail_start + N_EXPERTS
        worst = 2 * n_tiles * TM + n_tiles * N_EXPERTS * (SUBLANES - 1) + N_EXPERTS * (BM - SUBLANES)
        self.n_blocks = -(-worst // BM)
        self.cap = self.n_blocks * BM
        self.dump = self.cap
        self.xs_rows = self.cap + -(-2 * STEP_TILES * SORT_ROWS // BM) * BM


def _moe_tables(lay, tile_counts):
    c8 = ((tile_counts.astype(i32) + SUBLANES - 1) // SUBLANES) * SUBLANES
    base8 = jnp.cumsum(c8, axis=0) - c8
    tot8 = jnp.sum(c8, axis=0)
    region = ((tot8 + BM - 1) // BM) * BM
    g_end = jnp.cumsum(region)
    g_start = g_end - region
    seg_end = jnp.cumsum(c8, axis=1)
    seg_dst = g_start[None, :] + base8
    n_used = g_end[-1] // BM
    blk = jnp.arange(lay.n_blocks, dtype=i32)
    block_e = jnp.minimum(jnp.sum((g_end[None, :] <= blk[:, None] * BM).astype(i32), axis=1), N_EXPERTS - 1)
    block_e = jnp.where(blk < n_used, block_e, block_e[n_used - 1])
    row0 = jnp.arange(N_CHUNKS, dtype=i32) * SUBLANES
    owner = jnp.sum((seg_end[:, None, :] <= row0[None, :, None]).astype(i32), axis=-1)
    onehot = (owner[:, :, None] == jnp.arange(N_EXPERTS, dtype=i32)[None, None, :]).astype(i32)
    delta = seg_dst - (seg_end - c8)
    chunk_dst = jnp.where(owner < N_EXPERTS, row0[None, :] + jnp.sum(onehot * delta[:, None, :], axis=-1), -1)
    tab = jnp.concatenate([chunk_dst.ravel(), g_start + tot8, (region - tot8) // SUBLANES]).astype(i32)
    ids = jnp.arange(N_EXPERTS, dtype=i32)
    later = jnp.where((ids[None, :] > ids[:, None]) & (region[None, :] > 0), ids[None, :], N_EXPERTS)
    next_e = jnp.min(later, axis=1)
    next_e = jnp.where(next_e == N_EXPERTS, -1, next_e)
    owner_end = jnp.sum((block_e[:, None] == ids[None, :]).astype(i32) * (g_start + tot8)[None, :], axis=1)
    rows_used = jnp.clip(owner_end - blk * BM, 0, BM)
    return tab, jnp.concatenate([block_e, n_used[None], next_e, rows_used]).astype(i32)


def _for_chunks(n, fn):
    def body(c, carry):
        fn(c)
        return carry

    lax.fori_loop(0, n, body, 0)


def _dispatch_kernel(tab_ref, h2_ref, rows_ref, cols_ref, xs_hbm, sorted_ref, zero_ref, sem, *, lay, n_steps):
    i = pl.program_id(0)
    slot = i % 2
    step_rows = STEP_TILES * SORT_ROWS

    def tail_copy(dst):
        return pltpu.make_async_copy(zero_ref, xs_hbm.at[pl.ds(dst, SUBLANES)], sem.at[2])

    def wait_step(slot_):
        pltpu.make_async_copy(sorted_ref.at[slot_], xs_hbm.at[pl.ds(0, step_rows)], sem.at[slot_]).wait()

    @pl.when(i == 0)
    def _():
        zero_ref[...] = jnp.zeros_like(zero_ref)
        for e in range(N_EXPERTS):
            start = tab_ref[lay.tail_start + e]
            _for_chunks(tab_ref[lay.tail_n8 + e],
                        lambda c: tail_copy(pl.multiple_of(start + c * SUBLANES, SUBLANES)).start())

    @pl.when(i >= 2)
    def _():
        wait_step(slot)

    r_id = lax.broadcasted_iota(i32, (SORT_ROWS, TM), 0).astype(f32)
    lane = lax.broadcasted_iota(i32, (TM, LANES), 1)
    for t in range(STEP_TILES):
        pos0 = rows_ref[t, 0:1, :]
        pos1 = rows_ref[t, 1:2, :]
        p0 = r_id == pos0
        p1 = r_id == pos1
        perm = jnp.where(p0 | p1, 1.0, 0.0).astype(bf16)
        cols = cols_ref[t * TM:(t + 1) * TM, :]
        wpart0 = jnp.where((lane >= 2) & (lane < 5), cols, 0.0).astype(bf16)
        wpart1 = jnp.where((lane >= 5) & (lane < 8), cols, 0.0).astype(bf16)
        sw = (jnp.dot(jnp.where(p0, 1.0, 0.0).astype(bf16), wpart0, preferred_element_type=f32)
              + jnp.dot(jnp.where(p1, 1.0, 0.0).astype(bf16), wpart1, preferred_element_type=f32))
        base = t * SORT_ROWS
        sorted_ref[slot, base:base + SORT_ROWS, 0:XS_HALF] = _pack_bf16_pair(jnp.dot(
            perm, h2_ref[t * TM:(t + 1) * TM, :], preferred_element_type=f32))
        sorted_ref[slot, base:base + SORT_ROWS, XS_HALF:XS_W] = lax.bitcast_convert_type(jnp.broadcast_to(
            jnp.sum(sw, axis=-1, keepdims=True), (SORT_ROWS, LANES)), u32)

    for t in range(STEP_TILES):
        for c in range(N_CHUNKS):
            row = t * SORT_ROWS + c * SUBLANES
            dst = tab_ref[(i * STEP_TILES + t) * N_CHUNKS + c]
            dst = jnp.where(dst < 0, lay.dump + slot * step_rows + row, dst)
            pltpu.make_async_copy(sorted_ref.at[slot, pl.ds(row, SUBLANES)],
                                  xs_hbm.at[pl.ds(pl.multiple_of(dst, SUBLANES), SUBLANES)], sem.at[slot]).start()

    @pl.when(i == n_steps - 1)
    def _():
        if n_steps >= 2:
            wait_step(1 - slot)
        wait_step(slot)
        for e in range(N_EXPERTS):
            _for_chunks(tab_ref[lay.tail_n8 + e], lambda c: tail_copy(0).wait())


def _dispatch(tl, lay, tab, h2, rows, cols):
    grid_spec = pltpu.PrefetchScalarGridSpec(
        num_scalar_prefetch=1,
        grid=(tl.n_steps,),
        in_specs=[pl.BlockSpec((TS, D_MODEL), lambda i, t: (i, 0)),
                  pl.BlockSpec((STEP_TILES, SUBLANES, TM), lambda i, t: (i, 0, 0)),
                  pl.BlockSpec((TS, LANES), lambda i, t: (i, 0))],
        out_specs=pl.BlockSpec(memory_space=pl.ANY),
        scratch_shapes=[pltpu.VMEM((2, STEP_TILES * SORT_ROWS, XS_W), u32), pltpu.VMEM((SUBLANES, XS_W), u32),
                        pltpu.SemaphoreType.DMA((3,))],
    )
    return pl.pallas_call(
        functools.partial(_dispatch_kernel, lay=lay, n_steps=tl.n_steps),
        grid_spec=grid_spec,
        out_shape=jax.ShapeDtypeStruct((lay.xs_rows, XS_W), u32),
        compiler_params=_cparams(1),
        name="moe_dispatch",
    )(tab, h2, rows, cols)


def _expert_kernel(be_ref, xs_ref, wg_hbm, wu_hbm, wd_hbm, ys_ref, stage, w_bf, sem, *, n_blocks, layer):
    j = pl.program_id(0)

    def fetch(e):
        return [pltpu.make_async_copy(w.at[layer, e], stage.at[k], sem.at[k])
                for k, w in enumerate((wg_hbm, wu_hbm, wd_hbm))]

    @pl.when(j < be_ref[n_blocks])
    def _():
        e = be_ref[j]

        @pl.when(j == 0)
        def _():
            for copy in fetch(e):
                copy.start()

        @pl.when((j == 0) | (e != be_ref[jnp.maximum(j - 1, 0)]))
        def _():
            for copy in fetch(e):
                copy.wait()
            for k in range(3):
                w_bf[k] = stage[k].astype(bf16)
            nxt = be_ref[n_blocks + 1 + e]

            @pl.when(nxt >= 0)
            def _():
                for copy in fetch(nxt):
                    copy.start()

        def ffn(rows):
            x_lo, x_hi = _unpack_bf16_pair(xs_ref[rows, 0:XS_HALF])

            def first_layer(k):
                return (jnp.dot(x_lo, w_bf[k, 0:XS_HALF, :], preferred_element_type=f32)
                        + jnp.dot(x_hi, w_bf[k, XS_HALF:D_MODEL, :], preferred_element_type=f32))

            mid = (jax.nn.silu(first_layer(0)) * first_layer(1)).astype(bf16)
            slot_w = lax.bitcast_convert_type(xs_ref[rows, XS_HALF:XS_HALF + 1], f32)
            ys_ref[rows, :] = jnp.dot(mid, w_bf[2], preferred_element_type=f32) * slot_w

        half = BM // 2
        rows_used = be_ref[n_blocks + 1 + N_EXPERTS + j]

        @pl.when(rows_used > half)
        def _():
            ffn(slice(0, BM))

        @pl.when(rows_used <= half)
        def _():
            ffn(slice(0, half))
            ys_ref[half:BM, :] = jnp.zeros((BM - half, D_MODEL), f32)


def _experts(layer, lay, block_e, xs, w_gate, w_up, w_down):
    n_blocks = lay.n_blocks
    d_ff = w_gate.shape[-1]
    assert d_ff == D_MODEL
    used = lambda j, be: jnp.minimum(j, be[n_blocks] - 1)
    any_spec = pl.BlockSpec(memory_space=pl.ANY)
    grid_spec = pltpu.PrefetchScalarGridSpec(
        num_scalar_prefetch=1,
        grid=(n_blocks,),
        in_specs=[pl.BlockSpec((BM, XS_W), lambda j, be: (used(j, be), 0)), any_spec, any_spec, any_spec],
        out_specs=pl.BlockSpec((BM, D_MODEL), lambda j, be: (used(j, be), 0)),
        scratch_shapes=[pltpu.VMEM((3, D_MODEL, d_ff), f32), pltpu.VMEM((3, D_MODEL, d_ff), bf16),
                        pltpu.SemaphoreType.DMA((3,))],
    )
    return pl.pallas_call(
        functools.partial(_expert_kernel, n_blocks=n_blocks, layer=layer),
        grid_spec=grid_spec,
        out_shape=jax.ShapeDtypeStruct((lay.cap, D_MODEL), f32),
        compiler_params=_cparams(1),
        name="moe_experts",
    )(block_e, xs, w_gate, w_up, w_down)


def _combine_kernel(tab_ref, ys_hbm, cols_ref, x_ref, gts_ref, gtt_ref, *rest, n_steps, np_steps, final):
    if final:
        gf_ref, yp_ref, ysm_ref, staged, sem = rest
    else:
        xo_ref, staged, sem = rest
    i = pl.program_id(0)
    slot = i % 2
    is_s = i >= np_steps

    def start_step(step, slot_):
        for c in range(STEP_TILES * N_CHUNKS):
            src = jnp.maximum(tab_ref[step * (STEP_TILES * N_CHUNKS) + c], 0)
            pltpu.make_async_copy(ys_hbm.at[pl.ds(pl.multiple_of(src, SUBLANES), SUBLANES)],
                                  staged.at[slot_, pl.ds(c * SUBLANES, SUBLANES)], sem.at[slot_]).start()

    @pl.when(i == 0)
    def _():
        start_step(0, 0)

    @pl.when(i + 1 < n_steps)
    def _():
        start_step(i + 1, 1 - slot)

    pltpu.make_async_copy(ys_hbm.at[pl.ds(0, STEP_TILES * SORT_ROWS)], staged.at[slot], sem.at[slot]).wait()

    def step(sample):
        lane = lax.broadcasted_iota(i32, (TM, SORT_ROWS), 1).astype(f32)
        for t in range(STEP_TILES):
            rs = slice(t * TM, (t + 1) * TM)
            unperm = jnp.where((lane == cols_ref[rs, 0:1]) | (lane == cols_ref[rs, 1:2]), 1.0, 0.0).astype(bf16)
            parts = _split3(staged[slot, t * SORT_ROWS:(t + 1) * SORT_ROWS, :])
            ff = sum(jnp.dot(unperm, part, preferred_element_type=f32) for part in parts)
            xn = x_ref[rs, :] + (gtt_ref[rs, :] if sample else gts_ref[...]) * ff
            if final:
                (ysm_ref if sample else yp_ref)[rs, :] = _rms(xn, gf_ref[...])
            else:
                xo_ref[rs, :] = xn

    pl.when(is_s)(functools.partial(step, True))
    pl.when(jnp.logical_not(is_s))(functools.partial(step, False))


def _combine(tl, lay, layer, tab, ys, cols, x, mod_seq, mod_tok, g_final):
    final = g_final is not None
    tok_spec = pl.BlockSpec((TS, D_MODEL), lambda i, t: (i, 0))
    gt_seq, gt_tok = _mod_specs(tl, layer, 5)
    in_specs = [pl.BlockSpec(memory_space=pl.ANY), pl.BlockSpec((TS, LANES), lambda i, t: (i, 0)),
                tok_spec, gt_seq, gt_tok]
    args = [tab, ys, cols, x, mod_seq, mod_tok]
    if final:
        in_specs.append(pl.BlockSpec((1, D_MODEL), lambda i, t: (0, 0)))
        args.append(g_final)
        out_specs = [pl.BlockSpec((TS, D_MODEL), lambda i, t: (tl.prompt_block(i), 0)),
                     pl.BlockSpec((TS, D_MODEL), lambda i, t: (tl.sample_block(i), 0))]
        out_shape = [jax.ShapeDtypeStruct((tl.n_prompt, D_MODEL), f32),
                     jax.ShapeDtypeStruct((tl.n_sample, D_MODEL), f32)]
    else:
        out_specs = tok_spec
        out_shape = jax.ShapeDtypeStruct((tl.n_tok, D_MODEL), f32)
    grid_spec = pltpu.PrefetchScalarGridSpec(
        num_scalar_prefetch=1,
        grid=(tl.n_steps,),
        in_specs=in_specs,
        out_specs=out_specs,
        scratch_shapes=[pltpu.VMEM((2, STEP_TILES * SORT_ROWS, D_MODEL), f32), pltpu.SemaphoreType.DMA((2,))],
    )
    return pl.pallas_call(
        functools.partial(_combine_kernel, n_steps=tl.n_steps, np_steps=tl.np_steps, final=final),
        grid_spec=grid_spec,
        out_shape=out_shape,
        compiler_params=_cparams(1),
        name="moe_combine",
    )(*args)


def _rope_tables(tl):
    half = HEAD_D // 2
    inv = ROPE_BASE ** (-jnp.arange(half, dtype=f32) / half)
    pos_p = jnp.arange(tl.tp, dtype=i32)
    pos_s = PAST_LEN + jnp.arange(tl.ts, dtype=i32)
    pos = jnp.concatenate([pos_p, jnp.tile(pos_s, tl.bs)])
    ang = pos.astype(f32)[:, None] * inv[None, :]
    cos, sin = jnp.cos(ang), jnp.sin(ang)
    return jnp.concatenate([cos, cos], axis=-1), jnp.concatenate([-sin, sin], axis=-1)


def kernel(x_prompt, x_sample, state_ret, state_conv, c_prompt, c_sample, w_mod, b_mod, g_mix_norm, w_in,
           w_conv, b_conv, g_conv_ln, b_conv_ln, g_ret_gn, w_out, g_ffn_norm, w_router, b_router,
           w_exp_gate, w_exp_up, w_exp_down, g_final):
    bp, tp, _ = x_prompt.shape
    bs, ts, _ = x_sample.shape
    depth = w_mod.shape[0]
    tl = _Tiles(bp, tp, bs, ts)
    lay = _Layout(tl.n_tiles)

    c_all = jnp.concatenate([c_prompt, jnp.repeat(c_sample, ts, axis=0)], axis=0)
    mod_seq, mod_tok = _modulation(c_all, bp, w_mod, b_mod)
    mod_seq = mod_seq.reshape(depth, bp, 1, N_MOD * D_MODEL)

    cos_tab, sin_tab = _rope_tables(tl)
    w_in_bf = w_in.astype(bf16)
    w_out_bf = w_out.astype(bf16)
    wr_pad = jnp.pad(w_router.astype(f32), ((0, 0), (0, LANES - N_EXPERTS)))
    wr_bf = wr_pad.astype(bf16)
    br_pad = jnp.pad(b_router.astype(f32), (0, LANES - N_EXPERTS)).reshape(1, LANES)
    vec3 = lambda t: t.reshape(depth, 1, t.shape[-1])
    g_mix3, g_ffn3, gn3 = vec3(g_mix_norm), vec3(g_ffn_norm), vec3(g_ret_gn)
    b_conv3, g_ln3, b_ln3 = vec3(b_conv), vec3(g_conv_ln), vec3(b_conv_ln)

    x = (x_prompt.reshape(tl.n_prompt, D_MODEL), x_sample.reshape(tl.n_sample, D_MODEL))
    ret_p, conv_p = [], []
    ret_s_all = conv_s_all = None
    for layer in range(depth):
        q, k, v, gate, a = _inproj(tl, layer, x, mod_seq, mod_tok, g_mix3, w_in_bf, cos_tab, sin_tab)
        ro_p, s_p = _retention_prompt(tl, layer, q, k, v, gate, gn3)
        ro_s, ret_s_all = _retention_sample(tl, layer, q, k, v, gate, state_ret, gn3, ret_s_all)
        co_p, buf_p = _conv_prompt(tl, layer, a, w_conv, b_conv3, g_ln3, b_ln3)
        co_s, conv_s_all = _conv_sample(tl, layer, a, state_conv, w_conv, b_conv3, g_ln3, b_ln3, conv_s_all)
        x_mid, h2, rows, cols, tile_counts = _outproj(
            tl, layer, x, ro_p, ro_s, co_p, co_s, mod_seq, mod_tok, g_ffn3, w_out_bf, wr_bf, br_pad)
        tab, block_e = _moe_tables(lay, tile_counts[:, :, 0])
        xs = _dispatch(tl, lay, tab, h2, rows, cols)
        ys = _experts(layer, lay, block_e, xs, w_exp_gate, w_exp_up, w_exp_down)
        last = layer == depth - 1
        x = _combine(tl, lay, layer, tab, ys, cols, x_mid, mod_seq, mod_tok,
                     g_final.reshape(1, D_MODEL) if last else None)
        ret_p.append(s_p)
        conv_p.append(buf_p)
    y_p, y_s = x
    return (y_p.reshape(bp, tp, D_MODEL), y_s.reshape(bs, ts, D_MODEL),
            jnp.stack(ret_p), jnp.stack(conv_p), ret_s_all, conv_s_all)
```

```python
import functools

import jax
import jax.numpy as jnp
from jax import lax
from jax.experimental import pallas as pl
from jax.experimental.pallas import tpu as pltpu

f32 = jnp.float32
bf16 = jnp.bfloat16
i32 = jnp.int32

D_MODEL = 1024
D_RET = 512
D_CONV = 512
N_HEADS = 4
HEAD_D = 128
RET_CHUNK = 128
RET_CHUNKS_PER_STEP = 8
ROPE_BASE = 10000.0
CONV_WIDTH = 31
CONV_HALO = CONV_WIDTH - 1
N_EXPERTS = 16
N_GROUPS = 4
GROUP_SIZE = N_EXPERTS // N_GROUPS
N_MOD = 6
EPS = 1e-6
PAST_LEN = 16384
D_IN = 4 * D_RET + 2 * D_CONV

LANES = 128
SUBLANES = 8
TM = 256
STEP_TILES = 2
TS = TM * STEP_TILES
BM = 512
CONV_ROWS = 128
NORM_ROWS = 64
SAMPLE_GROUP = 16
HALO_PAD = 32
VMEM_LIMIT = 56 * 1024 * 1024


def _cparams(n_axes, vmem=VMEM_LIMIT):
    return pltpu.CompilerParams(dimension_semantics=("arbitrary",) * n_axes, vmem_limit_bytes=vmem)


def _mod_kernel(c_ref, w_ref, b_ref, seq_ref, tok_ref):
    cond = jax.nn.silu(c_ref[...]).astype(bf16)
    mod = jnp.dot(cond, w_ref[...].astype(bf16), preferred_element_type=f32) + b_ref[...]
    n_seq = seq_ref.shape[0]
    seq_ref[...] = mod[0:n_seq, :]
    tok_ref[...] = mod[n_seq:, :]


def _modulation(c_all, n_seq, w_mod, b_mod):
    depth = w_mod.shape[0]
    m = c_all.shape[0]
    n_tok = m - n_seq
    assert n_seq % SUBLANES == 0
    return pl.pallas_call(
        _mod_kernel,
        grid=(depth, N_MOD),
        in_specs=[
            pl.BlockSpec((m, D_MODEL), lambda l, j: (0, 0)),
            pl.BlockSpec((None, D_MODEL, D_MODEL), lambda l, j: (l, 0, j)),
            pl.BlockSpec((None, 1, D_MODEL), lambda l, j: (l, 0, j)),
        ],
        out_specs=[pl.BlockSpec((None, n_seq, D_MODEL), lambda l, j: (l, 0, j)),
                   pl.BlockSpec((None, n_tok, D_MODEL), lambda l, j: (l, 0, j))],
        out_shape=[jax.ShapeDtypeStruct((depth, n_seq, N_MOD * D_MODEL), f32),
                   jax.ShapeDtypeStruct((depth, n_tok, N_MOD * D_MODEL), f32)],
        compiler_params=_cparams(2),
        name="modulation",
    )(c_all, w_mod, b_mod.reshape(depth, 1, N_MOD * D_MODEL))


class _Tiles:
    def __init__(self, bp, tp, bs, ts):
        self.bp, self.tp, self.bs, self.ts = bp, tp, bs, ts
        self.n_prompt = bp * tp
        self.n_sample = bs * ts
        self.n_tok = self.n_prompt + self.n_sample
        assert tp % TS == 0 and self.n_sample % TS == 0
        self.tiles_per_seq = tp // TM
        self.np_tiles = self.n_prompt // TM
        self.n_tiles = self.n_tok // TM
        self.steps_per_seq = tp // TS
        self.np_steps = self.n_prompt // TS
        self.n_steps = self.n_tok // TS

    def prompt_block(self, i):
        return jnp.minimum(i, self.np_steps - 1)

    def sample_block(self, i):
        return jnp.maximum(i - self.np_steps, 0)

    def seq_index(self, i):
        return jnp.minimum(i // self.steps_per_seq, self.bp - 1)


def _mod_specs(tl, layer, col):
    seq = pl.BlockSpec((None, None, 1, D_MODEL), lambda i, *_: (layer, tl.seq_index(i), 0, col))
    tok = pl.BlockSpec((None, TS, D_MODEL), lambda i, *_: (layer, tl.sample_block(i), col))
    return seq, tok


def _rms(x, g):
    return x * lax.rsqrt(jnp.mean(x * x, axis=-1, keepdims=True) + EPS) * g


def _head_norm_gate(o, gn, gate):
    mu = jnp.mean(o, axis=-1, keepdims=True)
    var = jnp.mean(jnp.square(o - mu), axis=-1, keepdims=True)
    return jax.nn.silu(gate) * ((o - mu) * lax.rsqrt(var + EPS) * gn)


def _dot_nt(a, b):
    return lax.dot_general(a, b, (((1,), (1,)), ((), ())), preferred_element_type=f32)


def _dot_tn(a, b):
    return lax.dot_general(a, b, (((0,), (0,)), ((), ())), preferred_element_type=f32)


def _ret_prompt_kernel(q_ref, k_ref, v_ref, gate_ref, dec_ref, qd_ref, kd_ref, cd_ref, gn_ref,
                       o_ref, s_out_ref, s_ref):
    c = pl.program_id(1)

    @pl.when(c == 0)
    def _():
        s_ref[...] = jnp.zeros_like(s_ref)

    for ci in range(RET_CHUNKS_PER_STEP):
        rows = slice(ci * RET_CHUNK, (ci + 1) * RET_CHUNK)
        for hd in range(N_HEADS):
            sl = slice(hd * HEAD_D, (hd + 1) * HEAD_D)
            kh = k_ref[rows, sl]
            qb = q_ref[rows, sl]
            kb = kh.astype(bf16)
            vb = v_ref[rows, sl]
            s_old = s_ref[hd]
            scores = _dot_nt(qb, kb) * dec_ref[hd]
            inner = jnp.dot(scores.astype(bf16), vb, preferred_element_type=f32)
            cross = jnp.dot(qb, s_old.astype(bf16), preferred_element_type=f32) * qd_ref[hd]
            s_ref[hd] = s_old * cd_ref[hd] + _dot_tn((kh * kd_ref[hd]).astype(bf16), vb)
            o_ref[rows, sl] = _head_norm_gate(inner + cross, gn_ref[:, sl], gate_ref[rows, sl]).astype(bf16)

    @pl.when(c == pl.num_programs(1) - 1)
    def _():
        s_out_ref[...] = s_ref[...]


def _decay_tables(chunk, true_len):
    lg = jnp.log(1.0 - 2.0 ** (-5.0 - jnp.arange(N_HEADS, dtype=f32)))
    idx = jnp.arange(chunk, dtype=f32)
    rel = idx[:, None] - idx[None, :]
    decay = jnp.where(rel[None] >= 0, jnp.exp(jnp.maximum(rel, 0.0)[None] * lg[:, None, None]), 0.0)
    q_decay = jnp.exp((idx[None, :] + 1.0) * lg[:, None])
    k_decay = jnp.exp((true_len - 1.0 - idx[None, :]) * lg[:, None])
    c_decay = jnp.exp(true_len * lg)
    return decay, q_decay, k_decay, c_decay


def _retention_prompt(tl, layer, q, k, v, gate, g_ret_gn):
    step_rows = RET_CHUNK * RET_CHUNKS_PER_STEP
    assert tl.tp % step_rows == 0
    n_chunks = tl.tp // step_rows
    decay, q_decay, k_decay, c_decay = _decay_tables(RET_CHUNK, RET_CHUNK)
    bcast = lambda t: jnp.broadcast_to(t[:, :, None], (N_HEADS, RET_CHUNK, HEAD_D))
    cd = jnp.broadcast_to(c_decay[:, None, None], (N_HEADS, 1, HEAD_D))
    tok_spec = pl.BlockSpec((step_rows, D_RET), lambda b, c: (b * n_chunks + c, 0))
    tab_spec = pl.BlockSpec((N_HEADS, RET_CHUNK, HEAD_D), lambda b, c: (0, 0, 0))
    return pl.pallas_call(
        _ret_prompt_kernel,
        grid=(tl.bp, n_chunks),
        in_specs=[tok_spec] * 4 + [tab_spec] * 3 + [
            pl.BlockSpec((N_HEADS, 1, HEAD_D), lambda b, c: (0, 0, 0)),
            pl.BlockSpec((None, 1, D_RET), lambda b, c: (layer, 0, 0)),
        ],
        out_specs=[tok_spec, pl.BlockSpec((None, N_HEADS, HEAD_D, HEAD_D), lambda b, c: (b, 0, 0, 0))],
        out_shape=[jax.ShapeDtypeStruct((tl.n_prompt, D_RET), bf16),
                   jax.ShapeDtypeStruct((tl.bp, N_HEADS, HEAD_D, HEAD_D), f32)],
        scratch_shapes=[pltpu.VMEM((N_HEADS, HEAD_D, HEAD_D), f32)],
        compiler_params=_cparams(2),
        name="retention_prompt",
    )(q, k, v, gate, decay, bcast(q_decay), bcast(k_decay), cd, g_ret_gn)


def _ret_sample_kernel(q_ref, k_ref, v_ref, gate_ref, s_in_ref, dec_ref, qd_ref, kd_ref, cd_ref, gn_ref,
                       *rest, ts):
    o_ref, s_all_ref = rest[-2:]
    s_out_ref = s_all_ref.at[0]
    for other in range(1, s_all_ref.shape[0]):
        s_all_ref[other] = jnp.zeros(s_all_ref.shape[1:], f32)
    seqs_per_tile = SUBLANES // ts
    row = lax.broadcasted_iota(i32, (SUBLANES, HEAD_D), 0)
    q_all = q_ref[...].astype(f32)
    v_all = v_ref[...].astype(f32)
    outs = []
    for t in range(SAMPLE_GROUP // seqs_per_tile):
        rows = slice(t * SUBLANES, (t + 1) * SUBLANES)
        heads = []
        for hd in range(N_HEADS):
            sl = slice(hd * HEAD_D, (hd + 1) * HEAD_D)
            qh = q_all[rows, sl]
            kh = k_ref[rows, sl] * kd_ref[hd]
            vb = v_all[rows, sl].astype(bf16)
            qb = qh.astype(bf16)
            scores = _dot_nt(qb, k_ref[rows, sl].astype(bf16)) * dec_ref[hd]
            o = jnp.dot(scores.astype(bf16), vb, preferred_element_type=f32)
            for s in range(seqs_per_tile):
                b = t * seqs_per_tile + s
                mine = (row >= s * ts) & (row < (s + 1) * ts)
                s_old = s_in_ref[b, hd]
                q_s = jnp.where(mine, qh, 0.0).astype(bf16)
                k_s = jnp.where(mine, kh, 0.0).astype(bf16)
                o = o + jnp.dot(q_s, s_old.astype(bf16), preferred_element_type=f32) * qd_ref[hd]
                s_out_ref[b, hd] = s_old * cd_ref[hd] + _dot_tn(k_s, vb)
            heads.append(_head_norm_gate(o, gn_ref[:, sl], gate_ref[rows, sl]))
        outs.append(jnp.concatenate(heads, axis=-1))
    o_ref[...] = jnp.concatenate(outs, axis=0).astype(bf16)


def _retention_sample(tl, layer, q, k, v, gate, state_ret, g_ret_gn, prev_states):
    ts = tl.ts
    depth = state_ret.shape[0]
    assert SUBLANES % ts == 0 and tl.bs % SAMPLE_GROUP == 0
    seqs_per_tile = SUBLANES // ts
    decay, q_decay, k_decay, c_decay = _decay_tables(ts, ts)
    eye = jnp.eye(seqs_per_tile, dtype=f32)
    dec_tile = jnp.einsum("ab,hij->haibj", eye, decay).reshape(N_HEADS, SUBLANES, SUBLANES)
    tile_rows = lambda t: jnp.broadcast_to(jnp.tile(t, (1, seqs_per_tile))[:, :, None],
                                           (N_HEADS, SUBLANES, HEAD_D))
    cd = jnp.broadcast_to(c_decay[:, None, None], (N_HEADS, 1, HEAD_D))
    rows = SAMPLE_GROUP * ts
    first = tl.n_prompt // rows
    tok_spec = pl.BlockSpec((rows, D_RET), lambda i: (first + i, 0))
    const3 = lambda shape: pl.BlockSpec(shape, lambda i: (0, 0, 0))
    st_block = (SAMPLE_GROUP, N_HEADS, HEAD_D, HEAD_D)
    in_specs = [tok_spec] * 4 + [
        pl.BlockSpec((None,) + st_block, lambda i: (layer, i, 0, 0, 0)),
        const3((N_HEADS, SUBLANES, SUBLANES)),
        const3((N_HEADS, SUBLANES, HEAD_D)),
        const3((N_HEADS, SUBLANES, HEAD_D)),
        const3((N_HEADS, 1, HEAD_D)),
        pl.BlockSpec((None, 1, D_RET), lambda i: (layer, 0, 0)),
    ]
    args = [q, k, v, gate, state_ret, dec_tile, tile_rows(q_decay), tile_rows(k_decay), cd, g_ret_gn]
    if prev_states is None:
        state_spec = pl.BlockSpec((depth,) + st_block, lambda i: (0, i, 0, 0, 0))
        aliases = {}
    else:
        state_spec = pl.BlockSpec((1,) + st_block, lambda i: (layer, i, 0, 0, 0))
        in_specs.append(pl.BlockSpec(memory_space=pl.ANY))
        args.append(prev_states)
        aliases = {len(args) - 1: 1}
    return pl.pallas_call(
        functools.partial(_ret_sample_kernel, ts=ts),
        grid=(tl.bs // SAMPLE_GROUP,),
        in_specs=in_specs,
        out_specs=[pl.BlockSpec((rows, D_RET), lambda i: (i, 0)), state_spec],
        out_shape=[jax.ShapeDtypeStruct((tl.n_sample, D_RET), bf16),
                   jax.ShapeDtypeStruct((depth, tl.bs, N_HEADS, HEAD_D, HEAD_D), f32)],
        input_output_aliases=aliases,
        compiler_params=_cparams(1),
        name="retention_sample",
    )(*args)


def _ln_silu(cv, g, b):
    mu = jnp.mean(cv, axis=-1, keepdims=True)
    var = jnp.mean(jnp.square(cv - mu), axis=-1, keepdims=True)
    return jax.nn.silu((cv - mu) * lax.rsqrt(var + EPS) * g + b)


def _conv_taps(window, w_ref, b_ref, n_rows):
    cols = []
    for col in range(D_CONV // LANES):
        lanes = slice(col * LANES, (col + 1) * LANES)
        acc = jnp.broadcast_to(b_ref[:, lanes], (n_rows, LANES))
        for tap in range(CONV_WIDTH):
            acc = acc + window(col, tap) * w_ref[tap:tap + 1, lanes]
        cols.append(acc)
    return jnp.concatenate(cols, axis=-1)


def _conv_sample_kernel(a_ref, st_ref, w_ref, b_ref, g_ref, bl_ref, *rest, ts):
    o_ref, buf_all_ref, full_ref, cv_ref = rest[-4:]
    buf_ref = buf_all_ref.at[0]
    for other in range(1, buf_all_ref.shape[0]):
        buf_all_ref[other] = jnp.zeros(buf_all_ref.shape[1:], f32)
    for s in range(SAMPLE_GROUP):
        for col in range(D_CONV // LANES):
            lanes = slice(col * LANES, (col + 1) * LANES)
            full_ref[col, 0:CONV_HALO, :] = st_ref[:, s, lanes]
            full_ref[col, CONV_HALO:CONV_HALO + ts, :] = a_ref[s * ts:(s + 1) * ts, lanes]
            buf_ref[s, :, lanes] = full_ref[col, ts:ts + CONV_HALO, :]
        cv_ref[s * ts:(s + 1) * ts, :] = _conv_taps(lambda col, tap: full_ref[col, tap:tap + ts, :],
                                                    w_ref, b_ref, ts)
    o_ref[...] = _ln_silu(cv_ref[...], g_ref[...], bl_ref[...]).astype(bf16)


def _conv_sample(tl, layer, a, state_conv, w_conv, b_conv, g_ln, b_ln, prev_bufs):
    ts = tl.ts
    depth = state_conv.shape[0]
    rows = SAMPLE_GROUP * ts
    first_block = tl.n_prompt // rows
    vec = pl.BlockSpec((None, 1, D_CONV), lambda i: (layer, 0, 0))
    in_specs = [pl.BlockSpec((rows, D_CONV), lambda i: (first_block + i, 0)),
                pl.BlockSpec((None, CONV_HALO, SAMPLE_GROUP, D_CONV), lambda i: (layer, 0, i, 0)),
                pl.BlockSpec((None, CONV_WIDTH, D_CONV), lambda i: (layer, 0, 0)),
                vec, vec, vec]
    args = [a, state_conv.transpose(0, 2, 1, 3), w_conv, b_conv, g_ln, b_ln]
    buf_block = (SAMPLE_GROUP, CONV_HALO, D_CONV)
    if prev_bufs is None:
        buf_spec = pl.BlockSpec((depth,) + buf_block, lambda i: (0, i, 0, 0))
        aliases = {}
    else:
        buf_spec = pl.BlockSpec((1,) + buf_block, lambda i: (layer, i, 0, 0))
        in_specs.append(pl.BlockSpec(memory_space=pl.ANY))
        args.append(prev_bufs)
        aliases = {len(args) - 1: 1}
    return pl.pallas_call(
        functools.partial(_conv_sample_kernel, ts=ts),
        grid=(tl.bs // SAMPLE_GROUP,),
        in_specs=in_specs,
        out_specs=[pl.BlockSpec((rows, D_CONV), lambda i: (i, 0)), buf_spec],
        out_shape=[jax.ShapeDtypeStruct((tl.n_sample, D_CONV), bf16),
                   jax.ShapeDtypeStruct((depth, tl.bs, CONV_HALO, D_CONV), f32)],
        input_output_aliases=aliases,
        scratch_shapes=[pltpu.VMEM((D_CONV // LANES, CONV_HALO + ts + SUBLANES, LANES), f32),
                        pltpu.VMEM((rows, D_CONV), f32)],
        compiler_params=_cparams(1),
        name="conv_sample",
    )(*args)


N_CONV_COLS = D_CONV // LANES
N_CONV_CHUNKS = TM // CONV_ROWS


def _inproj_kernel(*refs, np_steps, split_x):
    if split_x:
        xp_ref, xs_ref = refs[:2]
        refs = refs[2:]
    else:
        x_ref = refs[0]
        refs = refs[1:]
    (shs_ref, sht_ref, scs_ref, sct_ref, g_ref, w_ref, cos_ref, sin_ref,
     q_ref, k_ref, v_ref, gate_ref, a_ref) = refs
    is_s = pl.program_id(0) >= np_steps

    def step(sample):
        for t in range(STEP_TILES):
            rs = slice(t * TM, (t + 1) * TM)
            if split_x:
                x = xs_ref[rs, :] if sample else xp_ref[rs, :]
            else:
                x = x_ref[rs, :]
            sh, sc = (sht_ref[rs, :], sct_ref[rs, :]) if sample else (shs_ref[...], scs_ref[...])
            hb = (_rms(x, g_ref[...]) * (1.0 + sc) + sh).astype(bf16)

            def group(g):
                return jnp.dot(hb, w_ref[:, g * D_RET:(g + 1) * D_RET], preferred_element_type=f32)

            cos = cos_ref[rs, :]
            sin = sin_ref[rs, :]

            def rope(th):
                return th * cos + pltpu.roll(th, HEAD_D // 2, 1) * sin

            qg, kg = group(0), group(1)
            for hd in range(N_HEADS):
                sl = slice(hd * HEAD_D, (hd + 1) * HEAD_D)
                q_ref[rs, sl] = rope(qg[:, sl]).astype(bf16)
                k_ref[rs, sl] = rope(kg[:, sl]) * (HEAD_D ** -0.5)
            v_ref[rs, :] = group(2).astype(bf16)
            gate_ref[rs, :] = group(3)
            a_ref[rs, :] = group(4) * jax.nn.sigmoid(group(5))

    pl.when(is_s)(functools.partial(step, True))
    pl.when(jnp.logical_not(is_s))(functools.partial(step, False))


def _conv_prompt_kernel(a_ref, w_ref, b_ref, g_ref, bl_ref, o_ref, buf_ref, full_ref, cv_ref):
    j = pl.program_id(1)

    @pl.when(j == 0)
    def _():
        full_ref[:, 0:HALO_PAD, :] = jnp.zeros((N_CONV_COLS, HALO_PAD, LANES), f32)

    @pl.when(j > 0)
    def _():
        full_ref[:, 0:HALO_PAD, :] = full_ref[:, TM:TM + HALO_PAD, :]

    for col in range(N_CONV_COLS):
        full_ref[col, HALO_PAD:HALO_PAD + TM, :] = a_ref[:, col * LANES:(col + 1) * LANES]
    shift = HALO_PAD - CONV_HALO

    def taps(idx, carry):
        col = idx // N_CONV_CHUNKS
        r0 = pl.multiple_of((idx % N_CONV_CHUNKS) * CONV_ROWS, CONV_ROWS)
        acc = jnp.broadcast_to(b_ref[col], (CONV_ROWS, LANES))
        for tap in range(CONV_WIDTH):
            acc = acc + full_ref[col, pl.ds(r0 + (tap + shift), CONV_ROWS), :] * w_ref[col, tap:tap + 1, :]
        cv_ref[col, pl.ds(r0, CONV_ROWS), :] = acc
        return carry

    lax.fori_loop(0, N_CONV_COLS * N_CONV_CHUNKS, taps, 0)

    for r0 in range(0, TM, NORM_ROWS):
        cv = jnp.concatenate([cv_ref[col, r0:r0 + NORM_ROWS, :] for col in range(N_CONV_COLS)], axis=-1)
        o_ref[r0:r0 + NORM_ROWS, :] = _ln_silu(cv, g_ref[...], bl_ref[...]).astype(bf16)

    @pl.when(j == pl.num_programs(1) - 1)
    def _():
        buf_ref[...] = a_ref[TM - CONV_HALO:TM, :]


def _conv_prompt(tl, layer, a, w_conv, b_conv, g_ln, b_ln):
    tps = tl.tiles_per_seq
    depth = w_conv.shape[0]
    w_cols = w_conv.reshape(depth, CONV_WIDTH, N_CONV_COLS, LANES).transpose(0, 2, 1, 3)
    b_cols = b_conv.reshape(depth, N_CONV_COLS, 1, LANES)
    vec = pl.BlockSpec((None, 1, D_CONV), lambda b, j: (layer, 0, 0))
    return pl.pallas_call(
        _conv_prompt_kernel,
        grid=(tl.bp, tps),
        in_specs=[pl.BlockSpec((TM, D_CONV), lambda b, j: (b * tps + j, 0)),
                  pl.BlockSpec((None, N_CONV_COLS, CONV_WIDTH, LANES), lambda b, j: (layer, 0, 0, 0)),
                  pl.BlockSpec((None, N_CONV_COLS, 1, LANES), lambda b, j: (layer, 0, 0, 0)),
                  vec, vec],
        out_specs=[pl.BlockSpec((TM, D_CONV), lambda b, j: (b * tps + j, 0)),
                   pl.BlockSpec((None, CONV_HALO, D_CONV), lambda b, j: (b, 0, 0))],
        out_shape=[jax.ShapeDtypeStruct((tl.n_prompt, D_CONV), bf16),
                   jax.ShapeDtypeStruct((tl.bp, CONV_HALO, D_CONV), f32)],
        scratch_shapes=[pltpu.VMEM((N_CONV_COLS, HALO_PAD + TM, LANES), f32),
                        pltpu.VMEM((N_CONV_COLS, TM, LANES), f32)],
        compiler_params=_cparams(2),
        name="conv_prompt",
    )(a, w_cols, b_cols, g_ln, b_ln)


def _inproj(tl, layer, x, mod_seq, mod_tok, g_norm, w_in_bf, cos_tab, sin_tab):
    split_x = isinstance(x, tuple)
    tok_spec = pl.BlockSpec((TS, D_MODEL), lambda i: (i, 0))
    if split_x:
        x_args = list(x)
        x_specs = [pl.BlockSpec((TS, D_MODEL), lambda i: (tl.prompt_block(i), 0)),
                   pl.BlockSpec((TS, D_MODEL), lambda i: (tl.sample_block(i), 0))]
    else:
        x_args, x_specs = [x], [tok_spec]
    sh_seq, sh_tok = _mod_specs(tl, layer, 0)
    sc_seq, sc_tok = _mod_specs(tl, layer, 1)

    def table_block(i):
        return jnp.where(i < tl.np_steps, i % tl.steps_per_seq, tl.steps_per_seq + tl.sample_block(i))

    tab_spec = pl.BlockSpec((TS, HEAD_D), lambda i: (table_block(i), 0))
    row_spec = pl.BlockSpec((TS, D_RET), lambda i: (i, 0))
    row_sd = lambda dt: jax.ShapeDtypeStruct((tl.n_tok, D_RET), dt)
    return pl.pallas_call(
        functools.partial(_inproj_kernel, np_steps=tl.np_steps, split_x=split_x),
        grid=(tl.n_steps,),
        in_specs=x_specs + [
            sh_seq, sh_tok, sc_seq, sc_tok,
            pl.BlockSpec((None, 1, D_MODEL), lambda i: (layer, 0, 0)),
            pl.BlockSpec((None, D_MODEL, D_IN), lambda i: (layer, 0, 0)),
            tab_spec, tab_spec,
        ],
        out_specs=[row_spec] * 5,
        out_shape=[row_sd(bf16), row_sd(f32), row_sd(bf16), row_sd(f32), row_sd(f32)],
        compiler_params=_cparams(1),
        name="inproj",
    )(*x_args, mod_seq, mod_tok, mod_seq, mod_tok, g_norm, w_in_bf, cos_tab, sin_tab)


SORT_ROWS = 2 * TM + N_EXPERTS * SUBLANES
XS_HALF = D_MODEL // 2
XS_W = XS_HALF + LANES
u32 = jnp.uint32


def _pack_bf16_pair(x):
    lo = lax.shift_right_logical(lax.bitcast_convert_type(x[:, 0:XS_HALF], u32), u32(16))
    hi = lax.bitcast_convert_type(x[:, XS_HALF:D_MODEL], u32) & u32(0xFFFF0000)
    return hi | lo


def _unpack_bf16_pair(words):
    lo = lax.bitcast_convert_type(lax.shift_left(words, u32(16)), f32).astype(bf16)
    hi = lax.bitcast_convert_type(words & u32(0xFFFF0000), f32).astype(bf16)
    return lo, hi


def _split3(x):
    a = x.astype(bf16)
    r = x - a.astype(f32)
    b = r.astype(bf16)
    c = (r - b.astype(f32)).astype(bf16)
    return a, b, c


def _first_of4(vals, m):
    return jnp.where(vals[0] == m, 0.0, jnp.where(vals[1] == m, 1.0, jnp.where(vals[2] == m, 2.0, 3.0)))


def _rows_to_tile(rows, n_rows):
    sub = lax.broadcasted_iota(i32, (n_rows, TM), 0)
    out = jnp.zeros((n_rows, TM), f32)
    for r, val in enumerate(rows):
        out = jnp.where(sub == r, val, out)
    return out


def _outproj_kernel(*refs, np_steps, split_x):
    if split_x:
        xp_ref, xs_ref = refs[:2]
        refs = refs[2:]
    else:
        x_ref = refs[0]
        refs = refs[1:]
    (retp_ref, rets_ref, cvp_ref, cvs_ref, gts_ref, gtt_ref, shs_ref, sht_ref, scs_ref, sct_ref,
     g_ref, wo_ref, wr_ref, br_ref,
     xo_ref, h2_ref, rows_ref, cols_ref, cnt_ref) = refs
    is_s = pl.program_id(0) >= np_steps

    def step(sample):
        for t in range(STEP_TILES):
            rs = slice(t * TM, (t + 1) * TM)
            if split_x:
                x = xs_ref[rs, :] if sample else xp_ref[rs, :]
            else:
                x = x_ref[rs, :]
            ret = rets_ref[rs, :] if sample else retp_ref[rs, :]
            cv = cvs_ref[rs, :] if sample else cvp_ref[rs, :]
            gt, sh, sc = ((gtt_ref[rs, :], sht_ref[rs, :], sct_ref[rs, :]) if sample
                          else (gts_ref[...], shs_ref[...], scs_ref[...]))
            mix_out = (jnp.dot(ret, wo_ref[0:D_RET, :], preferred_element_type=f32)
                       + jnp.dot(cv, wo_ref[D_RET:D_RET + D_CONV, :], preferred_element_type=f32))
            xn = x + gt * mix_out
            xo_ref[rs, :] = xn
            h2 = _rms(xn, g_ref[...]) * (1.0 + sc) + sh
            h_hi = h2.astype(bf16)
            h2_ref[rs, :] = h_hi
            rows, cols, cnt = _route_tile(h_hi, wr_ref, br_ref)
            rows_ref[t] = rows
            cols_ref[rs, :] = cols
            cnt_ref[t] = cnt

    pl.when(is_s)(functools.partial(step, True))
    pl.when(jnp.logical_not(is_s))(functools.partial(step, False))


def _route_tile(h_bf, wr_ref, br_ref):
    logits = jnp.dot(h_bf, wr_ref[...], preferred_element_type=f32) + br_ref[...]
    lt = logits.T
    row = [lt[e:e + 1, :] for e in range(N_EXPERTS)]
    top = functools.reduce(jnp.maximum, row)
    ex = [jnp.exp(r - top) for r in row]
    den = functools.reduce(jnp.add, ex)
    p = [v / den for v in ex]

    best = None
    for g in range(N_GROUPS):
        a = p[g * GROUP_SIZE:(g + 1) * GROUP_SIZE]
        m1 = functools.reduce(jnp.maximum, a)
        i1 = _first_of4(a, m1)
        b = [jnp.where(i1 == float(j), -1.0, a[j]) for j in range(GROUP_SIZE)]
        m2 = functools.reduce(jnp.maximum, b)
        i2 = _first_of4(b, m2)
        cand = (m1 + m2, m1, m2, i1 + float(g * GROUP_SIZE), i2 + float(g * GROUP_SIZE))
        if best is None:
            best = cand
        else:
            take = cand[0] > best[0]
            best = tuple(jnp.where(take, c, o) for c, o in zip(cand, best))
    _, m1, m2, e0, e1 = best
    denom = m1 + m2
    w0 = m1 / denom
    w1 = m2 / denom

    ex_id = lax.broadcasted_iota(i32, (N_EXPERTS, TM), 0).astype(f32)
    sel0 = ex_id == e0
    sel1 = ex_id == e1
    ind = jnp.where(sel0 | sel1, 1.0, 0.0)
    t_r = lax.broadcasted_iota(i32, (TM, TM), 0)
    t_c = lax.broadcasted_iota(i32, (TM, TM), 1)
    earlier = jnp.where(t_r < t_c, 1.0, 0.0).astype(bf16)
    prefix = jnp.dot(ind.astype(bf16), earlier, preferred_element_type=f32)
    cnt = jnp.sum(ind, axis=-1, keepdims=True)
    cnt8 = jnp.floor((cnt + float(SUBLANES - 1)) * (1.0 / SUBLANES)) * float(SUBLANES)
    e_r = lax.broadcasted_iota(i32, (N_EXPERTS, N_EXPERTS), 0)
    e_c = lax.broadcasted_iota(i32, (N_EXPERTS, N_EXPERTS), 1)
    below = jnp.where(e_c < e_r, 1.0, 0.0).astype(bf16)
    seg_off = jnp.dot(below, jnp.broadcast_to(cnt8, (N_EXPERTS, TM)).astype(bf16),
                      preferred_element_type=f32)
    where_to = seg_off + prefix
    pos0 = jnp.sum(jnp.where(sel0, where_to, 0.0), axis=0, keepdims=True)
    pos1 = jnp.sum(jnp.where(sel1, where_to, 0.0), axis=0, keepdims=True)

    w0p = [v.astype(f32) for v in _split3(w0)]
    w1p = [v.astype(f32) for v in _split3(w1)]
    info = [pos0, pos1] + w0p + w1p
    return (_rows_to_tile(info, SUBLANES), _rows_to_tile(info, LANES).T,
            jnp.broadcast_to(cnt, (N_EXPERTS, LANES)))


def _outproj(tl, layer, x, ret_p, ret_s, cv_p, cv_s, mod_seq, mod_tok, g_norm, w_out_bf, wr_bf, br_pad):
    split_x = isinstance(x, tuple)
    tok_spec = pl.BlockSpec((TS, D_MODEL), lambda i: (i, 0))
    p_spec = lambda w: pl.BlockSpec((TS, w), lambda i: (tl.prompt_block(i), 0))
    s_spec = lambda w: pl.BlockSpec((TS, w), lambda i: (tl.sample_block(i), 0))
    if split_x:
        x_args, x_specs = list(x), [p_spec(D_MODEL), s_spec(D_MODEL)]
    else:
        x_args, x_specs = [x], [tok_spec]
    mods = []
    for col in (2, 3, 4):
        mods += list(_mod_specs(tl, layer, col))
    wr_spec = pl.BlockSpec((D_MODEL, LANES), lambda i: (0, 0))
    return pl.pallas_call(
        functools.partial(_outproj_kernel, np_steps=tl.np_steps, split_x=split_x),
        grid=(tl.n_steps,),
        in_specs=x_specs + [p_spec(D_RET), s_spec(D_RET), p_spec(D_CONV), s_spec(D_CONV)] + mods + [
            pl.BlockSpec((None, 1, D_MODEL), lambda i: (layer, 0, 0)),
            pl.BlockSpec((None, D_MODEL, D_MODEL), lambda i: (layer, 0, 0)),
            wr_spec,
            pl.BlockSpec((1, LANES), lambda i: (0, 0)),
        ],
        out_specs=[tok_spec, tok_spec,
                   pl.BlockSpec((STEP_TILES, SUBLANES, TM), lambda i: (i, 0, 0)),
                   pl.BlockSpec((TS, LANES), lambda i: (i, 0)),
                   pl.BlockSpec((STEP_TILES, N_EXPERTS, LANES), lambda i: (i, 0, 0))],
        out_shape=[jax.ShapeDtypeStruct((tl.n_tok, D_MODEL), f32),
                   jax.ShapeDtypeStruct((tl.n_tok, D_MODEL), bf16),
                   jax.ShapeDtypeStruct((tl.n_tiles, SUBLANES, TM), f32),
                   jax.ShapeDtypeStruct((tl.n_tok, LANES), f32),
                   jax.ShapeDtypeStruct((tl.n_tiles, N_EXPERTS, LANES), f32)],
        compiler_params=_cparams(1),
        name="outproj_router",
    )(*x_args, ret_p, ret_s, cv_p, cv_s, *([mod_seq, mod_tok] * 3), g_norm, w_out_bf, wr_bf, br_pad)


N_CHUNKS = SORT_ROWS // SUBLANES


class _Layout:
    def __init__(self, n_tiles):
        self.n_tiles = n_tiles
        self.gather = n_tiles * N_CHUNKS
        self.tail_start = 2 * n_tiles * N_CHUNKS
        self.tail_n8 = self.tail_start + N_EXPERTS
        worst = 2 * n_tiles * TM + n_tiles * N_EXPERTS * (SUBLANES - 1) + N_EXPERTS * (BM - SUBLANES)
        self.n_blocks = -(-worst // BM)
        self.cap = self.n_blocks * BM
        self.dump = self.cap
        self.xs_rows = self.cap + -(-2 * STEP_TILES * SORT_ROWS // BM) * BM


def _moe_tables(lay, tile_counts):
    c8 = ((tile_counts.astype(i32) + SUBLANES - 1) // SUBLANES) * SUBLANES
    base8 = jnp.cumsum(c8, axis=0) - c8
    tot8 = jnp.sum(c8, axis=0)
    region = ((tot8 + BM - 1) // BM) * BM
    g_end = jnp.cumsum(region)
    g_start = g_end - region
    seg_end = jnp.cumsum(c8, axis=1)
    seg_dst = g_start[None, :] + base8
    n_used = g_end[-1] // BM
    blk = jnp.arange(lay.n_blocks, dtype=i32)
    block_e = jnp.minimum(jnp.sum((g_end[None, :] <= blk[:, None] * BM).astype(i32), axis=1), N_EXPERTS - 1)
    block_e = jnp.where(blk < n_used, block_e, block_e[n_used - 1])
    row0 = jnp.arange(N_CHUNKS, dtype=i32) * SUBLANES
    owner = jnp.sum((seg_end[:, None, :] <= row0[None, :, None]).astype(i32), axis=-1)
    onehot = (owner[:, :, None] == jnp.arange(N_EXPERTS, dtype=i32)[None, None, :]).astype(i32)
    delta = seg_dst - (seg_end - c8)
    chunk_dst = row0[None, :] + jnp.sum(onehot * delta[:, None, :], axis=-1)
    in_segment = owner < N_EXPERTS
    tile = jnp.arange(lay.n_tiles, dtype=i32)
    dump_row = (lay.dump + ((tile // STEP_TILES) % 2 * STEP_TILES + tile % STEP_TILES)[:, None] * SORT_ROWS
                + row0[None, :])
    scatter_to = jnp.where(in_segment, chunk_dst, dump_row)
    gather_from = jnp.where(in_segment, chunk_dst, 0)
    tab = jnp.concatenate([scatter_to.ravel(), gather_from.ravel(), g_start + tot8,
                           (region - tot8) // SUBLANES]).astype(i32)
    ids = jnp.arange(N_EXPERTS, dtype=i32)
    later = jnp.where((ids[None, :] > ids[:, None]) & (region[None, :] > 0), ids[None, :], N_EXPERTS)
    next_e = jnp.min(later, axis=1)
    next_e = jnp.where(next_e == N_EXPERTS, -1, next_e)
    owner_end = jnp.sum((block_e[:, None] == ids[None, :]).astype(i32) * (g_start + tot8)[None, :], axis=1)
    rows_used = jnp.clip(owner_end - blk * BM, 0, BM)
    return tab, jnp.concatenate([block_e, n_used[None], next_e, rows_used]).astype(i32)


def _for_chunks(n, fn):
    def body(c, carry):
        fn(c)
        return carry

    lax.fori_loop(0, n, body, 0)


def _dispatch_kernel(tab_ref, h2_ref, rows_ref, cols_ref, xs_hbm, sorted_ref, zero_ref, sem, *, lay, n_steps):
    i = pl.program_id(0)
    slot = i % 2
    step_rows = STEP_TILES * SORT_ROWS

    def tail_copy(dst):
        return pltpu.make_async_copy(zero_ref, xs_hbm.at[pl.ds(dst, SUBLANES)], sem.at[2])

    def wait_step(slot_):
        pltpu.make_async_copy(sorted_ref.at[slot_], xs_hbm.at[pl.ds(0, step_rows)], sem.at[slot_]).wait()

    @pl.when(i == 0)
    def _():
        zero_ref[...] = jnp.zeros_like(zero_ref)
        for e in range(N_EXPERTS):
            start = tab_ref[lay.tail_start + e]
            _for_chunks(tab_ref[lay.tail_n8 + e],
                        lambda c: tail_copy(pl.multiple_of(start + c * SUBLANES, SUBLANES)).start())

    @pl.when(i >= 2)
    def _():
        wait_step(slot)

    r_id = lax.broadcasted_iota(i32, (SORT_ROWS, TM), 0).astype(f32)
    lane = lax.broadcasted_iota(i32, (TM, LANES), 1)
    for t in range(STEP_TILES):
        pos0 = rows_ref[t, 0:1, :]
        pos1 = rows_ref[t, 1:2, :]
        p0 = r_id == pos0
        p1 = r_id == pos1
        perm = jnp.where(p0 | p1, 1.0, 0.0).astype(bf16)
        perm_k = jnp.where(p1, 2.0, jnp.where(p0, 1.0, 0.0)).astype(bf16)
        cols = cols_ref[t * TM:(t + 1) * TM, :]
        wparts = jnp.where(lane == 0, 1.0, jnp.where((lane >= 2) & (lane < 8), cols, 0.0)).astype(bf16)
        moved = jnp.dot(perm_k, wparts, preferred_element_type=f32)
        lane_s = lax.broadcasted_iota(i32, (SORT_ROWS, LANES), 1)
        w_first = jnp.sum(jnp.where((lane_s >= 2) & (lane_s < 5), moved, 0.0), axis=-1, keepdims=True)
        w_second = jnp.sum(jnp.where((lane_s >= 5) & (lane_s < 8), moved, 0.0), axis=-1, keepdims=True)
        slot_w = jnp.where(moved[:, 0:1] == 2.0, 0.5 * w_second, w_first)
        base = t * SORT_ROWS
        sorted_ref[slot, base:base + SORT_ROWS, 0:XS_HALF] = _pack_bf16_pair(jnp.dot(
            perm, h2_ref[t * TM:(t + 1) * TM, :], preferred_element_type=f32))
        sorted_ref[slot, base:base + SORT_ROWS, XS_HALF:XS_W] = lax.bitcast_convert_type(
            jnp.broadcast_to(slot_w, (SORT_ROWS, LANES)), u32)

    for t in range(STEP_TILES):
        for c in range(N_CHUNKS):
            row = t * SORT_ROWS + c * SUBLANES
            dst = tab_ref[(i * STEP_TILES + t) * N_CHUNKS + c]
            pltpu.make_async_copy(sorted_ref.at[slot, pl.ds(row, SUBLANES)],
                                  xs_hbm.at[pl.ds(pl.multiple_of(dst, SUBLANES), SUBLANES)], sem.at[slot]).start()

    @pl.when(i == n_steps - 1)
    def _():
        if n_steps >= 2:
            wait_step(1 - slot)
        wait_step(slot)
        for e in range(N_EXPERTS):
            _for_chunks(tab_ref[lay.tail_n8 + e], lambda c: tail_copy(0).wait())


def _dispatch(tl, lay, tab, h2, rows, cols):
    grid_spec = pltpu.PrefetchScalarGridSpec(
        num_scalar_prefetch=1,
        grid=(tl.n_steps,),
        in_specs=[pl.BlockSpec((TS, D_MODEL), lambda i, t: (i, 0)),
                  pl.BlockSpec((STEP_TILES, SUBLANES, TM), lambda i, t: (i, 0, 0)),
                  pl.BlockSpec((TS, LANES), lambda i, t: (i, 0))],
        out_specs=pl.BlockSpec(memory_space=pl.ANY),
        scratch_shapes=[pltpu.VMEM((2, STEP_TILES * SORT_ROWS, XS_W), u32), pltpu.VMEM((SUBLANES, XS_W), u32),
                        pltpu.SemaphoreType.DMA((3,))],
    )
    return pl.pallas_call(
        functools.partial(_dispatch_kernel, lay=lay, n_steps=tl.n_steps),
        grid_spec=grid_spec,
        out_shape=jax.ShapeDtypeStruct((lay.xs_rows, XS_W), u32),
        compiler_params=_cparams(1),
        name="moe_dispatch",
    )(tab, h2, rows, cols)


def _expert_kernel(be_ref, xs_ref, wg_hbm, wu_hbm, wd_hbm, ys_ref, stage, w_bf, sem, *, n_blocks, layer):
    j = pl.program_id(0)

    def fetch(e):
        return [pltpu.make_async_copy(w.at[layer, e], stage.at[k], sem.at[k])
                for k, w in enumerate((wg_hbm, wu_hbm, wd_hbm))]

    @pl.when(j < be_ref[n_blocks])
    def _():
        e = be_ref[j]

        @pl.when(j == 0)
        def _():
            for copy in fetch(e):
                copy.start()

        @pl.when((j == 0) | (e != be_ref[jnp.maximum(j - 1, 0)]))
        def _():
            for copy in fetch(e):
                copy.wait()
            for k in range(3):
                w_bf[k] = stage[k].astype(bf16)
            nxt = be_ref[n_blocks + 1 + e]

            @pl.when(nxt >= 0)
            def _():
                for copy in fetch(nxt):
                    copy.start()

        def ffn(rows):
            x_lo, x_hi = _unpack_bf16_pair(xs_ref[rows, 0:XS_HALF])

            def first_layer(k):
                return (jnp.dot(x_lo, w_bf[k, 0:XS_HALF, :], preferred_element_type=f32)
                        + jnp.dot(x_hi, w_bf[k, XS_HALF:D_MODEL, :], preferred_element_type=f32))

            mid = (jax.nn.silu(first_layer(0)) * first_layer(1)).astype(bf16)
            slot_w = lax.bitcast_convert_type(xs_ref[rows, XS_HALF:XS_HALF + 1], f32)
            ys_ref[rows, :] = jnp.dot(mid, w_bf[2], preferred_element_type=f32) * slot_w

        half = BM // 2
        rows_used = be_ref[n_blocks + 1 + N_EXPERTS + j]

        @pl.when(rows_used > half)
        def _():
            ffn(slice(0, BM))

        @pl.when(rows_used <= half)
        def _():
            ffn(slice(0, half))
            ys_ref[half:BM, :] = jnp.zeros((BM - half, D_MODEL), f32)


def _experts(layer, lay, block_e, xs, w_gate, w_up, w_down):
    n_blocks = lay.n_blocks
    d_ff = w_gate.shape[-1]
    assert d_ff == D_MODEL
    used = lambda j, be: jnp.minimum(j, be[n_blocks] - 1)
    any_spec = pl.BlockSpec(memory_space=pl.ANY)
    grid_spec = pltpu.PrefetchScalarGridSpec(
        num_scalar_prefetch=1,
        grid=(n_blocks,),
        in_specs=[pl.BlockSpec((BM, XS_W), lambda j, be: (used(j, be), 0)), any_spec, any_spec, any_spec],
        out_specs=pl.BlockSpec((BM, D_MODEL), lambda j, be: (used(j, be), 0)),
        scratch_shapes=[pltpu.VMEM((3, D_MODEL, d_ff), f32), pltpu.VMEM((3, D_MODEL, d_ff), bf16),
                        pltpu.SemaphoreType.DMA((3,))],
    )
    return pl.pallas_call(
        functools.partial(_expert_kernel, n_blocks=n_blocks, layer=layer),
        grid_spec=grid_spec,
        out_shape=jax.ShapeDtypeStruct((lay.cap, D_MODEL), f32),
        compiler_params=_cparams(1),
        name="moe_experts",
    )(block_e, xs, w_gate, w_up, w_down)


def _combine_kernel(tab_ref, ys_hbm, cols_ref, x_ref, gts_ref, gtt_ref, *rest, gather_base, n_steps, np_steps,
                    final):
    if final:
        gf_ref, yp_ref, ysm_ref, staged, sem = rest
    else:
        xo_ref, staged, sem = rest
    i = pl.program_id(0)
    slot = i % 2
    is_s = i >= np_steps

    def start_step(step, slot_):
        for c in range(STEP_TILES * N_CHUNKS):
            src = tab_ref[gather_base + step * (STEP_TILES * N_CHUNKS) + c]
            pltpu.make_async_copy(ys_hbm.at[pl.ds(pl.multiple_of(src, SUBLANES), SUBLANES)],
                                  staged.at[slot_, pl.ds(c * SUBLANES, SUBLANES)], sem.at[slot_]).start()

    @pl.when(i == 0)
    def _():
        start_step(0, 0)

    @pl.when(i + 1 < n_steps)
    def _():
        start_step(i + 1, 1 - slot)

    pltpu.make_async_copy(ys_hbm.at[pl.ds(0, STEP_TILES * SORT_ROWS)], staged.at[slot], sem.at[slot]).wait()

    def step(sample):
        lane = lax.broadcasted_iota(i32, (TM, SORT_ROWS), 1).astype(f32)
        for t in range(STEP_TILES):
            rs = slice(t * TM, (t + 1) * TM)
            unperm = jnp.where((lane == cols_ref[rs, 0:1]) | (lane == cols_ref[rs, 1:2]), 1.0, 0.0).astype(bf16)
            parts = _split3(staged[slot, t * SORT_ROWS:(t + 1) * SORT_ROWS, :])
            ff = sum(jnp.dot(unperm, part, preferred_element_type=f32) for part in parts)
            xn = x_ref[rs, :] + (gtt_ref[rs, :] if sample else gts_ref[...]) * ff
            if final:
                (ysm_ref if sample else yp_ref)[rs, :] = _rms(xn, gf_ref[...])
            else:
                xo_ref[rs, :] = xn

    pl.when(is_s)(functools.partial(step, True))
    pl.when(jnp.logical_not(is_s))(functools.partial(step, False))


def _combine(tl, lay, layer, tab, ys, cols, x, mod_seq, mod_tok, g_final):
    final = g_final is not None
    tok_spec = pl.BlockSpec((TS, D_MODEL), lambda i, t: (i, 0))
    gt_seq, gt_tok = _mod_specs(tl, layer, 5)
    in_specs = [pl.BlockSpec(memory_space=pl.ANY), pl.BlockSpec((TS, LANES), lambda i, t: (i, 0)),
                tok_spec, gt_seq, gt_tok]
    args = [tab, ys, cols, x, mod_seq, mod_tok]
    if final:
        in_specs.append(pl.BlockSpec((1, D_MODEL), lambda i, t: (0, 0)))
        args.append(g_final)
        out_specs = [pl.BlockSpec((TS, D_MODEL), lambda i, t: (tl.prompt_block(i), 0)),
                     pl.BlockSpec((TS, D_MODEL), lambda i, t: (tl.sample_block(i), 0))]
        out_shape = [jax.ShapeDtypeStruct((tl.n_prompt, D_MODEL), f32),
                     jax.ShapeDtypeStruct((tl.n_sample, D_MODEL), f32)]
    else:
        out_specs = tok_spec
        out_shape = jax.ShapeDtypeStruct((tl.n_tok, D_MODEL), f32)
    grid_spec = pltpu.PrefetchScalarGridSpec(
        num_scalar_prefetch=1,
        grid=(tl.n_steps,),
        in_specs=in_specs,
        out_specs=out_specs,
        scratch_shapes=[pltpu.VMEM((2, STEP_TILES * SORT_ROWS, D_MODEL), f32), pltpu.SemaphoreType.DMA((2,))],
    )
    return pl.pallas_call(
        functools.partial(_combine_kernel, gather_base=lay.gather, n_steps=tl.n_steps, np_steps=tl.np_steps,
                          final=final),
        grid_spec=grid_spec,
        out_shape=out_shape,
        compiler_params=_cparams(1),
        name="moe_combine",
    )(*args)


def _rope_tables(tl):
    half = HEAD_D // 2
    inv = ROPE_BASE ** (-jnp.arange(half, dtype=f32) / half)
    pos_p = jnp.arange(tl.tp, dtype=i32)
    pos_s = PAST_LEN + jnp.arange(tl.ts, dtype=i32)
    pos = jnp.concatenate([pos_p, jnp.tile(pos_s, tl.bs)])
    ang = pos.astype(f32)[:, None] * inv[None, :]
    cos, sin = jnp.cos(ang), jnp.sin(ang)
    return jnp.concatenate([cos, cos], axis=-1), jnp.concatenate([-sin, sin], axis=-1)


def kernel(x_prompt, x_sample, state_ret, state_conv, c_prompt, c_sample, w_mod, b_mod, g_mix_norm, w_in,
           w_conv, b_conv, g_conv_ln, b_conv_ln, g_ret_gn, w_out, g_ffn_norm, w_router, b_router,
           w_exp_gate, w_exp_up, w_exp_down, g_final):
    bp, tp, _ = x_prompt.shape
    bs, ts, _ = x_sample.shape
    depth = w_mod.shape[0]
    tl = _Tiles(bp, tp, bs, ts)
    lay = _Layout(tl.n_tiles)

    c_all = jnp.concatenate([c_prompt, jnp.repeat(c_sample, ts, axis=0)], axis=0)
    mod_seq, mod_tok = _modulation(c_all, bp, w_mod, b_mod)
    mod_seq = mod_seq.reshape(depth, bp, 1, N_MOD * D_MODEL)

    cos_tab, sin_tab = _rope_tables(tl)
    w_in_bf = w_in.astype(bf16)
    w_out_bf = w_out.astype(bf16)
    wr_pad = jnp.pad(w_router.astype(f32), ((0, 0), (0, LANES - N_EXPERTS)))
    wr_bf = wr_pad.astype(bf16)
    br_pad = jnp.pad(b_router.astype(f32), (0, LANES - N_EXPERTS)).reshape(1, LANES)
    vec3 = lambda t: t.reshape(depth, 1, t.shape[-1])
    g_mix3, g_ffn3, gn3 = vec3(g_mix_norm), vec3(g_ffn_norm), vec3(g_ret_gn)
    b_conv3, g_ln3, b_ln3 = vec3(b_conv), vec3(g_conv_ln), vec3(b_conv_ln)

    x = (x_prompt.reshape(tl.n_prompt, D_MODEL), x_sample.reshape(tl.n_sample, D_MODEL))
    ret_p, conv_p = [], []
    ret_s_all = conv_s_all = None
    for layer in range(depth):
        q, k, v, gate, a = _inproj(tl, layer, x, mod_seq, mod_tok, g_mix3, w_in_bf, cos_tab, sin_tab)
        ro_p, s_p = _retention_prompt(tl, layer, q, k, v, gate, gn3)
        ro_s, ret_s_all = _retention_sample(tl, layer, q, k, v, gate, state_ret, gn3, ret_s_all)
        co_p, buf_p = _conv_prompt(tl, layer, a, w_conv, b_conv3, g_ln3, b_ln3)
        co_s, conv_s_all = _conv_sample(tl, layer, a, state_conv, w_conv, b_conv3, g_ln3, b_ln3, conv_s_all)
        x_mid, h2, rows, cols, tile_counts = _outproj(
            tl, layer, x, ro_p, ro_s, co_p, co_s, mod_seq, mod_tok, g_ffn3, w_out_bf, wr_bf, br_pad)
        tab, block_e = _moe_tables(lay, tile_counts[:, :, 0])
        xs = _dispatch(tl, lay, tab, h2, rows, cols)
        ys = _experts(layer, lay, block_e, xs, w_exp_gate, w_exp_up, w_exp_down)
        last = layer == depth - 1
        x = _combine(tl, lay, layer, tab, ys, cols, x_mid, mod_seq, mod_tok,
                     g_final.reshape(1, D_MODEL) if last else None)
        ret_p.append(s_p)
        conv_p.append(buf_p)
    y_p, y_s = x
    return (y_p.reshape(bp, tp, D_MODEL), y_s.reshape(bs, ts, D_MODEL),
            jnp.stack(ret_p), jnp.stack(conv_p), ret_s_all, conv_s_all)
```

```python
import functools

import jax
import jax.numpy as jnp
from jax import lax
from jax.experimental import pallas as pl
from jax.experimental.pallas import tpu as pltpu

f32 = jnp.float32
bf16 = jnp.bfloat16
i32 = jnp.int32

D_MODEL = 1024
D_RET = 512
D_CONV = 512
N_HEADS = 4
HEAD_D = 128
RET_CHUNK = 128
RET_CHUNKS_PER_STEP = 16
ROPE_BASE = 10000.0
CONV_WIDTH = 31
CONV_HALO = CONV_WIDTH - 1
N_EXPERTS = 16
N_GROUPS = 4
GROUP_SIZE = N_EXPERTS // N_GROUPS
N_MOD = 6
EPS = 1e-6
PAST_LEN = 16384
D_IN = 4 * D_RET + 2 * D_CONV

LANES = 128
SUBLANES = 8
TM = 256
STEP_TILES = 2
TS = TM * STEP_TILES
BM = 512
CONV_ROWS = 256
NORM_ROWS = 64
SAMPLE_GROUP = 16
HALO_PAD = 32
VMEM_LIMIT = 56 * 1024 * 1024


def _cparams(n_axes, vmem=VMEM_LIMIT):
    return pltpu.CompilerParams(dimension_semantics=("arbitrary",) * n_axes, vmem_limit_bytes=vmem)


def _mod_kernel(c_ref, w_ref, b_ref, seq_ref, tok_ref):
    cond = jax.nn.silu(c_ref[...]).astype(bf16)
    mod = jnp.dot(cond, w_ref[...].astype(bf16), preferred_element_type=f32) + b_ref[...]
    n_seq = seq_ref.shape[0]
    seq_ref[...] = mod[0:n_seq, :]
    tok_ref[...] = mod[n_seq:, :]


def _modulation(c_all, n_seq, w_mod, b_mod):
    depth = w_mod.shape[0]
    m = c_all.shape[0]
    n_tok = m - n_seq
    assert n_seq % SUBLANES == 0
    return pl.pallas_call(
        _mod_kernel,
        grid=(depth, N_MOD),
        in_specs=[
            pl.BlockSpec((m, D_MODEL), lambda l, j: (0, 0)),
            pl.BlockSpec((None, D_MODEL, D_MODEL), lambda l, j: (l, 0, j)),
            pl.BlockSpec((None, 1, D_MODEL), lambda l, j: (l, 0, j)),
        ],
        out_specs=[pl.BlockSpec((None, n_seq, D_MODEL), lambda l, j: (l, 0, j)),
                   pl.BlockSpec((None, n_tok, D_MODEL), lambda l, j: (l, 0, j))],
        out_shape=[jax.ShapeDtypeStruct((depth, n_seq, N_MOD * D_MODEL), f32),
                   jax.ShapeDtypeStruct((depth, n_tok, N_MOD * D_MODEL), f32)],
        compiler_params=_cparams(2),
        name="modulation",
    )(c_all, w_mod, b_mod.reshape(depth, 1, N_MOD * D_MODEL))


class _Tiles:
    def __init__(self, bp, tp, bs, ts):
        self.bp, self.tp, self.bs, self.ts = bp, tp, bs, ts
        self.n_prompt = bp * tp
        self.n_sample = bs * ts
        self.n_tok = self.n_prompt + self.n_sample
        assert tp % TS == 0 and self.n_sample % TS == 0
        self.tiles_per_seq = tp // TM
        self.np_tiles = self.n_prompt // TM
        self.n_tiles = self.n_tok // TM
        self.steps_per_seq = tp // TS
        self.np_steps = self.n_prompt // TS
        self.n_steps = self.n_tok // TS

    def prompt_block(self, i):
        return jnp.minimum(i, self.np_steps - 1)

    def sample_block(self, i):
        return jnp.maximum(i - self.np_steps, 0)

    def seq_index(self, i):
        return jnp.minimum(i // self.steps_per_seq, self.bp - 1)


def _mod_specs(tl, layer, col):
    seq = pl.BlockSpec((None, None, 1, D_MODEL), lambda i, *_: (layer, tl.seq_index(i), 0, col))
    tok = pl.BlockSpec((None, TS, D_MODEL), lambda i, *_: (layer, tl.sample_block(i), col))
    return seq, tok


def _rms(x, g):
    return x * lax.rsqrt(jnp.mean(x * x, axis=-1, keepdims=True) + EPS) * g


def _head_norm_gate(o, gn, gate):
    mu = jnp.mean(o, axis=-1, keepdims=True)
    var = jnp.mean(jnp.square(o - mu), axis=-1, keepdims=True)
    return jax.nn.silu(gate) * ((o - mu) * lax.rsqrt(var + EPS) * gn)


def _dot_nt(a, b):
    return lax.dot_general(a, b, (((1,), (1,)), ((), ())), preferred_element_type=f32)


def _dot_tn(a, b):
    return lax.dot_general(a, b, (((0,), (0,)), ((), ())), preferred_element_type=f32)


def _ret_prompt_kernel(q_ref, k_ref, v_ref, gate_ref, dec_ref, qd_ref, kd_ref, cd_ref, gn_ref,
                       o_ref, s_out_ref, s_ref):
    c = pl.program_id(1)

    @pl.when(c == 0)
    def _():
        s_ref[...] = jnp.zeros_like(s_ref)

    for ci in range(RET_CHUNKS_PER_STEP):
        rows = slice(ci * RET_CHUNK, (ci + 1) * RET_CHUNK)
        for hd in range(N_HEADS):
            sl = slice(hd * HEAD_D, (hd + 1) * HEAD_D)
            kh = k_ref[rows, sl]
            qb = q_ref[rows, sl]
            kb = kh.astype(bf16)
            vb = v_ref[rows, sl]
            s_old = s_ref[hd]
            scores = _dot_nt(qb, kb) * dec_ref[hd]
            inner = jnp.dot(scores.astype(bf16), vb, preferred_element_type=f32)
            cross = jnp.dot(qb, s_old.astype(bf16), preferred_element_type=f32) * qd_ref[hd]
            s_ref[hd] = s_old * cd_ref[hd] + _dot_tn((kh * kd_ref[hd]).astype(bf16), vb)
            o_ref[rows, sl] = _head_norm_gate(inner + cross, gn_ref[:, sl], gate_ref[rows, sl]).astype(bf16)

    @pl.when(c == pl.num_programs(1) - 1)
    def _():
        s_out_ref[...] = s_ref[...]


def _decay_tables(chunk, true_len):
    lg = jnp.log(1.0 - 2.0 ** (-5.0 - jnp.arange(N_HEADS, dtype=f32)))
    idx = jnp.arange(chunk, dtype=f32)
    rel = idx[:, None] - idx[None, :]
    decay = jnp.where(rel[None] >= 0, jnp.exp(jnp.maximum(rel, 0.0)[None] * lg[:, None, None]), 0.0)
    q_decay = jnp.exp((idx[None, :] + 1.0) * lg[:, None])
    k_decay = jnp.exp((true_len - 1.0 - idx[None, :]) * lg[:, None])
    c_decay = jnp.exp(true_len * lg)
    return decay, q_decay, k_decay, c_decay


def _retention_prompt(tl, layer, q, k, v, gate, g_ret_gn):
    step_rows = RET_CHUNK * RET_CHUNKS_PER_STEP
    assert tl.tp % step_rows == 0
    n_chunks = tl.tp // step_rows
    decay, q_decay, k_decay, c_decay = _decay_tables(RET_CHUNK, RET_CHUNK)
    bcast = lambda t: jnp.broadcast_to(t[:, :, None], (N_HEADS, RET_CHUNK, HEAD_D))
    cd = jnp.broadcast_to(c_decay[:, None, None], (N_HEADS, 1, HEAD_D))
    tok_spec = pl.BlockSpec((step_rows, D_RET), lambda b, c: (b * n_chunks + c, 0))
    tab_spec = pl.BlockSpec((N_HEADS, RET_CHUNK, HEAD_D), lambda b, c: (0, 0, 0))
    return pl.pallas_call(
        _ret_prompt_kernel,
        grid=(tl.bp, n_chunks),
        in_specs=[tok_spec] * 4 + [tab_spec] * 3 + [
            pl.BlockSpec((N_HEADS, 1, HEAD_D), lambda b, c: (0, 0, 0)),
            pl.BlockSpec((None, 1, D_RET), lambda b, c: (layer, 0, 0)),
        ],
        out_specs=[tok_spec, pl.BlockSpec((None, N_HEADS, HEAD_D, HEAD_D), lambda b, c: (b, 0, 0, 0))],
        out_shape=[jax.ShapeDtypeStruct((tl.n_prompt, D_RET), bf16),
                   jax.ShapeDtypeStruct((tl.bp, N_HEADS, HEAD_D, HEAD_D), f32)],
        scratch_shapes=[pltpu.VMEM((N_HEADS, HEAD_D, HEAD_D), f32)],
        compiler_params=_cparams(2),
        name="retention_prompt",
    )(q, k, v, gate, decay, bcast(q_decay), bcast(k_decay), cd, g_ret_gn)


def _ret_sample_kernel(q_ref, k_ref, v_ref, gate_ref, s_in_ref, dec_ref, qd_ref, kd_ref, cd_ref, gn_ref,
                       *rest, ts):
    o_ref, s_all_ref = rest[-2:]
    s_out_ref = s_all_ref.at[0]
    for other in range(1, s_all_ref.shape[0]):
        s_all_ref[other] = jnp.zeros(s_all_ref.shape[1:], f32)
    seqs_per_tile = SUBLANES // ts
    row = lax.broadcasted_iota(i32, (SUBLANES, HEAD_D), 0)
    q_all = q_ref[...].astype(f32)
    v_all = v_ref[...].astype(f32)
    outs = []
    for t in range(SAMPLE_GROUP // seqs_per_tile):
        rows = slice(t * SUBLANES, (t + 1) * SUBLANES)
        heads = []
        for hd in range(N_HEADS):
            sl = slice(hd * HEAD_D, (hd + 1) * HEAD_D)
            qh = q_all[rows, sl]
            kh = k_ref[rows, sl] * kd_ref[hd]
            vb = v_all[rows, sl].astype(bf16)
            qb = qh.astype(bf16)
            scores = _dot_nt(qb, k_ref[rows, sl].astype(bf16)) * dec_ref[hd]
            o = jnp.dot(scores.astype(bf16), vb, preferred_element_type=f32)
            for s in range(seqs_per_tile):
                b = t * seqs_per_tile + s
                mine = (row >= s * ts) & (row < (s + 1) * ts)
                s_old = s_in_ref[b, hd]
                q_s = jnp.where(mine, qh, 0.0).astype(bf16)
                k_s = jnp.where(mine, kh, 0.0).astype(bf16)
                o = o + jnp.dot(q_s, s_old.astype(bf16), preferred_element_type=f32) * qd_ref[hd]
                s_out_ref[b, hd] = s_old * cd_ref[hd] + _dot_tn(k_s, vb)
            heads.append(_head_norm_gate(o, gn_ref[:, sl], gate_ref[rows, sl]))
        outs.append(jnp.concatenate(heads, axis=-1))
    o_ref[...] = jnp.concatenate(outs, axis=0).astype(bf16)


def _retention_sample(tl, layer, q, k, v, gate, state_ret, g_ret_gn, prev_states):
    ts = tl.ts
    depth = state_ret.shape[0]
    assert SUBLANES % ts == 0 and tl.bs % SAMPLE_GROUP == 0
    seqs_per_tile = SUBLANES // ts
    decay, q_decay, k_decay, c_decay = _decay_tables(ts, ts)
    eye = jnp.eye(seqs_per_tile, dtype=f32)
    dec_tile = jnp.einsum("ab,hij->haibj", eye, decay).reshape(N_HEADS, SUBLANES, SUBLANES)
    tile_rows = lambda t: jnp.broadcast_to(jnp.tile(t, (1, seqs_per_tile))[:, :, None],
                                           (N_HEADS, SUBLANES, HEAD_D))
    cd = jnp.broadcast_to(c_decay[:, None, None], (N_HEADS, 1, HEAD_D))
    rows = SAMPLE_GROUP * ts
    first = tl.n_prompt // rows
    tok_spec = pl.BlockSpec((rows, D_RET), lambda i: (first + i, 0))
    const3 = lambda shape: pl.BlockSpec(shape, lambda i: (0, 0, 0))
    st_block = (SAMPLE_GROUP, N_HEADS, HEAD_D, HEAD_D)
    in_specs = [tok_spec] * 4 + [
        pl.BlockSpec((None,) + st_block, lambda i: (layer, i, 0, 0, 0)),
        const3((N_HEADS, SUBLANES, SUBLANES)),
        const3((N_HEADS, SUBLANES, HEAD_D)),
        const3((N_HEADS, SUBLANES, HEAD_D)),
        const3((N_HEADS, 1, HEAD_D)),
        pl.BlockSpec((None, 1, D_RET), lambda i: (layer, 0, 0)),
    ]
    args = [q, k, v, gate, state_ret, dec_tile, tile_rows(q_decay), tile_rows(k_decay), cd, g_ret_gn]
    if prev_states is None:
        state_spec = pl.BlockSpec((depth,) + st_block, lambda i: (0, i, 0, 0, 0))
        aliases = {}
    else:
        state_spec = pl.BlockSpec((1,) + st_block, lambda i: (layer, i, 0, 0, 0))
        in_specs.append(pl.BlockSpec(memory_space=pl.ANY))
        args.append(prev_states)
        aliases = {len(args) - 1: 1}
    return pl.pallas_call(
        functools.partial(_ret_sample_kernel, ts=ts),
        grid=(tl.bs // SAMPLE_GROUP,),
        in_specs=in_specs,
        out_specs=[pl.BlockSpec((rows, D_RET), lambda i: (i, 0)), state_spec],
        out_shape=[jax.ShapeDtypeStruct((tl.n_sample, D_RET), bf16),
                   jax.ShapeDtypeStruct((depth, tl.bs, N_HEADS, HEAD_D, HEAD_D), f32)],
        input_output_aliases=aliases,
        compiler_params=_cparams(1),
        name="retention_sample",
    )(*args)


def _ln_silu(cv, g, b):
    mu = jnp.mean(cv, axis=-1, keepdims=True)
    var = jnp.mean(jnp.square(cv - mu), axis=-1, keepdims=True)
    return jax.nn.silu((cv - mu) * lax.rsqrt(var + EPS) * g + b)


def _conv_taps(window, w_ref, b_ref, n_rows):
    cols = []
    for col in range(D_CONV // LANES):
        lanes = slice(col * LANES, (col + 1) * LANES)
        acc = jnp.broadcast_to(b_ref[:, lanes], (n_rows, LANES))
        for tap in range(CONV_WIDTH):
            acc = acc + window(col, tap) * w_ref[tap:tap + 1, lanes]
        cols.append(acc)
    return jnp.concatenate(cols, axis=-1)


def _conv_sample_kernel(a_ref, st_ref, w_ref, b_ref, g_ref, bl_ref, *rest, ts):
    o_ref, buf_all_ref, full_ref, cv_ref = rest[-4:]
    buf_ref = buf_all_ref.at[0]
    for other in range(1, buf_all_ref.shape[0]):
        buf_all_ref[other] = jnp.zeros(buf_all_ref.shape[1:], f32)
    for s in range(SAMPLE_GROUP):
        for col in range(D_CONV // LANES):
            lanes = slice(col * LANES, (col + 1) * LANES)
            full_ref[col, 0:CONV_HALO, :] = st_ref[:, s, lanes]
            full_ref[col, CONV_HALO:CONV_HALO + ts, :] = a_ref[s * ts:(s + 1) * ts, lanes]
            buf_ref[s, :, lanes] = full_ref[col, ts:ts + CONV_HALO, :]
        cv_ref[s * ts:(s + 1) * ts, :] = _conv_taps(lambda col, tap: full_ref[col, tap:tap + ts, :],
                                                    w_ref, b_ref, ts)
    o_ref[...] = _ln_silu(cv_ref[...], g_ref[...], bl_ref[...]).astype(bf16)


def _conv_sample(tl, layer, a, state_conv, w_conv, b_conv, g_ln, b_ln, prev_bufs):
    ts = tl.ts
    depth = state_conv.shape[0]
    rows = SAMPLE_GROUP * ts
    first_block = tl.n_prompt // rows
    vec = pl.BlockSpec((None, 1, D_CONV), lambda i: (layer, 0, 0))
    in_specs = [pl.BlockSpec((rows, D_CONV), lambda i: (first_block + i, 0)),
                pl.BlockSpec((None, CONV_HALO, SAMPLE_GROUP, D_CONV), lambda i: (layer, 0, i, 0)),
                pl.BlockSpec((None, CONV_WIDTH, D_CONV), lambda i: (layer, 0, 0)),
                vec, vec, vec]
    args = [a, state_conv.transpose(0, 2, 1, 3), w_conv, b_conv, g_ln, b_ln]
    buf_block = (SAMPLE_GROUP, CONV_HALO, D_CONV)
    if prev_bufs is None:
        buf_spec = pl.BlockSpec((depth,) + buf_block, lambda i: (0, i, 0, 0))
        aliases = {}
    else:
        buf_spec = pl.BlockSpec((1,) + buf_block, lambda i: (layer, i, 0, 0))
        in_specs.append(pl.BlockSpec(memory_space=pl.ANY))
        args.append(prev_bufs)
        aliases = {len(args) - 1: 1}
    return pl.pallas_call(
        functools.partial(_conv_sample_kernel, ts=ts),
        grid=(tl.bs // SAMPLE_GROUP,),
        in_specs=in_specs,
        out_specs=[pl.BlockSpec((rows, D_CONV), lambda i: (i, 0)), buf_spec],
        out_shape=[jax.ShapeDtypeStruct((tl.n_sample, D_CONV), bf16),
                   jax.ShapeDtypeStruct((depth, tl.bs, CONV_HALO, D_CONV), f32)],
        input_output_aliases=aliases,
        scratch_shapes=[pltpu.VMEM((D_CONV // LANES, CONV_HALO + ts + SUBLANES, LANES), f32),
                        pltpu.VMEM((rows, D_CONV), f32)],
        compiler_params=_cparams(1),
        name="conv_sample",
    )(*args)


N_CONV_COLS = D_CONV // LANES
N_CONV_CHUNKS = TM // CONV_ROWS


def _inproj_kernel(*refs, np_steps, split_x):
    if split_x:
        xp_ref, xs_ref = refs[:2]
        refs = refs[2:]
    else:
        x_ref = refs[0]
        refs = refs[1:]
    (shs_ref, sht_ref, scs_ref, sct_ref, g_ref, w_ref, cos_ref, sin_ref,
     q_ref, k_ref, v_ref, gate_ref, a_ref) = refs
    is_s = pl.program_id(0) >= np_steps

    def step(sample):
        for t in range(STEP_TILES):
            rs = slice(t * TM, (t + 1) * TM)
            if split_x:
                x = xs_ref[rs, :] if sample else xp_ref[rs, :]
            else:
                x = x_ref[rs, :]
            sh, sc = (sht_ref[rs, :], sct_ref[rs, :]) if sample else (shs_ref[...], scs_ref[...])
            hb = (_rms(x, g_ref[...]) * (1.0 + sc) + sh).astype(bf16)

            def group(g):
                return jnp.dot(hb, w_ref[:, g * D_RET:(g + 1) * D_RET], preferred_element_type=f32)

            cos = cos_ref[rs, :]
            sin = sin_ref[rs, :]

            def rope(th):
                return th * cos + pltpu.roll(th, HEAD_D // 2, 1) * sin

            qg, kg = group(0), group(1)
            for hd in range(N_HEADS):
                sl = slice(hd * HEAD_D, (hd + 1) * HEAD_D)
                q_ref[rs, sl] = rope(qg[:, sl]).astype(bf16)
                k_ref[rs, sl] = rope(kg[:, sl]) * (HEAD_D ** -0.5)
            v_ref[rs, :] = group(2).astype(bf16)
            gate_ref[rs, :] = group(3)
            a_ref[rs, :] = group(4) * jax.nn.sigmoid(group(5))

    pl.when(is_s)(functools.partial(step, True))
    pl.when(jnp.logical_not(is_s))(functools.partial(step, False))


def _conv_prompt_kernel(a_ref, w_ref, b_ref, g_ref, bl_ref, o_ref, buf_ref, full_ref, cv_ref):
    j = pl.program_id(1)

    @pl.when(j == 0)
    def _():
        full_ref[:, 0:HALO_PAD, :] = jnp.zeros((N_CONV_COLS, HALO_PAD, LANES), f32)

    @pl.when(j > 0)
    def _():
        full_ref[:, 0:HALO_PAD, :] = full_ref[:, TM:TM + HALO_PAD, :]

    for col in range(N_CONV_COLS):
        full_ref[col, HALO_PAD:HALO_PAD + TM, :] = a_ref[:, col * LANES:(col + 1) * LANES]
    shift = HALO_PAD - CONV_HALO

    def taps(idx, carry):
        col = idx // N_CONV_CHUNKS
        r0 = pl.multiple_of((idx % N_CONV_CHUNKS) * CONV_ROWS, CONV_ROWS)
        acc = jnp.broadcast_to(b_ref[col], (CONV_ROWS, LANES))
        for tap in range(CONV_WIDTH):
            acc = acc + full_ref[col, pl.ds(r0 + (tap + shift), CONV_ROWS), :] * w_ref[col, tap:tap + 1, :]
        cv_ref[col, pl.ds(r0, CONV_ROWS), :] = acc
        return carry

    lax.fori_loop(0, N_CONV_COLS * N_CONV_CHUNKS, taps, 0)

    for r0 in range(0, TM, NORM_ROWS):
        cv = jnp.concatenate([cv_ref[col, r0:r0 + NORM_ROWS, :] for col in range(N_CONV_COLS)], axis=-1)
        o_ref[r0:r0 + NORM_ROWS, :] = _ln_silu(cv, g_ref[...], bl_ref[...]).astype(bf16)

    @pl.when(j == pl.num_programs(1) - 1)
    def _():
        buf_ref[...] = a_ref[TM - CONV_HALO:TM, :]


def _conv_prompt(tl, layer, a, w_conv, b_conv, g_ln, b_ln):
    tps = tl.tiles_per_seq
    depth = w_conv.shape[0]
    w_cols = w_conv.reshape(depth, CONV_WIDTH, N_CONV_COLS, LANES).transpose(0, 2, 1, 3)
    b_cols = b_conv.reshape(depth, N_CONV_COLS, 1, LANES)
    vec = pl.BlockSpec((None, 1, D_CONV), lambda b, j: (layer, 0, 0))
    return pl.pallas_call(
        _conv_prompt_kernel,
        grid=(tl.bp, tps),
        in_specs=[pl.BlockSpec((TM, D_CONV), lambda b, j: (b * tps + j, 0)),
                  pl.BlockSpec((None, N_CONV_COLS, CONV_WIDTH, LANES), lambda b, j: (layer, 0, 0, 0)),
                  pl.BlockSpec((None, N_CONV_COLS, 1, LANES), lambda b, j: (layer, 0, 0, 0)),
                  vec, vec],
        out_specs=[pl.BlockSpec((TM, D_CONV), lambda b, j: (b * tps + j, 0)),
                   pl.BlockSpec((None, CONV_HALO, D_CONV), lambda b, j: (b, 0, 0))],
        out_shape=[jax.ShapeDtypeStruct((tl.n_prompt, D_CONV), bf16),
                   jax.ShapeDtypeStruct((tl.bp, CONV_HALO, D_CONV), f32)],
        scratch_shapes=[pltpu.VMEM((N_CONV_COLS, HALO_PAD + TM, LANES), f32),
                        pltpu.VMEM((N_CONV_COLS, TM, LANES), f32)],
        compiler_params=_cparams(2),
        name="conv_prompt",
    )(a, w_cols, b_cols, g_ln, b_ln)


def _inproj(tl, layer, x, mod_seq, mod_tok, g_norm, w_in_bf, cos_tab, sin_tab):
    split_x = isinstance(x, tuple)
    tok_spec = pl.BlockSpec((TS, D_MODEL), lambda i: (i, 0))
    if split_x:
        x_args = list(x)
        x_specs = [pl.BlockSpec((TS, D_MODEL), lambda i: (tl.prompt_block(i), 0)),
                   pl.BlockSpec((TS, D_MODEL), lambda i: (tl.sample_block(i), 0))]
    else:
        x_args, x_specs = [x], [tok_spec]
    sh_seq, sh_tok = _mod_specs(tl, layer, 0)
    sc_seq, sc_tok = _mod_specs(tl, layer, 1)

    def table_block(i):
        return jnp.where(i < tl.np_steps, i % tl.steps_per_seq, tl.steps_per_seq + tl.sample_block(i))

    tab_spec = pl.BlockSpec((TS, HEAD_D), lambda i: (table_block(i), 0))
    row_spec = pl.BlockSpec((TS, D_RET), lambda i: (i, 0))
    row_sd = lambda dt: jax.ShapeDtypeStruct((tl.n_tok, D_RET), dt)
    return pl.pallas_call(
        functools.partial(_inproj_kernel, np_steps=tl.np_steps, split_x=split_x),
        grid=(tl.n_steps,),
        in_specs=x_specs + [
            sh_seq, sh_tok, sc_seq, sc_tok,
            pl.BlockSpec((None, 1, D_MODEL), lambda i: (layer, 0, 0)),
            pl.BlockSpec((None, D_MODEL, D_IN), lambda i: (layer, 0, 0)),
            tab_spec, tab_spec,
        ],
        out_specs=[row_spec] * 5,
        out_shape=[row_sd(bf16), row_sd(f32), row_sd(bf16), row_sd(f32), row_sd(f32)],
        compiler_params=_cparams(1),
        name="inproj",
    )(*x_args, mod_seq, mod_tok, mod_seq, mod_tok, g_norm, w_in_bf, cos_tab, sin_tab)


SORT_ROWS = 2 * TM + N_EXPERTS * SUBLANES
XS_HALF = D_MODEL // 2
XS_W = XS_HALF + LANES
u32 = jnp.uint32


def _pack_bf16_pair(x):
    lo = lax.shift_right_logical(lax.bitcast_convert_type(x[:, 0:XS_HALF], u32), u32(16))
    hi = lax.bitcast_convert_type(x[:, XS_HALF:D_MODEL], u32) & u32(0xFFFF0000)
    return hi | lo


def _unpack_bf16_pair(words):
    lo = lax.bitcast_convert_type(lax.shift_left(words, u32(16)), f32).astype(bf16)
    hi = lax.bitcast_convert_type(words & u32(0xFFFF0000), f32).astype(bf16)
    return lo, hi


def _split3(x):
    a = x.astype(bf16)
    r = x - a.astype(f32)
    b = r.astype(bf16)
    c = (r - b.astype(f32)).astype(bf16)
    return a, b, c


def _first_of4(vals, m):
    return jnp.where(vals[0] == m, 0.0, jnp.where(vals[1] == m, 1.0, jnp.where(vals[2] == m, 2.0, 3.0)))


def _rows_to_tile(rows, n_rows):
    sub = lax.broadcasted_iota(i32, (n_rows, TM), 0)
    out = jnp.zeros((n_rows, TM), f32)
    for r, val in enumerate(rows):
        out = jnp.where(sub == r, val, out)
    return out


def _outproj_kernel(*refs, np_steps, split_x):
    if split_x:
        xp_ref, xs_ref = refs[:2]
        refs = refs[2:]
    else:
        x_ref = refs[0]
        refs = refs[1:]
    (retp_ref, rets_ref, cvp_ref, cvs_ref, gts_ref, gtt_ref, shs_ref, sht_ref, scs_ref, sct_ref,
     g_ref, wo_ref, wr_ref, br_ref,
     xo_ref, h2_ref, rows_ref, cols_ref, cnt_ref) = refs
    is_s = pl.program_id(0) >= np_steps

    def step(sample):
        for t in range(STEP_TILES):
            rs = slice(t * TM, (t + 1) * TM)
            if split_x:
                x = xs_ref[rs, :] if sample else xp_ref[rs, :]
            else:
                x = x_ref[rs, :]
            ret = rets_ref[rs, :] if sample else retp_ref[rs, :]
            cv = cvs_ref[rs, :] if sample else cvp_ref[rs, :]
            gt, sh, sc = ((gtt_ref[rs, :], sht_ref[rs, :], sct_ref[rs, :]) if sample
                          else (gts_ref[...], shs_ref[...], scs_ref[...]))
            mix_out = (jnp.dot(ret, wo_ref[0:D_RET, :], preferred_element_type=f32)
                       + jnp.dot(cv, wo_ref[D_RET:D_RET + D_CONV, :], preferred_element_type=f32))
            xn = x + gt * mix_out
            xo_ref[rs, :] = xn
            h2 = _rms(xn, g_ref[...]) * (1.0 + sc) + sh
            h_hi = h2.astype(bf16)
            h2_ref[rs, :] = h_hi
            rows, cols, cnt = _route_tile(h_hi, wr_ref, br_ref)
            rows_ref[t] = rows
            cols_ref[rs, :] = cols
            cnt_ref[t] = cnt

    pl.when(is_s)(functools.partial(step, True))
    pl.when(jnp.logical_not(is_s))(functools.partial(step, False))


def _route_tile(h_bf, wr_ref, br_ref):
    logits = jnp.dot(h_bf, wr_ref[...], preferred_element_type=f32) + br_ref[...]
    lt = logits.T
    row = [lt[e:e + 1, :] for e in range(N_EXPERTS)]
    top = functools.reduce(jnp.maximum, row)
    ex = [jnp.exp(r - top) for r in row]
    den = functools.reduce(jnp.add, ex)
    p = [v / den for v in ex]

    best = None
    for g in range(N_GROUPS):
        a = p[g * GROUP_SIZE:(g + 1) * GROUP_SIZE]
        m1 = functools.reduce(jnp.maximum, a)
        i1 = _first_of4(a, m1)
        b = [jnp.where(i1 == float(j), -1.0, a[j]) for j in range(GROUP_SIZE)]
        m2 = functools.reduce(jnp.maximum, b)
        i2 = _first_of4(b, m2)
        cand = (m1 + m2, m1, m2, i1 + float(g * GROUP_SIZE), i2 + float(g * GROUP_SIZE))
        if best is None:
            best = cand
        else:
            take = cand[0] > best[0]
            best = tuple(jnp.where(take, c, o) for c, o in zip(cand, best))
    _, m1, m2, e0, e1 = best
    denom = m1 + m2
    w0 = m1 / denom
    w1 = m2 / denom

    ex_id = lax.broadcasted_iota(i32, (N_EXPERTS, TM), 0).astype(f32)
    sel0 = ex_id == e0
    sel1 = ex_id == e1
    ind = jnp.where(sel0 | sel1, 1.0, 0.0)
    t_r = lax.broadcasted_iota(i32, (TM, TM), 0)
    t_c = lax.broadcasted_iota(i32, (TM, TM), 1)
    earlier = jnp.where(t_r < t_c, 1.0, 0.0).astype(bf16)
    prefix = jnp.dot(ind.astype(bf16), earlier, preferred_element_type=f32)
    cnt = jnp.sum(ind, axis=-1, keepdims=True)
    cnt8 = jnp.floor((cnt + float(SUBLANES - 1)) * (1.0 / SUBLANES)) * float(SUBLANES)
    e_r = lax.broadcasted_iota(i32, (N_EXPERTS, N_EXPERTS), 0)
    e_c = lax.broadcasted_iota(i32, (N_EXPERTS, N_EXPERTS), 1)
    below = jnp.where(e_c < e_r, 1.0, 0.0).astype(bf16)
    seg_off = jnp.dot(below, jnp.broadcast_to(cnt8, (N_EXPERTS, TM)).astype(bf16),
                      preferred_element_type=f32)
    where_to = seg_off + prefix
    pos0 = jnp.sum(jnp.where(sel0, where_to, 0.0), axis=0, keepdims=True)
    pos1 = jnp.sum(jnp.where(sel1, where_to, 0.0), axis=0, keepdims=True)

    w0p = [v.astype(f32) for v in _split3(w0)]
    w1p = [v.astype(f32) for v in _split3(w1)]
    info = [pos0, pos1] + w0p + w1p
    return (_rows_to_tile(info, SUBLANES), _rows_to_tile(info, LANES).T,
            jnp.broadcast_to(cnt, (N_EXPERTS, LANES)))


def _outproj(tl, layer, x, ret_p, ret_s, cv_p, cv_s, mod_seq, mod_tok, g_norm, w_out_bf, wr_bf, br_pad):
    split_x = isinstance(x, tuple)
    tok_spec = pl.BlockSpec((TS, D_MODEL), lambda i: (i, 0))
    p_spec = lambda w: pl.BlockSpec((TS, w), lambda i: (tl.prompt_block(i), 0))
    s_spec = lambda w: pl.BlockSpec((TS, w), lambda i: (tl.sample_block(i), 0))
    if split_x:
        x_args, x_specs = list(x), [p_spec(D_MODEL), s_spec(D_MODEL)]
    else:
        x_args, x_specs = [x], [tok_spec]
    mods = []
    for col in (2, 3, 4):
        mods += list(_mod_specs(tl, layer, col))
    wr_spec = pl.BlockSpec((D_MODEL, LANES), lambda i: (0, 0))
    return pl.pallas_call(
        functools.partial(_outproj_kernel, np_steps=tl.np_steps, split_x=split_x),
        grid=(tl.n_steps,),
        in_specs=x_specs + [p_spec(D_RET), s_spec(D_RET), p_spec(D_CONV), s_spec(D_CONV)] + mods + [
            pl.BlockSpec((None, 1, D_MODEL), lambda i: (layer, 0, 0)),
            pl.BlockSpec((None, D_MODEL, D_MODEL), lambda i: (layer, 0, 0)),
            wr_spec,
            pl.BlockSpec((1, LANES), lambda i: (0, 0)),
        ],
        out_specs=[tok_spec, tok_spec,
                   pl.BlockSpec((STEP_TILES, SUBLANES, TM), lambda i: (i, 0, 0)),
                   pl.BlockSpec((TS, LANES), lambda i: (i, 0)),
                   pl.BlockSpec((STEP_TILES, N_EXPERTS, LANES), lambda i: (i, 0, 0))],
        out_shape=[jax.ShapeDtypeStruct((tl.n_tok, D_MODEL), f32),
                   jax.ShapeDtypeStruct((tl.n_tok, D_MODEL), bf16),
                   jax.ShapeDtypeStruct((tl.n_tiles, SUBLANES, TM), f32),
                   jax.ShapeDtypeStruct((tl.n_tok, LANES), f32),
                   jax.ShapeDtypeStruct((tl.n_tiles, N_EXPERTS, LANES), f32)],
        compiler_params=_cparams(1),
        name="outproj_router",
    )(*x_args, ret_p, ret_s, cv_p, cv_s, *([mod_seq, mod_tok] * 3), g_norm, w_out_bf, wr_bf, br_pad)


N_CHUNKS = SORT_ROWS // SUBLANES


class _Layout:
    def __init__(self, n_tiles):
        self.n_tiles = n_tiles
        self.tail_start = n_tiles * N_CHUNKS
        self.tail_n8 = self.tail_start + N_EXPERTS
        worst = 2 * n_tiles * TM + n_tiles * N_EXPERTS * (SUBLANES - 1) + N_EXPERTS * (BM - SUBLANES)
        self.n_blocks = -(-worst // BM)
        self.cap = self.n_blocks * BM
        self.dump = self.cap
        self.xs_rows = self.cap + -(-2 * STEP_TILES * SORT_ROWS // BM) * BM


def _moe_tables(lay, tile_counts):
    c8 = ((tile_counts.astype(i32) + SUBLANES - 1) // SUBLANES) * SUBLANES
    base8 = jnp.cumsum(c8, axis=0) - c8
    tot8 = jnp.sum(c8, axis=0)
    region = ((tot8 + BM - 1) // BM) * BM
    g_end = jnp.cumsum(region)
    g_start = g_end - region
    seg_end = jnp.cumsum(c8, axis=1)
    seg_dst = g_start[None, :] + base8
    n_used = g_end[-1] // BM
    blk = jnp.arange(lay.n_blocks, dtype=i32)
    block_e = jnp.minimum(jnp.sum((g_end[None, :] <= blk[:, None] * BM).astype(i32), axis=1), N_EXPERTS - 1)
    block_e = jnp.where(blk < n_used, block_e, block_e[n_used - 1])
    row0 = jnp.arange(N_CHUNKS, dtype=i32) * SUBLANES
    owner = jnp.sum((seg_end[:, None, :] <= row0[None, :, None]).astype(i32), axis=-1)
    onehot = (owner[:, :, None] == jnp.arange(N_EXPERTS, dtype=i32)[None, None, :]).astype(i32)
    delta = seg_dst - (seg_end - c8)
    chunk_dst = jnp.where(owner < N_EXPERTS, row0[None, :] + jnp.sum(onehot * delta[:, None, :], axis=-1), -1)
    tab = jnp.concatenate([chunk_dst.ravel(), g_start + tot8, (region - tot8) // SUBLANES]).astype(i32)
    ids = jnp.arange(N_EXPERTS, dtype=i32)
    later = jnp.where((ids[None, :] > ids[:, None]) & (region[None, :] > 0), ids[None, :], N_EXPERTS)
    next_e = jnp.min(later, axis=1)
    next_e = jnp.where(next_e == N_EXPERTS, -1, next_e)
    owner_end = jnp.sum((block_e[:, None] == ids[None, :]).astype(i32) * (g_start + tot8)[None, :], axis=1)
    rows_used = jnp.clip(owner_end - blk * BM, 0, BM)
    return tab, jnp.concatenate([block_e, n_used[None], next_e, rows_used]).astype(i32)


def _for_chunks(n, fn):
    def body(c, carry):
        fn(c)
        return carry

    lax.fori_loop(0, n, body, 0)


def _dispatch_kernel(tab_ref, h2_ref, rows_ref, cols_ref, xs_hbm, sorted_ref, zero_ref, sem, *, lay, n_steps):
    i = pl.program_id(0)
    slot = i % 2
    step_rows = STEP_TILES * SORT_ROWS

    def tail_copy(dst):
        return pltpu.make_async_copy(zero_ref, xs_hbm.at[pl.ds(dst, SUBLANES)], sem.at[2])

    def wait_step(slot_):
        pltpu.make_async_copy(sorted_ref.at[slot_], xs_hbm.at[pl.ds(0, step_rows)], sem.at[slot_]).wait()

    @pl.when(i == 0)
    def _():
        zero_ref[...] = jnp.zeros_like(zero_ref)
        for e in range(N_EXPERTS):
            start = tab_ref[lay.tail_start + e]
            _for_chunks(tab_ref[lay.tail_n8 + e],
                        lambda c: tail_copy(pl.multiple_of(start + c * SUBLANES, SUBLANES)).start())

    @pl.when(i >= 2)
    def _():
        wait_step(slot)

    r_id = lax.broadcasted_iota(i32, (SORT_ROWS, TM), 0).astype(f32)
    lane = lax.broadcasted_iota(i32, (TM, LANES), 1)
    for t in range(STEP_TILES):
        pos0 = rows_ref[t, 0:1, :]
        pos1 = rows_ref[t, 1:2, :]
        p0 = r_id == pos0
        p1 = r_id == pos1
        perm = jnp.where(p0 | p1, 1.0, 0.0).astype(bf16)
        cols = cols_ref[t * TM:(t + 1) * TM, :]
        wpart0 = jnp.where((lane >= 2) & (lane < 5), cols, 0.0).astype(bf16)
        wpart1 = jnp.where((lane >= 5) & (lane < 8), cols, 0.0).astype(bf16)
        sw = (jnp.dot(jnp.where(p0, 1.0, 0.0).astype(bf16), wpart0, preferred_element_type=f32)
              + jnp.dot(jnp.where(p1, 1.0, 0.0).astype(bf16), wpart1, preferred_element_type=f32))
        base = t * SORT_ROWS
        sorted_ref[slot, base:base + SORT_ROWS, 0:XS_HALF] = _pack_bf16_pair(jnp.dot(
            perm, h2_ref[t * TM:(t + 1) * TM, :], preferred_element_type=f32))
        sorted_ref[slot, base:base + SORT_ROWS, XS_HALF:XS_W] = lax.bitcast_convert_type(jnp.broadcast_to(
            jnp.sum(sw, axis=-1, keepdims=True), (SORT_ROWS, LANES)), u32)

    for t in range(STEP_TILES):
        for c in range(N_CHUNKS):
            row = t * SORT_ROWS + c * SUBLANES
            dst = tab_ref[(i * STEP_TILES + t) * N_CHUNKS + c]
            dst = jnp.where(dst < 0, lay.dump + slot * step_rows + row, dst)
            pltpu.make_async_copy(sorted_ref.at[slot, pl.ds(row, SUBLANES)],
                                  xs_hbm.at[pl.ds(pl.multiple_of(dst, SUBLANES), SUBLANES)], sem.at[slot]).start()

    @pl.when(i == n_steps - 1)
    def _():
        if n_steps >= 2:
            wait_step(1 - slot)
        wait_step(slot)
        for e in range(N_EXPERTS):
            _for_chunks(tab_ref[lay.tail_n8 + e], lambda c: tail_copy(0).wait())


def _dispatch(tl, lay, tab, h2, rows, cols):
    grid_spec = pltpu.PrefetchScalarGridSpec(
        num_scalar_prefetch=1,
        grid=(tl.n_steps,),
        in_specs=[pl.BlockSpec((TS, D_MODEL), lambda i, t: (i, 0)),
                  pl.BlockSpec((STEP_TILES, SUBLANES, TM), lambda i, t: (i, 0, 0)),
                  pl.BlockSpec((TS, LANES), lambda i, t: (i, 0))],
        out_specs=pl.BlockSpec(memory_space=pl.ANY),
        scratch_shapes=[pltpu.VMEM((2, STEP_TILES * SORT_ROWS, XS_W), u32), pltpu.VMEM((SUBLANES, XS_W), u32),
                        pltpu.SemaphoreType.DMA((3,))],
    )
    return pl.pallas_call(
        functools.partial(_dispatch_kernel, lay=lay, n_steps=tl.n_steps),
        grid_spec=grid_spec,
        out_shape=jax.ShapeDtypeStruct((lay.xs_rows, XS_W), u32),
        compiler_params=_cparams(1),
        name="moe_dispatch",
    )(tab, h2, rows, cols)


def _expert_kernel(be_ref, xs_ref, wg_hbm, wu_hbm, wd_hbm, ys_ref, stage, w_bf, sem, *, n_blocks, layer):
    j = pl.program_id(0)

    def fetch(e):
        return [pltpu.make_async_copy(w.at[layer, e], stage.at[k], sem.at[k])
                for k, w in enumerate((wg_hbm, wu_hbm, wd_hbm))]

    @pl.when(j < be_ref[n_blocks])
    def _():
        e = be_ref[j]

        @pl.when(j == 0)
        def _():
            for copy in fetch(e):
                copy.start()

        @pl.when((j == 0) | (e != be_ref[jnp.maximum(j - 1, 0)]))
        def _():
            for copy in fetch(e):
                copy.wait()
            for k in range(3):
                w_bf[k] = stage[k].astype(bf16)
            nxt = be_ref[n_blocks + 1 + e]

            @pl.when(nxt >= 0)
            def _():
                for copy in fetch(nxt):
                    copy.start()

        def ffn(rows):
            x_lo, x_hi = _unpack_bf16_pair(xs_ref[rows, 0:XS_HALF])

            def first_layer(k):
                return (jnp.dot(x_lo, w_bf[k, 0:XS_HALF, :], preferred_element_type=f32)
                        + jnp.dot(x_hi, w_bf[k, XS_HALF:D_MODEL, :], preferred_element_type=f32))

            mid = (jax.nn.silu(first_layer(0)) * first_layer(1)).astype(bf16)
            slot_w = lax.bitcast_convert_type(xs_ref[rows, XS_HALF:XS_HALF + 1], f32)
            ys_ref[rows, :] = jnp.dot(mid, w_bf[2], preferred_element_type=f32) * slot_w

        half = BM // 2
        rows_used = be_ref[n_blocks + 1 + N_EXPERTS + j]

        @pl.when(rows_used > half)
        def _():
            ffn(slice(0, BM))

        @pl.when(rows_used <= half)
        def _():
            ffn(slice(0, half))
            ys_ref[half:BM, :] = jnp.zeros((BM - half, D_MODEL), f32)


def _experts(layer, lay, block_e, xs, w_gate, w_up, w_down):
    n_blocks = lay.n_blocks
    d_ff = w_gate.shape[-1]
    assert d_ff == D_MODEL
    used = lambda j, be: jnp.minimum(j, be[n_blocks] - 1)
    any_spec = pl.BlockSpec(memory_space=pl.ANY)
    grid_spec = pltpu.PrefetchScalarGridSpec(
        num_scalar_prefetch=1,
        grid=(n_blocks,),
        in_specs=[pl.BlockSpec((BM, XS_W), lambda j, be: (used(j, be), 0)), any_spec, any_spec, any_spec],
        out_specs=pl.BlockSpec((BM, D_MODEL), lambda j, be: (used(j, be), 0)),
        scratch_shapes=[pltpu.VMEM((3, D_MODEL, d_ff), f32), pltpu.VMEM((3, D_MODEL, d_ff), bf16),
                        pltpu.SemaphoreType.DMA((3,))],
    )
    return pl.pallas_call(
        functools.partial(_expert_kernel, n_blocks=n_blocks, layer=layer),
        grid_spec=grid_spec,
        out_shape=jax.ShapeDtypeStruct((lay.cap, D_MODEL), f32),
        compiler_params=_cparams(1),
        name="moe_experts",
    )(block_e, xs, w_gate, w_up, w_down)


def _combine_kernel(tab_ref, ys_hbm, cols_ref, x_ref, gts_ref, gtt_ref, *rest, n_steps, np_steps, final):
    if final:
        gf_ref, yp_ref, ysm_ref, staged, sem = rest
    else:
        xo_ref, staged, sem = rest
    i = pl.program_id(0)
    slot = i % 2
    is_s = i >= np_steps

    def start_step(step, slot_):
        for c in range(STEP_TILES * N_CHUNKS):
            src = jnp.maximum(tab_ref[step * (STEP_TILES * N_CHUNKS) + c], 0)
            pltpu.make_async_copy(ys_hbm.at[pl.ds(pl.multiple_of(src, SUBLANES), SUBLANES)],
                                  staged.at[slot_, pl.ds(c * SUBLANES, SUBLANES)], sem.at[slot_]).start()

    @pl.when(i == 0)
    def _():
        start_step(0, 0)

    @pl.when(i + 1 < n_steps)
    def _():
        start_step(i + 1, 1 - slot)

    pltpu.make_async_copy(ys_hbm.at[pl.ds(0, STEP_TILES * SORT_ROWS)], staged.at[slot], sem.at[slot]).wait()

    def step(sample):
        lane = lax.broadcasted_iota(i32, (TM, SORT_ROWS), 1).astype(f32)
        for t in range(STEP_TILES):
            rs = slice(t * TM, (t + 1) * TM)
            unperm = jnp.where((lane == cols_ref[rs, 0:1]) | (lane == cols_ref[rs, 1:2]), 1.0, 0.0).astype(bf16)
            parts = _split3(staged[slot, t * SORT_ROWS:(t + 1) * SORT_ROWS, :])
            ff = sum(jnp.dot(unperm, part, preferred_element_type=f32) for part in parts)
            xn = x_ref[rs, :] + (gtt_ref[rs, :] if sample else gts_ref[...]) * ff
            if final:
                (ysm_ref if sample else yp_ref)[rs, :] = _rms(xn, gf_ref[...])
            else:
                xo_ref[rs, :] = xn

    pl.when(is_s)(functools.partial(step, True))
    pl.when(jnp.logical_not(is_s))(functools.partial(step, False))


def _combine(tl, layer, tab, ys, cols, x, mod_seq, mod_tok, g_final):
    final = g_final is not None
    tok_spec = pl.BlockSpec((TS, D_MODEL), lambda i, t: (i, 0))
    gt_seq, gt_tok = _mod_specs(tl, layer, 5)
    in_specs = [pl.BlockSpec(memory_space=pl.ANY), pl.BlockSpec((TS, LANES), lambda i, t: (i, 0)),
                tok_spec, gt_seq, gt_tok]
    args = [tab, ys, cols, x, mod_seq, mod_tok]
    if final:
        in_specs.append(pl.BlockSpec((1, D_MODEL), lambda i, t: (0, 0)))
        args.append(g_final)
        out_specs = [pl.BlockSpec((TS, D_MODEL), lambda i, t: (tl.prompt_block(i), 0)),
                     pl.BlockSpec((TS, D_MODEL), lambda i, t: (tl.sample_block(i), 0))]
        out_shape = [jax.ShapeDtypeStruct((tl.n_prompt, D_MODEL), f32),
                     jax.ShapeDtypeStruct((tl.n_sample, D_MODEL), f32)]
    else:
        out_specs = tok_spec
        out_shape = jax.ShapeDtypeStruct((tl.n_tok, D_MODEL), f32)
    grid_spec = pltpu.PrefetchScalarGridSpec(
        num_scalar_prefetch=1,
        grid=(tl.n_steps,),
        in_specs=in_specs,
        out_specs=out_specs,
        scratch_shapes=[pltpu.VMEM((2, STEP_TILES * SORT_ROWS, D_MODEL), f32), pltpu.SemaphoreType.DMA((2,))],
    )
    return pl.pallas_call(
        functools.partial(_combine_kernel, n_steps=tl.n_steps, np_steps=tl.np_steps, final=final),
        grid_spec=grid_spec,
        out_shape=out_shape,
        compiler_params=_cparams(1),
        name="moe_combine",
    )(*args)


def _rope_tables(tl):
    half = HEAD_D // 2
    inv = ROPE_BASE ** (-jnp.arange(half, dtype=f32) / half)
    pos_p = jnp.arange(tl.tp, dtype=i32)
    pos_s = PAST_LEN + jnp.arange(tl.ts, dtype=i32)
    pos = jnp.concatenate([pos_p, jnp.tile(pos_s, tl.bs)])
    ang = pos.astype(f32)[:, None] * inv[None, :]
    cos, sin = jnp.cos(ang), jnp.sin(ang)
    return jnp.concatenate([cos, cos], axis=-1), jnp.concatenate([-sin, sin], axis=-1)


def kernel(x_prompt, x_sample, state_ret, state_conv, c_prompt, c_sample, w_mod, b_mod, g_mix_norm, w_in,
           w_conv, b_conv, g_conv_ln, b_conv_ln, g_ret_gn, w_out, g_ffn_norm, w_router, b_router,
           w_exp_gate, w_exp_up, w_exp_down, g_final):
    bp, tp, _ = x_prompt.shape
    bs, ts, _ = x_sample.shape
    depth = w_mod.shape[0]
    tl = _Tiles(bp, tp, bs, ts)
    lay = _Layout(tl.n_tiles)

    c_all = jnp.concatenate([c_prompt, jnp.repeat(c_sample, ts, axis=0)], axis=0)
    mod_seq, mod_tok = _modulation(c_all, bp, w_mod, b_mod)
    mod_seq = mod_seq.reshape(depth, bp, 1, N_MOD * D_MODEL)

    cos_tab, sin_tab = _rope_tables(tl)
    w_in_bf = w_in.astype(bf16)
    w_out_bf = w_out.astype(bf16)
    wr_pad = jnp.pad(w_router.astype(f32), ((0, 0), (0, LANES - N_EXPERTS)))
    wr_bf = wr_pad.astype(bf16)
    br_pad = jnp.pad(b_router.astype(f32), (0, LANES - N_EXPERTS)).reshape(1, LANES)
    vec3 = lambda t: t.reshape(depth, 1, t.shape[-1])
    g_mix3, g_ffn3, gn3 = vec3(g_mix_norm), vec3(g_ffn_norm), vec3(g_ret_gn)
    b_conv3, g_ln3, b_ln3 = vec3(b_conv), vec3(g_conv_ln), vec3(b_conv_ln)

    x = (x_prompt.reshape(tl.n_prompt, D_MODEL), x_sample.reshape(tl.n_sample, D_MODEL))
    ret_p, conv_p = [], []
    ret_s_all = conv_s_all = None
    for layer in range(depth):
        q, k, v, gate, a = _inproj(tl, layer, x, mod_seq, mod_tok, g_mix3, w_in_bf, cos_tab, sin_tab)
        ro_p, s_p = _retention_prompt(tl, layer, q, k, v, gate, gn3)
        ro_s, ret_s_all = _retention_sample(tl, layer, q, k, v, gate, state_ret, gn3, ret_s_all)
        co_p, buf_p = _conv_prompt(tl, layer, a, w_conv, b_conv3, g_ln3, b_ln3)
        co_s, conv_s_all = _conv_sample(tl, layer, a, state_conv, w_conv, b_conv3, g_ln3, b_ln3, conv_s_all)
        x_mid, h2, rows, cols, tile_counts = _outproj(
            tl, layer, x, ro_p, ro_s, co_p, co_s, mod_seq, mod_tok, g_ffn3, w_out_bf, wr_bf, br_pad)
        tab, block_e = _moe_tables(lay, tile_counts[:, :, 0])
        xs = _dispatch(tl, lay, tab, h2, rows, cols)
        ys = _experts(layer, lay, block_e, xs, w_exp_gate, w_exp_up, w_exp_down)
        last = layer == depth - 1
        x = _combine(tl, layer, tab, ys, cols, x_mid, mod_seq, mod_tok,
                     g_final.reshape(1, D_MODEL) if last else None)
        ret_p.append(s_p)
        conv_p.append(buf_p)
    y_p, y_s = x
    return (y_p.reshape(bp, tp, D_MODEL), y_s.reshape(bs, ts, D_MODEL),
            jnp.stack(ret_p), jnp.stack(conv_p), ret_s_all, conv_s_all)
```

```python
import functools

import jax
import jax.numpy as jnp
from jax import lax
from jax.experimental import pallas as pl
from jax.experimental.pallas import tpu as pltpu

f32 = jnp.float32
bf16 = jnp.bfloat16
i32 = jnp.int32

D_MODEL = 1024
D_RET = 512
D_CONV = 512
N_HEADS = 4
HEAD_D = 128
RET_CHUNK = 128
RET_CHUNKS_PER_STEP = 16
ROPE_BASE = 10000.0
CONV_WIDTH = 31
CONV_HALO = CONV_WIDTH - 1
N_EXPERTS = 16
N_GROUPS = 4
GROUP_SIZE = N_EXPERTS // N_GROUPS
N_MOD = 6
EPS = 1e-6
PAST_LEN = 16384
D_IN = 4 * D_RET + 2 * D_CONV

LANES = 128
SUBLANES = 8
TM = 256
STEP_TILES = 2
TS = TM * STEP_TILES
BM = 512
CONV_ROWS = 256
NORM_ROWS = 64
SAMPLE_GROUP = 16
HALO_PAD = 32
V7X_VMEM_BYTES = 64 * 1024 * 1024
VMEM_LIMIT = V7X_VMEM_BYTES - 8 * 1024 * 1024


def _cparams(n_axes, vmem=VMEM_LIMIT):
    return pltpu.CompilerParams(dimension_semantics=("arbitrary",) * n_axes, vmem_limit_bytes=vmem)


def _mod_kernel(c_ref, w_ref, b_ref, seq_ref, tok_ref):
    cond = jax.nn.silu(c_ref[...]).astype(bf16)
    mod = jnp.dot(cond, w_ref[...].astype(bf16), preferred_element_type=f32) + b_ref[...]
    n_seq = seq_ref.shape[0]
    seq_ref[...] = mod[0:n_seq, :]
    tok_ref[...] = mod[n_seq:, :]


def _modulation(c_all, n_seq, w_mod, b_mod):
    depth = w_mod.shape[0]
    m = c_all.shape[0]
    n_tok = m - n_seq
    assert n_seq % SUBLANES == 0
    return pl.pallas_call(
        _mod_kernel,
        grid=(depth, N_MOD),
        in_specs=[
            pl.BlockSpec((m, D_MODEL), lambda l, j: (0, 0)),
            pl.BlockSpec((None, D_MODEL, D_MODEL), lambda l, j: (l, 0, j)),
            pl.BlockSpec((None, 1, D_MODEL), lambda l, j: (l, 0, j)),
        ],
        out_specs=[pl.BlockSpec((None, n_seq, D_MODEL), lambda l, j: (l, 0, j)),
                   pl.BlockSpec((None, n_tok, D_MODEL), lambda l, j: (l, 0, j))],
        out_shape=[jax.ShapeDtypeStruct((depth, n_seq, N_MOD * D_MODEL), f32),
                   jax.ShapeDtypeStruct((depth, n_tok, N_MOD * D_MODEL), f32)],
        compiler_params=_cparams(2),
        name="modulation",
    )(c_all, w_mod, b_mod.reshape(depth, 1, N_MOD * D_MODEL))


class _Tiles:
    def __init__(self, bp, tp, bs, ts):
        self.bp, self.tp, self.bs, self.ts = bp, tp, bs, ts
        self.n_prompt = bp * tp
        self.n_sample = bs * ts
        self.n_tok = self.n_prompt + self.n_sample
        assert tp % TS == 0 and self.n_sample % TS == 0
        self.tiles_per_seq = tp // TM
        self.np_tiles = self.n_prompt // TM
        self.n_tiles = self.n_tok // TM
        self.steps_per_seq = tp // TS
        self.np_steps = self.n_prompt // TS
        self.n_steps = self.n_tok // TS

    def prompt_block(self, i):
        return jnp.minimum(i, self.np_steps - 1)

    def sample_block(self, i):
        return jnp.maximum(i - self.np_steps, 0)

    def seq_index(self, i):
        return jnp.minimum(i // self.steps_per_seq, self.bp - 1)


def _mod_specs(tl, layer, col):
    seq = pl.BlockSpec((None, None, 1, D_MODEL), lambda i, *_: (layer, tl.seq_index(i), 0, col))
    tok = pl.BlockSpec((None, TS, D_MODEL), lambda i, *_: (layer, tl.sample_block(i), col))
    return seq, tok


def _rms(x, g):
    return x * lax.rsqrt(jnp.mean(x * x, axis=-1, keepdims=True) + EPS) * g


def _head_norm_gate(o, gn, gate):
    mu = jnp.mean(o, axis=-1, keepdims=True)
    var = jnp.mean(jnp.square(o - mu), axis=-1, keepdims=True)
    return jax.nn.silu(gate) * ((o - mu) * lax.rsqrt(var + EPS) * gn)


def _dot_nt(a, b):
    return lax.dot_general(a, b, (((1,), (1,)), ((), ())), preferred_element_type=f32)


def _dot_tn(a, b):
    return lax.dot_general(a, b, (((0,), (0,)), ((), ())), preferred_element_type=f32)


def _ret_prompt_kernel(q_ref, k_ref, v_ref, gate_ref, dec_ref, qd_ref, kd_ref, cd_ref, gn_ref,
                       o_ref, s_out_ref, s_ref):
    c = pl.program_id(1)

    @pl.when(c == 0)
    def _():
        s_ref[...] = jnp.zeros_like(s_ref)

    for ci in range(RET_CHUNKS_PER_STEP):
        rows = slice(ci * RET_CHUNK, (ci + 1) * RET_CHUNK)
        for hd in range(N_HEADS):
            sl = slice(hd * HEAD_D, (hd + 1) * HEAD_D)
            kh = k_ref[rows, sl]
            qb = q_ref[rows, sl]
            kb = kh.astype(bf16)
            vb = v_ref[rows, sl]
            s_old = s_ref[hd]
            scores = _dot_nt(qb, kb) * dec_ref[hd]
            inner = jnp.dot(scores.astype(bf16), vb, preferred_element_type=f32)
            cross = jnp.dot(qb, s_old.astype(bf16), preferred_element_type=f32) * qd_ref[hd]
            s_ref[hd] = s_old * cd_ref[hd] + _dot_tn((kh * kd_ref[hd]).astype(bf16), vb)
            o_ref[rows, sl] = _head_norm_gate(inner + cross, gn_ref[:, sl], gate_ref[rows, sl]).astype(bf16)

    @pl.when(c == pl.num_programs(1) - 1)
    def _():
        s_out_ref[...] = s_ref[...]


def _decay_tables(chunk, true_len):
    lg = jnp.log(1.0 - 2.0 ** (-5.0 - jnp.arange(N_HEADS, dtype=f32)))
    idx = jnp.arange(chunk, dtype=f32)
    rel = idx[:, None] - idx[None, :]
    decay = jnp.where(rel[None] >= 0, jnp.exp(jnp.maximum(rel, 0.0)[None] * lg[:, None, None]), 0.0)
    q_decay = jnp.exp((idx[None, :] + 1.0) * lg[:, None])
    k_decay = jnp.exp((true_len - 1.0 - idx[None, :]) * lg[:, None])
    c_decay = jnp.exp(true_len * lg)
    return decay, q_decay, k_decay, c_decay


def _retention_prompt(tl, layer, q, k, v, gate, g_ret_gn):
    step_rows = RET_CHUNK * RET_CHUNKS_PER_STEP
    assert tl.tp % step_rows == 0
    n_chunks = tl.tp // step_rows
    decay, q_decay, k_decay, c_decay = _decay_tables(RET_CHUNK, RET_CHUNK)
    bcast = lambda t: jnp.broadcast_to(t[:, :, None], (N_HEADS, RET_CHUNK, HEAD_D))
    cd = jnp.broadcast_to(c_decay[:, None, None], (N_HEADS, 1, HEAD_D))
    tok_spec = pl.BlockSpec((step_rows, D_RET), lambda b, c: (b * n_chunks + c, 0))
    tab_spec = pl.BlockSpec((N_HEADS, RET_CHUNK, HEAD_D), lambda b, c: (0, 0, 0))
    return pl.pallas_call(
        _ret_prompt_kernel,
        grid=(tl.bp, n_chunks),
        in_specs=[tok_spec] * 4 + [tab_spec] * 3 + [
            pl.BlockSpec((N_HEADS, 1, HEAD_D), lambda b, c: (0, 0, 0)),
            pl.BlockSpec((None, 1, D_RET), lambda b, c: (layer, 0, 0)),
        ],
        out_specs=[tok_spec, pl.BlockSpec((None, N_HEADS, HEAD_D, HEAD_D), lambda b, c: (b, 0, 0, 0))],
        out_shape=[jax.ShapeDtypeStruct((tl.n_prompt, D_RET), bf16),
                   jax.ShapeDtypeStruct((tl.bp, N_HEADS, HEAD_D, HEAD_D), f32)],
        scratch_shapes=[pltpu.VMEM((N_HEADS, HEAD_D, HEAD_D), f32)],
        compiler_params=_cparams(2),
        name="retention_prompt",
    )(q, k, v, gate, decay, bcast(q_decay), bcast(k_decay), cd, g_ret_gn)


def _ret_sample_kernel(q_ref, k_ref, v_ref, gate_ref, s_in_ref, dec_ref, qd_ref, kd_ref, cd_ref, gn_ref,
                       *rest, ts):
    o_ref, s_all_ref = rest[-2:]
    s_out_ref = s_all_ref.at[0]
    for other in range(1, s_all_ref.shape[0]):
        s_all_ref[other] = jnp.zeros(s_all_ref.shape[1:], f32)
    seqs_per_tile = SUBLANES // ts
    row = lax.broadcasted_iota(i32, (SUBLANES, HEAD_D), 0)
    q_all = q_ref[...].astype(f32)
    v_all = v_ref[...].astype(f32)
    outs = []
    for t in range(SAMPLE_GROUP // seqs_per_tile):
        rows = slice(t * SUBLANES, (t + 1) * SUBLANES)
        heads = []
        for hd in range(N_HEADS):
            sl = slice(hd * HEAD_D, (hd + 1) * HEAD_D)
            qh = q_all[rows, sl]
            kh = k_ref[rows, sl] * kd_ref[hd]
            vb = v_all[rows, sl].astype(bf16)
            qb = qh.astype(bf16)
            scores = _dot_nt(qb, k_ref[rows, sl].astype(bf16)) * dec_ref[hd]
            o = jnp.dot(scores.astype(bf16), vb, preferred_element_type=f32)
            for s in range(seqs_per_tile):
                b = t * seqs_per_tile + s
                mine = (row >= s * ts) & (row < (s + 1) * ts)
                s_old = s_in_ref[b, hd]
                q_s = jnp.where(mine, qh, 0.0).astype(bf16)
                k_s = jnp.where(mine, kh, 0.0).astype(bf16)
                o = o + jnp.dot(q_s, s_old.astype(bf16), preferred_element_type=f32) * qd_ref[hd]
                s_out_ref[b, hd] = s_old * cd_ref[hd] + _dot_tn(k_s, vb)
            heads.append(_head_norm_gate(o, gn_ref[:, sl], gate_ref[rows, sl]))
        outs.append(jnp.concatenate(heads, axis=-1))
    o_ref[...] = jnp.concatenate(outs, axis=0).astype(bf16)


def _retention_sample(tl, layer, q, k, v, gate, state_ret, g_ret_gn, prev_states):
    ts = tl.ts
    depth = state_ret.shape[0]
    assert SUBLANES % ts == 0 and tl.bs % SAMPLE_GROUP == 0
    seqs_per_tile = SUBLANES // ts
    decay, q_decay, k_decay, c_decay = _decay_tables(ts, ts)
    eye = jnp.eye(seqs_per_tile, dtype=f32)
    dec_tile = jnp.einsum("ab,hij->haibj", eye, decay).reshape(N_HEADS, SUBLANES, SUBLANES)
    tile_rows = lambda t: jnp.broadcast_to(jnp.tile(t, (1, seqs_per_tile))[:, :, None],
                                           (N_HEADS, SUBLANES, HEAD_D))
    cd = jnp.broadcast_to(c_decay[:, None, None], (N_HEADS, 1, HEAD_D))
    rows = SAMPLE_GROUP * ts
    first = tl.n_prompt // rows
    tok_spec = pl.BlockSpec((rows, D_RET), lambda i: (first + i, 0))
    const3 = lambda shape: pl.BlockSpec(shape, lambda i: (0, 0, 0))
    st_block = (SAMPLE_GROUP, N_HEADS, HEAD_D, HEAD_D)
    in_specs = [tok_spec] * 4 + [
        pl.BlockSpec((None,) + st_block, lambda i: (layer, i, 0, 0, 0)),
        const3((N_HEADS, SUBLANES, SUBLANES)),
        const3((N_HEADS, SUBLANES, HEAD_D)),
        const3((N_HEADS, SUBLANES, HEAD_D)),
        const3((N_HEADS, 1, HEAD_D)),
        pl.BlockSpec((None, 1, D_RET), lambda i: (layer, 0, 0)),
    ]
    args = [q, k, v, gate, state_ret, dec_tile, tile_rows(q_decay), tile_rows(k_decay), cd, g_ret_gn]
    if prev_states is None:
        state_spec = pl.BlockSpec((depth,) + st_block, lambda i: (0, i, 0, 0, 0))
        aliases = {}
    else:
        state_spec = pl.BlockSpec((1,) + st_block, lambda i: (layer, i, 0, 0, 0))
        in_specs.append(pl.BlockSpec(memory_space=pl.ANY))
        args.append(prev_states)
        aliases = {len(args) - 1: 1}
    return pl.pallas_call(
        functools.partial(_ret_sample_kernel, ts=ts),
        grid=(tl.bs // SAMPLE_GROUP,),
        in_specs=in_specs,
        out_specs=[pl.BlockSpec((rows, D_RET), lambda i: (i, 0)), state_spec],
        out_shape=[jax.ShapeDtypeStruct((tl.n_sample, D_RET), bf16),
                   jax.ShapeDtypeStruct((depth, tl.bs, N_HEADS, HEAD_D, HEAD_D), f32)],
        input_output_aliases=aliases,
        compiler_params=_cparams(1),
        name="retention_sample",
    )(*args)


def _ln_silu(cv, g, b):
    mu = jnp.mean(cv, axis=-1, keepdims=True)
    var = jnp.mean(jnp.square(cv - mu), axis=-1, keepdims=True)
    return jax.nn.silu((cv - mu) * lax.rsqrt(var + EPS) * g + b)


def _conv_taps(window, w_ref, b_ref, n_rows):
    cols = []
    for col in range(D_CONV // LANES):
        lanes = slice(col * LANES, (col + 1) * LANES)
        acc = jnp.broadcast_to(b_ref[:, lanes], (n_rows, LANES))
        for tap in range(CONV_WIDTH):
            acc = acc + window(col, tap) * w_ref[tap:tap + 1, lanes]
        cols.append(acc)
    return jnp.concatenate(cols, axis=-1)


def _conv_sample_kernel(a_ref, st_ref, w_ref, b_ref, g_ref, bl_ref, *rest, ts):
    o_ref, buf_all_ref, full_ref, cv_ref = rest[-4:]
    buf_ref = buf_all_ref.at[0]
    for other in range(1, buf_all_ref.shape[0]):
        buf_all_ref[other] = jnp.zeros(buf_all_ref.shape[1:], f32)
    for s in range(SAMPLE_GROUP):
        for col in range(D_CONV // LANES):
            lanes = slice(col * LANES, (col + 1) * LANES)
            full_ref[col, 0:CONV_HALO, :] = st_ref[:, s, lanes]
            full_ref[col, CONV_HALO:CONV_HALO + ts, :] = a_ref[s * ts:(s + 1) * ts, lanes]
            buf_ref[s, :, lanes] = full_ref[col, ts:ts + CONV_HALO, :]
        cv_ref[s * ts:(s + 1) * ts, :] = _conv_taps(lambda col, tap: full_ref[col, tap:tap + ts, :],
                                                    w_ref, b_ref, ts)
    o_ref[...] = _ln_silu(cv_ref[...], g_ref[...], bl_ref[...]).astype(bf16)


def _conv_sample(tl, layer, a, state_conv, w_conv, b_conv, g_ln, b_ln, prev_bufs):
    ts = tl.ts
    depth = state_conv.shape[0]
    rows = SAMPLE_GROUP * ts
    first_block = tl.n_prompt // rows
    vec = pl.BlockSpec((None, 1, D_CONV), lambda i: (layer, 0, 0))
    in_specs = [pl.BlockSpec((rows, D_CONV), lambda i: (first_block + i, 0)),
                pl.BlockSpec((None, CONV_HALO, SAMPLE_GROUP, D_CONV), lambda i: (layer, 0, i, 0)),
                pl.BlockSpec((None, CONV_WIDTH, D_CONV), lambda i: (layer, 0, 0)),
                vec, vec, vec]
    args = [a, state_conv.transpose(0, 2, 1, 3), w_conv, b_conv, g_ln, b_ln]
    buf_block = (SAMPLE_GROUP, CONV_HALO, D_CONV)
    if prev_bufs is None:
        buf_spec = pl.BlockSpec((depth,) + buf_block, lambda i: (0, i, 0, 0))
        aliases = {}
    else:
        buf_spec = pl.BlockSpec((1,) + buf_block, lambda i: (layer, i, 0, 0))
        in_specs.append(pl.BlockSpec(memory_space=pl.ANY))
        args.append(prev_bufs)
        aliases = {len(args) - 1: 1}
    return pl.pallas_call(
        functools.partial(_conv_sample_kernel, ts=ts),
        grid=(tl.bs // SAMPLE_GROUP,),
        in_specs=in_specs,
        out_specs=[pl.BlockSpec((rows, D_CONV), lambda i: (i, 0)), buf_spec],
        out_shape=[jax.ShapeDtypeStruct((tl.n_sample, D_CONV), bf16),
                   jax.ShapeDtypeStruct((depth, tl.bs, CONV_HALO, D_CONV), f32)],
        input_output_aliases=aliases,
        scratch_shapes=[pltpu.VMEM((D_CONV // LANES, CONV_HALO + ts + SUBLANES, LANES), f32),
                        pltpu.VMEM((rows, D_CONV), f32)],
        compiler_params=_cparams(1),
        name="conv_sample",
    )(*args)


N_CONV_COLS = D_CONV // LANES
N_CONV_CHUNKS = TM // CONV_ROWS


def _inproj_kernel(*refs, np_steps, split_x):
    if split_x:
        xp_ref, xs_ref = refs[:2]
        refs = refs[2:]
    else:
        x_ref = refs[0]
        refs = refs[1:]
    (shs_ref, sht_ref, scs_ref, sct_ref, g_ref, w_ref, cos_ref, sin_ref,
     q_ref, k_ref, v_ref, gate_ref, a_ref) = refs
    is_s = pl.program_id(0) >= np_steps

    def step(sample):
        for t in range(STEP_TILES):
            rs = slice(t * TM, (t + 1) * TM)
            if split_x:
                x = xs_ref[rs, :] if sample else xp_ref[rs, :]
            else:
                x = x_ref[rs, :]
            sh, sc = (sht_ref[rs, :], sct_ref[rs, :]) if sample else (shs_ref[...], scs_ref[...])
            hb = (_rms(x, g_ref[...]) * (1.0 + sc) + sh).astype(bf16)

            def group(g):
                return jnp.dot(hb, w_ref[:, g * D_RET:(g + 1) * D_RET], preferred_element_type=f32)

            cos = cos_ref[rs, :]
            sin = sin_ref[rs, :]

            def rope(th):
                return th * cos + pltpu.roll(th, HEAD_D // 2, 1) * sin

            qg, kg = group(0), group(1)
            for hd in range(N_HEADS):
                sl = slice(hd * HEAD_D, (hd + 1) * HEAD_D)
                q_ref[rs, sl] = rope(qg[:, sl]).astype(bf16)
                k_ref[rs, sl] = rope(kg[:, sl]) * (HEAD_D ** -0.5)
            v_ref[rs, :] = group(2).astype(bf16)
            gate_ref[rs, :] = group(3)
            a_ref[rs, :] = group(4) * jax.nn.sigmoid(group(5))

    pl.when(is_s)(functools.partial(step, True))
    pl.when(jnp.logical_not(is_s))(functools.partial(step, False))


def _conv_prompt_kernel(a_ref, w_ref, b_ref, g_ref, bl_ref, o_ref, buf_ref, full_ref, cv_ref):
    j = pl.program_id(1)

    @pl.when(j == 0)
    def _():
        full_ref[:, 0:HALO_PAD, :] = jnp.zeros((N_CONV_COLS, HALO_PAD, LANES), f32)

    @pl.when(j > 0)
    def _():
        full_ref[:, 0:HALO_PAD, :] = full_ref[:, TM:TM + HALO_PAD, :]

    for col in range(N_CONV_COLS):
        full_ref[col, HALO_PAD:HALO_PAD + TM, :] = a_ref[:, col * LANES:(col + 1) * LANES]
    shift = HALO_PAD - CONV_HALO

    def taps(idx, carry):
        col = idx // N_CONV_CHUNKS
        r0 = pl.multiple_of((idx % N_CONV_CHUNKS) * CONV_ROWS, CONV_ROWS)
        acc = jnp.broadcast_to(b_ref[col], (CONV_ROWS, LANES))
        for tap in range(CONV_WIDTH):
            acc = acc + full_ref[col, pl.ds(r0 + (tap + shift), CONV_ROWS), :] * w_ref[col, tap:tap + 1, :]
        cv_ref[col, pl.ds(r0, CONV_ROWS), :] = acc
        return carry

    lax.fori_loop(0, N_CONV_COLS * N_CONV_CHUNKS, taps, 0)

    for r0 in range(0, TM, NORM_ROWS):
        cv = jnp.concatenate([cv_ref[col, r0:r0 + NORM_ROWS, :] for col in range(N_CONV_COLS)], axis=-1)
        o_ref[r0:r0 + NORM_ROWS, :] = _ln_silu(cv, g_ref[...], bl_ref[...]).astype(bf16)

    @pl.when(j == pl.num_programs(1) - 1)
    def _():
        buf_ref[...] = a_ref[TM - CONV_HALO:TM, :]


def _conv_prompt(tl, layer, a, w_conv, b_conv, g_ln, b_ln):
    tps = tl.tiles_per_seq
    depth = w_conv.shape[0]
    w_cols = w_conv.reshape(depth, CONV_WIDTH, N_CONV_COLS, LANES).transpose(0, 2, 1, 3)
    b_cols = b_conv.reshape(depth, N_CONV_COLS, 1, LANES)
    vec = pl.BlockSpec((None, 1, D_CONV), lambda b, j: (layer, 0, 0))
    return pl.pallas_call(
        _conv_prompt_kernel,
        grid=(tl.bp, tps),
        in_specs=[pl.BlockSpec((TM, D_CONV), lambda b, j: (b * tps + j, 0)),
                  pl.BlockSpec((None, N_CONV_COLS, CONV_WIDTH, LANES), lambda b, j: (layer, 0, 0, 0)),
                  pl.BlockSpec((None, N_CONV_COLS, 1, LANES), lambda b, j: (layer, 0, 0, 0)),
                  vec, vec],
        out_specs=[pl.BlockSpec((TM, D_CONV), lambda b, j: (b * tps + j, 0)),
                   pl.BlockSpec((None, CONV_HALO, D_CONV), lambda b, j: (b, 0, 0))],
        out_shape=[jax.ShapeDtypeStruct((tl.n_prompt, D_CONV), bf16),
                   jax.ShapeDtypeStruct((tl.bp, CONV_HALO, D_CONV), f32)],
        scratch_shapes=[pltpu.VMEM((N_CONV_COLS, HALO_PAD + TM, LANES), f32),
                        pltpu.VMEM((N_CONV_COLS, TM, LANES), f32)],
        compiler_params=_cparams(2),
        name="conv_prompt",
    )(a, w_cols, b_cols, g_ln, b_ln)


def _inproj(tl, layer, x, mod_seq, mod_tok, g_norm, w_in_bf, cos_tab, sin_tab):
    split_x = isinstance(x, tuple)
    tok_spec = pl.BlockSpec((TS, D_MODEL), lambda i: (i, 0))
    if split_x:
        x_args = list(x)
        x_specs = [pl.BlockSpec((TS, D_MODEL), lambda i: (tl.prompt_block(i), 0)),
                   pl.BlockSpec((TS, D_MODEL), lambda i: (tl.sample_block(i), 0))]
    else:
        x_args, x_specs = [x], [tok_spec]
    sh_seq, sh_tok = _mod_specs(tl, layer, 0)
    sc_seq, sc_tok = _mod_specs(tl, layer, 1)

    def table_block(i):
        return jnp.where(i < tl.np_steps, i % tl.steps_per_seq, tl.steps_per_seq + tl.sample_block(i))

    tab_spec = pl.BlockSpec((TS, HEAD_D), lambda i: (table_block(i), 0))
    row_spec = pl.BlockSpec((TS, D_RET), lambda i: (i, 0))
    row_sd = lambda dt: jax.ShapeDtypeStruct((tl.n_tok, D_RET), dt)
    return pl.pallas_call(
        functools.partial(_inproj_kernel, np_steps=tl.np_steps, split_x=split_x),
        grid=(tl.n_steps,),
        in_specs=x_specs + [
            sh_seq, sh_tok, sc_seq, sc_tok,
            pl.BlockSpec((None, 1, D_MODEL), lambda i: (layer, 0, 0)),
            pl.BlockSpec((None, D_MODEL, D_IN), lambda i: (layer, 0, 0)),
            tab_spec, tab_spec,
        ],
        out_specs=[row_spec] * 5,
        out_shape=[row_sd(bf16), row_sd(f32), row_sd(bf16), row_sd(f32), row_sd(f32)],
        compiler_params=_cparams(1),
        name="inproj",
    )(*x_args, mod_seq, mod_tok, mod_seq, mod_tok, g_norm, w_in_bf, cos_tab, sin_tab)


SORT_ROWS = 2 * TM + N_EXPERTS * SUBLANES
XS_HALF = D_MODEL // 2
XS_W = XS_HALF + LANES
u32 = jnp.uint32
ROUTE_POS, ROUTE_W0, ROUTE_W1 = (0, 2), (2, 5), (5, 8)


def _pack_bf16_pair(x):
    lo = lax.shift_right_logical(lax.bitcast_convert_type(x[:, 0:XS_HALF], u32), u32(16))
    hi = lax.bitcast_convert_type(x[:, XS_HALF:D_MODEL], u32) & u32(0xFFFF0000)
    return hi | lo


def _unpack_bf16_pair(words):
    lo = lax.bitcast_convert_type(lax.shift_left(words, u32(16)), f32).astype(bf16)
    hi = lax.bitcast_convert_type(words & u32(0xFFFF0000), f32).astype(bf16)
    return lo, hi


def _split3(x):
    a = x.astype(bf16)
    r = x - a.astype(f32)
    b = r.astype(bf16)
    c = (r - b.astype(f32)).astype(bf16)
    return a, b, c


def _first_of4(vals, m):
    return jnp.where(vals[0] == m, 0.0, jnp.where(vals[1] == m, 1.0, jnp.where(vals[2] == m, 2.0, 3.0)))


def _rows_to_tile(rows, n_rows):
    sub = lax.broadcasted_iota(i32, (n_rows, TM), 0)
    out = jnp.zeros((n_rows, TM), f32)
    for r, val in enumerate(rows):
        out = jnp.where(sub == r, val, out)
    return out


def _outproj_kernel(*refs, np_steps, split_x):
    if split_x:
        xp_ref, xs_ref = refs[:2]
        refs = refs[2:]
    else:
        x_ref = refs[0]
        refs = refs[1:]
    (retp_ref, rets_ref, cvp_ref, cvs_ref, gts_ref, gtt_ref, shs_ref, sht_ref, scs_ref, sct_ref,
     g_ref, wo_ref, wr_ref, br_ref,
     xo_ref, h2_ref, rows_ref, cols_ref, cnt_ref) = refs
    is_s = pl.program_id(0) >= np_steps

    def step(sample):
        for t in range(STEP_TILES):
            rs = slice(t * TM, (t + 1) * TM)
            if split_x:
                x = xs_ref[rs, :] if sample else xp_ref[rs, :]
            else:
                x = x_ref[rs, :]
            ret = rets_ref[rs, :] if sample else retp_ref[rs, :]
            cv = cvs_ref[rs, :] if sample else cvp_ref[rs, :]
            gt, sh, sc = ((gtt_ref[rs, :], sht_ref[rs, :], sct_ref[rs, :]) if sample
                          else (gts_ref[...], shs_ref[...], scs_ref[...]))
            mix_out = (jnp.dot(ret, wo_ref[0:D_RET, :], preferred_element_type=f32)
                       + jnp.dot(cv, wo_ref[D_RET:D_RET + D_CONV, :], preferred_element_type=f32))
            xn = x + gt * mix_out
            xo_ref[rs, :] = xn
            h2 = _rms(xn, g_ref[...]) * (1.0 + sc) + sh
            h_hi = h2.astype(bf16)
            h2_ref[rs, :] = h_hi
            rows, cols, cnt = _route_tile(h_hi, wr_ref, br_ref)
            rows_ref[t] = rows
            cols_ref[rs, :] = cols
            cnt_ref[t] = cnt

    pl.when(is_s)(functools.partial(step, True))
    pl.when(jnp.logical_not(is_s))(functools.partial(step, False))


def _route_tile(h_bf, wr_ref, br_ref):
    logits = jnp.dot(h_bf, wr_ref[...], preferred_element_type=f32) + br_ref[...]
    lt = logits.T
    row = [lt[e:e + 1, :] for e in range(N_EXPERTS)]
    top = functools.reduce(jnp.maximum, row)
    ex = [jnp.exp(r - top) for r in row]
    den = functools.reduce(jnp.add, ex)
    p = [v / den for v in ex]

    best = None
    for g in range(N_GROUPS):
        a = p[g * GROUP_SIZE:(g + 1) * GROUP_SIZE]
        m1 = functools.reduce(jnp.maximum, a)
        i1 = _first_of4(a, m1)
        b = [jnp.where(i1 == float(j), -1.0, a[j]) for j in range(GROUP_SIZE)]
        m2 = functools.reduce(jnp.maximum, b)
        i2 = _first_of4(b, m2)
        cand = (m1 + m2, m1, m2, i1 + float(g * GROUP_SIZE), i2 + float(g * GROUP_SIZE))
        if best is None:
            best = cand
        else:
            take = cand[0] > best[0]
            best = tuple(jnp.where(take, c, o) for c, o in zip(cand, best))
    _, m1, m2, e0, e1 = best
    denom = m1 + m2
    w0 = m1 / denom
    w1 = m2 / denom

    ex_id = lax.broadcasted_iota(i32, (N_EXPERTS, TM), 0).astype(f32)
    sel0 = ex_id == e0
    sel1 = ex_id == e1
    ind = jnp.where(sel0 | sel1, 1.0, 0.0)
    t_r = lax.broadcasted_iota(i32, (TM, TM), 0)
    t_c = lax.broadcasted_iota(i32, (TM, TM), 1)
    earlier = jnp.where(t_r < t_c, 1.0, 0.0).astype(bf16)
    prefix = jnp.dot(ind.astype(bf16), earlier, preferred_element_type=f32)
    cnt = jnp.sum(ind, axis=-1, keepdims=True)
    cnt8 = jnp.floor((cnt + float(SUBLANES - 1)) * (1.0 / SUBLANES)) * float(SUBLANES)
    e_r = lax.broadcasted_iota(i32, (N_EXPERTS, N_EXPERTS), 0)
    e_c = lax.broadcasted_iota(i32, (N_EXPERTS, N_EXPERTS), 1)
    below = jnp.where(e_c < e_r, 1.0, 0.0).astype(bf16)
    seg_off = jnp.dot(below, jnp.broadcast_to(cnt8, (N_EXPERTS, TM)).astype(bf16),
                      preferred_element_type=f32)
    where_to = seg_off + prefix
    pos0 = jnp.sum(jnp.where(sel0, where_to, 0.0), axis=0, keepdims=True)
    pos1 = jnp.sum(jnp.where(sel1, where_to, 0.0), axis=0, keepdims=True)

    w0p = [v.astype(f32) for v in _split3(w0)]
    w1p = [v.astype(f32) for v in _split3(w1)]
    info = [pos0, pos1] + w0p + w1p
    return (_rows_to_tile(info, SUBLANES), _rows_to_tile(info, LANES).T,
            jnp.broadcast_to(cnt, (N_EXPERTS, LANES)))


def _outproj(tl, layer, x, ret_p, ret_s, cv_p, cv_s, mod_seq, mod_tok, g_norm, w_out_bf, wr_bf, br_pad):
    split_x = isinstance(x, tuple)
    tok_spec = pl.BlockSpec((TS, D_MODEL), lambda i: (i, 0))
    p_spec = lambda w: pl.BlockSpec((TS, w), lambda i: (tl.prompt_block(i), 0))
    s_spec = lambda w: pl.BlockSpec((TS, w), lambda i: (tl.sample_block(i), 0))
    if split_x:
        x_args, x_specs = list(x), [p_spec(D_MODEL), s_spec(D_MODEL)]
    else:
        x_args, x_specs = [x], [tok_spec]
    mods = []
    for col in (2, 3, 4):
        mods += list(_mod_specs(tl, layer, col))
    wr_spec = pl.BlockSpec((D_MODEL, LANES), lambda i: (0, 0))
    return pl.pallas_call(
        functools.partial(_outproj_kernel, np_steps=tl.np_steps, split_x=split_x),
        grid=(tl.n_steps,),
        in_specs=x_specs + [p_spec(D_RET), s_spec(D_RET), p_spec(D_CONV), s_spec(D_CONV)] + mods + [
            pl.BlockSpec((None, 1, D_MODEL), lambda i: (layer, 0, 0)),
            pl.BlockSpec((None, D_MODEL, D_MODEL), lambda i: (layer, 0, 0)),
            wr_spec,
            pl.BlockSpec((1, LANES), lambda i: (0, 0)),
        ],
        out_specs=[tok_spec, tok_spec,
                   pl.BlockSpec((STEP_TILES, SUBLANES, TM), lambda i: (i, 0, 0)),
                   pl.BlockSpec((TS, LANES), lambda i: (i, 0)),
                   pl.BlockSpec((STEP_TILES, N_EXPERTS, LANES), lambda i: (i, 0, 0))],
        out_shape=[jax.ShapeDtypeStruct((tl.n_tok, D_MODEL), f32),
                   jax.ShapeDtypeStruct((tl.n_tok, D_MODEL), bf16),
                   jax.ShapeDtypeStruct((tl.n_tiles, SUBLANES, TM), f32),
                   jax.ShapeDtypeStruct((tl.n_tok, LANES), f32),
                   jax.ShapeDtypeStruct((tl.n_tiles, N_EXPERTS, LANES), f32)],
        compiler_params=_cparams(1),
        name="outproj_router",
    )(*x_args, ret_p, ret_s, cv_p, cv_s, *([mod_seq, mod_tok] * 3), g_norm, w_out_bf, wr_bf, br_pad)


N_CHUNKS = SORT_ROWS // SUBLANES


class _Layout:
    def __init__(self, n_tiles):
        self.n_tiles = n_tiles
        self.tail_start = n_tiles * N_CHUNKS
        self.tail_n8 = self.tail_start + N_EXPERTS
        worst = 2 * n_tiles * TM + n_tiles * N_EXPERTS * (SUBLANES - 1) + N_EXPERTS * (BM - SUBLANES)
        self.n_blocks = -(-worst // BM)
        self.cap = self.n_blocks * BM
        self.dump = self.cap
        self.xs_rows = self.cap + -(-2 * STEP_TILES * SORT_ROWS // BM) * BM


def _moe_tables(lay, tile_counts):
    c8 = ((tile_counts.astype(i32) + SUBLANES - 1) // SUBLANES) * SUBLANES
    base8 = jnp.cumsum(c8, axis=0) - c8
    tot8 = jnp.sum(c8, axis=0)
    region = ((tot8 + BM - 1) // BM) * BM
    g_end = jnp.cumsum(region)
    g_start = g_end - region
    seg_end = jnp.cumsum(c8, axis=1)
    seg_dst = g_start[None, :] + base8
    n_used = g_end[-1] // BM
    blk = jnp.arange(lay.n_blocks, dtype=i32)
    block_e = jnp.minimum(jnp.sum((g_end[None, :] <= blk[:, None] * BM).astype(i32), axis=1), N_EXPERTS - 1)
    block_e = jnp.where(blk < n_used, block_e, block_e[n_used - 1])
    row0 = jnp.arange(N_CHUNKS, dtype=i32) * SUBLANES
    owner = jnp.sum((seg_end[:, None, :] <= row0[None, :, None]).astype(i32), axis=-1)
    onehot = (owner[:, :, None] == jnp.arange(N_EXPERTS, dtype=i32)[None, None, :]).astype(i32)
    delta = seg_dst - (seg_end - c8)
    chunk_dst = jnp.where(owner < N_EXPERTS, row0[None, :] + jnp.sum(onehot * delta[:, None, :], axis=-1), -1)
    tab = jnp.concatenate([chunk_dst.ravel(), g_start + tot8, (region - tot8) // SUBLANES]).astype(i32)
    ids = jnp.arange(N_EXPERTS, dtype=i32)
    later = jnp.where((ids[None, :] > ids[:, None]) & (region[None, :] > 0), ids[None, :], N_EXPERTS)
    next_e = jnp.min(later, axis=1)
    next_e = jnp.where(next_e == N_EXPERTS, -1, next_e)
    owner_end = jnp.sum((block_e[:, None] == ids[None, :]).astype(i32) * (g_start + tot8)[None, :], axis=1)
    rows_used = jnp.clip(owner_end - blk * BM, 0, BM)
    return tab, jnp.concatenate([block_e, n_used[None], next_e, rows_used]).astype(i32)


def _for_chunks(n, fn):
    def body(c, carry):
        fn(c)
        return carry

    lax.fori_loop(0, n, body, 0)


def _dispatch_kernel(tab_ref, h2_ref, rows_ref, cols_ref, xs_hbm, sorted_ref, zero_ref, sem, *, lay, n_steps):
    i = pl.program_id(0)
    slot = i % 2
    step_rows = STEP_TILES * SORT_ROWS

    def tail_copy(dst):
        return pltpu.make_async_copy(zero_ref, xs_hbm.at[pl.ds(dst, SUBLANES)], sem.at[2])

    def wait_step(slot_):
        pltpu.make_async_copy(sorted_ref.at[slot_], xs_hbm.at[pl.ds(0, step_rows)], sem.at[slot_]).wait()

    @pl.when(i == 0)
    def _():
        zero_ref[...] = jnp.zeros_like(zero_ref)
        for e in range(N_EXPERTS):
            start = tab_ref[lay.tail_start + e]
            _for_chunks(tab_ref[lay.tail_n8 + e],
                        lambda c: tail_copy(pl.multiple_of(start + c * SUBLANES, SUBLANES)).start())

    @pl.when(i >= 2)
    def _():
        wait_step(slot)

    r_id = lax.broadcasted_iota(i32, (SORT_ROWS, TM), 0).astype(f32)
    lane = lax.broadcasted_iota(i32, (TM, LANES), 1)
    for t in range(STEP_TILES):
        pos0 = rows_ref[t, 0:1, :]
        pos1 = rows_ref[t, 1:2, :]
        p0 = r_id == pos0
        p1 = r_id == pos1
        perm = jnp.where(p0 | p1, 1.0, 0.0).astype(bf16)
        cols = cols_ref[t * TM:(t + 1) * TM, :]
        wpart0 = jnp.where((lane >= ROUTE_W0[0]) & (lane < ROUTE_W0[1]), cols, 0.0).astype(bf16)
        wpart1 = jnp.where((lane >= ROUTE_W1[0]) & (lane < ROUTE_W1[1]), cols, 0.0).astype(bf16)
        sw = (jnp.dot(jnp.where(p0, 1.0, 0.0).astype(bf16), wpart0, preferred_element_type=f32)
              + jnp.dot(jnp.where(p1, 1.0, 0.0).astype(bf16), wpart1, preferred_element_type=f32))
        base = t * SORT_ROWS
        sorted_ref[slot, base:base + SORT_ROWS, 0:XS_HALF] = _pack_bf16_pair(jnp.dot(
            perm, h2_ref[t * TM:(t + 1) * TM, :], preferred_element_type=f32))
        sorted_ref[slot, base:base + SORT_ROWS, XS_HALF:XS_W] = lax.bitcast_convert_type(jnp.broadcast_to(
            jnp.sum(sw, axis=-1, keepdims=True), (SORT_ROWS, LANES)), u32)

    for t in range(STEP_TILES):
        for c in range(N_CHUNKS):
            row = t * SORT_ROWS + c * SUBLANES
            dst = tab_ref[(i * STEP_TILES + t) * N_CHUNKS + c]
            dst = jnp.where(dst < 0, lay.dump + slot * step_rows + row, dst)
            pltpu.make_async_copy(sorted_ref.at[slot, pl.ds(row, SUBLANES)],
                                  xs_hbm.at[pl.ds(pl.multiple_of(dst, SUBLANES), SUBLANES)], sem.at[slot]).start()

    @pl.when(i == n_steps - 1)
    def _():
        if n_steps >= 2:
            wait_step(1 - slot)
        wait_step(slot)
        for e in range(N_EXPERTS):
            _for_chunks(tab_ref[lay.tail_n8 + e], lambda c: tail_copy(0).wait())


def _dispatch(tl, lay, tab, h2, rows, cols):
    grid_spec = pltpu.PrefetchScalarGridSpec(
        num_scalar_prefetch=1,
        grid=(tl.n_steps,),
        in_specs=[pl.BlockSpec((TS, D_MODEL), lambda i, t: (i, 0)),
                  pl.BlockSpec((STEP_TILES, SUBLANES, TM), lambda i, t: (i, 0, 0)),
                  pl.BlockSpec((TS, LANES), lambda i, t: (i, 0))],
        out_specs=pl.BlockSpec(memory_space=pl.ANY),
        scratch_shapes=[pltpu.VMEM((2, STEP_TILES * SORT_ROWS, XS_W), u32), pltpu.VMEM((SUBLANES, XS_W), u32),
                        pltpu.SemaphoreType.DMA((3,))],
    )
    return pl.pallas_call(
        functools.partial(_dispatch_kernel, lay=lay, n_steps=tl.n_steps),
        grid_spec=grid_spec,
        out_shape=jax.ShapeDtypeStruct((lay.xs_rows, XS_W), u32),
        compiler_params=_cparams(1),
        name="moe_dispatch",
    )(tab, h2, rows, cols)


def _expert_kernel(be_ref, xs_ref, wg_hbm, wu_hbm, wd_hbm, ys_ref, stage, w_bf, sem, *, n_blocks, layer):
    j = pl.program_id(0)

    def fetch(e):
        return [pltpu.make_async_copy(w.at[layer, e], stage.at[k], sem.at[k])
                for k, w in enumerate((wg_hbm, wu_hbm, wd_hbm))]

    @pl.when(j < be_ref[n_blocks])
    def _():
        e = be_ref[j]

        @pl.when(j == 0)
        def _():
            for copy in fetch(e):
                copy.start()

        @pl.when((j == 0) | (e != be_ref[jnp.maximum(j - 1, 0)]))
        def _():
            for copy in fetch(e):
                copy.wait()
            for k in range(3):
                w_bf[k] = stage[k].astype(bf16)
            nxt = be_ref[n_blocks + 1 + e]

            @pl.when(nxt >= 0)
            def _():
                for copy in fetch(nxt):
                    copy.start()

        def ffn(rows):
            x_lo, x_hi = _unpack_bf16_pair(xs_ref[rows, 0:XS_HALF])

            def first_layer(k):
                return (jnp.dot(x_lo, w_bf[k, 0:XS_HALF, :], preferred_element_type=f32)
                        + jnp.dot(x_hi, w_bf[k, XS_HALF:D_MODEL, :], preferred_element_type=f32))

            mid = (jax.nn.silu(first_layer(0)) * first_layer(1)).astype(bf16)
            slot_w = lax.bitcast_convert_type(xs_ref[rows, XS_HALF:XS_HALF + 1], f32)
            ys_ref[rows, :] = jnp.dot(mid, w_bf[2], preferred_element_type=f32) * slot_w

        half = BM // 2
        rows_used = be_ref[n_blocks + 1 + N_EXPERTS + j]

        @pl.when(rows_used > half)
        def _():
            ffn(slice(0, BM))

        @pl.when(rows_used <= half)
        def _():
            ffn(slice(0, half))
            ys_ref[half:BM, :] = jnp.zeros((BM - half, D_MODEL), f32)


def _experts(layer, lay, block_e, xs, w_gate, w_up, w_down):
    n_blocks = lay.n_blocks
    d_ff = w_gate.shape[-1]
    assert d_ff == D_MODEL
    used = lambda j, be: jnp.minimum(j, be[n_blocks] - 1)
    any_spec = pl.BlockSpec(memory_space=pl.ANY)
    grid_spec = pltpu.PrefetchScalarGridSpec(
        num_scalar_prefetch=1,
        grid=(n_blocks,),
        in_specs=[pl.BlockSpec((BM, XS_W), lambda j, be: (used(j, be), 0)), any_spec, any_spec, any_spec],
        out_specs=pl.BlockSpec((BM, D_MODEL), lambda j, be: (used(j, be), 0)),
        scratch_shapes=[pltpu.VMEM((3, D_MODEL, d_ff), f32), pltpu.VMEM((3, D_MODEL, d_ff), bf16),
                        pltpu.SemaphoreType.DMA((3,))],
    )
    return pl.pallas_call(
        functools.partial(_expert_kernel, n_blocks=n_blocks, layer=layer),
        grid_spec=grid_spec,
        out_shape=jax.ShapeDtypeStruct((lay.cap, D_MODEL), f32),
        compiler_params=_cparams(1),
        name="moe_experts",
    )(block_e, xs, w_gate, w_up, w_down)


def _combine_kernel(tab_ref, ys_hbm, cols_ref, x_ref, gts_ref, gtt_ref, *rest, n_steps, np_steps, final):
    if final:
        gf_ref, yp_ref, ysm_ref, staged, sem = rest
    else:
        xo_ref, staged, sem = rest
    i = pl.program_id(0)
    slot = i % 2
    is_s = i >= np_steps

    def start_step(step, slot_):
        for c in range(STEP_TILES * N_CHUNKS):
            src = jnp.maximum(tab_ref[step * (STEP_TILES * N_CHUNKS) + c], 0)
            pltpu.make_async_copy(ys_hbm.at[pl.ds(pl.multiple_of(src, SUBLANES), SUBLANES)],
                                  staged.at[slot_, pl.ds(c * SUBLANES, SUBLANES)], sem.at[slot_]).start()

    @pl.when(i == 0)
    def _():
        start_step(0, 0)

    @pl.when(i + 1 < n_steps)
    def _():
        start_step(i + 1, 1 - slot)

    pltpu.make_async_copy(ys_hbm.at[pl.ds(0, STEP_TILES * SORT_ROWS)], staged.at[slot], sem.at[slot]).wait()

    def step(sample):
        lane = lax.broadcasted_iota(i32, (TM, SORT_ROWS), 1).astype(f32)
        for t in range(STEP_TILES):
            rs = slice(t * TM, (t + 1) * TM)
            unperm = jnp.where((lane == cols_ref[rs, 0:1]) | (lane == cols_ref[rs, 1:2]), 1.0, 0.0).astype(bf16)
            parts = _split3(staged[slot, t * SORT_ROWS:(t + 1) * SORT_ROWS, :])
            ff = sum(jnp.dot(unperm, part, preferred_element_type=f32) for part in parts)
            xn = x_ref[rs, :] + (gtt_ref[rs, :] if sample else gts_ref[...]) * ff
            if final:
                (ysm_ref if sample else yp_ref)[rs, :] = _rms(xn, gf_ref[...])
            else:
                xo_ref[rs, :] = xn

    pl.when(is_s)(functools.partial(step, True))
    pl.when(jnp.logical_not(is_s))(functools.partial(step, False))


def _combine(tl, layer, tab, ys, cols, x, mod_seq, mod_tok, g_final):
    final = g_final is not None
    tok_spec = pl.BlockSpec((TS, D_MODEL), lambda i, t: (i, 0))
    gt_seq, gt_tok = _mod_specs(tl, layer, 5)
    in_specs = [pl.BlockSpec(memory_space=pl.ANY), pl.BlockSpec((TS, LANES), lambda i, t: (i, 0)),
                tok_spec, gt_seq, gt_tok]
    args = [tab, ys, cols, x, mod_seq, mod_tok]
    if final:
        in_specs.append(pl.BlockSpec((1, D_MODEL), lambda i, t: (0, 0)))
        args.append(g_final)
        out_specs = [pl.BlockSpec((TS, D_MODEL), lambda i, t: (tl.prompt_block(i), 0)),
                     pl.BlockSpec((TS, D_MODEL), lambda i, t: (tl.sample_block(i), 0))]
        out_shape = [jax.ShapeDtypeStruct((tl.n_prompt, D_MODEL), f32),
                     jax.ShapeDtypeStruct((tl.n_sample, D_MODEL), f32)]
    else:
        out_specs = tok_spec
        out_shape = jax.ShapeDtypeStruct((tl.n_tok, D_MODEL), f32)
    grid_spec = pltpu.PrefetchScalarGridSpec(
        num_scalar_prefetch=1,
        grid=(tl.n_steps,),
        in_specs=in_specs,
        out_specs=out_specs,
        scratch_shapes=[pltpu.VMEM((2, STEP_TILES * SORT_ROWS, D_MODEL), f32), pltpu.SemaphoreType.DMA((2,))],
    )
    return pl.pallas_call(
        functools.partial(_combine_kernel, n_steps=tl.n_steps, np_steps=tl.np_steps, final=final),
        grid_spec=grid_spec,
        out_shape=out_shape,
        compiler_params=_cparams(1),
        name="moe_combine",
    )(*args)


def _rope_tables(tl):
    half = HEAD_D // 2
    inv = ROPE_BASE ** (-jnp.arange(half, dtype=f32) / half)
    pos_p = jnp.arange(tl.tp, dtype=i32)
    pos_s = PAST_LEN + jnp.arange(tl.ts, dtype=i32)
    pos = jnp.concatenate([pos_p, jnp.tile(pos_s, tl.bs)])
    ang = pos.astype(f32)[:, None] * inv[None, :]
    cos, sin = jnp.cos(ang), jnp.sin(ang)
    return jnp.concatenate([cos, cos], axis=-1), jnp.concatenate([-sin, sin], axis=-1)


def kernel(x_prompt, x_sample, state_ret, state_conv, c_prompt, c_sample, w_mod, b_mod, g_mix_norm, w_in,
           w_conv, b_conv, g_conv_ln, b_conv_ln, g_ret_gn, w_out, g_ffn_norm, w_router, b_router,
           w_exp_gate, w_exp_up, w_exp_down, g_final):
    bp, tp, _ = x_prompt.shape
    bs, ts, _ = x_sample.shape
    depth = w_mod.shape[0]
    tl = _Tiles(bp, tp, bs, ts)
    lay = _Layout(tl.n_tiles)

    c_all = jnp.concatenate([c_prompt, jnp.repeat(c_sample, ts, axis=0)], axis=0)
    mod_seq, mod_tok = _modulation(c_all, bp, w_mod, b_mod)
    mod_seq = mod_seq.reshape(depth, bp, 1, N_MOD * D_MODEL)

    cos_tab, sin_tab = _rope_tables(tl)
    w_in_bf = w_in.astype(bf16)
    w_out_bf = w_out.astype(bf16)
    wr_pad = jnp.pad(w_router.astype(f32), ((0, 0), (0, LANES - N_EXPERTS)))
    wr_bf = wr_pad.astype(bf16)
    br_pad = jnp.pad(b_router.astype(f32), (0, LANES - N_EXPERTS)).reshape(1, LANES)
    vec3 = lambda t: t.reshape(depth, 1, t.shape[-1])
    g_mix3, g_ffn3, gn3 = vec3(g_mix_norm), vec3(g_ffn_norm), vec3(g_ret_gn)
    b_conv3, g_ln3, b_ln3 = vec3(b_conv), vec3(g_conv_ln), vec3(b_conv_ln)

    x = (x_prompt.reshape(tl.n_prompt, D_MODEL), x_sample.reshape(tl.n_sample, D_MODEL))
    ret_p, conv_p = [], []
    ret_s_all = conv_s_all = None
    for layer in range(depth):
        q, k, v, gate, a = _inproj(tl, layer, x, mod_seq, mod_tok, g_mix3, w_in_bf, cos_tab, sin_tab)
        ro_p, s_p = _retention_prompt(tl, layer, q, k, v, gate, gn3)
        ro_s, ret_s_all = _retention_sample(tl, layer, q, k, v, gate, state_ret, gn3, ret_s_all)
        co_p, buf_p = _conv_prompt(tl, layer, a, w_conv, b_conv3, g_ln3, b_ln3)
        co_s, conv_s_all = _conv_sample(tl, layer, a, state_conv, w_conv, b_conv3, g_ln3, b_ln3, conv_s_all)
        x_mid, h2, rows, cols, tile_counts = _outproj(
            tl, layer, x, ro_p, ro_s, co_p, co_s, mod_seq, mod_tok, g_ffn3, w_out_bf, wr_bf, br_pad)
        tab, block_e = _moe_tables(lay, tile_counts[:, :, 0])
        xs = _dispatch(tl, lay, tab, h2, rows, cols)
        ys = _experts(layer, lay, block_e, xs, w_exp_gate, w_exp_up, w_exp_down)
        last = layer == depth - 1
        x = _combine(tl, layer, tab, ys, cols, x_mid, mod_seq, mod_tok,
                     g_final.reshape(1, D_MODEL) if last else None)
        ret_p.append(s_p)
        conv_p.append(buf_p)
    y_p, y_s = x
    return (y_p.reshape(bp, tp, D_MODEL), y_s.reshape(bs, ts, D_MODEL),
            jnp.stack(ret_p), jnp.stack(conv_p), ret_s_all, conv_s_all)
```

```python
import functools

import jax
import jax.numpy as jnp
from jax import lax
from jax.experimental import pallas as pl
from jax.experimental.pallas import tpu as pltpu

f32 = jnp.float32
bf16 = jnp.bfloat16
i32 = jnp.int32

D_MODEL = 1024
D_RET = 512
D_CONV = 512
N_HEADS = 4
HEAD_D = 128
RET_CHUNK = 128
RET_CHUNKS_PER_STEP = 16
ROPE_BASE = 10000.0
CONV_WIDTH = 31
CONV_HALO = CONV_WIDTH - 1
N_EXPERTS = 16
N_GROUPS = 4
GROUP_SIZE = N_EXPERTS // N_GROUPS
N_MOD = 6
EPS = 1e-6
PAST_LEN = 16384
D_IN = 4 * D_RET + 2 * D_CONV

LANES = 128
SUBLANES = 8
TM = 256
STEP_TILES = 2
TS = TM * STEP_TILES
BM = 1024
BM_PART = 256
CONV_ROWS = 256
NORM_ROWS = 64
SAMPLE_GROUP = 16
HALO_PAD = 32
V7X_VMEM_BYTES = 64 * 1024 * 1024
VMEM_LIMIT = V7X_VMEM_BYTES - 8 * 1024 * 1024


def _cparams(n_axes, vmem=VMEM_LIMIT):
    return pltpu.CompilerParams(dimension_semantics=("arbitrary",) * n_axes, vmem_limit_bytes=vmem)


def _mod_kernel(c_ref, w_ref, b_ref, seq_ref, tok_ref):
    cond = jax.nn.silu(c_ref[...]).astype(bf16)
    mod = jnp.dot(cond, w_ref[...].astype(bf16), preferred_element_type=f32) + b_ref[...]
    n_seq = seq_ref.shape[0]
    seq_ref[...] = mod[0:n_seq, :]
    tok_ref[...] = mod[n_seq:, :]


def _modulation(c_all, n_seq, w_mod, b_mod):
    depth = w_mod.shape[0]
    m = c_all.shape[0]
    n_tok = m - n_seq
    assert n_seq % SUBLANES == 0
    return pl.pallas_call(
        _mod_kernel,
        grid=(depth, N_MOD),
        in_specs=[
            pl.BlockSpec((m, D_MODEL), lambda l, j: (0, 0)),
            pl.BlockSpec((None, D_MODEL, D_MODEL), lambda l, j: (l, 0, j)),
            pl.BlockSpec((None, 1, D_MODEL), lambda l, j: (l, 0, j)),
        ],
        out_specs=[pl.BlockSpec((None, n_seq, D_MODEL), lambda l, j: (l, 0, j)),
                   pl.BlockSpec((None, n_tok, D_MODEL), lambda l, j: (l, 0, j))],
        out_shape=[jax.ShapeDtypeStruct((depth, n_seq, N_MOD * D_MODEL), f32),
                   jax.ShapeDtypeStruct((depth, n_tok, N_MOD * D_MODEL), f32)],
        compiler_params=_cparams(2),
        name="modulation",
    )(c_all, w_mod, b_mod.reshape(depth, 1, N_MOD * D_MODEL))


class _Tiles:
    def __init__(self, bp, tp, bs, ts):
        self.bp, self.tp, self.bs, self.ts = bp, tp, bs, ts
        self.n_prompt = bp * tp
        self.n_sample = bs * ts
        self.n_tok = self.n_prompt + self.n_sample
        assert tp % TS == 0 and self.n_sample % TS == 0
        self.tiles_per_seq = tp // TM
        self.np_tiles = self.n_prompt // TM
        self.n_tiles = self.n_tok // TM
        self.steps_per_seq = tp // TS
        self.np_steps = self.n_prompt // TS
        self.n_steps = self.n_tok // TS

    def prompt_block(self, i):
        return jnp.minimum(i, self.np_steps - 1)

    def sample_block(self, i):
        return jnp.maximum(i - self.np_steps, 0)

    def seq_index(self, i):
        return jnp.minimum(i // self.steps_per_seq, self.bp - 1)


def _mod_specs(tl, layer, col):
    seq = pl.BlockSpec((None, None, 1, D_MODEL), lambda i, *_: (layer, tl.seq_index(i), 0, col))
    tok = pl.BlockSpec((None, TS, D_MODEL), lambda i, *_: (layer, tl.sample_block(i), col))
    return seq, tok


def _rms(x, g):
    return x * lax.rsqrt(jnp.mean(x * x, axis=-1, keepdims=True) + EPS) * g


def _head_norm_gate(o, gn, gate):
    mu = jnp.mean(o, axis=-1, keepdims=True)
    var = jnp.mean(jnp.square(o - mu), axis=-1, keepdims=True)
    return jax.nn.silu(gate) * ((o - mu) * lax.rsqrt(var + EPS) * gn)


def _dot_nt(a, b):
    return lax.dot_general(a, b, (((1,), (1,)), ((), ())), preferred_element_type=f32)


def _dot_tn(a, b):
    return lax.dot_general(a, b, (((0,), (0,)), ((), ())), preferred_element_type=f32)


def _ret_prompt_kernel(q_ref, k_ref, v_ref, gate_ref, dec_ref, qd_ref, kd_ref, cd_ref, gn_ref,
                       o_ref, s_out_ref, s_ref):
    c = pl.program_id(1)

    @pl.when(c == 0)
    def _():
        s_ref[...] = jnp.zeros_like(s_ref)

    for ci in range(RET_CHUNKS_PER_STEP):
        rows = slice(ci * RET_CHUNK, (ci + 1) * RET_CHUNK)
        for hd in range(N_HEADS):
            sl = slice(hd * HEAD_D, (hd + 1) * HEAD_D)
            kh = k_ref[rows, sl]
            qb = q_ref[rows, sl]
            kb = kh.astype(bf16)
            vb = v_ref[rows, sl]
            s_old = s_ref[hd]
            scores = _dot_nt(qb, kb) * dec_ref[hd]
            inner = jnp.dot(scores.astype(bf16), vb, preferred_element_type=f32)
            cross = jnp.dot(qb, s_old.astype(bf16), preferred_element_type=f32) * qd_ref[hd]
            s_ref[hd] = s_old * cd_ref[hd] + _dot_tn((kh * kd_ref[hd]).astype(bf16), vb)
            o_ref[rows, sl] = _head_norm_gate(inner + cross, gn_ref[:, sl], gate_ref[rows, sl]).astype(bf16)

    @pl.when(c == pl.num_programs(1) - 1)
    def _():
        s_out_ref[...] = s_ref[...]


def _decay_tables(chunk, true_len):
    lg = jnp.log(1.0 - 2.0 ** (-5.0 - jnp.arange(N_HEADS, dtype=f32)))
    idx = jnp.arange(chunk, dtype=f32)
    rel = idx[:, None] - idx[None, :]
    decay = jnp.where(rel[None] >= 0, jnp.exp(jnp.maximum(rel, 0.0)[None] * lg[:, None, None]), 0.0)
    q_decay = jnp.exp((idx[None, :] + 1.0) * lg[:, None])
    k_decay = jnp.exp((true_len - 1.0 - idx[None, :]) * lg[:, None])
    c_decay = jnp.exp(true_len * lg)
    return decay, q_decay, k_decay, c_decay


def _retention_prompt(tl, layer, q, k, v, gate, g_ret_gn):
    step_rows = RET_CHUNK * RET_CHUNKS_PER_STEP
    assert tl.tp % step_rows == 0
    n_chunks = tl.tp // step_rows
    decay, q_decay, k_decay, c_decay = _decay_tables(RET_CHUNK, RET_CHUNK)
    bcast = lambda t: jnp.broadcast_to(t[:, :, None], (N_HEADS, RET_CHUNK, HEAD_D))
    cd = jnp.broadcast_to(c_decay[:, None, None], (N_HEADS, 1, HEAD_D))
    tok_spec = pl.BlockSpec((step_rows, D_RET), lambda b, c: (b * n_chunks + c, 0))
    tab_spec = pl.BlockSpec((N_HEADS, RET_CHUNK, HEAD_D), lambda b, c: (0, 0, 0))
    return pl.pallas_call(
        _ret_prompt_kernel,
        grid=(tl.bp, n_chunks),
        in_specs=[tok_spec] * 4 + [tab_spec] * 3 + [
            pl.BlockSpec((N_HEADS, 1, HEAD_D), lambda b, c: (0, 0, 0)),
            pl.BlockSpec((None, 1, D_RET), lambda b, c: (layer, 0, 0)),
        ],
        out_specs=[tok_spec, pl.BlockSpec((None, N_HEADS, HEAD_D, HEAD_D), lambda b, c: (b, 0, 0, 0))],
        out_shape=[jax.ShapeDtypeStruct((tl.n_prompt, D_RET), bf16),
                   jax.ShapeDtypeStruct((tl.bp, N_HEADS, HEAD_D, HEAD_D), f32)],
        scratch_shapes=[pltpu.VMEM((N_HEADS, HEAD_D, HEAD_D), f32)],
        compiler_params=_cparams(2),
        name="retention_prompt",
    )(q, k, v, gate, decay, bcast(q_decay), bcast(k_decay), cd, g_ret_gn)


def _ret_sample_kernel(q_ref, k_ref, v_ref, gate_ref, s_in_ref, dec_ref, qd_ref, kd_ref, cd_ref, gn_ref,
                       *rest, ts):
    o_ref, s_all_ref = rest[-2:]
    s_out_ref = s_all_ref.at[0]
    for other in range(1, s_all_ref.shape[0]):
        s_all_ref[other] = jnp.zeros(s_all_ref.shape[1:], f32)
    seqs_per_tile = SUBLANES // ts
    row = lax.broadcasted_iota(i32, (SUBLANES, HEAD_D), 0)
    q_all = q_ref[...].astype(f32)
    v_all = v_ref[...].astype(f32)
    outs = []
    for t in range(SAMPLE_GROUP // seqs_per_tile):
        rows = slice(t * SUBLANES, (t + 1) * SUBLANES)
        heads = []
        for hd in range(N_HEADS):
            sl = slice(hd * HEAD_D, (hd + 1) * HEAD_D)
            qh = q_all[rows, sl]
            kh = k_ref[rows, sl] * kd_ref[hd]
            vb = v_all[rows, sl].astype(bf16)
            qb = qh.astype(bf16)
            scores = _dot_nt(qb, k_ref[rows, sl].astype(bf16)) * dec_ref[hd]
            o = jnp.dot(scores.astype(bf16), vb, preferred_element_type=f32)
            for s in range(seqs_per_tile):
                b = t * seqs_per_tile + s
                mine = (row >= s * ts) & (row < (s + 1) * ts)
                s_old = s_in_ref[b, hd]
                q_s = jnp.where(mine, qh, 0.0).astype(bf16)
                k_s = jnp.where(mine, kh, 0.0).astype(bf16)
                o = o + jnp.dot(q_s, s_old.astype(bf16), preferred_element_type=f32) * qd_ref[hd]
                s_out_ref[b, hd] = s_old * cd_ref[hd] + _dot_tn(k_s, vb)
            heads.append(_head_norm_gate(o, gn_ref[:, sl], gate_ref[rows, sl]))
        outs.append(jnp.concatenate(heads, axis=-1))
    o_ref[...] = jnp.concatenate(outs, axis=0).astype(bf16)


def _retention_sample(tl, layer, q, k, v, gate, state_ret, g_ret_gn, prev_states):
    ts = tl.ts
    depth = state_ret.shape[0]
    assert SUBLANES % ts == 0 and tl.bs % SAMPLE_GROUP == 0
    seqs_per_tile = SUBLANES // ts
    decay, q_decay, k_decay, c_decay = _decay_tables(ts, ts)
    eye = jnp.eye(seqs_per_tile, dtype=f32)
    dec_tile = jnp.einsum("ab,hij->haibj", eye, decay).reshape(N_HEADS, SUBLANES, SUBLANES)
    tile_rows = lambda t: jnp.broadcast_to(jnp.tile(t, (1, seqs_per_tile))[:, :, None],
                                           (N_HEADS, SUBLANES, HEAD_D))
    cd = jnp.broadcast_to(c_decay[:, None, None], (N_HEADS, 1, HEAD_D))
    rows = SAMPLE_GROUP * ts
    first = tl.n_prompt // rows
    tok_spec = pl.BlockSpec((rows, D_RET), lambda i: (first + i, 0))
    const3 = lambda shape: pl.BlockSpec(shape, lambda i: (0, 0, 0))
    st_block = (SAMPLE_GROUP, N_HEADS, HEAD_D, HEAD_D)
    in_specs = [tok_spec] * 4 + [
        pl.BlockSpec((None,) + st_block, lambda i: (layer, i, 0, 0, 0)),
        const3((N_HEADS, SUBLANES, SUBLANES)),
        const3((N_HEADS, SUBLANES, HEAD_D)),
        const3((N_HEADS, SUBLANES, HEAD_D)),
        const3((N_HEADS, 1, HEAD_D)),
        pl.BlockSpec((None, 1, D_RET), lambda i: (layer, 0, 0)),
    ]
    args = [q, k, v, gate, state_ret, dec_tile, tile_rows(q_decay), tile_rows(k_decay), cd, g_ret_gn]
    if prev_states is None:
        state_spec = pl.BlockSpec((depth,) + st_block, lambda i: (0, i, 0, 0, 0))
        aliases = {}
    else:
        state_spec = pl.BlockSpec((1,) + st_block, lambda i: (layer, i, 0, 0, 0))
        in_specs.append(pl.BlockSpec(memory_space=pl.ANY))
        args.append(prev_states)
        aliases = {len(args) - 1: 1}
    return pl.pallas_call(
        functools.partial(_ret_sample_kernel, ts=ts),
        grid=(tl.bs // SAMPLE_GROUP,),
        in_specs=in_specs,
        out_specs=[pl.BlockSpec((rows, D_RET), lambda i: (i, 0)), state_spec],
        out_shape=[jax.ShapeDtypeStruct((tl.n_sample, D_RET), bf16),
                   jax.ShapeDtypeStruct((depth, tl.bs, N_HEADS, HEAD_D, HEAD_D), f32)],
        input_output_aliases=aliases,
        compiler_params=_cparams(1),
        name="retention_sample",
    )(*args)


def _ln_silu(cv, g, b):
    mu = jnp.mean(cv, axis=-1, keepdims=True)
    var = jnp.mean(jnp.square(cv - mu), axis=-1, keepdims=True)
    return jax.nn.silu((cv - mu) * lax.rsqrt(var + EPS) * g + b)


def _conv_taps(window, w_ref, b_ref, n_rows):
    cols = []
    for col in range(D_CONV // LANES):
        lanes = slice(col * LANES, (col + 1) * LANES)
        acc = jnp.broadcast_to(b_ref[:, lanes], (n_rows, LANES))
        for tap in range(CONV_WIDTH):
            acc = acc + window(col, tap) * w_ref[tap:tap + 1, lanes]
        cols.append(acc)
    return jnp.concatenate(cols, axis=-1)


def _conv_sample_kernel(a_ref, st_ref, w_ref, b_ref, g_ref, bl_ref, *rest, ts):
    o_ref, buf_all_ref, full_ref, cv_ref = rest[-4:]
    buf_ref = buf_all_ref.at[0]
    for other in range(1, buf_all_ref.shape[0]):
        buf_all_ref[other] = jnp.zeros(buf_all_ref.shape[1:], f32)
    for s in range(SAMPLE_GROUP):
        for col in range(D_CONV // LANES):
            lanes = slice(col * LANES, (col + 1) * LANES)
            full_ref[col, 0:CONV_HALO, :] = st_ref[:, s, lanes]
            full_ref[col, CONV_HALO:CONV_HALO + ts, :] = a_ref[s * ts:(s + 1) * ts, lanes]
            buf_ref[s, :, lanes] = full_ref[col, ts:ts + CONV_HALO, :]
        cv_ref[s * ts:(s + 1) * ts, :] = _conv_taps(lambda col, tap: full_ref[col, tap:tap + ts, :],
                                                    w_ref, b_ref, ts)
    o_ref[...] = _ln_silu(cv_ref[...], g_ref[...], bl_ref[...]).astype(bf16)


def _conv_sample(tl, layer, a, state_conv, w_conv, b_conv, g_ln, b_ln, prev_bufs):
    ts = tl.ts
    depth = state_conv.shape[0]
    rows = SAMPLE_GROUP * ts
    first_block = tl.n_prompt // rows
    vec = pl.BlockSpec((None, 1, D_CONV), lambda i: (layer, 0, 0))
    in_specs = [pl.BlockSpec((rows, D_CONV), lambda i: (first_block + i, 0)),
                pl.BlockSpec((None, CONV_HALO, SAMPLE_GROUP, D_CONV), lambda i: (layer, 0, i, 0)),
                pl.BlockSpec((None, CONV_WIDTH, D_CONV), lambda i: (layer, 0, 0)),
                vec, vec, vec]
    args = [a, state_conv.transpose(0, 2, 1, 3), w_conv, b_conv, g_ln, b_ln]
    buf_block = (SAMPLE_GROUP, CONV_HALO, D_CONV)
    if prev_bufs is None:
        buf_spec = pl.BlockSpec((depth,) + buf_block, lambda i: (0, i, 0, 0))
        aliases = {}
    else:
        buf_spec = pl.BlockSpec((1,) + buf_block, lambda i: (layer, i, 0, 0))
        in_specs.append(pl.BlockSpec(memory_space=pl.ANY))
        args.append(prev_bufs)
        aliases = {len(args) - 1: 1}
    return pl.pallas_call(
        functools.partial(_conv_sample_kernel, ts=ts),
        grid=(tl.bs // SAMPLE_GROUP,),
        in_specs=in_specs,
        out_specs=[pl.BlockSpec((rows, D_CONV), lambda i: (i, 0)), buf_spec],
        out_shape=[jax.ShapeDtypeStruct((tl.n_sample, D_CONV), bf16),
                   jax.ShapeDtypeStruct((depth, tl.bs, CONV_HALO, D_CONV), f32)],
        input_output_aliases=aliases,
        scratch_shapes=[pltpu.VMEM((D_CONV // LANES, CONV_HALO + ts + SUBLANES, LANES), f32),
                        pltpu.VMEM((rows, D_CONV), f32)],
        compiler_params=_cparams(1),
        name="conv_sample",
    )(*args)


N_CONV_COLS = D_CONV // LANES
N_CONV_CHUNKS = TM // CONV_ROWS


def _inproj_kernel(*refs, np_steps, split_x):
    if split_x:
        xp_ref, xs_ref = refs[:2]
        refs = refs[2:]
    else:
        x_ref = refs[0]
        refs = refs[1:]
    (shs_ref, sht_ref, scs_ref, sct_ref, g_ref, w_ref, cos_ref, sin_ref,
     q_ref, k_ref, v_ref, gate_ref, a_ref) = refs
    is_s = pl.program_id(0) >= np_steps

    def step(sample):
        for t in range(STEP_TILES):
            rs = slice(t * TM, (t + 1) * TM)
            if split_x:
                x = xs_ref[rs, :] if sample else xp_ref[rs, :]
            else:
                x = x_ref[rs, :]
            sh, sc = (sht_ref[rs, :], sct_ref[rs, :]) if sample else (shs_ref[...], scs_ref[...])
            hb = (_rms(x, g_ref[...]) * (1.0 + sc) + sh).astype(bf16)

            def group(g):
                return jnp.dot(hb, w_ref[:, g * D_RET:(g + 1) * D_RET], preferred_element_type=f32)

            cos = cos_ref[rs, :]
            sin = sin_ref[rs, :]

            def rope(th):
                return th * cos + pltpu.roll(th, HEAD_D // 2, 1) * sin

            qg, kg = group(0), group(1)
            for hd in range(N_HEADS):
                sl = slice(hd * HEAD_D, (hd + 1) * HEAD_D)
                q_ref[rs, sl] = rope(qg[:, sl]).astype(bf16)
                k_ref[rs, sl] = rope(kg[:, sl]) * (HEAD_D ** -0.5)
            v_ref[rs, :] = group(2).astype(bf16)
            gate_ref[rs, :] = group(3)
            a_ref[rs, :] = group(4) * jax.nn.sigmoid(group(5))

    pl.when(is_s)(functools.partial(step, True))
    pl.when(jnp.logical_not(is_s))(functools.partial(step, False))


def _conv_prompt_kernel(a_ref, w_ref, b_ref, g_ref, bl_ref, o_ref, buf_ref, full_ref, cv_ref):
    j = pl.program_id(1)

    @pl.when(j == 0)
    def _():
        full_ref[:, 0:HALO_PAD, :] = jnp.zeros((N_CONV_COLS, HALO_PAD, LANES), f32)

    @pl.when(j > 0)
    def _():
        full_ref[:, 0:HALO_PAD, :] = full_ref[:, TM:TM + HALO_PAD, :]

    for col in range(N_CONV_COLS):
        full_ref[col, HALO_PAD:HALO_PAD + TM, :] = a_ref[:, col * LANES:(col + 1) * LANES]
    shift = HALO_PAD - CONV_HALO

    def taps(idx, carry):
        col = idx // N_CONV_CHUNKS
        r0 = pl.multiple_of((idx % N_CONV_CHUNKS) * CONV_ROWS, CONV_ROWS)
        acc = jnp.broadcast_to(b_ref[col], (CONV_ROWS, LANES))
        for tap in range(CONV_WIDTH):
            acc = acc + full_ref[col, pl.ds(r0 + (tap + shift), CONV_ROWS), :] * w_ref[col, tap:tap + 1, :]
        cv_ref[col, pl.ds(r0, CONV_ROWS), :] = acc
        return carry

    lax.fori_loop(0, N_CONV_COLS * N_CONV_CHUNKS, taps, 0)

    for r0 in range(0, TM, NORM_ROWS):
        cv = jnp.concatenate([cv_ref[col, r0:r0 + NORM_ROWS, :] for col in range(N_CONV_COLS)], axis=-1)
        o_ref[r0:r0 + NORM_ROWS, :] = _ln_silu(cv, g_ref[...], bl_ref[...]).astype(bf16)

    @pl.when(j == pl.num_programs(1) - 1)
    def _():
        buf_ref[...] = a_ref[TM - CONV_HALO:TM, :]


def _conv_prompt(tl, layer, a, w_conv, b_conv, g_ln, b_ln):
    tps = tl.tiles_per_seq
    depth = w_conv.shape[0]
    w_cols = w_conv.reshape(depth, CONV_WIDTH, N_CONV_COLS, LANES).transpose(0, 2, 1, 3)
    b_cols = b_conv.reshape(depth, N_CONV_COLS, 1, LANES)
    vec = pl.BlockSpec((None, 1, D_CONV), lambda b, j: (layer, 0, 0))
    return pl.pallas_call(
        _conv_prompt_kernel,
        grid=(tl.bp, tps),
        in_specs=[pl.BlockSpec((TM, D_CONV), lambda b, j: (b * tps + j, 0)),
                  pl.BlockSpec((None, N_CONV_COLS, CONV_WIDTH, LANES), lambda b, j: (layer, 0, 0, 0)),
                  pl.BlockSpec((None, N_CONV_COLS, 1, LANES), lambda b, j: (layer, 0, 0, 0)),
                  vec, vec],
        out_specs=[pl.BlockSpec((TM, D_CONV), lambda b, j: (b * tps + j, 0)),
                   pl.BlockSpec((None, CONV_HALO, D_CONV), lambda b, j: (b, 0, 0))],
        out_shape=[jax.ShapeDtypeStruct((tl.n_prompt, D_CONV), bf16),
                   jax.ShapeDtypeStruct((tl.bp, CONV_HALO, D_CONV), f32)],
        scratch_shapes=[pltpu.VMEM((N_CONV_COLS, HALO_PAD + TM, LANES), f32),
                        pltpu.VMEM((N_CONV_COLS, TM, LANES), f32)],
        compiler_params=_cparams(2),
        name="conv_prompt",
    )(a, w_cols, b_cols, g_ln, b_ln)


def _inproj(tl, layer, x, mod_seq, mod_tok, g_norm, w_in_bf, cos_tab, sin_tab):
    split_x = isinstance(x, tuple)
    tok_spec = pl.BlockSpec((TS, D_MODEL), lambda i: (i, 0))
    if split_x:
        x_args = list(x)
        x_specs = [pl.BlockSpec((TS, D_MODEL), lambda i: (tl.prompt_block(i), 0)),
                   pl.BlockSpec((TS, D_MODEL), lambda i: (tl.sample_block(i), 0))]
    else:
        x_args, x_specs = [x], [tok_spec]
    sh_seq, sh_tok = _mod_specs(tl, layer, 0)
    sc_seq, sc_tok = _mod_specs(tl, layer, 1)

    def table_block(i):
        return jnp.where(i < tl.np_steps, i % tl.steps_per_seq, tl.steps_per_seq + tl.sample_block(i))

    tab_spec = pl.BlockSpec((TS, HEAD_D), lambda i: (table_block(i), 0))
    row_spec = pl.BlockSpec((TS, D_RET), lambda i: (i, 0))
    row_sd = lambda dt: jax.ShapeDtypeStruct((tl.n_tok, D_RET), dt)
    return pl.pallas_call(
        functools.partial(_inproj_kernel, np_steps=tl.np_steps, split_x=split_x),
        grid=(tl.n_steps,),
        in_specs=x_specs + [
            sh_seq, sh_tok, sc_seq, sc_tok,
            pl.BlockSpec((None, 1, D_MODEL), lambda i: (layer, 0, 0)),
            pl.BlockSpec((None, D_MODEL, D_IN), lambda i: (layer, 0, 0)),
            tab_spec, tab_spec,
        ],
        out_specs=[row_spec] * 5,
        out_shape=[row_sd(bf16), row_sd(f32), row_sd(bf16), row_sd(f32), row_sd(f32)],
        compiler_params=_cparams(1),
        name="inproj",
    )(*x_args, mod_seq, mod_tok, mod_seq, mod_tok, g_norm, w_in_bf, cos_tab, sin_tab)


SORT_ROWS = 2 * TM + N_EXPERTS * SUBLANES
XS_HALF = D_MODEL // 2
XS_W = XS_HALF + LANES
u32 = jnp.uint32
ROUTE_POS, ROUTE_W0, ROUTE_W1 = (0, 2), (2, 5), (5, 8)


def _pack_bf16_pair(x):
    lo = lax.shift_right_logical(lax.bitcast_convert_type(x[:, 0:XS_HALF], u32), u32(16))
    hi = lax.bitcast_convert_type(x[:, XS_HALF:D_MODEL], u32) & u32(0xFFFF0000)
    return hi | lo


def _unpack_bf16_pair(words):
    lo = lax.bitcast_convert_type(lax.shift_left(words, u32(16)), f32).astype(bf16)
    hi = lax.bitcast_convert_type(words & u32(0xFFFF0000), f32).astype(bf16)
    return lo, hi


def _split3(x):
    a = x.astype(bf16)
    r = x - a.astype(f32)
    b = r.astype(bf16)
    c = (r - b.astype(f32)).astype(bf16)
    return a, b, c


def _first_of4(vals, m):
    return jnp.where(vals[0] == m, 0.0, jnp.where(vals[1] == m, 1.0, jnp.where(vals[2] == m, 2.0, 3.0)))


def _rows_to_tile(rows, n_rows):
    sub = lax.broadcasted_iota(i32, (n_rows, TM), 0)
    out = jnp.zeros((n_rows, TM), f32)
    for r, val in enumerate(rows):
        out = jnp.where(sub == r, val, out)
    return out


def _outproj_kernel(*refs, np_steps, split_x):
    if split_x:
        xp_ref, xs_ref = refs[:2]
        refs = refs[2:]
    else:
        x_ref = refs[0]
        refs = refs[1:]
    (retp_ref, rets_ref, cvp_ref, cvs_ref, gts_ref, gtt_ref, shs_ref, sht_ref, scs_ref, sct_ref,
     g_ref, wo_ref, wr_ref, br_ref,
     xo_ref, h2_ref, rows_ref, cols_ref, cnt_ref) = refs
    is_s = pl.program_id(0) >= np_steps

    def step(sample):
        for t in range(STEP_TILES):
            rs = slice(t * TM, (t + 1) * TM)
            if split_x:
                x = xs_ref[rs, :] if sample else xp_ref[rs, :]
            else:
                x = x_ref[rs, :]
            ret = rets_ref[rs, :] if sample else retp_ref[rs, :]
            cv = cvs_ref[rs, :] if sample else cvp_ref[rs, :]
            gt, sh, sc = ((gtt_ref[rs, :], sht_ref[rs, :], sct_ref[rs, :]) if sample
                          else (gts_ref[...], shs_ref[...], scs_ref[...]))
            mix_out = (jnp.dot(ret, wo_ref[0:D_RET, :], preferred_element_type=f32)
                       + jnp.dot(cv, wo_ref[D_RET:D_RET + D_CONV, :], preferred_element_type=f32))
            xn = x + gt * mix_out
            xo_ref[rs, :] = xn
            h2 = _rms(xn, g_ref[...]) * (1.0 + sc) + sh
            h_hi = h2.astype(bf16)
            h2_ref[rs, :] = h_hi
            rows, cols, cnt = _route_tile(h_hi, wr_ref, br_ref)
            rows_ref[t] = rows
            cols_ref[rs, :] = cols
            cnt_ref[t] = cnt

    pl.when(is_s)(functools.partial(step, True))
    pl.when(jnp.logical_not(is_s))(functools.partial(step, False))


def _route_tile(h_bf, wr_ref, br_ref):
    logits = jnp.dot(h_bf, wr_ref[...], preferred_element_type=f32) + br_ref[...]
    lt = logits.T
    row = [lt[e:e + 1, :] for e in range(N_EXPERTS)]
    top = functools.reduce(jnp.maximum, row)
    ex = [jnp.exp(r - top) for r in row]
    den = functools.reduce(jnp.add, ex)
    p = [v / den for v in ex]

    best = None
    for g in range(N_GROUPS):
        a = p[g * GROUP_SIZE:(g + 1) * GROUP_SIZE]
        m1 = functools.reduce(jnp.maximum, a)
        i1 = _first_of4(a, m1)
        b = [jnp.where(i1 == float(j), -1.0, a[j]) for j in range(GROUP_SIZE)]
        m2 = functools.reduce(jnp.maximum, b)
        i2 = _first_of4(b, m2)
        cand = (m1 + m2, m1, m2, i1 + float(g * GROUP_SIZE), i2 + float(g * GROUP_SIZE))
        if best is None:
            best = cand
        else:
            take = cand[0] > best[0]
            best = tuple(jnp.where(take, c, o) for c, o in zip(cand, best))
    _, m1, m2, e0, e1 = best
    denom = m1 + m2
    w0 = m1 / denom
    w1 = m2 / denom

    ex_id = lax.broadcasted_iota(i32, (N_EXPERTS, TM), 0).astype(f32)
    sel0 = ex_id == e0
    sel1 = ex_id == e1
    ind = jnp.where(sel0 | sel1, 1.0, 0.0)
    t_r = lax.broadcasted_iota(i32, (TM, TM), 0)
    t_c = lax.broadcasted_iota(i32, (TM, TM), 1)
    earlier = jnp.where(t_r < t_c, 1.0, 0.0).astype(bf16)
    prefix = jnp.dot(ind.astype(bf16), earlier, preferred_element_type=f32)
    cnt = jnp.sum(ind, axis=-1, keepdims=True)
    cnt8 = jnp.floor((cnt + float(SUBLANES - 1)) * (1.0 / SUBLANES)) * float(SUBLANES)
    e_r = lax.broadcasted_iota(i32, (N_EXPERTS, N_EXPERTS), 0)
    e_c = lax.broadcasted_iota(i32, (N_EXPERTS, N_EXPERTS), 1)
    below = jnp.where(e_c < e_r, 1.0, 0.0).astype(bf16)
    seg_off = jnp.dot(below, jnp.broadcast_to(cnt8, (N_EXPERTS, TM)).astype(bf16),
                      preferred_element_type=f32)
    where_to = seg_off + prefix
    pos0 = jnp.sum(jnp.where(sel0, where_to, 0.0), axis=0, keepdims=True)
    pos1 = jnp.sum(jnp.where(sel1, where_to, 0.0), axis=0, keepdims=True)

    w0p = [v.astype(f32) for v in _split3(w0)]
    w1p = [v.astype(f32) for v in _split3(w1)]
    info = [pos0, pos1] + w0p + w1p
    return (_rows_to_tile(info, SUBLANES), _rows_to_tile(info, LANES).T,
            jnp.broadcast_to(cnt, (N_EXPERTS, LANES)))


def _outproj(tl, layer, x, ret_p, ret_s, cv_p, cv_s, mod_seq, mod_tok, g_norm, w_out_bf, wr_bf, br_pad):
    split_x = isinstance(x, tuple)
    tok_spec = pl.BlockSpec((TS, D_MODEL), lambda i: (i, 0))
    p_spec = lambda w: pl.BlockSpec((TS, w), lambda i: (tl.prompt_block(i), 0))
    s_spec = lambda w: pl.BlockSpec((TS, w), lambda i: (tl.sample_block(i), 0))
    if split_x:
        x_args, x_specs = list(x), [p_spec(D_MODEL), s_spec(D_MODEL)]
    else:
        x_args, x_specs = [x], [tok_spec]
    mods = []
    for col in (2, 3, 4):
        mods += list(_mod_specs(tl, layer, col))
    wr_spec = pl.BlockSpec((D_MODEL, LANES), lambda i: (0, 0))
    return pl.pallas_call(
        functools.partial(_outproj_kernel, np_steps=tl.np_steps, split_x=split_x),
        grid=(tl.n_steps,),
        in_specs=x_specs + [p_spec(D_RET), s_spec(D_RET), p_spec(D_CONV), s_spec(D_CONV)] + mods + [
            pl.BlockSpec((None, 1, D_MODEL), lambda i: (layer, 0, 0)),
            pl.BlockSpec((None, D_MODEL, D_MODEL), lambda i: (layer, 0, 0)),
            wr_spec,
            pl.BlockSpec((1, LANES), lambda i: (0, 0)),
        ],
        out_specs=[tok_spec, tok_spec,
                   pl.BlockSpec((STEP_TILES, SUBLANES, TM), lambda i: (i, 0, 0)),
                   pl.BlockSpec((TS, LANES), lambda i: (i, 0)),
                   pl.BlockSpec((STEP_TILES, N_EXPERTS, LANES), lambda i: (i, 0, 0))],
        out_shape=[jax.ShapeDtypeStruct((tl.n_tok, D_MODEL), f32),
                   jax.ShapeDtypeStruct((tl.n_tok, D_MODEL), bf16),
                   jax.ShapeDtypeStruct((tl.n_tiles, SUBLANES, TM), f32),
                   jax.ShapeDtypeStruct((tl.n_tok, LANES), f32),
                   jax.ShapeDtypeStruct((tl.n_tiles, N_EXPERTS, LANES), f32)],
        compiler_params=_cparams(1),
        name="outproj_router",
    )(*x_args, ret_p, ret_s, cv_p, cv_s, *([mod_seq, mod_tok] * 3), g_norm, w_out_bf, wr_bf, br_pad)


N_CHUNKS = SORT_ROWS // SUBLANES


class _Layout:
    def __init__(self, n_tiles):
        self.n_tiles = n_tiles
        self.tail_start = n_tiles * N_CHUNKS
        self.tail_n8 = self.tail_start + N_EXPERTS
        worst = 2 * n_tiles * TM + n_tiles * N_EXPERTS * (SUBLANES - 1) + N_EXPERTS * (BM - SUBLANES)
        self.n_blocks = -(-worst // BM)
        self.cap = self.n_blocks * BM
        self.dump = self.cap
        self.xs_rows = self.cap + -(-2 * STEP_TILES * SORT_ROWS // BM) * BM


def _moe_tables(lay, tile_counts):
    c8 = ((tile_counts.astype(i32) + SUBLANES - 1) // SUBLANES) * SUBLANES
    base8 = jnp.cumsum(c8, axis=0) - c8
    tot8 = jnp.sum(c8, axis=0)
    region = ((tot8 + BM - 1) // BM) * BM
    g_end = jnp.cumsum(region)
    g_start = g_end - region
    seg_end = jnp.cumsum(c8, axis=1)
    seg_dst = g_start[None, :] + base8
    n_used = g_end[-1] // BM
    blk = jnp.arange(lay.n_blocks, dtype=i32)
    block_e = jnp.minimum(jnp.sum((g_end[None, :] <= blk[:, None] * BM).astype(i32), axis=1), N_EXPERTS - 1)
    block_e = jnp.where(blk < n_used, block_e, block_e[n_used - 1])
    row0 = jnp.arange(N_CHUNKS, dtype=i32) * SUBLANES
    owner = jnp.sum((seg_end[:, None, :] <= row0[None, :, None]).astype(i32), axis=-1)
    onehot = (owner[:, :, None] == jnp.arange(N_EXPERTS, dtype=i32)[None, None, :]).astype(i32)
    delta = seg_dst - (seg_end - c8)
    chunk_dst = jnp.where(owner < N_EXPERTS, row0[None, :] + jnp.sum(onehot * delta[:, None, :], axis=-1), -1)
    run_end = ((tot8 + BM_PART - 1) // BM_PART) * BM_PART
    tab = jnp.concatenate([chunk_dst.ravel(), g_start + tot8, (run_end - tot8) // SUBLANES]).astype(i32)
    ids = jnp.arange(N_EXPERTS, dtype=i32)
    later = jnp.where((ids[None, :] > ids[:, None]) & (region[None, :] > 0), ids[None, :], N_EXPERTS)
    next_e = jnp.min(later, axis=1)
    next_e = jnp.where(next_e == N_EXPERTS, -1, next_e)
    owner_end = jnp.sum((block_e[:, None] == ids[None, :]).astype(i32) * (g_start + tot8)[None, :], axis=1)
    rows_used = jnp.clip(owner_end - blk * BM, 0, BM)
    return tab, jnp.concatenate([block_e, n_used[None], next_e, rows_used]).astype(i32)


def _for_chunks(n, fn):
    def body(c, carry):
        fn(c)
        return carry

    lax.fori_loop(0, n, body, 0)


def _dispatch_kernel(tab_ref, h2_ref, rows_ref, cols_ref, xs_hbm, sorted_ref, zero_ref, sem, *, lay, n_steps):
    i = pl.program_id(0)
    slot = i % 2
    step_rows = STEP_TILES * SORT_ROWS

    def tail_copy(dst):
        return pltpu.make_async_copy(zero_ref, xs_hbm.at[pl.ds(dst, SUBLANES)], sem.at[2])

    def wait_step(slot_):
        pltpu.make_async_copy(sorted_ref.at[slot_], xs_hbm.at[pl.ds(0, step_rows)], sem.at[slot_]).wait()

    @pl.when(i == 0)
    def _():
        zero_ref[...] = jnp.zeros_like(zero_ref)
        for e in range(N_EXPERTS):
            start = tab_ref[lay.tail_start + e]
            _for_chunks(tab_ref[lay.tail_n8 + e],
                        lambda c: tail_copy(pl.multiple_of(start + c * SUBLANES, SUBLANES)).start())

    @pl.when(i >= 2)
    def _():
        wait_step(slot)

    r_id = lax.broadcasted_iota(i32, (SORT_ROWS, TM), 0).astype(f32)
    lane = lax.broadcasted_iota(i32, (TM, LANES), 1)
    for t in range(STEP_TILES):
        pos0 = rows_ref[t, 0:1, :]
        pos1 = rows_ref[t, 1:2, :]
        p0 = r_id == pos0
        p1 = r_id == pos1
        perm = jnp.where(p0 | p1, 1.0, 0.0).astype(bf16)
        cols = cols_ref[t * TM:(t + 1) * TM, :]
        wpart0 = jnp.where((lane >= ROUTE_W0[0]) & (lane < ROUTE_W0[1]), cols, 0.0).astype(bf16)
        wpart1 = jnp.where((lane >= ROUTE_W1[0]) & (lane < ROUTE_W1[1]), cols, 0.0).astype(bf16)
        sw = (jnp.dot(jnp.where(p0, 1.0, 0.0).astype(bf16), wpart0, preferred_element_type=f32)
              + jnp.dot(jnp.where(p1, 1.0, 0.0).astype(bf16), wpart1, preferred_element_type=f32))
        base = t * SORT_ROWS
        sorted_ref[slot, base:base + SORT_ROWS, 0:XS_HALF] = _pack_bf16_pair(jnp.dot(
            perm, h2_ref[t * TM:(t + 1) * TM, :], preferred_element_type=f32))
        sorted_ref[slot, base:base + SORT_ROWS, XS_HALF:XS_W] = lax.bitcast_convert_type(jnp.broadcast_to(
            jnp.sum(sw, axis=-1, keepdims=True), (SORT_ROWS, LANES)), u32)

    for t in range(STEP_TILES):
        for c in range(N_CHUNKS):
            row = t * SORT_ROWS + c * SUBLANES
            dst = tab_ref[(i * STEP_TILES + t) * N_CHUNKS + c]
            dst = jnp.where(dst < 0, lay.dump + slot * step_rows + row, dst)
            pltpu.make_async_copy(sorted_ref.at[slot, pl.ds(row, SUBLANES)],
                                  xs_hbm.at[pl.ds(pl.multiple_of(dst, SUBLANES), SUBLANES)], sem.at[slot]).start()

    @pl.when(i == n_steps - 1)
    def _():
        if n_steps >= 2:
            wait_step(1 - slot)
        wait_step(slot)
        for e in range(N_EXPERTS):
            _for_chunks(tab_ref[lay.tail_n8 + e], lambda c: tail_copy(0).wait())


def _dispatch(tl, lay, tab, h2, rows, cols):
    grid_spec = pltpu.PrefetchScalarGridSpec(
        num_scalar_prefetch=1,
        grid=(tl.n_steps,),
        in_specs=[pl.BlockSpec((TS, D_MODEL), lambda i, t: (i, 0)),
                  pl.BlockSpec((STEP_TILES, SUBLANES, TM), lambda i, t: (i, 0, 0)),
                  pl.BlockSpec((TS, LANES), lambda i, t: (i, 0))],
        out_specs=pl.BlockSpec(memory_space=pl.ANY),
        scratch_shapes=[pltpu.VMEM((2, STEP_TILES * SORT_ROWS, XS_W), u32), pltpu.VMEM((SUBLANES, XS_W), u32),
                        pltpu.SemaphoreType.DMA((3,))],
    )
    return pl.pallas_call(
        functools.partial(_dispatch_kernel, lay=lay, n_steps=tl.n_steps),
        grid_spec=grid_spec,
        out_shape=jax.ShapeDtypeStruct((lay.xs_rows, XS_W), u32),
        compiler_params=_cparams(1),
        name="moe_dispatch",
    )(tab, h2, rows, cols)


def _expert_kernel(be_ref, xs_ref, wg_hbm, wu_hbm, wd_hbm, ys_ref, stage, w_bf, sem, *, n_blocks, layer):
    j = pl.program_id(0)

    def fetch(e):
        return [pltpu.make_async_copy(w.at[layer, e], stage.at[k], sem.at[k])
                for k, w in enumerate((wg_hbm, wu_hbm, wd_hbm))]

    @pl.when(j < be_ref[n_blocks])
    def _():
        e = be_ref[j]

        @pl.when(j == 0)
        def _():
            for copy in fetch(e):
                copy.start()

        @pl.when((j == 0) | (e != be_ref[jnp.maximum(j - 1, 0)]))
        def _():
            for copy in fetch(e):
                copy.wait()
            for k in range(3):
                w_bf[k] = stage[k].astype(bf16)
            nxt = be_ref[n_blocks + 1 + e]

            @pl.when(nxt >= 0)
            def _():
                for copy in fetch(nxt):
                    copy.start()

        def ffn(rows):
            x_lo, x_hi = _unpack_bf16_pair(xs_ref[rows, 0:XS_HALF])

            def first_layer(k):
                return (jnp.dot(x_lo, w_bf[k, 0:XS_HALF, :], preferred_element_type=f32)
                        + jnp.dot(x_hi, w_bf[k, XS_HALF:D_MODEL, :], preferred_element_type=f32))

            mid = (jax.nn.silu(first_layer(0)) * first_layer(1)).astype(bf16)
            slot_w = lax.bitcast_convert_type(xs_ref[rows, XS_HALF:XS_HALF + 1], f32)
            ys_ref[rows, :] = jnp.dot(mid, w_bf[2], preferred_element_type=f32) * slot_w

        rows_used = be_ref[n_blocks + 1 + N_EXPERTS + j]
        for parts in range(1, BM // BM_PART + 1):
            top = parts * BM_PART

            @pl.when((rows_used > top - BM_PART) & (rows_used <= top))
            def _(top=top):
                ffn(slice(0, top))
                if top < BM:
                    ys_ref[top:BM, :] = jnp.zeros((BM - top, D_MODEL), f32)


def _experts(layer, lay, block_e, xs, w_gate, w_up, w_down):
    n_blocks = lay.n_blocks
    d_ff = w_gate.shape[-1]
    assert d_ff == D_MODEL
    used = lambda j, be: jnp.minimum(j, be[n_blocks] - 1)
    any_spec = pl.BlockSpec(memory_space=pl.ANY)
    grid_spec = pltpu.PrefetchScalarGridSpec(
        num_scalar_prefetch=1,
        grid=(n_blocks,),
        in_specs=[pl.BlockSpec((BM, XS_W), lambda j, be: (used(j, be), 0)), any_spec, any_spec, any_spec],
        out_specs=pl.BlockSpec((BM, D_MODEL), lambda j, be: (used(j, be), 0)),
        scratch_shapes=[pltpu.VMEM((3, D_MODEL, d_ff), f32), pltpu.VMEM((3, D_MODEL, d_ff), bf16),
                        pltpu.SemaphoreType.DMA((3,))],
    )
    return pl.pallas_call(
        functools.partial(_expert_kernel, n_blocks=n_blocks, layer=layer),
        grid_spec=grid_spec,
        out_shape=jax.ShapeDtypeStruct((lay.cap, D_MODEL), f32),
        compiler_params=_cparams(1),
        name="moe_experts",
    )(block_e, xs, w_gate, w_up, w_down)


def _combine_kernel(tab_ref, ys_hbm, cols_ref, x_ref, gts_ref, gtt_ref, *rest, n_steps, np_steps, final):
    if final:
        gf_ref, yp_ref, ysm_ref, staged, sem = rest
    else:
        xo_ref, staged, sem = rest
    i = pl.program_id(0)
    slot = i % 2
    is_s = i >= np_steps

    def start_step(step, slot_):
        for c in range(STEP_TILES * N_CHUNKS):
            src = jnp.maximum(tab_ref[step * (STEP_TILES * N_CHUNKS) + c], 0)
            pltpu.make_async_copy(ys_hbm.at[pl.ds(pl.multiple_of(src, SUBLANES), SUBLANES)],
                                  staged.at[slot_, pl.ds(c * SUBLANES, SUBLANES)], sem.at[slot_]).start()

    @pl.when(i == 0)
    def _():
        start_step(0, 0)

    @pl.when(i + 1 < n_steps)
    def _():
        start_step(i + 1, 1 - slot)

    pltpu.make_async_copy(ys_hbm.at[pl.ds(0, STEP_TILES * SORT_ROWS)], staged.at[slot], sem.at[slot]).wait()

    def step(sample):
        lane = lax.broadcasted_iota(i32, (TM, SORT_ROWS), 1).astype(f32)
        for t in range(STEP_TILES):
            rs = slice(t * TM, (t + 1) * TM)
            unperm = jnp.where((lane == cols_ref[rs, 0:1]) | (lane == cols_ref[rs, 1:2]), 1.0, 0.0).astype(bf16)
            parts = _split3(staged[slot, t * SORT_ROWS:(t + 1) * SORT_ROWS, :])
            ff = sum(jnp.dot(unperm, part, preferred_element_type=f32) for part in parts)
            xn = x_ref[rs, :] + (gtt_ref[rs, :] if sample else gts_ref[...]) * ff
            if final:
                (ysm_ref if sample else yp_ref)[rs, :] = _rms(xn, gf_ref[...])
            else:
                xo_ref[rs, :] = xn

    pl.when(is_s)(functools.partial(step, True))
    pl.when(jnp.logical_not(is_s))(functools.partial(step, False))


def _combine(tl, layer, tab, ys, cols, x, mod_seq, mod_tok, g_final):
    final = g_final is not None
    tok_spec = pl.BlockSpec((TS, D_MODEL), lambda i, t: (i, 0))
    gt_seq, gt_tok = _mod_specs(tl, layer, 5)
    in_specs = [pl.BlockSpec(memory_space=pl.ANY), pl.BlockSpec((TS, LANES), lambda i, t: (i, 0)),
                tok_spec, gt_seq, gt_tok]
    args = [tab, ys, cols, x, mod_seq, mod_tok]
    if final:
        in_specs.append(pl.BlockSpec((1, D_MODEL), lambda i, t: (0, 0)))
        args.append(g_final)
        out_specs = [pl.BlockSpec((TS, D_MODEL), lambda i, t: (tl.prompt_block(i), 0)),
                     pl.BlockSpec((TS, D_MODEL), lambda i, t: (tl.sample_block(i), 0))]
        out_shape = [jax.ShapeDtypeStruct((tl.n_prompt, D_MODEL), f32),
                     jax.ShapeDtypeStruct((tl.n_sample, D_MODEL), f32)]
    else:
        out_specs = tok_spec
        out_shape = jax.ShapeDtypeStruct((tl.n_tok, D_MODEL), f32)
    grid_spec = pltpu.PrefetchScalarGridSpec(
        num_scalar_prefetch=1,
        grid=(tl.n_steps,),
        in_specs=in_specs,
        out_specs=out_specs,
        scratch_shapes=[pltpu.VMEM((2, STEP_TILES * SORT_ROWS, D_MODEL), f32), pltpu.SemaphoreType.DMA((2,))],
    )
    return pl.pallas_call(
        functools.partial(_combine_kernel, n_steps=tl.n_steps, np_steps=tl.np_steps, final=final),
        grid_spec=grid_spec,
        out_shape=out_shape,
        compiler_params=_cparams(1),
        name="moe_combine",
    )(*args)


def _rope_tables(tl):
    half = HEAD_D // 2
    inv = ROPE_BASE ** (-jnp.arange(half, dtype=f32) / half)
    pos_p = jnp.arange(tl.tp, dtype=i32)
    pos_s = PAST_LEN + jnp.arange(tl.ts, dtype=i32)
    pos = jnp.concatenate([pos_p, jnp.tile(pos_s, tl.bs)])
    ang = pos.astype(f32)[:, None] * inv[None, :]
    cos, sin = jnp.cos(ang), jnp.sin(ang)
    return jnp.concatenate([cos, cos], axis=-1), jnp.concatenate([-sin, sin], axis=-1)


def kernel(x_prompt, x_sample, state_ret, state_conv, c_prompt, c_sample, w_mod, b_mod, g_mix_norm, w_in,
           w_conv, b_conv, g_conv_ln, b_conv_ln, g_ret_gn, w_out, g_ffn_norm, w_router, b_router,
           w_exp_gate, w_exp_up, w_exp_down, g_final):
    bp, tp, _ = x_prompt.shape
    bs, ts, _ = x_sample.shape
    depth = w_mod.shape[0]
    tl = _Tiles(bp, tp, bs, ts)
    lay = _Layout(tl.n_tiles)

    c_all = jnp.concatenate([c_prompt, jnp.repeat(c_sample, ts, axis=0)], axis=0)
    mod_seq, mod_tok = _modulation(c_all, bp, w_mod, b_mod)
    mod_seq = mod_seq.reshape(depth, bp, 1, N_MOD * D_MODEL)

    cos_tab, sin_tab = _rope_tables(tl)
    w_in_bf = w_in.astype(bf16)
    w_out_bf = w_out.astype(bf16)
    wr_pad = jnp.pad(w_router.astype(f32), ((0, 0), (0, LANES - N_EXPERTS)))
    wr_bf = wr_pad.astype(bf16)
    br_pad = jnp.pad(b_router.astype(f32), (0, LANES - N_EXPERTS)).reshape(1, LANES)
    vec3 = lambda t: t.reshape(depth, 1, t.shape[-1])
    g_mix3, g_ffn3, gn3 = vec3(g_mix_norm), vec3(g_ffn_norm), vec3(g_ret_gn)
    b_conv3, g_ln3, b_ln3 = vec3(b_conv), vec3(g_conv_ln), vec3(b_conv_ln)

    x = (x_prompt.reshape(tl.n_prompt, D_MODEL), x_sample.reshape(tl.n_sample, D_MODEL))
    ret_p, conv_p = [], []
    ret_s_all = conv_s_all = None
    for layer in range(depth):
        q, k, v, gate, a = _inproj(tl, layer, x, mod_seq, mod_tok, g_mix3, w_in_bf, cos_tab, sin_tab)
        ro_p, s_p = _retention_prompt(tl, layer, q, k, v, gate, gn3)
        ro_s, ret_s_all = _retention_sample(tl, layer, q, k, v, gate, state_ret, gn3, ret_s_all)
        co_p, buf_p = _conv_prompt(tl, layer, a, w_conv, b_conv3, g_ln3, b_ln3)
        co_s, conv_s_all = _conv_sample(tl, layer, a, state_conv, w_conv, b_conv3, g_ln3, b_ln3, conv_s_all)
        x_mid, h2, rows, cols, tile_counts = _outproj(
            tl, layer, x, ro_p, ro_s, co_p, co_s, mod_seq, mod_tok, g_ffn3, w_out_bf, wr_bf, br_pad)
        tab, block_e = _moe_tables(lay, tile_counts[:, :, 0])
        xs = _dispatch(tl, lay, tab, h2, rows, cols)
        ys = _experts(layer, lay, block_e, xs, w_exp_gate, w_exp_up, w_exp_down)
        last = layer == depth - 1
        x = _combine(tl, layer, tab, ys, cols, x_mid, mod_seq, mod_tok,
                     g_final.reshape(1, D_MODEL) if last else None)
        ret_p.append(s_p)
        conv_p.append(buf_p)
    y_p, y_s = x
    return (y_p.reshape(bp, tp, D_MODEL), y_s.reshape(bs, ts, D_MODEL),
            jnp.stack(ret_p), jnp.stack(conv_p), ret_s_all, conv_s_all)
```

```python
import functools

import jax
import jax.numpy as jnp
from jax import lax
from jax.experimental import pallas as pl
from jax.experimental.pallas import tpu as pltpu

f32 = jnp.float32
bf16 = jnp.bfloat16
i32 = jnp.int32

D_MODEL = 1024
D_RET = 512
D_CONV = 512
N_HEADS = 4
HEAD_D = 128
RET_CHUNK = 128
RET_CHUNKS_PER_STEP = 16
ROPE_BASE = 10000.0
CONV_WIDTH = 31
CONV_HALO = CONV_WIDTH - 1
N_EXPERTS = 16
N_GROUPS = 4
GROUP_SIZE = N_EXPERTS // N_GROUPS
N_MOD = 6
EPS = 1e-6
PAST_LEN = 16384
D_IN = 4 * D_RET + 2 * D_CONV

LANES = 128
SUBLANES = 8
TM = 256
STEP_TILES = 2
TS = TM * STEP_TILES
BM = 512
CONV_ROWS = 256
NORM_ROWS = 64
SAMPLE_GROUP = 16
HALO_PAD = 32
V7X_VMEM_BYTES = 64 * 1024 * 1024
VMEM_LIMIT = V7X_VMEM_BYTES - 8 * 1024 * 1024


def _cparams(n_axes, vmem=VMEM_LIMIT):
    return pltpu.CompilerParams(dimension_semantics=("arbitrary",) * n_axes, vmem_limit_bytes=vmem)


def _mod_kernel(c_ref, w_ref, b_ref, seq_ref, tok_ref):
    cond = jax.nn.silu(c_ref[...]).astype(bf16)
    mod = jnp.dot(cond, w_ref[...].astype(bf16), preferred_element_type=f32) + b_ref[...]
    n_seq = seq_ref.shape[0]
    seq_ref[...] = mod[0:n_seq, :]
    tok_ref[...] = mod[n_seq:, :]


def _modulation(c_all, n_seq, w_mod, b_mod):
    depth = w_mod.shape[0]
    m = c_all.shape[0]
    n_tok = m - n_seq
    assert n_seq % SUBLANES == 0
    return pl.pallas_call(
        _mod_kernel,
        grid=(depth, N_MOD),
        in_specs=[
            pl.BlockSpec((m, D_MODEL), lambda l, j: (0, 0)),
            pl.BlockSpec((None, D_MODEL, D_MODEL), lambda l, j: (l, 0, j)),
            pl.BlockSpec((None, 1, D_MODEL), lambda l, j: (l, 0, j)),
        ],
        out_specs=[pl.BlockSpec((None, n_seq, D_MODEL), lambda l, j: (l, 0, j)),
                   pl.BlockSpec((None, n_tok, D_MODEL), lambda l, j: (l, 0, j))],
        out_shape=[jax.ShapeDtypeStruct((depth, n_seq, N_MOD * D_MODEL), f32),
                   jax.ShapeDtypeStruct((depth, n_tok, N_MOD * D_MODEL), f32)],
        compiler_params=_cparams(2),
        name="modulation",
    )(c_all, w_mod, b_mod.reshape(depth, 1, N_MOD * D_MODEL))


class _Tiles:
    def __init__(self, bp, tp, bs, ts):
        self.bp, self.tp, self.bs, self.ts = bp, tp, bs, ts
        self.n_prompt = bp * tp
        self.n_sample = bs * ts
        self.n_tok = self.n_prompt + self.n_sample
        assert tp % TS == 0 and self.n_sample % TS == 0
        self.tiles_per_seq = tp // TM
        self.np_tiles = self.n_prompt // TM
        self.n_tiles = self.n_tok // TM
        self.steps_per_seq = tp // TS
        self.np_steps = self.n_prompt // TS
        self.n_steps = self.n_tok // TS

    def prompt_block(self, i):
        return jnp.minimum(i, self.np_steps - 1)

    def sample_block(self, i):
        return jnp.maximum(i - self.np_steps, 0)

    def seq_index(self, i):
        return jnp.minimum(i // self.steps_per_seq, self.bp - 1)


def _mod_specs(tl, layer, col):
    seq = pl.BlockSpec((None, None, 1, D_MODEL), lambda i, *_: (layer, tl.seq_index(i), 0, col))
    tok = pl.BlockSpec((None, TS, D_MODEL), lambda i, *_: (layer, tl.sample_block(i), col))
    return seq, tok


def _rms(x, g):
    return x * lax.rsqrt(jnp.mean(x * x, axis=-1, keepdims=True) + EPS) * g


def _head_norm_gate(o, gn, gate):
    mu = jnp.mean(o, axis=-1, keepdims=True)
    var = jnp.mean(jnp.square(o - mu), axis=-1, keepdims=True)
    return jax.nn.silu(gate) * ((o - mu) * lax.rsqrt(var + EPS) * gn)


def _dot_nt(a, b):
    return lax.dot_general(a, b, (((1,), (1,)), ((), ())), preferred_element_type=f32)


def _dot_tn(a, b):
    return lax.dot_general(a, b, (((0,), (0,)), ((), ())), preferred_element_type=f32)


def _ret_prompt_kernel(q_ref, k_ref, v_ref, gate_ref, dec_ref, qd_ref, kd_ref, cd_ref, gn_ref,
                       o_ref, s_out_ref, s_ref):
    c = pl.program_id(1)

    @pl.when(c == 0)
    def _():
        s_ref[...] = jnp.zeros_like(s_ref)

    for ci in range(RET_CHUNKS_PER_STEP):
        rows = slice(ci * RET_CHUNK, (ci + 1) * RET_CHUNK)
        for hd in range(N_HEADS):
            sl = slice(hd * HEAD_D, (hd + 1) * HEAD_D)
            kh = k_ref[rows, sl]
            qb = q_ref[rows, sl]
            kb = kh.astype(bf16)
            vb = v_ref[rows, sl]
            s_old = s_ref[hd]
            scores = _dot_nt(qb, kb) * dec_ref[hd]
            inner = jnp.dot(scores.astype(bf16), vb, preferred_element_type=f32)
            cross = jnp.dot(qb, s_old.astype(bf16), preferred_element_type=f32) * qd_ref[hd]
            s_ref[hd] = s_old * cd_ref[hd] + _dot_tn((kh * kd_ref[hd]).astype(bf16), vb)
            o_ref[rows, sl] = _head_norm_gate(inner + cross, gn_ref[:, sl], gate_ref[rows, sl]).astype(bf16)

    @pl.when(c == pl.num_programs(1) - 1)
    def _():
        s_out_ref[...] = s_ref[...]


def _decay_tables(chunk, true_len):
    lg = jnp.log(1.0 - 2.0 ** (-5.0 - jnp.arange(N_HEADS, dtype=f32)))
    idx = jnp.arange(chunk, dtype=f32)
    rel = idx[:, None] - idx[None, :]
    decay = jnp.where(rel[None] >= 0, jnp.exp(jnp.maximum(rel, 0.0)[None] * lg[:, None, None]), 0.0)
    q_decay = jnp.exp((idx[None, :] + 1.0) * lg[:, None])
    k_decay = jnp.exp((true_len - 1.0 - idx[None, :]) * lg[:, None])
    c_decay = jnp.exp(true_len * lg)
    return decay, q_decay, k_decay, c_decay


def _retention_prompt(tl, layer, q, k, v, gate, g_ret_gn):
    step_rows = RET_CHUNK * RET_CHUNKS_PER_STEP
    assert tl.tp % step_rows == 0
    n_chunks = tl.tp // step_rows
    decay, q_decay, k_decay, c_decay = _decay_tables(RET_CHUNK, RET_CHUNK)
    bcast = lambda t: jnp.broadcast_to(t[:, :, None], (N_HEADS, RET_CHUNK, HEAD_D))
    cd = jnp.broadcast_to(c_decay[:, None, None], (N_HEADS, 1, HEAD_D))
    tok_spec = pl.BlockSpec((step_rows, D_RET), lambda b, c: (b * n_chunks + c, 0))
    tab_spec = pl.BlockSpec((N_HEADS, RET_CHUNK, HEAD_D), lambda b, c: (0, 0, 0))
    return pl.pallas_call(
        _ret_prompt_kernel,
        grid=(tl.bp, n_chunks),
        in_specs=[tok_spec] * 4 + [tab_spec] * 3 + [
            pl.BlockSpec((N_HEADS, 1, HEAD_D), lambda b, c: (0, 0, 0)),
            pl.BlockSpec((None, 1, D_RET), lambda b, c: (layer, 0, 0)),
        ],
        out_specs=[tok_spec, pl.BlockSpec((None, N_HEADS, HEAD_D, HEAD_D), lambda b, c: (b, 0, 0, 0))],
        out_shape=[jax.ShapeDtypeStruct((tl.n_prompt, D_RET), bf16),
                   jax.ShapeDtypeStruct((tl.bp, N_HEADS, HEAD_D, HEAD_D), f32)],
        scratch_shapes=[pltpu.VMEM((N_HEADS, HEAD_D, HEAD_D), f32)],
        compiler_params=_cparams(2),
        name="retention_prompt",
    )(q, k, v, gate, decay, bcast(q_decay), bcast(k_decay), cd, g_ret_gn)


def _ret_sample_kernel(q_ref, k_ref, v_ref, gate_ref, s_in_ref, dec_ref, qd_ref, kd_ref, cd_ref, gn_ref,
                       *rest, ts):
    o_ref, s_all_ref = rest[-2:]
    s_out_ref = s_all_ref.at[0]
    for other in range(1, s_all_ref.shape[0]):
        s_all_ref[other] = jnp.zeros(s_all_ref.shape[1:], f32)
    seqs_per_tile = SUBLANES // ts
    row = lax.broadcasted_iota(i32, (SUBLANES, HEAD_D), 0)
    q_all = q_ref[...].astype(f32)
    v_all = v_ref[...].astype(f32)
    outs = []
    for t in range(SAMPLE_GROUP // seqs_per_tile):
        rows = slice(t * SUBLANES, (t + 1) * SUBLANES)
        heads = []
        for hd in range(N_HEADS):
            sl = slice(hd * HEAD_D, (hd + 1) * HEAD_D)
            qh = q_all[rows, sl]
            kh = k_ref[rows, sl] * kd_ref[hd]
            vb = v_all[rows, sl].astype(bf16)
            qb = qh.astype(bf16)
            scores = _dot_nt(qb, k_ref[rows, sl].astype(bf16)) * dec_ref[hd]
            o = jnp.dot(scores.astype(bf16), vb, preferred_element_type=f32)
            for s in range(seqs_per_tile):
                b = t * seqs_per_tile + s
                mine = (row >= s * ts) & (row < (s + 1) * ts)
                s_old = s_in_ref[b, hd]
                q_s = jnp.where(mine, qh, 0.0).astype(bf16)
                k_s = jnp.where(mine, kh, 0.0).astype(bf16)
                o = o + jnp.dot(q_s, s_old.astype(bf16), preferred_element_type=f32) * qd_ref[hd]
                s_out_ref[b, hd] = s_old * cd_ref[hd] + _dot_tn(k_s, vb)
            heads.append(_head_norm_gate(o, gn_ref[:, sl], gate_ref[rows, sl]))
        outs.append(jnp.concatenate(heads, axis=-1))
    o_ref[...] = jnp.concatenate(outs, axis=0).astype(bf16)


def _retention_sample(tl, layer, q, k, v, gate, state_ret, g_ret_gn, prev_states):
    ts = tl.ts
    depth = state_ret.shape[0]
    assert SUBLANES % ts == 0 and tl.bs % SAMPLE_GROUP == 0
    seqs_per_tile = SUBLANES // ts
    decay, q_decay, k_decay, c_decay = _decay_tables(ts, ts)
    eye = jnp.eye(seqs_per_tile, dtype=f32)
    dec_tile = jnp.einsum("ab,hij->haibj", eye, decay).reshape(N_HEADS, SUBLANES, SUBLANES)
    tile_rows = lambda t: jnp.broadcast_to(jnp.tile(t, (1, seqs_per_tile))[:, :, None],
                                           (N_HEADS, SUBLANES, HEAD_D))
    cd = jnp.broadcast_to(c_decay[:, None, None], (N_HEADS, 1, HEAD_D))
    rows = SAMPLE_GROUP * ts
    first = tl.n_prompt // rows
    tok_spec = pl.BlockSpec((rows, D_RET), lambda i: (first + i, 0))
    const3 = lambda shape: pl.BlockSpec(shape, lambda i: (0, 0, 0))
    st_block = (SAMPLE_GROUP, N_HEADS, HEAD_D, HEAD_D)
    in_specs = [tok_spec] * 4 + [
        pl.BlockSpec((None,) + st_block, lambda i: (layer, i, 0, 0, 0)),
        const3((N_HEADS, SUBLANES, SUBLANES)),
        const3((N_HEADS, SUBLANES, HEAD_D)),
        const3((N_HEADS, SUBLANES, HEAD_D)),
        const3((N_HEADS, 1, HEAD_D)),
        pl.BlockSpec((None, 1, D_RET), lambda i: (layer, 0, 0)),
    ]
    args = [q, k, v, gate, state_ret, dec_tile, tile_rows(q_decay), tile_rows(k_decay), cd, g_ret_gn]
    if prev_states is None:
        state_spec = pl.BlockSpec((depth,) + st_block, lambda i: (0, i, 0, 0, 0))
        aliases = {}
    else:
        state_spec = pl.BlockSpec((1,) + st_block, lambda i: (layer, i, 0, 0, 0))
        in_specs.append(pl.BlockSpec(memory_space=pl.ANY))
        args.append(prev_states)
        aliases = {len(args) - 1: 1}
    return pl.pallas_call(
        functools.partial(_ret_sample_kernel, ts=ts),
        grid=(tl.bs // SAMPLE_GROUP,),
        in_specs=in_specs,
        out_specs=[pl.BlockSpec((rows, D_RET), lambda i: (i, 0)), state_spec],
        out_shape=[jax.ShapeDtypeStruct((tl.n_sample, D_RET), bf16),
                   jax.ShapeDtypeStruct((depth, tl.bs, N_HEADS, HEAD_D, HEAD_D), f32)],
        input_output_aliases=aliases,
        compiler_params=_cparams(1),
        name="retention_sample",
    )(*args)


def _ln_silu(cv, g, b):
    mu = jnp.mean(cv, axis=-1, keepdims=True)
    var = jnp.mean(jnp.square(cv - mu), axis=-1, keepdims=True)
    return jax.nn.silu((cv - mu) * lax.rsqrt(var + EPS) * g + b)


def _conv_taps(window, w_ref, b_ref, n_rows):
    cols = []
    for col in range(D_CONV // LANES):
        lanes = slice(col * LANES, (col + 1) * LANES)
        acc = jnp.broadcast_to(b_ref[:, lanes], (n_rows, LANES))
        for tap in range(CONV_WIDTH):
            acc = acc + window(col, tap) * w_ref[tap:tap + 1, lanes]
        cols.append(acc)
    return jnp.concatenate(cols, axis=-1)


def _conv_sample_kernel(a_ref, st_ref, w_ref, b_ref, g_ref, bl_ref, *rest, ts):
    o_ref, buf_all_ref, full_ref, cv_ref = rest[-4:]
    buf_ref = buf_all_ref.at[0]
    for other in range(1, buf_all_ref.shape[0]):
        buf_all_ref[other] = jnp.zeros(buf_all_ref.shape[1:], f32)
    for s in range(SAMPLE_GROUP):
        for col in range(D_CONV // LANES):
            lanes = slice(col * LANES, (col + 1) * LANES)
            full_ref[col, 0:CONV_HALO, :] = st_ref[:, s, lanes]
            full_ref[col, CONV_HALO:CONV_HALO + ts, :] = a_ref[s * ts:(s + 1) * ts, lanes]
            buf_ref[s, :, lanes] = full_ref[col, ts:ts + CONV_HALO, :]
        cv_ref[s * ts:(s + 1) * ts, :] = _conv_taps(lambda col, tap: full_ref[col, tap:tap + ts, :],
                                                    w_ref, b_ref, ts)
    o_ref[...] = _ln_silu(cv_ref[...], g_ref[...], bl_ref[...]).astype(bf16)


def _conv_sample(tl, layer, a, state_conv, w_conv, b_conv, g_ln, b_ln, prev_bufs):
    ts = tl.ts
    depth = state_conv.shape[0]
    rows = SAMPLE_GROUP * ts
    first_block = tl.n_prompt // rows
    vec = pl.BlockSpec((None, 1, D_CONV), lambda i: (layer, 0, 0))
    in_specs = [pl.BlockSpec((rows, D_CONV), lambda i: (first_block + i, 0)),
                pl.BlockSpec((None, CONV_HALO, SAMPLE_GROUP, D_CONV), lambda i: (layer, 0, i, 0)),
                pl.BlockSpec((None, CONV_WIDTH, D_CONV), lambda i: (layer, 0, 0)),
                vec, vec, vec]
    args = [a, state_conv.transpose(0, 2, 1, 3), w_conv, b_conv, g_ln, b_ln]
    buf_block = (SAMPLE_GROUP, CONV_HALO, D_CONV)
    if prev_bufs is None:
        buf_spec = pl.BlockSpec((depth,) + buf_block, lambda i: (0, i, 0, 0))
        aliases = {}
    else:
        buf_spec = pl.BlockSpec((1,) + buf_block, lambda i: (layer, i, 0, 0))
        in_specs.append(pl.BlockSpec(memory_space=pl.ANY))
        args.append(prev_bufs)
        aliases = {len(args) - 1: 1}
    return pl.pallas_call(
        functools.partial(_conv_sample_kernel, ts=ts),
        grid=(tl.bs // SAMPLE_GROUP,),
        in_specs=in_specs,
        out_specs=[pl.BlockSpec((rows, D_CONV), lambda i: (i, 0)), buf_spec],
        out_shape=[jax.ShapeDtypeStruct((tl.n_sample, D_CONV), bf16),
                   jax.ShapeDtypeStruct((depth, tl.bs, CONV_HALO, D_CONV), f32)],
        input_output_aliases=aliases,
        scratch_shapes=[pltpu.VMEM((D_CONV // LANES, CONV_HALO + ts + SUBLANES, LANES), f32),
                        pltpu.VMEM((rows, D_CONV), f32)],
        compiler_params=_cparams(1),
        name="conv_sample",
    )(*args)


N_CONV_COLS = D_CONV // LANES
CONV_TILE = TS
N_CONV_CHUNKS = CONV_TILE // CONV_ROWS


def _inproj_kernel(*refs, np_steps, split_x):
    if split_x:
        xp_ref, xs_ref = refs[:2]
        refs = refs[2:]
    else:
        x_ref = refs[0]
        refs = refs[1:]
    (shs_ref, sht_ref, scs_ref, sct_ref, g_ref, w_ref, cos_ref, sin_ref,
     q_ref, k_ref, v_ref, gate_ref, a_ref) = refs
    is_s = pl.program_id(0) >= np_steps

    def step(sample):
        for t in range(STEP_TILES):
            rs = slice(t * TM, (t + 1) * TM)
            if split_x:
                x = xs_ref[rs, :] if sample else xp_ref[rs, :]
            else:
                x = x_ref[rs, :]
            sh, sc = (sht_ref[rs, :], sct_ref[rs, :]) if sample else (shs_ref[...], scs_ref[...])
            hb = (_rms(x, g_ref[...]) * (1.0 + sc) + sh).astype(bf16)

            def group(g):
                return jnp.dot(hb, w_ref[:, g * D_RET:(g + 1) * D_RET], preferred_element_type=f32)

            cos = cos_ref[rs, :]
            sin = sin_ref[rs, :]

            def rope(th):
                return th * cos + pltpu.roll(th, HEAD_D // 2, 1) * sin

            qg, kg = group(0), group(1)
            for hd in range(N_HEADS):
                sl = slice(hd * HEAD_D, (hd + 1) * HEAD_D)
                q_ref[rs, sl] = rope(qg[:, sl]).astype(bf16)
                k_ref[rs, sl] = rope(kg[:, sl]) * (HEAD_D ** -0.5)
            v_ref[rs, :] = group(2).astype(bf16)
            gate_ref[rs, :] = group(3)
            a_ref[rs, :] = group(4) * jax.nn.sigmoid(group(5))

    pl.when(is_s)(functools.partial(step, True))
    pl.when(jnp.logical_not(is_s))(functools.partial(step, False))


def _conv_prompt_kernel(a_ref, w_ref, b_ref, g_ref, bl_ref, o_ref, buf_ref, full_ref, cv_ref):
    j = pl.program_id(1)

    @pl.when(j == 0)
    def _():
        full_ref[:, 0:HALO_PAD, :] = jnp.zeros((N_CONV_COLS, HALO_PAD, LANES), f32)

    @pl.when(j > 0)
    def _():
        full_ref[:, 0:HALO_PAD, :] = full_ref[:, CONV_TILE:CONV_TILE + HALO_PAD, :]

    for col in range(N_CONV_COLS):
        full_ref[col, HALO_PAD:HALO_PAD + CONV_TILE, :] = a_ref[:, col * LANES:(col + 1) * LANES]
    shift = HALO_PAD - CONV_HALO

    def taps(idx, carry):
        col = idx // N_CONV_CHUNKS
        r0 = pl.multiple_of((idx % N_CONV_CHUNKS) * CONV_ROWS, CONV_ROWS)
        acc = jnp.broadcast_to(b_ref[col], (CONV_ROWS, LANES))
        for tap in range(CONV_WIDTH):
            acc = acc + full_ref[col, pl.ds(r0 + (tap + shift), CONV_ROWS), :] * w_ref[col, tap:tap + 1, :]
        cv_ref[col, pl.ds(r0, CONV_ROWS), :] = acc
        return carry

    lax.fori_loop(0, N_CONV_COLS * N_CONV_CHUNKS, taps, 0)

    for r0 in range(0, CONV_TILE, NORM_ROWS):
        cv = jnp.concatenate([cv_ref[col, r0:r0 + NORM_ROWS, :] for col in range(N_CONV_COLS)], axis=-1)
        o_ref[r0:r0 + NORM_ROWS, :] = _ln_silu(cv, g_ref[...], bl_ref[...]).astype(bf16)

    @pl.when(j == pl.num_programs(1) - 1)
    def _():
        buf_ref[...] = a_ref[CONV_TILE - CONV_HALO:CONV_TILE, :]


def _conv_prompt(tl, layer, a, w_conv, b_conv, g_ln, b_ln):
    assert tl.tp % CONV_TILE == 0
    tps = tl.tp // CONV_TILE
    depth = w_conv.shape[0]
    w_cols = w_conv.reshape(depth, CONV_WIDTH, N_CONV_COLS, LANES).transpose(0, 2, 1, 3)
    b_cols = b_conv.reshape(depth, N_CONV_COLS, 1, LANES)
    vec = pl.BlockSpec((None, 1, D_CONV), lambda b, j: (layer, 0, 0))
    return pl.pallas_call(
        _conv_prompt_kernel,
        grid=(tl.bp, tps),
        in_specs=[pl.BlockSpec((CONV_TILE, D_CONV), lambda b, j: (b * tps + j, 0)),
                  pl.BlockSpec((None, N_CONV_COLS, CONV_WIDTH, LANES), lambda b, j: (layer, 0, 0, 0)),
                  pl.BlockSpec((None, N_CONV_COLS, 1, LANES), lambda b, j: (layer, 0, 0, 0)),
                  vec, vec],
        out_specs=[pl.BlockSpec((CONV_TILE, D_CONV), lambda b, j: (b * tps + j, 0)),
                   pl.BlockSpec((None, CONV_HALO, D_CONV), lambda b, j: (b, 0, 0))],
        out_shape=[jax.ShapeDtypeStruct((tl.n_prompt, D_CONV), bf16),
                   jax.ShapeDtypeStruct((tl.bp, CONV_HALO, D_CONV), f32)],
        scratch_shapes=[pltpu.VMEM((N_CONV_COLS, HALO_PAD + CONV_TILE, LANES), f32),
                        pltpu.VMEM((N_CONV_COLS, CONV_TILE, LANES), f32)],
        compiler_params=_cparams(2),
        name="conv_prompt",
    )(a, w_cols, b_cols, g_ln, b_ln)


def _inproj(tl, layer, x, mod_seq, mod_tok, g_norm, w_in_bf, cos_tab, sin_tab):
    split_x = isinstance(x, tuple)
    tok_spec = pl.BlockSpec((TS, D_MODEL), lambda i: (i, 0))
    if split_x:
        x_args = list(x)
        x_specs = [pl.BlockSpec((TS, D_MODEL), lambda i: (tl.prompt_block(i), 0)),
                   pl.BlockSpec((TS, D_MODEL), lambda i: (tl.sample_block(i), 0))]
    else:
        x_args, x_specs = [x], [tok_spec]
    sh_seq, sh_tok = _mod_specs(tl, layer, 0)
    sc_seq, sc_tok = _mod_specs(tl, layer, 1)

    def table_block(i):
        return jnp.where(i < tl.np_steps, i % tl.steps_per_seq, tl.steps_per_seq + tl.sample_block(i))

    tab_spec = pl.BlockSpec((TS, HEAD_D), lambda i: (table_block(i), 0))
    row_spec = pl.BlockSpec((TS, D_RET), lambda i: (i, 0))
    row_sd = lambda dt: jax.ShapeDtypeStruct((tl.n_tok, D_RET), dt)
    return pl.pallas_call(
        functools.partial(_inproj_kernel, np_steps=tl.np_steps, split_x=split_x),
        grid=(tl.n_steps,),
        in_specs=x_specs + [
            sh_seq, sh_tok, sc_seq, sc_tok,
            pl.BlockSpec((None, 1, D_MODEL), lambda i: (layer, 0, 0)),
            pl.BlockSpec((None, D_MODEL, D_IN), lambda i: (layer, 0, 0)),
            tab_spec, tab_spec,
        ],
        out_specs=[row_spec] * 5,
        out_shape=[row_sd(bf16), row_sd(f32), row_sd(bf16), row_sd(f32), row_sd(f32)],
        compiler_params=_cparams(1),
        name="inproj",
    )(*x_args, mod_seq, mod_tok, mod_seq, mod_tok, g_norm, w_in_bf, cos_tab, sin_tab)


SORT_ROWS = 2 * TM + N_EXPERTS * SUBLANES
XS_HALF = D_MODEL // 2
XS_W = XS_HALF + LANES
u32 = jnp.uint32
ROUTE_POS, ROUTE_W0, ROUTE_W1 = (0, 2), (2, 5), (5, 8)


def _pack_bf16_pair(x):
    lo = lax.shift_right_logical(lax.bitcast_convert_type(x[:, 0:XS_HALF], u32), u32(16))
    hi = lax.bitcast_convert_type(x[:, XS_HALF:D_MODEL], u32) & u32(0xFFFF0000)
    return hi | lo


def _unpack_bf16_pair(words):
    lo = lax.bitcast_convert_type(lax.shift_left(words, u32(16)), f32).astype(bf16)
    hi = lax.bitcast_convert_type(words & u32(0xFFFF0000), f32).astype(bf16)
    return lo, hi


def _split3(x):
    a = x.astype(bf16)
    r = x - a.astype(f32)
    b = r.astype(bf16)
    c = (r - b.astype(f32)).astype(bf16)
    return a, b, c


def _first_of4(vals, m):
    return jnp.where(vals[0] == m, 0.0, jnp.where(vals[1] == m, 1.0, jnp.where(vals[2] == m, 2.0, 3.0)))


def _rows_to_tile(rows, n_rows):
    sub = lax.broadcasted_iota(i32, (n_rows, TM), 0)
    out = jnp.zeros((n_rows, TM), f32)
    for r, val in enumerate(rows):
        out = jnp.where(sub == r, val, out)
    return out


def _outproj_kernel(*refs, np_steps, split_x):
    if split_x:
        xp_ref, xs_ref = refs[:2]
        refs = refs[2:]
    else:
        x_ref = refs[0]
        refs = refs[1:]
    (retp_ref, rets_ref, cvp_ref, cvs_ref, gts_ref, gtt_ref, shs_ref, sht_ref, scs_ref, sct_ref,
     g_ref, wo_ref, wr_ref, br_ref,
     xo_ref, h2_ref, rows_ref, cols_ref, cnt_ref) = refs
    is_s = pl.program_id(0) >= np_steps

    def step(sample):
        for t in range(STEP_TILES):
            rs = slice(t * TM, (t + 1) * TM)
            if split_x:
                x = xs_ref[rs, :] if sample else xp_ref[rs, :]
            else:
                x = x_ref[rs, :]
            ret = rets_ref[rs, :] if sample else retp_ref[rs, :]
            cv = cvs_ref[rs, :] if sample else cvp_ref[rs, :]
            gt, sh, sc = ((gtt_ref[rs, :], sht_ref[rs, :], sct_ref[rs, :]) if sample
                          else (gts_ref[...], shs_ref[...], scs_ref[...]))
            mix_out = (jnp.dot(ret, wo_ref[0:D_RET, :], preferred_element_type=f32)
                       + jnp.dot(cv, wo_ref[D_RET:D_RET + D_CONV, :], preferred_element_type=f32))
            xn = x + gt * mix_out
            xo_ref[rs, :] = xn
            h2 = _rms(xn, g_ref[...]) * (1.0 + sc) + sh
            h_hi = h2.astype(bf16)
            h2_ref[rs, :] = h_hi
            rows, cols, cnt = _route_tile(h_hi, wr_ref, br_ref)
            rows_ref[t] = rows
            cols_ref[rs, :] = cols
            cnt_ref[t] = cnt

    pl.when(is_s)(functools.partial(step, True))
    pl.when(jnp.logical_not(is_s))(functools.partial(step, False))


def _route_tile(h_bf, wr_ref, br_ref):
    logits = jnp.dot(h_bf, wr_ref[...], preferred_element_type=f32) + br_ref[...]
    lt = logits.T
    row = [lt[e:e + 1, :] for e in range(N_EXPERTS)]
    top = functools.reduce(jnp.maximum, row)
    ex = [jnp.exp(r - top) for r in row]
    den = functools.reduce(jnp.add, ex)
    p = [v / den for v in ex]

    best = None
    for g in range(N_GROUPS):
        a = p[g * GROUP_SIZE:(g + 1) * GROUP_SIZE]
        m1 = functools.reduce(jnp.maximum, a)
        i1 = _first_of4(a, m1)
        b = [jnp.where(i1 == float(j), -1.0, a[j]) for j in range(GROUP_SIZE)]
        m2 = functools.reduce(jnp.maximum, b)
        i2 = _first_of4(b, m2)
        cand = (m1 + m2, m1, m2, i1 + float(g * GROUP_SIZE), i2 + float(g * GROUP_SIZE))
        if best is None:
            best = cand
        else:
            take = cand[0] > best[0]
            best = tuple(jnp.where(take, c, o) for c, o in zip(cand, best))
    _, m1, m2, e0, e1 = best
    denom = m1 + m2
    w0 = m1 / denom
    w1 = m2 / denom

    ex_id = lax.broadcasted_iota(i32, (N_EXPERTS, TM), 0).astype(f32)
    sel0 = ex_id == e0
    sel1 = ex_id == e1
    ind = jnp.where(sel0 | sel1, 1.0, 0.0)
    t_r = lax.broadcasted_iota(i32, (TM, TM), 0)
    t_c = lax.broadcasted_iota(i32, (TM, TM), 1)
    earlier = jnp.where(t_r < t_c, 1.0, 0.0).astype(bf16)
    prefix = jnp.dot(ind.astype(bf16), earlier, preferred_element_type=f32)
    cnt = jnp.sum(ind, axis=-1, keepdims=True)
    cnt8 = jnp.floor((cnt + float(SUBLANES - 1)) * (1.0 / SUBLANES)) * float(SUBLANES)
    e_r = lax.broadcasted_iota(i32, (N_EXPERTS, N_EXPERTS), 0)
    e_c = lax.broadcasted_iota(i32, (N_EXPERTS, N_EXPERTS), 1)
    below = jnp.where(e_c < e_r, 1.0, 0.0).astype(bf16)
    seg_off = jnp.dot(below, jnp.broadcast_to(cnt8, (N_EXPERTS, TM)).astype(bf16),
                      preferred_element_type=f32)
    where_to = seg_off + prefix
    pos0 = jnp.sum(jnp.where(sel0, where_to, 0.0), axis=0, keepdims=True)
    pos1 = jnp.sum(jnp.where(sel1, where_to, 0.0), axis=0, keepdims=True)

    w0p = [v.astype(f32) for v in _split3(w0)]
    w1p = [v.astype(f32) for v in _split3(w1)]
    info = [pos0, pos1] + w0p + w1p
    return (_rows_to_tile(info, SUBLANES), _rows_to_tile(info, LANES).T,
            jnp.broadcast_to(cnt, (N_EXPERTS, LANES)))


def _outproj(tl, layer, x, ret_p, ret_s, cv_p, cv_s, mod_seq, mod_tok, g_norm, w_out_bf, wr_bf, br_pad):
    split_x = isinstance(x, tuple)
    tok_spec = pl.BlockSpec((TS, D_MODEL), lambda i: (i, 0))
    p_spec = lambda w: pl.BlockSpec((TS, w), lambda i: (tl.prompt_block(i), 0))
    s_spec = lambda w: pl.BlockSpec((TS, w), lambda i: (tl.sample_block(i), 0))
    if split_x:
        x_args, x_specs = list(x), [p_spec(D_MODEL), s_spec(D_MODEL)]
    else:
        x_args, x_specs = [x], [tok_spec]
    mods = []
    for col in (2, 3, 4):
        mods += list(_mod_specs(tl, layer, col))
    wr_spec = pl.BlockSpec((D_MODEL, LANES), lambda i: (0, 0))
    return pl.pallas_call(
        functools.partial(_outproj_kernel, np_steps=tl.np_steps, split_x=split_x),
        grid=(tl.n_steps,),
        in_specs=x_specs + [p_spec(D_RET), s_spec(D_RET), p_spec(D_CONV), s_spec(D_CONV)] + mods + [
            pl.BlockSpec((None, 1, D_MODEL), lambda i: (layer, 0, 0)),
            pl.BlockSpec((None, D_MODEL, D_MODEL), lambda i: (layer, 0, 0)),
            wr_spec,
            pl.BlockSpec((1, LANES), lambda i: (0, 0)),
        ],
        out_specs=[tok_spec, tok_spec,
                   pl.BlockSpec((STEP_TILES, SUBLANES, TM), lambda i: (i, 0, 0)),
                   pl.BlockSpec((TS, LANES), lambda i: (i, 0)),
                   pl.BlockSpec((STEP_TILES, N_EXPERTS, LANES), lambda i: (i, 0, 0))],
        out_shape=[jax.ShapeDtypeStruct((tl.n_tok, D_MODEL), f32),
                   jax.ShapeDtypeStruct((tl.n_tok, D_MODEL), bf16),
                   jax.ShapeDtypeStruct((tl.n_tiles, SUBLANES, TM), f32),
                   jax.ShapeDtypeStruct((tl.n_tok, LANES), f32),
                   jax.ShapeDtypeStruct((tl.n_tiles, N_EXPERTS, LANES), f32)],
        compiler_params=_cparams(1),
        name="outproj_router",
    )(*x_args, ret_p, ret_s, cv_p, cv_s, *([mod_seq, mod_tok] * 3), g_norm, w_out_bf, wr_bf, br_pad)


N_CHUNKS = SORT_ROWS // SUBLANES


class _Layout:
    def __init__(self, n_tiles):
        self.n_tiles = n_tiles
        self.tail_start = n_tiles * N_CHUNKS
        self.tail_n8 = self.tail_start + N_EXPERTS
        worst = 2 * n_tiles * TM + n_tiles * N_EXPERTS * (SUBLANES - 1) + N_EXPERTS * (BM - SUBLANES)
        self.n_blocks = -(-worst // BM)
        self.cap = self.n_blocks * BM
        self.dump = self.cap
        self.xs_rows = self.cap + -(-2 * STEP_TILES * SORT_ROWS // BM) * BM


def _moe_tables(lay, tile_counts):
    c8 = ((tile_counts.astype(i32) + SUBLANES - 1) // SUBLANES) * SUBLANES
    base8 = jnp.cumsum(c8, axis=0) - c8
    tot8 = jnp.sum(c8, axis=0)
    region = ((tot8 + BM - 1) // BM) * BM
    g_end = jnp.cumsum(region)
    g_start = g_end - region
    seg_end = jnp.cumsum(c8, axis=1)
    seg_dst = g_start[None, :] + base8
    n_used = g_end[-1] // BM
    blk = jnp.arange(lay.n_blocks, dtype=i32)
    block_e = jnp.minimum(jnp.sum((g_end[None, :] <= blk[:, None] * BM).astype(i32), axis=1), N_EXPERTS - 1)
    block_e = jnp.where(blk < n_used, block_e, block_e[n_used - 1])
    row0 = jnp.arange(N_CHUNKS, dtype=i32) * SUBLANES
    owner = jnp.sum((seg_end[:, None, :] <= row0[None, :, None]).astype(i32), axis=-1)
    onehot = (owner[:, :, None] == jnp.arange(N_EXPERTS, dtype=i32)[None, None, :]).astype(i32)
    delta = seg_dst - (seg_end - c8)
    chunk_dst = jnp.where(owner < N_EXPERTS, row0[None, :] + jnp.sum(onehot * delta[:, None, :], axis=-1), -1)
    tab = jnp.concatenate([chunk_dst.ravel(), g_start + tot8, (region - tot8) // SUBLANES]).astype(i32)
    ids = jnp.arange(N_EXPERTS, dtype=i32)
    later = jnp.where((ids[None, :] > ids[:, None]) & (region[None, :] > 0), ids[None, :], N_EXPERTS)
    next_e = jnp.min(later, axis=1)
    next_e = jnp.where(next_e == N_EXPERTS, -1, next_e)
    owner_end = jnp.sum((block_e[:, None] == ids[None, :]).astype(i32) * (g_start + tot8)[None, :], axis=1)
    rows_used = jnp.clip(owner_end - blk * BM, 0, BM)
    return tab, jnp.concatenate([block_e, n_used[None], next_e, rows_used]).astype(i32)


def _for_chunks(n, fn):
    def body(c, carry):
        fn(c)
        return carry

    lax.fori_loop(0, n, body, 0)


def _dispatch_kernel(tab_ref, h2_ref, rows_ref, cols_ref, xs_hbm, sorted_ref, zero_ref, sem, *, lay, n_steps):
    i = pl.program_id(0)
    slot = i % 2
    step_rows = STEP_TILES * SORT_ROWS

    def tail_copy(dst):
        return pltpu.make_async_copy(zero_ref, xs_hbm.at[pl.ds(dst, SUBLANES)], sem.at[2])

    def wait_step(slot_):
        pltpu.make_async_copy(sorted_ref.at[slot_], xs_hbm.at[pl.ds(0, step_rows)], sem.at[slot_]).wait()

    @pl.when(i == 0)
    def _():
        zero_ref[...] = jnp.zeros_like(zero_ref)
        for e in range(N_EXPERTS):
            start = tab_ref[lay.tail_start + e]
            _for_chunks(tab_ref[lay.tail_n8 + e],
                        lambda c: tail_copy(pl.multiple_of(start + c * SUBLANES, SUBLANES)).start())

    @pl.when(i >= 2)
    def _():
        wait_step(slot)

    r_id = lax.broadcasted_iota(i32, (SORT_ROWS, TM), 0).astype(f32)
    lane = lax.broadcasted_iota(i32, (TM, LANES), 1)
    for t in range(STEP_TILES):
        pos0 = rows_ref[t, 0:1, :]
        pos1 = rows_ref[t, 1:2, :]
        p0 = r_id == pos0
        p1 = r_id == pos1
        perm = jnp.where(p0 | p1, 1.0, 0.0).astype(bf16)
        cols = cols_ref[t * TM:(t + 1) * TM, :]
        wpart0 = jnp.where((lane >= ROUTE_W0[0]) & (lane < ROUTE_W0[1]), cols, 0.0).astype(bf16)
        wpart1 = jnp.where((lane >= ROUTE_W1[0]) & (lane < ROUTE_W1[1]), cols, 0.0).astype(bf16)
        sw = (jnp.dot(jnp.where(p0, 1.0, 0.0).astype(bf16), wpart0, preferred_element_type=f32)
              + jnp.dot(jnp.where(p1, 1.0, 0.0).astype(bf16), wpart1, preferred_element_type=f32))
        base = t * SORT_ROWS
        sorted_ref[slot, base:base + SORT_ROWS, 0:XS_HALF] = _pack_bf16_pair(jnp.dot(
            perm, h2_ref[t * TM:(t + 1) * TM, :], preferred_element_type=f32))
        sorted_ref[slot, base:base + SORT_ROWS, XS_HALF:XS_W] = lax.bitcast_convert_type(jnp.broadcast_to(
            jnp.sum(sw, axis=-1, keepdims=True), (SORT_ROWS, LANES)), u32)

    for t in range(STEP_TILES):
        for c in range(N_CHUNKS):
            row = t * SORT_ROWS + c * SUBLANES
            dst = tab_ref[(i * STEP_TILES + t) * N_CHUNKS + c]
            dst = jnp.where(dst < 0, lay.dump + slot * step_rows + row, dst)
            pltpu.make_async_copy(sorted_ref.at[slot, pl.ds(row, SUBLANES)],
                                  xs_hbm.at[pl.ds(pl.multiple_of(dst, SUBLANES), SUBLANES)], sem.at[slot]).start()

    @pl.when(i == n_steps - 1)
    def _():
        if n_steps >= 2:
            wait_step(1 - slot)
        wait_step(slot)
        for e in range(N_EXPERTS):
            _for_chunks(tab_ref[lay.tail_n8 + e], lambda c: tail_copy(0).wait())


def _dispatch(tl, lay, tab, h2, rows, cols):
    grid_spec = pltpu.PrefetchScalarGridSpec(
        num_scalar_prefetch=1,
        grid=(tl.n_steps,),
        in_specs=[pl.BlockSpec((TS, D_MODEL), lambda i, t: (i, 0)),
                  pl.BlockSpec((STEP_TILES, SUBLANES, TM), lambda i, t: (i, 0, 0)),
                  pl.BlockSpec((TS, LANES), lambda i, t: (i, 0))],
        out_specs=pl.BlockSpec(memory_space=pl.ANY),
        scratch_shapes=[pltpu.VMEM((2, STEP_TILES * SORT_ROWS, XS_W), u32), pltpu.VMEM((SUBLANES, XS_W), u32),
                        pltpu.SemaphoreType.DMA((3,))],
    )
    return pl.pallas_call(
        functools.partial(_dispatch_kernel, lay=lay, n_steps=tl.n_steps),
        grid_spec=grid_spec,
        out_shape=jax.ShapeDtypeStruct((lay.xs_rows, XS_W), u32),
        compiler_params=_cparams(1),
        name="moe_dispatch",
    )(tab, h2, rows, cols)


def _expert_kernel(be_ref, xs_ref, wg_hbm, wu_hbm, wd_hbm, ys_ref, stage, w_bf, sem, *, n_blocks, layer):
    j = pl.program_id(0)

    def fetch(e):
        return [pltpu.make_async_copy(w.at[layer, e], stage.at[k], sem.at[k])
                for k, w in enumerate((wg_hbm, wu_hbm, wd_hbm))]

    @pl.when(j < be_ref[n_blocks])
    def _():
        e = be_ref[j]

        @pl.when(j == 0)
        def _():
            for copy in fetch(e):
                copy.start()

        @pl.when((j == 0) | (e != be_ref[jnp.maximum(j - 1, 0)]))
        def _():
            for copy in fetch(e):
                copy.wait()
            for k in range(3):
                w_bf[k] = stage[k].astype(bf16)
            nxt = be_ref[n_blocks + 1 + e]

            @pl.when(nxt >= 0)
            def _():
                for copy in fetch(nxt):
                    copy.start()

        def ffn(rows):
            x_lo, x_hi = _unpack_bf16_pair(xs_ref[rows, 0:XS_HALF])

            def first_layer(k):
                return (jnp.dot(x_lo, w_bf[k, 0:XS_HALF, :], preferred_element_type=f32)
                        + jnp.dot(x_hi, w_bf[k, XS_HALF:D_MODEL, :], preferred_element_type=f32))

            mid = (jax.nn.silu(first_layer(0)) * first_layer(1)).astype(bf16)
            slot_w = lax.bitcast_convert_type(xs_ref[rows, XS_HALF:XS_HALF + 1], f32)
            ys_ref[rows, :] = jnp.dot(mid, w_bf[2], preferred_element_type=f32) * slot_w

        half = BM // 2
        rows_used = be_ref[n_blocks + 1 + N_EXPERTS + j]

        @pl.when(rows_used > half)
        def _():
            ffn(slice(0, BM))

        @pl.when(rows_used <= half)
        def _():
            ffn(slice(0, half))
            ys_ref[half:BM, :] = jnp.zeros((BM - half, D_MODEL), f32)


def _experts(layer, lay, block_e, xs, w_gate, w_up, w_down):
    n_blocks = lay.n_blocks
    d_ff = w_gate.shape[-1]
    assert d_ff == D_MODEL
    used = lambda j, be: jnp.minimum(j, be[n_blocks] - 1)
    any_spec = pl.BlockSpec(memory_space=pl.ANY)
    grid_spec = pltpu.PrefetchScalarGridSpec(
        num_scalar_prefetch=1,
        grid=(n_blocks,),
        in_specs=[pl.BlockSpec((BM, XS_W), lambda j, be: (used(j, be), 0)), any_spec, any_spec, any_spec],
        out_specs=pl.BlockSpec((BM, D_MODEL), lambda j, be: (used(j, be), 0)),
        scratch_shapes=[pltpu.VMEM((3, D_MODEL, d_ff), f32), pltpu.VMEM((3, D_MODEL, d_ff), bf16),
                        pltpu.SemaphoreType.DMA((3,))],
    )
    return pl.pallas_call(
        functools.partial(_expert_kernel, n_blocks=n_blocks, layer=layer),
        grid_spec=grid_spec,
        out_shape=jax.ShapeDtypeStruct((lay.cap, D_MODEL), f32),
        compiler_params=_cparams(1),
        name="moe_experts",
    )(block_e, xs, w_gate, w_up, w_down)


def _combine_kernel(tab_ref, ys_hbm, cols_ref, x_ref, gts_ref, gtt_ref, *rest, n_steps, np_steps, final):
    if final:
        gf_ref, yp_ref, ysm_ref, staged, sem = rest
    else:
        xo_ref, staged, sem = rest
    i = pl.program_id(0)
    slot = i % 2
    is_s = i >= np_steps

    def start_step(step, slot_):
        for c in range(STEP_TILES * N_CHUNKS):
            src = jnp.maximum(tab_ref[step * (STEP_TILES * N_CHUNKS) + c], 0)
            pltpu.make_async_copy(ys_hbm.at[pl.ds(pl.multiple_of(src, SUBLANES), SUBLANES)],
                                  staged.at[slot_, pl.ds(c * SUBLANES, SUBLANES)], sem.at[slot_]).start()

    @pl.when(i == 0)
    def _():
        start_step(0, 0)

    @pl.when(i + 1 < n_steps)
    def _():
        start_step(i + 1, 1 - slot)

    pltpu.make_async_copy(ys_hbm.at[pl.ds(0, STEP_TILES * SORT_ROWS)], staged.at[slot], sem.at[slot]).wait()

    def step(sample):
        lane = lax.broadcasted_iota(i32, (TM, SORT_ROWS), 1).astype(f32)
        for t in range(STEP_TILES):
            rs = slice(t * TM, (t + 1) * TM)
            unperm = jnp.where((lane == cols_ref[rs, 0:1]) | (lane == cols_ref[rs, 1:2]), 1.0, 0.0).astype(bf16)
            parts = _split3(staged[slot, t * SORT_ROWS:(t + 1) * SORT_ROWS, :])
            ff = sum(jnp.dot(unperm, part, preferred_element_type=f32) for part in parts)
            xn = x_ref[rs, :] + (gtt_ref[rs, :] if sample else gts_ref[...]) * ff
            if final:
                (ysm_ref if sample else yp_ref)[rs, :] = _rms(xn, gf_ref[...])
            else:
                xo_ref[rs, :] = xn

    pl.when(is_s)(functools.partial(step, True))
    pl.when(jnp.logical_not(is_s))(functools.partial(step, False))


def _combine(tl, layer, tab, ys, cols, x, mod_seq, mod_tok, g_final):
    final = g_final is not None
    tok_spec = pl.BlockSpec((TS, D_MODEL), lambda i, t: (i, 0))
    gt_seq, gt_tok = _mod_specs(tl, layer, 5)
    in_specs = [pl.BlockSpec(memory_space=pl.ANY), pl.BlockSpec((TS, LANES), lambda i, t: (i, 0)),
                tok_spec, gt_seq, gt_tok]
    args = [tab, ys, cols, x, mod_seq, mod_tok]
    if final:
        in_specs.append(pl.BlockSpec((1, D_MODEL), lambda i, t: (0, 0)))
        args.append(g_final)
        out_specs = [pl.BlockSpec((TS, D_MODEL), lambda i, t: (tl.prompt_block(i), 0)),
                     pl.BlockSpec((TS, D_MODEL), lambda i, t: (tl.sample_block(i), 0))]
        out_shape = [jax.ShapeDtypeStruct((tl.n_prompt, D_MODEL), f32),
                     jax.ShapeDtypeStruct((tl.n_sample, D_MODEL), f32)]
    else:
        out_specs = tok_spec
        out_shape = jax.ShapeDtypeStruct((tl.n_tok, D_MODEL), f32)
    grid_spec = pltpu.PrefetchScalarGridSpec(
        num_scalar_prefetch=1,
        grid=(tl.n_steps,),
        in_specs=in_specs,
        out_specs=out_specs,
        scratch_shapes=[pltpu.VMEM((2, STEP_TILES * SORT_ROWS, D_MODEL), f32), pltpu.SemaphoreType.DMA((2,))],
    )
    return pl.pallas_call(
        functools.partial(_combine_kernel, n_steps=tl.n_steps, np_steps=tl.np_steps, final=final),
        grid_spec=grid_spec,
        out_shape=out_shape,
        compiler_params=_cparams(1),
        name="moe_combine",
    )(*args)


def _rope_tables(tl):
    half = HEAD_D // 2
    inv = ROPE_BASE ** (-jnp.arange(half, dtype=f32) / half)
    pos_p = jnp.arange(tl.tp, dtype=i32)
    pos_s = PAST_LEN + jnp.arange(tl.ts, dtype=i32)
    pos = jnp.concatenate([pos_p, jnp.tile(pos_s, tl.bs)])
    ang = pos.astype(f32)[:, None] * inv[None, :]
    cos, sin = jnp.cos(ang), jnp.sin(ang)
    return jnp.concatenate([cos, cos], axis=-1), jnp.concatenate([-sin, sin], axis=-1)


def kernel(x_prompt, x_sample, state_ret, state_conv, c_prompt, c_sample, w_mod, b_mod, g_mix_norm, w_in,
           w_conv, b_conv, g_conv_ln, b_conv_ln, g_ret_gn, w_out, g_ffn_norm, w_router, b_router,
           w_exp_gate, w_exp_up, w_exp_down, g_final):
    bp, tp, _ = x_prompt.shape
    bs, ts, _ = x_sample.shape
    depth = w_mod.shape[0]
    tl = _Tiles(bp, tp, bs, ts)
    lay = _Layout(tl.n_tiles)

    c_all = jnp.concatenate([c_prompt, jnp.repeat(c_sample, ts, axis=0)], axis=0)
    mod_seq, mod_tok = _modulation(c_all, bp, w_mod, b_mod)
    mod_seq = mod_seq.reshape(depth, bp, 1, N_MOD * D_MODEL)

    cos_tab, sin_tab = _rope_tables(tl)
    w_in_bf = w_in.astype(bf16)
    w_out_bf = w_out.astype(bf16)
    wr_pad = jnp.pad(w_router.astype(f32), ((0, 0), (0, LANES - N_EXPERTS)))
    wr_bf = wr_pad.astype(bf16)
    br_pad = jnp.pad(b_router.astype(f32), (0, LANES - N_EXPERTS)).reshape(1, LANES)
    vec3 = lambda t: t.reshape(depth, 1, t.shape[-1])
    g_mix3, g_ffn3, gn3 = vec3(g_mix_norm), vec3(g_ffn_norm), vec3(g_ret_gn)
    b_conv3, g_ln3, b_ln3 = vec3(b_conv), vec3(g_conv_ln), vec3(b_conv_ln)

    x = (x_prompt.reshape(tl.n_prompt, D_MODEL), x_sample.reshape(tl.n_sample, D_MODEL))
    ret_p, conv_p = [], []
    ret_s_all = conv_s_all = None
    for layer in range(depth):
        q, k, v, gate, a = _inproj(tl, layer, x, mod_seq, mod_tok, g_mix3, w_in_bf, cos_tab, sin_tab)
        ro_p, s_p = _retention_prompt(tl, layer, q, k, v, gate, gn3)
        ro_s, ret_s_all = _retention_sample(tl, layer, q, k, v, gate, state_ret, gn3, ret_s_all)
        co_p, buf_p = _conv_prompt(tl, layer, a, w_conv, b_conv3, g_ln3, b_ln3)
        co_s, conv_s_all = _conv_sample(tl, layer, a, state_conv, w_conv, b_conv3, g_ln3, b_ln3, conv_s_all)
        x_mid, h2, rows, cols, tile_counts = _outproj(
            tl, layer, x, ro_p, ro_s, co_p, co_s, mod_seq, mod_tok, g_ffn3, w_out_bf, wr_bf, br_pad)
        tab, block_e = _moe_tables(lay, tile_counts[:, :, 0])
        xs = _dispatch(tl, lay, tab, h2, rows, cols)
        ys = _experts(layer, lay, block_e, xs, w_exp_gate, w_exp_up, w_exp_down)
        last = layer == depth - 1
        x = _combine(tl, layer, tab, ys, cols, x_mid, mod_seq, mod_tok,
                     g_final.reshape(1, D_MODEL) if last else None)
        ret_p.append(s_p)
        conv_p.append(buf_p)
    y_p, y_s = x
    return (y_p.reshape(bp, tp, D_MODEL), y_s.reshape(bs, ts, D_MODEL),
            jnp.stack(ret_p), jnp.stack(conv_p), ret_s_all, conv_s_all)
```

```python
import functools

import jax
import jax.numpy as jnp
from jax import lax
from jax.experimental import pallas as pl
from jax.experimental.pallas import tpu as pltpu

f32 = jnp.float32
bf16 = jnp.bfloat16
i32 = jnp.int32

D_MODEL = 1024
D_RET = 512
D_CONV = 512
N_HEADS = 4
HEAD_D = 128
RET_CHUNK = 128
RET_CHUNKS_PER_STEP = 16
ROPE_BASE = 10000.0
CONV_WIDTH = 31
CONV_HALO = CONV_WIDTH - 1
N_EXPERTS = 16
N_GROUPS = 4
GROUP_SIZE = N_EXPERTS // N_GROUPS
N_MOD = 6
EPS = 1e-6
PAST_LEN = 16384
D_IN = 4 * D_RET + 2 * D_CONV

LANES = 128
SUBLANES = 8
TM = 256
STEP_TILES = 2
TS = TM * STEP_TILES
BM = 512
BM_PART = 128
CONV_ROWS = 256
NORM_ROWS = 64
SAMPLE_GROUP = 16
HALO_PAD = 32
V7X_VMEM_BYTES = 64 * 1024 * 1024
VMEM_LIMIT = V7X_VMEM_BYTES - 8 * 1024 * 1024


def _cparams(n_axes, vmem=VMEM_LIMIT):
    return pltpu.CompilerParams(dimension_semantics=("arbitrary",) * n_axes, vmem_limit_bytes=vmem)


def _mod_kernel(c_ref, w_ref, b_ref, seq_ref, tok_ref):
    cond = jax.nn.silu(c_ref[...]).astype(bf16)
    mod = jnp.dot(cond, w_ref[...].astype(bf16), preferred_element_type=f32) + b_ref[...]
    n_seq = seq_ref.shape[0]
    seq_ref[...] = mod[0:n_seq, :]
    tok_ref[...] = mod[n_seq:, :]


def _modulation(c_all, n_seq, w_mod, b_mod):
    depth = w_mod.shape[0]
    m = c_all.shape[0]
    n_tok = m - n_seq
    assert n_seq % SUBLANES == 0
    return pl.pallas_call(
        _mod_kernel,
        grid=(depth, N_MOD),
        in_specs=[
            pl.BlockSpec((m, D_MODEL), lambda l, j: (0, 0)),
            pl.BlockSpec((None, D_MODEL, D_MODEL), lambda l, j: (l, 0, j)),
            pl.BlockSpec((None, 1, D_MODEL), lambda l, j: (l, 0, j)),
        ],
        out_specs=[pl.BlockSpec((None, n_seq, D_MODEL), lambda l, j: (l, 0, j)),
                   pl.BlockSpec((None, n_tok, D_MODEL), lambda l, j: (l, 0, j))],
        out_shape=[jax.ShapeDtypeStruct((depth, n_seq, N_MOD * D_MODEL), f32),
                   jax.ShapeDtypeStruct((depth, n_tok, N_MOD * D_MODEL), f32)],
        compiler_params=_cparams(2),
        name="modulation",
    )(c_all, w_mod, b_mod.reshape(depth, 1, N_MOD * D_MODEL))


class _Tiles:
    def __init__(self, bp, tp, bs, ts):
        self.bp, self.tp, self.bs, self.ts = bp, tp, bs, ts
        self.n_prompt = bp * tp
        self.n_sample = bs * ts
        self.n_tok = self.n_prompt + self.n_sample
        assert tp % TS == 0 and self.n_sample % TS == 0
        self.tiles_per_seq = tp // TM
        self.np_tiles = self.n_prompt // TM
        self.n_tiles = self.n_tok // TM
        self.steps_per_seq = tp // TS
        self.np_steps = self.n_prompt // TS
        self.n_steps = self.n_tok // TS

    def prompt_block(self, i):
        return jnp.minimum(i, self.np_steps - 1)

    def sample_block(self, i):
        return jnp.maximum(i - self.np_steps, 0)

    def seq_index(self, i):
        return jnp.minimum(i // self.steps_per_seq, self.bp - 1)


def _mod_specs(tl, layer, col):
    seq = pl.BlockSpec((None, None, 1, D_MODEL), lambda i, *_: (layer, tl.seq_index(i), 0, col))
    tok = pl.BlockSpec((None, TS, D_MODEL), lambda i, *_: (layer, tl.sample_block(i), col))
    return seq, tok


def _rms(x, g):
    return x * lax.rsqrt(jnp.mean(x * x, axis=-1, keepdims=True) + EPS) * g


def _head_norm_gate(o, gn, gate):
    mu = jnp.mean(o, axis=-1, keepdims=True)
    var = jnp.mean(jnp.square(o - mu), axis=-1, keepdims=True)
    return jax.nn.silu(gate) * ((o - mu) * lax.rsqrt(var + EPS) * gn)


def _dot_nt(a, b):
    return lax.dot_general(a, b, (((1,), (1,)), ((), ())), preferred_element_type=f32)


def _dot_tn(a, b):
    return lax.dot_general(a, b, (((0,), (0,)), ((), ())), preferred_element_type=f32)


def _ret_prompt_kernel(q_ref, k_ref, v_ref, gate_ref, dec_ref, qd_ref, kd_ref, cd_ref, gn_ref,
                       o_ref, s_out_ref, s_ref):
    c = pl.program_id(1)

    @pl.when(c == 0)
    def _():
        s_ref[...] = jnp.zeros_like(s_ref)

    for ci in range(RET_CHUNKS_PER_STEP):
        rows = slice(ci * RET_CHUNK, (ci + 1) * RET_CHUNK)
        for hd in range(N_HEADS):
            sl = slice(hd * HEAD_D, (hd + 1) * HEAD_D)
            kh = k_ref[rows, sl]
            qb = q_ref[rows, sl]
            kb = kh.astype(bf16)
            vb = v_ref[rows, sl]
            s_old = s_ref[hd]
            scores = _dot_nt(qb, kb) * dec_ref[hd]
            inner = jnp.dot(scores.astype(bf16), vb, preferred_element_type=f32)
            cross = jnp.dot(qb, s_old.astype(bf16), preferred_element_type=f32) * qd_ref[hd]
            s_ref[hd] = s_old * cd_ref[hd] + _dot_tn((kh * kd_ref[hd]).astype(bf16), vb)
            o_ref[rows, sl] = _head_norm_gate(inner + cross, gn_ref[:, sl], gate_ref[rows, sl]).astype(bf16)

    @pl.when(c == pl.num_programs(1) - 1)
    def _():
        s_out_ref[...] = s_ref[...]


def _decay_tables(chunk, true_len):
    lg = jnp.log(1.0 - 2.0 ** (-5.0 - jnp.arange(N_HEADS, dtype=f32)))
    idx = jnp.arange(chunk, dtype=f32)
    rel = idx[:, None] - idx[None, :]
    decay = jnp.where(rel[None] >= 0, jnp.exp(jnp.maximum(rel, 0.0)[None] * lg[:, None, None]), 0.0)
    q_decay = jnp.exp((idx[None, :] + 1.0) * lg[:, None])
    k_decay = jnp.exp((true_len - 1.0 - idx[None, :]) * lg[:, None])
    c_decay = jnp.exp(true_len * lg)
    return decay, q_decay, k_decay, c_decay


def _retention_prompt(tl, layer, q, k, v, gate, g_ret_gn):
    step_rows = RET_CHUNK * RET_CHUNKS_PER_STEP
    assert tl.tp % step_rows == 0
    n_chunks = tl.tp // step_rows
    decay, q_decay, k_decay, c_decay = _decay_tables(RET_CHUNK, RET_CHUNK)
    bcast = lambda t: jnp.broadcast_to(t[:, :, None], (N_HEADS, RET_CHUNK, HEAD_D))
    cd = jnp.broadcast_to(c_decay[:, None, None], (N_HEADS, 1, HEAD_D))
    tok_spec = pl.BlockSpec((step_rows, D_RET), lambda b, c: (b * n_chunks + c, 0))
    tab_spec = pl.BlockSpec((N_HEADS, RET_CHUNK, HEAD_D), lambda b, c: (0, 0, 0))
    return pl.pallas_call(
        _ret_prompt_kernel,
        grid=(tl.bp, n_chunks),
        in_specs=[tok_spec] * 4 + [tab_spec] * 3 + [
            pl.BlockSpec((N_HEADS, 1, HEAD_D), lambda b, c: (0, 0, 0)),
            pl.BlockSpec((None, 1, D_RET), lambda b, c: (layer, 0, 0)),
        ],
        out_specs=[tok_spec, pl.BlockSpec((None, N_HEADS, HEAD_D, HEAD_D), lambda b, c: (b, 0, 0, 0))],
        out_shape=[jax.ShapeDtypeStruct((tl.n_prompt, D_RET), bf16),
                   jax.ShapeDtypeStruct((tl.bp, N_HEADS, HEAD_D, HEAD_D), f32)],
        scratch_shapes=[pltpu.VMEM((N_HEADS, HEAD_D, HEAD_D), f32)],
        compiler_params=_cparams(2),
        name="retention_prompt",
    )(q, k, v, gate, decay, bcast(q_decay), bcast(k_decay), cd, g_ret_gn)


def _ret_sample_kernel(q_ref, k_ref, v_ref, gate_ref, s_in_ref, dec_ref, qd_ref, kd_ref, cd_ref, gn_ref,
                       *rest, ts):
    o_ref, s_all_ref = rest[-2:]
    s_out_ref = s_all_ref.at[0]
    for other in range(1, s_all_ref.shape[0]):
        s_all_ref[other] = jnp.zeros(s_all_ref.shape[1:], f32)
    seqs_per_tile = SUBLANES // ts
    row = lax.broadcasted_iota(i32, (SUBLANES, HEAD_D), 0)
    q_all = q_ref[...].astype(f32)
    v_all = v_ref[...].astype(f32)
    outs = []
    for t in range(SAMPLE_GROUP // seqs_per_tile):
        rows = slice(t * SUBLANES, (t + 1) * SUBLANES)
        heads = []
        for hd in range(N_HEADS):
            sl = slice(hd * HEAD_D, (hd + 1) * HEAD_D)
            qh = q_all[rows, sl]
            kh = k_ref[rows, sl] * kd_ref[hd]
            vb = v_all[rows, sl].astype(bf16)
            qb = qh.astype(bf16)
            scores = _dot_nt(qb, k_ref[rows, sl].astype(bf16)) * dec_ref[hd]
            o = jnp.dot(scores.astype(bf16), vb, preferred_element_type=f32)
            for s in range(seqs_per_tile):
                b = t * seqs_per_tile + s
                mine = (row >= s * ts) & (row < (s + 1) * ts)
                s_old = s_in_ref[b, hd]
                q_s = jnp.where(mine, qh, 0.0).astype(bf16)
                k_s = jnp.where(mine, kh, 0.0).astype(bf16)
                o = o + jnp.dot(q_s, s_old.astype(bf16), preferred_element_type=f32) * qd_ref[hd]
                s_out_ref[b, hd] = s_old * cd_ref[hd] + _dot_tn(k_s, vb)
            heads.append(_head_norm_gate(o, gn_ref[:, sl], gate_ref[rows, sl]))
        outs.append(jnp.concatenate(heads, axis=-1))
    o_ref[...] = jnp.concatenate(outs, axis=0).astype(bf16)


def _retention_sample(tl, layer, q, k, v, gate, state_ret, g_ret_gn, prev_states):
    ts = tl.ts
    depth = state_ret.shape[0]
    assert SUBLANES % ts == 0 and tl.bs % SAMPLE_GROUP == 0
    seqs_per_tile = SUBLANES // ts
    decay, q_decay, k_decay, c_decay = _decay_tables(ts, ts)
    eye = jnp.eye(seqs_per_tile, dtype=f32)
    dec_tile = jnp.einsum("ab,hij->haibj", eye, decay).reshape(N_HEADS, SUBLANES, SUBLANES)
    tile_rows = lambda t: jnp.broadcast_to(jnp.tile(t, (1, seqs_per_tile))[:, :, None],
                                           (N_HEADS, SUBLANES, HEAD_D))
    cd = jnp.broadcast_to(c_decay[:, None, None], (N_HEADS, 1, HEAD_D))
    rows = SAMPLE_GROUP * ts
    first = tl.n_prompt // rows
    tok_spec = pl.BlockSpec((rows, D_RET), lambda i: (first + i, 0))
    const3 = lambda shape: pl.BlockSpec(shape, lambda i: (0, 0, 0))
    st_block = (SAMPLE_GROUP, N_HEADS, HEAD_D, HEAD_D)
    in_specs = [tok_spec] * 4 + [
        pl.BlockSpec((None,) + st_block, lambda i: (layer, i, 0, 0, 0)),
        const3((N_HEADS, SUBLANES, SUBLANES)),
        const3((N_HEADS, SUBLANES, HEAD_D)),
        const3((N_HEADS, SUBLANES, HEAD_D)),
        const3((N_HEADS, 1, HEAD_D)),
        pl.BlockSpec((None, 1, D_RET), lambda i: (layer, 0, 0)),
    ]
    args = [q, k, v, gate, state_ret, dec_tile, tile_rows(q_decay), tile_rows(k_decay), cd, g_ret_gn]
    if prev_states is None:
        state_spec = pl.BlockSpec((depth,) + st_block, lambda i: (0, i, 0, 0, 0))
        aliases = {}
    else:
        state_spec = pl.BlockSpec((1,) + st_block, lambda i: (layer, i, 0, 0, 0))
        in_specs.append(pl.BlockSpec(memory_space=pl.ANY))
        args.append(prev_states)
        aliases = {len(args) - 1: 1}
    return pl.pallas_call(
        functools.partial(_ret_sample_kernel, ts=ts),
        grid=(tl.bs // SAMPLE_GROUP,),
        in_specs=in_specs,
        out_specs=[pl.BlockSpec((rows, D_RET), lambda i: (i, 0)), state_spec],
        out_shape=[jax.ShapeDtypeStruct((tl.n_sample, D_RET), bf16),
                   jax.ShapeDtypeStruct((depth, tl.bs, N_HEADS, HEAD_D, HEAD_D), f32)],
        input_output_aliases=aliases,
        compiler_params=_cparams(1),
        name="retention_sample",
    )(*args)


def _ln_silu(cv, g, b):
    mu = jnp.mean(cv, axis=-1, keepdims=True)
    var = jnp.mean(jnp.square(cv - mu), axis=-1, keepdims=True)
    return jax.nn.silu((cv - mu) * lax.rsqrt(var + EPS) * g + b)


def _conv_taps(window, w_ref, b_ref, n_rows):
    cols = []
    for col in range(D_CONV // LANES):
        lanes = slice(col * LANES, (col + 1) * LANES)
        acc = jnp.broadcast_to(b_ref[:, lanes], (n_rows, LANES))
        for tap in range(CONV_WIDTH):
            acc = acc + window(col, tap) * w_ref[tap:tap + 1, lanes]
        cols.append(acc)
    return jnp.concatenate(cols, axis=-1)


def _conv_sample_kernel(a_ref, st_ref, w_ref, b_ref, g_ref, bl_ref, *rest, ts):
    o_ref, buf_all_ref, full_ref, cv_ref = rest[-4:]
    buf_ref = buf_all_ref.at[0]
    for other in range(1, buf_all_ref.shape[0]):
        buf_all_ref[other] = jnp.zeros(buf_all_ref.shape[1:], f32)
    for s in range(SAMPLE_GROUP):
        for col in range(D_CONV // LANES):
            lanes = slice(col * LANES, (col + 1) * LANES)
            full_ref[col, 0:CONV_HALO, :] = st_ref[:, s, lanes]
            full_ref[col, CONV_HALO:CONV_HALO + ts, :] = a_ref[s * ts:(s + 1) * ts, lanes]
            buf_ref[s, :, lanes] = full_ref[col, ts:ts + CONV_HALO, :]
        cv_ref[s * ts:(s + 1) * ts, :] = _conv_taps(lambda col, tap: full_ref[col, tap:tap + ts, :],
                                                    w_ref, b_ref, ts)
    o_ref[...] = _ln_silu(cv_ref[...], g_ref[...], bl_ref[...]).astype(bf16)


def _conv_sample(tl, layer, a, state_conv, w_conv, b_conv, g_ln, b_ln, prev_bufs):
    ts = tl.ts
    depth = state_conv.shape[0]
    rows = SAMPLE_GROUP * ts
    first_block = tl.n_prompt // rows
    vec = pl.BlockSpec((None, 1, D_CONV), lambda i: (layer, 0, 0))
    in_specs = [pl.BlockSpec((rows, D_CONV), lambda i: (first_block + i, 0)),
                pl.BlockSpec((None, CONV_HALO, SAMPLE_GROUP, D_CONV), lambda i: (layer, 0, i, 0)),
                pl.BlockSpec((None, CONV_WIDTH, D_CONV), lambda i: (layer, 0, 0)),
                vec, vec, vec]
    args = [a, state_conv.transpose(0, 2, 1, 3), w_conv, b_conv, g_ln, b_ln]
    buf_block = (SAMPLE_GROUP, CONV_HALO, D_CONV)
    if prev_bufs is None:
        buf_spec = pl.BlockSpec((depth,) + buf_block, lambda i: (0, i, 0, 0))
        aliases = {}
    else:
        buf_spec = pl.BlockSpec((1,) + buf_block, lambda i: (layer, i, 0, 0))
        in_specs.append(pl.BlockSpec(memory_space=pl.ANY))
        args.append(prev_bufs)
        aliases = {len(args) - 1: 1}
    return pl.pallas_call(
        functools.partial(_conv_sample_kernel, ts=ts),
        grid=(tl.bs // SAMPLE_GROUP,),
        in_specs=in_specs,
        out_specs=[pl.BlockSpec((rows, D_CONV), lambda i: (i, 0)), buf_spec],
        out_shape=[jax.ShapeDtypeStruct((tl.n_sample, D_CONV), bf16),
                   jax.ShapeDtypeStruct((depth, tl.bs, CONV_HALO, D_CONV), f32)],
        input_output_aliases=aliases,
        scratch_shapes=[pltpu.VMEM((D_CONV // LANES, CONV_HALO + ts + SUBLANES, LANES), f32),
                        pltpu.VMEM((rows, D_CONV), f32)],
        compiler_params=_cparams(1),
        name="conv_sample",
    )(*args)


N_CONV_COLS = D_CONV // LANES
CONV_TILE = TS
N_CONV_CHUNKS = CONV_TILE // CONV_ROWS


def _inproj_kernel(*refs, np_steps, split_x):
    if split_x:
        xp_ref, xs_ref = refs[:2]
        refs = refs[2:]
    else:
        x_ref = refs[0]
        refs = refs[1:]
    (shs_ref, sht_ref, scs_ref, sct_ref, g_ref, w_ref, cos_ref, sin_ref,
     q_ref, k_ref, v_ref, gate_ref, a_ref) = refs
    is_s = pl.program_id(0) >= np_steps

    def step(sample):
        for t in range(STEP_TILES):
            rs = slice(t * TM, (t + 1) * TM)
            if split_x:
                x = xs_ref[rs, :] if sample else xp_ref[rs, :]
            else:
                x = x_ref[rs, :]
            sh, sc = (sht_ref[rs, :], sct_ref[rs, :]) if sample else (shs_ref[...], scs_ref[...])
            hb = (_rms(x, g_ref[...]) * (1.0 + sc) + sh).astype(bf16)

            def group(g):
                return jnp.dot(hb, w_ref[:, g * D_RET:(g + 1) * D_RET], preferred_element_type=f32)

            cos = cos_ref[rs, :]
            sin = sin_ref[rs, :]

            def rope(th):
                return th * cos + pltpu.roll(th, HEAD_D // 2, 1) * sin

            qg, kg = group(0), group(1)
            for hd in range(N_HEADS):
                sl = slice(hd * HEAD_D, (hd + 1) * HEAD_D)
                q_ref[rs, sl] = rope(qg[:, sl]).astype(bf16)
                k_ref[rs, sl] = rope(kg[:, sl]) * (HEAD_D ** -0.5)
            v_ref[rs, :] = group(2).astype(bf16)
            gate_ref[rs, :] = group(3)
            a_ref[rs, :] = group(4) * jax.nn.sigmoid(group(5))

    pl.when(is_s)(functools.partial(step, True))
    pl.when(jnp.logical_not(is_s))(functools.partial(step, False))


def _conv_prompt_kernel(a_ref, w_ref, b_ref, g_ref, bl_ref, o_ref, buf_ref, full_ref, cv_ref):
    j = pl.program_id(1)

    @pl.when(j == 0)
    def _():
        full_ref[:, 0:HALO_PAD, :] = jnp.zeros((N_CONV_COLS, HALO_PAD, LANES), f32)

    @pl.when(j > 0)
    def _():
        full_ref[:, 0:HALO_PAD, :] = full_ref[:, CONV_TILE:CONV_TILE + HALO_PAD, :]

    for col in range(N_CONV_COLS):
        full_ref[col, HALO_PAD:HALO_PAD + CONV_TILE, :] = a_ref[:, col * LANES:(col + 1) * LANES]
    shift = HALO_PAD - CONV_HALO

    def taps(idx, carry):
        col = idx // N_CONV_CHUNKS
        r0 = pl.multiple_of((idx % N_CONV_CHUNKS) * CONV_ROWS, CONV_ROWS)
        acc = jnp.broadcast_to(b_ref[col], (CONV_ROWS, LANES))
        for tap in range(CONV_WIDTH):
            acc = acc + full_ref[col, pl.ds(r0 + (tap + shift), CONV_ROWS), :] * w_ref[col, tap:tap + 1, :]
        cv_ref[col, pl.ds(r0, CONV_ROWS), :] = acc
        return carry

    lax.fori_loop(0, N_CONV_COLS * N_CONV_CHUNKS, taps, 0)

    for r0 in range(0, CONV_TILE, NORM_ROWS):
        cv = jnp.concatenate([cv_ref[col, r0:r0 + NORM_ROWS, :] for col in range(N_CONV_COLS)], axis=-1)
        o_ref[r0:r0 + NORM_ROWS, :] = _ln_silu(cv, g_ref[...], bl_ref[...]).astype(bf16)

    @pl.when(j == pl.num_programs(1) - 1)
    def _():
        buf_ref[...] = a_ref[CONV_TILE - CONV_HALO:CONV_TILE, :]


def _conv_prompt(tl, layer, a, w_conv, b_conv, g_ln, b_ln):
    assert tl.tp % CONV_TILE == 0
    tps = tl.tp // CONV_TILE
    depth = w_conv.shape[0]
    w_cols = w_conv.reshape(depth, CONV_WIDTH, N_CONV_COLS, LANES).transpose(0, 2, 1, 3)
    b_cols = b_conv.reshape(depth, N_CONV_COLS, 1, LANES)
    vec = pl.BlockSpec((None, 1, D_CONV), lambda b, j: (layer, 0, 0))
    return pl.pallas_call(
        _conv_prompt_kernel,
        grid=(tl.bp, tps),
        in_specs=[pl.BlockSpec((CONV_TILE, D_CONV), lambda b, j: (b * tps + j, 0)),
                  pl.BlockSpec((None, N_CONV_COLS, CONV_WIDTH, LANES), lambda b, j: (layer, 0, 0, 0)),
                  pl.BlockSpec((None, N_CONV_COLS, 1, LANES), lambda b, j: (layer, 0, 0, 0)),
                  vec, vec],
        out_specs=[pl.BlockSpec((CONV_TILE, D_CONV), lambda b, j: (b * tps + j, 0)),
                   pl.BlockSpec((None, CONV_HALO, D_CONV), lambda b, j: (b, 0, 0))],
        out_shape=[jax.ShapeDtypeStruct((tl.n_prompt, D_CONV), bf16),
                   jax.ShapeDtypeStruct((tl.bp, CONV_HALO, D_CONV), f32)],
        scratch_shapes=[pltpu.VMEM((N_CONV_COLS, HALO_PAD + CONV_TILE, LANES), f32),
                        pltpu.VMEM((N_CONV_COLS, CONV_TILE, LANES), f32)],
        compiler_params=_cparams(2),
        name="conv_prompt",
    )(a, w_cols, b_cols, g_ln, b_ln)


def _inproj(tl, layer, x, mod_seq, mod_tok, g_norm, w_in_bf, cos_tab, sin_tab):
    split_x = isinstance(x, tuple)
    tok_spec = pl.BlockSpec((TS, D_MODEL), lambda i: (i, 0))
    if split_x:
        x_args = list(x)
        x_specs = [pl.BlockSpec((TS, D_MODEL), lambda i: (tl.prompt_block(i), 0)),
                   pl.BlockSpec((TS, D_MODEL), lambda i: (tl.sample_block(i), 0))]
    else:
        x_args, x_specs = [x], [tok_spec]
    sh_seq, sh_tok = _mod_specs(tl, layer, 0)
    sc_seq, sc_tok = _mod_specs(tl, layer, 1)

    def table_block(i):
        return jnp.where(i < tl.np_steps, i % tl.steps_per_seq, tl.steps_per_seq + tl.sample_block(i))

    tab_spec = pl.BlockSpec((TS, HEAD_D), lambda i: (table_block(i), 0))
    row_spec = pl.BlockSpec((TS, D_RET), lambda i: (i, 0))
    row_sd = lambda dt: jax.ShapeDtypeStruct((tl.n_tok, D_RET), dt)
    return pl.pallas_call(
        functools.partial(_inproj_kernel, np_steps=tl.np_steps, split_x=split_x),
        grid=(tl.n_steps,),
        in_specs=x_specs + [
            sh_seq, sh_tok, sc_seq, sc_tok,
            pl.BlockSpec((None, 1, D_MODEL), lambda i: (layer, 0, 0)),
            pl.BlockSpec((None, D_MODEL, D_IN), lambda i: (layer, 0, 0)),
            tab_spec, tab_spec,
        ],
        out_specs=[row_spec] * 5,
        out_shape=[row_sd(bf16), row_sd(f32), row_sd(bf16), row_sd(f32), row_sd(f32)],
        compiler_params=_cparams(1),
        name="inproj",
    )(*x_args, mod_seq, mod_tok, mod_seq, mod_tok, g_norm, w_in_bf, cos_tab, sin_tab)


SORT_ROWS = 2 * TM + N_EXPERTS * SUBLANES
XS_HALF = D_MODEL // 2
XS_W = XS_HALF + LANES
u32 = jnp.uint32
ROUTE_POS, ROUTE_W0, ROUTE_W1 = (0, 2), (2, 5), (5, 8)


def _pack_bf16_pair(x):
    lo = lax.shift_right_logical(lax.bitcast_convert_type(x[:, 0:XS_HALF], u32), u32(16))
    hi = lax.bitcast_convert_type(x[:, XS_HALF:D_MODEL], u32) & u32(0xFFFF0000)
    return hi | lo


def _unpack_bf16_pair(words):
    lo = lax.bitcast_convert_type(lax.shift_left(words, u32(16)), f32).astype(bf16)
    hi = lax.bitcast_convert_type(words & u32(0xFFFF0000), f32).astype(bf16)
    return lo, hi


def _split3(x):
    a = x.astype(bf16)
    r = x - a.astype(f32)
    b = r.astype(bf16)
    c = (r - b.astype(f32)).astype(bf16)
    return a, b, c


def _first_of4(vals, m):
    return jnp.where(vals[0] == m, 0.0, jnp.where(vals[1] == m, 1.0, jnp.where(vals[2] == m, 2.0, 3.0)))


def _rows_to_tile(rows, n_rows):
    sub = lax.broadcasted_iota(i32, (n_rows, TM), 0)
    out = jnp.zeros((n_rows, TM), f32)
    for r, val in enumerate(rows):
        out = jnp.where(sub == r, val, out)
    return out


def _outproj_kernel(*refs, np_steps, split_x):
    if split_x:
        xp_ref, xs_ref = refs[:2]
        refs = refs[2:]
    else:
        x_ref = refs[0]
        refs = refs[1:]
    (retp_ref, rets_ref, cvp_ref, cvs_ref, gts_ref, gtt_ref, shs_ref, sht_ref, scs_ref, sct_ref,
     g_ref, wo_ref, wr_ref, br_ref,
     xo_ref, h2_ref, rows_ref, cols_ref, cnt_ref) = refs
    is_s = pl.program_id(0) >= np_steps

    def step(sample):
        for t in range(STEP_TILES):
            rs = slice(t * TM, (t + 1) * TM)
            if split_x:
                x = xs_ref[rs, :] if sample else xp_ref[rs, :]
            else:
                x = x_ref[rs, :]
            ret = rets_ref[rs, :] if sample else retp_ref[rs, :]
            cv = cvs_ref[rs, :] if sample else cvp_ref[rs, :]
            gt, sh, sc = ((gtt_ref[rs, :], sht_ref[rs, :], sct_ref[rs, :]) if sample
                          else (gts_ref[...], shs_ref[...], scs_ref[...]))
            mix_out = (jnp.dot(ret, wo_ref[0:D_RET, :], preferred_element_type=f32)
                       + jnp.dot(cv, wo_ref[D_RET:D_RET + D_CONV, :], preferred_element_type=f32))
            xn = x + gt * mix_out
            xo_ref[rs, :] = xn
            h2 = _rms(xn, g_ref[...]) * (1.0 + sc) + sh
            h_hi = h2.astype(bf16)
            h2_ref[rs, :] = h_hi
            rows, cols, cnt = _route_tile(h_hi, wr_ref, br_ref)
            rows_ref[t] = rows
            cols_ref[rs, :] = cols
            cnt_ref[t] = cnt

    pl.when(is_s)(functools.partial(step, True))
    pl.when(jnp.logical_not(is_s))(functools.partial(step, False))


def _route_tile(h_bf, wr_ref, br_ref):
    logits = jnp.dot(h_bf, wr_ref[...], preferred_element_type=f32) + br_ref[...]
    lt = logits.T
    row = [lt[e:e + 1, :] for e in range(N_EXPERTS)]
    top = functools.reduce(jnp.maximum, row)
    ex = [jnp.exp(r - top) for r in row]
    den = functools.reduce(jnp.add, ex)
    p = [v / den for v in ex]

    best = None
    for g in range(N_GROUPS):
        a = p[g * GROUP_SIZE:(g + 1) * GROUP_SIZE]
        m1 = functools.reduce(jnp.maximum, a)
        i1 = _first_of4(a, m1)
        b = [jnp.where(i1 == float(j), -1.0, a[j]) for j in range(GROUP_SIZE)]
        m2 = functools.reduce(jnp.maximum, b)
        i2 = _first_of4(b, m2)
        cand = (m1 + m2, m1, m2, i1 + float(g * GROUP_SIZE), i2 + float(g * GROUP_SIZE))
        if best is None:
            best = cand
        else:
            take = cand[0] > best[0]
            best = tuple(jnp.where(take, c, o) for c, o in zip(cand, best))
    _, m1, m2, e0, e1 = best
    denom = m1 + m2
    w0 = m1 / denom
    w1 = m2 / denom

    ex_id = lax.broadcasted_iota(i32, (N_EXPERTS, TM), 0).astype(f32)
    sel0 = ex_id == e0
    sel1 = ex_id == e1
    ind = jnp.where(sel0 | sel1, 1.0, 0.0)
    t_r = lax.broadcasted_iota(i32, (TM, TM), 0)
    t_c = lax.broadcasted_iota(i32, (TM, TM), 1)
    earlier = jnp.where(t_r < t_c, 1.0, 0.0).astype(bf16)
    prefix = jnp.dot(ind.astype(bf16), earlier, preferred_element_type=f32)
    cnt = jnp.sum(ind, axis=-1, keepdims=True)
    cnt8 = jnp.floor((cnt + float(SUBLANES - 1)) * (1.0 / SUBLANES)) * float(SUBLANES)
    e_r = lax.broadcasted_iota(i32, (N_EXPERTS, N_EXPERTS), 0)
    e_c = lax.broadcasted_iota(i32, (N_EXPERTS, N_EXPERTS), 1)
    below = jnp.where(e_c < e_r, 1.0, 0.0).astype(bf16)
    seg_off = jnp.dot(below, jnp.broadcast_to(cnt8, (N_EXPERTS, TM)).astype(bf16),
                      preferred_element_type=f32)
    where_to = seg_off + prefix
    pos0 = jnp.sum(jnp.where(sel0, where_to, 0.0), axis=0, keepdims=True)
    pos1 = jnp.sum(jnp.where(sel1, where_to, 0.0), axis=0, keepdims=True)

    w0p = [v.astype(f32) for v in _split3(w0)]
    w1p = [v.astype(f32) for v in _split3(w1)]
    info = [pos0, pos1] + w0p + w1p
    return (_rows_to_tile(info, SUBLANES), _rows_to_tile(info, LANES).T,
            jnp.broadcast_to(cnt, (N_EXPERTS, LANES)))


def _outproj(tl, layer, x, ret_p, ret_s, cv_p, cv_s, mod_seq, mod_tok, g_norm, w_out_bf, wr_bf, br_pad):
    split_x = isinstance(x, tuple)
    tok_spec = pl.BlockSpec((TS, D_MODEL), lambda i: (i, 0))
    p_spec = lambda w: pl.BlockSpec((TS, w), lambda i: (tl.prompt_block(i), 0))
    s_spec = lambda w: pl.BlockSpec((TS, w), lambda i: (tl.sample_block(i), 0))
    if split_x:
        x_args, x_specs = list(x), [p_spec(D_MODEL), s_spec(D_MODEL)]
    else:
        x_args, x_specs = [x], [tok_spec]
    mods = []
    for col in (2, 3, 4):
        mods += list(_mod_specs(tl, layer, col))
    wr_spec = pl.BlockSpec((D_MODEL, LANES), lambda i: (0, 0))
    return pl.pallas_call(
        functools.partial(_outproj_kernel, np_steps=tl.np_steps, split_x=split_x),
        grid=(tl.n_steps,),
        in_specs=x_specs + [p_spec(D_RET), s_spec(D_RET), p_spec(D_CONV), s_spec(D_CONV)] + mods + [
            pl.BlockSpec((None, 1, D_MODEL), lambda i: (layer, 0, 0)),
            pl.BlockSpec((None, D_MODEL, D_MODEL), lambda i: (layer, 0, 0)),
            wr_spec,
            pl.BlockSpec((1, LANES), lambda i: (0, 0)),
        ],
        out_specs=[tok_spec, tok_spec,
                   pl.BlockSpec((STEP_TILES, SUBLANES, TM), lambda i: (i, 0, 0)),
                   pl.BlockSpec((TS, LANES), lambda i: (i, 0)),
                   pl.BlockSpec((STEP_TILES, N_EXPERTS, LANES), lambda i: (i, 0, 0))],
        out_shape=[jax.ShapeDtypeStruct((tl.n_tok, D_MODEL), f32),
                   jax.ShapeDtypeStruct((tl.n_tok, D_MODEL), bf16),
                   jax.ShapeDtypeStruct((tl.n_tiles, SUBLANES, TM), f32),
                   jax.ShapeDtypeStruct((tl.n_tok, LANES), f32),
                   jax.ShapeDtypeStruct((tl.n_tiles, N_EXPERTS, LANES), f32)],
        compiler_params=_cparams(1),
        name="outproj_router",
    )(*x_args, ret_p, ret_s, cv_p, cv_s, *([mod_seq, mod_tok] * 3), g_norm, w_out_bf, wr_bf, br_pad)


N_CHUNKS = SORT_ROWS // SUBLANES


class _Layout:
    def __init__(self, n_tiles):
        self.n_tiles = n_tiles
        self.tail_start = n_tiles * N_CHUNKS
        self.tail_n8 = self.tail_start + N_EXPERTS
        worst = 2 * n_tiles * TM + n_tiles * N_EXPERTS * (SUBLANES - 1) + N_EXPERTS * (BM - SUBLANES)
        self.n_blocks = -(-worst // BM)
        self.cap = self.n_blocks * BM
        self.dump = self.cap
        self.xs_rows = self.cap + -(-2 * STEP_TILES * SORT_ROWS // BM) * BM


def _moe_tables(lay, tile_counts):
    c8 = ((tile_counts.astype(i32) + SUBLANES - 1) // SUBLANES) * SUBLANES
    base8 = jnp.cumsum(c8, axis=0) - c8
    tot8 = jnp.sum(c8, axis=0)
    region = ((tot8 + BM - 1) // BM) * BM
    g_end = jnp.cumsum(region)
    g_start = g_end - region
    seg_end = jnp.cumsum(c8, axis=1)
    seg_dst = g_start[None, :] + base8
    n_used = g_end[-1] // BM
    blk = jnp.arange(lay.n_blocks, dtype=i32)
    block_e = jnp.minimum(jnp.sum((g_end[None, :] <= blk[:, None] * BM).astype(i32), axis=1), N_EXPERTS - 1)
    block_e = jnp.where(blk < n_used, block_e, block_e[n_used - 1])
    row0 = jnp.arange(N_CHUNKS, dtype=i32) * SUBLANES
    owner = jnp.sum((seg_end[:, None, :] <= row0[None, :, None]).astype(i32), axis=-1)
    onehot = (owner[:, :, None] == jnp.arange(N_EXPERTS, dtype=i32)[None, None, :]).astype(i32)
    delta = seg_dst - (seg_end - c8)
    chunk_dst = jnp.where(owner < N_EXPERTS, row0[None, :] + jnp.sum(onehot * delta[:, None, :], axis=-1), -1)
    tab = jnp.concatenate([chunk_dst.ravel(), g_start + tot8, (region - tot8) // SUBLANES]).astype(i32)
    ids = jnp.arange(N_EXPERTS, dtype=i32)
    later = jnp.where((ids[None, :] > ids[:, None]) & (region[None, :] > 0), ids[None, :], N_EXPERTS)
    next_e = jnp.min(later, axis=1)
    next_e = jnp.where(next_e == N_EXPERTS, -1, next_e)
    owner_end = jnp.sum((block_e[:, None] == ids[None, :]).astype(i32) * (g_start + tot8)[None, :], axis=1)
    rows_used = jnp.clip(owner_end - blk * BM, 0, BM)
    return tab, jnp.concatenate([block_e, n_used[None], next_e, rows_used]).astype(i32)


def _for_chunks(n, fn):
    def body(c, carry):
        fn(c)
        return carry

    lax.fori_loop(0, n, body, 0)


def _dispatch_kernel(tab_ref, h2_ref, rows_ref, cols_ref, xs_hbm, sorted_ref, zero_ref, sem, *, lay, n_steps):
    i = pl.program_id(0)
    slot = i % 2
    step_rows = STEP_TILES * SORT_ROWS

    def tail_copy(dst):
        return pltpu.make_async_copy(zero_ref, xs_hbm.at[pl.ds(dst, SUBLANES)], sem.at[2])

    def wait_step(slot_):
        pltpu.make_async_copy(sorted_ref.at[slot_], xs_hbm.at[pl.ds(0, step_rows)], sem.at[slot_]).wait()

    @pl.when(i == 0)
    def _():
        zero_ref[...] = jnp.zeros_like(zero_ref)
        for e in range(N_EXPERTS):
            start = tab_ref[lay.tail_start + e]
            _for_chunks(tab_ref[lay.tail_n8 + e],
                        lambda c: tail_copy(pl.multiple_of(start + c * SUBLANES, SUBLANES)).start())

    @pl.when(i >= 2)
    def _():
        wait_step(slot)

    r_id = lax.broadcasted_iota(i32, (SORT_ROWS, TM), 0).astype(f32)
    lane = lax.broadcasted_iota(i32, (TM, LANES), 1)
    for t in range(STEP_TILES):
        pos0 = rows_ref[t, 0:1, :]
        pos1 = rows_ref[t, 1:2, :]
        p0 = r_id == pos0
        p1 = r_id == pos1
        perm = jnp.where(p0 | p1, 1.0, 0.0).astype(bf16)
        cols = cols_ref[t * TM:(t + 1) * TM, :]
        wpart0 = jnp.where((lane >= ROUTE_W0[0]) & (lane < ROUTE_W0[1]), cols, 0.0).astype(bf16)
        wpart1 = jnp.where((lane >= ROUTE_W1[0]) & (lane < ROUTE_W1[1]), cols, 0.0).astype(bf16)
        sw = (jnp.dot(jnp.where(p0, 1.0, 0.0).astype(bf16), wpart0, preferred_element_type=f32)
              + jnp.dot(jnp.where(p1, 1.0, 0.0).astype(bf16), wpart1, preferred_element_type=f32))
        base = t * SORT_ROWS
        sorted_ref[slot, base:base + SORT_ROWS, 0:XS_HALF] = _pack_bf16_pair(jnp.dot(
            perm, h2_ref[t * TM:(t + 1) * TM, :], preferred_element_type=f32))
        sorted_ref[slot, base:base + SORT_ROWS, XS_HALF:XS_W] = lax.bitcast_convert_type(jnp.broadcast_to(
            jnp.sum(sw, axis=-1, keepdims=True), (SORT_ROWS, LANES)), u32)

    for t in range(STEP_TILES):
        for c in range(N_CHUNKS):
            row = t * SORT_ROWS + c * SUBLANES
            dst = tab_ref[(i * STEP_TILES + t) * N_CHUNKS + c]
            dst = jnp.where(dst < 0, lay.dump + slot * step_rows + row, dst)
            pltpu.make_async_copy(sorted_ref.at[slot, pl.ds(row, SUBLANES)],
                                  xs_hbm.at[pl.ds(pl.multiple_of(dst, SUBLANES), SUBLANES)], sem.at[slot]).start()

    @pl.when(i == n_steps - 1)
    def _():
        if n_steps >= 2:
            wait_step(1 - slot)
        wait_step(slot)
        for e in range(N_EXPERTS):
            _for_chunks(tab_ref[lay.tail_n8 + e], lambda c: tail_copy(0).wait())


def _dispatch(tl, lay, tab, h2, rows, cols):
    grid_spec = pltpu.PrefetchScalarGridSpec(
        num_scalar_prefetch=1,
        grid=(tl.n_steps,),
        in_specs=[pl.BlockSpec((TS, D_MODEL), lambda i, t: (i, 0)),
                  pl.BlockSpec((STEP_TILES, SUBLANES, TM), lambda i, t: (i, 0, 0)),
                  pl.BlockSpec((TS, LANES), lambda i, t: (i, 0))],
        out_specs=pl.BlockSpec(memory_space=pl.ANY),
        scratch_shapes=[pltpu.VMEM((2, STEP_TILES * SORT_ROWS, XS_W), u32), pltpu.VMEM((SUBLANES, XS_W), u32),
                        pltpu.SemaphoreType.DMA((3,))],
    )
    return pl.pallas_call(
        functools.partial(_dispatch_kernel, lay=lay, n_steps=tl.n_steps),
        grid_spec=grid_spec,
        out_shape=jax.ShapeDtypeStruct((lay.xs_rows, XS_W), u32),
        compiler_params=_cparams(1),
        name="moe_dispatch",
    )(tab, h2, rows, cols)


def _expert_kernel(be_ref, xs_ref, wg_hbm, wu_hbm, wd_hbm, ys_ref, stage, w_bf, sem, *, n_blocks, layer):
    j = pl.program_id(0)

    def fetch(e):
        return [pltpu.make_async_copy(w.at[layer, e], stage.at[k], sem.at[k])
                for k, w in enumerate((wg_hbm, wu_hbm, wd_hbm))]

    @pl.when(j < be_ref[n_blocks])
    def _():
        e = be_ref[j]

        @pl.when(j == 0)
        def _():
            for copy in fetch(e):
                copy.start()

        @pl.when((j == 0) | (e != be_ref[jnp.maximum(j - 1, 0)]))
        def _():
            for copy in fetch(e):
                copy.wait()
            for k in range(3):
                w_bf[k] = stage[k].astype(bf16)
            nxt = be_ref[n_blocks + 1 + e]

            @pl.when(nxt >= 0)
            def _():
                for copy in fetch(nxt):
                    copy.start()

        def ffn(rows):
            x_lo, x_hi = _unpack_bf16_pair(xs_ref[rows, 0:XS_HALF])

            def first_layer(k):
                return (jnp.dot(x_lo, w_bf[k, 0:XS_HALF, :], preferred_element_type=f32)
                        + jnp.dot(x_hi, w_bf[k, XS_HALF:D_MODEL, :], preferred_element_type=f32))

            mid = (jax.nn.silu(first_layer(0)) * first_layer(1)).astype(bf16)
            slot_w = lax.bitcast_convert_type(xs_ref[rows, XS_HALF:XS_HALF + 1], f32)
            ys_ref[rows, :] = jnp.dot(mid, w_bf[2], preferred_element_type=f32) * slot_w

        rows_used = be_ref[n_blocks + 1 + N_EXPERTS + j]
        for parts in range(1, BM // BM_PART + 1):
            top = parts * BM_PART

            @pl.when((rows_used > top - BM_PART) & (rows_used <= top))
            def _(top=top):
                ffn(slice(0, top))
                if top < BM:
                    ys_ref[top:BM, :] = jnp.zeros((BM - top, D_MODEL), f32)


def _experts(layer, lay, block_e, xs, w_gate, w_up, w_down):
    n_blocks = lay.n_blocks
    d_ff = w_gate.shape[-1]
    assert d_ff == D_MODEL
    used = lambda j, be: jnp.minimum(j, be[n_blocks] - 1)
    any_spec = pl.BlockSpec(memory_space=pl.ANY)
    grid_spec = pltpu.PrefetchScalarGridSpec(
        num_scalar_prefetch=1,
        grid=(n_blocks,),
        in_specs=[pl.BlockSpec((BM, XS_W), lambda j, be: (used(j, be), 0)), any_spec, any_spec, any_spec],
        out_specs=pl.BlockSpec((BM, D_MODEL), lambda j, be: (used(j, be), 0)),
        scratch_shapes=[pltpu.VMEM((3, D_MODEL, d_ff), f32), pltpu.VMEM((3, D_MODEL, d_ff), bf16),
                        pltpu.SemaphoreType.DMA((3,))],
    )
    return pl.pallas_call(
        functools.partial(_expert_kernel, n_blocks=n_blocks, layer=layer),
        grid_spec=grid_spec,
        out_shape=jax.ShapeDtypeStruct((lay.cap, D_MODEL), f32),
        compiler_params=_cparams(1),
        name="moe_experts",
    )(block_e, xs, w_gate, w_up, w_down)


def _combine_kernel(tab_ref, ys_hbm, cols_ref, x_ref, gts_ref, gtt_ref, *rest, n_steps, np_steps, final):
    if final:
        gf_ref, yp_ref, ysm_ref, staged, sem = rest
    else:
        xo_ref, staged, sem = rest
    i = pl.program_id(0)
    slot = i % 2
    is_s = i >= np_steps

    def start_step(step, slot_):
        for c in range(STEP_TILES * N_CHUNKS):
            src = jnp.maximum(tab_ref[step * (STEP_TILES * N_CHUNKS) + c], 0)
            pltpu.make_async_copy(ys_hbm.at[pl.ds(pl.multiple_of(src, SUBLANES), SUBLANES)],
                                  staged.at[slot_, pl.ds(c * SUBLANES, SUBLANES)], sem.at[slot_]).start()

    @pl.when(i == 0)
    def _():
        start_step(0, 0)

    @pl.when(i + 1 < n_steps)
    def _():
        start_step(i + 1, 1 - slot)

    pltpu.make_async_copy(ys_hbm.at[pl.ds(0, STEP_TILES * SORT_ROWS)], staged.at[slot], sem.at[slot]).wait()

    def step(sample):
        lane = lax.broadcasted_iota(i32, (TM, SORT_ROWS), 1).astype(f32)
        for t in range(STEP_TILES):
            rs = slice(t * TM, (t + 1) * TM)
            unperm = jnp.where((lane == cols_ref[rs, 0:1]) | (lane == cols_ref[rs, 1:2]), 1.0, 0.0).astype(bf16)
            parts = _split3(staged[slot, t * SORT_ROWS:(t + 1) * SORT_ROWS, :])
            ff = sum(jnp.dot(unperm, part, preferred_element_type=f32) for part in parts)
            xn = x_ref[rs, :] + (gtt_ref[rs, :] if sample else gts_ref[...]) * ff
            if final:
                (ysm_ref if sample else yp_ref)[rs, :] = _rms(xn, gf_ref[...])
            else:
                xo_ref[rs, :] = xn

    pl.when(is_s)(functools.partial(step, True))
    pl.when(jnp.logical_not(is_s))(functools.partial(step, False))


def _combine(tl, layer, tab, ys, cols, x, mod_seq, mod_tok, g_final):
    final = g_final is not None
    tok_spec = pl.BlockSpec((TS, D_MODEL), lambda i, t: (i, 0))
    gt_seq, gt_tok = _mod_specs(tl, layer, 5)
    in_specs = [pl.BlockSpec(memory_space=pl.ANY), pl.BlockSpec((TS, LANES), lambda i, t: (i, 0)),
                tok_spec, gt_seq, gt_tok]
    args = [tab, ys, cols, x, mod_seq, mod_tok]
    if final:
        in_specs.append(pl.BlockSpec((1, D_MODEL), lambda i, t: (0, 0)))
        args.append(g_final)
        out_specs = [pl.BlockSpec((TS, D_MODEL), lambda i, t: (tl.prompt_block(i), 0)),
                     pl.BlockSpec((TS, D_MODEL), lambda i, t: (tl.sample_block(i), 0))]
        out_shape = [jax.ShapeDtypeStruct((tl.n_prompt, D_MODEL), f32),
                     jax.ShapeDtypeStruct((tl.n_sample, D_MODEL), f32)]
    else:
        out_specs = tok_spec
        out_shape = jax.ShapeDtypeStruct((tl.n_tok, D_MODEL), f32)
    grid_spec = pltpu.PrefetchScalarGridSpec(
        num_scalar_prefetch=1,
        grid=(tl.n_steps,),
        in_specs=in_specs,
        out_specs=out_specs,
        scratch_shapes=[pltpu.VMEM((2, STEP_TILES * SORT_ROWS, D_MODEL), f32), pltpu.SemaphoreType.DMA((2,))],
    )
    return pl.pallas_call(
        functools.partial(_combine_kernel, n_steps=tl.n_steps, np_steps=tl.np_steps, final=final),
        grid_spec=grid_spec,
        out_shape=out_shape,
        compiler_params=_cparams(1),
        name="moe_combine",
    )(*args)


def _rope_tables(tl):
    half = HEAD_D // 2
    inv = ROPE_BASE ** (-jnp.arange(half, dtype=f32) / half)
    pos_p = jnp.arange(tl.tp, dtype=i32)
    pos_s = PAST_LEN + jnp.arange(tl.ts, dtype=i32)
    pos = jnp.concatenate([pos_p, jnp.tile(pos_s, tl.bs)])
    ang = pos.astype(f32)[:, None] * inv[None, :]
    cos, sin = jnp.cos(ang), jnp.sin(ang)
    return jnp.concatenate([cos, cos], axis=-1), jnp.concatenate([-sin, sin], axis=-1)


def kernel(x_prompt, x_sample, state_ret, state_conv, c_prompt, c_sample, w_mod, b_mod, g_mix_norm, w_in,
           w_conv, b_conv, g_conv_ln, b_conv_ln, g_ret_gn, w_out, g_ffn_norm, w_router, b_router,
           w_exp_gate, w_exp_up, w_exp_down, g_final):
    bp, tp, _ = x_prompt.shape
    bs, ts, _ = x_sample.shape
    depth = w_mod.shape[0]
    tl = _Tiles(bp, tp, bs, ts)
    lay = _Layout(tl.n_tiles)

    c_all = jnp.concatenate([c_prompt, jnp.repeat(c_sample, ts, axis=0)], axis=0)
    mod_seq, mod_tok = _modulation(c_all, bp, w_mod, b_mod)
    mod_seq = mod_seq.reshape(depth, bp, 1, N_MOD * D_MODEL)

    cos_tab, sin_tab = _rope_tables(tl)
    w_in_bf = w_in.astype(bf16)
    w_out_bf = w_out.astype(bf16)
    wr_pad = jnp.pad(w_router.astype(f32), ((0, 0), (0, LANES - N_EXPERTS)))
    wr_bf = wr_pad.astype(bf16)
    br_pad = jnp.pad(b_router.astype(f32), (0, LANES - N_EXPERTS)).reshape(1, LANES)
    vec3 = lambda t: t.reshape(depth, 1, t.shape[-1])
    g_mix3, g_ffn3, gn3 = vec3(g_mix_norm), vec3(g_ffn_norm), vec3(g_ret_gn)
    b_conv3, g_ln3, b_ln3 = vec3(b_conv), vec3(g_conv_ln), vec3(b_conv_ln)

    x = (x_prompt.reshape(tl.n_prompt, D_MODEL), x_sample.reshape(tl.n_sample, D_MODEL))
    ret_p, conv_p = [], []
    ret_s_all = conv_s_all = None
    for layer in range(depth):
        q, k, v, gate, a = _inproj(tl, layer, x, mod_seq, mod_tok, g_mix3, w_in_bf, cos_tab, sin_tab)
        ro_p, s_p = _retention_prompt(tl, layer, q, k, v, gate, gn3)
        ro_s, ret_s_all = _retention_sample(tl, layer, q, k, v, gate, state_ret, gn3, ret_s_all)
        co_p, buf_p = _conv_prompt(tl, layer, a, w_conv, b_conv3, g_ln3, b_ln3)
        co_s, conv_s_all = _conv_sample(tl, layer, a, state_conv, w_conv, b_conv3, g_ln3, b_ln3, conv_s_all)
        x_mid, h2, rows, cols, tile_counts = _outproj(
            tl, layer, x, ro_p, ro_s, co_p, co_s, mod_seq, mod_tok, g_ffn3, w_out_bf, wr_bf, br_pad)
        tab, block_e = _moe_tables(lay, tile_counts[:, :, 0])
        xs = _dispatch(tl, lay, tab, h2, rows, cols)
        ys = _experts(layer, lay, block_e, xs, w_exp_gate, w_exp_up, w_exp_down)
        last = layer == depth - 1
        x = _combine(tl, layer, tab, ys, cols, x_mid, mod_seq, mod_tok,
                     g_final.reshape(1, D_MODEL) if last else None)
        ret_p.append(s_p)
        conv_p.append(buf_p)
    y_p, y_s = x
    return (y_p.reshape(bp, tp, D_MODEL), y_s.reshape(bs, ts, D_MODEL),
            jnp.stack(ret_p), jnp.stack(conv_p), ret_s_all, conv_s_all)
```

```python
import functools

import jax
import jax.numpy as jnp
from jax import lax
from jax.experimental import pallas as pl
from jax.experimental.pallas import tpu as pltpu

f32 = jnp.float32
bf16 = jnp.bfloat16
i32 = jnp.int32

D_MODEL = 1024
D_RET = 512
D_CONV = 512
N_HEADS = 4
HEAD_D = 128
RET_CHUNK = 128
RET_CHUNKS_PER_STEP = 16
ROPE_BASE = 10000.0
CONV_WIDTH = 31
CONV_HALO = CONV_WIDTH - 1
N_EXPERTS = 16
N_GROUPS = 4
GROUP_SIZE = N_EXPERTS // N_GROUPS
N_MOD = 6
EPS = 1e-6
PAST_LEN = 16384
D_IN = 4 * D_RET + 2 * D_CONV

LANES = 128
SUBLANES = 8
TM = 256
STEP_TILES = 2
TS = TM * STEP_TILES
BM = 512
BM_PART = 128
CONV_ROWS = 256
NORM_ROWS = 64
SAMPLE_GROUP = 16
HALO_PAD = 32
V7X_VMEM_BYTES = 64 * 1024 * 1024
VMEM_LIMIT = V7X_VMEM_BYTES - 8 * 1024 * 1024


def _cparams(n_axes, vmem=VMEM_LIMIT):
    return pltpu.CompilerParams(dimension_semantics=("arbitrary",) * n_axes, vmem_limit_bytes=vmem)


def _mod_kernel(c_ref, w_ref, b_ref, seq_ref, tok_ref):
    cond = jax.nn.silu(c_ref[...]).astype(bf16)
    mod = jnp.dot(cond, w_ref[...].astype(bf16), preferred_element_type=f32) + b_ref[...]
    n_seq = seq_ref.shape[0]
    seq_ref[...] = mod[0:n_seq, :]
    tok_ref[...] = mod[n_seq:, :]


def _modulation(c_all, n_seq, w_mod, b_mod):
    depth = w_mod.shape[0]
    m = c_all.shape[0]
    n_tok = m - n_seq
    assert n_seq % SUBLANES == 0
    return pl.pallas_call(
        _mod_kernel,
        grid=(depth, N_MOD),
        in_specs=[
            pl.BlockSpec((m, D_MODEL), lambda l, j: (0, 0)),
            pl.BlockSpec((None, D_MODEL, D_MODEL), lambda l, j: (l, 0, j)),
            pl.BlockSpec((None, 1, D_MODEL), lambda l, j: (l, 0, j)),
        ],
        out_specs=[pl.BlockSpec((None, n_seq, D_MODEL), lambda l, j: (l, 0, j)),
                   pl.BlockSpec((None, n_tok, D_MODEL), lambda l, j: (l, 0, j))],
        out_shape=[jax.ShapeDtypeStruct((depth, n_seq, N_MOD * D_MODEL), f32),
                   jax.ShapeDtypeStruct((depth, n_tok, N_MOD * D_MODEL), f32)],
        compiler_params=_cparams(2),
        name="modulation",
    )(c_all, w_mod, b_mod.reshape(depth, 1, N_MOD * D_MODEL))


class _Tiles:
    def __init__(self, bp, tp, bs, ts):
        self.bp, self.tp, self.bs, self.ts = bp, tp, bs, ts
        self.n_prompt = bp * tp
        self.n_sample = bs * ts
        self.n_tok = self.n_prompt + self.n_sample
        assert tp % TS == 0 and self.n_sample % TS == 0
        self.tiles_per_seq = tp // TM
        self.np_tiles = self.n_prompt // TM
        self.n_tiles = self.n_tok // TM
        self.steps_per_seq = tp // TS
        self.np_steps = self.n_prompt // TS
        self.n_steps = self.n_tok // TS

    def prompt_block(self, i):
        return jnp.minimum(i, self.np_steps - 1)

    def sample_block(self, i):
        return jnp.maximum(i - self.np_steps, 0)

    def seq_index(self, i):
        return jnp.minimum(i // self.steps_per_seq, self.bp - 1)


def _mod_specs(tl, layer, col):
    seq = pl.BlockSpec((None, None, 1, D_MODEL), lambda i, *_: (layer, tl.seq_index(i), 0, col))
    tok = pl.BlockSpec((None, TS, D_MODEL), lambda i, *_: (layer, tl.sample_block(i), col))
    return seq, tok


def _rms(x, g):
    return x * lax.rsqrt(jnp.mean(x * x, axis=-1, keepdims=True) + EPS) * g


def _head_norm_gate(o, gn, gate):
    mu = jnp.mean(o, axis=-1, keepdims=True)
    var = jnp.mean(jnp.square(o - mu), axis=-1, keepdims=True)
    return jax.nn.silu(gate) * ((o - mu) * lax.rsqrt(var + EPS) * gn)


def _dot_nt(a, b):
    return lax.dot_general(a, b, (((1,), (1,)), ((), ())), preferred_element_type=f32)


def _dot_tn(a, b):
    return lax.dot_general(a, b, (((0,), (0,)), ((), ())), preferred_element_type=f32)


def _ret_prompt_kernel(q_ref, k_ref, v_ref, gate_ref, dec_ref, qd_ref, kd_ref, cd_ref, gn_ref,
                       o_ref, s_out_ref, s_ref):
    c = pl.program_id(1)

    @pl.when(c == 0)
    def _():
        s_ref[...] = jnp.zeros_like(s_ref)

    for ci in range(RET_CHUNKS_PER_STEP):
        rows = slice(ci * RET_CHUNK, (ci + 1) * RET_CHUNK)
        for hd in range(N_HEADS):
            sl = slice(hd * HEAD_D, (hd + 1) * HEAD_D)
            kh = k_ref[rows, sl]
            qb = q_ref[rows, sl]
            kb = kh.astype(bf16)
            vb = v_ref[rows, sl]
            s_old = s_ref[hd]
            scores = _dot_nt(qb, kb) * dec_ref[hd]
            inner = jnp.dot(scores.astype(bf16), vb, preferred_element_type=f32)
            cross = jnp.dot(qb, s_old.astype(bf16), preferred_element_type=f32) * qd_ref[hd]
            s_ref[hd] = s_old * cd_ref[hd] + _dot_tn((kh * kd_ref[hd]).astype(bf16), vb)
            o_ref[rows, sl] = _head_norm_gate(inner + cross, gn_ref[:, sl], gate_ref[rows, sl]).astype(bf16)

    @pl.when(c == pl.num_programs(1) - 1)
    def _():
        s_out_ref[...] = s_ref[...]


def _decay_tables(chunk, true_len):
    lg = jnp.log(1.0 - 2.0 ** (-5.0 - jnp.arange(N_HEADS, dtype=f32)))
    idx = jnp.arange(chunk, dtype=f32)
    rel = idx[:, None] - idx[None, :]
    decay = jnp.where(rel[None] >= 0, jnp.exp(jnp.maximum(rel, 0.0)[None] * lg[:, None, None]), 0.0)
    q_decay = jnp.exp((idx[None, :] + 1.0) * lg[:, None])
    k_decay = jnp.exp((true_len - 1.0 - idx[None, :]) * lg[:, None])
    c_decay = jnp.exp(true_len * lg)
    return decay, q_decay, k_decay, c_decay


def _retention_prompt(tl, layer, q, k, v, gate, g_ret_gn):
    step_rows = RET_CHUNK * RET_CHUNKS_PER_STEP
    assert tl.tp % step_rows == 0
    n_chunks = tl.tp // step_rows
    decay, q_decay, k_decay, c_decay = _decay_tables(RET_CHUNK, RET_CHUNK)
    bcast = lambda t: jnp.broadcast_to(t[:, :, None], (N_HEADS, RET_CHUNK, HEAD_D))
    cd = jnp.broadcast_to(c_decay[:, None, None], (N_HEADS, 1, HEAD_D))
    tok_spec = pl.BlockSpec((step_rows, D_RET), lambda b, c: (b * n_chunks + c, 0))
    tab_spec = pl.BlockSpec((N_HEADS, RET_CHUNK, HEAD_D), lambda b, c: (0, 0, 0))
    return pl.pallas_call(
        _ret_prompt_kernel,
        grid=(tl.bp, n_chunks),
        in_specs=[tok_spec] * 4 + [tab_spec] * 3 + [
            pl.BlockSpec((N_HEADS, 1, HEAD_D), lambda b, c: (0, 0, 0)),
            pl.BlockSpec((None, 1, D_RET), lambda b, c: (layer, 0, 0)),
        ],
        out_specs=[tok_spec, pl.BlockSpec((None, N_HEADS, HEAD_D, HEAD_D), lambda b, c: (b, 0, 0, 0))],
        out_shape=[jax.ShapeDtypeStruct((tl.n_prompt, D_RET), bf16),
                   jax.ShapeDtypeStruct((tl.bp, N_HEADS, HEAD_D, HEAD_D), f32)],
        scratch_shapes=[pltpu.VMEM((N_HEADS, HEAD_D, HEAD_D), f32)],
        compiler_params=_cparams(2),
        name="retention_prompt",
    )(q, k, v, gate, decay, bcast(q_decay), bcast(k_decay), cd, g_ret_gn)


def _ret_sample_kernel(q_ref, k_ref, v_ref, gate_ref, s_in_ref, dec_ref, qd_ref, kd_ref, cd_ref, gn_ref,
                       *rest, ts):
    o_ref, s_all_ref = rest[-2:]
    s_out_ref = s_all_ref.at[0]
    for other in range(1, s_all_ref.shape[0]):
        s_all_ref[other] = jnp.zeros(s_all_ref.shape[1:], f32)
    seqs_per_tile = SUBLANES // ts
    row = lax.broadcasted_iota(i32, (SUBLANES, HEAD_D), 0)
    q_all = q_ref[...].astype(f32)
    v_all = v_ref[...].astype(f32)
    outs = []
    for t in range(SAMPLE_GROUP // seqs_per_tile):
        rows = slice(t * SUBLANES, (t + 1) * SUBLANES)
        heads = []
        for hd in range(N_HEADS):
            sl = slice(hd * HEAD_D, (hd + 1) * HEAD_D)
            qh = q_all[rows, sl]
            kh = k_ref[rows, sl] * kd_ref[hd]
            vb = v_all[rows, sl].astype(bf16)
            qb = qh.astype(bf16)
            scores = _dot_nt(qb, k_ref[rows, sl].astype(bf16)) * dec_ref[hd]
            o = jnp.dot(scores.astype(bf16), vb, preferred_element_type=f32)
            for s in range(seqs_per_tile):
                b = t * seqs_per_tile + s
                mine = (row >= s * ts) & (row < (s + 1) * ts)
                s_old = s_in_ref[b, hd]
                q_s = jnp.where(mine, qh, 0.0).astype(bf16)
                k_s = jnp.where(mine, kh, 0.0).astype(bf16)
                o = o + jnp.dot(q_s, s_old.astype(bf16), preferred_element_type=f32) * qd_ref[hd]
                s_out_ref[b, hd] = s_old * cd_ref[hd] + _dot_tn(k_s, vb)
            heads.append(_head_norm_gate(o, gn_ref[:, sl], gate_ref[rows, sl]))
        outs.append(jnp.concatenate(heads, axis=-1))
    o_ref[...] = jnp.concatenate(outs, axis=0).astype(bf16)


def _retention_sample(tl, layer, q, k, v, gate, state_ret, g_ret_gn, prev_states):
    ts = tl.ts
    depth = state_ret.shape[0]
    assert SUBLANES % ts == 0 and tl.bs % SAMPLE_GROUP == 0
    seqs_per_tile = SUBLANES // ts
    decay, q_decay, k_decay, c_decay = _decay_tables(ts, ts)
    eye = jnp.eye(seqs_per_tile, dtype=f32)
    dec_tile = jnp.einsum("ab,hij->haibj", eye, decay).reshape(N_HEADS, SUBLANES, SUBLANES)
    tile_rows = lambda t: jnp.broadcast_to(jnp.tile(t, (1, seqs_per_tile))[:, :, None],
                                           (N_HEADS, SUBLANES, HEAD_D))
    cd = jnp.broadcast_to(c_decay[:, None, None], (N_HEADS, 1, HEAD_D))
    rows = SAMPLE_GROUP * ts
    first = tl.n_prompt // rows
    tok_spec = pl.BlockSpec((rows, D_RET), lambda i: (first + i, 0))
    const3 = lambda shape: pl.BlockSpec(shape, lambda i: (0, 0, 0))
    st_block = (SAMPLE_GROUP, N_HEADS, HEAD_D, HEAD_D)
    in_specs = [tok_spec] * 4 + [
        pl.BlockSpec((None,) + st_block, lambda i: (layer, i, 0, 0, 0)),
        const3((N_HEADS, SUBLANES, SUBLANES)),
        const3((N_HEADS, SUBLANES, HEAD_D)),
        const3((N_HEADS, SUBLANES, HEAD_D)),
        const3((N_HEADS, 1, HEAD_D)),
        pl.BlockSpec((None, 1, D_RET), lambda i: (layer, 0, 0)),
    ]
    args = [q, k, v, gate, state_ret, dec_tile, tile_rows(q_decay), tile_rows(k_decay), cd, g_ret_gn]
    if prev_states is None:
        state_spec = pl.BlockSpec((depth,) + st_block, lambda i: (0, i, 0, 0, 0))
        aliases = {}
    else:
        state_spec = pl.BlockSpec((1,) + st_block, lambda i: (layer, i, 0, 0, 0))
        in_specs.append(pl.BlockSpec(memory_space=pl.ANY))
        args.append(prev_states)
        aliases = {len(args) - 1: 1}
    return pl.pallas_call(
        functools.partial(_ret_sample_kernel, ts=ts),
        grid=(tl.bs // SAMPLE_GROUP,),
        in_specs=in_specs,
        out_specs=[pl.BlockSpec((rows, D_RET), lambda i: (i, 0)), state_spec],
        out_shape=[jax.ShapeDtypeStruct((tl.n_sample, D_RET), bf16),
                   jax.ShapeDtypeStruct((depth, tl.bs, N_HEADS, HEAD_D, HEAD_D), f32)],
        input_output_aliases=aliases,
        compiler_params=_cparams(1),
        name="retention_sample",
    )(*args)


def _ln_silu(cv, g, b):
    mu = jnp.mean(cv, axis=-1, keepdims=True)
    var = jnp.mean(jnp.square(cv - mu), axis=-1, keepdims=True)
    return jax.nn.silu((cv - mu) * lax.rsqrt(var + EPS) * g + b)


def _conv_taps(window, w_ref, b_ref, n_rows):
    cols = []
    for col in range(D_CONV // LANES):
        lanes = slice(col * LANES, (col + 1) * LANES)
        acc = jnp.broadcast_to(b_ref[:, lanes], (n_rows, LANES))
        for tap in range(CONV_WIDTH):
            acc = acc + window(col, tap) * w_ref[tap:tap + 1, lanes]
        cols.append(acc)
    return jnp.concatenate(cols, axis=-1)


def _conv_sample_kernel(a_ref, st_ref, w_ref, b_ref, g_ref, bl_ref, *rest, ts):
    o_ref, buf_all_ref, full_ref, cv_ref = rest[-4:]
    buf_ref = buf_all_ref.at[0]
    for other in range(1, buf_all_ref.shape[0]):
        buf_all_ref[other] = jnp.zeros(buf_all_ref.shape[1:], f32)
    for s in range(SAMPLE_GROUP):
        for col in range(D_CONV // LANES):
            lanes = slice(col * LANES, (col + 1) * LANES)
            full_ref[col, 0:CONV_HALO, :] = st_ref[:, s, lanes]
            full_ref[col, CONV_HALO:CONV_HALO + ts, :] = a_ref[s * ts:(s + 1) * ts, lanes]
            buf_ref[s, :, lanes] = full_ref[col, ts:ts + CONV_HALO, :]
        cv_ref[s * ts:(s + 1) * ts, :] = _conv_taps(lambda col, tap: full_ref[col, tap:tap + ts, :],
                                                    w_ref, b_ref, ts)
    o_ref[...] = _ln_silu(cv_ref[...], g_ref[...], bl_ref[...]).astype(bf16)


def _conv_sample(tl, layer, a, state_conv, w_conv, b_conv, g_ln, b_ln, prev_bufs):
    ts = tl.ts
    depth = state_conv.shape[0]
    rows = SAMPLE_GROUP * ts
    first_block = tl.n_prompt // rows
    vec = pl.BlockSpec((None, 1, D_CONV), lambda i: (layer, 0, 0))
    in_specs = [pl.BlockSpec((rows, D_CONV), lambda i: (first_block + i, 0)),
                pl.BlockSpec((None, CONV_HALO, SAMPLE_GROUP, D_CONV), lambda i: (layer, 0, i, 0)),
                pl.BlockSpec((None, CONV_WIDTH, D_CONV), lambda i: (layer, 0, 0)),
                vec, vec, vec]
    args = [a, state_conv.transpose(0, 2, 1, 3), w_conv, b_conv, g_ln, b_ln]
    buf_block = (SAMPLE_GROUP, CONV_HALO, D_CONV)
    if prev_bufs is None:
        buf_spec = pl.BlockSpec((depth,) + buf_block, lambda i: (0, i, 0, 0))
        aliases = {}
    else:
        buf_spec = pl.BlockSpec((1,) + buf_block, lambda i: (layer, i, 0, 0))
        in_specs.append(pl.BlockSpec(memory_space=pl.ANY))
        args.append(prev_bufs)
        aliases = {len(args) - 1: 1}
    return pl.pallas_call(
        functools.partial(_conv_sample_kernel, ts=ts),
        grid=(tl.bs // SAMPLE_GROUP,),
        in_specs=in_specs,
        out_specs=[pl.BlockSpec((rows, D_CONV), lambda i: (i, 0)), buf_spec],
        out_shape=[jax.ShapeDtypeStruct((tl.n_sample, D_CONV), bf16),
                   jax.ShapeDtypeStruct((depth, tl.bs, CONV_HALO, D_CONV), f32)],
        input_output_aliases=aliases,
        scratch_shapes=[pltpu.VMEM((D_CONV // LANES, CONV_HALO + ts + SUBLANES, LANES), f32),
                        pltpu.VMEM((rows, D_CONV), f32)],
        compiler_params=_cparams(1),
        name="conv_sample",
    )(*args)


N_CONV_COLS = D_CONV // LANES
CONV_TILE = TS
N_CONV_CHUNKS = CONV_TILE // CONV_ROWS


def _inproj_kernel(*refs, np_steps, split_x):
    if split_x:
        xp_ref, xs_ref = refs[:2]
        refs = refs[2:]
    else:
        x_ref = refs[0]
        refs = refs[1:]
    (shs_ref, sht_ref, scs_ref, sct_ref, g_ref, w_ref, cos_ref, sin_ref,
     q_ref, k_ref, v_ref, gate_ref, a_ref) = refs
    is_s = pl.program_id(0) >= np_steps

    def step(sample):
        for t in range(STEP_TILES):
            rs = slice(t * TM, (t + 1) * TM)
            if split_x:
                x = xs_ref[rs, :] if sample else xp_ref[rs, :]
            else:
                x = x_ref[rs, :]
            sh, sc = (sht_ref[rs, :], sct_ref[rs, :]) if sample else (shs_ref[...], scs_ref[...])
            hb = (_rms(x, g_ref[...]) * (1.0 + sc) + sh).astype(bf16)

            def group(g):
                return jnp.dot(hb, w_ref[:, g * D_RET:(g + 1) * D_RET], preferred_element_type=f32)

            cos = cos_ref[rs, :]
            sin = sin_ref[rs, :]

            def rope(th):
                return th * cos + pltpu.roll(th, HEAD_D // 2, 1) * sin

            qg, kg = group(0), group(1)
            for hd in range(N_HEADS):
                sl = slice(hd * HEAD_D, (hd + 1) * HEAD_D)
                q_ref[rs, sl] = rope(qg[:, sl]).astype(bf16)
                k_ref[rs, sl] = rope(kg[:, sl]) * (HEAD_D ** -0.5)
            v_ref[rs, :] = group(2).astype(bf16)
            gate_ref[rs, :] = group(3)
            a_ref[rs, :] = group(4) * jax.nn.sigmoid(group(5))

    pl.when(is_s)(functools.partial(step, True))
    pl.when(jnp.logical_not(is_s))(functools.partial(step, False))


def _conv_prompt_kernel(a_ref, w_ref, b_ref, g_ref, bl_ref, o_ref, buf_ref, full_ref, cv_ref):
    j = pl.program_id(1)

    @pl.when(j == 0)
    def _():
        full_ref[:, 0:HALO_PAD, :] = jnp.zeros((N_CONV_COLS, HALO_PAD, LANES), f32)

    @pl.when(j > 0)
    def _():
        full_ref[:, 0:HALO_PAD, :] = full_ref[:, CONV_TILE:CONV_TILE + HALO_PAD, :]

    for col in range(N_CONV_COLS):
        full_ref[col, HALO_PAD:HALO_PAD + CONV_TILE, :] = a_ref[:, col * LANES:(col + 1) * LANES]
    shift = HALO_PAD - CONV_HALO

    def taps(idx, carry):
        col = idx // N_CONV_CHUNKS
        r0 = pl.multiple_of((idx % N_CONV_CHUNKS) * CONV_ROWS, CONV_ROWS)
        acc = jnp.broadcast_to(b_ref[col], (CONV_ROWS, LANES))
        for tap in range(CONV_WIDTH):
            acc = acc + full_ref[col, pl.ds(r0 + (tap + shift), CONV_ROWS), :] * w_ref[col, tap:tap + 1, :]
        cv_ref[col, pl.ds(r0, CONV_ROWS), :] = acc
        return carry

    lax.fori_loop(0, N_CONV_COLS * N_CONV_CHUNKS, taps, 0)

    for r0 in range(0, CONV_TILE, NORM_ROWS):
        cv = jnp.concatenate([cv_ref[col, r0:r0 + NORM_ROWS, :] for col in range(N_CONV_COLS)], axis=-1)
        o_ref[r0:r0 + NORM_ROWS, :] = _ln_silu(cv, g_ref[...], bl_ref[...]).astype(bf16)

    @pl.when(j == pl.num_programs(1) - 1)
    def _():
        buf_ref[...] = a_ref[CONV_TILE - CONV_HALO:CONV_TILE, :]


def _conv_prompt(tl, layer, a, w_conv, b_conv, g_ln, b_ln):
    assert tl.tp % CONV_TILE == 0
    tps = tl.tp // CONV_TILE
    depth = w_conv.shape[0]
    w_cols = w_conv.reshape(depth, CONV_WIDTH, N_CONV_COLS, LANES).transpose(0, 2, 1, 3)
    b_cols = b_conv.reshape(depth, N_CONV_COLS, 1, LANES)
    vec = pl.BlockSpec((None, 1, D_CONV), lambda b, j: (layer, 0, 0))
    return pl.pallas_call(
        _conv_prompt_kernel,
        grid=(tl.bp, tps),
        in_specs=[pl.BlockSpec((CONV_TILE, D_CONV), lambda b, j: (b * tps + j, 0)),
                  pl.BlockSpec((None, N_CONV_COLS, CONV_WIDTH, LANES), lambda b, j: (layer, 0, 0, 0)),
                  pl.BlockSpec((None, N_CONV_COLS, 1, LANES), lambda b, j: (layer, 0, 0, 0)),
                  vec, vec],
        out_specs=[pl.BlockSpec((CONV_TILE, D_CONV), lambda b, j: (b * tps + j, 0)),
                   pl.BlockSpec((None, CONV_HALO, D_CONV), lambda b, j: (b, 0, 0))],
        out_shape=[jax.ShapeDtypeStruct((tl.n_prompt, D_CONV), bf16),
                   jax.ShapeDtypeStruct((tl.bp, CONV_HALO, D_CONV), f32)],
        scratch_shapes=[pltpu.VMEM((N_CONV_COLS, HALO_PAD + CONV_TILE, LANES), f32),
                        pltpu.VMEM((N_CONV_COLS, CONV_TILE, LANES), f32)],
        compiler_params=_cparams(2),
        name="conv_prompt",
    )(a, w_cols, b_cols, g_ln, b_ln)


def _inproj(tl, layer, x, mod_seq, mod_tok, g_norm, w_in_bf, cos_tab, sin_tab):
    split_x = isinstance(x, tuple)
    tok_spec = pl.BlockSpec((TS, D_MODEL), lambda i: (i, 0))
    if split_x:
        x_args = list(x)
        x_specs = [pl.BlockSpec((TS, D_MODEL), lambda i: (tl.prompt_block(i), 0)),
                   pl.BlockSpec((TS, D_MODEL), lambda i: (tl.sample_block(i), 0))]
    else:
        x_args, x_specs = [x], [tok_spec]
    sh_seq, sh_tok = _mod_specs(tl, layer, 0)
    sc_seq, sc_tok = _mod_specs(tl, layer, 1)

    def table_block(i):
        return jnp.where(i < tl.np_steps, i % tl.steps_per_seq, tl.steps_per_seq + tl.sample_block(i))

    tab_spec = pl.BlockSpec((TS, HEAD_D), lambda i: (table_block(i), 0))
    row_spec = pl.BlockSpec((TS, D_RET), lambda i: (i, 0))
    row_sd = lambda dt: jax.ShapeDtypeStruct((tl.n_tok, D_RET), dt)
    return pl.pallas_call(
        functools.partial(_inproj_kernel, np_steps=tl.np_steps, split_x=split_x),
        grid=(tl.n_steps,),
        in_specs=x_specs + [
            sh_seq, sh_tok, sc_seq, sc_tok,
            pl.BlockSpec((None, 1, D_MODEL), lambda i: (layer, 0, 0)),
            pl.BlockSpec((None, D_MODEL, D_IN), lambda i: (layer, 0, 0)),
            tab_spec, tab_spec,
        ],
        out_specs=[row_spec] * 5,
        out_shape=[row_sd(bf16), row_sd(f32), row_sd(bf16), row_sd(f32), row_sd(f32)],
        compiler_params=_cparams(1),
        name="inproj",
    )(*x_args, mod_seq, mod_tok, mod_seq, mod_tok, g_norm, w_in_bf, cos_tab, sin_tab)


SORT_ROWS = 2 * TM + N_EXPERTS * SUBLANES
XS_HALF = D_MODEL // 2
XS_W = XS_HALF + LANES
u32 = jnp.uint32
ROUTE_POS, ROUTE_W0, ROUTE_W1 = (0, 2), (2, 5), (5, 8)


def _pack_bf16_pair(x):
    lo = lax.shift_right_logical(lax.bitcast_convert_type(x[:, 0:XS_HALF], u32), u32(16))
    hi = lax.bitcast_convert_type(x[:, XS_HALF:D_MODEL], u32) & u32(0xFFFF0000)
    return hi | lo


def _unpack_bf16_pair(words):
    lo = lax.bitcast_convert_type(lax.shift_left(words, u32(16)), f32).astype(bf16)
    hi = lax.bitcast_convert_type(words & u32(0xFFFF0000), f32).astype(bf16)
    return lo, hi


def _split3(x):
    a = x.astype(bf16)
    r = x - a.astype(f32)
    b = r.astype(bf16)
    c = (r - b.astype(f32)).astype(bf16)
    return a, b, c


def _first_of4(vals, m):
    return jnp.where(vals[0] == m, 0.0, jnp.where(vals[1] == m, 1.0, jnp.where(vals[2] == m, 2.0, 3.0)))


def _rows_to_tile(rows, n_rows):
    sub = lax.broadcasted_iota(i32, (n_rows, TM), 0)
    out = jnp.zeros((n_rows, TM), f32)
    for r, val in enumerate(rows):
        out = jnp.where(sub == r, val, out)
    return out


def _outproj_kernel(*refs, np_steps, split_x):
    if split_x:
        xp_ref, xs_ref = refs[:2]
        refs = refs[2:]
    else:
        x_ref = refs[0]
        refs = refs[1:]
    (retp_ref, rets_ref, cvp_ref, cvs_ref, gts_ref, gtt_ref, shs_ref, sht_ref, scs_ref, sct_ref,
     g_ref, wo_ref, wr_ref, br_ref,
     xo_ref, h2_ref, rows_ref, cols_ref, cnt_ref) = refs
    is_s = pl.program_id(0) >= np_steps

    def step(sample):
        for t in range(STEP_TILES):
            rs = slice(t * TM, (t + 1) * TM)
            if split_x:
                x = xs_ref[rs, :] if sample else xp_ref[rs, :]
            else:
                x = x_ref[rs, :]
            ret = rets_ref[rs, :] if sample else retp_ref[rs, :]
            cv = cvs_ref[rs, :] if sample else cvp_ref[rs, :]
            gt, sh, sc = ((gtt_ref[rs, :], sht_ref[rs, :], sct_ref[rs, :]) if sample
                          else (gts_ref[...], shs_ref[...], scs_ref[...]))
            mix_out = (jnp.dot(ret, wo_ref[0:D_RET, :], preferred_element_type=f32)
                       + jnp.dot(cv, wo_ref[D_RET:D_RET + D_CONV, :], preferred_element_type=f32))
            xn = x + gt * mix_out
            xo_ref[rs, :] = xn
            h2 = _rms(xn, g_ref[...]) * (1.0 + sc) + sh
            h_hi = h2.astype(bf16)
            h2_ref[rs, :] = h_hi
            rows, cols, cnt = _route_tile(h_hi, wr_ref, br_ref)
            rows_ref[t] = rows
            cols_ref[rs, :] = cols
            cnt_ref[t] = cnt

    pl.when(is_s)(functools.partial(step, True))
    pl.when(jnp.logical_not(is_s))(functools.partial(step, False))


def _route_tile(h_bf, wr_ref, br_ref):
    logits = jnp.dot(h_bf, wr_ref[...], preferred_element_type=f32) + br_ref[...]
    lt = logits.T
    row = [lt[e:e + 1, :] for e in range(N_EXPERTS)]
    top = functools.reduce(jnp.maximum, row)
    ex = [jnp.exp(r - top) for r in row]
    den = functools.reduce(jnp.add, ex)
    p = [v / den for v in ex]

    best = None
    for g in range(N_GROUPS):
        a = p[g * GROUP_SIZE:(g + 1) * GROUP_SIZE]
        m1 = functools.reduce(jnp.maximum, a)
        i1 = _first_of4(a, m1)
        b = [jnp.where(i1 == float(j), -1.0, a[j]) for j in range(GROUP_SIZE)]
        m2 = functools.reduce(jnp.maximum, b)
        i2 = _first_of4(b, m2)
        cand = (m1 + m2, m1, m2, i1 + float(g * GROUP_SIZE), i2 + float(g * GROUP_SIZE))
        if best is None:
            best = cand
        else:
            take = cand[0] > best[0]
            best = tuple(jnp.where(take, c, o) for c, o in zip(cand, best))
    _, m1, m2, e0, e1 = best
    denom = m1 + m2
    w0 = m1 / denom
    w1 = m2 / denom

    ex_id = lax.broadcasted_iota(i32, (N_EXPERTS, TM), 0).astype(f32)
    sel0 = ex_id == e0
    sel1 = ex_id == e1
    ind = jnp.where(sel0 | sel1, 1.0, 0.0)
    t_r = lax.broadcasted_iota(i32, (TM, TM), 0)
    t_c = lax.broadcasted_iota(i32, (TM, TM), 1)
    earlier = jnp.where(t_r < t_c, 1.0, 0.0).astype(bf16)
    prefix = jnp.dot(ind.astype(bf16), earlier, preferred_element_type=f32)
    cnt = jnp.sum(ind, axis=-1, keepdims=True)
    cnt8 = jnp.floor((cnt + float(SUBLANES - 1)) * (1.0 / SUBLANES)) * float(SUBLANES)
    e_r = lax.broadcasted_iota(i32, (N_EXPERTS, N_EXPERTS), 0)
    e_c = lax.broadcasted_iota(i32, (N_EXPERTS, N_EXPERTS), 1)
    below = jnp.where(e_c < e_r, 1.0, 0.0).astype(bf16)
    seg_off = jnp.dot(below, jnp.broadcast_to(cnt8, (N_EXPERTS, TM)).astype(bf16),
                      preferred_element_type=f32)
    where_to = seg_off + prefix
    pos0 = jnp.sum(jnp.where(sel0, where_to, 0.0), axis=0, keepdims=True)
    pos1 = jnp.sum(jnp.where(sel1, where_to, 0.0), axis=0, keepdims=True)

    w0p = [v.astype(f32) for v in _split3(w0)]
    w1p = [v.astype(f32) for v in _split3(w1)]
    info = [pos0, pos1] + w0p + w1p
    return (_rows_to_tile(info, SUBLANES), _rows_to_tile(info, LANES).T,
            jnp.broadcast_to(cnt, (N_EXPERTS, LANES)))


def _outproj(tl, layer, x, ret_p, ret_s, cv_p, cv_s, mod_seq, mod_tok, g_norm, w_out_bf, wr_bf, br_pad):
    split_x = isinstance(x, tuple)
    tok_spec = pl.BlockSpec((TS, D_MODEL), lambda i: (i, 0))
    p_spec = lambda w: pl.BlockSpec((TS, w), lambda i: (tl.prompt_block(i), 0))
    s_spec = lambda w: pl.BlockSpec((TS, w), lambda i: (tl.sample_block(i), 0))
    if split_x:
        x_args, x_specs = list(x), [p_spec(D_MODEL), s_spec(D_MODEL)]
    else:
        x_args, x_specs = [x], [tok_spec]
    mods = []
    for col in (2, 3, 4):
        mods += list(_mod_specs(tl, layer, col))
    wr_spec = pl.BlockSpec((D_MODEL, LANES), lambda i: (0, 0))
    return pl.pallas_call(
        functools.partial(_outproj_kernel, np_steps=tl.np_steps, split_x=split_x),
        grid=(tl.n_steps,),
        in_specs=x_specs + [p_spec(D_RET), s_spec(D_RET), p_spec(D_CONV), s_spec(D_CONV)] + mods + [
            pl.BlockSpec((None, 1, D_MODEL), lambda i: (layer, 0, 0)),
            pl.BlockSpec((None, D_MODEL, D_MODEL), lambda i: (layer, 0, 0)),
            wr_spec,
            pl.BlockSpec((1, LANES), lambda i: (0, 0)),
        ],
        out_specs=[tok_spec, tok_spec,
                   pl.BlockSpec((STEP_TILES, SUBLANES, TM), lambda i: (i, 0, 0)),
                   pl.BlockSpec((TS, LANES), lambda i: (i, 0)),
                   pl.BlockSpec((STEP_TILES, N_EXPERTS, LANES), lambda i: (i, 0, 0))],
        out_shape=[jax.ShapeDtypeStruct((tl.n_tok, D_MODEL), f32),
                   jax.ShapeDtypeStruct((tl.n_tok, D_MODEL), bf16),
                   jax.ShapeDtypeStruct((tl.n_tiles, SUBLANES, TM), f32),
                   jax.ShapeDtypeStruct((tl.n_tok, LANES), f32),
                   jax.ShapeDtypeStruct((tl.n_tiles, N_EXPERTS, LANES), f32)],
        compiler_params=_cparams(1),
        name="outproj_router",
    )(*x_args, ret_p, ret_s, cv_p, cv_s, *([mod_seq, mod_tok] * 3), g_norm, w_out_bf, wr_bf, br_pad)


N_CHUNKS = SORT_ROWS // SUBLANES


class _Layout:
    def __init__(self, n_tiles):
        self.n_tiles = n_tiles
        self.tail_start = n_tiles * N_CHUNKS
        self.tail_n8 = self.tail_start + N_EXPERTS
        worst = 2 * n_tiles * TM + n_tiles * N_EXPERTS * (SUBLANES - 1) + N_EXPERTS * (BM - SUBLANES)
        self.n_blocks = -(-worst // BM)
        self.cap = self.n_blocks * BM
        self.dump = self.cap
        self.xs_rows = self.cap + -(-2 * STEP_TILES * SORT_ROWS // BM) * BM


def _moe_tables(lay, tile_counts):
    c8 = ((tile_counts.astype(i32) + SUBLANES - 1) // SUBLANES) * SUBLANES
    base8 = jnp.cumsum(c8, axis=0) - c8
    tot8 = jnp.sum(c8, axis=0)
    region = ((tot8 + BM - 1) // BM) * BM
    g_end = jnp.cumsum(region)
    g_start = g_end - region
    seg_end = jnp.cumsum(c8, axis=1)
    seg_dst = g_start[None, :] + base8
    n_used = g_end[-1] // BM
    blk = jnp.arange(lay.n_blocks, dtype=i32)
    block_e = jnp.minimum(jnp.sum((g_end[None, :] <= blk[:, None] * BM).astype(i32), axis=1), N_EXPERTS - 1)
    block_e = jnp.where(blk < n_used, block_e, block_e[n_used - 1])
    row0 = jnp.arange(N_CHUNKS, dtype=i32) * SUBLANES
    owner = jnp.sum((seg_end[:, None, :] <= row0[None, :, None]).astype(i32), axis=-1)
    onehot = (owner[:, :, None] == jnp.arange(N_EXPERTS, dtype=i32)[None, None, :]).astype(i32)
    delta = seg_dst - (seg_end - c8)
    chunk_dst = jnp.where(owner < N_EXPERTS, row0[None, :] + jnp.sum(onehot * delta[:, None, :], axis=-1), -1)
    tab = jnp.concatenate([chunk_dst.ravel(), g_start + tot8, (region - tot8) // SUBLANES]).astype(i32)
    ids = jnp.arange(N_EXPERTS, dtype=i32)
    later = jnp.where((ids[None, :] > ids[:, None]) & (region[None, :] > 0), ids[None, :], N_EXPERTS)
    next_e = jnp.min(later, axis=1)
    next_e = jnp.where(next_e == N_EXPERTS, -1, next_e)
    owner_end = jnp.sum((block_e[:, None] == ids[None, :]).astype(i32) * (g_start + tot8)[None, :], axis=1)
    rows_used = jnp.clip(owner_end - blk * BM, 0, BM)
    return tab, jnp.concatenate([block_e, n_used[None], next_e, rows_used]).astype(i32)


def _for_chunks(n, fn):
    def body(c, carry):
        fn(c)
        return carry

    lax.fori_loop(0, n, body, 0)


def _dispatch_kernel(tab_ref, h2_ref, rows_ref, cols_ref, xs_hbm, sorted_ref, zero_ref, sem, *, lay, n_steps):
    i = pl.program_id(0)
    slot = i % 2
    step_rows = STEP_TILES * SORT_ROWS

    def tail_copy(dst):
        return pltpu.make_async_copy(zero_ref, xs_hbm.at[pl.ds(dst, SUBLANES)], sem.at[2])

    def wait_step(slot_):
        pltpu.make_async_copy(sorted_ref.at[slot_], xs_hbm.at[pl.ds(0, step_rows)], sem.at[slot_]).wait()

    @pl.when(i == 0)
    def _():
        zero_ref[...] = jnp.zeros_like(zero_ref)
        for e in range(N_EXPERTS):
            start = tab_ref[lay.tail_start + e]
            _for_chunks(tab_ref[lay.tail_n8 + e],
                        lambda c: tail_copy(pl.multiple_of(start + c * SUBLANES, SUBLANES)).start())

    @pl.when(i >= 2)
    def _():
        wait_step(slot)

    r_id = lax.broadcasted_iota(i32, (SORT_ROWS, TM), 0).astype(f32)
    lane = lax.broadcasted_iota(i32, (TM, LANES), 1)
    for t in range(STEP_TILES):
        pos0 = rows_ref[t, 0:1, :]
        pos1 = rows_ref[t, 1:2, :]
        p0 = r_id == pos0
        p1 = r_id == pos1
        perm = jnp.where(p0 | p1, 1.0, 0.0).astype(bf16)
        cols = cols_ref[t * TM:(t + 1) * TM, :]
        wpart0 = jnp.where((lane >= ROUTE_W0[0]) & (lane < ROUTE_W0[1]), cols, 0.0).astype(bf16)
        wpart1 = jnp.where((lane >= ROUTE_W1[0]) & (lane < ROUTE_W1[1]), cols, 0.0).astype(bf16)
        sw = (jnp.dot(jnp.where(p0, 1.0, 0.0).astype(bf16), wpart0, preferred_element_type=f32)
              + jnp.dot(jnp.where(p1, 1.0, 0.0).astype(bf16), wpart1, preferred_element_type=f32))
        base = t * SORT_ROWS
        sorted_ref[slot, base:base + SORT_ROWS, 0:XS_HALF] = _pack_bf16_pair(jnp.dot(
            perm, h2_ref[t * TM:(t + 1) * TM, :], preferred_element_type=f32))
        sorted_ref[slot, base:base + SORT_ROWS, XS_HALF:XS_W] = lax.bitcast_convert_type(jnp.broadcast_to(
            jnp.sum(sw, axis=-1, keepdims=True), (SORT_ROWS, LANES)), u32)

    for t in range(STEP_TILES):
        for c in range(N_CHUNKS):
            row = t * SORT_ROWS + c * SUBLANES
            dst = tab_ref[(i * STEP_TILES + t) * N_CHUNKS + c]
            dst = jnp.where(dst < 0, lay.dump + slot * step_rows + row, dst)
            pltpu.make_async_copy(sorted_ref.at[slot, pl.ds(row, SUBLANES)],
                                  xs_hbm.at[pl.ds(pl.multiple_of(dst, SUBLANES), SUBLANES)], sem.at[slot]
                                  ).start(priority=c % 2)

    @pl.when(i == n_steps - 1)
    def _():
        if n_steps >= 2:
            wait_step(1 - slot)
        wait_step(slot)
        for e in range(N_EXPERTS):
            _for_chunks(tab_ref[lay.tail_n8 + e], lambda c: tail_copy(0).wait())


def _dispatch(tl, lay, tab, h2, rows, cols):
    grid_spec = pltpu.PrefetchScalarGridSpec(
        num_scalar_prefetch=1,
        grid=(tl.n_steps,),
        in_specs=[pl.BlockSpec((TS, D_MODEL), lambda i, t: (i, 0)),
                  pl.BlockSpec((STEP_TILES, SUBLANES, TM), lambda i, t: (i, 0, 0)),
                  pl.BlockSpec((TS, LANES), lambda i, t: (i, 0))],
        out_specs=pl.BlockSpec(memory_space=pl.ANY),
        scratch_shapes=[pltpu.VMEM((2, STEP_TILES * SORT_ROWS, XS_W), u32), pltpu.VMEM((SUBLANES, XS_W), u32),
                        pltpu.SemaphoreType.DMA((3,))],
    )
    return pl.pallas_call(
        functools.partial(_dispatch_kernel, lay=lay, n_steps=tl.n_steps),
        grid_spec=grid_spec,
        out_shape=jax.ShapeDtypeStruct((lay.xs_rows, XS_W), u32),
        compiler_params=_cparams(1),
        name="moe_dispatch",
    )(tab, h2, rows, cols)


def _expert_kernel(be_ref, xs_ref, wg_hbm, wu_hbm, wd_hbm, ys_ref, stage, w_bf, sem, *, n_blocks, layer):
    j = pl.program_id(0)

    def fetch(e):
        return [pltpu.make_async_copy(w.at[layer, e], stage.at[k], sem.at[k])
                for k, w in enumerate((wg_hbm, wu_hbm, wd_hbm))]

    @pl.when(j < be_ref[n_blocks])
    def _():
        e = be_ref[j]

        @pl.when(j == 0)
        def _():
            for copy in fetch(e):
                copy.start()

        @pl.when((j == 0) | (e != be_ref[jnp.maximum(j - 1, 0)]))
        def _():
            for copy in fetch(e):
                copy.wait()
            for k in range(3):
                w_bf[k] = stage[k].astype(bf16)
            nxt = be_ref[n_blocks + 1 + e]

            @pl.when(nxt >= 0)
            def _():
                for copy in fetch(nxt):
                    copy.start()

        def ffn(rows):
            x_lo, x_hi = _unpack_bf16_pair(xs_ref[rows, 0:XS_HALF])

            def first_layer(k):
                return (jnp.dot(x_lo, w_bf[k, 0:XS_HALF, :], preferred_element_type=f32)
                        + jnp.dot(x_hi, w_bf[k, XS_HALF:D_MODEL, :], preferred_element_type=f32))

            mid = (jax.nn.silu(first_layer(0)) * first_layer(1)).astype(bf16)
            slot_w = lax.bitcast_convert_type(xs_ref[rows, XS_HALF:XS_HALF + 1], f32)
            ys_ref[rows, :] = jnp.dot(mid, w_bf[2], preferred_element_type=f32) * slot_w

        rows_used = be_ref[n_blocks + 1 + N_EXPERTS + j]
        for parts in range(1, BM // BM_PART + 1):
            top = parts * BM_PART

            @pl.when((rows_used > top - BM_PART) & (rows_used <= top))
            def _(top=top):
                ffn(slice(0, top))
                if top < BM:
                    ys_ref[top:BM, :] = jnp.zeros((BM - top, D_MODEL), f32)


def _experts(layer, lay, block_e, xs, w_gate, w_up, w_down):
    n_blocks = lay.n_blocks
    d_ff = w_gate.shape[-1]
    assert d_ff == D_MODEL
    used = lambda j, be: jnp.minimum(j, be[n_blocks] - 1)
    any_spec = pl.BlockSpec(memory_space=pl.ANY)
    grid_spec = pltpu.PrefetchScalarGridSpec(
        num_scalar_prefetch=1,
        grid=(n_blocks,),
        in_specs=[pl.BlockSpec((BM, XS_W), lambda j, be: (used(j, be), 0)), any_spec, any_spec, any_spec],
        out_specs=pl.BlockSpec((BM, D_MODEL), lambda j, be: (used(j, be), 0)),
        scratch_shapes=[pltpu.VMEM((3, D_MODEL, d_ff), f32), pltpu.VMEM((3, D_MODEL, d_ff), bf16),
                        pltpu.SemaphoreType.DMA((3,))],
    )
    return pl.pallas_call(
        functools.partial(_expert_kernel, n_blocks=n_blocks, layer=layer),
        grid_spec=grid_spec,
        out_shape=jax.ShapeDtypeStruct((lay.cap, D_MODEL), f32),
        compiler_params=_cparams(1),
        name="moe_experts",
    )(block_e, xs, w_gate, w_up, w_down)


def _combine_kernel(tab_ref, ys_hbm, cols_ref, x_ref, gts_ref, gtt_ref, *rest, n_steps, np_steps, final):
    if final:
        gf_ref, yp_ref, ysm_ref, staged, sem = rest
    else:
        xo_ref, staged, sem = rest
    i = pl.program_id(0)
    slot = i % 2
    is_s = i >= np_steps

    def start_step(step, slot_):
        for c in range(STEP_TILES * N_CHUNKS):
            src = jnp.maximum(tab_ref[step * (STEP_TILES * N_CHUNKS) + c], 0)
            pltpu.make_async_copy(ys_hbm.at[pl.ds(pl.multiple_of(src, SUBLANES), SUBLANES)],
                                  staged.at[slot_, pl.ds(c * SUBLANES, SUBLANES)], sem.at[slot_]
                                  ).start(priority=c % 2)

    @pl.when(i == 0)
    def _():
        start_step(0, 0)

    @pl.when(i + 1 < n_steps)
    def _():
        start_step(i + 1, 1 - slot)

    pltpu.make_async_copy(ys_hbm.at[pl.ds(0, STEP_TILES * SORT_ROWS)], staged.at[slot], sem.at[slot]).wait()

    def step(sample):
        lane = lax.broadcasted_iota(i32, (TM, SORT_ROWS), 1).astype(f32)
        for t in range(STEP_TILES):
            rs = slice(t * TM, (t + 1) * TM)
            unperm = jnp.where((lane == cols_ref[rs, 0:1]) | (lane == cols_ref[rs, 1:2]), 1.0, 0.0).astype(bf16)
            parts = _split3(staged[slot, t * SORT_ROWS:(t + 1) * SORT_ROWS, :])
            ff = sum(jnp.dot(unperm, part, preferred_element_type=f32) for part in parts)
            xn = x_ref[rs, :] + (gtt_ref[rs, :] if sample else gts_ref[...]) * ff
            if final:
                (ysm_ref if sample else yp_ref)[rs, :] = _rms(xn, gf_ref[...])
            else:
                xo_ref[rs, :] = xn

    pl.when(is_s)(functools.partial(step, True))
    pl.when(jnp.logical_not(is_s))(functools.partial(step, False))


def _combine(tl, layer, tab, ys, cols, x, mod_seq, mod_tok, g_final):
    final = g_final is not None
    tok_spec = pl.BlockSpec((TS, D_MODEL), lambda i, t: (i, 0))
    gt_seq, gt_tok = _mod_specs(tl, layer, 5)
    in_specs = [pl.BlockSpec(memory_space=pl.ANY), pl.BlockSpec((TS, LANES), lambda i, t: (i, 0)),
                tok_spec, gt_seq, gt_tok]
    args = [tab, ys, cols, x, mod_seq, mod_tok]
    if final:
        in_specs.append(pl.BlockSpec((1, D_MODEL), lambda i, t: (0, 0)))
        args.append(g_final)
        out_specs = [pl.BlockSpec((TS, D_MODEL), lambda i, t: (tl.prompt_block(i), 0)),
                     pl.BlockSpec((TS, D_MODEL), lambda i, t: (tl.sample_block(i), 0))]
        out_shape = [jax.ShapeDtypeStruct((tl.n_prompt, D_MODEL), f32),
                     jax.ShapeDtypeStruct((tl.n_sample, D_MODEL), f32)]
    else:
        out_specs = tok_spec
        out_shape = jax.ShapeDtypeStruct((tl.n_tok, D_MODEL), f32)
    grid_spec = pltpu.PrefetchScalarGridSpec(
        num_scalar_prefetch=1,
        grid=(tl.n_steps,),
        in_specs=in_specs,
        out_specs=out_specs,
        scratch_shapes=[pltpu.VMEM((2, STEP_TILES * SORT_ROWS, D_MODEL), f32), pltpu.SemaphoreType.DMA((2,))],
    )
    return pl.pallas_call(
        functools.partial(_combine_kernel, n_steps=tl.n_steps, np_steps=tl.np_steps, final=final),
        grid_spec=grid_spec,
        out_shape=out_shape,
        compiler_params=_cparams(1),
        name="moe_combine",
    )(*args)


def _rope_tables(tl):
    half = HEAD_D // 2
    inv = ROPE_BASE ** (-jnp.arange(half, dtype=f32) / half)
    pos_p = jnp.arange(tl.tp, dtype=i32)
    pos_s = PAST_LEN + jnp.arange(tl.ts, dtype=i32)
    pos = jnp.concatenate([pos_p, jnp.tile(pos_s, tl.bs)])
    ang = pos.astype(f32)[:, None] * inv[None, :]
    cos, sin = jnp.cos(ang), jnp.sin(ang)
    return jnp.concatenate([cos, cos], axis=-1), jnp.concatenate([-sin, sin], axis=-1)


def kernel(x_prompt, x_sample, state_ret, state_conv, c_prompt, c_sample, w_mod, b_mod, g_mix_norm, w_in,
           w_conv, b_conv, g_conv_ln, b_conv_ln, g_ret_gn, w_out, g_ffn_norm, w_router, b_router,
           w_exp_gate, w_exp_up, w_exp_down, g_final):
    bp, tp, _ = x_prompt.shape
    bs, ts, _ = x_sample.shape
    depth = w_mod.shape[0]
    tl = _Tiles(bp, tp, bs, ts)
    lay = _Layout(tl.n_tiles)

    c_all = jnp.concatenate([c_prompt, jnp.repeat(c_sample, ts, axis=0)], axis=0)
    mod_seq, mod_tok = _modulation(c_all, bp, w_mod, b_mod)
    mod_seq = mod_seq.reshape(depth, bp, 1, N_MOD * D_MODEL)

    cos_tab, sin_tab = _rope_tables(tl)
    w_in_bf = w_in.astype(bf16)
    w_out_bf = w_out.astype(bf16)
    wr_pad = jnp.pad(w_router.astype(f32), ((0, 0), (0, LANES - N_EXPERTS)))
    wr_bf = wr_pad.astype(bf16)
    br_pad = jnp.pad(b_router.astype(f32), (0, LANES - N_EXPERTS)).reshape(1, LANES)
    vec3 = lambda t: t.reshape(depth, 1, t.shape[-1])
    g_mix3, g_ffn3, gn3 = vec3(g_mix_norm), vec3(g_ffn_norm), vec3(g_ret_gn)
    b_conv3, g_ln3, b_ln3 = vec3(b_conv), vec3(g_conv_ln), vec3(b_conv_ln)

    x = (x_prompt.reshape(tl.n_prompt, D_MODEL), x_sample.reshape(tl.n_sample, D_MODEL))
    ret_p, conv_p = [], []
    ret_s_all = conv_s_all = None
    for layer in range(depth):
        q, k, v, gate, a = _inproj(tl, layer, x, mod_seq, mod_tok, g_mix3, w_in_bf, cos_tab, sin_tab)
        ro_p, s_p = _retention_prompt(tl, layer, q, k, v, gate, gn3)
        ro_s, ret_s_all = _retention_sample(tl, layer, q, k, v, gate, state_ret, gn3, ret_s_all)
        co_p, buf_p = _conv_prompt(tl, layer, a, w_conv, b_conv3, g_ln3, b_ln3)
        co_s, conv_s_all = _conv_sample(tl, layer, a, state_conv, w_conv, b_conv3, g_ln3, b_ln3, conv_s_all)
        x_mid, h2, rows, cols, tile_counts = _outproj(
            tl, layer, x, ro_p, ro_s, co_p, co_s, mod_seq, mod_tok, g_ffn3, w_out_bf, wr_bf, br_pad)
        tab, block_e = _moe_tables(lay, tile_counts[:, :, 0])
        xs = _dispatch(tl, lay, tab, h2, rows, cols)
        ys = _experts(layer, lay, block_e, xs, w_exp_gate, w_exp_up, w_exp_down)
        last = layer == depth - 1
        x = _combine(tl, layer, tab, ys, cols, x_mid, mod_seq, mod_tok,
                     g_final.reshape(1, D_MODEL) if last else None)
        ret_p.append(s_p)
        conv_p.append(buf_p)
    y_p, y_s = x
    return (y_p.reshape(bp, tp, D_MODEL), y_s.reshape(bs, ts, D_MODEL),
            jnp.stack(ret_p), jnp.stack(conv_p), ret_s_all, conv_s_all)
```
